```python
import jax, jax.numpy as jnp
from jax import lax
import numpy as np

D_MODEL = 1024
BATCH = 16
SEQ = 2048
DEPTH = 1
DEC_BATCH = 128
DEC_SEQ = 1
PAST_LEN = 16384
PAGE_SIZE = 128

HEAD_DIM = 64
N_Q_HEADS = 8
N_KV_HEADS = 2
Q_PER_KV = N_Q_HEADS // N_KV_HEADS
ATTN_WIDTH = N_Q_HEADS * HEAD_DIM
KV_WIDTH = N_KV_HEADS * HEAD_DIM
WINDOW = 128
ROPE_THETA = 10000.0
GMLP_WIDTH = D_MODEL - ATTN_WIDTH
GMLP_GROUPS = 8
GMLP_GROUP_DIM = GMLP_WIDTH // GMLP_GROUPS
CHUNK = 128
MIX_WIDTH = ATTN_WIDTH + GMLP_WIDTH
Q_END = ATTN_WIDTH
K_END = Q_END + KV_WIDTH
V_END = K_END + KV_WIDTH
U_END = V_END + GMLP_WIDTH
IN_PROJ_WIDTH = U_END + GMLP_WIDTH
N_GROUPS = 4
EXPERTS_PER_GROUP = 8
N_EXPERTS = N_GROUPS * EXPERTS_PER_GROUP
TOP_K = 2
D_EXPERT = 256
EPS = 1e-6
NEG_INF = -1e30

kernel_name = "hymba_swa_sink_gmlp_hier_moe_step"


def rmsnorm(x, g):
    xf = x.astype(jnp.float32)
    y = xf * lax.rsqrt(jnp.mean(xf * xf, axis=-1, keepdims=True) + EPS)
    return (y * g.astype(jnp.float32)).astype(x.dtype)


def layernorm(x, g, b):
    xf = x.astype(jnp.float32)
    mu = jnp.mean(xf, axis=-1, keepdims=True)
    xc = xf - mu
    y = xc * lax.rsqrt(jnp.mean(xc * xc, axis=-1, keepdims=True) + EPS)
    return (y * g.astype(jnp.float32) + b.astype(jnp.float32)).astype(x.dtype)


def rope(x, pos):
    half = HEAD_DIM // 2
    inv_freq = ROPE_THETA ** (-jnp.arange(half, dtype=jnp.float32) * 2.0 / HEAD_DIM)
    ang = pos.astype(jnp.float32)[:, None] * inv_freq[None, :]
    cos = jnp.cos(ang)[:, None, :]
    sin = jnp.sin(ang)[:, None, :]
    xf = x.astype(jnp.float32)
    x1, x2 = xf[..., :half], xf[..., half:]
    return jnp.concatenate([x1 * cos - x2 * sin, x2 * cos + x1 * sin], axis=-1).astype(x.dtype)


def sink_softmax(scores, mask, sinks_kg):
    s = jnp.where(mask, scores, NEG_INF)
    sink = jnp.broadcast_to(sinks_kg.astype(jnp.float32)[:, :, None, None], s.shape[:-1] + (1,))
    p = jax.nn.softmax(jnp.concatenate([s, sink], axis=-1), axis=-1)
    return p[..., :-1]


def mix_inputs(x, norm_g, w_in):
    b, s = x.shape[:2]
    z = rmsnorm(x, norm_g) @ w_in
    q = z[..., :Q_END].reshape(b, s, N_Q_HEADS, HEAD_DIM)
    k = z[..., Q_END:K_END].reshape(b, s, N_KV_HEADS, HEAD_DIM)
    v = z[..., K_END:V_END].reshape(b, s, N_KV_HEADS, HEAD_DIM)
    u = jax.nn.gelu(z[..., V_END:U_END], approximate=False)
    vg = jax.nn.gelu(z[..., U_END:], approximate=False)
    return q, k, v, u, vg


def swa_prompt(q, k, v, sinks):
    b, s = q.shape[:2]
    nb = s // WINDOW
    qb = q.reshape(b, nb, WINDOW, N_KV_HEADS, Q_PER_KV, HEAD_DIM)
    kb = k.reshape(b, nb, WINDOW, N_KV_HEADS, HEAD_DIM)
    vb = v.reshape(b, nb, WINDOW, N_KV_HEADS, HEAD_DIM)
    pad = ((0, 0), (1, 0), (0, 0), (0, 0), (0, 0))
    kk = jnp.concatenate([jnp.pad(kb, pad)[:, :-1], kb], axis=2)
    vv = jnp.concatenate([jnp.pad(vb, pad)[:, :-1], vb], axis=2)
    scores = jnp.einsum('bnqkgd,bnskd->bnkgqs', qb, kk,
                        preferred_element_type=jnp.float32) * (HEAD_DIM ** -0.5)
    blk = jnp.arange(nb, dtype=jnp.int32)[:, None, None] * WINDOW
    qpos = blk + jnp.arange(WINDOW, dtype=jnp.int32)[None, :, None]
    kpos = blk - WINDOW + jnp.arange(2 * WINDOW, dtype=jnp.int32)[None, None, :]
    diff = qpos - kpos
    mask = (diff >= 0) & (diff <= WINDOW) & (kpos >= 0)
    p = sink_softmax(scores, mask[None, :, None, None], sinks.reshape(N_KV_HEADS, Q_PER_KV))
    o = jnp.einsum('bnkgqs,bnskd->bnqkgd', p.astype(vv.dtype), vv)
    return o.reshape(b, s, ATTN_WIDTH)


def swa_sample(q, k_new, v_new, k_cache, v_cache, sinks):
    db, s = q.shape[:2]
    kk = jnp.concatenate([k_cache, k_new], axis=1)
    vv = jnp.concatenate([v_cache, v_new], axis=1)
    qg = q.reshape(db, s, N_KV_HEADS, Q_PER_KV, HEAD_DIM)
    scores = jnp.einsum('bqkgd,bskd->bkgqs', qg, kk,
                        preferred_element_type=jnp.float32) * (HEAD_DIM ** -0.5)
    qpos = WINDOW + jnp.arange(s, dtype=jnp.int32)[:, None]
    kpos = jnp.arange(kk.shape[1], dtype=jnp.int32)[None, :]
    diff = qpos - kpos
    mask = (diff >= 0) & (diff <= WINDOW)
    p = sink_softmax(scores, mask[None, None, None], sinks.reshape(N_KV_HEADS, Q_PER_KV))
    o = jnp.einsum('bkgqs,bskd->bqkgd', p.astype(vv.dtype), vv)
    return o.reshape(db, s, ATTN_WIDTH)


def causal_spatial(w_s):
    tril = jnp.tril(jnp.ones((CHUNK, CHUNK), dtype=bool))
    return jnp.where(tril[None], w_s, jnp.zeros_like(w_s))


def gmlp_prompt(u, vn, w_s, b_s):
    b, s = u.shape[:2]
    vc = vn.reshape(b, s // CHUNK, CHUNK, GMLP_GROUPS, GMLP_GROUP_DIM)
    sp = jnp.einsum('gts,bnsgd->bntgd', causal_spatial(w_s), vc) + b_s.T[None, None, :, :, None]
    return u * sp.reshape(b, s, GMLP_WIDTH)


def gmlp_sample(u, vn, w_s, b_s):
    db, s = u.shape[:2]
    vc = vn.reshape(db, s, GMLP_GROUPS, GMLP_GROUP_DIM)
    wm = causal_spatial(w_s)[:, :s, :s]
    sp = jnp.einsum('gts,bsgd->btgd', wm, vc) + b_s[:, :s].T[None, :, :, None]
    return u * sp.reshape(db, s, GMLP_WIDTH)


def hier_moe(h, w_rg, b_rg, w_re, b_re, w_gate, w_up, w_down):
    shape = h.shape
    t = h.reshape(-1, D_MODEL)
    n = t.shape[0]
    g_logits = jnp.einsum('td,dg->tg', t, w_rg, preferred_element_type=jnp.float32) + b_rg.astype(jnp.float32)
    g_prob = jax.nn.softmax(g_logits, axis=-1)
    g_idx = jnp.argmax(g_logits, axis=-1)
    g_w = jnp.take_along_axis(g_prob, g_idx[:, None], axis=-1)
    e_logits = jnp.einsum('td,dge->tge', t, w_re, preferred_element_type=jnp.float32) + b_re.astype(jnp.float32)
    e_sel = jnp.take_along_axis(e_logits, g_idx[:, None, None], axis=1)[:, 0]
    e_prob = jax.nn.softmax(e_sel, axis=-1)
    top_w, top_i = lax.top_k(e_prob, TOP_K)
    top_w = top_w / jnp.sum(top_w, axis=-1, keepdims=True)
    within = jnp.sum(jax.nn.one_hot(top_i, EXPERTS_PER_GROUP, dtype=jnp.float32) * top_w[..., None], axis=1)
    combine = (jax.nn.one_hot(g_idx, N_GROUPS, dtype=jnp.float32)[:, :, None]
               * within[:, None, :] * g_w[:, :, None]).reshape(n, N_EXPERTS)
    y = jnp.zeros((n, D_MODEL), jnp.float32)
    for e in range(N_EXPERTS):
        hid = jax.nn.silu(t @ w_gate[e]) * (t @ w_up[e])
        y = y + combine[:, e:e + 1] * (hid @ w_down[e]).astype(jnp.float32)
    return y.astype(h.dtype).reshape(shape)


def setup_inputs(seed: int = 0) -> dict:
    key = jax.random.key(seed)
    ks = jax.random.split(key, 24)
    f32 = jnp.float32
    nrm = lambda k, shp, sc: jax.random.normal(k, shp, f32) * sc
    return {
        "x_prompt": nrm(ks[0], (BATCH, SEQ, D_MODEL), 1.0),
        "x_sample": nrm(ks[1], (DEC_BATCH, DEC_SEQ, D_MODEL), 1.0),
        "cache_swa_k": nrm(ks[2], (DEPTH, DEC_BATCH, WINDOW, N_KV_HEADS, HEAD_DIM), 1.0),
        "cache_swa_v": nrm(ks[3], (DEPTH, DEC_BATCH, WINDOW, N_KV_HEADS, HEAD_DIM), 1.0),
        "norm_mix_g": 1.0 + nrm(ks[4], (DEPTH, D_MODEL), 0.02),
        "w_in": nrm(ks[5], (DEPTH, D_MODEL, IN_PROJ_WIDTH), D_MODEL ** -0.5),
        "attn_sinks": nrm(ks[6], (DEPTH, N_Q_HEADS), 0.5),
        "gmlp_ln_g": 1.0 + nrm(ks[7], (DEPTH, GMLP_WIDTH), 0.02),
        "gmlp_ln_b": nrm(ks[8], (DEPTH, GMLP_WIDTH), 0.02),
        "gmlp_w_s": nrm(ks[9], (DEPTH, GMLP_GROUPS, CHUNK, CHUNK), CHUNK ** -0.5),
        "gmlp_b_s": 1.0 + nrm(ks[10], (DEPTH, GMLP_GROUPS, CHUNK), 0.1),
        "w_out": nrm(ks[11], (DEPTH, MIX_WIDTH, D_MODEL), MIX_WIDTH ** -0.5),
        "norm_ffn_g": 1.0 + nrm(ks[12], (DEPTH, D_MODEL), 0.02),
        "router_group_w": nrm(ks[13], (DEPTH, D_MODEL, N_GROUPS), D_MODEL ** -0.5),
        "router_group_b": nrm(ks[14], (DEPTH, N_GROUPS), 0.01),
        "router_expert_w": nrm(ks[15], (DEPTH, D_MODEL, N_GROUPS, EXPERTS_PER_GROUP), D_MODEL ** -0.5),
        "router_expert_b": nrm(ks[16], (DEPTH, N_GROUPS, EXPERTS_PER_GROUP), 0.01),
        "expert_w_gate": nrm(ks[17], (DEPTH, N_EXPERTS, D_MODEL, D_EXPERT), D_MODEL ** -0.5),
        "expert_w_up": nrm(ks[18], (DEPTH, N_EXPERTS, D_MODEL, D_EXPERT), D_MODEL ** -0.5),
        "expert_w_down": nrm(ks[19], (DEPTH, N_EXPERTS, D_EXPERT, D_MODEL), D_EXPERT ** -0.5),
        "final_norm_g": 1.0 + nrm(ks[20], (D_MODEL,), 0.02),
    }


def reference(x_prompt, x_sample, cache_swa_k, cache_swa_v, norm_mix_g, w_in, attn_sinks,
              gmlp_ln_g, gmlp_ln_b, gmlp_w_s, gmlp_b_s, w_out, norm_ffn_g,
              router_group_w, router_group_b, router_expert_w, router_expert_b,
              expert_w_gate, expert_w_up, expert_w_down, final_norm_g):
    xp, xs = x_prompt, x_sample
    s_p = xp.shape[1]
    s_s = xs.shape[1]
    pos_p = jnp.arange(s_p, dtype=jnp.int32)
    pos_s = PAST_LEN + jnp.arange(s_s, dtype=jnp.int32)
    kp_l, vp_l, ks_l, vs_l, gs_l = [], [], [], [], []
    for l in range(DEPTH):
        moe_args = (router_group_w[l], router_group_b[l], router_expert_w[l], router_expert_b[l],
                    expert_w_gate[l], expert_w_up[l], expert_w_down[l])
        q, k, v, u, vg = mix_inputs(xp, norm_mix_g[l], w_in[l])
        q = rope(q, pos_p)
        k = rope(k, pos_p)
        att = swa_prompt(q, k, v, attn_sinks[l])
        vn = layernorm(vg, gmlp_ln_g[l], gmlp_ln_b[l])
        gm = gmlp_prompt(u, vn, gmlp_w_s[l], gmlp_b_s[l])
        xp = xp + jnp.concatenate([att, gm], axis=-1) @ w_out[l]
        xp = xp + hier_moe(rmsnorm(xp, norm_ffn_g[l]), *moe_args)
        kp_l.append(k[:, s_p - WINDOW:])
        vp_l.append(v[:, s_p - WINDOW:])
        q, k, v, u, vg = mix_inputs(xs, norm_mix_g[l], w_in[l])
        q = rope(q, pos_s)
        k = rope(k, pos_s)
        att = swa_sample(q, k, v, cache_swa_k[l], cache_swa_v[l], attn_sinks[l])
        vn = layernorm(vg, gmlp_ln_g[l], gmlp_ln_b[l])
        gm = gmlp_sample(u, vn, gmlp_w_s[l], gmlp_b_s[l])
        xs = xs + jnp.concatenate([att, gm], axis=-1) @ w_out[l]
        xs = xs + hier_moe(rmsnorm(xs, norm_ffn_g[l]), *moe_args)
        ks_l.append(k)
        vs_l.append(v)
        gs_l.append(vn)
    y_prompt = rmsnorm(xp, final_norm_g)
    y_sample = rmsnorm(xs, final_norm_g)
    new_k_prompt = jnp.stack(kp_l, axis=0)
    new_v_prompt = jnp.stack(vp_l, axis=0)
    new_k_sample = jnp.stack(ks_l, axis=0)
    new_v_sample = jnp.stack(vs_l, axis=0)
    new_gmlp_v_sample = jnp.stack(gs_l, axis=0)
    return (y_prompt, y_sample, new_k_prompt, new_v_prompt, new_k_sample, new_v_sample, new_gmlp_v_sample)
```

```python
import jax
import jax.numpy as jnp
import numpy as np
from jax import lax
from jax.experimental import pallas as pl
from jax.experimental.pallas import tpu as pltpu

F32 = jnp.float32
BF16 = jnp.bfloat16

D_MODEL = 1024
HEAD_DIM = 64
HALF = HEAD_DIM // 2
N_Q_HEADS = 8
N_KV_HEADS = 2
ATTN_WIDTH = N_Q_HEADS * HEAD_DIM
KV_WIDTH = N_KV_HEADS * HEAD_DIM
WINDOW = 128
ROPE_THETA = 10000.0
GMLP_WIDTH = D_MODEL - ATTN_WIDTH
GMLP_GROUPS = 8
CHUNK = 128
N_GROUPS = 4
EXPERTS_PER_GROUP = 8
N_EXPERTS = N_GROUPS * EXPERTS_PER_GROUP
D_EXPERT = 256
EPS = 1e-6
NEG_INF = -1e30
PAST_LEN = 16384

LANES = 128
V7X_VMEM_BYTES = 64 * 1024 * 1024

C_Q = 0
C_QR = C_Q + ATTN_WIDTH
C_K = C_QR + ATTN_WIDTH
C_KR = C_K + KV_WIDTH
C_V = C_KR + KV_WIDTH
C_U = C_V + KV_WIDTH
C_VG = C_U + GMLP_WIDTH
IN_EXT = C_VG + GMLP_WIDTH

ROUTER_ROWS = 48
TOK_TILE = 512
BLOCKS_PER_TILE = TOK_TILE // WINDOW
MOE_TILE = 1024


def _dot(a, b):
    return jnp.dot(a, b, preferred_element_type=F32)


def _dot_nt(a, b):
    return lax.dot_general(a, b, (((1,), (1,)), ((), ())), preferred_element_type=F32)


def _gelu(x):
    return 0.5 * x * (1.0 + lax.erf(x * np.float32(np.sqrt(0.5))))


def _rmsnorm(x, g):
    return x * lax.rsqrt(jnp.mean(x * x, axis=-1, keepdims=True) + EPS) * g


def _layernorm(x, g, b):
    mu = jnp.mean(x, axis=-1, keepdims=True)
    xc = x - mu
    return xc * lax.rsqrt(jnp.mean(xc * xc, axis=-1, keepdims=True) + EPS) * g + b


def _first_argmax_rows(x, row_iota, n_rows):
    m = jnp.max(x, axis=0, keepdims=True)
    idx = jnp.min(jnp.where(x == m, row_iota, n_rows), axis=0, keepdims=True)
    return m, idx


def _route(h2b, wrt_ref, brt_ref):
    n_tok = h2b.shape[0]
    lt = _dot_nt(wrt_ref[...], h2b) + brt_ref[...]
    row8 = lax.broadcasted_iota(jnp.int32, (8, n_tok), 0).astype(F32)
    glog = lt[0:8]
    gmax, gidx = _first_argmax_rows(glog, row8, 8)
    g_w = 1.0 / jnp.sum(jnp.exp(glog - gmax), axis=0, keepdims=True)
    esel = lt[8:16]
    for g in range(1, N_GROUPS):
        esel = jnp.where(gidx == g, lt[8 + 8 * g:16 + 8 * g], esel)
    m1, i1 = _first_argmax_rows(esel, row8, 8)
    esel2 = jnp.where(row8 == i1, -jnp.inf, esel)
    m2, i2 = _first_argmax_rows(esel2, row8, 8)
    r = jnp.exp(m2 - m1)
    w1 = 1.0 / (1.0 + r)
    w2 = r / (1.0 + r)
    e1 = gidx * EXPERTS_PER_GROUP + i1
    e2 = gidx * EXPERTS_PER_GROUP + i2
    slab = jnp.where(row8 == 0, e1, 0.0)
    for r_idx, val in ((1, e2), (2, w1 * g_w), (3, w2 * g_w)):
        slab = jnp.where(row8 == r_idx, val, slab)
    return slab


def _softmax_with_sink(s_prev, s_cur, sink):
    m = jnp.maximum(jnp.max(s_prev, axis=-1, keepdims=True), jnp.max(s_cur, axis=-1, keepdims=True))
    m = jnp.maximum(m, sink)
    e_prev = jnp.exp(s_prev - m)
    e_cur = jnp.exp(s_cur - m)
    den = jnp.sum(e_prev, axis=-1, keepdims=True) + jnp.sum(e_cur, axis=-1, keepdims=True) + jnp.exp(sink - m)
    inv = 1.0 / den
    return e_prev * inv, e_cur * inv


_NAT_HEADS = (0, 2, 5, 7)
_SWP_HEADS = (1, 3, 4, 6)


def _prompt_mixer_kernel(sinks_ref, x_ref, g1_ref, win_ref, cos_ref, sin_ref, lng_ref, lnb_ref, ws_ref, bsf_ref,
                         wout_ref, g2_ref, wrt_ref, brt_ref,
                         xmid_ref, h2_ref, meta_ref, kout_ref, vout_ref,
                         z_ref, mix_ref, kp_n, kp_s, vp_n, vp_s):
    t = pl.program_id(1)

    @pl.when(t == 0)
    def _():
        for ref in (kp_n, kp_s, vp_n, vp_s):
            ref[...] = jnp.zeros_like(ref)

    x = x_ref[...]
    h = _rmsnorm(x, g1_ref[...]).astype(BF16)
    z_ref[...] = _dot(h, win_ref[...])

    cos = cos_ref[...]
    sin = sin_ref[...]
    lane = lax.broadcasted_iota(jnp.int32, (WINDOW, LANES), 1)
    lo = lane < HEAD_DIM
    row = lax.broadcasted_iota(jnp.int32, (WINDOW, WINDOW), 0)
    col = lax.broadcasted_iota(jnp.int32, (WINDOW, WINDOW), 1)
    mask_cur = col <= row
    mask_prev_band = col >= row
    mask_prev_first = jnp.logical_and(mask_prev_band, (jnp.zeros_like(row) + t) > 0)

    cq = cos * np.float32(HEAD_DIM ** -0.5)
    sq = sin * np.float32(HEAD_DIM ** -0.5)
    kf = z_ref[:, C_K:C_K + KV_WIDTH] * cos + z_ref[:, C_KR:C_KR + KV_WIDTH] * sin
    vf = z_ref[:, C_V:C_V + KV_WIDTH]
    k_n = kf.astype(BF16)
    k_s = pltpu.roll(kf, HEAD_DIM, 1).astype(BF16)
    v_n = vf.astype(BF16)
    v_s = pltpu.roll(vf, HEAD_DIM, 1).astype(BF16)

    @pl.when(t == pl.num_programs(1) - 1)
    def _():
        kout_ref[...] = kf[TOK_TILE - WINDOW:]
        vout_ref[...] = vf[TOK_TILE - WINDOW:]

    u = _gelu(z_ref[:, C_U:C_U + GMLP_WIDTH])
    vn = _layernorm(_gelu(z_ref[:, C_VG:C_VG + GMLP_WIDTH]), lng_ref[...], lnb_ref[...]).astype(BF16)
    wcat = []
    for m in range(GMLP_GROUPS // 2):
        w0 = jnp.where(mask_cur, ws_ref[2 * m], 0.0).astype(BF16)
        w1 = jnp.where(mask_cur, ws_ref[2 * m + 1], 0.0).astype(BF16)
        wcat.append(jnp.concatenate([w0, w1], axis=1))
    bsf = bsf_ref[...]

    for j in range(BLOCKS_PER_TILE):
        r0 = j * WINDOW
        rows = slice(r0, r0 + WINDOW)
        q_cols = []
        for m in range(N_Q_HEADS // 2):
            q_cols.append(z_ref[rows, C_Q + m * LANES:C_Q + (m + 1) * LANES] * cq[rows]
                          + z_ref[rows, C_QR + m * LANES:C_QR + (m + 1) * LANES] * sq[rows])
        q_nat = jnp.concatenate(
            [jnp.where(lo if (hd % 2 == 0) else ~lo, q_cols[hd // 2], 0.0) for hd in _NAT_HEADS], axis=0).astype(BF16)
        q_swp = jnp.concatenate(
            [jnp.where(lo if (hd % 2 == 0) else ~lo, q_cols[hd // 2], 0.0) for hd in _SWP_HEADS], axis=0).astype(BF16)
        if j == 0:
            kpn, kps, vpn, vps = kp_n[...], kp_s[...], vp_n[...], vp_s[...]
            m_prev = mask_prev_first
        else:
            prev = slice(r0 - WINDOW, r0)
            kpn, kps, vpn, vps = k_n[prev], k_s[prev], v_n[prev], v_s[prev]
            m_prev = mask_prev_band
        outs = []
        for q_stack, heads, kp, kc, vp, vc in ((q_nat, _NAT_HEADS, kpn, k_n[rows], vpn, v_n[rows]),
                                               (q_swp, _SWP_HEADS, kps, k_s[rows], vps, v_s[rows])):
            s_prev = _dot_nt(q_stack, kp)
            s_cur = _dot_nt(q_stack, kc)
            p_prev, p_cur = [], []
            for i, hd in enumerate(heads):
                pr = slice(i * WINDOW, (i + 1) * WINDOW)
                pp, pc = _softmax_with_sink(jnp.where(m_prev, s_prev[pr], NEG_INF),
                                            jnp.where(mask_cur, s_cur[pr], NEG_INF), sinks_ref[hd])
                p_prev.append(pp.astype(BF16))
                p_cur.append(pc.astype(BF16))
            outs.append(_dot(jnp.concatenate(p_prev, axis=0), vp) + _dot(jnp.concatenate(p_cur, axis=0), vc))
        o_nat, o_swp = outs
        for m in range(N_Q_HEADS // 2):
            pr = slice(m * WINDOW, (m + 1) * WINDOW)
            even_nat = (2 * m) in _NAT_HEADS
            att = jnp.where(lo, o_nat[pr], o_swp[pr]) if even_nat else jnp.where(lo, o_swp[pr], o_nat[pr])
            mix_ref[rows, m * LANES:(m + 1) * LANES] = att.astype(BF16)
        for m in range(GMLP_GROUPS // 2):
            cs = slice(m * LANES, (m + 1) * LANES)
            vcol = vn[rows, cs]
            rhs = jnp.concatenate([jnp.where(lo, vcol, jnp.zeros_like(vcol)),
                                   jnp.where(lo, jnp.zeros_like(vcol), vcol)], axis=0)
            sp = _dot(wcat[m], rhs) + bsf[:, cs]
            mix_ref[rows, ATTN_WIDTH + m * LANES:ATTN_WIDTH + (m + 1) * LANES] = (u[rows, cs] * sp).astype(BF16)

    last = slice(TOK_TILE - WINDOW, TOK_TILE)
    kp_n[...] = k_n[last]
    kp_s[...] = k_s[last]
    vp_n[...] = v_n[last]
    vp_s[...] = v_s[last]

    xmid = x + _dot(mix_ref[...], wout_ref[...])
    xmid_ref[...] = xmid
    h2b = _rmsnorm(xmid, g2_ref[...]).astype(BF16)
    h2_ref[...] = h2b
    meta_ref[...] = _route(h2b, wrt_ref, brt_ref)


def _sample_mixer_kernel(sinks_ref, x_ref, ck_ref, cv_ref, g1_ref, win_ref, cos_ref, sin_ref, lng_ref, lnb_ref,
                         ws0_ref, bs0_ref, wout_ref, g2_ref, wrt_ref, brt_ref,
                         xmid_ref, h2_ref, meta_ref, kout_ref, vout_ref, vnout_ref,
                         mix_ref):
    n_seq = x_ref.shape[0]
    seq_chunk = 16
    x = x_ref[...]
    h = _rmsnorm(x, g1_ref[...]).astype(BF16)
    z = _dot(h, win_ref[...])
    cos = cos_ref[...]
    sin = sin_ref[...]
    kf = z[:, C_K:C_K + KV_WIDTH] * cos + z[:, C_KR:C_KR + KV_WIDTH] * sin
    vf = z[:, C_V:C_V + KV_WIDTH]
    kout_ref[...] = kf
    vout_ref[...] = vf
    scale = np.float32(HEAD_DIM ** -0.5)
    lane = lax.broadcasted_iota(jnp.int32, (n_seq, LANES), 1)
    lo = lane < HEAD_DIM
    kb = kf.astype(BF16).astype(F32)
    vb = vf.astype(BF16).astype(F32)

    q_heads = []
    for hd in range(N_Q_HEADS):
        m = hd // 2
        qc = (z[:, C_Q + m * LANES:C_Q + (m + 1) * LANES] * cos
              + z[:, C_QR + m * LANES:C_QR + (m + 1) * LANES] * sin) * scale
        keep = lo if hd % 2 == 0 else ~lo
        qm = jnp.where(keep, qc, 0.0)
        if (hd % 2) != (hd // (N_Q_HEADS // N_KV_HEADS)):
            qm = pltpu.roll(qm, HEAD_DIM, 1)
        q_heads.append(qm.astype(BF16))

    s_new = [jnp.sum(q_heads[hd].astype(F32) * kb, axis=-1, keepdims=True) for hd in range(N_Q_HEADS)]

    rr = lax.broadcasted_iota(jnp.int32, (N_Q_HEADS * seq_chunk, seq_chunk * WINDOW), 0)
    cc = lax.broadcasted_iota(jnp.int32, (N_Q_HEADS * seq_chunk, seq_chunk * WINDOW), 1)
    same_seq = (rr % seq_chunk) == (cc // WINDOW)
    kv_lo = lax.broadcasted_iota(jnp.int32, (seq_chunk, LANES), 1) < HEAD_DIM

    for c in range(n_seq // seq_chunk):
        sr = slice(c * seq_chunk, (c + 1) * seq_chunk)
        kc = ck_ref[sr].reshape(seq_chunk * WINDOW, KV_WIDTH).astype(BF16)
        vc = cv_ref[sr].reshape(seq_chunk * WINDOW, KV_WIDTH).astype(BF16)
        qs = jnp.concatenate([q_heads[hd][sr] for hd in range(N_Q_HEADS)], axis=0)
        s = jnp.where(same_seq, _dot_nt(qs, kc), NEG_INF)
        sn = jnp.concatenate([s_new[hd][sr] for hd in range(N_Q_HEADS)], axis=0)
        sink = jnp.concatenate([jnp.full((seq_chunk, 1), sinks_ref[hd], F32) for hd in range(N_Q_HEADS)], axis=0)
        m = jnp.maximum(jnp.maximum(jnp.max(s, axis=-1, keepdims=True), sn), sink)
        e = jnp.exp(s - m)
        en = jnp.exp(sn - m)
        inv = 1.0 / (jnp.sum(e, axis=-1, keepdims=True) + en + jnp.exp(sink - m))
        o = _dot((e * inv).astype(BF16), vc)
        pn = (en * inv).astype(BF16).astype(F32)
        for mcol in range(N_Q_HEADS // 2):
            halves = []
            for hd in (2 * mcol, 2 * mcol + 1):
                oh = o[hd * seq_chunk:(hd + 1) * seq_chunk] + pn[hd * seq_chunk:(hd + 1) * seq_chunk] * vb[sr]
                if (hd % 2) != (hd // (N_Q_HEADS // N_KV_HEADS)):
                    oh = pltpu.roll(oh, HEAD_DIM, 1)
                halves.append(oh)
            att = jnp.where(kv_lo, halves[0], halves[1])
            mix_ref[sr, mcol * LANES:(mcol + 1) * LANES] = att.astype(BF16)

    u = _gelu(z[:, C_U:C_U + GMLP_WIDTH])
    vn = _layernorm(_gelu(z[:, C_VG:C_VG + GMLP_WIDTH]), lng_ref[...], lnb_ref[...])
    vnout_ref[...] = vn
    sp = ws0_ref[...].astype(BF16).astype(F32) * vn.astype(BF16).astype(F32) + bs0_ref[...]
    mix_ref[:, ATTN_WIDTH:] = (u * sp).astype(BF16)

    xmid = x + _dot(mix_ref[...], wout_ref[...])
    xmid_ref[...] = xmid
    h2b = _rmsnorm(xmid, g2_ref[...]).astype(BF16)
    h2_ref[...] = h2b
    meta_ref[...] = _route(h2b, wrt_ref, brt_ref)


def _dense_moe_kernel(t_ref, xmid_ref, comb_ref, wg_ref, wu_ref, wd_ref, gf_ref, y_ref, acc_ref):
    e = pl.program_id(1)

    @pl.when(e == 0)
    def _():
        acc_ref[...] = xmid_ref[...]

    tb = t_ref[...]
    gate = _dot(tb, wg_ref[...])
    up = _dot(tb, wu_ref[...])
    hid = (gate * (1.0 / (1.0 + jnp.exp(-gate))) * up).astype(BF16)
    out = _dot(hid, wd_ref[...])
    lane = lax.broadcasted_iota(jnp.int32, comb_ref.shape, 1)
    c_e = jnp.sum(jnp.where(lane == e, comb_ref[...], 0.0), axis=-1, keepdims=True)
    acc_ref[...] += c_e * out

    @pl.when(e == pl.num_programs(1) - 1)
    def _():
        y_ref[...] = _rmsnorm(acc_ref[...], gf_ref[...])


def _vmem_limit(n_bytes):
    return int(min(n_bytes, V7X_VMEM_BYTES - 4 * 1024 * 1024))


def _rot_cols(w, n_heads):
    d = w.shape[0]
    w4 = w.reshape(d, n_heads, 2, HALF)
    return jnp.concatenate([-w4[:, :, 1], w4[:, :, 0]], axis=-1).reshape(d, n_heads * HEAD_DIM)


def _rope_tables(pos):
    inv_freq = ROPE_THETA ** (-jnp.arange(HALF, dtype=F32) * 2.0 / HEAD_DIM)
    ang = pos.astype(F32)[:, None] * inv_freq[None, :]
    return jnp.tile(jnp.cos(ang), (1, LANES // HALF)), jnp.tile(jnp.sin(ang), (1, LANES // HALF))


def _combine_matrix(meta):
    m = jnp.swapaxes(meta, 1, 2).reshape(-1, 8)
    ids = m[:, 0:2].astype(jnp.int32)
    return jnp.sum(jax.nn.one_hot(ids, N_EXPERTS, dtype=F32) * m[:, 2:4, None], axis=1)


def _dense_moe(t, xmid, comb, wg, wu, wd, gf, tile):
    n = t.shape[0]
    comb = jnp.pad(comb, ((0, 0), (0, LANES - N_EXPERTS)))
    return pl.pallas_call(
        _dense_moe_kernel,
        grid=(n // tile, N_EXPERTS),
        in_specs=[
            pl.BlockSpec((tile, D_MODEL), lambda i, e: (i, 0)),
            pl.BlockSpec((tile, D_MODEL), lambda i, e: (i, 0)),
            pl.BlockSpec((tile, LANES), lambda i, e: (i, 0)),
            pl.BlockSpec((None, D_MODEL, D_EXPERT), lambda i, e: (e, 0, 0)),
            pl.BlockSpec((None, D_MODEL, D_EXPERT), lambda i, e: (e, 0, 0)),
            pl.BlockSpec((None, D_EXPERT, D_MODEL), lambda i, e: (e, 0, 0)),
            pl.BlockSpec((1, D_MODEL), lambda i, e: (0, 0)),
        ],
        out_specs=pl.BlockSpec((tile, D_MODEL), lambda i, e: (i, 0)),
        out_shape=jax.ShapeDtypeStruct((n, D_MODEL), F32),
        scratch_shapes=[pltpu.VMEM((tile, D_MODEL), F32)],
        compiler_params=pltpu.CompilerParams(
            dimension_semantics=("arbitrary", "arbitrary"),
            vmem_limit_bytes=_vmem_limit(48 * 1024 * 1024)),
        name="dense_moe",
    )(t, xmid, comb, wg, wu, wd, gf)


def kernel(x_prompt, x_sample, cache_swa_k, cache_swa_v, norm_mix_g, w_in, attn_sinks, gmlp_ln_g, gmlp_ln_b,
           gmlp_w_s, gmlp_b_s, w_out, norm_ffn_g, router_group_w, router_group_b, router_expert_w,
           router_expert_b, expert_w_gate, expert_w_up, expert_w_down, final_norm_g):
    assert norm_mix_g.shape[0] == 1, "single-layer trunk"
    batch, seq, _ = x_prompt.shape
    dec_batch = x_sample.shape[0]
    assert x_sample.shape[1] == 1 and seq % TOK_TILE == 0

    w = w_in[0]
    wq, wk = w[:, :ATTN_WIDTH], w[:, ATTN_WIDTH:ATTN_WIDTH + KV_WIDTH]
    win_ext = jnp.concatenate(
        [wq, _rot_cols(wq, N_Q_HEADS), wk, _rot_cols(wk, N_KV_HEADS), w[:, ATTN_WIDTH + KV_WIDTH:]],
        axis=1).astype(BF16)
    wout = w_out[0].astype(BF16)
    g1 = norm_mix_g[0][None, :]
    g2 = norm_ffn_g[0][None, :]
    gf = final_norm_g[None, :]
    lng = gmlp_ln_g[0][None, :]
    lnb = gmlp_ln_b[0][None, :]
    sinks = attn_sinks[0]
    ws = gmlp_w_s[0]
    group_dim = GMLP_WIDTH // GMLP_GROUPS
    bsf = jnp.repeat(gmlp_b_s[0].T, group_dim, axis=1)
    ws0 = jnp.repeat(ws[:, 0, 0], group_dim)[None, :]
    bs0 = jnp.repeat(gmlp_b_s[0][:, 0], group_dim)[None, :]
    wrt = jnp.zeros((ROUTER_ROWS, D_MODEL), F32)
    wrt = wrt.at[0:N_GROUPS].set(router_group_w[0].T)
    wrt = wrt.at[8:8 + N_EXPERTS].set(router_expert_w[0].reshape(D_MODEL, N_EXPERTS).T).astype(BF16)
    brt = jnp.full((ROUTER_ROWS, 1), NEG_INF, F32)
    brt = brt.at[0:N_GROUPS, 0].set(router_group_b[0])
    brt = brt.at[8:8 + N_EXPERTS, 0].set(router_expert_b[0].reshape(N_EXPERTS))
    wg = expert_w_gate[0].astype(BF16)
    wu = expert_w_up[0].astype(BF16)
    wd = expert_w_down[0].astype(BF16)
    cos_p, sin_p = _rope_tables(jnp.arange(seq, dtype=jnp.int32))
    cos_s, sin_s = _rope_tables(PAST_LEN + jnp.arange(1, dtype=jnp.int32))

    full = lambda shape: pl.BlockSpec(shape, lambda *_: (0,) * len(shape))
    smem = pl.BlockSpec(memory_space=pltpu.SMEM)
    n_tiles = seq // TOK_TILE

    xmid_p, h2_p, meta_p, k_p, v_p = pl.pallas_call(
        _prompt_mixer_kernel,
        grid=(batch, n_tiles),
        in_specs=[
            smem,
            pl.BlockSpec((None, TOK_TILE, D_MODEL), lambda b, t: (b, t, 0)),
            full((1, D_MODEL)),
            full((D_MODEL, IN_EXT)),
            pl.BlockSpec((TOK_TILE, LANES), lambda b, t: (t, 0)),
            pl.BlockSpec((TOK_TILE, LANES), lambda b, t: (t, 0)),
            full((1, GMLP_WIDTH)),
            full((1, GMLP_WIDTH)),
            full((GMLP_GROUPS, CHUNK, CHUNK)),
            full((CHUNK, GMLP_WIDTH)),
            full((D_MODEL, D_MODEL)),
            full((1, D_MODEL)),
            full((ROUTER_ROWS, D_MODEL)),
            full((ROUTER_ROWS, 1)),
        ],
        out_specs=[
            pl.BlockSpec((None, TOK_TILE, D_MODEL), lambda b, t: (b, t, 0)),
            pl.BlockSpec((None, TOK_TILE, D_MODEL), lambda b, t: (b, t, 0)),
            pl.BlockSpec((None, None, 8, TOK_TILE), lambda b, t: (b, t, 0, 0)),
            pl.BlockSpec((None, WINDOW, KV_WIDTH), lambda b, t: (b, 0, 0)),
            pl.BlockSpec((None, WINDOW, KV_WIDTH), lambda b, t: (b, 0, 0)),
        ],
        out_shape=[
            jax.ShapeDtypeStruct((batch, seq, D_MODEL), F32),
            jax.ShapeDtypeStruct((batch, seq, D_MODEL), BF16),
            jax.ShapeDtypeStruct((batch, n_tiles, 8, TOK_TILE), F32),
            jax.ShapeDtypeStruct((batch, WINDOW, KV_WIDTH), F32),
            jax.ShapeDtypeStruct((batch, WINDOW, KV_WIDTH), F32),
        ],
        scratch_shapes=[
            pltpu.VMEM((TOK_TILE, IN_EXT), F32),
            pltpu.VMEM((TOK_TILE, D_MODEL), BF16),
            pltpu.VMEM((WINDOW, KV_WIDTH), BF16),
            pltpu.VMEM((WINDOW, KV_WIDTH), BF16),
            pltpu.VMEM((WINDOW, KV_WIDTH), BF16),
            pltpu.VMEM((WINDOW, KV_WIDTH), BF16),
        ],
        compiler_params=pltpu.CompilerParams(
            dimension_semantics=("arbitrary", "arbitrary"),
            vmem_limit_bytes=_vmem_limit(56 * 1024 * 1024)),
        name="prompt_mixer",
    )(sinks, x_prompt, g1, win_ext, cos_p, sin_p, lng, lnb, ws, bsf, wout, g2, wrt, brt)

    xs = x_sample.reshape(dec_batch, D_MODEL)
    ck = cache_swa_k[0].reshape(dec_batch, WINDOW, KV_WIDTH)
    cv = cache_swa_v[0].reshape(dec_batch, WINDOW, KV_WIDTH)
    xmid_s, h2_s, meta_s, k_s, v_s, vn_s = pl.pallas_call(
        _sample_mixer_kernel,
        in_specs=[smem] + [pl.BlockSpec(memory_space=pltpu.VMEM)] * 15,
        out_shape=[
            jax.ShapeDtypeStruct((dec_batch, D_MODEL), F32),
            jax.ShapeDtypeStruct((dec_batch, D_MODEL), BF16),
            jax.ShapeDtypeStruct((8, dec_batch), F32),
            jax.ShapeDtypeStruct((dec_batch, KV_WIDTH), F32),
            jax.ShapeDtypeStruct((dec_batch, KV_WIDTH), F32),
            jax.ShapeDtypeStruct((dec_batch, GMLP_WIDTH), F32),
        ],
        scratch_shapes=[pltpu.VMEM((dec_batch, D_MODEL), BF16)],
        compiler_params=pltpu.CompilerParams(vmem_limit_bytes=_vmem_limit(56 * 1024 * 1024)),
        name="sample_mixer",
    )(sinks, xs, ck, cv, g1, win_ext, cos_s, sin_s, lng, lnb, ws0, bs0, wout, g2, wrt, brt)

    comb_p = _combine_matrix(meta_p.reshape(batch * n_tiles, 8, TOK_TILE))
    comb_s = _combine_matrix(meta_s[None])
    y_p = _dense_moe(h2_p.reshape(batch * seq, D_MODEL), xmid_p.reshape(batch * seq, D_MODEL), comb_p,
                     wg, wu, wd, gf, MOE_TILE)
    y_s = _dense_moe(h2_s, xmid_s, comb_s, wg, wu, wd, gf, dec_batch)

    return (y_p.reshape(batch, seq, D_MODEL),
            y_s.reshape(dec_batch, 1, D_MODEL),
            k_p.reshape(1, batch, WINDOW, N_KV_HEADS, HEAD_DIM),
            v_p.reshape(1, batch, WINDOW, N_KV_HEADS, HEAD_DIM),
            k_s.reshape(1, dec_batch, 1, N_KV_HEADS, HEAD_DIM),
            v_s.reshape(1, dec_batch, 1, N_KV_HEADS, HEAD_DIM),
            vn_s.reshape(1, dec_batch, 1, GMLP_WIDTH))
```

```python
import jax
import jax.numpy as jnp
import numpy as np
from jax import lax
from jax.experimental import pallas as pl
from jax.experimental.pallas import tpu as pltpu

F32 = jnp.float32
BF16 = jnp.bfloat16

D_MODEL = 1024
HEAD_DIM = 64
HALF = HEAD_DIM // 2
N_Q_HEADS = 8
N_KV_HEADS = 2
ATTN_WIDTH = N_Q_HEADS * HEAD_DIM
KV_WIDTH = N_KV_HEADS * HEAD_DIM
WINDOW = 128
ROPE_THETA = 10000.0
GMLP_WIDTH = D_MODEL - ATTN_WIDTH
GMLP_GROUPS = 8
CHUNK = 128
N_GROUPS = 4
EXPERTS_PER_GROUP = 8
N_EXPERTS = N_GROUPS * EXPERTS_PER_GROUP
D_EXPERT = 256
EPS = 1e-6
NEG_INF = -1e30
PAST_LEN = 16384

LANES = 128
V7X_VMEM_BYTES = 64 * 1024 * 1024

C_Q = 0
C_QR = C_Q + ATTN_WIDTH
C_K = C_QR + ATTN_WIDTH
C_KR = C_K + KV_WIDTH
C_V = C_KR + KV_WIDTH
C_U = C_V + KV_WIDTH
C_VG = C_U + GMLP_WIDTH
IN_EXT = C_VG + GMLP_WIDTH

ROUTER_ROWS = 48
TOK_TILE = 512
BLOCKS_PER_TILE = TOK_TILE // WINDOW
ROW_GRANULE = 16
SORT_CHUNK = 512
SORT_ROWS = -(-(2 * TOK_TILE + N_EXPERTS * (ROW_GRANULE - 1)) // SORT_CHUNK) * SORT_CHUNK
GRANULES_PER_TILE = SORT_ROWS // ROW_GRANULE
FFN_ROWS = 512
FFN_SLOTS = FFN_ROWS // ROW_GRANULE
CODE_SHIFT = 7
assert GRANULES_PER_TILE <= (1 << CODE_SHIFT)


def _dot(a, b):
    return jnp.dot(a, b, preferred_element_type=F32)


def _dot_nt(a, b):
    return lax.dot_general(a, b, (((1,), (1,)), ((), ())), preferred_element_type=F32)


def _gelu(x):
    return 0.5 * x * (1.0 + lax.erf(x * np.float32(np.sqrt(0.5))))


def _rmsnorm(x, g):
    return x * lax.rsqrt(jnp.mean(x * x, axis=-1, keepdims=True) + EPS) * g


def _layernorm(x, g, b):
    mu = jnp.mean(x, axis=-1, keepdims=True)
    xc = x - mu
    return xc * lax.rsqrt(jnp.mean(xc * xc, axis=-1, keepdims=True) + EPS) * g + b


def _first_argmax_rows(x, row_iota, n_rows):
    m = jnp.max(x, axis=0, keepdims=True)
    idx = jnp.min(jnp.where(x == m, row_iota, n_rows), axis=0, keepdims=True)
    return m, idx


def _route(h2b, wrt_ref, brt_ref):
    n_tok = h2b.shape[0]
    lt = _dot_nt(wrt_ref[...], h2b) + brt_ref[...]
    row8 = lax.broadcasted_iota(jnp.int32, (8, n_tok), 0).astype(F32)
    glog = lt[0:8]
    gmax, gidx = _first_argmax_rows(glog, row8, 8)
    g_w = 1.0 / jnp.sum(jnp.exp(glog - gmax), axis=0, keepdims=True)
    esel = lt[8:16]
    for g in range(1, N_GROUPS):
        esel = jnp.where(gidx == g, lt[8 + 8 * g:16 + 8 * g], esel)
    m1, i1 = _first_argmax_rows(esel, row8, 8)
    esel2 = jnp.where(row8 == i1, -jnp.inf, esel)
    m2, i2 = _first_argmax_rows(esel2, row8, 8)
    r = jnp.exp(m2 - m1)
    w1 = 1.0 / (1.0 + r)
    w2 = r / (1.0 + r)
    e1 = gidx * EXPERTS_PER_GROUP + i1
    e2 = gidx * EXPERTS_PER_GROUP + i2
    slab = jnp.where(row8 == 0, e1, 0.0)
    for r_idx, val in ((1, e2), (2, w1 * g_w), (3, w2 * g_w)):
        slab = jnp.where(row8 == r_idx, val, slab)
    return slab


def _softmax_with_sink(s_prev, s_cur, sink):
    m = jnp.maximum(jnp.max(s_prev, axis=-1, keepdims=True), jnp.max(s_cur, axis=-1, keepdims=True))
    m = jnp.maximum(m, sink)
    e_prev = jnp.exp(s_prev - m)
    e_cur = jnp.exp(s_cur - m)
    den = jnp.sum(e_prev, axis=-1, keepdims=True) + jnp.sum(e_cur, axis=-1, keepdims=True) + jnp.exp(sink - m)
    inv = 1.0 / den
    return e_prev * inv, e_cur * inv


_NAT_HEADS = (0, 2, 5, 7)
_SWP_HEADS = (1, 3, 4, 6)


def _prompt_mixer_kernel(sinks_ref, x_ref, g1_ref, win_ref, cos_ref, sin_ref, lng_ref, lnb_ref, ws_ref, bsf_ref,
                         wout_ref, g2_ref, wrt_ref, brt_ref, upper_ref, lpad_ref,
                         xmid_ref, xl_ref, meta_ref, tab_ref, kout_ref, vout_ref,
                         z_ref, mix_ref, kp_n, kp_s, vp_n, vp_s):
    t = pl.program_id(1)

    @pl.when(t == 0)
    def _():
        for ref in (kp_n, kp_s, vp_n, vp_s):
            ref[...] = jnp.zeros_like(ref)

    x = x_ref[...]
    h = _rmsnorm(x, g1_ref[...]).astype(BF16)
    z_ref[...] = _dot(h, win_ref[...])

    cos = cos_ref[...]
    sin = sin_ref[...]
    lane = lax.broadcasted_iota(jnp.int32, (WINDOW, LANES), 1)
    lo = lane < HEAD_DIM
    row = lax.broadcasted_iota(jnp.int32, (WINDOW, WINDOW), 0)
    col = lax.broadcasted_iota(jnp.int32, (WINDOW, WINDOW), 1)
    mask_cur = col <= row
    mask_prev_band = col >= row
    mask_prev_first = jnp.logical_and(mask_prev_band, (jnp.zeros_like(row) + t) > 0)

    cq = cos * np.float32(HEAD_DIM ** -0.5)
    sq = sin * np.float32(HEAD_DIM ** -0.5)
    kf = z_ref[:, C_K:C_K + KV_WIDTH] * cos + z_ref[:, C_KR:C_KR + KV_WIDTH] * sin
    vf = z_ref[:, C_V:C_V + KV_WIDTH]
    k_n = kf.astype(BF16)
    k_s = pltpu.roll(kf, HEAD_DIM, 1).astype(BF16)
    v_n = vf.astype(BF16)
    v_s = pltpu.roll(vf, HEAD_DIM, 1).astype(BF16)

    @pl.when(t == pl.num_programs(1) - 1)
    def _():
        kout_ref[...] = kf[TOK_TILE - WINDOW:]
        vout_ref[...] = vf[TOK_TILE - WINDOW:]

    u = _gelu(z_ref[:, C_U:C_U + GMLP_WIDTH])
    vn = _layernorm(_gelu(z_ref[:, C_VG:C_VG + GMLP_WIDTH]), lng_ref[...], lnb_ref[...]).astype(BF16)
    wcat = []
    for m in range(GMLP_GROUPS // 2):
        w0 = jnp.where(mask_cur, ws_ref[2 * m], 0.0).astype(BF16)
        w1 = jnp.where(mask_cur, ws_ref[2 * m + 1], 0.0).astype(BF16)
        wcat.append(jnp.concatenate([w0, w1], axis=1))
    bsf = bsf_ref[...]

    for j in range(BLOCKS_PER_TILE):
        r0 = j * WINDOW
        rows = slice(r0, r0 + WINDOW)
        q_cols = []
        for m in range(N_Q_HEADS // 2):
            q_cols.append(z_ref[rows, C_Q + m * LANES:C_Q + (m + 1) * LANES] * cq[rows]
                          + z_ref[rows, C_QR + m * LANES:C_QR + (m + 1) * LANES] * sq[rows])
        q_nat = jnp.concatenate(
            [jnp.where(lo if (hd % 2 == 0) else ~lo, q_cols[hd // 2], 0.0) for hd in _NAT_HEADS], axis=0).astype(BF16)
        q_swp = jnp.concatenate(
            [jnp.where(lo if (hd % 2 == 0) else ~lo, q_cols[hd // 2], 0.0) for hd in _SWP_HEADS], axis=0).astype(BF16)
        if j == 0:
            kpn, kps, vpn, vps = kp_n[...], kp_s[...], vp_n[...], vp_s[...]
            m_prev = mask_prev_first
        else:
            prev = slice(r0 - WINDOW, r0)
            kpn, kps, vpn, vps = k_n[prev], k_s[prev], v_n[prev], v_s[prev]
            m_prev = mask_prev_band
        outs = []
        for q_stack, heads, kp, kc, vp, vc in ((q_nat, _NAT_HEADS, kpn, k_n[rows], vpn, v_n[rows]),
                                               (q_swp, _SWP_HEADS, kps, k_s[rows], vps, v_s[rows])):
            s_prev = _dot_nt(q_stack, kp)
            s_cur = _dot_nt(q_stack, kc)
            p_prev, p_cur = [], []
            for i, hd in enumerate(heads):
                pr = slice(i * WINDOW, (i + 1) * WINDOW)
                pp, pc = _softmax_with_sink(jnp.where(m_prev, s_prev[pr], NEG_INF),
                                            jnp.where(mask_cur, s_cur[pr], NEG_INF), sinks_ref[hd])
                p_prev.append(pp.astype(BF16))
                p_cur.append(pc.astype(BF16))
            outs.append(_dot(jnp.concatenate(p_prev, axis=0), vp) + _dot(jnp.concatenate(p_cur, axis=0), vc))
        o_nat, o_swp = outs
        for m in range(N_Q_HEADS // 2):
            pr = slice(m * WINDOW, (m + 1) * WINDOW)
            even_nat = (2 * m) in _NAT_HEADS
            att = jnp.where(lo, o_nat[pr], o_swp[pr]) if even_nat else jnp.where(lo, o_swp[pr], o_nat[pr])
            mix_ref[rows, m * LANES:(m + 1) * LANES] = att.astype(BF16)
        for m in range(GMLP_GROUPS // 2):
            cs = slice(m * LANES, (m + 1) * LANES)
            vcol = vn[rows, cs]
            rhs = jnp.concatenate([jnp.where(lo, vcol, jnp.zeros_like(vcol)),
                                   jnp.where(lo, jnp.zeros_like(vcol), vcol)], axis=0)
            sp = _dot(wcat[m], rhs) + bsf[:, cs]
            mix_ref[rows, ATTN_WIDTH + m * LANES:ATTN_WIDTH + (m + 1) * LANES] = (u[rows, cs] * sp).astype(BF16)

    last = slice(TOK_TILE - WINDOW, TOK_TILE)
    kp_n[...] = k_n[last]
    kp_s[...] = k_s[last]
    vp_n[...] = v_n[last]
    vp_s[...] = v_s[last]

    xmid = x + _dot(mix_ref[...], wout_ref[...])
    xmid_ref[...] = xmid
    h2b = _rmsnorm(xmid, g2_ref[...]).astype(BF16)
    slab = _route(h2b, wrt_ref, brt_ref)
    _local_sort(slab, h2b, upper_ref, lpad_ref, xl_ref, meta_ref, tab_ref)


def _local_sort(slab, h2b, upper_ref, lpad_ref, xl_ref, meta_ref, tab_ref):
    n_tok = h2b.shape[0]
    e1, e2 = slab[0:1], slab[1:2]
    row32 = lax.broadcasted_iota(jnp.int32, (N_EXPERTS, n_tok), 0).astype(F32)
    sel1 = row32 == e1
    sel2 = row32 == e2
    onehot = jnp.where(sel1, 1.0, jnp.where(sel2, 1.0, 0.0))
    earlier = _dot(onehot.astype(BF16), upper_ref[...])
    cnt = jnp.sum(onehot, axis=1, keepdims=True)
    pc = jnp.floor((cnt + (ROW_GRANULE - 1)) * (1.0 / ROW_GRANULE)) * ROW_GRANULE
    pc_b = jnp.broadcast_to(pc, (N_EXPERTS, LANES))
    pc_pad = jnp.concatenate([pc_b, jnp.zeros((LANES - N_EXPERTS, LANES), F32)], axis=0).astype(BF16)
    start = _dot(lpad_ref[...], pc_pad)
    base = start[:, 0:1] + earlier
    d1 = jnp.sum(jnp.where(sel1, base, 0.0), axis=0, keepdims=True)
    d2 = jnp.sum(jnp.where(sel2, base, 0.0), axis=0, keepdims=True)
    for c in range(SORT_ROWS // SORT_CHUNK):
        r_iota = (lax.broadcasted_iota(jnp.int32, (SORT_CHUNK, n_tok), 0) + c * SORT_CHUNK).astype(F32)
        perm = jnp.where(r_iota == d1, 1.0, jnp.where(r_iota == d2, 1.0, 0.0)).astype(BF16)
        xl_ref[c * SORT_CHUNK:(c + 1) * SORT_CHUNK, :] = _dot(perm, h2b).astype(BF16)
    row8 = lax.broadcasted_iota(jnp.int32, (8, n_tok), 0)
    meta_ref[...] = jnp.where(row8 == 0, d1, jnp.where(row8 == 1, d2, jnp.where(row8 >= 4, 0.0, slab)))
    lane = lax.broadcasted_iota(jnp.int32, (N_EXPERTS, LANES), 1)
    tab_ref[...] = jnp.where(lane == 0, pc_b, jnp.where(lane == 1, start, 0.0))


def _sample_mixer_kernel(sinks_ref, x_ref, ck_ref, cv_ref, g1_ref, win_ref, cos_ref, sin_ref, lng_ref, lnb_ref,
                         ws0_ref, bs0_ref, wout_ref, g2_ref, wrt_ref, brt_ref,
                         xmid_ref, h2_ref, meta_ref, kout_ref, vout_ref, vnout_ref,
                         mix_ref):
    n_seq = x_ref.shape[0]
    seq_chunk = 16
    x = x_ref[...]
    h = _rmsnorm(x, g1_ref[...]).astype(BF16)
    z = _dot(h, win_ref[...])
    cos = cos_ref[...]
    sin = sin_ref[...]
    kf = z[:, C_K:C_K + KV_WIDTH] * cos + z[:, C_KR:C_KR + KV_WIDTH] * sin
    vf = z[:, C_V:C_V + KV_WIDTH]
    kout_ref[...] = kf
    vout_ref[...] = vf
    scale = np.float32(HEAD_DIM ** -0.5)
    lane = lax.broadcasted_iota(jnp.int32, (n_seq, LANES), 1)
    lo = lane < HEAD_DIM
    kb = kf.astype(BF16).astype(F32)
    vb = vf.astype(BF16).astype(F32)

    q_heads = []
    for hd in range(N_Q_HEADS):
        m = hd // 2
        qc = (z[:, C_Q + m * LANES:C_Q + (m + 1) * LANES] * cos
              + z[:, C_QR + m * LANES:C_QR + (m + 1) * LANES] * sin) * scale
        keep = lo if hd % 2 == 0 else ~lo
        qm = jnp.where(keep, qc, 0.0)
        if (hd % 2) != (hd // (N_Q_HEADS // N_KV_HEADS)):
            qm = pltpu.roll(qm, HEAD_DIM, 1)
        q_heads.append(qm.astype(BF16))

    s_new = [jnp.sum(q_heads[hd].astype(F32) * kb, axis=-1, keepdims=True) for hd in range(N_Q_HEADS)]

    rr = lax.broadcasted_iota(jnp.int32, (N_Q_HEADS * seq_chunk, seq_chunk * WINDOW), 0)
    cc = lax.broadcasted_iota(jnp.int32, (N_Q_HEADS * seq_chunk, seq_chunk * WINDOW), 1)
    same_seq = (rr % seq_chunk) == (cc // WINDOW)
    kv_lo = lax.broadcasted_iota(jnp.int32, (seq_chunk, LANES), 1) < HEAD_DIM

    for c in range(n_seq // seq_chunk):
        sr = slice(c * seq_chunk, (c + 1) * seq_chunk)
        kc = ck_ref[sr].reshape(seq_chunk * WINDOW, KV_WIDTH).astype(BF16)
        vc = cv_ref[sr].reshape(seq_chunk * WINDOW, KV_WIDTH).astype(BF16)
        qs = jnp.concatenate([q_heads[hd][sr] for hd in range(N_Q_HEADS)], axis=0)
        s = jnp.where(same_seq, _dot_nt(qs, kc), NEG_INF)
        sn = jnp.concatenate([s_new[hd][sr] for hd in range(N_Q_HEADS)], axis=0)
        sink = jnp.concatenate([jnp.full((seq_chunk, 1), sinks_ref[hd], F32) for hd in range(N_Q_HEADS)], axis=0)
        m = jnp.maximum(jnp.maximum(jnp.max(s, axis=-1, keepdims=True), sn), sink)
        e = jnp.exp(s - m)
        en = jnp.exp(sn - m)
        inv = 1.0 / (jnp.sum(e, axis=-1, keepdims=True) + en + jnp.exp(sink - m))
        o = _dot((e * inv).astype(BF16), vc)
        pn = (en * inv).astype(BF16).astype(F32)
        for mcol in range(N_Q_HEADS // 2):
            halves = []
            for hd in (2 * mcol, 2 * mcol + 1):
                oh = o[hd * seq_chunk:(hd + 1) * seq_chunk] + pn[hd * seq_chunk:(hd + 1) * seq_chunk] * vb[sr]
                if (hd % 2) != (hd // (N_Q_HEADS // N_KV_HEADS)):
                    oh = pltpu.roll(oh, HEAD_DIM, 1)
                halves.append(oh)
            att = jnp.where(kv_lo, halves[0], halves[1])
            mix_ref[sr, mcol * LANES:(mcol + 1) * LANES] = att.astype(BF16)

    u = _gelu(z[:, C_U:C_U + GMLP_WIDTH])
    vn = _layernorm(_gelu(z[:, C_VG:C_VG + GMLP_WIDTH]), lng_ref[...], lnb_ref[...])
    vnout_ref[...] = vn
    sp = ws0_ref[...].astype(BF16).astype(F32) * vn.astype(BF16).astype(F32) + bs0_ref[...]
    mix_ref[:, ATTN_WIDTH:] = (u * sp).astype(BF16)

    xmid = x + _dot(mix_ref[...], wout_ref[...])
    xmid_ref[...] = xmid
    h2b = _rmsnorm(xmid, g2_ref[...]).astype(BF16)
    h2_ref[...] = h2b
    meta_ref[...] = _route(h2b, wrt_ref, brt_ref)


def _dense_moe_kernel(t_ref, xmid_ref, comb_ref, wg_ref, wu_ref, wd_ref, gf_ref, y_ref, acc_ref):
    e = pl.program_id(1)

    @pl.when(e == 0)
    def _():
        acc_ref[...] = xmid_ref[...]

    tb = t_ref[...]
    gate = _dot(tb, wg_ref[...])
    up = _dot(tb, wu_ref[...])
    hid = (gate * (1.0 / (1.0 + jnp.exp(-gate))) * up).astype(BF16)
    out = _dot(hid, wd_ref[...])
    lane = lax.broadcasted_iota(jnp.int32, comb_ref.shape, 1)
    c_e = jnp.sum(jnp.where(lane == e, comb_ref[...], 0.0), axis=-1, keepdims=True)
    acc_ref[...] += c_e * out

    @pl.when(e == pl.num_programs(1) - 1)
    def _():
        y_ref[...] = _rmsnorm(acc_ref[...], gf_ref[...])


def _grouped_ffn_kernel(texp_ref, code_ref, ntiles_ref, xl_hbm, wg_ref, wu_ref, wd_ref, yl_hbm,
                        xbuf, ybuf, zbuf, gsem, ssem, tsem):
    del texp_ref
    j = pl.program_id(0)
    n_tiles = ntiles_ref[0]
    spare_tile = yl_hbm.shape[0] - 1

    def slot_addr(tile, s, pad_tile, pad_row):
        code = code_ref[tile * FFN_SLOTS + s]
        valid = code >= 0
        tok = jnp.where(valid, code >> CODE_SHIFT, pad_tile)
        row = jnp.where(valid, (code & ((1 << CODE_SHIFT) - 1)) * ROW_GRANULE, pad_row)
        return tok, pl.multiple_of(row, ROW_GRANULE)

    def gather_copy(tile, s, b):
        tok, row = slot_addr(tile, s, 0, s * ROW_GRANULE)
        return pltpu.make_async_copy(xl_hbm.at[tok, pl.ds(row, ROW_GRANULE)],
                                     xbuf.at[b, pl.ds(s * ROW_GRANULE, ROW_GRANULE)], gsem.at[b])

    def scatter_copy(tile, s, b):
        tok, row = slot_addr(tile, s, spare_tile, (b * FFN_SLOTS + s) * ROW_GRANULE)
        return pltpu.make_async_copy(ybuf.at[b, pl.ds(s * ROW_GRANULE, ROW_GRANULE)],
                                     yl_hbm.at[tok, pl.ds(row, ROW_GRANULE)], ssem.at[b])

    def spare_init_copy(c):
        return pltpu.make_async_copy(zbuf, yl_hbm.at[spare_tile, pl.ds(c * FFN_ROWS, FFN_ROWS)], tsem.at[0])

    @pl.when(j < n_tiles)
    def _():
        b = j % 2

        @pl.when(j == 0)
        def _():
            zbuf[...] = jnp.zeros_like(zbuf)
            for c in range(SORT_ROWS // FFN_ROWS):
                spare_init_copy(c).start()
            for s in range(FFN_SLOTS):
                gather_copy(0, s, 0).start()
            for c in range(SORT_ROWS // FFN_ROWS):
                spare_init_copy(c).wait()

        @pl.when(j + 1 < n_tiles)
        def _():
            for s in range(FFN_SLOTS):
                gather_copy(j + 1, s, 1 - b).start()

        for s in range(FFN_SLOTS):
            gather_copy(j, s, b).wait()

        @pl.when(j >= 2)
        def _():
            for s in range(FFN_SLOTS):
                scatter_copy(j - 2, s, b).wait()

        x = xbuf[b]
        gate = _dot(x, wg_ref[...])
        up = _dot(x, wu_ref[...])
        hid = (gate * (1.0 / (1.0 + jnp.exp(-gate))) * up).astype(BF16)
        ybuf[b] = _dot(hid, wd_ref[...]).astype(BF16)
        for s in range(FFN_SLOTS):
            scatter_copy(j, s, b).start()

        @pl.when(j == n_tiles - 1)
        def _():
            for s in range(FFN_SLOTS):
                scatter_copy(j, s, b).wait()

            @pl.when(j >= 1)
            def _():
                for s in range(FFN_SLOTS):
                    scatter_copy(j - 1, s, 1 - b).wait()


def _combine_kernel(xmid_ref, yl_ref, meta_ref, gf_ref, y_ref):
    n_tok = xmid_ref.shape[0]
    meta = meta_ref[...]
    meta_t = jnp.concatenate([meta, jnp.zeros((LANES - 8, n_tok), F32)], axis=0).T
    d1, d2, w1, w2 = meta_t[:, 0:1], meta_t[:, 1:2], meta_t[:, 2:3], meta_t[:, 3:4]
    acc = xmid_ref[...]
    for c in range(SORT_ROWS // SORT_CHUNK):
        r_iota = (lax.broadcasted_iota(jnp.int32, (n_tok, SORT_CHUNK), 1) + c * SORT_CHUNK).astype(F32)
        unsort = jnp.where(r_iota == d1, w1, jnp.where(r_iota == d2, w2, 0.0)).astype(BF16)
        acc = acc + _dot(unsort, yl_ref[c * SORT_CHUNK:(c + 1) * SORT_CHUNK, :])
    y_ref[...] = _rmsnorm(acc, gf_ref[...])


def _ffn_schedule(tab):
    n_tok_tiles = tab.shape[0]
    strips = (tab[:, :, 0] * (1.0 / ROW_GRANULE)).astype(jnp.int32)
    starts = (tab[:, :, 1] * (1.0 / ROW_GRANULE)).astype(jnp.int32)
    used = jnp.sum(strips, axis=1)
    cnt = jnp.concatenate([strips.T.reshape(-1), GRANULES_PER_TILE - used])
    row0 = jnp.concatenate([starts.T.reshape(-1), used])
    n_strips = cnt.shape[0]
    tok = jnp.arange(n_strips, dtype=jnp.int32) % n_tok_tiles
    strip_e = jnp.minimum(jnp.arange(n_strips, dtype=jnp.int32) // n_tok_tiles, N_EXPERTS - 1)
    n_e = jax.ops.segment_sum(cnt, strip_e, num_segments=N_EXPERTS)
    np_e = (n_e + FFN_SLOTS - 1) // FFN_SLOTS * FFN_SLOTS
    ends = jnp.cumsum(np_e)
    base_e = ends - np_e
    cs = jnp.cumsum(cnt) - cnt
    first = cs[::n_tok_tiles][:N_EXPERTS]
    s_start = base_e[strip_e] + cs - first[strip_e]
    n_steps_max = _ffn_steps_max(n_tok_tiles)
    p = jnp.arange(n_steps_max * FFN_SLOTS, dtype=jnp.int32)
    idx = jnp.searchsorted(s_start, p, side="right").astype(jnp.int32) - 1
    g = p - s_start[idx]
    code = jnp.where(g < cnt[idx], (tok[idx] << CODE_SHIFT) | (row0[idx] + g), -1).astype(jnp.int32)
    step0 = jnp.arange(n_steps_max, dtype=jnp.int32) * FFN_SLOTS
    texp = jnp.minimum(jnp.searchsorted(ends, step0, side="right"), N_EXPERTS - 1).astype(jnp.int32)
    n_steps = (ends[-1] // FFN_SLOTS).astype(jnp.int32).reshape(1)
    return texp, code, n_steps


def _ffn_steps_max(n_tok_tiles):
    return -(-(n_tok_tiles * GRANULES_PER_TILE + N_EXPERTS * (FFN_SLOTS - 1)) // FFN_SLOTS)


def _vmem_limit(n_bytes):
    return int(min(n_bytes, V7X_VMEM_BYTES - 4 * 1024 * 1024))


def _rot_cols(w, n_heads):
    d = w.shape[0]
    w4 = w.reshape(d, n_heads, 2, HALF)
    return jnp.concatenate([-w4[:, :, 1], w4[:, :, 0]], axis=-1).reshape(d, n_heads * HEAD_DIM)


def _rope_tables(pos):
    inv_freq = ROPE_THETA ** (-jnp.arange(HALF, dtype=F32) * 2.0 / HEAD_DIM)
    ang = pos.astype(F32)[:, None] * inv_freq[None, :]
    return jnp.tile(jnp.cos(ang), (1, LANES // HALF)), jnp.tile(jnp.sin(ang), (1, LANES // HALF))


def _combine_matrix(meta):
    m = jnp.swapaxes(meta, 1, 2).reshape(-1, 8)
    ids = m[:, 0:2].astype(jnp.int32)
    return jnp.sum(jax.nn.one_hot(ids, N_EXPERTS, dtype=F32) * m[:, 2:4, None], axis=1)


def _dense_moe(t, xmid, comb, wg, wu, wd, gf, tile):
    n = t.shape[0]
    comb = jnp.pad(comb, ((0, 0), (0, LANES - N_EXPERTS)))
    return pl.pallas_call(
        _dense_moe_kernel,
        grid=(n // tile, N_EXPERTS),
        in_specs=[
            pl.BlockSpec((tile, D_MODEL), lambda i, e: (i, 0)),
            pl.BlockSpec((tile, D_MODEL), lambda i, e: (i, 0)),
            pl.BlockSpec((tile, LANES), lambda i, e: (i, 0)),
            pl.BlockSpec((None, D_MODEL, D_EXPERT), lambda i, e: (e, 0, 0)),
            pl.BlockSpec((None, D_MODEL, D_EXPERT), lambda i, e: (e, 0, 0)),
            pl.BlockSpec((None, D_EXPERT, D_MODEL), lambda i, e: (e, 0, 0)),
            pl.BlockSpec((1, D_MODEL), lambda i, e: (0, 0)),
        ],
        out_specs=pl.BlockSpec((tile, D_MODEL), lambda i, e: (i, 0)),
        out_shape=jax.ShapeDtypeStruct((n, D_MODEL), F32),
        scratch_shapes=[pltpu.VMEM((tile, D_MODEL), F32)],
        compiler_params=pltpu.CompilerParams(
            dimension_semantics=("arbitrary", "arbitrary"),
            vmem_limit_bytes=_vmem_limit(48 * 1024 * 1024)),
        name="dense_moe",
    )(t, xmid, comb, wg, wu, wd, gf)


def kernel(x_prompt, x_sample, cache_swa_k, cache_swa_v, norm_mix_g, w_in, attn_sinks, gmlp_ln_g, gmlp_ln_b,
           gmlp_w_s, gmlp_b_s, w_out, norm_ffn_g, router_group_w, router_group_b, router_expert_w,
           router_expert_b, expert_w_gate, expert_w_up, expert_w_down, final_norm_g):
    assert norm_mix_g.shape[0] == 1, "single-layer trunk"
    batch, seq, _ = x_prompt.shape
    dec_batch = x_sample.shape[0]
    assert x_sample.shape[1] == 1 and seq % TOK_TILE == 0

    w = w_in[0]
    wq, wk = w[:, :ATTN_WIDTH], w[:, ATTN_WIDTH:ATTN_WIDTH + KV_WIDTH]
    win_ext = jnp.concatenate(
        [wq, _rot_cols(wq, N_Q_HEADS), wk, _rot_cols(wk, N_KV_HEADS), w[:, ATTN_WIDTH + KV_WIDTH:]],
        axis=1).astype(BF16)
    wout = w_out[0].astype(BF16)
    g1 = norm_mix_g[0][None, :]
    g2 = norm_ffn_g[0][None, :]
    gf = final_norm_g[None, :]
    lng = gmlp_ln_g[0][None, :]
    lnb = gmlp_ln_b[0][None, :]
    sinks = attn_sinks[0]
    ws = gmlp_w_s[0]
    group_dim = GMLP_WIDTH // GMLP_GROUPS
    bsf = jnp.repeat(gmlp_b_s[0].T, group_dim, axis=1)
    ws0 = jnp.repeat(ws[:, 0, 0], group_dim)[None, :]
    bs0 = jnp.repeat(gmlp_b_s[0][:, 0], group_dim)[None, :]
    wrt = jnp.zeros((ROUTER_ROWS, D_MODEL), F32)
    wrt = wrt.at[0:N_GROUPS].set(router_group_w[0].T)
    wrt = wrt.at[8:8 + N_EXPERTS].set(router_expert_w[0].reshape(D_MODEL, N_EXPERTS).T).astype(BF16)
    brt = jnp.full((ROUTER_ROWS, 1), NEG_INF, F32)
    brt = brt.at[0:N_GROUPS, 0].set(router_group_b[0])
    brt = brt.at[8:8 + N_EXPERTS, 0].set(router_expert_b[0].reshape(N_EXPERTS))
    wg = expert_w_gate[0].astype(BF16)
    wu = expert_w_up[0].astype(BF16)
    wd = expert_w_down[0].astype(BF16)
    cos_p, sin_p = _rope_tables(jnp.arange(seq, dtype=jnp.int32))
    cos_s, sin_s = _rope_tables(PAST_LEN + jnp.arange(1, dtype=jnp.int32))

    full = lambda shape: pl.BlockSpec(shape, lambda *_: (0,) * len(shape))
    smem = pl.BlockSpec(memory_space=pltpu.SMEM)
    n_tiles = seq // TOK_TILE

    upper = jnp.triu(jnp.ones((TOK_TILE, TOK_TILE), BF16), k=1)
    lpad = (jnp.arange(LANES)[None, :] < jnp.arange(N_EXPERTS)[:, None]).astype(BF16)
    n_tok_tiles = batch * n_tiles

    xmid_p, xl_p, meta_p, tab_p, k_p, v_p = pl.pallas_call(
        _prompt_mixer_kernel,
        grid=(batch, n_tiles),
        in_specs=[
            smem,
            pl.BlockSpec((None, TOK_TILE, D_MODEL), lambda b, t: (b, t, 0)),
            full((1, D_MODEL)),
            full((D_MODEL, IN_EXT)),
            pl.BlockSpec((TOK_TILE, LANES), lambda b, t: (t, 0)),
            pl.BlockSpec((TOK_TILE, LANES), lambda b, t: (t, 0)),
            full((1, GMLP_WIDTH)),
            full((1, GMLP_WIDTH)),
            full((GMLP_GROUPS, CHUNK, CHUNK)),
            full((CHUNK, GMLP_WIDTH)),
            full((D_MODEL, D_MODEL)),
            full((1, D_MODEL)),
            full((ROUTER_ROWS, D_MODEL)),
            full((ROUTER_ROWS, 1)),
            full((TOK_TILE, TOK_TILE)),
            full((N_EXPERTS, LANES)),
        ],
        out_specs=[
            pl.BlockSpec((None, TOK_TILE, D_MODEL), lambda b, t: (b, t, 0)),
            pl.BlockSpec((None, SORT_ROWS, D_MODEL), lambda b, t: (b * n_tiles + t, 0, 0)),
            pl.BlockSpec((None, 8, TOK_TILE), lambda b, t: (b * n_tiles + t, 0, 0)),
            pl.BlockSpec((None, N_EXPERTS, LANES), lambda b, t: (b * n_tiles + t, 0, 0)),
            pl.BlockSpec((None, WINDOW, KV_WIDTH), lambda b, t: (b, 0, 0)),
            pl.BlockSpec((None, WINDOW, KV_WIDTH), lambda b, t: (b, 0, 0)),
        ],
        out_shape=[
            jax.ShapeDtypeStruct((batch, seq, D_MODEL), F32),
            jax.ShapeDtypeStruct((n_tok_tiles, SORT_ROWS, D_MODEL), BF16),
            jax.ShapeDtypeStruct((n_tok_tiles, 8, TOK_TILE), F32),
            jax.ShapeDtypeStruct((n_tok_tiles, N_EXPERTS, LANES), F32),
            jax.ShapeDtypeStruct((batch, WINDOW, KV_WIDTH), F32),
            jax.ShapeDtypeStruct((batch, WINDOW, KV_WIDTH), F32),
        ],
        scratch_shapes=[
            pltpu.VMEM((TOK_TILE, IN_EXT), F32),
            pltpu.VMEM((TOK_TILE, D_MODEL), BF16),
            pltpu.VMEM((WINDOW, KV_WIDTH), BF16),
            pltpu.VMEM((WINDOW, KV_WIDTH), BF16),
            pltpu.VMEM((WINDOW, KV_WIDTH), BF16),
            pltpu.VMEM((WINDOW, KV_WIDTH), BF16),
        ],
        compiler_params=pltpu.CompilerParams(
            dimension_semantics=("arbitrary", "arbitrary"),
            vmem_limit_bytes=_vmem_limit(56 * 1024 * 1024)),
        name="prompt_mixer",
    )(sinks, x_prompt, g1, win_ext, cos_p, sin_p, lng, lnb, ws, bsf, wout, g2, wrt, brt, upper, lpad)

    xs = x_sample.reshape(dec_batch, D_MODEL)
    ck = cache_swa_k[0].reshape(dec_batch, WINDOW, KV_WIDTH)
    cv = cache_swa_v[0].reshape(dec_batch, WINDOW, KV_WIDTH)
    xmid_s, h2_s, meta_s, k_s, v_s, vn_s = pl.pallas_call(
        _sample_mixer_kernel,
        in_specs=[smem] + [pl.BlockSpec(memory_space=pltpu.VMEM)] * 15,
        out_shape=[
            jax.ShapeDtypeStruct((dec_batch, D_MODEL), F32),
            jax.ShapeDtypeStruct((dec_batch, D_MODEL), BF16),
            jax.ShapeDtypeStruct((8, dec_batch), F32),
            jax.ShapeDtypeStruct((dec_batch, KV_WIDTH), F32),
            jax.ShapeDtypeStruct((dec_batch, KV_WIDTH), F32),
            jax.ShapeDtypeStruct((dec_batch, GMLP_WIDTH), F32),
        ],
        scratch_shapes=[pltpu.VMEM((dec_batch, D_MODEL), BF16)],
        compiler_params=pltpu.CompilerParams(vmem_limit_bytes=_vmem_limit(56 * 1024 * 1024)),
        name="sample_mixer",
    )(sinks, xs, ck, cv, g1, win_ext, cos_s, sin_s, lng, lnb, ws0, bs0, wout, g2, wrt, brt)

    texp, code, n_steps = _ffn_schedule(tab_p)
    yl_p = pl.pallas_call(
        _grouped_ffn_kernel,
        grid_spec=pltpu.PrefetchScalarGridSpec(
            num_scalar_prefetch=3,
            grid=(_ffn_steps_max(n_tok_tiles),),
            in_specs=[
                pl.BlockSpec(memory_space=pl.ANY),
                pl.BlockSpec((None, D_MODEL, D_EXPERT), lambda j, te, cd, ns: (te[j], 0, 0)),
                pl.BlockSpec((None, D_MODEL, D_EXPERT), lambda j, te, cd, ns: (te[j], 0, 0)),
                pl.BlockSpec((None, D_EXPERT, D_MODEL), lambda j, te, cd, ns: (te[j], 0, 0)),
            ],
            out_specs=pl.BlockSpec(memory_space=pl.ANY),
            scratch_shapes=[
                pltpu.VMEM((2, FFN_ROWS, D_MODEL), BF16),
                pltpu.VMEM((2, FFN_ROWS, D_MODEL), BF16),
                pltpu.VMEM((FFN_ROWS, D_MODEL), BF16),
                pltpu.SemaphoreType.DMA((2,)),
                pltpu.SemaphoreType.DMA((2,)),
                pltpu.SemaphoreType.DMA((1,)),
            ],
        ),
        out_shape=jax.ShapeDtypeStruct((n_tok_tiles + 1, SORT_ROWS, D_MODEL), BF16),
        compiler_params=pltpu.CompilerParams(
            dimension_semantics=("arbitrary",),
            vmem_limit_bytes=_vmem_limit(32 * 1024 * 1024)),
        name="grouped_ffn",
    )(texp, code, n_steps, xl_p, wg, wu, wd)

    y_p = pl.pallas_call(
        _combine_kernel,
        grid=(n_tok_tiles,),
        in_specs=[
            pl.BlockSpec((TOK_TILE, D_MODEL), lambda i: (i, 0)),
            pl.BlockSpec((None, SORT_ROWS, D_MODEL), lambda i: (i, 0, 0)),
            pl.BlockSpec((None, 8, TOK_TILE), lambda i: (i, 0, 0)),
            pl.BlockSpec((1, D_MODEL), lambda i: (0, 0)),
        ],
        out_specs=pl.BlockSpec((TOK_TILE, D_MODEL), lambda i: (i, 0)),
        out_shape=jax.ShapeDtypeStruct((batch * seq, D_MODEL), F32),
        compiler_params=pltpu.CompilerParams(
            dimension_semantics=("arbitrary",),
            vmem_limit_bytes=_vmem_limit(40 * 1024 * 1024)),
        name="moe_combine",
    )(xmid_p.reshape(batch * seq, D_MODEL), yl_p, meta_p, gf)

    comb_s = _combine_matrix(meta_s[None])
    y_s = _dense_moe(h2_s, xmid_s, comb_s, wg, wu, wd, gf, dec_batch)

    return (y_p.reshape(batch, seq, D_MODEL),
            y_s.reshape(dec_batch, 1, D_MODEL),
            k_p.reshape(1, batch, WINDOW, N_KV_HEADS, HEAD_DIM),
            v_p.reshape(1, batch, WINDOW, N_KV_HEADS, HEAD_DIM),
            k_s.reshape(1, dec_batch, 1, N_KV_HEADS, HEAD_DIM),
            v_s.reshape(1, dec_batch, 1, N_KV_HEADS, HEAD_DIM),
            vn_s.reshape(1, dec_batch, 1, GMLP_WIDTH))
```

```python
import jax
import jax.numpy as jnp
import numpy as np
from jax import lax
from jax.experimental import pallas as pl
from jax.experimental.pallas import tpu as pltpu

F32 = jnp.float32
BF16 = jnp.bfloat16

D_MODEL = 1024
HEAD_DIM = 64
HALF = HEAD_DIM // 2
N_Q_HEADS = 8
N_KV_HEADS = 2
ATTN_WIDTH = N_Q_HEADS * HEAD_DIM
KV_WIDTH = N_KV_HEADS * HEAD_DIM
WINDOW = 128
ROPE_THETA = 10000.0
GMLP_WIDTH = D_MODEL - ATTN_WIDTH
GMLP_GROUPS = 8
CHUNK = 128
N_GROUPS = 4
EXPERTS_PER_GROUP = 8
N_EXPERTS = N_GROUPS * EXPERTS_PER_GROUP
D_EXPERT = 256
EPS = 1e-6
NEG_INF = -1e30
PAST_LEN = 16384

LANES = 128
V7X_VMEM_BYTES = 64 * 1024 * 1024

C_Q = 0
C_QR = C_Q + ATTN_WIDTH
C_K = C_QR + ATTN_WIDTH
C_KR = C_K + KV_WIDTH
C_V = C_KR + KV_WIDTH
C_U = C_V + KV_WIDTH
C_VG = C_U + GMLP_WIDTH
IN_EXT = C_VG + GMLP_WIDTH

ROUTER_ROWS = 48
TOK_TILE = 512
BLOCKS_PER_TILE = TOK_TILE // WINDOW
ROW_GRANULE = 16
SORT_CHUNK = 512
SORT_ROWS = -(-(2 * TOK_TILE + N_EXPERTS * (ROW_GRANULE - 1)) // SORT_CHUNK) * SORT_CHUNK
GRANULES_PER_TILE = SORT_ROWS // ROW_GRANULE
FFN_ROWS = 512
FFN_SLOTS = FFN_ROWS // ROW_GRANULE
CODE_SHIFT = 7
assert GRANULES_PER_TILE <= (1 << CODE_SHIFT)


def _dot(a, b):
    return jnp.dot(a, b, preferred_element_type=F32)


def _dot_nt(a, b):
    return lax.dot_general(a, b, (((1,), (1,)), ((), ())), preferred_element_type=F32)


def _gelu(x):
    return 0.5 * x * (1.0 + lax.erf(x * np.float32(np.sqrt(0.5))))


def _rmsnorm(x, g):
    return x * lax.rsqrt(jnp.mean(x * x, axis=-1, keepdims=True) + EPS) * g


def _layernorm(x, g, b):
    mu = jnp.mean(x, axis=-1, keepdims=True)
    xc = x - mu
    return xc * lax.rsqrt(jnp.mean(xc * xc, axis=-1, keepdims=True) + EPS) * g + b


def _first_argmax_rows(x, row_iota, n_rows):
    m = jnp.max(x, axis=0, keepdims=True)
    idx = jnp.min(jnp.where(x == m, row_iota, n_rows), axis=0, keepdims=True)
    return m, idx


def _route(h2b, wrt_ref, brt_ref):
    n_tok = h2b.shape[0]
    lt = _dot_nt(wrt_ref[...], h2b) + brt_ref[...]
    row8 = lax.broadcasted_iota(jnp.int32, (8, n_tok), 0).astype(F32)
    glog = lt[0:8]
    gmax, gidx = _first_argmax_rows(glog, row8, 8)
    g_w = 1.0 / jnp.sum(jnp.exp(glog - gmax), axis=0, keepdims=True)
    esel = lt[8:16]
    for g in range(1, N_GROUPS):
        esel = jnp.where(gidx == g, lt[8 + 8 * g:16 + 8 * g], esel)
    m1, i1 = _first_argmax_rows(esel, row8, 8)
    esel2 = jnp.where(row8 == i1, -jnp.inf, esel)
    m2, i2 = _first_argmax_rows(esel2, row8, 8)
    r = jnp.exp(m2 - m1)
    w1 = 1.0 / (1.0 + r)
    w2 = r / (1.0 + r)
    e1 = gidx * EXPERTS_PER_GROUP + i1
    e2 = gidx * EXPERTS_PER_GROUP + i2
    slab = jnp.where(row8 == 0, e1, 0.0)
    for r_idx, val in ((1, e2), (2, w1 * g_w), (3, w2 * g_w)):
        slab = jnp.where(row8 == r_idx, val, slab)
    return slab


def _softmax_with_sink(s_prev, s_cur, sink):
    m = jnp.maximum(jnp.max(s_prev, axis=-1, keepdims=True), jnp.max(s_cur, axis=-1, keepdims=True))
    m = jnp.maximum(m, sink)
    e_prev = jnp.exp(s_prev - m)
    e_cur = jnp.exp(s_cur - m)
    den = jnp.sum(e_prev, axis=-1, keepdims=True) + jnp.sum(e_cur, axis=-1, keepdims=True) + jnp.exp(sink - m)
    inv = 1.0 / den
    return e_prev * inv, e_cur * inv


_NAT_HEADS = (0, 2, 5, 7)
_SWP_HEADS = (1, 3, 4, 6)


def _prompt_mixer_kernel(sinks_ref, x_ref, g1_ref, win_ref, cos_ref, sin_ref, lng_ref, lnb_ref, ws_ref, bsf_ref,
                         wout_ref, g2_ref, wrt_ref, brt_ref, upper_ref, lpad_ref,
                         xmid_ref, xl_ref, meta_ref, tab_ref, kout_ref, vout_ref,
                         z_ref, mix_ref, kp_n, kp_s, vp_n, vp_s):
    t = pl.program_id(1)

    @pl.when(t == 0)
    def _():
        for ref in (kp_n, kp_s, vp_n, vp_s):
            ref[...] = jnp.zeros_like(ref)

    x = x_ref[...]
    h = _rmsnorm(x, g1_ref[...]).astype(BF16)
    z_ref[...] = _dot(h, win_ref[...])

    cos = cos_ref[...]
    sin = sin_ref[...]
    lane = lax.broadcasted_iota(jnp.int32, (WINDOW, LANES), 1)
    lo = lane < HEAD_DIM
    row = lax.broadcasted_iota(jnp.int32, (WINDOW, WINDOW), 0)
    col = lax.broadcasted_iota(jnp.int32, (WINDOW, WINDOW), 1)
    mask_cur = col <= row
    mask_prev_band = col >= row
    mask_prev_first = jnp.logical_and(mask_prev_band, (jnp.zeros_like(row) + t) > 0)

    cq = cos * np.float32(HEAD_DIM ** -0.5)
    sq = sin * np.float32(HEAD_DIM ** -0.5)
    kf = z_ref[:, C_K:C_K + KV_WIDTH] * cos + z_ref[:, C_KR:C_KR + KV_WIDTH] * sin
    vf = z_ref[:, C_V:C_V + KV_WIDTH]
    k_n = kf.astype(BF16)
    k_s = pltpu.roll(kf, HEAD_DIM, 1).astype(BF16)
    v_n = vf.astype(BF16)
    v_s = pltpu.roll(vf, HEAD_DIM, 1).astype(BF16)

    @pl.when(t == pl.num_programs(1) - 1)
    def _():
        kout_ref[...] = kf[TOK_TILE - WINDOW:]
        vout_ref[...] = vf[TOK_TILE - WINDOW:]

    u = _gelu(z_ref[:, C_U:C_U + GMLP_WIDTH])
    vn = _layernorm(_gelu(z_ref[:, C_VG:C_VG + GMLP_WIDTH]), lng_ref[...], lnb_ref[...]).astype(BF16)
    wcat = []
    for m in range(GMLP_GROUPS // 2):
        w0 = jnp.where(mask_cur, ws_ref[2 * m], 0.0).astype(BF16)
        w1 = jnp.where(mask_cur, ws_ref[2 * m + 1], 0.0).astype(BF16)
        wcat.append(jnp.concatenate([w0, w1], axis=1))
    bsf = bsf_ref[...]

    for j in range(BLOCKS_PER_TILE):
        r0 = j * WINDOW
        rows = slice(r0, r0 + WINDOW)
        q_cols = []
        for m in range(N_Q_HEADS // 2):
            q_cols.append(z_ref[rows, C_Q + m * LANES:C_Q + (m + 1) * LANES] * cq[rows]
                          + z_ref[rows, C_QR + m * LANES:C_QR + (m + 1) * LANES] * sq[rows])
        q_nat = jnp.concatenate(
            [jnp.where(lo if (hd % 2 == 0) else ~lo, q_cols[hd // 2], 0.0) for hd in _NAT_HEADS], axis=0).astype(BF16)
        q_swp = jnp.concatenate(
            [jnp.where(lo if (hd % 2 == 0) else ~lo, q_cols[hd // 2], 0.0) for hd in _SWP_HEADS], axis=0).astype(BF16)
        if j == 0:
            kpn, kps, vpn, vps = kp_n[...], kp_s[...], vp_n[...], vp_s[...]
            m_prev = mask_prev_first
        else:
            prev = slice(r0 - WINDOW, r0)
            kpn, kps, vpn, vps = k_n[prev], k_s[prev], v_n[prev], v_s[prev]
            m_prev = mask_prev_band
        outs = []
        for q_stack, heads, kp, kc, vp, vc in ((q_nat, _NAT_HEADS, kpn, k_n[rows], vpn, v_n[rows]),
                                               (q_swp, _SWP_HEADS, kps, k_s[rows], vps, v_s[rows])):
            s_prev = _dot_nt(q_stack, kp)
            s_cur = _dot_nt(q_stack, kc)
            p_prev, p_cur = [], []
            for i, hd in enumerate(heads):
                pr = slice(i * WINDOW, (i + 1) * WINDOW)
                pp, pc = _softmax_with_sink(jnp.where(m_prev, s_prev[pr], NEG_INF),
                                            jnp.where(mask_cur, s_cur[pr], NEG_INF), sinks_ref[hd])
                p_prev.append(pp.astype(BF16))
                p_cur.append(pc.astype(BF16))
            outs.append(_dot(jnp.concatenate(p_prev, axis=0), vp) + _dot(jnp.concatenate(p_cur, axis=0), vc))
        o_nat, o_swp = outs
        for m in range(N_Q_HEADS // 2):
            pr = slice(m * WINDOW, (m + 1) * WINDOW)
            even_nat = (2 * m) in _NAT_HEADS
            att = jnp.where(lo, o_nat[pr], o_swp[pr]) if even_nat else jnp.where(lo, o_swp[pr], o_nat[pr])
            mix_ref[rows, m * LANES:(m + 1) * LANES] = att.astype(BF16)
        for m in range(GMLP_GROUPS // 2):
            cs = slice(m * LANES, (m + 1) * LANES)
            vcol = vn[rows, cs]
            rhs = jnp.concatenate([jnp.where(lo, vcol, jnp.zeros_like(vcol)),
                                   jnp.where(lo, jnp.zeros_like(vcol), vcol)], axis=0)
            sp = _dot(wcat[m], rhs) + bsf[:, cs]
            mix_ref[rows, ATTN_WIDTH + m * LANES:ATTN_WIDTH + (m + 1) * LANES] = (u[rows, cs] * sp).astype(BF16)

    last = slice(TOK_TILE - WINDOW, TOK_TILE)
    kp_n[...] = k_n[last]
    kp_s[...] = k_s[last]
    vp_n[...] = v_n[last]
    vp_s[...] = v_s[last]

    xmid = x + _dot(mix_ref[...], wout_ref[...])
    xmid_ref[...] = xmid
    h2b = _rmsnorm(xmid, g2_ref[...]).astype(BF16)
    slab = _route(h2b, wrt_ref, brt_ref)
    _local_sort(slab, h2b, upper_ref, lpad_ref, xl_ref, meta_ref, tab_ref)


def _local_sort(slab, h2b, upper_ref, lpad_ref, xl_ref, meta_ref, tab_ref):
    n_tok = h2b.shape[0]
    e1, e2 = slab[0:1], slab[1:2]
    row32 = lax.broadcasted_iota(jnp.int32, (N_EXPERTS, n_tok), 0).astype(F32)
    sel1 = row32 == e1
    sel2 = row32 == e2
    onehot = jnp.where(sel1, 1.0, jnp.where(sel2, 1.0, 0.0))
    earlier = _dot(onehot.astype(BF16), upper_ref[...])
    cnt = jnp.sum(onehot, axis=1, keepdims=True)
    pc = jnp.floor((cnt + (ROW_GRANULE - 1)) * (1.0 / ROW_GRANULE)) * ROW_GRANULE
    pc_b = jnp.broadcast_to(pc, (N_EXPERTS, LANES))
    pc_pad = jnp.concatenate([pc_b, jnp.zeros((LANES - N_EXPERTS, LANES), F32)], axis=0).astype(BF16)
    start = _dot(lpad_ref[...], pc_pad)
    base = start[:, 0:1] + earlier
    d1 = jnp.sum(jnp.where(sel1, base, 0.0), axis=0, keepdims=True)
    d2 = jnp.sum(jnp.where(sel2, base, 0.0), axis=0, keepdims=True)
    for c in range(SORT_ROWS // SORT_CHUNK):
        r_iota = (lax.broadcasted_iota(jnp.int32, (SORT_CHUNK, n_tok), 0) + c * SORT_CHUNK).astype(F32)
        perm = jnp.where(r_iota == d1, 1.0, jnp.where(r_iota == d2, 1.0, 0.0)).astype(BF16)
        xl_ref[c * SORT_CHUNK:(c + 1) * SORT_CHUNK, :] = _dot(perm, h2b).astype(BF16)
    row8 = lax.broadcasted_iota(jnp.int32, (8, n_tok), 0)
    meta_ref[...] = jnp.where(row8 == 0, d1, jnp.where(row8 == 1, d2, jnp.where(row8 >= 4, 0.0, slab)))
    lane = lax.broadcasted_iota(jnp.int32, (N_EXPERTS, LANES), 1)
    tab_ref[...] = jnp.where(lane == 0, pc_b, jnp.where(lane == 1, start, 0.0))


def _sample_mixer_kernel(sinks_ref, x_ref, ck_ref, cv_ref, g1_ref, win_ref, cos_ref, sin_ref, lng_ref, lnb_ref,
                         ws0_ref, bs0_ref, wout_ref, g2_ref, wrt_ref, brt_ref,
                         xmid_ref, h2_ref, meta_ref, kout_ref, vout_ref, vnout_ref,
                         mix_ref):
    n_seq = x_ref.shape[0]
    seq_chunk = 16
    x = x_ref[...]
    h = _rmsnorm(x, g1_ref[...]).astype(BF16)
    z = _dot(h, win_ref[...])
    cos = cos_ref[...]
    sin = sin_ref[...]
    kf = z[:, C_K:C_K + KV_WIDTH] * cos + z[:, C_KR:C_KR + KV_WIDTH] * sin
    vf = z[:, C_V:C_V + KV_WIDTH]
    kout_ref[...] = kf
    vout_ref[...] = vf
    scale = np.float32(HEAD_DIM ** -0.5)
    lane = lax.broadcasted_iota(jnp.int32, (n_seq, LANES), 1)
    lo = lane < HEAD_DIM
    kb = kf.astype(BF16).astype(F32)
    vb = vf.astype(BF16).astype(F32)

    q_heads = []
    for hd in range(N_Q_HEADS):
        m = hd // 2
        qc = (z[:, C_Q + m * LANES:C_Q + (m + 1) * LANES] * cos
              + z[:, C_QR + m * LANES:C_QR + (m + 1) * LANES] * sin) * scale
        keep = lo if hd % 2 == 0 else ~lo
        qm = jnp.where(keep, qc, 0.0)
        if (hd % 2) != (hd // (N_Q_HEADS // N_KV_HEADS)):
            qm = pltpu.roll(qm, HEAD_DIM, 1)
        q_heads.append(qm.astype(BF16))

    s_new = [jnp.sum(q_heads[hd].astype(F32) * kb, axis=-1, keepdims=True) for hd in range(N_Q_HEADS)]

    rr = lax.broadcasted_iota(jnp.int32, (N_Q_HEADS * seq_chunk, seq_chunk * WINDOW), 0)
    cc = lax.broadcasted_iota(jnp.int32, (N_Q_HEADS * seq_chunk, seq_chunk * WINDOW), 1)
    same_seq = (rr % seq_chunk) == (cc // WINDOW)
    kv_lo = lax.broadcasted_iota(jnp.int32, (seq_chunk, LANES), 1) < HEAD_DIM

    for c in range(n_seq // seq_chunk):
        sr = slice(c * seq_chunk, (c + 1) * seq_chunk)
        kc = ck_ref[sr].reshape(seq_chunk * WINDOW, KV_WIDTH).astype(BF16)
        vc = cv_ref[sr].reshape(seq_chunk * WINDOW, KV_WIDTH).astype(BF16)
        qs = jnp.concatenate([q_heads[hd][sr] for hd in range(N_Q_HEADS)], axis=0)
        s = jnp.where(same_seq, _dot_nt(qs, kc), NEG_INF)
        sn = jnp.concatenate([s_new[hd][sr] for hd in range(N_Q_HEADS)], axis=0)
        sink = jnp.concatenate([jnp.full((seq_chunk, 1), sinks_ref[hd], F32) for hd in range(N_Q_HEADS)], axis=0)
        m = jnp.maximum(jnp.maximum(jnp.max(s, axis=-1, keepdims=True), sn), sink)
        e = jnp.exp(s - m)
        en = jnp.exp(sn - m)
        inv = 1.0 / (jnp.sum(e, axis=-1, keepdims=True) + en + jnp.exp(sink - m))
        o = _dot((e * inv).astype(BF16), vc)
        pn = (en * inv).astype(BF16).astype(F32)
        for mcol in range(N_Q_HEADS // 2):
            halves = []
            for hd in (2 * mcol, 2 * mcol + 1):
                oh = o[hd * seq_chunk:(hd + 1) * seq_chunk] + pn[hd * seq_chunk:(hd + 1) * seq_chunk] * vb[sr]
                if (hd % 2) != (hd // (N_Q_HEADS // N_KV_HEADS)):
                    oh = pltpu.roll(oh, HEAD_DIM, 1)
                halves.append(oh)
            att = jnp.where(kv_lo, halves[0], halves[1])
            mix_ref[sr, mcol * LANES:(mcol + 1) * LANES] = att.astype(BF16)

    u = _gelu(z[:, C_U:C_U + GMLP_WIDTH])
    vn = _layernorm(_gelu(z[:, C_VG:C_VG + GMLP_WIDTH]), lng_ref[...], lnb_ref[...])
    vnout_ref[...] = vn
    sp = ws0_ref[...].astype(BF16).astype(F32) * vn.astype(BF16).astype(F32) + bs0_ref[...]
    mix_ref[:, ATTN_WIDTH:] = (u * sp).astype(BF16)

    xmid = x + _dot(mix_ref[...], wout_ref[...])
    xmid_ref[...] = xmid
    h2b = _rmsnorm(xmid, g2_ref[...]).astype(BF16)
    h2_ref[...] = h2b
    meta_ref[...] = _route(h2b, wrt_ref, brt_ref)


def _dense_moe_kernel(t_ref, xmid_ref, comb_ref, wg_ref, wu_ref, wd_ref, gf_ref, y_ref, acc_ref):
    e = pl.program_id(1)

    @pl.when(e == 0)
    def _():
        acc_ref[...] = xmid_ref[...]

    tb = t_ref[...]
    gate = _dot(tb, wg_ref[...])
    up = _dot(tb, wu_ref[...])
    hid = (gate * (1.0 / (1.0 + jnp.exp(-gate))) * up).astype(BF16)
    out = _dot(hid, wd_ref[...])
    lane = lax.broadcasted_iota(jnp.int32, comb_ref.shape, 1)
    c_e = jnp.sum(jnp.where(lane == e, comb_ref[...], 0.0), axis=-1, keepdims=True)
    acc_ref[...] += c_e * out

    @pl.when(e == pl.num_programs(1) - 1)
    def _():
        y_ref[...] = _rmsnorm(acc_ref[...], gf_ref[...])


def _grouped_ffn_kernel(texp_ref, code_ref, ntiles_ref, xl_hbm, wg_ref, wu_ref, wd_ref, yl_hbm,
                        xbuf, ybuf, zbuf, gsem, ssem, tsem):
    del texp_ref
    j = pl.program_id(0)
    n_tiles = ntiles_ref[0]
    spare_tile = yl_hbm.shape[0] - 1

    def slot_addr(tile, s, pad_tile, pad_row):
        code = code_ref[tile * FFN_SLOTS + s]
        valid = code >= 0
        tok = jnp.where(valid, code >> CODE_SHIFT, pad_tile)
        row = jnp.where(valid, (code & ((1 << CODE_SHIFT) - 1)) * ROW_GRANULE, pad_row)
        return tok, pl.multiple_of(row, ROW_GRANULE)

    def gather_copy(tile, s, b):
        tok, row = slot_addr(tile, s, 0, s * ROW_GRANULE)
        return pltpu.make_async_copy(xl_hbm.at[tok, pl.ds(row, ROW_GRANULE)],
                                     xbuf.at[b, pl.ds(s * ROW_GRANULE, ROW_GRANULE)], gsem.at[b])

    def scatter_copy(tile, s, b):
        tok, row = slot_addr(tile, s, spare_tile, (b * FFN_SLOTS + s) * ROW_GRANULE)
        return pltpu.make_async_copy(ybuf.at[b, pl.ds(s * ROW_GRANULE, ROW_GRANULE)],
                                     yl_hbm.at[tok, pl.ds(row, ROW_GRANULE)], ssem.at[b])

    def spare_init_copy(c):
        return pltpu.make_async_copy(zbuf, yl_hbm.at[spare_tile, pl.ds(c * FFN_ROWS, FFN_ROWS)], tsem.at[0])

    @pl.when(j < n_tiles)
    def _():
        b = j % 2

        @pl.when(j == 0)
        def _():
            zbuf[...] = jnp.zeros_like(zbuf)
            for c in range(SORT_ROWS // FFN_ROWS):
                spare_init_copy(c).start()
            for s in range(FFN_SLOTS):
                gather_copy(0, s, 0).start()
            for c in range(SORT_ROWS // FFN_ROWS):
                spare_init_copy(c).wait()

        @pl.when(j + 1 < n_tiles)
        def _():
            for s in range(FFN_SLOTS):
                gather_copy(j + 1, s, 1 - b).start()

        for s in range(FFN_SLOTS):
            gather_copy(j, s, b).wait()

        @pl.when(j >= 2)
        def _():
            for s in range(FFN_SLOTS):
                scatter_copy(j - 2, s, b).wait()

        x = xbuf[b]
        gate = _dot(x, wg_ref[...])
        up = _dot(x, wu_ref[...])
        hid = (gate * (1.0 / (1.0 + jnp.exp(-gate))) * up).astype(BF16)
        ybuf[b] = _dot(hid, wd_ref[...]).astype(BF16)
        for s in range(FFN_SLOTS):
            scatter_copy(j, s, b).start()

        @pl.when(j == n_tiles - 1)
        def _():
            for s in range(FFN_SLOTS):
                scatter_copy(j, s, b).wait()

            @pl.when(j >= 1)
            def _():
                for s in range(FFN_SLOTS):
                    scatter_copy(j - 1, s, 1 - b).wait()


def _combine_kernel(xmid_ref, yl_ref, meta_ref, gf_ref, y_ref):
    n_tok = xmid_ref.shape[0]
    meta = meta_ref[...]
    meta_t = jnp.concatenate([meta, jnp.zeros((LANES - 8, n_tok), F32)], axis=0).T
    d1, d2, w1, w2 = meta_t[:, 0:1], meta_t[:, 1:2], meta_t[:, 2:3], meta_t[:, 3:4]
    acc = xmid_ref[...]
    for c in range(SORT_ROWS // SORT_CHUNK):
        r_iota = (lax.broadcasted_iota(jnp.int32, (n_tok, SORT_CHUNK), 1) + c * SORT_CHUNK).astype(F32)
        unsort = jnp.where(r_iota == d1, w1, jnp.where(r_iota == d2, w2, 0.0)).astype(BF16)
        acc = acc + _dot(unsort, yl_ref[c * SORT_CHUNK:(c + 1) * SORT_CHUNK, :])
    y_ref[...] = _rmsnorm(acc, gf_ref[...])


def _ffn_schedule(tab):
    n_tok_tiles = tab.shape[0]
    strips = (tab[:, :, 0] * (1.0 / ROW_GRANULE)).astype(jnp.int32)
    starts = (tab[:, :, 1] * (1.0 / ROW_GRANULE)).astype(jnp.int32)
    used = jnp.sum(strips, axis=1)
    cnt = jnp.concatenate([strips.T, (GRANULES_PER_TILE - used)[None, :]], axis=0)
    row0 = jnp.concatenate([starts.T, used[None, :]], axis=0)
    cs = jnp.cumsum(cnt, axis=1) - cnt
    n_str = jnp.sum(cnt, axis=1)
    np_str = (n_str + FFN_SLOTS - 1) // FFN_SLOTS * FFN_SLOTS
    ends = jnp.cumsum(np_str)
    base = ends - np_str
    n_steps_max = _ffn_steps_max(n_tok_tiles)
    step0 = jnp.arange(n_steps_max, dtype=jnp.int32) * FFN_SLOTS
    stream = jnp.minimum(jnp.sum(ends[None, :] <= step0[:, None], axis=1), N_STREAMS - 1)
    pick = stream[:, None] == jnp.arange(N_STREAMS, dtype=jnp.int32)[None, :]
    sel = lambda x: jnp.sum(jnp.where(pick[:, :, None], x[None], 0), axis=1)
    cs_j, cnt_j, row0_j = sel(cs), sel(cnt), sel(row0)
    base_j = jnp.sum(jnp.where(pick, base[None, :], 0), axis=1)
    q = step0[:, None] + jnp.arange(FFN_SLOTS, dtype=jnp.int32)[None, :] - base_j[:, None]
    reached = cs_j[:, None, :] <= q[:, :, None]
    last = lambda x: jnp.sum(jnp.where(reached, jnp.diff(x, axis=1, prepend=0)[:, None, :], 0), axis=2)
    tile_idx = jnp.sum(reached, axis=2).astype(jnp.int32) - 1
    g = q - last(cs_j)
    code = jnp.where(g < last(cnt_j), (tile_idx << CODE_SHIFT) | (last(row0_j) + g), -1).astype(jnp.int32)
    texp = jnp.minimum(stream, N_EXPERTS - 1).astype(jnp.int32)
    n_steps = (ends[-1] // FFN_SLOTS).astype(jnp.int32).reshape(1)
    return texp, code.reshape(-1), n_steps


N_STREAMS = N_EXPERTS + 1


def _ffn_steps_max(n_tok_tiles):
    return -(-(n_tok_tiles * GRANULES_PER_TILE + N_STREAMS * (FFN_SLOTS - 1)) // FFN_SLOTS)


def _vmem_limit(n_bytes):
    return int(min(n_bytes, V7X_VMEM_BYTES - 4 * 1024 * 1024))


def _rot_cols(w, n_heads):
    d = w.shape[0]
    w4 = w.reshape(d, n_heads, 2, HALF)
    return jnp.concatenate([-w4[:, :, 1], w4[:, :, 0]], axis=-1).reshape(d, n_heads * HEAD_DIM)


def _rope_tables(pos):
    inv_freq = ROPE_THETA ** (-jnp.arange(HALF, dtype=F32) * 2.0 / HEAD_DIM)
    ang = pos.astype(F32)[:, None] * inv_freq[None, :]
    return jnp.tile(jnp.cos(ang), (1, LANES // HALF)), jnp.tile(jnp.sin(ang), (1, LANES // HALF))


def _combine_matrix(meta):
    m = jnp.swapaxes(meta, 1, 2).reshape(-1, 8)
    ids = m[:, 0:2].astype(jnp.int32)
    return jnp.sum(jax.nn.one_hot(ids, N_EXPERTS, dtype=F32) * m[:, 2:4, None], axis=1)


def _dense_moe(t, xmid, comb, wg, wu, wd, gf, tile):
    n = t.shape[0]
    comb = jnp.pad(comb, ((0, 0), (0, LANES - N_EXPERTS)))
    return pl.pallas_call(
        _dense_moe_kernel,
        grid=(n // tile, N_EXPERTS),
        in_specs=[
            pl.BlockSpec((tile, D_MODEL), lambda i, e: (i, 0)),
            pl.BlockSpec((tile, D_MODEL), lambda i, e: (i, 0)),
            pl.BlockSpec((tile, LANES), lambda i, e: (i, 0)),
            pl.BlockSpec((None, D_MODEL, D_EXPERT), lambda i, e: (e, 0, 0)),
            pl.BlockSpec((None, D_MODEL, D_EXPERT), lambda i, e: (e, 0, 0)),
            pl.BlockSpec((None, D_EXPERT, D_MODEL), lambda i, e: (e, 0, 0)),
            pl.BlockSpec((1, D_MODEL), lambda i, e: (0, 0)),
        ],
        out_specs=pl.BlockSpec((tile, D_MODEL), lambda i, e: (i, 0)),
        out_shape=jax.ShapeDtypeStruct((n, D_MODEL), F32),
        scratch_shapes=[pltpu.VMEM((tile, D_MODEL), F32)],
        compiler_params=pltpu.CompilerParams(
            dimension_semantics=("arbitrary", "arbitrary"),
            vmem_limit_bytes=_vmem_limit(48 * 1024 * 1024)),
        name="dense_moe",
    )(t, xmid, comb, wg, wu, wd, gf)


def kernel(x_prompt, x_sample, cache_swa_k, cache_swa_v, norm_mix_g, w_in, attn_sinks, gmlp_ln_g, gmlp_ln_b,
           gmlp_w_s, gmlp_b_s, w_out, norm_ffn_g, router_group_w, router_group_b, router_expert_w,
           router_expert_b, expert_w_gate, expert_w_up, expert_w_down, final_norm_g):
    assert norm_mix_g.shape[0] == 1, "single-layer trunk"
    batch, seq, _ = x_prompt.shape
    dec_batch = x_sample.shape[0]
    assert x_sample.shape[1] == 1 and seq % TOK_TILE == 0

    w = w_in[0]
    wq, wk = w[:, :ATTN_WIDTH], w[:, ATTN_WIDTH:ATTN_WIDTH + KV_WIDTH]
    win_ext = jnp.concatenate(
        [wq, _rot_cols(wq, N_Q_HEADS), wk, _rot_cols(wk, N_KV_HEADS), w[:, ATTN_WIDTH + KV_WIDTH:]],
        axis=1).astype(BF16)
    wout = w_out[0].astype(BF16)
    g1 = norm_mix_g[0][None, :]
    g2 = norm_ffn_g[0][None, :]
    gf = final_norm_g[None, :]
    lng = gmlp_ln_g[0][None, :]
    lnb = gmlp_ln_b[0][None, :]
    sinks = attn_sinks[0]
    ws = gmlp_w_s[0]
    group_dim = GMLP_WIDTH // GMLP_GROUPS
    bsf = jnp.repeat(gmlp_b_s[0].T, group_dim, axis=1)
    ws0 = jnp.repeat(ws[:, 0, 0], group_dim)[None, :]
    bs0 = jnp.repeat(gmlp_b_s[0][:, 0], group_dim)[None, :]
    wrt = jnp.zeros((ROUTER_ROWS, D_MODEL), F32)
    wrt = wrt.at[0:N_GROUPS].set(router_group_w[0].T)
    wrt = wrt.at[8:8 + N_EXPERTS].set(router_expert_w[0].reshape(D_MODEL, N_EXPERTS).T).astype(BF16)
    brt = jnp.full((ROUTER_ROWS, 1), NEG_INF, F32)
    brt = brt.at[0:N_GROUPS, 0].set(router_group_b[0])
    brt = brt.at[8:8 + N_EXPERTS, 0].set(router_expert_b[0].reshape(N_EXPERTS))
    wg = expert_w_gate[0].astype(BF16)
    wu = expert_w_up[0].astype(BF16)
    wd = expert_w_down[0].astype(BF16)
    cos_p, sin_p = _rope_tables(jnp.arange(seq, dtype=jnp.int32))
    cos_s, sin_s = _rope_tables(PAST_LEN + jnp.arange(1, dtype=jnp.int32))

    full = lambda shape: pl.BlockSpec(shape, lambda *_: (0,) * len(shape))
    smem = pl.BlockSpec(memory_space=pltpu.SMEM)
    n_tiles = seq // TOK_TILE

    upper = jnp.triu(jnp.ones((TOK_TILE, TOK_TILE), BF16), k=1)
    lpad = (jnp.arange(LANES)[None, :] < jnp.arange(N_EXPERTS)[:, None]).astype(BF16)
    n_tok_tiles = batch * n_tiles

    xmid_p, xl_p, meta_p, tab_p, k_p, v_p = pl.pallas_call(
        _prompt_mixer_kernel,
        grid=(batch, n_tiles),
        in_specs=[
            smem,
            pl.BlockSpec((None, TOK_TILE, D_MODEL), lambda b, t: (b, t, 0)),
            full((1, D_MODEL)),
            full((D_MODEL, IN_EXT)),
            pl.BlockSpec((TOK_TILE, LANES), lambda b, t: (t, 0)),
            pl.BlockSpec((TOK_TILE, LANES), lambda b, t: (t, 0)),
            full((1, GMLP_WIDTH)),
            full((1, GMLP_WIDTH)),
            full((GMLP_GROUPS, CHUNK, CHUNK)),
            full((CHUNK, GMLP_WIDTH)),
            full((D_MODEL, D_MODEL)),
            full((1, D_MODEL)),
            full((ROUTER_ROWS, D_MODEL)),
            full((ROUTER_ROWS, 1)),
            full((TOK_TILE, TOK_TILE)),
            full((N_EXPERTS, LANES)),
        ],
        out_specs=[
            pl.BlockSpec((None, TOK_TILE, D_MODEL), lambda b, t: (b, t, 0)),
            pl.BlockSpec((None, SORT_ROWS, D_MODEL), lambda b, t: (b * n_tiles + t, 0, 0)),
            pl.BlockSpec((None, 8, TOK_TILE), lambda b, t: (b * n_tiles + t, 0, 0)),
            pl.BlockSpec((None, N_EXPERTS, LANES), lambda b, t: (b * n_tiles + t, 0, 0)),
            pl.BlockSpec((None, WINDOW, KV_WIDTH), lambda b, t: (b, 0, 0)),
            pl.BlockSpec((None, WINDOW, KV_WIDTH), lambda b, t: (b, 0, 0)),
        ],
        out_shape=[
            jax.ShapeDtypeStruct((batch, seq, D_MODEL), F32),
            jax.ShapeDtypeStruct((n_tok_tiles, SORT_ROWS, D_MODEL), BF16),
            jax.ShapeDtypeStruct((n_tok_tiles, 8, TOK_TILE), F32),
            jax.ShapeDtypeStruct((n_tok_tiles, N_EXPERTS, LANES), F32),
            jax.ShapeDtypeStruct((batch, WINDOW, KV_WIDTH), F32),
            jax.ShapeDtypeStruct((batch, WINDOW, KV_WIDTH), F32),
        ],
        scratch_shapes=[
            pltpu.VMEM((TOK_TILE, IN_EXT), F32),
            pltpu.VMEM((TOK_TILE, D_MODEL), BF16),
            pltpu.VMEM((WINDOW, KV_WIDTH), BF16),
            pltpu.VMEM((WINDOW, KV_WIDTH), BF16),
            pltpu.VMEM((WINDOW, KV_WIDTH), BF16),
            pltpu.VMEM((WINDOW, KV_WIDTH), BF16),
        ],
        compiler_params=pltpu.CompilerParams(
            dimension_semantics=("arbitrary", "arbitrary"),
            vmem_limit_bytes=_vmem_limit(56 * 1024 * 1024)),
        name="prompt_mixer",
    )(sinks, x_prompt, g1, win_ext, cos_p, sin_p, lng, lnb, ws, bsf, wout, g2, wrt, brt, upper, lpad)

    xs = x_sample.reshape(dec_batch, D_MODEL)
    ck = cache_swa_k[0].reshape(dec_batch, WINDOW, KV_WIDTH)
    cv = cache_swa_v[0].reshape(dec_batch, WINDOW, KV_WIDTH)
    xmid_s, h2_s, meta_s, k_s, v_s, vn_s = pl.pallas_call(
        _sample_mixer_kernel,
        in_specs=[smem] + [pl.BlockSpec(memory_space=pltpu.VMEM)] * 15,
        out_shape=[
            jax.ShapeDtypeStruct((dec_batch, D_MODEL), F32),
            jax.ShapeDtypeStruct((dec_batch, D_MODEL), BF16),
            jax.ShapeDtypeStruct((8, dec_batch), F32),
            jax.ShapeDtypeStruct((dec_batch, KV_WIDTH), F32),
            jax.ShapeDtypeStruct((dec_batch, KV_WIDTH), F32),
            jax.ShapeDtypeStruct((dec_batch, GMLP_WIDTH), F32),
        ],
        scratch_shapes=[pltpu.VMEM((dec_batch, D_MODEL), BF16)],
        compiler_params=pltpu.CompilerParams(vmem_limit_bytes=_vmem_limit(56 * 1024 * 1024)),
        name="sample_mixer",
    )(sinks, xs, ck, cv, g1, win_ext, cos_s, sin_s, lng, lnb, ws0, bs0, wout, g2, wrt, brt)

    texp, code, n_steps = _ffn_schedule(tab_p)
    yl_p = pl.pallas_call(
        _grouped_ffn_kernel,
        grid_spec=pltpu.PrefetchScalarGridSpec(
            num_scalar_prefetch=3,
            grid=(_ffn_steps_max(n_tok_tiles),),
            in_specs=[
                pl.BlockSpec(memory_space=pl.ANY),
                pl.BlockSpec((None, D_MODEL, D_EXPERT), lambda j, te, cd, ns: (te[j], 0, 0)),
                pl.BlockSpec((None, D_MODEL, D_EXPERT), lambda j, te, cd, ns: (te[j], 0, 0)),
                pl.BlockSpec((None, D_EXPERT, D_MODEL), lambda j, te, cd, ns: (te[j], 0, 0)),
            ],
            out_specs=pl.BlockSpec(memory_space=pl.ANY),
            scratch_shapes=[
                pltpu.VMEM((2, FFN_ROWS, D_MODEL), BF16),
                pltpu.VMEM((2, FFN_ROWS, D_MODEL), BF16),
                pltpu.VMEM((FFN_ROWS, D_MODEL), BF16),
                pltpu.SemaphoreType.DMA((2,)),
                pltpu.SemaphoreType.DMA((2,)),
                pltpu.SemaphoreType.DMA((1,)),
            ],
        ),
        out_shape=jax.ShapeDtypeStruct((n_tok_tiles + 1, SORT_ROWS, D_MODEL), BF16),
        compiler_params=pltpu.CompilerParams(
            dimension_semantics=("arbitrary",),
            vmem_limit_bytes=_vmem_limit(32 * 1024 * 1024)),
        name="grouped_ffn",
    )(texp, code, n_steps, xl_p, wg, wu, wd)

    y_p = pl.pallas_call(
        _combine_kernel,
        grid=(n_tok_tiles,),
        in_specs=[
            pl.BlockSpec((TOK_TILE, D_MODEL), lambda i: (i, 0)),
            pl.BlockSpec((None, SORT_ROWS, D_MODEL), lambda i: (i, 0, 0)),
            pl.BlockSpec((None, 8, TOK_TILE), lambda i: (i, 0, 0)),
            pl.BlockSpec((1, D_MODEL), lambda i: (0, 0)),
        ],
        out_specs=pl.BlockSpec((TOK_TILE, D_MODEL), lambda i: (i, 0)),
        out_shape=jax.ShapeDtypeStruct((batch * seq, D_MODEL), F32),
        compiler_params=pltpu.CompilerParams(
            dimension_semantics=("arbitrary",),
            vmem_limit_bytes=_vmem_limit(40 * 1024 * 1024)),
        name="moe_combine",
    )(xmid_p.reshape(batch * seq, D_MODEL), yl_p, meta_p, gf)

    comb_s = _combine_matrix(meta_s[None])
    y_s = _dense_moe(h2_s, xmid_s, comb_s, wg, wu, wd, gf, dec_batch)

    return (y_p.reshape(batch, seq, D_MODEL),
            y_s.reshape(dec_batch, 1, D_MODEL),
            k_p.reshape(1, batch, WINDOW, N_KV_HEADS, HEAD_DIM),
            v_p.reshape(1, batch, WINDOW, N_KV_HEADS, HEAD_DIM),
            k_s.reshape(1, dec_batch, 1, N_KV_HEADS, HEAD_DIM),
            v_s.reshape(1, dec_batch, 1, N_KV_HEADS, HEAD_DIM),
            vn_s.reshape(1, dec_batch, 1, GMLP_WIDTH))
```

```python
import jax
import jax.numpy as jnp
import numpy as np
from jax import lax
from jax.experimental import pallas as pl
from jax.experimental.pallas import tpu as pltpu

F32 = jnp.float32
BF16 = jnp.bfloat16

D_MODEL = 1024
HEAD_DIM = 64
HALF = HEAD_DIM // 2
N_Q_HEADS = 8
N_KV_HEADS = 2
ATTN_WIDTH = N_Q_HEADS * HEAD_DIM
KV_WIDTH = N_KV_HEADS * HEAD_DIM
WINDOW = 128
ROPE_THETA = 10000.0
GMLP_WIDTH = D_MODEL - ATTN_WIDTH
GMLP_GROUPS = 8
CHUNK = 128
N_GROUPS = 4
EXPERTS_PER_GROUP = 8
N_EXPERTS = N_GROUPS * EXPERTS_PER_GROUP
D_EXPERT = 256
EPS = 1e-6
NEG_INF = -1e30
PAST_LEN = 16384

LANES = 128
V7X_VMEM_BYTES = 64 * 1024 * 1024

C_Q = 0
C_K = C_Q + ATTN_WIDTH
C_V = C_K + KV_WIDTH
C_U = C_V + KV_WIDTH
C_VG = C_U + GMLP_WIDTH
IN_WIDTH = C_VG + GMLP_WIDTH

ROUTER_ROWS = 48
TOK_TILE = 512
BLOCKS_PER_TILE = TOK_TILE // WINDOW
ROW_GRANULE = 16
SORT_CHUNK = 512
SORT_ROWS = -(-(2 * TOK_TILE + N_EXPERTS * (ROW_GRANULE - 1)) // SORT_CHUNK) * SORT_CHUNK
GRANULES_PER_TILE = SORT_ROWS // ROW_GRANULE
FFN_ROWS = 512
FFN_SLOTS = FFN_ROWS // ROW_GRANULE
CODE_SHIFT = 7
assert GRANULES_PER_TILE <= (1 << CODE_SHIFT)


def _dot(a, b):
    return jnp.dot(a, b, preferred_element_type=F32)


def _dot_nt(a, b):
    return lax.dot_general(a, b, (((1,), (1,)), ((), ())), preferred_element_type=F32)


def _gelu(x):
    return 0.5 * x * (1.0 + lax.erf(x * np.float32(np.sqrt(0.5))))


def _rmsnorm(x, g):
    return x * lax.rsqrt(jnp.mean(x * x, axis=-1, keepdims=True) + EPS) * g


def _layernorm(x, g, b):
    mu = jnp.mean(x, axis=-1, keepdims=True)
    xc = x - mu
    return xc * lax.rsqrt(jnp.mean(xc * xc, axis=-1, keepdims=True) + EPS) * g + b


def _first_argmax_rows(x, row_iota, n_rows):
    m = jnp.max(x, axis=0, keepdims=True)
    idx = jnp.min(jnp.where(x == m, row_iota, n_rows), axis=0, keepdims=True)
    return m, idx


def _route(h2b, wrt_ref, brt_ref):
    n_tok = h2b.shape[0]
    lt = _dot_nt(wrt_ref[...], h2b) + brt_ref[...]
    row8 = lax.broadcasted_iota(jnp.int32, (8, n_tok), 0).astype(F32)
    glog = lt[0:8]
    gmax, gidx = _first_argmax_rows(glog, row8, 8)
    g_w = 1.0 / jnp.sum(jnp.exp(glog - gmax), axis=0, keepdims=True)
    esel = lt[8:16]
    for g in range(1, N_GROUPS):
        esel = jnp.where(gidx == g, lt[8 + 8 * g:16 + 8 * g], esel)
    m1, i1 = _first_argmax_rows(esel, row8, 8)
    esel2 = jnp.where(row8 == i1, -jnp.inf, esel)
    m2, i2 = _first_argmax_rows(esel2, row8, 8)
    r = jnp.exp(m2 - m1)
    w1 = 1.0 / (1.0 + r)
    w2 = r / (1.0 + r)
    e1 = gidx * EXPERTS_PER_GROUP + i1
    e2 = gidx * EXPERTS_PER_GROUP + i2
    slab = jnp.where(row8 == 0, e1, 0.0)
    for r_idx, val in ((1, e2), (2, w1 * g_w), (3, w2 * g_w)):
        slab = jnp.where(row8 == r_idx, val, slab)
    return slab


def _softmax_with_sink(s, sink):
    m = jnp.maximum(jnp.max(s, axis=-1, keepdims=True), sink)
    e = jnp.exp(s - m)
    den = jnp.sum(e, axis=-1, keepdims=True) + jnp.exp(sink - m)
    return e * (1.0 / den)


def _rope(x, cos, sin_signed):
    first_half = (lax.broadcasted_iota(jnp.int32, x.shape, 1) & HALF) == 0
    partner = jnp.where(first_half, pltpu.roll(x, LANES - HALF, 1), pltpu.roll(x, HALF, 1))
    return x * cos + partner * sin_signed


_NAT_HEADS = (0, 2, 5, 7)
_SWP_HEADS = (1, 3, 4, 6)


def _prompt_mixer_kernel(sinks_ref, x_ref, g1_ref, win_ref, cos_ref, sin_ref, lng_ref, lnb_ref, ws_ref, bsf_ref,
                         wout_ref, g2_ref, wrt_ref, brt_ref, upper_ref, lpad_ref,
                         xmid_ref, xl_ref, meta_ref, tab_ref, kout_ref, vout_ref,
                         z_ref, mix_ref, k_n, k_s, v_n, v_s, q_nat, q_swp, s_ref, p_ref, vn_ref):
    t = pl.program_id(1)
    kv_bufs = (k_n, k_s, v_n, v_s)

    @pl.when(t == 0)
    def _():
        for ref in kv_bufs:
            ref[0:WINDOW, :] = jnp.zeros((WINDOW, KV_WIDTH), BF16)

    x = x_ref[...]
    h = _rmsnorm(x, g1_ref[...]).astype(BF16)
    z_ref[...] = _dot(h, win_ref[...])

    cos = cos_ref[...]
    sin = sin_ref[...]
    lane = lax.broadcasted_iota(jnp.int32, (WINDOW, LANES), 1)
    lo = lane < HEAD_DIM
    row = lax.broadcasted_iota(jnp.int32, (WINDOW, WINDOW), 0)
    col = lax.broadcasted_iota(jnp.int32, (WINDOW, WINDOW), 1)
    mask_cur = col <= row
    mask_prev_band = col >= row
    mask_prev_first = jnp.logical_and(mask_prev_band, (jnp.zeros_like(row) + t) > 0)
    mask_band = jnp.concatenate([mask_prev_band, mask_cur], axis=1)
    mask_first = jnp.concatenate([mask_prev_first, mask_cur], axis=1)

    cq = cos * np.float32(HEAD_DIM ** -0.5)
    sq = sin * np.float32(HEAD_DIM ** -0.5)
    kf = _rope(z_ref[:, C_K:C_K + KV_WIDTH], cos, sin)
    vf = z_ref[:, C_V:C_V + KV_WIDTH]
    k_n[WINDOW:, :] = kf.astype(BF16)
    k_s[WINDOW:, :] = pltpu.roll(kf, HEAD_DIM, 1).astype(BF16)
    v_n[WINDOW:, :] = vf.astype(BF16)
    v_s[WINDOW:, :] = pltpu.roll(vf, HEAD_DIM, 1).astype(BF16)

    @pl.when(t == pl.num_programs(1) - 1)
    def _():
        kout_ref[...] = kf[TOK_TILE - WINDOW:]
        vout_ref[...] = vf[TOK_TILE - WINDOW:]

    lo_t = lax.broadcasted_iota(jnp.int32, (TOK_TILE, LANES), 1) < HEAD_DIM
    for m in range(N_Q_HEADS // 2):
        qc = _rope(z_ref[:, C_Q + m * LANES:C_Q + (m + 1) * LANES], cq, sq)
        for hd, qh in ((2 * m, jnp.where(lo_t, qc, 0.0)), (2 * m + 1, jnp.where(lo_t, 0.0, qc))):
            if hd in _NAT_HEADS:
                q_nat[_NAT_HEADS.index(hd)] = qh.astype(BF16)
            else:
                q_swp[_SWP_HEADS.index(hd)] = qh.astype(BF16)

    stacks = ((q_nat, _NAT_HEADS, k_n, v_n), (q_swp, _SWP_HEADS, k_s, v_s))
    for j in range(BLOCKS_PER_TILE):
        rows = slice(j * WINDOW, (j + 1) * WINDOW)
        keys = slice(j * WINDOW, (j + 2) * WINDOW)
        for si, (q_ref, _, k_buf, _) in enumerate(stacks):
            q_stack = jnp.concatenate([q_ref[i, rows, :] for i in range(len(_NAT_HEADS))], axis=0)
            s_ref[2 * j + si] = _dot_nt(q_stack, k_buf[keys, :])
    for j in range(BLOCKS_PER_TILE):
        mask = mask_first if j == 0 else mask_band
        for si, (_, heads, _, _) in enumerate(stacks):
            for i, hd in enumerate(heads):
                pr = slice(i * WINDOW, (i + 1) * WINDOW)
                p_ref[2 * j + si, pr, :] = _softmax_with_sink(
                    jnp.where(mask, s_ref[2 * j + si, pr, :], NEG_INF), sinks_ref[hd]).astype(BF16)
    for j in range(BLOCKS_PER_TILE):
        rows = slice(j * WINDOW, (j + 1) * WINDOW)
        keys = slice(j * WINDOW, (j + 2) * WINDOW)
        o_nat = _dot(p_ref[2 * j], v_n[keys, :])
        o_swp = _dot(p_ref[2 * j + 1], v_s[keys, :])
        for m in range(N_Q_HEADS // 2):
            pr = slice(m * WINDOW, (m + 1) * WINDOW)
            even_nat = (2 * m) in _NAT_HEADS
            att = jnp.where(lo, o_nat[pr], o_swp[pr]) if even_nat else jnp.where(lo, o_swp[pr], o_nat[pr])
            mix_ref[rows, m * LANES:(m + 1) * LANES] = att.astype(BF16)

    vn_ref[...] = _layernorm(_gelu(z_ref[:, C_VG:C_VG + GMLP_WIDTH]), lng_ref[...], lnb_ref[...]).astype(BF16)
    for m in range(GMLP_GROUPS // 2):
        cs = slice(m * LANES, (m + 1) * LANES)
        w0 = jnp.where(mask_cur, ws_ref[2 * m], 0.0).astype(BF16)
        w1 = jnp.where(mask_cur, ws_ref[2 * m + 1], 0.0).astype(BF16)
        wcat = jnp.concatenate([w0, w1], axis=1)
        for j in range(BLOCKS_PER_TILE):
            rows = slice(j * WINDOW, (j + 1) * WINDOW)
            vcol = vn_ref[rows, cs]
            rhs = jnp.concatenate([jnp.where(lo, vcol, jnp.zeros_like(vcol)),
                                   jnp.where(lo, jnp.zeros_like(vcol), vcol)], axis=0)
            sp = _dot(wcat, rhs) + bsf_ref[:, cs]
            u = _gelu(z_ref[rows, C_U + m * LANES:C_U + (m + 1) * LANES])
            mix_ref[rows, ATTN_WIDTH + m * LANES:ATTN_WIDTH + (m + 1) * LANES] = (u * sp).astype(BF16)

    for ref in kv_bufs:
        ref[0:WINDOW, :] = ref[TOK_TILE:TOK_TILE + WINDOW, :]

    xmid = x + _dot(mix_ref[...], wout_ref[...])
    xmid_ref[...] = xmid
    h2b = _rmsnorm(xmid, g2_ref[...]).astype(BF16)
    slab = _route(h2b, wrt_ref, brt_ref)
    _local_sort(slab, h2b, upper_ref, lpad_ref, xl_ref, meta_ref, tab_ref)


def _local_sort(slab, h2b, upper_ref, lpad_ref, xl_ref, meta_ref, tab_ref):
    n_tok = h2b.shape[0]
    e1, e2 = slab[0:1], slab[1:2]
    row32 = lax.broadcasted_iota(jnp.int32, (N_EXPERTS, n_tok), 0).astype(F32)
    sel1 = row32 == e1
    sel2 = row32 == e2
    onehot = jnp.where(sel1, 1.0, jnp.where(sel2, 1.0, 0.0))
    earlier = _dot(onehot.astype(BF16), upper_ref[...])
    cnt = jnp.sum(onehot, axis=1, keepdims=True)
    pc = jnp.floor((cnt + (ROW_GRANULE - 1)) * (1.0 / ROW_GRANULE)) * ROW_GRANULE
    pc_b = jnp.broadcast_to(pc, (N_EXPERTS, LANES))
    pc_pad = jnp.concatenate([pc_b, jnp.zeros((LANES - N_EXPERTS, LANES), F32)], axis=0).astype(BF16)
    start = _dot(lpad_ref[...], pc_pad)
    base = start[:, 0:1] + earlier
    d1 = jnp.sum(jnp.where(sel1, base, 0.0), axis=0, keepdims=True)
    d2 = jnp.sum(jnp.where(sel2, base, 0.0), axis=0, keepdims=True)
    for c in range(SORT_ROWS // SORT_CHUNK):
        r_iota = (lax.broadcasted_iota(jnp.int32, (SORT_CHUNK, n_tok), 0) + c * SORT_CHUNK).astype(F32)
        perm = jnp.where(r_iota == d1, 1.0, jnp.where(r_iota == d2, 1.0, 0.0)).astype(BF16)
        xl_ref[c * SORT_CHUNK:(c + 1) * SORT_CHUNK, :] = _dot(perm, h2b).astype(BF16)
    row8 = lax.broadcasted_iota(jnp.int32, (8, n_tok), 0)
    meta_ref[...] = jnp.where(row8 == 0, d1, jnp.where(row8 == 1, d2, jnp.where(row8 >= 4, 0.0, slab)))
    lane = lax.broadcasted_iota(jnp.int32, (N_EXPERTS, LANES), 1)
    tab_ref[...] = jnp.where(lane == 0, pc_b, jnp.where(lane == 1, start, 0.0))


def _sample_mixer_kernel(sinks_ref, x_ref, ck_ref, cv_ref, g1_ref, win_ref, cos_ref, sin_ref, lng_ref, lnb_ref,
                         ws0_ref, bs0_ref, wout_ref, g2_ref, wrt_ref, brt_ref,
                         xmid_ref, h2_ref, meta_ref, kout_ref, vout_ref, vnout_ref,
                         mix_ref):
    n_seq = x_ref.shape[0]
    seq_chunk = 16
    x = x_ref[...]
    h = _rmsnorm(x, g1_ref[...]).astype(BF16)
    z = _dot(h, win_ref[...])
    cos = cos_ref[...]
    sin = sin_ref[...]
    scale = np.float32(HEAD_DIM ** -0.5)
    lane = lax.broadcasted_iota(jnp.int32, (n_seq, LANES), 1)
    lo = lane < HEAD_DIM
    kf = _rope(z[:, C_K:C_K + KV_WIDTH], cos, sin)
    vf = z[:, C_V:C_V + KV_WIDTH]
    kout_ref[...] = kf
    vout_ref[...] = vf
    kb = kf.astype(BF16).astype(F32)
    vb = vf.astype(BF16).astype(F32)

    q_heads = []
    for hd in range(N_Q_HEADS):
        m = hd // 2
        qc = _rope(z[:, C_Q + m * LANES:C_Q + (m + 1) * LANES], cos, sin) * scale
        keep = lo if hd % 2 == 0 else ~lo
        qm = jnp.where(keep, qc, 0.0)
        if (hd % 2) != (hd // (N_Q_HEADS // N_KV_HEADS)):
            qm = pltpu.roll(qm, HEAD_DIM, 1)
        q_heads.append(qm.astype(BF16))

    s_new = [jnp.sum(q_heads[hd].astype(F32) * kb, axis=-1, keepdims=True) for hd in range(N_Q_HEADS)]

    rr = lax.broadcasted_iota(jnp.int32, (N_Q_HEADS * seq_chunk, seq_chunk * WINDOW), 0)
    cc = lax.broadcasted_iota(jnp.int32, (N_Q_HEADS * seq_chunk, seq_chunk * WINDOW), 1)
    same_seq = (rr % seq_chunk) == (cc // WINDOW)
    kv_lo = lax.broadcasted_iota(jnp.int32, (seq_chunk, LANES), 1) < HEAD_DIM

    for c in range(n_seq // seq_chunk):
        sr = slice(c * seq_chunk, (c + 1) * seq_chunk)
        kc = ck_ref[sr].reshape(seq_chunk * WINDOW, KV_WIDTH).astype(BF16)
        vc = cv_ref[sr].reshape(seq_chunk * WINDOW, KV_WIDTH).astype(BF16)
        qs = jnp.concatenate([q_heads[hd][sr] for hd in range(N_Q_HEADS)], axis=0)
        s = jnp.where(same_seq, _dot_nt(qs, kc), NEG_INF)
        sn = jnp.concatenate([s_new[hd][sr] for hd in range(N_Q_HEADS)], axis=0)
        sink = jnp.concatenate([jnp.full((seq_chunk, 1), sinks_ref[hd], F32) for hd in range(N_Q_HEADS)], axis=0)
        m = jnp.maximum(jnp.maximum(jnp.max(s, axis=-1, keepdims=True), sn), sink)
        e = jnp.exp(s - m)
        en = jnp.exp(sn - m)
        inv = 1.0 / (jnp.sum(e, axis=-1, keepdims=True) + en + jnp.exp(sink - m))
        o = _dot((e * inv).astype(BF16), vc)
        pn = (en * inv).astype(BF16).astype(F32)
        for mcol in range(N_Q_HEADS // 2):
            halves = []
            for hd in (2 * mcol, 2 * mcol + 1):
                oh = o[hd * seq_chunk:(hd + 1) * seq_chunk] + pn[hd * seq_chunk:(hd + 1) * seq_chunk] * vb[sr]
                if (hd % 2) != (hd // (N_Q_HEADS // N_KV_HEADS)):
                    oh = pltpu.roll(oh, HEAD_DIM, 1)
                halves.append(oh)
            att = jnp.where(kv_lo, halves[0], halves[1])
            mix_ref[sr, mcol * LANES:(mcol + 1) * LANES] = att.astype(BF16)

    u = _gelu(z[:, C_U:C_U + GMLP_WIDTH])
    vn = _layernorm(_gelu(z[:, C_VG:C_VG + GMLP_WIDTH]), lng_ref[...], lnb_ref[...])
    vnout_ref[...] = vn
    sp = ws0_ref[...].astype(BF16).astype(F32) * vn.astype(BF16).astype(F32) + bs0_ref[...]
    mix_ref[:, ATTN_WIDTH:] = (u * sp).astype(BF16)

    xmid = x + _dot(mix_ref[...], wout_ref[...])
    xmid_ref[...] = xmid
    h2b = _rmsnorm(xmid, g2_ref[...]).astype(BF16)
    h2_ref[...] = h2b
    meta_ref[...] = _route(h2b, wrt_ref, brt_ref)


def _dense_moe_kernel(t_ref, xmid_ref, comb_ref, wg_ref, wu_ref, wd_ref, gf_ref, y_ref, acc_ref):
    e = pl.program_id(1)

    @pl.when(e == 0)
    def _():
        acc_ref[...] = xmid_ref[...]

    tb = t_ref[...]
    gate = _dot(tb, wg_ref[...].astype(BF16))
    up = _dot(tb, wu_ref[...].astype(BF16))
    hid = (gate * (1.0 / (1.0 + jnp.exp(-gate))) * up).astype(BF16)
    out = _dot(hid, wd_ref[...].astype(BF16))
    lane = lax.broadcasted_iota(jnp.int32, comb_ref.shape, 1)
    c_e = jnp.sum(jnp.where(lane == e, comb_ref[...], 0.0), axis=-1, keepdims=True)
    acc_ref[...] += c_e * out

    @pl.when(e == pl.num_programs(1) - 1)
    def _():
        y_ref[...] = _rmsnorm(acc_ref[...], gf_ref[...])


def _grouped_ffn_kernel(texp_ref, code_ref, ntiles_ref, xl_hbm, wg_ref, wu_ref, wd_ref, yl_hbm,
                        xbuf, ybuf, zbuf, wg_b, wu_b, wd_b, gsem, ssem, tsem):
    j = pl.program_id(0)
    n_tiles = ntiles_ref[0]
    spare_tile = yl_hbm.shape[0] - 1

    def slot_addr(tile, s, pad_tile, pad_row):
        code = code_ref[tile * FFN_SLOTS + s]
        valid = code >= 0
        tok = jnp.where(valid, code >> CODE_SHIFT, pad_tile)
        row = jnp.where(valid, (code & ((1 << CODE_SHIFT) - 1)) * ROW_GRANULE, pad_row)
        return tok, pl.multiple_of(row, ROW_GRANULE)

    def gather_copy(tile, s, b):
        tok, row = slot_addr(tile, s, 0, s * ROW_GRANULE)
        return pltpu.make_async_copy(xl_hbm.at[tok, pl.ds(row, ROW_GRANULE)],
                                     xbuf.at[b, pl.ds(s * ROW_GRANULE, ROW_GRANULE)], gsem.at[b])

    def scatter_copy(tile, s, b):
        tok, row = slot_addr(tile, s, spare_tile, (b * FFN_SLOTS + s) * ROW_GRANULE)
        return pltpu.make_async_copy(ybuf.at[b, pl.ds(s * ROW_GRANULE, ROW_GRANULE)],
                                     yl_hbm.at[tok, pl.ds(row, ROW_GRANULE)], ssem.at[b])

    def spare_init_copy(c):
        return pltpu.make_async_copy(zbuf, yl_hbm.at[spare_tile, pl.ds(c * FFN_ROWS, FFN_ROWS)], tsem.at[0])

    @pl.when(j < n_tiles)
    def _():
        b = j % 2

        @pl.when(j == 0)
        def _():
            zbuf[...] = jnp.zeros_like(zbuf)
            for c in range(SORT_ROWS // FFN_ROWS):
                spare_init_copy(c).start()
            for s in range(FFN_SLOTS):
                gather_copy(0, s, 0).start()
            for c in range(SORT_ROWS // FFN_ROWS):
                spare_init_copy(c).wait()

        @pl.when(j + 1 < n_tiles)
        def _():
            for s in range(FFN_SLOTS):
                gather_copy(j + 1, s, 1 - b).start()

        for s in range(FFN_SLOTS):
            gather_copy(j, s, b).wait()

        @pl.when(j >= 2)
        def _():
            for s in range(FFN_SLOTS):
                scatter_copy(j - 2, s, b).wait()

        @pl.when(jnp.logical_or(j == 0, texp_ref[j] != texp_ref[jnp.maximum(j - 1, 0)]))
        def _():
            wg_b[...] = wg_ref[...].astype(BF16)
            wu_b[...] = wu_ref[...].astype(BF16)
            wd_b[...] = wd_ref[...].astype(BF16)

        x = xbuf[b]
        gate = _dot(x, wg_b[...])
        up = _dot(x, wu_b[...])
        hid = (gate * (1.0 / (1.0 + jnp.exp(-gate))) * up).astype(BF16)
        ybuf[b] = _dot(hid, wd_b[...]).astype(BF16)
        for s in range(FFN_SLOTS):
            scatter_copy(j, s, b).start()

        @pl.when(j == n_tiles - 1)
        def _():
            for s in range(FFN_SLOTS):
                scatter_copy(j, s, b).wait()

            @pl.when(j >= 1)
            def _():
                for s in range(FFN_SLOTS):
                    scatter_copy(j - 1, s, 1 - b).wait()


def _combine_kernel(xmid_ref, yl_ref, meta_ref, gf_ref, y_ref):
    n_tok = xmid_ref.shape[0]
    meta = meta_ref[...]
    meta_t = jnp.concatenate([meta, jnp.zeros((LANES - 8, n_tok), F32)], axis=0).T
    d1, d2, w1, w2 = meta_t[:, 0:1], meta_t[:, 1:2], meta_t[:, 2:3], meta_t[:, 3:4]
    acc = xmid_ref[...]
    for c in range(SORT_ROWS // SORT_CHUNK):
        r_iota = (lax.broadcasted_iota(jnp.int32, (n_tok, SORT_CHUNK), 1) + c * SORT_CHUNK).astype(F32)
        unsort = jnp.where(r_iota == d1, w1, jnp.where(r_iota == d2, w2, 0.0)).astype(BF16)
        acc = acc + _dot(unsort, yl_ref[c * SORT_CHUNK:(c + 1) * SORT_CHUNK, :])
    y_ref[...] = _rmsnorm(acc, gf_ref[...])


def _ffn_schedule(tab):
    n_tok_tiles = tab.shape[0]
    strips = (tab[:, :, 0] * (1.0 / ROW_GRANULE)).astype(jnp.int32)
    starts = (tab[:, :, 1] * (1.0 / ROW_GRANULE)).astype(jnp.int32)
    used = jnp.sum(strips, axis=1)
    cnt = jnp.concatenate([strips.T, (GRANULES_PER_TILE - used)[None, :]], axis=0)
    row0 = jnp.concatenate([starts.T, used[None, :]], axis=0)
    cs = jnp.cumsum(cnt, axis=1) - cnt
    n_str = jnp.sum(cnt, axis=1)
    np_str = (n_str + FFN_SLOTS - 1) // FFN_SLOTS * FFN_SLOTS
    ends = jnp.cumsum(np_str)
    base = ends - np_str
    n_steps_max = _ffn_steps_max(n_tok_tiles)
    step0 = jnp.arange(n_steps_max, dtype=jnp.int32) * FFN_SLOTS
    stream = jnp.minimum(jnp.sum(ends[None, :] <= step0[:, None], axis=1), N_STREAMS - 1)
    pick = stream[:, None] == jnp.arange(N_STREAMS, dtype=jnp.int32)[None, :]
    sel = lambda x: jnp.sum(jnp.where(pick[:, :, None], x[None], 0), axis=1)
    cs_j, cnt_j, row0_j = sel(cs), sel(cnt), sel(row0)
    base_j = jnp.sum(jnp.where(pick, base[None, :], 0), axis=1)
    q = step0[:, None] + jnp.arange(FFN_SLOTS, dtype=jnp.int32)[None, :] - base_j[:, None]
    reached = cs_j[:, None, :] <= q[:, :, None]
    last = lambda x: jnp.sum(jnp.where(reached, jnp.diff(x, axis=1, prepend=0)[:, None, :], 0), axis=2)
    tile_idx = jnp.sum(reached, axis=2).astype(jnp.int32) - 1
    g = q - last(cs_j)
    code = jnp.where(g < last(cnt_j), (tile_idx << CODE_SHIFT) | (last(row0_j) + g), -1).astype(jnp.int32)
    texp = jnp.minimum(stream, N_EXPERTS - 1).astype(jnp.int32)
    n_steps = (ends[-1] // FFN_SLOTS).astype(jnp.int32).reshape(1)
    return texp, code.reshape(-1), n_steps


N_STREAMS = N_EXPERTS + 1


def _ffn_steps_max(n_tok_tiles):
    return -(-(n_tok_tiles * GRANULES_PER_TILE + N_STREAMS * (FFN_SLOTS - 1)) // FFN_SLOTS)


def _vmem_limit(n_bytes):
    return int(min(n_bytes, V7X_VMEM_BYTES - 4 * 1024 * 1024))


def _rope_tables(pos):
    inv_freq = ROPE_THETA ** (-jnp.arange(HALF, dtype=F32) * 2.0 / HEAD_DIM)
    ang = pos.astype(F32)[:, None] * inv_freq[None, :]
    cos, sin = jnp.cos(ang), jnp.sin(ang)
    reps = LANES // HEAD_DIM
    return jnp.tile(jnp.concatenate([cos, cos], axis=1), (1, reps)), jnp.tile(jnp.concatenate([-sin, sin], axis=1), (1, reps))


def _combine_matrix(meta):
    m = jnp.swapaxes(meta, 1, 2).reshape(-1, 8)
    ids = m[:, 0:2].astype(jnp.int32)
    return jnp.sum(jax.nn.one_hot(ids, N_EXPERTS, dtype=F32) * m[:, 2:4, None], axis=1)


def _dense_moe(t, xmid, comb, wg, wu, wd, gf, tile):
    n = t.shape[0]
    comb = jnp.pad(comb, ((0, 0), (0, LANES - N_EXPERTS)))
    return pl.pallas_call(
        _dense_moe_kernel,
        grid=(n // tile, N_EXPERTS),
        in_specs=[
            pl.BlockSpec((tile, D_MODEL), lambda i, e: (i, 0)),
            pl.BlockSpec((tile, D_MODEL), lambda i, e: (i, 0)),
            pl.BlockSpec((tile, LANES), lambda i, e: (i, 0)),
            pl.BlockSpec((None, D_MODEL, D_EXPERT), lambda i, e: (e, 0, 0)),
            pl.BlockSpec((None, D_MODEL, D_EXPERT), lambda i, e: (e, 0, 0)),
            pl.BlockSpec((None, D_EXPERT, D_MODEL), lambda i, e: (e, 0, 0)),
            pl.BlockSpec((1, D_MODEL), lambda i, e: (0, 0)),
        ],
        out_specs=pl.BlockSpec((tile, D_MODEL), lambda i, e: (i, 0)),
        out_shape=jax.ShapeDtypeStruct((n, D_MODEL), F32),
        scratch_shapes=[pltpu.VMEM((tile, D_MODEL), F32)],
        compiler_params=pltpu.CompilerParams(
            dimension_semantics=("arbitrary", "arbitrary"),
            vmem_limit_bytes=_vmem_limit(48 * 1024 * 1024)),
        name="dense_moe",
    )(t, xmid, comb, wg, wu, wd, gf)


def kernel(x_prompt, x_sample, cache_swa_k, cache_swa_v, norm_mix_g, w_in, attn_sinks, gmlp_ln_g, gmlp_ln_b,
           gmlp_w_s, gmlp_b_s, w_out, norm_ffn_g, router_group_w, router_group_b, router_expert_w,
           router_expert_b, expert_w_gate, expert_w_up, expert_w_down, final_norm_g):
    assert norm_mix_g.shape[0] == 1, "single-layer trunk"
    batch, seq, _ = x_prompt.shape
    dec_batch = x_sample.shape[0]
    assert x_sample.shape[1] == 1 and seq % TOK_TILE == 0

    win_ext = w_in[0].astype(BF16)
    wout = w_out[0].astype(BF16)
    g1 = norm_mix_g[0][None, :]
    g2 = norm_ffn_g[0][None, :]
    gf = final_norm_g[None, :]
    lng = gmlp_ln_g[0][None, :]
    lnb = gmlp_ln_b[0][None, :]
    sinks = attn_sinks[0]
    ws = gmlp_w_s[0]
    group_dim = GMLP_WIDTH // GMLP_GROUPS
    bsf = jnp.repeat(gmlp_b_s[0].T, group_dim, axis=1)
    ws0 = jnp.repeat(ws[:, 0, 0], group_dim)[None, :]
    bs0 = jnp.repeat(gmlp_b_s[0][:, 0], group_dim)[None, :]
    wrt = jnp.zeros((ROUTER_ROWS, D_MODEL), F32)
    wrt = wrt.at[0:N_GROUPS].set(router_group_w[0].T)
    wrt = wrt.at[8:8 + N_EXPERTS].set(router_expert_w[0].reshape(D_MODEL, N_EXPERTS).T).astype(BF16)
    brt = jnp.full((ROUTER_ROWS, 1), NEG_INF, F32)
    brt = brt.at[0:N_GROUPS, 0].set(router_group_b[0])
    brt = brt.at[8:8 + N_EXPERTS, 0].set(router_expert_b[0].reshape(N_EXPERTS))
    wg, wu, wd = expert_w_gate[0], expert_w_up[0], expert_w_down[0]
    cos_p, sin_p = _rope_tables(jnp.arange(seq, dtype=jnp.int32))
    cos_s, sin_s = _rope_tables(PAST_LEN + jnp.arange(1, dtype=jnp.int32))

    full = lambda shape: pl.BlockSpec(shape, lambda *_: (0,) * len(shape))
    smem = pl.BlockSpec(memory_space=pltpu.SMEM)
    n_tiles = seq // TOK_TILE

    upper = jnp.triu(jnp.ones((TOK_TILE, TOK_TILE), BF16), k=1)
    lpad = (jnp.arange(LANES)[None, :] < jnp.arange(N_EXPERTS)[:, None]).astype(BF16)
    n_tok_tiles = batch * n_tiles

    xmid_p, xl_p, meta_p, tab_p, k_p, v_p = pl.pallas_call(
        _prompt_mixer_kernel,
        grid=(batch, n_tiles),
        in_specs=[
            smem,
            pl.BlockSpec((None, TOK_TILE, D_MODEL), lambda b, t: (b, t, 0)),
            full((1, D_MODEL)),
            full((D_MODEL, IN_WIDTH)),
            pl.BlockSpec((TOK_TILE, LANES), lambda b, t: (t, 0)),
            pl.BlockSpec((TOK_TILE, LANES), lambda b, t: (t, 0)),
            full((1, GMLP_WIDTH)),
            full((1, GMLP_WIDTH)),
            full((GMLP_GROUPS, CHUNK, CHUNK)),
            full((CHUNK, GMLP_WIDTH)),
            full((D_MODEL, D_MODEL)),
            full((1, D_MODEL)),
            full((ROUTER_ROWS, D_MODEL)),
            full((ROUTER_ROWS, 1)),
            full((TOK_TILE, TOK_TILE)),
            full((N_EXPERTS, LANES)),
        ],
        out_specs=[
            pl.BlockSpec((None, TOK_TILE, D_MODEL), lambda b, t: (b, t, 0)),
            pl.BlockSpec((None, SORT_ROWS, D_MODEL), lambda b, t: (b * n_tiles + t, 0, 0)),
            pl.BlockSpec((None, 8, TOK_TILE), lambda b, t: (b * n_tiles + t, 0, 0)),
            pl.BlockSpec((None, N_EXPERTS, LANES), lambda b, t: (b * n_tiles + t, 0, 0)),
            pl.BlockSpec((None, WINDOW, KV_WIDTH), lambda b, t: (b, 0, 0)),
            pl.BlockSpec((None, WINDOW, KV_WIDTH), lambda b, t: (b, 0, 0)),
        ],
        out_shape=[
            jax.ShapeDtypeStruct((batch, seq, D_MODEL), F32),
            jax.ShapeDtypeStruct((n_tok_tiles, SORT_ROWS, D_MODEL), BF16),
            jax.ShapeDtypeStruct((n_tok_tiles, 8, TOK_TILE), F32),
            jax.ShapeDtypeStruct((n_tok_tiles, N_EXPERTS, LANES), F32),
            jax.ShapeDtypeStruct((batch, WINDOW, KV_WIDTH), F32),
            jax.ShapeDtypeStruct((batch, WINDOW, KV_WIDTH), F32),
        ],
        scratch_shapes=[
            pltpu.VMEM((TOK_TILE, IN_WIDTH), F32),
            pltpu.VMEM((TOK_TILE, D_MODEL), BF16),
        ] + [pltpu.VMEM((WINDOW + TOK_TILE, KV_WIDTH), BF16)] * 4 + [
            pltpu.VMEM((len(_NAT_HEADS), TOK_TILE, LANES), BF16),
            pltpu.VMEM((len(_SWP_HEADS), TOK_TILE, LANES), BF16),
            pltpu.VMEM((2 * BLOCKS_PER_TILE, len(_NAT_HEADS) * WINDOW, 2 * WINDOW), F32),
            pltpu.VMEM((2 * BLOCKS_PER_TILE, len(_NAT_HEADS) * WINDOW, 2 * WINDOW), BF16),
            pltpu.VMEM((TOK_TILE, GMLP_WIDTH), BF16),
        ],
        compiler_params=pltpu.CompilerParams(
            dimension_semantics=("arbitrary", "arbitrary"),
            vmem_limit_bytes=_vmem_limit(56 * 1024 * 1024)),
        name="prompt_mixer",
    )(sinks, x_prompt, g1, win_ext, cos_p, sin_p, lng, lnb, ws, bsf, wout, g2, wrt, brt, upper, lpad)

    xs = x_sample.reshape(dec_batch, D_MODEL)
    ck = cache_swa_k[0].reshape(dec_batch, WINDOW, KV_WIDTH)
    cv = cache_swa_v[0].reshape(dec_batch, WINDOW, KV_WIDTH)
    xmid_s, h2_s, meta_s, k_s, v_s, vn_s = pl.pallas_call(
        _sample_mixer_kernel,
        in_specs=[smem] + [pl.BlockSpec(memory_space=pltpu.VMEM)] * 15,
        out_shape=[
            jax.ShapeDtypeStruct((dec_batch, D_MODEL), F32),
            jax.ShapeDtypeStruct((dec_batch, D_MODEL), BF16),
            jax.ShapeDtypeStruct((8, dec_batch), F32),
            jax.ShapeDtypeStruct((dec_batch, KV_WIDTH), F32),
            jax.ShapeDtypeStruct((dec_batch, KV_WIDTH), F32),
            jax.ShapeDtypeStruct((dec_batch, GMLP_WIDTH), F32),
        ],
        scratch_shapes=[pltpu.VMEM((dec_batch, D_MODEL), BF16)],
        compiler_params=pltpu.CompilerParams(vmem_limit_bytes=_vmem_limit(56 * 1024 * 1024)),
        name="sample_mixer",
    )(sinks, xs, ck, cv, g1, win_ext, cos_s, sin_s, lng, lnb, ws0, bs0, wout, g2, wrt, brt)

    texp, code, n_steps = _ffn_schedule(tab_p)
    yl_p = pl.pallas_call(
        _grouped_ffn_kernel,
        grid_spec=pltpu.PrefetchScalarGridSpec(
            num_scalar_prefetch=3,
            grid=(_ffn_steps_max(n_tok_tiles),),
            in_specs=[
                pl.BlockSpec(memory_space=pl.ANY),
                pl.BlockSpec((None, D_MODEL, D_EXPERT), lambda j, te, cd, ns: (te[j], 0, 0)),
                pl.BlockSpec((None, D_MODEL, D_EXPERT), lambda j, te, cd, ns: (te[j], 0, 0)),
                pl.BlockSpec((None, D_EXPERT, D_MODEL), lambda j, te, cd, ns: (te[j], 0, 0)),
            ],
            out_specs=pl.BlockSpec(memory_space=pl.ANY),
            scratch_shapes=[
                pltpu.VMEM((2, FFN_ROWS, D_MODEL), BF16),
                pltpu.VMEM((2, FFN_ROWS, D_MODEL), BF16),
                pltpu.VMEM((FFN_ROWS, D_MODEL), BF16),
                pltpu.VMEM((D_MODEL, D_EXPERT), BF16),
                pltpu.VMEM((D_MODEL, D_EXPERT), BF16),
                pltpu.VMEM((D_EXPERT, D_MODEL), BF16),
                pltpu.SemaphoreType.DMA((2,)),
                pltpu.SemaphoreType.DMA((2,)),
                pltpu.SemaphoreType.DMA((1,)),
            ],
        ),
        out_shape=jax.ShapeDtypeStruct((n_tok_tiles + 1, SORT_ROWS, D_MODEL), BF16),
        compiler_params=pltpu.CompilerParams(
            dimension_semantics=("arbitrary",),
            vmem_limit_bytes=_vmem_limit(32 * 1024 * 1024)),
        name="grouped_ffn",
    )(texp, code, n_steps, xl_p, wg, wu, wd)

    y_p = pl.pallas_call(
        _combine_kernel,
        grid=(n_tok_tiles,),
        in_specs=[
            pl.BlockSpec((TOK_TILE, D_MODEL), lambda i: (i, 0)),
            pl.BlockSpec((None, SORT_ROWS, D_MODEL), lambda i: (i, 0, 0)),
            pl.BlockSpec((None, 8, TOK_TILE), lambda i: (i, 0, 0)),
            pl.BlockSpec((1, D_MODEL), lambda i: (0, 0)),
        ],
        out_specs=pl.BlockSpec((TOK_TILE, D_MODEL), lambda i: (i, 0)),
        out_shape=jax.ShapeDtypeStruct((batch * seq, D_MODEL), F32),
        compiler_params=pltpu.CompilerParams(
            dimension_semantics=("arbitrary",),
            vmem_limit_bytes=_vmem_limit(40 * 1024 * 1024)),
        name="moe_combine",
    )(xmid_p.reshape(batch * seq, D_MODEL), yl_p, meta_p, gf)

    comb_s = _combine_matrix(meta_s[None])
    y_s = _dense_moe(h2_s, xmid_s, comb_s, wg, wu, wd, gf, dec_batch)

    return (y_p.reshape(batch, seq, D_MODEL),
            y_s.reshape(dec_batch, 1, D_MODEL),
            k_p.reshape(1, batch, WINDOW, N_KV_HEADS, HEAD_DIM),
            v_p.reshape(1, batch, WINDOW, N_KV_HEADS, HEAD_DIM),
            k_s.reshape(1, dec_batch, 1, N_KV_HEADS, HEAD_DIM),
            v_s.reshape(1, dec_batch, 1, N_KV_HEADS, HEAD_DIM),
            vn_s.reshape(1, dec_batch, 1, GMLP_WIDTH))
```

```python
import jax
import jax.numpy as jnp
import numpy as np
from jax import lax
from jax.experimental import pallas as pl
from jax.experimental.pallas import tpu as pltpu

F32 = jnp.float32
BF16 = jnp.bfloat16

D_MODEL = 1024
HEAD_DIM = 64
HALF = HEAD_DIM // 2
N_Q_HEADS = 8
N_KV_HEADS = 2
ATTN_WIDTH = N_Q_HEADS * HEAD_DIM
KV_WIDTH = N_KV_HEADS * HEAD_DIM
WINDOW = 128
ROPE_THETA = 10000.0
GMLP_WIDTH = D_MODEL - ATTN_WIDTH
GMLP_GROUPS = 8
CHUNK = 128
N_GROUPS = 4
EXPERTS_PER_GROUP = 8
N_EXPERTS = N_GROUPS * EXPERTS_PER_GROUP
D_EXPERT = 256
EPS = 1e-6
NEG_INF = -1e30
PAST_LEN = 16384

LANES = 128
V7X_VMEM_BYTES = 64 * 1024 * 1024

C_Q = 0
C_K = C_Q + ATTN_WIDTH
C_V = C_K + KV_WIDTH
C_U = C_V + KV_WIDTH
C_VG = C_U + GMLP_WIDTH
IN_WIDTH = C_VG + GMLP_WIDTH

ROUTER_ROWS = 48
TOK_TILE = 512
BLOCKS_PER_TILE = TOK_TILE // WINDOW
ROW_GRANULE = 16
SORT_CHUNK = 512
SORT_ROWS = -(-(2 * TOK_TILE + N_EXPERTS * (ROW_GRANULE - 1)) // SORT_CHUNK) * SORT_CHUNK
GRANULES_PER_TILE = SORT_ROWS // ROW_GRANULE
FFN_ROWS = 512
FFN_SLOTS = FFN_ROWS // ROW_GRANULE


def _dot(a, b):
    return jnp.dot(a, b, preferred_element_type=F32)


def _dot_nt(a, b):
    return lax.dot_general(a, b, (((1,), (1,)), ((), ())), preferred_element_type=F32)


def _gelu(x):
    return 0.5 * x * (1.0 + lax.erf(x * np.float32(np.sqrt(0.5))))


def _rmsnorm(x, g):
    return x * lax.rsqrt(jnp.mean(x * x, axis=-1, keepdims=True) + EPS) * g


def _layernorm(x, g, b):
    mu = jnp.mean(x, axis=-1, keepdims=True)
    xc = x - mu
    return xc * lax.rsqrt(jnp.mean(xc * xc, axis=-1, keepdims=True) + EPS) * g + b


def _first_argmax_rows(x, row_iota, n_rows):
    m = jnp.max(x, axis=0, keepdims=True)
    idx = jnp.min(jnp.where(x == m, row_iota, n_rows), axis=0, keepdims=True)
    return m, idx


def _route(h2b, wrt_ref, brt_ref):
    n_tok = h2b.shape[0]
    lt = _dot_nt(wrt_ref[...], h2b) + brt_ref[...]
    row8 = lax.broadcasted_iota(jnp.int32, (8, n_tok), 0).astype(F32)
    glog = lt[0:8]
    gmax, gidx = _first_argmax_rows(glog, row8, 8)
    g_w = 1.0 / jnp.sum(jnp.exp(glog - gmax), axis=0, keepdims=True)
    esel = lt[8:16]
    for g in range(1, N_GROUPS):
        esel = jnp.where(gidx == g, lt[8 + 8 * g:16 + 8 * g], esel)
    m1, i1 = _first_argmax_rows(esel, row8, 8)
    esel2 = jnp.where(row8 == i1, -jnp.inf, esel)
    m2, i2 = _first_argmax_rows(esel2, row8, 8)
    r = jnp.exp(m2 - m1)
    w1 = 1.0 / (1.0 + r)
    w2 = r / (1.0 + r)
    e1 = gidx * EXPERTS_PER_GROUP + i1
    e2 = gidx * EXPERTS_PER_GROUP + i2
    slab = jnp.where(row8 == 0, e1, 0.0)
    for r_idx, val in ((1, e2), (2, w1 * g_w), (3, w2 * g_w)):
        slab = jnp.where(row8 == r_idx, val, slab)
    return slab


def _softmax_with_sink(s, sink):
    m = jnp.maximum(jnp.max(s, axis=-1, keepdims=True), sink)
    e = jnp.exp(s - m)
    den = jnp.sum(e, axis=-1, keepdims=True) + jnp.exp(sink - m)
    return e * (1.0 / den)


def _rope(x, cos, sin_signed):
    first_half = (lax.broadcasted_iota(jnp.int32, x.shape, 1) & HALF) == 0
    partner = jnp.where(first_half, pltpu.roll(x, LANES - HALF, 1), pltpu.roll(x, HALF, 1))
    return x * cos + partner * sin_signed


_NAT_HEADS = (0, 2, 5, 7)
_SWP_HEADS = (1, 3, 4, 6)


def _prompt_mixer_kernel(sinks_ref, x_ref, g1_ref, win_ref, cos_ref, sin_ref, lng_ref, lnb_ref, ws_ref, bsf_ref,
                         wout_ref, g2_ref, wrt_ref, brt_ref, upper_ref, lpad_ref,
                         xmid_ref, xl_ref, meta_ref, tab_ref, kout_ref, vout_ref,
                         z_ref, mix_ref, k_n, k_s, v_n, v_s, q_nat, q_swp, s_ref, p_ref, vn_ref):
    t = pl.program_id(1)
    kv_bufs = (k_n, k_s, v_n, v_s)

    @pl.when(t == 0)
    def _():
        for ref in kv_bufs:
            ref[0:WINDOW, :] = jnp.zeros((WINDOW, KV_WIDTH), BF16)

    x = x_ref[...]
    h = _rmsnorm(x, g1_ref[...]).astype(BF16)
    z_ref[...] = _dot(h, win_ref[...])

    cos = cos_ref[...]
    sin = sin_ref[...]
    lane = lax.broadcasted_iota(jnp.int32, (WINDOW, LANES), 1)
    lo = lane < HEAD_DIM
    row = lax.broadcasted_iota(jnp.int32, (WINDOW, WINDOW), 0)
    col = lax.broadcasted_iota(jnp.int32, (WINDOW, WINDOW), 1)
    mask_cur = col <= row
    mask_prev_band = col >= row
    mask_prev_first = jnp.logical_and(mask_prev_band, (jnp.zeros_like(row) + t) > 0)
    mask_band = jnp.concatenate([mask_prev_band, mask_cur], axis=1)
    mask_first = jnp.concatenate([mask_prev_first, mask_cur], axis=1)

    cq = cos * np.float32(HEAD_DIM ** -0.5)
    sq = sin * np.float32(HEAD_DIM ** -0.5)
    kf = _rope(z_ref[:, C_K:C_K + KV_WIDTH], cos, sin)
    vf = z_ref[:, C_V:C_V + KV_WIDTH]
    k_n[WINDOW:, :] = kf.astype(BF16)
    k_s[WINDOW:, :] = pltpu.roll(kf, HEAD_DIM, 1).astype(BF16)
    v_n[WINDOW:, :] = vf.astype(BF16)
    v_s[WINDOW:, :] = pltpu.roll(vf, HEAD_DIM, 1).astype(BF16)

    @pl.when(t == pl.num_programs(1) - 1)
    def _():
        kout_ref[...] = kf[TOK_TILE - WINDOW:]
        vout_ref[...] = vf[TOK_TILE - WINDOW:]

    lo_t = lax.broadcasted_iota(jnp.int32, (TOK_TILE, LANES), 1) < HEAD_DIM
    for m in range(N_Q_HEADS // 2):
        qc = _rope(z_ref[:, C_Q + m * LANES:C_Q + (m + 1) * LANES], cq, sq)
        for hd, qh in ((2 * m, jnp.where(lo_t, qc, 0.0)), (2 * m + 1, jnp.where(lo_t, 0.0, qc))):
            if hd in _NAT_HEADS:
                q_nat[_NAT_HEADS.index(hd)] = qh.astype(BF16)
            else:
                q_swp[_SWP_HEADS.index(hd)] = qh.astype(BF16)

    stacks = ((q_nat, _NAT_HEADS, k_n, v_n), (q_swp, _SWP_HEADS, k_s, v_s))
    for j in range(BLOCKS_PER_TILE):
        rows = slice(j * WINDOW, (j + 1) * WINDOW)
        keys = slice(j * WINDOW, (j + 2) * WINDOW)
        for si, (q_ref, _, k_buf, _) in enumerate(stacks):
            q_stack = jnp.concatenate([q_ref[i, rows, :] for i in range(len(_NAT_HEADS))], axis=0)
            s_ref[2 * j + si] = _dot_nt(q_stack, k_buf[keys, :])
    for j in range(BLOCKS_PER_TILE):
        mask = mask_first if j == 0 else mask_band
        for si, (_, heads, _, _) in enumerate(stacks):
            for i, hd in enumerate(heads):
                pr = slice(i * WINDOW, (i + 1) * WINDOW)
                p_ref[2 * j + si, pr, :] = _softmax_with_sink(
                    jnp.where(mask, s_ref[2 * j + si, pr, :], NEG_INF), sinks_ref[hd]).astype(BF16)
    for j in range(BLOCKS_PER_TILE):
        rows = slice(j * WINDOW, (j + 1) * WINDOW)
        keys = slice(j * WINDOW, (j + 2) * WINDOW)
        o_nat = _dot(p_ref[2 * j], v_n[keys, :])
        o_swp = _dot(p_ref[2 * j + 1], v_s[keys, :])
        for m in range(N_Q_HEADS // 2):
            pr = slice(m * WINDOW, (m + 1) * WINDOW)
            even_nat = (2 * m) in _NAT_HEADS
            att = jnp.where(lo, o_nat[pr], o_swp[pr]) if even_nat else jnp.where(lo, o_swp[pr], o_nat[pr])
            mix_ref[rows, m * LANES:(m + 1) * LANES] = att.astype(BF16)

    vn_ref[...] = _layernorm(_gelu(z_ref[:, C_VG:C_VG + GMLP_WIDTH]), lng_ref[...], lnb_ref[...]).astype(BF16)
    for m in range(GMLP_GROUPS // 2):
        cs = slice(m * LANES, (m + 1) * LANES)
        w0 = jnp.where(mask_cur, ws_ref[2 * m], 0.0).astype(BF16)
        w1 = jnp.where(mask_cur, ws_ref[2 * m + 1], 0.0).astype(BF16)
        wcat = jnp.concatenate([w0, w1], axis=1)
        for j in range(BLOCKS_PER_TILE):
            rows = slice(j * WINDOW, (j + 1) * WINDOW)
            vcol = vn_ref[rows, cs]
            rhs = jnp.concatenate([jnp.where(lo, vcol, jnp.zeros_like(vcol)),
                                   jnp.where(lo, jnp.zeros_like(vcol), vcol)], axis=0)
            sp = _dot(wcat, rhs) + bsf_ref[:, cs]
            u = _gelu(z_ref[rows, C_U + m * LANES:C_U + (m + 1) * LANES])
            mix_ref[rows, ATTN_WIDTH + m * LANES:ATTN_WIDTH + (m + 1) * LANES] = (u * sp).astype(BF16)

    for ref in kv_bufs:
        ref[0:WINDOW, :] = ref[TOK_TILE:TOK_TILE + WINDOW, :]

    xmid = x + _dot(mix_ref[...], wout_ref[...])
    xmid_ref[...] = xmid
    h2b = _rmsnorm(xmid, g2_ref[...]).astype(BF16)
    slab = _route(h2b, wrt_ref, brt_ref)
    _local_sort(slab, h2b, upper_ref, lpad_ref, xl_ref, meta_ref, tab_ref)


def _local_sort(slab, h2b, upper_ref, lpad_ref, xl_ref, meta_ref, tab_ref):
    n_tok = h2b.shape[0]
    e1, e2 = slab[0:1], slab[1:2]
    row32 = lax.broadcasted_iota(jnp.int32, (N_EXPERTS, n_tok), 0).astype(F32)
    sel1 = row32 == e1
    sel2 = row32 == e2
    onehot = jnp.where(sel1, 1.0, jnp.where(sel2, 1.0, 0.0))
    earlier = _dot(onehot.astype(BF16), upper_ref[...])
    cnt = jnp.sum(onehot, axis=1, keepdims=True)
    pc = jnp.floor((cnt + (ROW_GRANULE - 1)) * (1.0 / ROW_GRANULE)) * ROW_GRANULE
    pc_b = jnp.broadcast_to(pc, (N_EXPERTS, LANES))
    pc_pad = jnp.concatenate([pc_b, jnp.zeros((LANES - N_EXPERTS, LANES), F32)], axis=0).astype(BF16)
    start = _dot(lpad_ref[...], pc_pad)
    base = start[:, 0:1] + earlier
    d1 = jnp.sum(jnp.where(sel1, base, 0.0), axis=0, keepdims=True)
    d2 = jnp.sum(jnp.where(sel2, base, 0.0), axis=0, keepdims=True)
    for c in range(SORT_ROWS // SORT_CHUNK):
        r_iota = (lax.broadcasted_iota(jnp.int32, (SORT_CHUNK, n_tok), 0) + c * SORT_CHUNK).astype(F32)
        perm = jnp.where(r_iota == d1, 1.0, jnp.where(r_iota == d2, 1.0, 0.0)).astype(BF16)
        xl_ref[c * SORT_CHUNK:(c + 1) * SORT_CHUNK, :] = _dot(perm, h2b).astype(BF16)
    row8 = lax.broadcasted_iota(jnp.int32, (8, n_tok), 0)
    meta_ref[...] = jnp.where(row8 == 0, d1, jnp.where(row8 == 1, d2, jnp.where(row8 >= 4, 0.0, slab)))
    lane = lax.broadcasted_iota(jnp.int32, (N_EXPERTS, LANES), 1)
    tab_ref[...] = jnp.where(lane == 0, pc_b, jnp.where(lane == 1, start, 0.0))


def _sample_mixer_kernel(sinks_ref, x_ref, ck_ref, cv_ref, g1_ref, win_ref, cos_ref, sin_ref, lng_ref, lnb_ref,
                         ws0_ref, bs0_ref, wout_ref, g2_ref, wrt_ref, brt_ref,
                         xmid_ref, h2_ref, meta_ref, kout_ref, vout_ref, vnout_ref,
                         mix_ref):
    n_seq = x_ref.shape[0]
    seq_chunk = 16
    x = x_ref[...]
    h = _rmsnorm(x, g1_ref[...]).astype(BF16)
    z = _dot(h, win_ref[...])
    cos = cos_ref[...]
    sin = sin_ref[...]
    scale = np.float32(HEAD_DIM ** -0.5)
    lane = lax.broadcasted_iota(jnp.int32, (n_seq, LANES), 1)
    lo = lane < HEAD_DIM
    kf = _rope(z[:, C_K:C_K + KV_WIDTH], cos, sin)
    vf = z[:, C_V:C_V + KV_WIDTH]
    kout_ref[...] = kf
    vout_ref[...] = vf
    kb = kf.astype(BF16).astype(F32)
    vb = vf.astype(BF16).astype(F32)

    q_heads = []
    for hd in range(N_Q_HEADS):
        m = hd // 2
        qc = _rope(z[:, C_Q + m * LANES:C_Q + (m + 1) * LANES], cos, sin) * scale
        keep = lo if hd % 2 == 0 else ~lo
        qm = jnp.where(keep, qc, 0.0)
        if (hd % 2) != (hd // (N_Q_HEADS // N_KV_HEADS)):
            qm = pltpu.roll(qm, HEAD_DIM, 1)
        q_heads.append(qm.astype(BF16))

    s_new = [jnp.sum(q_heads[hd].astype(F32) * kb, axis=-1, keepdims=True) for hd in range(N_Q_HEADS)]

    rr = lax.broadcasted_iota(jnp.int32, (N_Q_HEADS * seq_chunk, seq_chunk * WINDOW), 0)
    cc = lax.broadcasted_iota(jnp.int32, (N_Q_HEADS * seq_chunk, seq_chunk * WINDOW), 1)
    same_seq = (rr % seq_chunk) == (cc // WINDOW)
    kv_lo = lax.broadcasted_iota(jnp.int32, (seq_chunk, LANES), 1) < HEAD_DIM

    for c in range(n_seq // seq_chunk):
        sr = slice(c * seq_chunk, (c + 1) * seq_chunk)
        kc = ck_ref[sr].reshape(seq_chunk * WINDOW, KV_WIDTH).astype(BF16)
        vc = cv_ref[sr].reshape(seq_chunk * WINDOW, KV_WIDTH).astype(BF16)
        qs = jnp.concatenate([q_heads[hd][sr] for hd in range(N_Q_HEADS)], axis=0)
        s = jnp.where(same_seq, _dot_nt(qs, kc), NEG_INF)
        sn = jnp.concatenate([s_new[hd][sr] for hd in range(N_Q_HEADS)], axis=0)
        sink = jnp.concatenate([jnp.full((seq_chunk, 1), sinks_ref[hd], F32) for hd in range(N_Q_HEADS)], axis=0)
        m = jnp.maximum(jnp.maximum(jnp.max(s, axis=-1, keepdims=True), sn), sink)
        e = jnp.exp(s - m)
        en = jnp.exp(sn - m)
        inv = 1.0 / (jnp.sum(e, axis=-1, keepdims=True) + en + jnp.exp(sink - m))
        o = _dot((e * inv).astype(BF16), vc)
        pn = (en * inv).astype(BF16).astype(F32)
        for mcol in range(N_Q_HEADS // 2):
            halves = []
            for hd in (2 * mcol, 2 * mcol + 1):
                oh = o[hd * seq_chunk:(hd + 1) * seq_chunk] + pn[hd * seq_chunk:(hd + 1) * seq_chunk] * vb[sr]
                if (hd % 2) != (hd // (N_Q_HEADS // N_KV_HEADS)):
                    oh = pltpu.roll(oh, HEAD_DIM, 1)
                halves.append(oh)
            att = jnp.where(kv_lo, halves[0], halves[1])
            mix_ref[sr, mcol * LANES:(mcol + 1) * LANES] = att.astype(BF16)

    u = _gelu(z[:, C_U:C_U + GMLP_WIDTH])
    vn = _layernorm(_gelu(z[:, C_VG:C_VG + GMLP_WIDTH]), lng_ref[...], lnb_ref[...])
    vnout_ref[...] = vn
    sp = ws0_ref[...].astype(BF16).astype(F32) * vn.astype(BF16).astype(F32) + bs0_ref[...]
    mix_ref[:, ATTN_WIDTH:] = (u * sp).astype(BF16)

    xmid = x + _dot(mix_ref[...], wout_ref[...])
    xmid_ref[...] = xmid
    h2b = _rmsnorm(xmid, g2_ref[...]).astype(BF16)
    h2_ref[...] = h2b
    meta_ref[...] = _route(h2b, wrt_ref, brt_ref)


def _dense_moe_kernel(t_ref, xmid_ref, comb_ref, wg_ref, wu_ref, wd_ref, gf_ref, y_ref, acc_ref):
    e = pl.program_id(1)

    @pl.when(e == 0)
    def _():
        acc_ref[...] = xmid_ref[...]

    tb = t_ref[...]
    gate = _dot(tb, wg_ref[...].astype(BF16))
    up = _dot(tb, wu_ref[...].astype(BF16))
    hid = (gate * (1.0 / (1.0 + jnp.exp(-gate))) * up).astype(BF16)
    out = _dot(hid, wd_ref[...].astype(BF16))
    lane = lax.broadcasted_iota(jnp.int32, comb_ref.shape, 1)
    c_e = jnp.sum(jnp.where(lane == e, comb_ref[...], 0.0), axis=-1, keepdims=True)
    acc_ref[...] += c_e * out

    @pl.when(e == pl.num_programs(1) - 1)
    def _():
        y_ref[...] = _rmsnorm(acc_ref[...], gf_ref[...])


def _grouped_ffn_kernel(texp_ref, gsrc_ref, sdst_ref, ntiles_ref, xl_hbm, wg_ref, wu_ref, wd_ref, yl_hbm,
                        xbuf, ybuf, zbuf, wg_b, wu_b, wd_b, gsem, ssem, tsem):
    j = pl.program_id(0)
    n_tiles = ntiles_ref[0]
    spare_g0 = yl_hbm.shape[0] - GRANULES_PER_TILE

    def gather_copy(tile, s, b):
        return pltpu.make_async_copy(xl_hbm.at[gsrc_ref[tile * FFN_SLOTS + s]], xbuf.at[b, s], gsem.at[b])

    def scatter_copy(tile, s, b):
        return pltpu.make_async_copy(ybuf.at[b, s], yl_hbm.at[sdst_ref[tile * FFN_SLOTS + s]], ssem.at[b])

    def spare_init_copy(c):
        return pltpu.make_async_copy(zbuf, yl_hbm.at[pl.ds(spare_g0 + c * FFN_SLOTS, FFN_SLOTS)], tsem.at[0])

    @pl.when(j < n_tiles)
    def _():
        b = j % 2
        nxt = jnp.minimum(j + 1, n_tiles - 1)

        @pl.when(j == 0)
        def _():
            zbuf[...] = jnp.zeros_like(zbuf)
            for c in range(SORT_ROWS // FFN_ROWS):
                spare_init_copy(c).start()
            for s in range(FFN_SLOTS):
                gather_copy(0, s, 0).start()
            for c in range(SORT_ROWS // FFN_ROWS):
                spare_init_copy(c).wait()

        for s in range(FFN_SLOTS):
            gather_copy(j, s, b).wait()

        @pl.when(j >= 2)
        def _():
            for s in range(FFN_SLOTS):
                scatter_copy(j - 2, s, b).wait()

        @pl.when(jnp.logical_or(j == 0, texp_ref[j] != texp_ref[jnp.maximum(j - 1, 0)]))
        def _():
            wg_b[...] = wg_ref[...].astype(BF16)
            wu_b[...] = wu_ref[...].astype(BF16)
            wd_b[...] = wd_ref[...].astype(BF16)

        x = xbuf[b].reshape(FFN_ROWS, D_MODEL)
        gate = _dot(x, wg_b[...])
        up = _dot(x, wu_b[...])
        for s in range(FFN_SLOTS):
            gather_copy(nxt, s, 1 - b).start()
        hid = (gate * (1.0 / (1.0 + jnp.exp(-gate))) * up).astype(BF16)
        ybuf[b] = _dot(hid, wd_b[...]).astype(BF16).reshape(FFN_SLOTS, ROW_GRANULE, D_MODEL)
        for s in range(FFN_SLOTS):
            scatter_copy(j, s, b).start()

        @pl.when(j == n_tiles - 1)
        def _():
            for s in range(FFN_SLOTS):
                gather_copy(nxt, s, 1 - b).wait()
            for s in range(FFN_SLOTS):
                scatter_copy(j, s, b).wait()

            @pl.when(j >= 1)
            def _():
                for s in range(FFN_SLOTS):
                    scatter_copy(j - 1, s, 1 - b).wait()


def _combine_kernel(xmid_ref, yl_ref, meta_ref, gf_ref, y_ref):
    n_tok = xmid_ref.shape[0]
    meta = meta_ref[...]
    meta_t = jnp.concatenate([meta, jnp.zeros((LANES - 8, n_tok), F32)], axis=0).T
    d1, d2, w1, w2 = meta_t[:, 0:1], meta_t[:, 1:2], meta_t[:, 2:3], meta_t[:, 3:4]
    acc = xmid_ref[...]
    for c in range(SORT_ROWS // SORT_CHUNK):
        r_iota = (lax.broadcasted_iota(jnp.int32, (n_tok, SORT_CHUNK), 1) + c * SORT_CHUNK).astype(F32)
        unsort = jnp.where(r_iota == d1, w1, jnp.where(r_iota == d2, w2, 0.0)).astype(BF16)
        acc = acc + _dot(unsort, yl_ref[c * SORT_CHUNK:(c + 1) * SORT_CHUNK, :])
    y_ref[...] = _rmsnorm(acc, gf_ref[...])


def _ffn_schedule(tab):
    n_tok_tiles = tab.shape[0]
    strips = (tab[:, :, 0] * (1.0 / ROW_GRANULE)).astype(jnp.int32)
    starts = (tab[:, :, 1] * (1.0 / ROW_GRANULE)).astype(jnp.int32)
    used = jnp.sum(strips, axis=1)
    cnt = jnp.concatenate([strips.T, (GRANULES_PER_TILE - used)[None, :]], axis=0)
    row0 = jnp.concatenate([starts.T, used[None, :]], axis=0)
    cs = jnp.cumsum(cnt, axis=1) - cnt
    n_str = jnp.sum(cnt, axis=1)
    np_str = (n_str + FFN_SLOTS - 1) // FFN_SLOTS * FFN_SLOTS
    ends = jnp.cumsum(np_str)
    base = ends - np_str
    n_steps_max = _ffn_steps_max(n_tok_tiles)
    step0 = jnp.arange(n_steps_max, dtype=jnp.int32) * FFN_SLOTS
    stream = jnp.minimum(jnp.sum(ends[None, :] <= step0[:, None], axis=1), N_STREAMS - 1)
    pick = stream[:, None] == jnp.arange(N_STREAMS, dtype=jnp.int32)[None, :]
    sel = lambda x: jnp.sum(jnp.where(pick[:, :, None], x[None], 0), axis=1)
    cs_j, cnt_j, row0_j = sel(cs), sel(cnt), sel(row0)
    base_j = jnp.sum(jnp.where(pick, base[None, :], 0), axis=1)
    q = step0[:, None] + jnp.arange(FFN_SLOTS, dtype=jnp.int32)[None, :] - base_j[:, None]
    reached = cs_j[:, None, :] <= q[:, :, None]
    last = lambda x: jnp.sum(jnp.where(reached, jnp.diff(x, axis=1, prepend=0)[:, None, :], 0), axis=2)
    tile_idx = jnp.sum(reached, axis=2).astype(jnp.int32) - 1
    g = q - last(cs_j)
    valid = g < last(cnt_j)
    granule = tile_idx * GRANULES_PER_TILE + last(row0_j) + g
    slot = jnp.arange(FFN_SLOTS, dtype=jnp.int32)[None, :]
    spare = n_tok_tiles * GRANULES_PER_TILE + (jnp.arange(n_steps_max, dtype=jnp.int32)[:, None] % 2) * FFN_SLOTS + slot
    gsrc = jnp.where(valid, granule, slot).astype(jnp.int32)
    sdst = jnp.where(valid, granule, spare).astype(jnp.int32)
    texp = jnp.minimum(stream, N_EXPERTS - 1).astype(jnp.int32)
    n_steps = (ends[-1] // FFN_SLOTS).astype(jnp.int32).reshape(1)
    return texp, gsrc.reshape(-1), sdst.reshape(-1), n_steps


N_STREAMS = N_EXPERTS + 1


def _ffn_steps_max(n_tok_tiles):
    return -(-(n_tok_tiles * GRANULES_PER_TILE + N_STREAMS * (FFN_SLOTS - 1)) // FFN_SLOTS)


def _vmem_limit(n_bytes):
    return int(min(n_bytes, V7X_VMEM_BYTES - 4 * 1024 * 1024))


def _rope_tables(pos):
    inv_freq = ROPE_THETA ** (-jnp.arange(HALF, dtype=F32) * 2.0 / HEAD_DIM)
    ang = pos.astype(F32)[:, None] * inv_freq[None, :]
    cos, sin = jnp.cos(ang), jnp.sin(ang)
    reps = LANES // HEAD_DIM
    return jnp.tile(jnp.concatenate([cos, cos], axis=1), (1, reps)), jnp.tile(jnp.concatenate([-sin, sin], axis=1), (1, reps))


def _combine_matrix(meta):
    m = jnp.swapaxes(meta, 1, 2).reshape(-1, 8)
    ids = m[:, 0:2].astype(jnp.int32)
    return jnp.sum(jax.nn.one_hot(ids, N_EXPERTS, dtype=F32) * m[:, 2:4, None], axis=1)


def _dense_moe(t, xmid, comb, wg, wu, wd, gf, tile):
    n = t.shape[0]
    comb = jnp.pad(comb, ((0, 0), (0, LANES - N_EXPERTS)))
    return pl.pallas_call(
        _dense_moe_kernel,
        grid=(n // tile, N_EXPERTS),
        in_specs=[
            pl.BlockSpec((tile, D_MODEL), lambda i, e: (i, 0)),
            pl.BlockSpec((tile, D_MODEL), lambda i, e: (i, 0)),
            pl.BlockSpec((tile, LANES), lambda i, e: (i, 0)),
            pl.BlockSpec((None, D_MODEL, D_EXPERT), lambda i, e: (e, 0, 0)),
            pl.BlockSpec((None, D_MODEL, D_EXPERT), lambda i, e: (e, 0, 0)),
            pl.BlockSpec((None, D_EXPERT, D_MODEL), lambda i, e: (e, 0, 0)),
            pl.BlockSpec((1, D_MODEL), lambda i, e: (0, 0)),
        ],
        out_specs=pl.BlockSpec((tile, D_MODEL), lambda i, e: (i, 0)),
        out_shape=jax.ShapeDtypeStruct((n, D_MODEL), F32),
        scratch_shapes=[pltpu.VMEM((tile, D_MODEL), F32)],
        compiler_params=pltpu.CompilerParams(
            dimension_semantics=("arbitrary", "arbitrary"),
            vmem_limit_bytes=_vmem_limit(48 * 1024 * 1024)),
        name="dense_moe",
    )(t, xmid, comb, wg, wu, wd, gf)


def kernel(x_prompt, x_sample, cache_swa_k, cache_swa_v, norm_mix_g, w_in, attn_sinks, gmlp_ln_g, gmlp_ln_b,
           gmlp_w_s, gmlp_b_s, w_out, norm_ffn_g, router_group_w, router_group_b, router_expert_w,
           router_expert_b, expert_w_gate, expert_w_up, expert_w_down, final_norm_g):
    assert norm_mix_g.shape[0] == 1, "single-layer trunk"
    batch, seq, _ = x_prompt.shape
    dec_batch = x_sample.shape[0]
    assert x_sample.shape[1] == 1 and seq % TOK_TILE == 0

    win_ext = w_in[0].astype(BF16)
    wout = w_out[0].astype(BF16)
    g1 = norm_mix_g[0][None, :]
    g2 = norm_ffn_g[0][None, :]
    gf = final_norm_g[None, :]
    lng = gmlp_ln_g[0][None, :]
    lnb = gmlp_ln_b[0][None, :]
    sinks = attn_sinks[0]
    ws = gmlp_w_s[0]
    group_dim = GMLP_WIDTH // GMLP_GROUPS
    bsf = jnp.repeat(gmlp_b_s[0].T, group_dim, axis=1)
    ws0 = jnp.repeat(ws[:, 0, 0], group_dim)[None, :]
    bs0 = jnp.repeat(gmlp_b_s[0][:, 0], group_dim)[None, :]
    wrt = jnp.zeros((ROUTER_ROWS, D_MODEL), F32)
    wrt = wrt.at[0:N_GROUPS].set(router_group_w[0].T)
    wrt = wrt.at[8:8 + N_EXPERTS].set(router_expert_w[0].reshape(D_MODEL, N_EXPERTS).T).astype(BF16)
    brt = jnp.full((ROUTER_ROWS, 1), NEG_INF, F32)
    brt = brt.at[0:N_GROUPS, 0].set(router_group_b[0])
    brt = brt.at[8:8 + N_EXPERTS, 0].set(router_expert_b[0].reshape(N_EXPERTS))
    wg, wu, wd = expert_w_gate[0], expert_w_up[0], expert_w_down[0]
    cos_p, sin_p = _rope_tables(jnp.arange(seq, dtype=jnp.int32))
    cos_s, sin_s = _rope_tables(PAST_LEN + jnp.arange(1, dtype=jnp.int32))

    full = lambda shape: pl.BlockSpec(shape, lambda *_: (0,) * len(shape))
    smem = pl.BlockSpec(memory_space=pltpu.SMEM)
    n_tiles = seq // TOK_TILE

    upper = jnp.triu(jnp.ones((TOK_TILE, TOK_TILE), BF16), k=1)
    lpad = (jnp.arange(LANES)[None, :] < jnp.arange(N_EXPERTS)[:, None]).astype(BF16)
    n_tok_tiles = batch * n_tiles

    xmid_p, xl_p, meta_p, tab_p, k_p, v_p = pl.pallas_call(
        _prompt_mixer_kernel,
        grid=(batch, n_tiles),
        in_specs=[
            smem,
            pl.BlockSpec((None, TOK_TILE, D_MODEL), lambda b, t: (b, t, 0)),
            full((1, D_MODEL)),
            full((D_MODEL, IN_WIDTH)),
            pl.BlockSpec((TOK_TILE, LANES), lambda b, t: (t, 0)),
            pl.BlockSpec((TOK_TILE, LANES), lambda b, t: (t, 0)),
            full((1, GMLP_WIDTH)),
            full((1, GMLP_WIDTH)),
            full((GMLP_GROUPS, CHUNK, CHUNK)),
            full((CHUNK, GMLP_WIDTH)),
            full((D_MODEL, D_MODEL)),
            full((1, D_MODEL)),
            full((ROUTER_ROWS, D_MODEL)),
            full((ROUTER_ROWS, 1)),
            full((TOK_TILE, TOK_TILE)),
            full((N_EXPERTS, LANES)),
        ],
        out_specs=[
            pl.BlockSpec((None, TOK_TILE, D_MODEL), lambda b, t: (b, t, 0)),
            pl.BlockSpec((None, SORT_ROWS, D_MODEL), lambda b, t: (b * n_tiles + t, 0, 0)),
            pl.BlockSpec((None, 8, TOK_TILE), lambda b, t: (b * n_tiles + t, 0, 0)),
            pl.BlockSpec((None, N_EXPERTS, LANES), lambda b, t: (b * n_tiles + t, 0, 0)),
            pl.BlockSpec((None, WINDOW, KV_WIDTH), lambda b, t: (b, 0, 0)),
            pl.BlockSpec((None, WINDOW, KV_WIDTH), lambda b, t: (b, 0, 0)),
        ],
        out_shape=[
            jax.ShapeDtypeStruct((batch, seq, D_MODEL), F32),
            jax.ShapeDtypeStruct((n_tok_tiles, SORT_ROWS, D_MODEL), BF16),
            jax.ShapeDtypeStruct((n_tok_tiles, 8, TOK_TILE), F32),
            jax.ShapeDtypeStruct((n_tok_tiles, N_EXPERTS, LANES), F32),
            jax.ShapeDtypeStruct((batch, WINDOW, KV_WIDTH), F32),
            jax.ShapeDtypeStruct((batch, WINDOW, KV_WIDTH), F32),
        ],
        scratch_shapes=[
            pltpu.VMEM((TOK_TILE, IN_WIDTH), F32),
            pltpu.VMEM((TOK_TILE, D_MODEL), BF16),
        ] + [pltpu.VMEM((WINDOW + TOK_TILE, KV_WIDTH), BF16)] * 4 + [
            pltpu.VMEM((len(_NAT_HEADS), TOK_TILE, LANES), BF16),
            pltpu.VMEM((len(_SWP_HEADS), TOK_TILE, LANES), BF16),
            pltpu.VMEM((2 * BLOCKS_PER_TILE, len(_NAT_HEADS) * WINDOW, 2 * WINDOW), F32),
            pltpu.VMEM((2 * BLOCKS_PER_TILE, len(_NAT_HEADS) * WINDOW, 2 * WINDOW), BF16),
            pltpu.VMEM((TOK_TILE, GMLP_WIDTH), BF16),
        ],
        compiler_params=pltpu.CompilerParams(
            dimension_semantics=("arbitrary", "arbitrary"),
            vmem_limit_bytes=_vmem_limit(56 * 1024 * 1024)),
        name="prompt_mixer",
    )(sinks, x_prompt, g1, win_ext, cos_p, sin_p, lng, lnb, ws, bsf, wout, g2, wrt, brt, upper, lpad)

    xs = x_sample.reshape(dec_batch, D_MODEL)
    ck = cache_swa_k[0].reshape(dec_batch, WINDOW, KV_WIDTH)
    cv = cache_swa_v[0].reshape(dec_batch, WINDOW, KV_WIDTH)
    xmid_s, h2_s, meta_s, k_s, v_s, vn_s = pl.pallas_call(
        _sample_mixer_kernel,
        in_specs=[smem] + [pl.BlockSpec(memory_space=pltpu.VMEM)] * 15,
        out_shape=[
            jax.ShapeDtypeStruct((dec_batch, D_MODEL), F32),
            jax.ShapeDtypeStruct((dec_batch, D_MODEL), BF16),
            jax.ShapeDtypeStruct((8, dec_batch), F32),
            jax.ShapeDtypeStruct((dec_batch, KV_WIDTH), F32),
            jax.ShapeDtypeStruct((dec_batch, KV_WIDTH), F32),
            jax.ShapeDtypeStruct((dec_batch, GMLP_WIDTH), F32),
        ],
        scratch_shapes=[pltpu.VMEM((dec_batch, D_MODEL), BF16)],
        compiler_params=pltpu.CompilerParams(vmem_limit_bytes=_vmem_limit(56 * 1024 * 1024)),
        name="sample_mixer",
    )(sinks, xs, ck, cv, g1, win_ext, cos_s, sin_s, lng, lnb, ws0, bs0, wout, g2, wrt, brt)

    texp, gsrc, sdst, n_steps = _ffn_schedule(tab_p)
    yl_p = pl.pallas_call(
        _grouped_ffn_kernel,
        grid_spec=pltpu.PrefetchScalarGridSpec(
            num_scalar_prefetch=4,
            grid=(_ffn_steps_max(n_tok_tiles),),
            in_specs=[
                pl.BlockSpec(memory_space=pl.ANY),
                pl.BlockSpec((None, D_MODEL, D_EXPERT), lambda j, te, gs, sd, ns: (te[j], 0, 0)),
                pl.BlockSpec((None, D_MODEL, D_EXPERT), lambda j, te, gs, sd, ns: (te[j], 0, 0)),
                pl.BlockSpec((None, D_EXPERT, D_MODEL), lambda j, te, gs, sd, ns: (te[j], 0, 0)),
            ],
            out_specs=pl.BlockSpec(memory_space=pl.ANY),
            scratch_shapes=[
                pltpu.VMEM((2, FFN_SLOTS, ROW_GRANULE, D_MODEL), BF16),
                pltpu.VMEM((2, FFN_SLOTS, ROW_GRANULE, D_MODEL), BF16),
                pltpu.VMEM((FFN_SLOTS, ROW_GRANULE, D_MODEL), BF16),
                pltpu.VMEM((D_MODEL, D_EXPERT), BF16),
                pltpu.VMEM((D_MODEL, D_EXPERT), BF16),
                pltpu.VMEM((D_EXPERT, D_MODEL), BF16),
                pltpu.SemaphoreType.DMA((2,)),
                pltpu.SemaphoreType.DMA((2,)),
                pltpu.SemaphoreType.DMA((1,)),
            ],
        ),
        out_shape=jax.ShapeDtypeStruct(((n_tok_tiles + 1) * GRANULES_PER_TILE, ROW_GRANULE, D_MODEL), BF16),
        compiler_params=pltpu.CompilerParams(
            dimension_semantics=("arbitrary",),
            vmem_limit_bytes=_vmem_limit(32 * 1024 * 1024)),
        name="grouped_ffn",
    )(texp, gsrc, sdst, n_steps, xl_p.reshape(n_tok_tiles * GRANULES_PER_TILE, ROW_GRANULE, D_MODEL), wg, wu, wd)
    yl_p = yl_p.reshape((n_tok_tiles + 1) * SORT_ROWS, D_MODEL)

    y_p = pl.pallas_call(
        _combine_kernel,
        grid=(n_tok_tiles,),
        in_specs=[
            pl.BlockSpec((TOK_TILE, D_MODEL), lambda i: (i, 0)),
            pl.BlockSpec((SORT_ROWS, D_MODEL), lambda i: (i, 0)),
            pl.BlockSpec((None, 8, TOK_TILE), lambda i: (i, 0, 0)),
            pl.BlockSpec((1, D_MODEL), lambda i: (0, 0)),
        ],
        out_specs=pl.BlockSpec((TOK_TILE, D_MODEL), lambda i: (i, 0)),
        out_shape=jax.ShapeDtypeStruct((batch * seq, D_MODEL), F32),
        compiler_params=pltpu.CompilerParams(
            dimension_semantics=("arbitrary",),
            vmem_limit_bytes=_vmem_limit(40 * 1024 * 1024)),
        name="moe_combine",
    )(xmid_p.reshape(batch * seq, D_MODEL), yl_p, meta_p, gf)

    comb_s = _combine_matrix(meta_s[None])
    y_s = _dense_moe(h2_s, xmid_s, comb_s, wg, wu, wd, gf, dec_batch)

    return (y_p.reshape(batch, seq, D_MODEL),
            y_s.reshape(dec_batch, 1, D_MODEL),
            k_p.reshape(1, batch, WINDOW, N_KV_HEADS, HEAD_DIM),
            v_p.reshape(1, batch, WINDOW, N_KV_HEADS, HEAD_DIM),
            k_s.reshape(1, dec_batch, 1, N_KV_HEADS, HEAD_DIM),
            v_s.reshape(1, dec_batch, 1, N_KV_HEADS, HEAD_DIM),
            vn_s.reshape(1, dec_batch, 1, GMLP_WIDTH))
```

```python
import jax
import jax.numpy as jnp
import numpy as np
from jax import lax
from jax.experimental import pallas as pl
from jax.experimental.pallas import tpu as pltpu

F32 = jnp.float32
BF16 = jnp.bfloat16

D_MODEL = 1024
HEAD_DIM = 64
HALF = HEAD_DIM // 2
N_Q_HEADS = 8
N_KV_HEADS = 2
ATTN_WIDTH = N_Q_HEADS * HEAD_DIM
KV_WIDTH = N_KV_HEADS * HEAD_DIM
WINDOW = 128
ROPE_THETA = 10000.0
GMLP_WIDTH = D_MODEL - ATTN_WIDTH
GMLP_GROUPS = 8
CHUNK = 128
N_GROUPS = 4
EXPERTS_PER_GROUP = 8
N_EXPERTS = N_GROUPS * EXPERTS_PER_GROUP
D_EXPERT = 256
EPS = 1e-6
NEG_INF = -1e30
PAST_LEN = 16384

LANES = 128
V7X_VMEM_BYTES = 64 * 1024 * 1024

C_Q = 0
C_K = C_Q + ATTN_WIDTH
C_V = C_K + KV_WIDTH
C_U = C_V + KV_WIDTH
C_VG = C_U + GMLP_WIDTH
IN_WIDTH = C_VG + GMLP_WIDTH

ROUTER_ROWS = 48
TOK_TILE = 512
BLOCKS_PER_TILE = TOK_TILE // WINDOW
ROW_GRANULE = 16
SORT_CHUNK = 512
SORT_ROWS = -(-(2 * TOK_TILE + N_EXPERTS * (ROW_GRANULE - 1)) // SORT_CHUNK) * SORT_CHUNK
GRANULES_PER_TILE = SORT_ROWS // ROW_GRANULE
FFN_ROWS = 512
FFN_SLOTS = FFN_ROWS // ROW_GRANULE
GATHER_BUFS = 3


def _dot(a, b):
    return jnp.dot(a, b, preferred_element_type=F32)


def _dot_nt(a, b):
    return lax.dot_general(a, b, (((1,), (1,)), ((), ())), preferred_element_type=F32)


def _gelu(x):
    return 0.5 * x * (1.0 + lax.erf(x * np.float32(np.sqrt(0.5))))


def _rmsnorm(x, g):
    return x * lax.rsqrt(jnp.mean(x * x, axis=-1, keepdims=True) + EPS) * g


def _layernorm(x, g, b):
    mu = jnp.mean(x, axis=-1, keepdims=True)
    xc = x - mu
    return xc * lax.rsqrt(jnp.mean(xc * xc, axis=-1, keepdims=True) + EPS) * g + b


def _first_argmax_rows(x, row_iota, n_rows):
    m = jnp.max(x, axis=0, keepdims=True)
    idx = jnp.min(jnp.where(x == m, row_iota, n_rows), axis=0, keepdims=True)
    return m, idx


def _route(h2b, wrt_ref, brt_ref):
    n_tok = h2b.shape[0]
    lt = _dot_nt(wrt_ref[...], h2b) + brt_ref[...]
    row8 = lax.broadcasted_iota(jnp.int32, (8, n_tok), 0).astype(F32)
    glog = lt[0:8]
    gmax, gidx = _first_argmax_rows(glog, row8, 8)
    g_w = 1.0 / jnp.sum(jnp.exp(glog - gmax), axis=0, keepdims=True)
    esel = lt[8:16]
    for g in range(1, N_GROUPS):
        esel = jnp.where(gidx == g, lt[8 + 8 * g:16 + 8 * g], esel)
    m1, i1 = _first_argmax_rows(esel, row8, 8)
    esel2 = jnp.where(row8 == i1, -jnp.inf, esel)
    m2, i2 = _first_argmax_rows(esel2, row8, 8)
    r = jnp.exp(m2 - m1)
    w1 = 1.0 / (1.0 + r)
    w2 = r / (1.0 + r)
    e1 = gidx * EXPERTS_PER_GROUP + i1
    e2 = gidx * EXPERTS_PER_GROUP + i2
    slab = jnp.where(row8 == 0, e1, 0.0)
    for r_idx, val in ((1, e2), (2, w1 * g_w), (3, w2 * g_w)):
        slab = jnp.where(row8 == r_idx, val, slab)
    return slab


def _softmax_with_sink(s, sink):
    m = jnp.maximum(jnp.max(s, axis=-1, keepdims=True), sink)
    e = jnp.exp(s - m)
    den = jnp.sum(e, axis=-1, keepdims=True) + jnp.exp(sink - m)
    return e * (1.0 / den)


def _rope(x, cos, sin_signed):
    first_half = (lax.broadcasted_iota(jnp.int32, x.shape, 1) & HALF) == 0
    partner = jnp.where(first_half, pltpu.roll(x, LANES - HALF, 1), pltpu.roll(x, HALF, 1))
    return x * cos + partner * sin_signed


_NAT_HEADS = (0, 2, 5, 7)
_SWP_HEADS = (1, 3, 4, 6)


def _prompt_mixer_kernel(sinks_ref, x_ref, g1_ref, win_ref, cos_ref, sin_ref, lng_ref, lnb_ref, ws_ref, bsf_ref,
                         wout_ref, g2_ref, wrt_ref, brt_ref, upper_ref, lpad_ref,
                         xmid_ref, xl_ref, meta_ref, tab_ref, kout_ref, vout_ref,
                         z_ref, mix_ref, k_n, k_s, v_n, v_s, q_nat, q_swp, s_ref, p_ref, vn_ref):
    t = pl.program_id(1)
    kv_bufs = (k_n, k_s, v_n, v_s)

    @pl.when(t == 0)
    def _():
        for ref in kv_bufs:
            ref[0:WINDOW, :] = jnp.zeros((WINDOW, KV_WIDTH), BF16)

    x = x_ref[...]
    h = _rmsnorm(x, g1_ref[...]).astype(BF16)
    z_ref[...] = _dot(h, win_ref[...])

    cos = cos_ref[...]
    sin = sin_ref[...]
    lane = lax.broadcasted_iota(jnp.int32, (WINDOW, LANES), 1)
    lo = lane < HEAD_DIM
    row = lax.broadcasted_iota(jnp.int32, (WINDOW, WINDOW), 0)
    col = lax.broadcasted_iota(jnp.int32, (WINDOW, WINDOW), 1)
    mask_cur = col <= row
    mask_prev_band = col >= row
    mask_prev_first = jnp.logical_and(mask_prev_band, (jnp.zeros_like(row) + t) > 0)
    mask_band = jnp.concatenate([mask_prev_band, mask_cur], axis=1)
    mask_first = jnp.concatenate([mask_prev_first, mask_cur], axis=1)

    cq = cos * np.float32(HEAD_DIM ** -0.5)
    sq = sin * np.float32(HEAD_DIM ** -0.5)
    kf = _rope(z_ref[:, C_K:C_K + KV_WIDTH], cos, sin)
    vf = z_ref[:, C_V:C_V + KV_WIDTH]
    k_n[WINDOW:, :] = kf.astype(BF16)
    k_s[WINDOW:, :] = pltpu.roll(kf, HEAD_DIM, 1).astype(BF16)
    v_n[WINDOW:, :] = vf.astype(BF16)
    v_s[WINDOW:, :] = pltpu.roll(vf, HEAD_DIM, 1).astype(BF16)

    @pl.when(t == pl.num_programs(1) - 1)
    def _():
        kout_ref[...] = kf[TOK_TILE - WINDOW:]
        vout_ref[...] = vf[TOK_TILE - WINDOW:]

    lo_t = lax.broadcasted_iota(jnp.int32, (TOK_TILE, LANES), 1) < HEAD_DIM
    for m in range(N_Q_HEADS // 2):
        qc = _rope(z_ref[:, C_Q + m * LANES:C_Q + (m + 1) * LANES], cq, sq)
        for hd, qh in ((2 * m, jnp.where(lo_t, qc, 0.0)), (2 * m + 1, jnp.where(lo_t, 0.0, qc))):
            if hd in _NAT_HEADS:
                q_nat[_NAT_HEADS.index(hd)] = qh.astype(BF16)
            else:
                q_swp[_SWP_HEADS.index(hd)] = qh.astype(BF16)

    stacks = ((q_nat, _NAT_HEADS, k_n, v_n), (q_swp, _SWP_HEADS, k_s, v_s))
    for j in range(BLOCKS_PER_TILE):
        rows = slice(j * WINDOW, (j + 1) * WINDOW)
        keys = slice(j * WINDOW, (j + 2) * WINDOW)
        for si, (q_ref, _, k_buf, _) in enumerate(stacks):
            q_stack = jnp.concatenate([q_ref[i, rows, :] for i in range(len(_NAT_HEADS))], axis=0)
            s_ref[2 * j + si] = _dot_nt(q_stack, k_buf[keys, :])
    for j in range(BLOCKS_PER_TILE):
        mask = mask_first if j == 0 else mask_band
        for si, (_, heads, _, _) in enumerate(stacks):
            for i, hd in enumerate(heads):
                pr = slice(i * WINDOW, (i + 1) * WINDOW)
                p_ref[2 * j + si, pr, :] = _softmax_with_sink(
                    jnp.where(mask, s_ref[2 * j + si, pr, :], NEG_INF), sinks_ref[hd]).astype(BF16)
    for j in range(BLOCKS_PER_TILE):
        rows = slice(j * WINDOW, (j + 1) * WINDOW)
        keys = slice(j * WINDOW, (j + 2) * WINDOW)
        o_nat = _dot(p_ref[2 * j], v_n[keys, :])
        o_swp = _dot(p_ref[2 * j + 1], v_s[keys, :])
        for m in range(N_Q_HEADS // 2):
            pr = slice(m * WINDOW, (m + 1) * WINDOW)
            even_nat = (2 * m) in _NAT_HEADS
            att = jnp.where(lo, o_nat[pr], o_swp[pr]) if even_nat else jnp.where(lo, o_swp[pr], o_nat[pr])
            mix_ref[rows, m * LANES:(m + 1) * LANES] = att.astype(BF16)

    vn_ref[...] = _layernorm(_gelu(z_ref[:, C_VG:C_VG + GMLP_WIDTH]), lng_ref[...], lnb_ref[...]).astype(BF16)
    for m in range(GMLP_GROUPS // 2):
        cs = slice(m * LANES, (m + 1) * LANES)
        w0 = jnp.where(mask_cur, ws_ref[2 * m], 0.0).astype(BF16)
        w1 = jnp.where(mask_cur, ws_ref[2 * m + 1], 0.0).astype(BF16)
        wcat = jnp.concatenate([w0, w1], axis=1)
        for j in range(BLOCKS_PER_TILE):
            rows = slice(j * WINDOW, (j + 1) * WINDOW)
            vcol = vn_ref[rows, cs]
            rhs = jnp.concatenate([jnp.where(lo, vcol, jnp.zeros_like(vcol)),
                                   jnp.where(lo, jnp.zeros_like(vcol), vcol)], axis=0)
            sp = _dot(wcat, rhs) + bsf_ref[:, cs]
            u = _gelu(z_ref[rows, C_U + m * LANES:C_U + (m + 1) * LANES])
            mix_ref[rows, ATTN_WIDTH + m * LANES:ATTN_WIDTH + (m + 1) * LANES] = (u * sp).astype(BF16)

    for ref in kv_bufs:
        ref[0:WINDOW, :] = ref[TOK_TILE:TOK_TILE + WINDOW, :]

    xmid = x + _dot(mix_ref[...], wout_ref[...])
    xmid_ref[...] = xmid
    h2b = _rmsnorm(xmid, g2_ref[...]).astype(BF16)
    slab = _route(h2b, wrt_ref, brt_ref)
    _local_sort(slab, h2b, upper_ref, lpad_ref, xl_ref, meta_ref, tab_ref)


def _local_sort(slab, h2b, upper_ref, lpad_ref, xl_ref, meta_ref, tab_ref):
    n_tok = h2b.shape[0]
    e1, e2 = slab[0:1], slab[1:2]
    row32 = lax.broadcasted_iota(jnp.int32, (N_EXPERTS, n_tok), 0).astype(F32)
    sel1 = row32 == e1
    sel2 = row32 == e2
    onehot = jnp.where(sel1, 1.0, jnp.where(sel2, 1.0, 0.0))
    earlier = _dot(onehot.astype(BF16), upper_ref[...])
    cnt = jnp.sum(onehot, axis=1, keepdims=True)
    pc = jnp.floor((cnt + (ROW_GRANULE - 1)) * (1.0 / ROW_GRANULE)) * ROW_GRANULE
    pc_b = jnp.broadcast_to(pc, (N_EXPERTS, LANES))
    pc_pad = jnp.concatenate([pc_b, jnp.zeros((LANES - N_EXPERTS, LANES), F32)], axis=0).astype(BF16)
    start = _dot(lpad_ref[...], pc_pad)
    base = start[:, 0:1] + earlier
    d1 = jnp.sum(jnp.where(sel1, base, 0.0), axis=0, keepdims=True)
    d2 = jnp.sum(jnp.where(sel2, base, 0.0), axis=0, keepdims=True)
    for c in range(SORT_ROWS // SORT_CHUNK):
        r_iota = (lax.broadcasted_iota(jnp.int32, (SORT_CHUNK, n_tok), 0) + c * SORT_CHUNK).astype(F32)
        perm = jnp.where(r_iota == d1, 1.0, jnp.where(r_iota == d2, 1.0, 0.0)).astype(BF16)
        xl_ref[c * SORT_CHUNK:(c + 1) * SORT_CHUNK, :] = _dot(perm, h2b).astype(BF16)
    row8 = lax.broadcasted_iota(jnp.int32, (8, n_tok), 0)
    meta_ref[...] = jnp.where(row8 == 0, d1, jnp.where(row8 == 1, d2, jnp.where(row8 >= 4, 0.0, slab)))
    lane = lax.broadcasted_iota(jnp.int32, (N_EXPERTS, LANES), 1)
    tab_ref[...] = jnp.where(lane == 0, pc_b, jnp.where(lane == 1, start, 0.0))


def _sample_mixer_kernel(sinks_ref, x_ref, ck_ref, cv_ref, g1_ref, win_ref, cos_ref, sin_ref, lng_ref, lnb_ref,
                         ws0_ref, bs0_ref, wout_ref, g2_ref, wrt_ref, brt_ref,
                         xmid_ref, h2_ref, meta_ref, kout_ref, vout_ref, vnout_ref,
                         mix_ref):
    n_seq = x_ref.shape[0]
    seq_chunk = 16
    x = x_ref[...]
    h = _rmsnorm(x, g1_ref[...]).astype(BF16)
    z = _dot(h, win_ref[...])
    cos = cos_ref[...]
    sin = sin_ref[...]
    scale = np.float32(HEAD_DIM ** -0.5)
    lane = lax.broadcasted_iota(jnp.int32, (n_seq, LANES), 1)
    lo = lane < HEAD_DIM
    kf = _rope(z[:, C_K:C_K + KV_WIDTH], cos, sin)
    vf = z[:, C_V:C_V + KV_WIDTH]
    kout_ref[...] = kf
    vout_ref[...] = vf
    kb = kf.astype(BF16).astype(F32)
    vb = vf.astype(BF16).astype(F32)

    q_heads = []
    for hd in range(N_Q_HEADS):
        m = hd // 2
        qc = _rope(z[:, C_Q + m * LANES:C_Q + (m + 1) * LANES], cos, sin) * scale
        keep = lo if hd % 2 == 0 else ~lo
        qm = jnp.where(keep, qc, 0.0)
        if (hd % 2) != (hd // (N_Q_HEADS // N_KV_HEADS)):
            qm = pltpu.roll(qm, HEAD_DIM, 1)
        q_heads.append(qm.astype(BF16))

    s_new = [jnp.sum(q_heads[hd].astype(F32) * kb, axis=-1, keepdims=True) for hd in range(N_Q_HEADS)]

    rr = lax.broadcasted_iota(jnp.int32, (N_Q_HEADS * seq_chunk, seq_chunk * WINDOW), 0)
    cc = lax.broadcasted_iota(jnp.int32, (N_Q_HEADS * seq_chunk, seq_chunk * WINDOW), 1)
    same_seq = (rr % seq_chunk) == (cc // WINDOW)
    kv_lo = lax.broadcasted_iota(jnp.int32, (seq_chunk, LANES), 1) < HEAD_DIM

    for c in range(n_seq // seq_chunk):
        sr = slice(c * seq_chunk, (c + 1) * seq_chunk)
        kc = ck_ref[sr].reshape(seq_chunk * WINDOW, KV_WIDTH).astype(BF16)
        vc = cv_ref[sr].reshape(seq_chunk * WINDOW, KV_WIDTH).astype(BF16)
        qs = jnp.concatenate([q_heads[hd][sr] for hd in range(N_Q_HEADS)], axis=0)
        s = jnp.where(same_seq, _dot_nt(qs, kc), NEG_INF)
        sn = jnp.concatenate([s_new[hd][sr] for hd in range(N_Q_HEADS)], axis=0)
        sink = jnp.concatenate([jnp.full((seq_chunk, 1), sinks_ref[hd], F32) for hd in range(N_Q_HEADS)], axis=0)
        m = jnp.maximum(jnp.maximum(jnp.max(s, axis=-1, keepdims=True), sn), sink)
        e = jnp.exp(s - m)
        en = jnp.exp(sn - m)
        inv = 1.0 / (jnp.sum(e, axis=-1, keepdims=True) + en + jnp.exp(sink - m))
        o = _dot((e * inv).astype(BF16), vc)
        pn = (en * inv).astype(BF16).astype(F32)
        for mcol in range(N_Q_HEADS // 2):
            halves = []
            for hd in (2 * mcol, 2 * mcol + 1):
                oh = o[hd * seq_chunk:(hd + 1) * seq_chunk] + pn[hd * seq_chunk:(hd + 1) * seq_chunk] * vb[sr]
                if (hd % 2) != (hd // (N_Q_HEADS // N_KV_HEADS)):
                    oh = pltpu.roll(oh, HEAD_DIM, 1)
                halves.append(oh)
            att = jnp.where(kv_lo, halves[0], halves[1])
            mix_ref[sr, mcol * LANES:(mcol + 1) * LANES] = att.astype(BF16)

    u = _gelu(z[:, C_U:C_U + GMLP_WIDTH])
    vn = _layernorm(_gelu(z[:, C_VG:C_VG + GMLP_WIDTH]), lng_ref[...], lnb_ref[...])
    vnout_ref[...] = vn
    sp = ws0_ref[...].astype(BF16).astype(F32) * vn.astype(BF16).astype(F32) + bs0_ref[...]
    mix_ref[:, ATTN_WIDTH:] = (u * sp).astype(BF16)

    xmid = x + _dot(mix_ref[...], wout_ref[...])
    xmid_ref[...] = xmid
    h2b = _rmsnorm(xmid, g2_ref[...]).astype(BF16)
    h2_ref[...] = h2b
    meta_ref[...] = _route(h2b, wrt_ref, brt_ref)


def _dense_moe_kernel(t_ref, xmid_ref, comb_ref, wg_ref, wu_ref, wd_ref, gf_ref, y_ref, acc_ref):
    e = pl.program_id(1)

    @pl.when(e == 0)
    def _():
        acc_ref[...] = xmid_ref[...]

    tb = t_ref[...]
    gate = _dot(tb, wg_ref[...].astype(BF16))
    up = _dot(tb, wu_ref[...].astype(BF16))
    hid = (gate * (1.0 / (1.0 + jnp.exp(-gate))) * up).astype(BF16)
    out = _dot(hid, wd_ref[...].astype(BF16))
    lane = lax.broadcasted_iota(jnp.int32, comb_ref.shape, 1)
    c_e = jnp.sum(jnp.where(lane == e, comb_ref[...], 0.0), axis=-1, keepdims=True)
    acc_ref[...] += c_e * out

    @pl.when(e == pl.num_programs(1) - 1)
    def _():
        y_ref[...] = _rmsnorm(acc_ref[...], gf_ref[...])


def _grouped_ffn_kernel(texp_ref, gsrc_ref, sdst_ref, ntiles_ref, xl_hbm, wg_ref, wu_ref, wd_ref, yl_hbm,
                        xbuf, ybuf, zbuf, wg_b, wu_b, wd_b, gsem, ssem, tsem):
    j = pl.program_id(0)
    n_tiles = ntiles_ref[0]
    spare_g0 = yl_hbm.shape[0] - GRANULES_PER_TILE

    def gather_copy(tile, s, b):
        return pltpu.make_async_copy(xl_hbm.at[gsrc_ref[tile * FFN_SLOTS + s]], xbuf.at[b, s], gsem.at[b])

    def scatter_copy(tile, s, b):
        return pltpu.make_async_copy(ybuf.at[b, s], yl_hbm.at[sdst_ref[tile * FFN_SLOTS + s]], ssem.at[b])

    def spare_init_copy(c):
        return pltpu.make_async_copy(zbuf, yl_hbm.at[pl.ds(spare_g0 + c * FFN_SLOTS, FFN_SLOTS)], tsem.at[0])

    @pl.when(j < n_tiles)
    def _():
        b = j % 2
        gb = j % GATHER_BUFS
        ahead = GATHER_BUFS - 1
        nxt = jnp.minimum(j + ahead, n_tiles - 1)
        nxt_b = (j + ahead) % GATHER_BUFS

        @pl.when(j == 0)
        def _():
            zbuf[...] = jnp.zeros_like(zbuf)
            for c in range(SORT_ROWS // FFN_ROWS):
                spare_init_copy(c).start()
            for k in range(ahead):
                for s in range(FFN_SLOTS):
                    gather_copy(jnp.minimum(k, n_tiles - 1), s, k).start()
            for c in range(SORT_ROWS // FFN_ROWS):
                spare_init_copy(c).wait()

        for s in range(FFN_SLOTS):
            gather_copy(j, s, gb).wait()

        @pl.when(j >= 2)
        def _():
            for s in range(FFN_SLOTS):
                scatter_copy(j - 2, s, b).wait()

        @pl.when(jnp.logical_or(j == 0, texp_ref[j] != texp_ref[jnp.maximum(j - 1, 0)]))
        def _():
            wg_b[...] = wg_ref[...].astype(BF16)
            wu_b[...] = wu_ref[...].astype(BF16)
            wd_b[...] = wd_ref[...].astype(BF16)

        x = xbuf[gb].reshape(FFN_ROWS, D_MODEL)
        gate = _dot(x, wg_b[...])
        up = _dot(x, wu_b[...])
        for s in range(FFN_SLOTS):
            gather_copy(nxt, s, nxt_b).start()
        hid = (gate * (1.0 / (1.0 + jnp.exp(-gate))) * up).astype(BF16)
        ybuf[b] = _dot(hid, wd_b[...]).astype(BF16).reshape(FFN_SLOTS, ROW_GRANULE, D_MODEL)
        for s in range(FFN_SLOTS):
            scatter_copy(j, s, b).start()

        @pl.when(j == n_tiles - 1)
        def _():
            for k in range(1, GATHER_BUFS):
                for s in range(FFN_SLOTS):
                    gather_copy(nxt, s, (j + k) % GATHER_BUFS).wait()
            for s in range(FFN_SLOTS):
                scatter_copy(j, s, b).wait()

            @pl.when(j >= 1)
            def _():
                for s in range(FFN_SLOTS):
                    scatter_copy(j - 1, s, 1 - b).wait()


def _combine_kernel(xmid_ref, yl_ref, meta_ref, gf_ref, y_ref):
    n_tok = xmid_ref.shape[0]
    meta = meta_ref[...]
    meta_t = jnp.concatenate([meta, jnp.zeros((LANES - 8, n_tok), F32)], axis=0).T
    d1, d2, w1, w2 = meta_t[:, 0:1], meta_t[:, 1:2], meta_t[:, 2:3], meta_t[:, 3:4]
    acc = xmid_ref[...]
    for c in range(SORT_ROWS // SORT_CHUNK):
        r_iota = (lax.broadcasted_iota(jnp.int32, (n_tok, SORT_CHUNK), 1) + c * SORT_CHUNK).astype(F32)
        unsort = jnp.where(r_iota == d1, w1, jnp.where(r_iota == d2, w2, 0.0)).astype(BF16)
        acc = acc + _dot(unsort, yl_ref[c * SORT_CHUNK:(c + 1) * SORT_CHUNK, :])
    y_ref[...] = _rmsnorm(acc, gf_ref[...])


def _ffn_schedule(tab):
    n_tok_tiles = tab.shape[0]
    strips = (tab[:, :, 0] * (1.0 / ROW_GRANULE)).astype(jnp.int32)
    starts = (tab[:, :, 1] * (1.0 / ROW_GRANULE)).astype(jnp.int32)
    used = jnp.sum(strips, axis=1)
    cnt = jnp.concatenate([strips.T, (GRANULES_PER_TILE - used)[None, :]], axis=0)
    row0 = jnp.concatenate([starts.T, used[None, :]], axis=0)
    cs = jnp.cumsum(cnt, axis=1) - cnt
    n_str = jnp.sum(cnt, axis=1)
    np_str = (n_str + FFN_SLOTS - 1) // FFN_SLOTS * FFN_SLOTS
    ends = jnp.cumsum(np_str)
    base = ends - np_str
    n_steps_max = _ffn_steps_max(n_tok_tiles)
    step0 = jnp.arange(n_steps_max, dtype=jnp.int32) * FFN_SLOTS
    stream = jnp.minimum(jnp.sum(ends[None, :] <= step0[:, None], axis=1), N_STREAMS - 1)
    pick = stream[:, None] == jnp.arange(N_STREAMS, dtype=jnp.int32)[None, :]
    sel = lambda x: jnp.sum(jnp.where(pick[:, :, None], x[None], 0), axis=1)
    cs_j, cnt_j, row0_j = sel(cs), sel(cnt), sel(row0)
    base_j = jnp.sum(jnp.where(pick, base[None, :], 0), axis=1)
    q = step0[:, None] + jnp.arange(FFN_SLOTS, dtype=jnp.int32)[None, :] - base_j[:, None]
    reached = cs_j[:, None, :] <= q[:, :, None]
    last = lambda x: jnp.sum(jnp.where(reached, jnp.diff(x, axis=1, prepend=0)[:, None, :], 0), axis=2)
    tile_idx = jnp.sum(reached, axis=2).astype(jnp.int32) - 1
    g = q - last(cs_j)
    valid = g < last(cnt_j)
    granule = tile_idx * GRANULES_PER_TILE + last(row0_j) + g
    slot = jnp.arange(FFN_SLOTS, dtype=jnp.int32)[None, :]
    spare = n_tok_tiles * GRANULES_PER_TILE + (jnp.arange(n_steps_max, dtype=jnp.int32)[:, None] % 2) * FFN_SLOTS + slot
    gsrc = jnp.where(valid, granule, slot).astype(jnp.int32)
    sdst = jnp.where(valid, granule, spare).astype(jnp.int32)
    texp = jnp.minimum(stream, N_EXPERTS - 1).astype(jnp.int32)
    n_steps = (ends[-1] // FFN_SLOTS).astype(jnp.int32).reshape(1)
    return texp, gsrc.reshape(-1), sdst.reshape(-1), n_steps


N_STREAMS = N_EXPERTS + 1


def _ffn_steps_max(n_tok_tiles):
    return -(-(n_tok_tiles * GRANULES_PER_TILE + N_STREAMS * (FFN_SLOTS - 1)) // FFN_SLOTS)


def _vmem_limit(n_bytes):
    return int(min(n_bytes, V7X_VMEM_BYTES - 4 * 1024 * 1024))


def _rope_tables(pos):
    inv_freq = ROPE_THETA ** (-jnp.arange(HALF, dtype=F32) * 2.0 / HEAD_DIM)
    ang = pos.astype(F32)[:, None] * inv_freq[None, :]
    cos, sin = jnp.cos(ang), jnp.sin(ang)
    reps = LANES // HEAD_DIM
    return jnp.tile(jnp.concatenate([cos, cos], axis=1), (1, reps)), jnp.tile(jnp.concatenate([-sin, sin], axis=1), (1, reps))


def _combine_matrix(meta):
    m = jnp.swapaxes(meta, 1, 2).reshape(-1, 8)
    ids = m[:, 0:2].astype(jnp.int32)
    return jnp.sum(jax.nn.one_hot(ids, N_EXPERTS, dtype=F32) * m[:, 2:4, None], axis=1)


def _dense_moe(t, xmid, comb, wg, wu, wd, gf, tile):
    n = t.shape[0]
    comb = jnp.pad(comb, ((0, 0), (0, LANES - N_EXPERTS)))
    return pl.pallas_call(
        _dense_moe_kernel,
        grid=(n // tile, N_EXPERTS),
        in_specs=[
            pl.BlockSpec((tile, D_MODEL), lambda i, e: (i, 0)),
            pl.BlockSpec((tile, D_MODEL), lambda i, e: (i, 0)),
            pl.BlockSpec((tile, LANES), lambda i, e: (i, 0)),
            pl.BlockSpec((None, D_MODEL, D_EXPERT), lambda i, e: (e, 0, 0)),
            pl.BlockSpec((None, D_MODEL, D_EXPERT), lambda i, e: (e, 0, 0)),
            pl.BlockSpec((None, D_EXPERT, D_MODEL), lambda i, e: (e, 0, 0)),
            pl.BlockSpec((1, D_MODEL), lambda i, e: (0, 0)),
        ],
        out_specs=pl.BlockSpec((tile, D_MODEL), lambda i, e: (i, 0)),
        out_shape=jax.ShapeDtypeStruct((n, D_MODEL), F32),
        scratch_shapes=[pltpu.VMEM((tile, D_MODEL), F32)],
        compiler_params=pltpu.CompilerParams(
            dimension_semantics=("arbitrary", "arbitrary"),
            vmem_limit_bytes=_vmem_limit(48 * 1024 * 1024)),
        name="dense_moe",
    )(t, xmid, comb, wg, wu, wd, gf)


def kernel(x_prompt, x_sample, cache_swa_k, cache_swa_v, norm_mix_g, w_in, attn_sinks, gmlp_ln_g, gmlp_ln_b,
           gmlp_w_s, gmlp_b_s, w_out, norm_ffn_g, router_group_w, router_group_b, router_expert_w,
           router_expert_b, expert_w_gate, expert_w_up, expert_w_down, final_norm_g):
    assert norm_mix_g.shape[0] == 1, "single-layer trunk"
    batch, seq, _ = x_prompt.shape
    dec_batch = x_sample.shape[0]
    assert x_sample.shape[1] == 1 and seq % TOK_TILE == 0

    win_ext = w_in[0].astype(BF16)
    wout = w_out[0].astype(BF16)
    g1 = norm_mix_g[0][None, :]
    g2 = norm_ffn_g[0][None, :]
    gf = final_norm_g[None, :]
    lng = gmlp_ln_g[0][None, :]
    lnb = gmlp_ln_b[0][None, :]
    sinks = attn_sinks[0]
    ws = gmlp_w_s[0]
    group_dim = GMLP_WIDTH // GMLP_GROUPS
    bsf = jnp.repeat(gmlp_b_s[0].T, group_dim, axis=1)
    ws0 = jnp.repeat(ws[:, 0, 0], group_dim)[None, :]
    bs0 = jnp.repeat(gmlp_b_s[0][:, 0], group_dim)[None, :]
    wrt = jnp.zeros((ROUTER_ROWS, D_MODEL), F32)
    wrt = wrt.at[0:N_GROUPS].set(router_group_w[0].T)
    wrt = wrt.at[8:8 + N_EXPERTS].set(router_expert_w[0].reshape(D_MODEL, N_EXPERTS).T).astype(BF16)
    brt = jnp.full((ROUTER_ROWS, 1), NEG_INF, F32)
    brt = brt.at[0:N_GROUPS, 0].set(router_group_b[0])
    brt = brt.at[8:8 + N_EXPERTS, 0].set(router_expert_b[0].reshape(N_EXPERTS))
    wg, wu, wd = expert_w_gate[0], expert_w_up[0], expert_w_down[0]
    cos_p, sin_p = _rope_tables(jnp.arange(seq, dtype=jnp.int32))
    cos_s, sin_s = _rope_tables(PAST_LEN + jnp.arange(1, dtype=jnp.int32))

    full = lambda shape: pl.BlockSpec(shape, lambda *_: (0,) * len(shape))
    smem = pl.BlockSpec(memory_space=pltpu.SMEM)
    n_tiles = seq // TOK_TILE

    upper = jnp.triu(jnp.ones((TOK_TILE, TOK_TILE), BF16), k=1)
    lpad = (jnp.arange(LANES)[None, :] < jnp.arange(N_EXPERTS)[:, None]).astype(BF16)
    n_tok_tiles = batch * n_tiles

    xmid_p, xl_p, meta_p, tab_p, k_p, v_p = pl.pallas_call(
        _prompt_mixer_kernel,
        grid=(batch, n_tiles),
        in_specs=[
            smem,
            pl.BlockSpec((None, TOK_TILE, D_MODEL), lambda b, t: (b, t, 0)),
            full((1, D_MODEL)),
            full((D_MODEL, IN_WIDTH)),
            pl.BlockSpec((TOK_TILE, LANES), lambda b, t: (t, 0)),
            pl.BlockSpec((TOK_TILE, LANES), lambda b, t: (t, 0)),
            full((1, GMLP_WIDTH)),
            full((1, GMLP_WIDTH)),
            full((GMLP_GROUPS, CHUNK, CHUNK)),
            full((CHUNK, GMLP_WIDTH)),
            full((D_MODEL, D_MODEL)),
            full((1, D_MODEL)),
            full((ROUTER_ROWS, D_MODEL)),
            full((ROUTER_ROWS, 1)),
            full((TOK_TILE, TOK_TILE)),
            full((N_EXPERTS, LANES)),
        ],
        out_specs=[
            pl.BlockSpec((None, TOK_TILE, D_MODEL), lambda b, t: (b, t, 0)),
            pl.BlockSpec((None, SORT_ROWS, D_MODEL), lambda b, t: (b * n_tiles + t, 0, 0)),
            pl.BlockSpec((None, 8, TOK_TILE), lambda b, t: (b * n_tiles + t, 0, 0)),
            pl.BlockSpec((None, N_EXPERTS, LANES), lambda b, t: (b * n_tiles + t, 0, 0)),
            pl.BlockSpec((None, WINDOW, KV_WIDTH), lambda b, t: (b, 0, 0)),
            pl.BlockSpec((None, WINDOW, KV_WIDTH), lambda b, t: (b, 0, 0)),
        ],
        out_shape=[
            jax.ShapeDtypeStruct((batch, seq, D_MODEL), F32),
            jax.ShapeDtypeStruct((n_tok_tiles, SORT_ROWS, D_MODEL), BF16),
            jax.ShapeDtypeStruct((n_tok_tiles, 8, TOK_TILE), F32),
            jax.ShapeDtypeStruct((n_tok_tiles, N_EXPERTS, LANES), F32),
            jax.ShapeDtypeStruct((batch, WINDOW, KV_WIDTH), F32),
            jax.ShapeDtypeStruct((batch, WINDOW, KV_WIDTH), F32),
        ],
        scratch_shapes=[
            pltpu.VMEM((TOK_TILE, IN_WIDTH), F32),
            pltpu.VMEM((TOK_TILE, D_MODEL), BF16),
        ] + [pltpu.VMEM((WINDOW + TOK_TILE, KV_WIDTH), BF16)] * 4 + [
            pltpu.VMEM((len(_NAT_HEADS), TOK_TILE, LANES), BF16),
            pltpu.VMEM((len(_SWP_HEADS), TOK_TILE, LANES), BF16),
            pltpu.VMEM((2 * BLOCKS_PER_TILE, len(_NAT_HEADS) * WINDOW, 2 * WINDOW), F32),
            pltpu.VMEM((2 * BLOCKS_PER_TILE, len(_NAT_HEADS) * WINDOW, 2 * WINDOW), BF16),
            pltpu.VMEM((TOK_TILE, GMLP_WIDTH), BF16),
        ],
        compiler_params=pltpu.CompilerParams(
            dimension_semantics=("arbitrary", "arbitrary"),
            vmem_limit_bytes=_vmem_limit(56 * 1024 * 1024)),
        name="prompt_mixer",
    )(sinks, x_prompt, g1, win_ext, cos_p, sin_p, lng, lnb, ws, bsf, wout, g2, wrt, brt, upper, lpad)

    xs = x_sample.reshape(dec_batch, D_MODEL)
    ck = cache_swa_k[0].reshape(dec_batch, WINDOW, KV_WIDTH)
    cv = cache_swa_v[0].reshape(dec_batch, WINDOW, KV_WIDTH)
    xmid_s, h2_s, meta_s, k_s, v_s, vn_s = pl.pallas_call(
        _sample_mixer_kernel,
        in_specs=[smem] + [pl.BlockSpec(memory_space=pltpu.VMEM)] * 15,
        out_shape=[
            jax.ShapeDtypeStruct((dec_batch, D_MODEL), F32),
            jax.ShapeDtypeStruct((dec_batch, D_MODEL), BF16),
            jax.ShapeDtypeStruct((8, dec_batch), F32),
            jax.ShapeDtypeStruct((dec_batch, KV_WIDTH), F32),
            jax.ShapeDtypeStruct((dec_batch, KV_WIDTH), F32),
            jax.ShapeDtypeStruct((dec_batch, GMLP_WIDTH), F32),
        ],
        scratch_shapes=[pltpu.VMEM((dec_batch, D_MODEL), BF16)],
        compiler_params=pltpu.CompilerParams(vmem_limit_bytes=_vmem_limit(56 * 1024 * 1024)),
        name="sample_mixer",
    )(sinks, xs, ck, cv, g1, win_ext, cos_s, sin_s, lng, lnb, ws0, bs0, wout, g2, wrt, brt)

    texp, gsrc, sdst, n_steps = _ffn_schedule(tab_p)
    yl_p = pl.pallas_call(
        _grouped_ffn_kernel,
        grid_spec=pltpu.PrefetchScalarGridSpec(
            num_scalar_prefetch=4,
            grid=(_ffn_steps_max(n_tok_tiles),),
            in_specs=[
                pl.BlockSpec(memory_space=pl.ANY),
                pl.BlockSpec((None, D_MODEL, D_EXPERT), lambda j, te, gs, sd, ns: (te[j], 0, 0)),
                pl.BlockSpec((None, D_MODEL, D_EXPERT), lambda j, te, gs, sd, ns: (te[j], 0, 0)),
                pl.BlockSpec((None, D_EXPERT, D_MODEL), lambda j, te, gs, sd, ns: (te[j], 0, 0)),
            ],
            out_specs=pl.BlockSpec(memory_space=pl.ANY),
            scratch_shapes=[
                pltpu.VMEM((GATHER_BUFS, FFN_SLOTS, ROW_GRANULE, D_MODEL), BF16),
                pltpu.VMEM((2, FFN_SLOTS, ROW_GRANULE, D_MODEL), BF16),
                pltpu.VMEM((FFN_SLOTS, ROW_GRANULE, D_MODEL), BF16),
                pltpu.VMEM((D_MODEL, D_EXPERT), BF16),
                pltpu.VMEM((D_MODEL, D_EXPERT), BF16),
                pltpu.VMEM((D_EXPERT, D_MODEL), BF16),
                pltpu.SemaphoreType.DMA((GATHER_BUFS,)),
                pltpu.SemaphoreType.DMA((2,)),
                pltpu.SemaphoreType.DMA((1,)),
            ],
        ),
        out_shape=jax.ShapeDtypeStruct(((n_tok_tiles + 1) * GRANULES_PER_TILE, ROW_GRANULE, D_MODEL), BF16),
        compiler_params=pltpu.CompilerParams(
            dimension_semantics=("arbitrary",),
            vmem_limit_bytes=_vmem_limit(32 * 1024 * 1024)),
        name="grouped_ffn",
    )(texp, gsrc, sdst, n_steps, xl_p.reshape(n_tok_tiles * GRANULES_PER_TILE, ROW_GRANULE, D_MODEL), wg, wu, wd)
    yl_p = yl_p.reshape((n_tok_tiles + 1) * SORT_ROWS, D_MODEL)

    y_p = pl.pallas_call(
        _combine_kernel,
        grid=(n_tok_tiles,),
        in_specs=[
            pl.BlockSpec((TOK_TILE, D_MODEL), lambda i: (i, 0)),
            pl.BlockSpec((SORT_ROWS, D_MODEL), lambda i: (i, 0)),
            pl.BlockSpec((None, 8, TOK_TILE), lambda i: (i, 0, 0)),
            pl.BlockSpec((1, D_MODEL), lambda i: (0, 0)),
        ],
        out_specs=pl.BlockSpec((TOK_TILE, D_MODEL), lambda i: (i, 0)),
        out_shape=jax.ShapeDtypeStruct((batch * seq, D_MODEL), F32),
        compiler_params=pltpu.CompilerParams(
            dimension_semantics=("arbitrary",),
            vmem_limit_bytes=_vmem_limit(40 * 1024 * 1024)),
        name="moe_combine",
    )(xmid_p.reshape(batch * seq, D_MODEL), yl_p, meta_p, gf)

    comb_s = _combine_matrix(meta_s[None])
    y_s = _dense_moe(h2_s, xmid_s, comb_s, wg, wu, wd, gf, dec_batch)

    return (y_p.reshape(batch, seq, D_MODEL),
            y_s.reshape(dec_batch, 1, D_MODEL),
            k_p.reshape(1, batch, WINDOW, N_KV_HEADS, HEAD_DIM),
            v_p.reshape(1, batch, WINDOW, N_KV_HEADS, HEAD_DIM),
            k_s.reshape(1, dec_batch, 1, N_KV_HEADS, HEAD_DIM),
            v_s.reshape(1, dec_batch, 1, N_KV_HEADS, HEAD_DIM),
            vn_s.reshape(1, dec_batch, 1, GMLP_WIDTH))
```

```python
import functools

import jax
import jax.numpy as jnp
import numpy as np
from jax import lax
from jax.experimental import pallas as pl
from jax.experimental.pallas import tpu as pltpu

F32 = jnp.float32
BF16 = jnp.bfloat16

D_MODEL = 1024
HEAD_DIM = 64
HALF = HEAD_DIM // 2
N_Q_HEADS = 8
N_KV_HEADS = 2
ATTN_WIDTH = N_Q_HEADS * HEAD_DIM
KV_WIDTH = N_KV_HEADS * HEAD_DIM
WINDOW = 128
ROPE_THETA = 10000.0
GMLP_WIDTH = D_MODEL - ATTN_WIDTH
GMLP_GROUPS = 8
CHUNK = 128
N_GROUPS = 4
EXPERTS_PER_GROUP = 8
N_EXPERTS = N_GROUPS * EXPERTS_PER_GROUP
D_EXPERT = 256
EPS = 1e-6
NEG_INF = -1e30
PAST_LEN = 16384

LANES = 128
V7X_VMEM_BYTES = 64 * 1024 * 1024

C_Q = 0
C_K = C_Q + ATTN_WIDTH
C_V = C_K + KV_WIDTH
C_U = C_V + KV_WIDTH
C_VG = C_U + GMLP_WIDTH
IN_WIDTH = C_VG + GMLP_WIDTH

ROUTER_ROWS = 48
TOK_TILE = 512
BLOCKS_PER_TILE = TOK_TILE // WINDOW
ROW_GRANULE = 16
SORT_CHUNK = 512
SORT_ROWS = -(-(2 * TOK_TILE + N_EXPERTS * (ROW_GRANULE - 1)) // SORT_CHUNK) * SORT_CHUNK
GRANULES_PER_TILE = SORT_ROWS // ROW_GRANULE
FFN_ROWS = 512
FFN_SLOTS = FFN_ROWS // ROW_GRANULE
GATHER_BUFS = 3


def _dot(a, b):
    return jnp.dot(a, b, preferred_element_type=F32)


def _dot_nt(a, b):
    return lax.dot_general(a, b, (((1,), (1,)), ((), ())), preferred_element_type=F32)


def _gelu(x):
    return 0.5 * x * (1.0 + lax.erf(x * np.float32(np.sqrt(0.5))))


def _rmsnorm(x, g):
    return x * lax.rsqrt(jnp.mean(x * x, axis=-1, keepdims=True) + EPS) * g


def _layernorm(x, g, b):
    mu = jnp.mean(x, axis=-1, keepdims=True)
    xc = x - mu
    return xc * lax.rsqrt(jnp.mean(xc * xc, axis=-1, keepdims=True) + EPS) * g + b


def _first_argmax_rows(x, row_iota, n_rows):
    m = jnp.max(x, axis=0, keepdims=True)
    idx = jnp.min(jnp.where(x == m, row_iota, n_rows), axis=0, keepdims=True)
    return m, idx


def _route(h2b, wrt_ref, brt_ref):
    n_tok = h2b.shape[0]
    lt = _dot_nt(wrt_ref[...], h2b) + brt_ref[...]
    row8 = lax.broadcasted_iota(jnp.int32, (8, n_tok), 0).astype(F32)
    glog = lt[0:8]
    gmax, gidx = _first_argmax_rows(glog, row8, 8)
    g_w = 1.0 / jnp.sum(jnp.exp(glog - gmax), axis=0, keepdims=True)
    esel = lt[8:16]
    for g in range(1, N_GROUPS):
        esel = jnp.where(gidx == g, lt[8 + 8 * g:16 + 8 * g], esel)
    m1, i1 = _first_argmax_rows(esel, row8, 8)
    esel2 = jnp.where(row8 == i1, -jnp.inf, esel)
    m2, i2 = _first_argmax_rows(esel2, row8, 8)
    r = jnp.exp(m2 - m1)
    w1 = 1.0 / (1.0 + r)
    w2 = r / (1.0 + r)
    e1 = gidx * EXPERTS_PER_GROUP + i1
    e2 = gidx * EXPERTS_PER_GROUP + i2
    slab = jnp.where(row8 == 0, e1, 0.0)
    for r_idx, val in ((1, e2), (2, w1 * g_w), (3, w2 * g_w)):
        slab = jnp.where(row8 == r_idx, val, slab)
    return slab


def _softmax_with_sink(s, sink):
    m = jnp.maximum(jnp.max(s, axis=-1, keepdims=True), sink)
    e = jnp.exp(s - m)
    den = jnp.sum(e, axis=-1, keepdims=True) + jnp.exp(sink - m)
    return e * (1.0 / den)


def _rope(x, cos, sin_signed):
    first_half = (lax.broadcasted_iota(jnp.int32, x.shape, 1) & HALF) == 0
    partner = jnp.where(first_half, pltpu.roll(x, LANES - HALF, 1), pltpu.roll(x, HALF, 1))
    return x * cos + partner * sin_signed


_NAT_HEADS = (0, 2, 5, 7)
_SWP_HEADS = (1, 3, 4, 6)


def _prompt_mixer_kernel(sinks_ref, x_next_ref, x_ref, g1_ref, win_ref, cos_ref, sin_ref, lng_ref, lnb_ref, ws_ref,
                         bsf_ref, wout_ref, g2_ref, wrt_ref, brt_ref, upper_ref, lpad_ref,
                         xmid_ref, xl_ref, meta_ref, tab_ref, kout_ref, vout_ref,
                         z_a, z_b, mix_ref, k_n, k_s, v_n, v_s, q_nat, q_swp, s_ref, p_ref, vn_ref, *, tiles_per_seq):
    step = pl.program_id(0)
    t = jnp.maximum(step - 1, 0) % tiles_per_seq
    kv_bufs = (k_n, k_s, v_n, v_s)

    @pl.when(step == 0)
    def _():
        z_b[...] = jnp.zeros_like(z_b)

    @pl.when(t == 0)
    def _():
        for ref in kv_bufs:
            ref[0:WINDOW, :] = jnp.zeros((WINDOW, KV_WIDTH), BF16)

    args = (t, sinks_ref, x_next_ref, x_ref, g1_ref, win_ref, cos_ref, sin_ref, lng_ref, lnb_ref, ws_ref, bsf_ref,
            wout_ref, g2_ref, wrt_ref, brt_ref, upper_ref, lpad_ref, xmid_ref, xl_ref, meta_ref, tab_ref, kout_ref,
            vout_ref, mix_ref, kv_bufs, q_nat, q_swp, s_ref, p_ref, vn_ref)

    @pl.when(step % 2 == 0)
    def _():
        _prompt_mixer_body(z_a, z_b, *args)

    @pl.when(step % 2 == 1)
    def _():
        _prompt_mixer_body(z_b, z_a, *args)


def _prompt_mixer_body(z_next, z_ref, t, sinks_ref, x_next_ref, x_ref, g1_ref, win_ref, cos_ref, sin_ref, lng_ref,
                       lnb_ref, ws_ref, bsf_ref, wout_ref, g2_ref, wrt_ref, brt_ref, upper_ref, lpad_ref,
                       xmid_ref, xl_ref, meta_ref, tab_ref, kout_ref, vout_ref, mix_ref, kv_bufs,
                       q_nat, q_swp, s_ref, p_ref, vn_ref):
    k_n, k_s, v_n, v_s = kv_bufs
    z_next[...] = _dot(_rmsnorm(x_next_ref[...], g1_ref[...]).astype(BF16), win_ref[...])

    x = x_ref[...]
    cos = cos_ref[...]
    sin = sin_ref[...]
    lane = lax.broadcasted_iota(jnp.int32, (WINDOW, LANES), 1)
    lo = lane < HEAD_DIM
    row = lax.broadcasted_iota(jnp.int32, (WINDOW, WINDOW), 0)
    col = lax.broadcasted_iota(jnp.int32, (WINDOW, WINDOW), 1)
    mask_cur = col <= row
    mask_prev_band = col >= row
    mask_prev_first = jnp.logical_and(mask_prev_band, (jnp.zeros_like(row) + t) > 0)
    mask_band = jnp.concatenate([mask_prev_band, mask_cur], axis=1)
    mask_first = jnp.concatenate([mask_prev_first, mask_cur], axis=1)

    cq = cos * np.float32(HEAD_DIM ** -0.5)
    sq = sin * np.float32(HEAD_DIM ** -0.5)
    kf = _rope(z_ref[:, C_K:C_K + KV_WIDTH], cos, sin)
    vf = z_ref[:, C_V:C_V + KV_WIDTH]
    k_n[WINDOW:, :] = kf.astype(BF16)
    k_s[WINDOW:, :] = pltpu.roll(kf, HEAD_DIM, 1).astype(BF16)
    v_n[WINDOW:, :] = vf.astype(BF16)
    v_s[WINDOW:, :] = pltpu.roll(vf, HEAD_DIM, 1).astype(BF16)

    kout_ref[...] = kf[TOK_TILE - WINDOW:]
    vout_ref[...] = vf[TOK_TILE - WINDOW:]

    lo_t = lax.broadcasted_iota(jnp.int32, (TOK_TILE, LANES), 1) < HEAD_DIM
    for m in range(N_Q_HEADS // 2):
        qc = _rope(z_ref[:, C_Q + m * LANES:C_Q + (m + 1) * LANES], cq, sq)
        for hd, qh in ((2 * m, jnp.where(lo_t, qc, 0.0)), (2 * m + 1, jnp.where(lo_t, 0.0, qc))):
            if hd in _NAT_HEADS:
                q_nat[_NAT_HEADS.index(hd)] = qh.astype(BF16)
            else:
                q_swp[_SWP_HEADS.index(hd)] = qh.astype(BF16)

    stacks = ((q_nat, _NAT_HEADS, k_n, v_n), (q_swp, _SWP_HEADS, k_s, v_s))
    for j in range(BLOCKS_PER_TILE):
        rows = slice(j * WINDOW, (j + 1) * WINDOW)
        keys = slice(j * WINDOW, (j + 2) * WINDOW)
        for si, (q_ref, _, k_buf, _) in enumerate(stacks):
            q_stack = jnp.concatenate([q_ref[i, rows, :] for i in range(len(_NAT_HEADS))], axis=0)
            s_ref[2 * j + si] = _dot_nt(q_stack, k_buf[keys, :])
    for j in range(BLOCKS_PER_TILE):
        mask = mask_first if j == 0 else mask_band
        for si, (_, heads, _, _) in enumerate(stacks):
            for i, hd in enumerate(heads):
                pr = slice(i * WINDOW, (i + 1) * WINDOW)
                p_ref[2 * j + si, pr, :] = _softmax_with_sink(
                    jnp.where(mask, s_ref[2 * j + si, pr, :], NEG_INF), sinks_ref[hd]).astype(BF16)
    for j in range(BLOCKS_PER_TILE):
        rows = slice(j * WINDOW, (j + 1) * WINDOW)
        keys = slice(j * WINDOW, (j + 2) * WINDOW)
        o_nat = _dot(p_ref[2 * j], v_n[keys, :])
        o_swp = _dot(p_ref[2 * j + 1], v_s[keys, :])
        for m in range(N_Q_HEADS // 2):
            pr = slice(m * WINDOW, (m + 1) * WINDOW)
            even_nat = (2 * m) in _NAT_HEADS
            att = jnp.where(lo, o_nat[pr], o_swp[pr]) if even_nat else jnp.where(lo, o_swp[pr], o_nat[pr])
            mix_ref[rows, m * LANES:(m + 1) * LANES] = att.astype(BF16)

    vn_ref[...] = _layernorm(_gelu(z_ref[:, C_VG:C_VG + GMLP_WIDTH]), lng_ref[...], lnb_ref[...]).astype(BF16)
    for m in range(GMLP_GROUPS // 2):
        cs = slice(m * LANES, (m + 1) * LANES)
        w0 = jnp.where(mask_cur, ws_ref[2 * m], 0.0).astype(BF16)
        w1 = jnp.where(mask_cur, ws_ref[2 * m + 1], 0.0).astype(BF16)
        wcat = jnp.concatenate([w0, w1], axis=1)
        for j in range(BLOCKS_PER_TILE):
            rows = slice(j * WINDOW, (j + 1) * WINDOW)
            vcol = vn_ref[rows, cs]
            rhs = jnp.concatenate([jnp.where(lo, vcol, jnp.zeros_like(vcol)),
                                   jnp.where(lo, jnp.zeros_like(vcol), vcol)], axis=0)
            sp = _dot(wcat, rhs) + bsf_ref[:, cs]
            u = _gelu(z_ref[rows, C_U + m * LANES:C_U + (m + 1) * LANES])
            mix_ref[rows, ATTN_WIDTH + m * LANES:ATTN_WIDTH + (m + 1) * LANES] = (u * sp).astype(BF16)

    for ref in kv_bufs:
        ref[0:WINDOW, :] = ref[TOK_TILE:TOK_TILE + WINDOW, :]

    xmid = x + _dot(mix_ref[...], wout_ref[...])
    xmid_ref[...] = xmid
    h2b = _rmsnorm(xmid, g2_ref[...]).astype(BF16)
    slab = _route(h2b, wrt_ref, brt_ref)
    _local_sort(slab, h2b, upper_ref, lpad_ref, xl_ref, meta_ref, tab_ref)


def _local_sort(slab, h2b, upper_ref, lpad_ref, xl_ref, meta_ref, tab_ref):
    n_tok = h2b.shape[0]
    e1, e2 = slab[0:1], slab[1:2]
    row32 = lax.broadcasted_iota(jnp.int32, (N_EXPERTS, n_tok), 0).astype(F32)
    sel1 = row32 == e1
    sel2 = row32 == e2
    onehot = jnp.where(sel1, 1.0, jnp.where(sel2, 1.0, 0.0))
    earlier = _dot(onehot.astype(BF16), upper_ref[...])
    cnt = jnp.sum(onehot, axis=1, keepdims=True)
    pc = jnp.floor((cnt + (ROW_GRANULE - 1)) * (1.0 / ROW_GRANULE)) * ROW_GRANULE
    pc_b = jnp.broadcast_to(pc, (N_EXPERTS, LANES))
    pc_pad = jnp.concatenate([pc_b, jnp.zeros((LANES - N_EXPERTS, LANES), F32)], axis=0).astype(BF16)
    start = _dot(lpad_ref[...], pc_pad)
    base = start[:, 0:1] + earlier
    d1 = jnp.sum(jnp.where(sel1, base, 0.0), axis=0, keepdims=True)
    d2 = jnp.sum(jnp.where(sel2, base, 0.0), axis=0, keepdims=True)
    for c in range(SORT_ROWS // SORT_CHUNK):
        r_iota = (lax.broadcasted_iota(jnp.int32, (SORT_CHUNK, n_tok), 0) + c * SORT_CHUNK).astype(F32)
        perm = jnp.where(r_iota == d1, 1.0, jnp.where(r_iota == d2, 1.0, 0.0)).astype(BF16)
        xl_ref[c * SORT_CHUNK:(c + 1) * SORT_CHUNK, :] = _dot(perm, h2b).astype(BF16)
    row8 = lax.broadcasted_iota(jnp.int32, (8, n_tok), 0)
    meta_ref[...] = jnp.where(row8 == 0, d1, jnp.where(row8 == 1, d2, jnp.where(row8 >= 4, 0.0, slab)))
    lane = lax.broadcasted_iota(jnp.int32, (N_EXPERTS, LANES), 1)
    tab_ref[...] = jnp.where(lane == 0, pc_b, jnp.where(lane == 1, start, 0.0))


def _sample_mixer_kernel(sinks_ref, x_ref, ck_ref, cv_ref, g1_ref, win_ref, cos_ref, sin_ref, lng_ref, lnb_ref,
                         ws0_ref, bs0_ref, wout_ref, g2_ref, wrt_ref, brt_ref,
                         xmid_ref, h2_ref, meta_ref, kout_ref, vout_ref, vnout_ref,
                         mix_ref):
    n_seq = x_ref.shape[0]
    seq_chunk = 16
    x = x_ref[...]
    h = _rmsnorm(x, g1_ref[...]).astype(BF16)
    z = _dot(h, win_ref[...])
    cos = cos_ref[...]
    sin = sin_ref[...]
    scale = np.float32(HEAD_DIM ** -0.5)
    lane = lax.broadcasted_iota(jnp.int32, (n_seq, LANES), 1)
    lo = lane < HEAD_DIM
    kf = _rope(z[:, C_K:C_K + KV_WIDTH], cos, sin)
    vf = z[:, C_V:C_V + KV_WIDTH]
    kout_ref[...] = kf
    vout_ref[...] = vf
    kb = kf.astype(BF16).astype(F32)
    vb = vf.astype(BF16).astype(F32)

    q_heads = []
    for hd in range(N_Q_HEADS):
        m = hd // 2
        qc = _rope(z[:, C_Q + m * LANES:C_Q + (m + 1) * LANES], cos, sin) * scale
        keep = lo if hd % 2 == 0 else ~lo
        qm = jnp.where(keep, qc, 0.0)
        if (hd % 2) != (hd // (N_Q_HEADS // N_KV_HEADS)):
            qm = pltpu.roll(qm, HEAD_DIM, 1)
        q_heads.append(qm.astype(BF16))

    s_new = [jnp.sum(q_heads[hd].astype(F32) * kb, axis=-1, keepdims=True) for hd in range(N_Q_HEADS)]

    rr = lax.broadcasted_iota(jnp.int32, (N_Q_HEADS * seq_chunk, seq_chunk * WINDOW), 0)
    cc = lax.broadcasted_iota(jnp.int32, (N_Q_HEADS * seq_chunk, seq_chunk * WINDOW), 1)
    same_seq = (rr % seq_chunk) == (cc // WINDOW)
    kv_lo = lax.broadcasted_iota(jnp.int32, (seq_chunk, LANES), 1) < HEAD_DIM

    for c in range(n_seq // seq_chunk):
        sr = slice(c * seq_chunk, (c + 1) * seq_chunk)
        kc = ck_ref[sr].reshape(seq_chunk * WINDOW, KV_WIDTH).astype(BF16)
        vc = cv_ref[sr].reshape(seq_chunk * WINDOW, KV_WIDTH).astype(BF16)
        qs = jnp.concatenate([q_heads[hd][sr] for hd in range(N_Q_HEADS)], axis=0)
        s = jnp.where(same_seq, _dot_nt(qs, kc), NEG_INF)
        sn = jnp.concatenate([s_new[hd][sr] for hd in range(N_Q_HEADS)], axis=0)
        sink = jnp.concatenate([jnp.full((seq_chunk, 1), sinks_ref[hd], F32) for hd in range(N_Q_HEADS)], axis=0)
        m = jnp.maximum(jnp.maximum(jnp.max(s, axis=-1, keepdims=True), sn), sink)
        e = jnp.exp(s - m)
        en = jnp.exp(sn - m)
        inv = 1.0 / (jnp.sum(e, axis=-1, keepdims=True) + en + jnp.exp(sink - m))
        o = _dot((e * inv).astype(BF16), vc)
        pn = (en * inv).astype(BF16).astype(F32)
        for mcol in range(N_Q_HEADS // 2):
            halves = []
            for hd in (2 * mcol, 2 * mcol + 1):
                oh = o[hd * seq_chunk:(hd + 1) * seq_chunk] + pn[hd * seq_chunk:(hd + 1) * seq_chunk] * vb[sr]
                if (hd % 2) != (hd // (N_Q_HEADS // N_KV_HEADS)):
                    oh = pltpu.roll(oh, HEAD_DIM, 1)
                halves.append(oh)
            att = jnp.where(kv_lo, halves[0], halves[1])
            mix_ref[sr, mcol * LANES:(mcol + 1) * LANES] = att.astype(BF16)

    u = _gelu(z[:, C_U:C_U + GMLP_WIDTH])
    vn = _layernorm(_gelu(z[:, C_VG:C_VG + GMLP_WIDTH]), lng_ref[...], lnb_ref[...])
    vnout_ref[...] = vn
    sp = ws0_ref[...].astype(BF16).astype(F32) * vn.astype(BF16).astype(F32) + bs0_ref[...]
    mix_ref[:, ATTN_WIDTH:] = (u * sp).astype(BF16)

    xmid = x + _dot(mix_ref[...], wout_ref[...])
    xmid_ref[...] = xmid
    h2b = _rmsnorm(xmid, g2_ref[...]).astype(BF16)
    h2_ref[...] = h2b
    meta_ref[...] = _route(h2b, wrt_ref, brt_ref)


def _dense_moe_kernel(t_ref, xmid_ref, comb_ref, wg_ref, wu_ref, wd_ref, gf_ref, y_ref, acc_ref):
    e = pl.program_id(1)

    @pl.when(e == 0)
    def _():
        acc_ref[...] = xmid_ref[...]

    tb = t_ref[...]
    gate = _dot(tb, wg_ref[...].astype(BF16))
    up = _dot(tb, wu_ref[...].astype(BF16))
    hid = (gate * (1.0 / (1.0 + jnp.exp(-gate))) * up).astype(BF16)
    out = _dot(hid, wd_ref[...].astype(BF16))
    lane = lax.broadcasted_iota(jnp.int32, comb_ref.shape, 1)
    c_e = jnp.sum(jnp.where(lane == e, comb_ref[...], 0.0), axis=-1, keepdims=True)
    acc_ref[...] += c_e * out

    @pl.when(e == pl.num_programs(1) - 1)
    def _():
        y_ref[...] = _rmsnorm(acc_ref[...], gf_ref[...])


def _grouped_ffn_kernel(texp_ref, gsrc_ref, sdst_ref, ntiles_ref, xl_hbm, wg_ref, wu_ref, wd_ref, yl_hbm,
                        xbuf, ybuf, zbuf, wg_b, wu_b, wd_b, gsem, ssem, tsem):
    j = pl.program_id(0)
    n_tiles = ntiles_ref[0]
    spare_g0 = yl_hbm.shape[0] - GRANULES_PER_TILE

    def gather_copy(tile, s, b):
        return pltpu.make_async_copy(xl_hbm.at[gsrc_ref[tile * FFN_SLOTS + s]], xbuf.at[b, s], gsem.at[b])

    def scatter_copy(tile, s, b):
        return pltpu.make_async_copy(ybuf.at[b, s], yl_hbm.at[sdst_ref[tile * FFN_SLOTS + s]], ssem.at[b])

    def spare_init_copy(c):
        return pltpu.make_async_copy(zbuf, yl_hbm.at[pl.ds(spare_g0 + c * FFN_SLOTS, FFN_SLOTS)], tsem.at[0])

    @pl.when(j < n_tiles)
    def _():
        b = j % 2
        gb = j % GATHER_BUFS
        ahead = GATHER_BUFS - 1
        nxt = jnp.minimum(j + ahead, n_tiles - 1)
        nxt_b = (j + ahead) % GATHER_BUFS

        @pl.when(j == 0)
        def _():
            zbuf[...] = jnp.zeros_like(zbuf)
            for c in range(SORT_ROWS // FFN_ROWS):
                spare_init_copy(c).start()
            for k in range(ahead):
                for s in range(FFN_SLOTS):
                    gather_copy(jnp.minimum(k, n_tiles - 1), s, k).start()
            for c in range(SORT_ROWS // FFN_ROWS):
                spare_init_copy(c).wait()

        for s in range(FFN_SLOTS):
            gather_copy(j, s, gb).wait()

        @pl.when(j >= 2)
        def _():
            for s in range(FFN_SLOTS):
                scatter_copy(j - 2, s, b).wait()

        @pl.when(jnp.logical_or(j == 0, texp_ref[j] != texp_ref[jnp.maximum(j - 1, 0)]))
        def _():
            wg_b[...] = wg_ref[...].astype(BF16)
            wu_b[...] = wu_ref[...].astype(BF16)
            wd_b[...] = wd_ref[...].astype(BF16)

        x = xbuf[gb].reshape(FFN_ROWS, D_MODEL)
        gate = _dot(x, wg_b[...])
        up = _dot(x, wu_b[...])
        for s in range(FFN_SLOTS):
            gather_copy(nxt, s, nxt_b).start()
        hid = (gate * (1.0 / (1.0 + jnp.exp(-gate))) * up).astype(BF16)
        ybuf[b] = _dot(hid, wd_b[...]).astype(BF16).reshape(FFN_SLOTS, ROW_GRANULE, D_MODEL)
        for s in range(FFN_SLOTS):
            scatter_copy(j, s, b).start()

        @pl.when(j == n_tiles - 1)
        def _():
            for k in range(1, GATHER_BUFS):
                for s in range(FFN_SLOTS):
                    gather_copy(nxt, s, (j + k) % GATHER_BUFS).wait()
            for s in range(FFN_SLOTS):
                scatter_copy(j, s, b).wait()

            @pl.when(j >= 1)
            def _():
                for s in range(FFN_SLOTS):
                    scatter_copy(j - 1, s, 1 - b).wait()


def _combine_kernel(xmid_ref, yl_ref, meta_ref, gf_ref, y_ref):
    n_tok = xmid_ref.shape[0]
    meta = meta_ref[...]
    meta_t = jnp.concatenate([meta, jnp.zeros((LANES - 8, n_tok), F32)], axis=0).T
    d1, d2, w1, w2 = meta_t[:, 0:1], meta_t[:, 1:2], meta_t[:, 2:3], meta_t[:, 3:4]
    acc = xmid_ref[...]
    for c in range(SORT_ROWS // SORT_CHUNK):
        r_iota = (lax.broadcasted_iota(jnp.int32, (n_tok, SORT_CHUNK), 1) + c * SORT_CHUNK).astype(F32)
        unsort = jnp.where(r_iota == d1, w1, jnp.where(r_iota == d2, w2, 0.0)).astype(BF16)
        acc = acc + _dot(unsort, yl_ref[c * SORT_CHUNK:(c + 1) * SORT_CHUNK, :])
    y_ref[...] = _rmsnorm(acc, gf_ref[...])


def _ffn_schedule(tab):
    n_tok_tiles = tab.shape[0]
    strips = (tab[:, :, 0] * (1.0 / ROW_GRANULE)).astype(jnp.int32)
    starts = (tab[:, :, 1] * (1.0 / ROW_GRANULE)).astype(jnp.int32)
    used = jnp.sum(strips, axis=1)
    cnt = jnp.concatenate([strips.T, (GRANULES_PER_TILE - used)[None, :]], axis=0)
    row0 = jnp.concatenate([starts.T, used[None, :]], axis=0)
    cs = jnp.cumsum(cnt, axis=1) - cnt
    n_str = jnp.sum(cnt, axis=1)
    np_str = (n_str + FFN_SLOTS - 1) // FFN_SLOTS * FFN_SLOTS
    ends = jnp.cumsum(np_str)
    base = ends - np_str
    n_steps_max = _ffn_steps_max(n_tok_tiles)
    step0 = jnp.arange(n_steps_max, dtype=jnp.int32) * FFN_SLOTS
    stream = jnp.minimum(jnp.sum(ends[None, :] <= step0[:, None], axis=1), N_STREAMS - 1)
    pick = stream[:, None] == jnp.arange(N_STREAMS, dtype=jnp.int32)[None, :]
    sel = lambda x: jnp.sum(jnp.where(pick[:, :, None], x[None], 0), axis=1)
    cs_j, cnt_j, row0_j = sel(cs), sel(cnt), sel(row0)
    base_j = jnp.sum(jnp.where(pick, base[None, :], 0), axis=1)
    q = step0[:, None] + jnp.arange(FFN_SLOTS, dtype=jnp.int32)[None, :] - base_j[:, None]
    reached = cs_j[:, None, :] <= q[:, :, None]
    last = lambda x: jnp.sum(jnp.where(reached, jnp.diff(x, axis=1, prepend=0)[:, None, :], 0), axis=2)
    tile_idx = jnp.sum(reached, axis=2).astype(jnp.int32) - 1
    g = q - last(cs_j)
    valid = g < last(cnt_j)
    granule = tile_idx * GRANULES_PER_TILE + last(row0_j) + g
    slot = jnp.arange(FFN_SLOTS, dtype=jnp.int32)[None, :]
    spare = n_tok_tiles * GRANULES_PER_TILE + (jnp.arange(n_steps_max, dtype=jnp.int32)[:, None] % 2) * FFN_SLOTS + slot
    gsrc = jnp.where(valid, granule, slot).astype(jnp.int32)
    sdst = jnp.where(valid, granule, spare).astype(jnp.int32)
    texp = jnp.minimum(stream, N_EXPERTS - 1).astype(jnp.int32)
    n_steps = (ends[-1] // FFN_SLOTS).astype(jnp.int32).reshape(1)
    return texp, gsrc.reshape(-1), sdst.reshape(-1), n_steps


N_STREAMS = N_EXPERTS + 1


def _ffn_steps_max(n_tok_tiles):
    return -(-(n_tok_tiles * GRANULES_PER_TILE + N_STREAMS * (FFN_SLOTS - 1)) // FFN_SLOTS)


def _vmem_limit(n_bytes):
    return int(min(n_bytes, V7X_VMEM_BYTES - 4 * 1024 * 1024))


def _rope_tables(pos):
    inv_freq = ROPE_THETA ** (-jnp.arange(HALF, dtype=F32) * 2.0 / HEAD_DIM)
    ang = pos.astype(F32)[:, None] * inv_freq[None, :]
    cos, sin = jnp.cos(ang), jnp.sin(ang)
    reps = LANES // HEAD_DIM
    return jnp.tile(jnp.concatenate([cos, cos], axis=1), (1, reps)), jnp.tile(jnp.concatenate([-sin, sin], axis=1), (1, reps))


def _combine_matrix(meta):
    m = jnp.swapaxes(meta, 1, 2).reshape(-1, 8)
    ids = m[:, 0:2].astype(jnp.int32)
    return jnp.sum(jax.nn.one_hot(ids, N_EXPERTS, dtype=F32) * m[:, 2:4, None], axis=1)


def _dense_moe(t, xmid, comb, wg, wu, wd, gf, tile):
    n = t.shape[0]
    comb = jnp.pad(comb, ((0, 0), (0, LANES - N_EXPERTS)))
    return pl.pallas_call(
        _dense_moe_kernel,
        grid=(n // tile, N_EXPERTS),
        in_specs=[
            pl.BlockSpec((tile, D_MODEL), lambda i, e: (i, 0)),
            pl.BlockSpec((tile, D_MODEL), lambda i, e: (i, 0)),
            pl.BlockSpec((tile, LANES), lambda i, e: (i, 0)),
            pl.BlockSpec((None, D_MODEL, D_EXPERT), lambda i, e: (e, 0, 0)),
            pl.BlockSpec((None, D_MODEL, D_EXPERT), lambda i, e: (e, 0, 0)),
            pl.BlockSpec((None, D_EXPERT, D_MODEL), lambda i, e: (e, 0, 0)),
            pl.BlockSpec((1, D_MODEL), lambda i, e: (0, 0)),
        ],
        out_specs=pl.BlockSpec((tile, D_MODEL), lambda i, e: (i, 0)),
        out_shape=jax.ShapeDtypeStruct((n, D_MODEL), F32),
        scratch_shapes=[pltpu.VMEM((tile, D_MODEL), F32)],
        compiler_params=pltpu.CompilerParams(
            dimension_semantics=("arbitrary", "arbitrary"),
            vmem_limit_bytes=_vmem_limit(48 * 1024 * 1024)),
        name="dense_moe",
    )(t, xmid, comb, wg, wu, wd, gf)


def kernel(x_prompt, x_sample, cache_swa_k, cache_swa_v, norm_mix_g, w_in, attn_sinks, gmlp_ln_g, gmlp_ln_b,
           gmlp_w_s, gmlp_b_s, w_out, norm_ffn_g, router_group_w, router_group_b, router_expert_w,
           router_expert_b, expert_w_gate, expert_w_up, expert_w_down, final_norm_g):
    assert norm_mix_g.shape[0] == 1, "single-layer trunk"
    batch, seq, _ = x_prompt.shape
    dec_batch = x_sample.shape[0]
    assert x_sample.shape[1] == 1 and seq % TOK_TILE == 0

    win_ext = w_in[0].astype(BF16)
    wout = w_out[0].astype(BF16)
    g1 = norm_mix_g[0][None, :]
    g2 = norm_ffn_g[0][None, :]
    gf = final_norm_g[None, :]
    lng = gmlp_ln_g[0][None, :]
    lnb = gmlp_ln_b[0][None, :]
    sinks = attn_sinks[0]
    ws = gmlp_w_s[0]
    group_dim = GMLP_WIDTH // GMLP_GROUPS
    bsf = jnp.repeat(gmlp_b_s[0].T, group_dim, axis=1)
    ws0 = jnp.repeat(ws[:, 0, 0], group_dim)[None, :]
    bs0 = jnp.repeat(gmlp_b_s[0][:, 0], group_dim)[None, :]
    wrt = jnp.zeros((ROUTER_ROWS, D_MODEL), F32)
    wrt = wrt.at[0:N_GROUPS].set(router_group_w[0].T)
    wrt = wrt.at[8:8 + N_EXPERTS].set(router_expert_w[0].reshape(D_MODEL, N_EXPERTS).T).astype(BF16)
    brt = jnp.full((ROUTER_ROWS, 1), NEG_INF, F32)
    brt = brt.at[0:N_GROUPS, 0].set(router_group_b[0])
    brt = brt.at[8:8 + N_EXPERTS, 0].set(router_expert_b[0].reshape(N_EXPERTS))
    wg, wu, wd = expert_w_gate[0], expert_w_up[0], expert_w_down[0]
    cos_p, sin_p = _rope_tables(jnp.arange(seq, dtype=jnp.int32))
    cos_s, sin_s = _rope_tables(PAST_LEN + jnp.arange(1, dtype=jnp.int32))

    full = lambda shape: pl.BlockSpec(shape, lambda *_: (0,) * len(shape))
    smem = pl.BlockSpec(memory_space=pltpu.SMEM)
    n_tiles = seq // TOK_TILE

    upper = jnp.triu(jnp.ones((TOK_TILE, TOK_TILE), BF16), k=1)
    lpad = (jnp.arange(LANES)[None, :] < jnp.arange(N_EXPERTS)[:, None]).astype(BF16)
    n_tok_tiles = batch * n_tiles

    x2d = x_prompt.reshape(batch * seq, D_MODEL)
    cur = lambda s: jnp.maximum(s - 1, 0)
    xmid_p, xl_p, meta_p, tab_p, k_p, v_p = pl.pallas_call(
        functools.partial(_prompt_mixer_kernel, tiles_per_seq=n_tiles),
        grid=(n_tok_tiles + 1,),
        in_specs=[
            smem,
            pl.BlockSpec((TOK_TILE, D_MODEL), lambda s: (jnp.minimum(s, n_tok_tiles - 1), 0)),
            pl.BlockSpec((TOK_TILE, D_MODEL), lambda s: (cur(s), 0)),
            full((1, D_MODEL)),
            full((D_MODEL, IN_WIDTH)),
            pl.BlockSpec((TOK_TILE, LANES), lambda s: (cur(s) % n_tiles, 0)),
            pl.BlockSpec((TOK_TILE, LANES), lambda s: (cur(s) % n_tiles, 0)),
            full((1, GMLP_WIDTH)),
            full((1, GMLP_WIDTH)),
            full((GMLP_GROUPS, CHUNK, CHUNK)),
            full((CHUNK, GMLP_WIDTH)),
            full((D_MODEL, D_MODEL)),
            full((1, D_MODEL)),
            full((ROUTER_ROWS, D_MODEL)),
            full((ROUTER_ROWS, 1)),
            full((TOK_TILE, TOK_TILE)),
            full((N_EXPERTS, LANES)),
        ],
        out_specs=[
            pl.BlockSpec((TOK_TILE, D_MODEL), lambda s: (cur(s), 0)),
            pl.BlockSpec((None, SORT_ROWS, D_MODEL), lambda s: (cur(s), 0, 0)),
            pl.BlockSpec((None, 8, TOK_TILE), lambda s: (cur(s), 0, 0)),
            pl.BlockSpec((None, N_EXPERTS, LANES), lambda s: (cur(s), 0, 0)),
            pl.BlockSpec((None, WINDOW, KV_WIDTH), lambda s: (cur(s) // n_tiles, 0, 0)),
            pl.BlockSpec((None, WINDOW, KV_WIDTH), lambda s: (cur(s) // n_tiles, 0, 0)),
        ],
        out_shape=[
            jax.ShapeDtypeStruct((batch * seq, D_MODEL), F32),
            jax.ShapeDtypeStruct((n_tok_tiles, SORT_ROWS, D_MODEL), BF16),
            jax.ShapeDtypeStruct((n_tok_tiles, 8, TOK_TILE), F32),
            jax.ShapeDtypeStruct((n_tok_tiles, N_EXPERTS, LANES), F32),
            jax.ShapeDtypeStruct((batch, WINDOW, KV_WIDTH), F32),
            jax.ShapeDtypeStruct((batch, WINDOW, KV_WIDTH), F32),
        ],
        scratch_shapes=[
            pltpu.VMEM((TOK_TILE, IN_WIDTH), F32),
            pltpu.VMEM((TOK_TILE, IN_WIDTH), F32),
            pltpu.VMEM((TOK_TILE, D_MODEL), BF16),
        ] + [pltpu.VMEM((WINDOW + TOK_TILE, KV_WIDTH), BF16)] * 4 + [
            pltpu.VMEM((len(_NAT_HEADS), TOK_TILE, LANES), BF16),
            pltpu.VMEM((len(_SWP_HEADS), TOK_TILE, LANES), BF16),
            pltpu.VMEM((2 * BLOCKS_PER_TILE, len(_NAT_HEADS) * WINDOW, 2 * WINDOW), F32),
            pltpu.VMEM((2 * BLOCKS_PER_TILE, len(_NAT_HEADS) * WINDOW, 2 * WINDOW), BF16),
            pltpu.VMEM((TOK_TILE, GMLP_WIDTH), BF16),
        ],
        compiler_params=pltpu.CompilerParams(
            dimension_semantics=("arbitrary",),
            vmem_limit_bytes=_vmem_limit(56 * 1024 * 1024)),
        name="prompt_mixer",
    )(sinks, x2d, x2d, g1, win_ext, cos_p, sin_p, lng, lnb, ws, bsf, wout, g2, wrt, brt, upper, lpad)

    xs = x_sample.reshape(dec_batch, D_MODEL)
    ck = cache_swa_k[0].reshape(dec_batch, WINDOW, KV_WIDTH)
    cv = cache_swa_v[0].reshape(dec_batch, WINDOW, KV_WIDTH)
    xmid_s, h2_s, meta_s, k_s, v_s, vn_s = pl.pallas_call(
        _sample_mixer_kernel,
        in_specs=[smem] + [pl.BlockSpec(memory_space=pltpu.VMEM)] * 15,
        out_shape=[
            jax.ShapeDtypeStruct((dec_batch, D_MODEL), F32),
            jax.ShapeDtypeStruct((dec_batch, D_MODEL), BF16),
            jax.ShapeDtypeStruct((8, dec_batch), F32),
            jax.ShapeDtypeStruct((dec_batch, KV_WIDTH), F32),
            jax.ShapeDtypeStruct((dec_batch, KV_WIDTH), F32),
            jax.ShapeDtypeStruct((dec_batch, GMLP_WIDTH), F32),
        ],
        scratch_shapes=[pltpu.VMEM((dec_batch, D_MODEL), BF16)],
        compiler_params=pltpu.CompilerParams(vmem_limit_bytes=_vmem_limit(56 * 1024 * 1024)),
        name="sample_mixer",
    )(sinks, xs, ck, cv, g1, win_ext, cos_s, sin_s, lng, lnb, ws0, bs0, wout, g2, wrt, brt)

    texp, gsrc, sdst, n_steps = _ffn_schedule(tab_p)
    yl_p = pl.pallas_call(
        _grouped_ffn_kernel,
        grid_spec=pltpu.PrefetchScalarGridSpec(
            num_scalar_prefetch=4,
            grid=(_ffn_steps_max(n_tok_tiles),),
            in_specs=[
                pl.BlockSpec(memory_space=pl.ANY),
                pl.BlockSpec((None, D_MODEL, D_EXPERT), lambda j, te, gs, sd, ns: (te[j], 0, 0)),
                pl.BlockSpec((None, D_MODEL, D_EXPERT), lambda j, te, gs, sd, ns: (te[j], 0, 0)),
                pl.BlockSpec((None, D_EXPERT, D_MODEL), lambda j, te, gs, sd, ns: (te[j], 0, 0)),
            ],
            out_specs=pl.BlockSpec(memory_space=pl.ANY),
            scratch_shapes=[
                pltpu.VMEM((GATHER_BUFS, FFN_SLOTS, ROW_GRANULE, D_MODEL), BF16),
                pltpu.VMEM((2, FFN_SLOTS, ROW_GRANULE, D_MODEL), BF16),
                pltpu.VMEM((FFN_SLOTS, ROW_GRANULE, D_MODEL), BF16),
                pltpu.VMEM((D_MODEL, D_EXPERT), BF16),
                pltpu.VMEM((D_MODEL, D_EXPERT), BF16),
                pltpu.VMEM((D_EXPERT, D_MODEL), BF16),
                pltpu.SemaphoreType.DMA((GATHER_BUFS,)),
                pltpu.SemaphoreType.DMA((2,)),
                pltpu.SemaphoreType.DMA((1,)),
            ],
        ),
        out_shape=jax.ShapeDtypeStruct(((n_tok_tiles + 1) * GRANULES_PER_TILE, ROW_GRANULE, D_MODEL), BF16),
        compiler_params=pltpu.CompilerParams(
            dimension_semantics=("arbitrary",),
            vmem_limit_bytes=_vmem_limit(32 * 1024 * 1024)),
        name="grouped_ffn",
    )(texp, gsrc, sdst, n_steps, xl_p.reshape(n_tok_tiles * GRANULES_PER_TILE, ROW_GRANULE, D_MODEL), wg, wu, wd)
    yl_p = yl_p.reshape((n_tok_tiles + 1) * SORT_ROWS, D_MODEL)

    y_p = pl.pallas_call(
        _combine_kernel,
        grid=(n_tok_tiles,),
        in_specs=[
            pl.BlockSpec((TOK_TILE, D_MODEL), lambda i: (i, 0)),
            pl.BlockSpec((SORT_ROWS, D_MODEL), lambda i: (i, 0)),
            pl.BlockSpec((None, 8, TOK_TILE), lambda i: (i, 0, 0)),
            pl.BlockSpec((1, D_MODEL), lambda i: (0, 0)),
        ],
        out_specs=pl.BlockSpec((TOK_TILE, D_MODEL), lambda i: (i, 0)),
        out_shape=jax.ShapeDtypeStruct((batch * seq, D_MODEL), F32),
        compiler_params=pltpu.CompilerParams(
            dimension_semantics=("arbitrary",),
            vmem_limit_bytes=_vmem_limit(40 * 1024 * 1024)),
        name="moe_combine",
    )(xmid_p, yl_p, meta_p, gf)

    comb_s = _combine_matrix(meta_s[None])
    y_s = _dense_moe(h2_s, xmid_s, comb_s, wg, wu, wd, gf, dec_batch)

    return (y_p.reshape(batch, seq, D_MODEL),
            y_s.reshape(dec_batch, 1, D_MODEL),
            k_p.reshape(1, batch, WINDOW, N_KV_HEADS, HEAD_DIM),
            v_p.reshape(1, batch, WINDOW, N_KV_HEADS, HEAD_DIM),
            k_s.reshape(1, dec_batch, 1, N_KV_HEADS, HEAD_DIM),
            v_s.reshape(1, dec_batch, 1, N_KV_HEADS, HEAD_DIM),
            vn_s.reshape(1, dec_batch, 1, GMLP_WIDTH))
```

```python
import functools

import jax
import jax.numpy as jnp
import numpy as np
from jax import lax
from jax.experimental import pallas as pl
from jax.experimental.pallas import tpu as pltpu

F32 = jnp.float32
BF16 = jnp.bfloat16

D_MODEL = 1024
HEAD_DIM = 64
HALF = HEAD_DIM // 2
N_Q_HEADS = 8
N_KV_HEADS = 2
ATTN_WIDTH = N_Q_HEADS * HEAD_DIM
KV_WIDTH = N_KV_HEADS * HEAD_DIM
WINDOW = 128
ROPE_THETA = 10000.0
GMLP_WIDTH = D_MODEL - ATTN_WIDTH
GMLP_GROUPS = 8
CHUNK = 128
N_GROUPS = 4
EXPERTS_PER_GROUP = 8
N_EXPERTS = N_GROUPS * EXPERTS_PER_GROUP
D_EXPERT = 256
EPS = 1e-6
NEG_INF = -1e30
PAST_LEN = 16384

LANES = 128
V7X_VMEM_BYTES = 64 * 1024 * 1024

C_Q = 0
C_K = C_Q + ATTN_WIDTH
C_V = C_K + KV_WIDTH
C_U = C_V + KV_WIDTH
C_VG = C_U + GMLP_WIDTH
IN_WIDTH = C_VG + GMLP_WIDTH

ROUTER_ROWS = 48
TOK_TILE = 512
BLOCKS_PER_TILE = TOK_TILE // WINDOW
ROW_GRANULE = 16
SORT_CHUNK = 512
SORT_ROWS = -(-(2 * TOK_TILE + N_EXPERTS * (ROW_GRANULE - 1)) // SORT_CHUNK) * SORT_CHUNK
GRANULES_PER_TILE = SORT_ROWS // ROW_GRANULE
FFN_ROWS = 512
FFN_SLOTS = FFN_ROWS // ROW_GRANULE
GATHER_BUFS = 3


def _dot(a, b):
    return jnp.dot(a, b, preferred_element_type=F32)


def _dot_nt(a, b):
    return lax.dot_general(a, b, (((1,), (1,)), ((), ())), preferred_element_type=F32)


def _gelu(x):
    return 0.5 * x * (1.0 + lax.erf(x * np.float32(np.sqrt(0.5))))


def _rmsnorm(x, g):
    return x * lax.rsqrt(jnp.mean(x * x, axis=-1, keepdims=True) + EPS) * g


def _layernorm(x, g, b):
    mu = jnp.mean(x, axis=-1, keepdims=True)
    xc = x - mu
    return xc * lax.rsqrt(jnp.mean(xc * xc, axis=-1, keepdims=True) + EPS) * g + b


def _first_argmax_rows(x, row_iota, n_rows):
    m = jnp.max(x, axis=0, keepdims=True)
    idx = jnp.min(jnp.where(x == m, row_iota, n_rows), axis=0, keepdims=True)
    return m, idx


def _route(h2b, wrt_ref, brt_ref):
    n_tok = h2b.shape[0]
    lt = _dot_nt(wrt_ref[...], h2b) + brt_ref[...]
    row8 = lax.broadcasted_iota(jnp.int32, (8, n_tok), 0).astype(F32)
    glog = lt[0:8]
    gmax, gidx = _first_argmax_rows(glog, row8, 8)
    g_w = 1.0 / jnp.sum(jnp.exp(glog - gmax), axis=0, keepdims=True)
    esel = lt[8:16]
    for g in range(1, N_GROUPS):
        esel = jnp.where(gidx == g, lt[8 + 8 * g:16 + 8 * g], esel)
    m1, i1 = _first_argmax_rows(esel, row8, 8)
    esel2 = jnp.where(row8 == i1, -jnp.inf, esel)
    m2, i2 = _first_argmax_rows(esel2, row8, 8)
    r = jnp.exp(m2 - m1)
    w1 = 1.0 / (1.0 + r)
    w2 = r / (1.0 + r)
    e1 = gidx * EXPERTS_PER_GROUP + i1
    e2 = gidx * EXPERTS_PER_GROUP + i2
    slab = jnp.where(row8 == 0, e1, 0.0)
    for r_idx, val in ((1, e2), (2, w1 * g_w), (3, w2 * g_w)):
        slab = jnp.where(row8 == r_idx, val, slab)
    return slab


def _softmax_with_sink(s, sink):
    m = jnp.maximum(jnp.max(s, axis=-1, keepdims=True), sink)
    e = jnp.exp(s - m)
    den = jnp.sum(e, axis=-1, keepdims=True) + jnp.exp(sink - m)
    return e * (1.0 / den)


def _rope(x, cos, sin_signed):
    first_half = (lax.broadcasted_iota(jnp.int32, x.shape, 1) & HALF) == 0
    partner = jnp.where(first_half, pltpu.roll(x, LANES - HALF, 1), pltpu.roll(x, HALF, 1))
    return x * cos + partner * sin_signed


_NAT_HEADS = (0, 2, 5, 7)
_SWP_HEADS = (1, 3, 4, 6)


def _prompt_mixer_kernel(sinks_ref, x_next_ref, x_ref, g1_ref, win_ref, cos_ref, sin_ref, lng_ref, lnb_ref, ws_ref,
                         bsf_ref, wout_ref, g2_ref, wrt_ref, brt_ref, upper_ref, lpad_ref,
                         xmid_ref, xl_ref, meta_ref, tab_ref, kout_ref, vout_ref,
                         z_a, z_b, mix_ref, k_n, k_s, v_n, v_s, q_nat, q_swp, s_ref, p_ref, vn_ref, h_ref, h2_ref,
                         *, tiles_per_seq):
    step = pl.program_id(0)
    t = jnp.maximum(step - 1, 0) % tiles_per_seq
    kv_bufs = (k_n, k_s, v_n, v_s)

    @pl.when(step == 0)
    def _():
        z_b[...] = jnp.zeros_like(z_b)

    @pl.when(t == 0)
    def _():
        for ref in kv_bufs:
            ref[0:WINDOW, :] = jnp.zeros((WINDOW, KV_WIDTH), BF16)

    args = (t, sinks_ref, x_next_ref, x_ref, g1_ref, win_ref, cos_ref, sin_ref, lng_ref, lnb_ref, ws_ref, bsf_ref,
            wout_ref, g2_ref, wrt_ref, brt_ref, upper_ref, lpad_ref, xmid_ref, xl_ref, meta_ref, tab_ref, kout_ref,
            vout_ref, kv_bufs, q_nat, q_swp, s_ref, p_ref, vn_ref, h_ref, h2_ref)

    @pl.when(step % 2 == 0)
    def _():
        _prompt_mixer_body(z_a, z_b, mix_ref, *args)

    @pl.when(step % 2 == 1)
    def _():
        _prompt_mixer_body(z_b, z_a, mix_ref, *args)


def _prompt_mixer_body(z_next, z_ref, mix_ref, t, sinks_ref, x_next_ref, x_ref, g1_ref, win_ref, cos_ref,
                       sin_ref, lng_ref, lnb_ref, ws_ref, bsf_ref, wout_ref, g2_ref, wrt_ref, brt_ref, upper_ref,
                       lpad_ref, xmid_ref, xl_ref, meta_ref, tab_ref, kout_ref, vout_ref, kv_bufs,
                       q_nat, q_swp, s_ref, p_ref, vn_ref, h_ref, h2_ref):
    k_n, k_s, v_n, v_s = kv_bufs
    cos = cos_ref[...]
    sin = sin_ref[...]
    lane = lax.broadcasted_iota(jnp.int32, (WINDOW, LANES), 1)
    lo = lane < HEAD_DIM
    row = lax.broadcasted_iota(jnp.int32, (WINDOW, WINDOW), 0)
    col = lax.broadcasted_iota(jnp.int32, (WINDOW, WINDOW), 1)
    mask_cur = col <= row
    mask_prev_band = col >= row
    mask_prev_first = jnp.logical_and(mask_prev_band, (jnp.zeros_like(row) + t) > 0)
    mask_band = jnp.concatenate([mask_prev_band, mask_cur], axis=1)
    mask_first = jnp.concatenate([mask_prev_first, mask_cur], axis=1)

    cq = cos * np.float32(HEAD_DIM ** -0.5)
    sq = sin * np.float32(HEAD_DIM ** -0.5)
    kf = _rope(z_ref[:, C_K:C_K + KV_WIDTH], cos, sin)
    vf = z_ref[:, C_V:C_V + KV_WIDTH]
    k_n[WINDOW:, :] = kf.astype(BF16)
    k_s[WINDOW:, :] = pltpu.roll(kf, HEAD_DIM, 1).astype(BF16)
    v_n[WINDOW:, :] = vf.astype(BF16)
    v_s[WINDOW:, :] = pltpu.roll(vf, HEAD_DIM, 1).astype(BF16)

    kout_ref[...] = kf[TOK_TILE - WINDOW:]
    vout_ref[...] = vf[TOK_TILE - WINDOW:]

    lo_t = lax.broadcasted_iota(jnp.int32, (TOK_TILE, LANES), 1) < HEAD_DIM
    for m in range(N_Q_HEADS // 2):
        qc = _rope(z_ref[:, C_Q + m * LANES:C_Q + (m + 1) * LANES], cq, sq)
        for hd, qh in ((2 * m, jnp.where(lo_t, qc, 0.0)), (2 * m + 1, jnp.where(lo_t, 0.0, qc))):
            if hd in _NAT_HEADS:
                q_nat[_NAT_HEADS.index(hd)] = qh.astype(BF16)
            else:
                q_swp[_SWP_HEADS.index(hd)] = qh.astype(BF16)

    stacks = ((q_nat, _NAT_HEADS, k_n, v_n), (q_swp, _SWP_HEADS, k_s, v_s))
    proj_cols = 2 * LANES

    def project_next(c):
        cols = slice(c * proj_cols, (c + 1) * proj_cols)
        z_next[:, cols] = _dot(h_ref[...], win_ref[:, cols])

    def scores(j):
        rows = slice(j * WINDOW, (j + 1) * WINDOW)
        keys = slice(j * WINDOW, (j + 2) * WINDOW)
        for si, (q_ref, _, k_buf, _) in enumerate(stacks):
            q_stack = jnp.concatenate([q_ref[i, rows, :] for i in range(len(_NAT_HEADS))], axis=0)
            s_ref[2 * j + si] = _dot_nt(q_stack, k_buf[keys, :])

    def softmax(j):
        mask = mask_first if j == 0 else mask_band
        for si, (_, heads, _, _) in enumerate(stacks):
            for i, hd in enumerate(heads):
                pr = slice(i * WINDOW, (i + 1) * WINDOW)
                p_ref[2 * j + si, pr, :] = _softmax_with_sink(
                    jnp.where(mask, s_ref[2 * j + si, pr, :], NEG_INF), sinks_ref[hd]).astype(BF16)

    def attend(j):
        rows = slice(j * WINDOW, (j + 1) * WINDOW)
        keys = slice(j * WINDOW, (j + 2) * WINDOW)
        o_nat = _dot(p_ref[2 * j], v_n[keys, :])
        o_swp = _dot(p_ref[2 * j + 1], v_s[keys, :])
        for m in range(N_Q_HEADS // 2):
            pr = slice(m * WINDOW, (m + 1) * WINDOW)
            even_nat = (2 * m) in _NAT_HEADS
            att = jnp.where(lo, o_nat[pr], o_swp[pr]) if even_nat else jnp.where(lo, o_swp[pr], o_nat[pr])
            mix_ref[rows, m * LANES:(m + 1) * LANES] = att.astype(BF16)

    def gmlp(m):
        cs = slice(m * LANES, (m + 1) * LANES)
        w0 = jnp.where(mask_cur, ws_ref[2 * m], 0.0).astype(BF16)
        w1 = jnp.where(mask_cur, ws_ref[2 * m + 1], 0.0).astype(BF16)
        wcat = jnp.concatenate([w0, w1], axis=1)
        for j in range(BLOCKS_PER_TILE):
            rows = slice(j * WINDOW, (j + 1) * WINDOW)
            vcol = vn_ref[rows, cs]
            rhs = jnp.concatenate([jnp.where(lo, vcol, jnp.zeros_like(vcol)),
                                   jnp.where(lo, jnp.zeros_like(vcol), vcol)], axis=0)
            sp = _dot(wcat, rhs) + bsf_ref[:, cs]
            u = _gelu(z_ref[rows, C_U + m * LANES:C_U + (m + 1) * LANES])
            mix_ref[rows, ATTN_WIDTH + m * LANES:ATTN_WIDTH + (m + 1) * LANES] = (u * sp).astype(BF16)

    def project_out(c):
        cols = slice(c * proj_cols, (c + 1) * proj_cols)
        xmid_ref[:, cols] = x_ref[:, cols] + _dot(mix_ref[...], wout_ref[:, cols])

    h_ref[...] = _rmsnorm(x_next_ref[...], g1_ref[...]).astype(BF16)
    project_next(0)
    project_next(1)
    for j in range(BLOCKS_PER_TILE):
        scores(j)
    vn_ref[...] = _layernorm(_gelu(z_ref[:, C_VG:C_VG + GMLP_WIDTH]), lng_ref[...], lnb_ref[...]).astype(BF16)
    project_next(2)
    softmax(0)
    project_next(3)
    softmax(1)
    project_next(4)
    softmax(2)
    project_next(5)
    softmax(3)
    project_next(6)
    for j in range(BLOCKS_PER_TILE):
        attend(j)
    for m in range(GMLP_GROUPS // 2):
        gmlp(m)
    for c in range(D_MODEL // proj_cols):
        project_out(c)
    h2_ref[...] = _rmsnorm(xmid_ref[...], g2_ref[...]).astype(BF16)
    d1, d2 = _sort_plan(_route(h2_ref[...], wrt_ref, brt_ref), upper_ref, lpad_ref, meta_ref, tab_ref)
    for c in range(SORT_ROWS // SORT_CHUNK):
        _sort_chunk(c, d1, d2, h2_ref[...], xl_ref)

    for ref in kv_bufs:
        ref[0:WINDOW, :] = ref[TOK_TILE:TOK_TILE + WINDOW, :]


def _sort_plan(slab, upper_ref, lpad_ref, meta_ref, tab_ref):
    n_tok = slab.shape[1]
    e1, e2 = slab[0:1], slab[1:2]
    row32 = lax.broadcasted_iota(jnp.int32, (N_EXPERTS, n_tok), 0).astype(F32)
    sel1 = row32 == e1
    sel2 = row32 == e2
    onehot = jnp.where(sel1, 1.0, jnp.where(sel2, 1.0, 0.0))
    earlier = _dot(onehot.astype(BF16), upper_ref[...])
    cnt = jnp.sum(onehot, axis=1, keepdims=True)
    pc = jnp.floor((cnt + (ROW_GRANULE - 1)) * (1.0 / ROW_GRANULE)) * ROW_GRANULE
    pc_b = jnp.broadcast_to(pc, (N_EXPERTS, LANES))
    pc_pad = jnp.concatenate([pc_b, jnp.zeros((LANES - N_EXPERTS, LANES), F32)], axis=0).astype(BF16)
    start = _dot(lpad_ref[...], pc_pad)
    base = start[:, 0:1] + earlier
    d1 = jnp.sum(jnp.where(sel1, base, 0.0), axis=0, keepdims=True)
    d2 = jnp.sum(jnp.where(sel2, base, 0.0), axis=0, keepdims=True)
    row8 = lax.broadcasted_iota(jnp.int32, (8, n_tok), 0)
    meta_ref[...] = jnp.where(row8 == 0, d1, jnp.where(row8 == 1, d2, jnp.where(row8 >= 4, 0.0, slab)))
    lane = lax.broadcasted_iota(jnp.int32, (N_EXPERTS, LANES), 1)
    tab_ref[...] = jnp.where(lane == 0, pc_b, jnp.where(lane == 1, start, 0.0))
    return d1, d2


def _sort_chunk(c, d1, d2, h2b, xl_ref):
    n_tok = h2b.shape[0]
    r_iota = (lax.broadcasted_iota(jnp.int32, (SORT_CHUNK, n_tok), 0) + c * SORT_CHUNK).astype(F32)
    perm = jnp.where(r_iota == d1, 1.0, jnp.where(r_iota == d2, 1.0, 0.0)).astype(BF16)
    xl_ref[c * SORT_CHUNK:(c + 1) * SORT_CHUNK, :] = _dot(perm, h2b).astype(BF16)


def _sample_mixer_kernel(sinks_ref, x_ref, ck_ref, cv_ref, g1_ref, win_ref, cos_ref, sin_ref, lng_ref, lnb_ref,
                         ws0_ref, bs0_ref, wout_ref, g2_ref, wrt_ref, brt_ref,
                         xmid_ref, h2_ref, meta_ref, kout_ref, vout_ref, vnout_ref,
                         mix_ref):
    n_seq = x_ref.shape[0]
    seq_chunk = 16
    x = x_ref[...]
    h = _rmsnorm(x, g1_ref[...]).astype(BF16)
    z = _dot(h, win_ref[...])
    cos = cos_ref[...]
    sin = sin_ref[...]
    scale = np.float32(HEAD_DIM ** -0.5)
    lane = lax.broadcasted_iota(jnp.int32, (n_seq, LANES), 1)
    lo = lane < HEAD_DIM
    kf = _rope(z[:, C_K:C_K + KV_WIDTH], cos, sin)
    vf = z[:, C_V:C_V + KV_WIDTH]
    kout_ref[...] = kf
    vout_ref[...] = vf
    kb = kf.astype(BF16).astype(F32)
    vb = vf.astype(BF16).astype(F32)

    q_heads = []
    for hd in range(N_Q_HEADS):
        m = hd // 2
        qc = _rope(z[:, C_Q + m * LANES:C_Q + (m + 1) * LANES], cos, sin) * scale
        keep = lo if hd % 2 == 0 else ~lo
        qm = jnp.where(keep, qc, 0.0)
        if (hd % 2) != (hd // (N_Q_HEADS // N_KV_HEADS)):
            qm = pltpu.roll(qm, HEAD_DIM, 1)
        q_heads.append(qm.astype(BF16))

    s_new = [jnp.sum(q_heads[hd].astype(F32) * kb, axis=-1, keepdims=True) for hd in range(N_Q_HEADS)]

    rr = lax.broadcasted_iota(jnp.int32, (N_Q_HEADS * seq_chunk, seq_chunk * WINDOW), 0)
    cc = lax.broadcasted_iota(jnp.int32, (N_Q_HEADS * seq_chunk, seq_chunk * WINDOW), 1)
    same_seq = (rr % seq_chunk) == (cc // WINDOW)
    kv_lo = lax.broadcasted_iota(jnp.int32, (seq_chunk, LANES), 1) < HEAD_DIM

    for c in range(n_seq // seq_chunk):
        sr = slice(c * seq_chunk, (c + 1) * seq_chunk)
        kc = ck_ref[sr].reshape(seq_chunk * WINDOW, KV_WIDTH).astype(BF16)
        vc = cv_ref[sr].reshape(seq_chunk * WINDOW, KV_WIDTH).astype(BF16)
        qs = jnp.concatenate([q_heads[hd][sr] for hd in range(N_Q_HEADS)], axis=0)
        s = jnp.where(same_seq, _dot_nt(qs, kc), NEG_INF)
        sn = jnp.concatenate([s_new[hd][sr] for hd in range(N_Q_HEADS)], axis=0)
        sink = jnp.concatenate([jnp.full((seq_chunk, 1), sinks_ref[hd], F32) for hd in range(N_Q_HEADS)], axis=0)
        m = jnp.maximum(jnp.maximum(jnp.max(s, axis=-1, keepdims=True), sn), sink)
        e = jnp.exp(s - m)
        en = jnp.exp(sn - m)
        inv = 1.0 / (jnp.sum(e, axis=-1, keepdims=True) + en + jnp.exp(sink - m))
        o = _dot((e * inv).astype(BF16), vc)
        pn = (en * inv).astype(BF16).astype(F32)
        for mcol in range(N_Q_HEADS // 2):
            halves = []
            for hd in (2 * mcol, 2 * mcol + 1):
                oh = o[hd * seq_chunk:(hd + 1) * seq_chunk] + pn[hd * seq_chunk:(hd + 1) * seq_chunk] * vb[sr]
                if (hd % 2) != (hd // (N_Q_HEADS // N_KV_HEADS)):
                    oh = pltpu.roll(oh, HEAD_DIM, 1)
                halves.append(oh)
            att = jnp.where(kv_lo, halves[0], halves[1])
            mix_ref[sr, mcol * LANES:(mcol + 1) * LANES] = att.astype(BF16)

    u = _gelu(z[:, C_U:C_U + GMLP_WIDTH])
    vn = _layernorm(_gelu(z[:, C_VG:C_VG + GMLP_WIDTH]), lng_ref[...], lnb_ref[...])
    vnout_ref[...] = vn
    sp = ws0_ref[...].astype(BF16).astype(F32) * vn.astype(BF16).astype(F32) + bs0_ref[...]
    mix_ref[:, ATTN_WIDTH:] = (u * sp).astype(BF16)

    xmid = x + _dot(mix_ref[...], wout_ref[...])
    xmid_ref[...] = xmid
    h2b = _rmsnorm(xmid, g2_ref[...]).astype(BF16)
    h2_ref[...] = h2b
    meta_ref[...] = _route(h2b, wrt_ref, brt_ref)


def _dense_moe_kernel(t_ref, xmid_ref, comb_ref, wg_ref, wu_ref, wd_ref, gf_ref, y_ref, acc_ref):
    e = pl.program_id(1)

    @pl.when(e == 0)
    def _():
        acc_ref[...] = xmid_ref[...]

    tb = t_ref[...]
    gate = _dot(tb, wg_ref[...].astype(BF16))
    up = _dot(tb, wu_ref[...].astype(BF16))
    hid = (gate * (1.0 / (1.0 + jnp.exp(-gate))) * up).astype(BF16)
    out = _dot(hid, wd_ref[...].astype(BF16))
    lane = lax.broadcasted_iota(jnp.int32, comb_ref.shape, 1)
    c_e = jnp.sum(jnp.where(lane == e, comb_ref[...], 0.0), axis=-1, keepdims=True)
    acc_ref[...] += c_e * out

    @pl.when(e == pl.num_programs(1) - 1)
    def _():
        y_ref[...] = _rmsnorm(acc_ref[...], gf_ref[...])


def _grouped_ffn_kernel(texp_ref, gsrc_ref, sdst_ref, ntiles_ref, xl_hbm, wg_ref, wu_ref, wd_ref, yl_hbm,
                        xbuf, ybuf, wg_b, wu_b, wd_b, gsem, ssem):
    j = pl.program_id(0)
    n_tiles = ntiles_ref[0]

    def gather_copy(tile, s, b):
        return pltpu.make_async_copy(xl_hbm.at[gsrc_ref[tile * FFN_SLOTS + s]], xbuf.at[b, s], gsem.at[b])

    def scatter_copy(tile, s, b):
        return pltpu.make_async_copy(ybuf.at[b, s], yl_hbm.at[sdst_ref[tile * FFN_SLOTS + s]], ssem.at[b])

    @pl.when(j < n_tiles)
    def _():
        b = j % 2
        gb = j % GATHER_BUFS
        ahead = GATHER_BUFS - 1
        nxt = j + ahead
        nxt_b = nxt % GATHER_BUFS

        @pl.when(j == 0)
        def _():
            for k in range(ahead):
                for s in range(FFN_SLOTS):
                    gather_copy(k, s, k).start()

        for s in range(FFN_SLOTS):
            gather_copy(j, s, gb).wait()

        @pl.when(j >= 2)
        def _():
            for s in range(FFN_SLOTS):
                scatter_copy(j - 2, s, b).wait()

        @pl.when(jnp.logical_or(j == 0, texp_ref[j] != texp_ref[jnp.maximum(j - 1, 0)]))
        def _():
            wg_b[...] = wg_ref[...].astype(BF16)
            wu_b[...] = wu_ref[...].astype(BF16)
            wd_b[...] = wd_ref[...].astype(BF16)

        x = xbuf[gb].reshape(FFN_ROWS, D_MODEL)
        gate = _dot(x, wg_b[...])
        up = _dot(x, wu_b[...])
        for s in range(FFN_SLOTS):
            gather_copy(nxt, s, nxt_b).start()
        hid = (gate * (1.0 / (1.0 + jnp.exp(-gate))) * up).astype(BF16)
        ybuf[b] = _dot(hid, wd_b[...]).astype(BF16).reshape(FFN_SLOTS, ROW_GRANULE, D_MODEL)
        for s in range(FFN_SLOTS):
            scatter_copy(j, s, b).start()

        @pl.when(j == n_tiles - 1)
        def _():
            for k in range(1, GATHER_BUFS):
                for s in range(FFN_SLOTS):
                    gather_copy(nxt, s, (j + k) % GATHER_BUFS).wait()
            for s in range(FFN_SLOTS):
                scatter_copy(j, s, b).wait()

            @pl.when(j >= 1)
            def _():
                for s in range(FFN_SLOTS):
                    scatter_copy(j - 1, s, 1 - b).wait()


def _combine_kernel(xmid_ref, yl_ref, meta_ref, gf_ref, y_ref):
    n_tok = xmid_ref.shape[0]
    meta = meta_ref[...]
    meta_t = jnp.concatenate([meta, jnp.zeros((LANES - 8, n_tok), F32)], axis=0).T
    d1, d2, w1, w2 = meta_t[:, 0:1], meta_t[:, 1:2], meta_t[:, 2:3], meta_t[:, 3:4]
    acc = xmid_ref[...]
    for c in range(SORT_ROWS // SORT_CHUNK):
        r_iota = (lax.broadcasted_iota(jnp.int32, (n_tok, SORT_CHUNK), 1) + c * SORT_CHUNK).astype(F32)
        unsort = jnp.where(r_iota == d1, w1, jnp.where(r_iota == d2, w2, 0.0)).astype(BF16)
        acc = acc + _dot(unsort, yl_ref[c * SORT_CHUNK:(c + 1) * SORT_CHUNK, :])
    y_ref[...] = _rmsnorm(acc, gf_ref[...])


def _ffn_schedule(tab):
    n_tok_tiles = tab.shape[0]
    strips = (tab[:, :, 0] * (1.0 / ROW_GRANULE)).astype(jnp.int32)
    starts = (tab[:, :, 1] * (1.0 / ROW_GRANULE)).astype(jnp.int32)
    cnt = strips.T
    row0 = starts.T
    cs = jnp.cumsum(cnt, axis=1) - cnt
    n_str = jnp.sum(cnt, axis=1)
    np_str = (n_str + FFN_SLOTS - 1) // FFN_SLOTS * FFN_SLOTS
    ends = jnp.cumsum(np_str)
    base = ends - np_str
    n_steps_max = _ffn_steps_max(n_tok_tiles) + GATHER_BUFS - 1
    step0 = jnp.arange(n_steps_max, dtype=jnp.int32) * FFN_SLOTS
    stream = jnp.minimum(jnp.sum(ends[None, :] <= step0[:, None], axis=1), N_EXPERTS - 1)
    pick = stream[:, None] == jnp.arange(N_EXPERTS, dtype=jnp.int32)[None, :]
    sel = lambda x: jnp.sum(jnp.where(pick[:, :, None], x[None], 0), axis=1)
    cs_j, cnt_j, row0_j = sel(cs), sel(cnt), sel(row0)
    base_j = jnp.sum(jnp.where(pick, base[None, :], 0), axis=1)
    q = step0[:, None] + jnp.arange(FFN_SLOTS, dtype=jnp.int32)[None, :] - base_j[:, None]
    reached = cs_j[:, None, :] <= q[:, :, None]
    last = lambda x: jnp.sum(jnp.where(reached, jnp.diff(x, axis=1, prepend=0)[:, None, :], 0), axis=2)
    tile_idx = jnp.sum(reached, axis=2).astype(jnp.int32) - 1
    g = q - last(cs_j)
    valid = g < last(cnt_j)
    granule = tile_idx * GRANULES_PER_TILE + last(row0_j) + g
    assert n_tok_tiles >= 2 * FFN_SLOTS
    slot = jnp.arange(FFN_SLOTS, dtype=jnp.int32)[None, :]
    parity = jnp.arange(n_steps_max, dtype=jnp.int32)[:, None] % 2
    gsrc = jnp.where(valid, granule, slot * GRANULES_PER_TILE + GRANULES_PER_TILE - 2).astype(jnp.int32)
    sdst = jnp.where(valid, granule, (slot + parity * FFN_SLOTS) * GRANULES_PER_TILE + GRANULES_PER_TILE - 1)
    texp = stream.astype(jnp.int32)
    n_steps = (ends[-1] // FFN_SLOTS).astype(jnp.int32).reshape(1)
    return texp, gsrc.reshape(-1), sdst.astype(jnp.int32).reshape(-1), n_steps


assert GRANULES_PER_TILE - (2 * TOK_TILE + N_EXPERTS * (ROW_GRANULE - 1)) // ROW_GRANULE >= 2


def _ffn_steps_max(n_tok_tiles):
    return -(-(n_tok_tiles * GRANULES_PER_TILE + N_EXPERTS * (FFN_SLOTS - 1)) // FFN_SLOTS)


def _vmem_limit(n_bytes):
    return int(min(n_bytes, V7X_VMEM_BYTES - 4 * 1024 * 1024))


def _rope_tables(pos):
    inv_freq = ROPE_THETA ** (-jnp.arange(HALF, dtype=F32) * 2.0 / HEAD_DIM)
    ang = pos.astype(F32)[:, None] * inv_freq[None, :]
    cos, sin = jnp.cos(ang), jnp.sin(ang)
    reps = LANES // HEAD_DIM
    return jnp.tile(jnp.concatenate([cos, cos], axis=1), (1, reps)), jnp.tile(jnp.concatenate([-sin, sin], axis=1), (1, reps))


def _combine_matrix(meta):
    m = jnp.swapaxes(meta, 1, 2).reshape(-1, 8)
    ids = m[:, 0:2].astype(jnp.int32)
    return jnp.sum(jax.nn.one_hot(ids, N_EXPERTS, dtype=F32) * m[:, 2:4, None], axis=1)


def _dense_moe(t, xmid, comb, wg, wu, wd, gf, tile):
    n = t.shape[0]
    comb = jnp.pad(comb, ((0, 0), (0, LANES - N_EXPERTS)))
    return pl.pallas_call(
        _dense_moe_kernel,
        grid=(n // tile, N_EXPERTS),
        in_specs=[
            pl.BlockSpec((tile, D_MODEL), lambda i, e: (i, 0)),
            pl.BlockSpec((tile, D_MODEL), lambda i, e: (i, 0)),
            pl.BlockSpec((tile, LANES), lambda i, e: (i, 0)),
            pl.BlockSpec((None, D_MODEL, D_EXPERT), lambda i, e: (e, 0, 0)),
            pl.BlockSpec((None, D_MODEL, D_EXPERT), lambda i, e: (e, 0, 0)),
            pl.BlockSpec((None, D_EXPERT, D_MODEL), lambda i, e: (e, 0, 0)),
            pl.BlockSpec((1, D_MODEL), lambda i, e: (0, 0)),
        ],
        out_specs=pl.BlockSpec((tile, D_MODEL), lambda i, e: (i, 0)),
        out_shape=jax.ShapeDtypeStruct((n, D_MODEL), F32),
        scratch_shapes=[pltpu.VMEM((tile, D_MODEL), F32)],
        compiler_params=pltpu.CompilerParams(
            dimension_semantics=("arbitrary", "arbitrary"),
            vmem_limit_bytes=_vmem_limit(48 * 1024 * 1024)),
        name="dense_moe",
    )(t, xmid, comb, wg, wu, wd, gf)


def kernel(x_prompt, x_sample, cache_swa_k, cache_swa_v, norm_mix_g, w_in, attn_sinks, gmlp_ln_g, gmlp_ln_b,
           gmlp_w_s, gmlp_b_s, w_out, norm_ffn_g, router_group_w, router_group_b, router_expert_w,
           router_expert_b, expert_w_gate, expert_w_up, expert_w_down, final_norm_g):
    assert norm_mix_g.shape[0] == 1, "single-layer trunk"
    batch, seq, _ = x_prompt.shape
    dec_batch = x_sample.shape[0]
    assert x_sample.shape[1] == 1 and seq % TOK_TILE == 0

    win_ext = w_in[0].astype(BF16)
    wout = w_out[0].astype(BF16)
    g1 = norm_mix_g[0][None, :]
    g2 = norm_ffn_g[0][None, :]
    gf = final_norm_g[None, :]
    lng = gmlp_ln_g[0][None, :]
    lnb = gmlp_ln_b[0][None, :]
    sinks = attn_sinks[0]
    ws = gmlp_w_s[0]
    group_dim = GMLP_WIDTH // GMLP_GROUPS
    bsf = jnp.repeat(gmlp_b_s[0].T, group_dim, axis=1)
    ws0 = jnp.repeat(ws[:, 0, 0], group_dim)[None, :]
    bs0 = jnp.repeat(gmlp_b_s[0][:, 0], group_dim)[None, :]
    wrt = jnp.zeros((ROUTER_ROWS, D_MODEL), F32)
    wrt = wrt.at[0:N_GROUPS].set(router_group_w[0].T)
    wrt = wrt.at[8:8 + N_EXPERTS].set(router_expert_w[0].reshape(D_MODEL, N_EXPERTS).T).astype(BF16)
    brt = jnp.full((ROUTER_ROWS, 1), NEG_INF, F32)
    brt = brt.at[0:N_GROUPS, 0].set(router_group_b[0])
    brt = brt.at[8:8 + N_EXPERTS, 0].set(router_expert_b[0].reshape(N_EXPERTS))
    wg, wu, wd = expert_w_gate[0], expert_w_up[0], expert_w_down[0]
    cos_p, sin_p = _rope_tables(jnp.arange(seq, dtype=jnp.int32))
    cos_s, sin_s = _rope_tables(PAST_LEN + jnp.arange(1, dtype=jnp.int32))

    full = lambda shape: pl.BlockSpec(shape, lambda *_: (0,) * len(shape))
    smem = pl.BlockSpec(memory_space=pltpu.SMEM)
    n_tiles = seq // TOK_TILE

    upper = jnp.triu(jnp.ones((TOK_TILE, TOK_TILE), BF16), k=1)
    lpad = (jnp.arange(LANES)[None, :] < jnp.arange(N_EXPERTS)[:, None]).astype(BF16)
    n_tok_tiles = batch * n_tiles

    x2d = x_prompt.reshape(batch * seq, D_MODEL)
    tile_a = lambda s: jnp.minimum(s, n_tok_tiles - 1)
    cur = lambda s: jnp.maximum(s - 1, 0)
    tile_b = cur
    xmid_p, xl_p, meta_p, tab_p, k_p, v_p = pl.pallas_call(
        functools.partial(_prompt_mixer_kernel, tiles_per_seq=n_tiles),
        grid=(n_tok_tiles + 1,),
        in_specs=[
            smem,
            pl.BlockSpec((TOK_TILE, D_MODEL), lambda s: (tile_a(s), 0)),
            pl.BlockSpec((TOK_TILE, D_MODEL), lambda s: (cur(s), 0)),
            full((1, D_MODEL)),
            full((D_MODEL, IN_WIDTH)),
            pl.BlockSpec((TOK_TILE, LANES), lambda s: (tile_b(s) % n_tiles, 0)),
            pl.BlockSpec((TOK_TILE, LANES), lambda s: (tile_b(s) % n_tiles, 0)),
            full((1, GMLP_WIDTH)),
            full((1, GMLP_WIDTH)),
            full((GMLP_GROUPS, CHUNK, CHUNK)),
            full((CHUNK, GMLP_WIDTH)),
            full((D_MODEL, D_MODEL)),
            full((1, D_MODEL)),
            full((ROUTER_ROWS, D_MODEL)),
            full((ROUTER_ROWS, 1)),
            full((TOK_TILE, TOK_TILE)),
            full((N_EXPERTS, LANES)),
        ],
        out_specs=[
            pl.BlockSpec((TOK_TILE, D_MODEL), lambda s: (cur(s), 0)),
            pl.BlockSpec((None, SORT_ROWS, D_MODEL), lambda s: (cur(s), 0, 0)),
            pl.BlockSpec((None, 8, TOK_TILE), lambda s: (cur(s), 0, 0)),
            pl.BlockSpec((None, N_EXPERTS, LANES), lambda s: (cur(s), 0, 0)),
            pl.BlockSpec((None, WINDOW, KV_WIDTH), lambda s: (tile_b(s) // n_tiles, 0, 0)),
            pl.BlockSpec((None, WINDOW, KV_WIDTH), lambda s: (tile_b(s) // n_tiles, 0, 0)),
        ],
        out_shape=[
            jax.ShapeDtypeStruct((batch * seq, D_MODEL), F32),
            jax.ShapeDtypeStruct((n_tok_tiles, SORT_ROWS, D_MODEL), BF16),
            jax.ShapeDtypeStruct((n_tok_tiles, 8, TOK_TILE), F32),
            jax.ShapeDtypeStruct((n_tok_tiles, N_EXPERTS, LANES), F32),
            jax.ShapeDtypeStruct((batch, WINDOW, KV_WIDTH), F32),
            jax.ShapeDtypeStruct((batch, WINDOW, KV_WIDTH), F32),
        ],
        scratch_shapes=[
            pltpu.VMEM((TOK_TILE, IN_WIDTH), F32),
            pltpu.VMEM((TOK_TILE, IN_WIDTH), F32),
            pltpu.VMEM((TOK_TILE, D_MODEL), BF16),
        ] + [pltpu.VMEM((WINDOW + TOK_TILE, KV_WIDTH), BF16)] * 4 + [
            pltpu.VMEM((len(_NAT_HEADS), TOK_TILE, LANES), BF16),
            pltpu.VMEM((len(_SWP_HEADS), TOK_TILE, LANES), BF16),
            pltpu.VMEM((2 * BLOCKS_PER_TILE, len(_NAT_HEADS) * WINDOW, 2 * WINDOW), F32),
            pltpu.VMEM((2 * BLOCKS_PER_TILE, len(_NAT_HEADS) * WINDOW, 2 * WINDOW), BF16),
            pltpu.VMEM((TOK_TILE, GMLP_WIDTH), BF16),
            pltpu.VMEM((TOK_TILE, D_MODEL), BF16),
            pltpu.VMEM((TOK_TILE, D_MODEL), BF16),
        ],
        compiler_params=pltpu.CompilerParams(
            dimension_semantics=("arbitrary",),
            vmem_limit_bytes=_vmem_limit(56 * 1024 * 1024)),
        name="prompt_mixer",
    )(sinks, x2d, x2d, g1, win_ext, cos_p, sin_p, lng, lnb, ws, bsf, wout, g2, wrt, brt, upper, lpad)

    xs = x_sample.reshape(dec_batch, D_MODEL)
    ck = cache_swa_k[0].reshape(dec_batch, WINDOW, KV_WIDTH)
    cv = cache_swa_v[0].reshape(dec_batch, WINDOW, KV_WIDTH)
    xmid_s, h2_s, meta_s, k_s, v_s, vn_s = pl.pallas_call(
        _sample_mixer_kernel,
        in_specs=[smem] + [pl.BlockSpec(memory_space=pltpu.VMEM)] * 15,
        out_shape=[
            jax.ShapeDtypeStruct((dec_batch, D_MODEL), F32),
            jax.ShapeDtypeStruct((dec_batch, D_MODEL), BF16),
            jax.ShapeDtypeStruct((8, dec_batch), F32),
            jax.ShapeDtypeStruct((dec_batch, KV_WIDTH), F32),
            jax.ShapeDtypeStruct((dec_batch, KV_WIDTH), F32),
            jax.ShapeDtypeStruct((dec_batch, GMLP_WIDTH), F32),
        ],
        scratch_shapes=[pltpu.VMEM((dec_batch, D_MODEL), BF16)],
        compiler_params=pltpu.CompilerParams(vmem_limit_bytes=_vmem_limit(56 * 1024 * 1024)),
        name="sample_mixer",
    )(sinks, xs, ck, cv, g1, win_ext, cos_s, sin_s, lng, lnb, ws0, bs0, wout, g2, wrt, brt)

    texp, gsrc, sdst, n_steps = _ffn_schedule(tab_p)
    yl_p = pl.pallas_call(
        _grouped_ffn_kernel,
        grid_spec=pltpu.PrefetchScalarGridSpec(
            num_scalar_prefetch=4,
            grid=(_ffn_steps_max(n_tok_tiles),),
            in_specs=[
                pl.BlockSpec(memory_space=pl.ANY),
                pl.BlockSpec((None, D_MODEL, D_EXPERT), lambda j, te, gs, sd, ns: (te[j], 0, 0)),
                pl.BlockSpec((None, D_MODEL, D_EXPERT), lambda j, te, gs, sd, ns: (te[j], 0, 0)),
                pl.BlockSpec((None, D_EXPERT, D_MODEL), lambda j, te, gs, sd, ns: (te[j], 0, 0)),
            ],
            out_specs=pl.BlockSpec(memory_space=pl.ANY),
            scratch_shapes=[
                pltpu.VMEM((GATHER_BUFS, FFN_SLOTS, ROW_GRANULE, D_MODEL), BF16),
                pltpu.VMEM((2, FFN_SLOTS, ROW_GRANULE, D_MODEL), BF16),
                pltpu.VMEM((D_MODEL, D_EXPERT), BF16),
                pltpu.VMEM((D_MODEL, D_EXPERT), BF16),
                pltpu.VMEM((D_EXPERT, D_MODEL), BF16),
                pltpu.SemaphoreType.DMA((GATHER_BUFS,)),
                pltpu.SemaphoreType.DMA((2,)),
            ],
        ),
        out_shape=jax.ShapeDtypeStruct((n_tok_tiles * GRANULES_PER_TILE, ROW_GRANULE, D_MODEL), BF16),
        compiler_params=pltpu.CompilerParams(
            dimension_semantics=("arbitrary",),
            vmem_limit_bytes=_vmem_limit(32 * 1024 * 1024)),
        input_output_aliases={4: 0},
        name="grouped_ffn",
    )(texp, gsrc, sdst, n_steps, xl_p.reshape(n_tok_tiles * GRANULES_PER_TILE, ROW_GRANULE, D_MODEL), wg, wu, wd)
    yl_p = yl_p.reshape(n_tok_tiles * SORT_ROWS, D_MODEL)

    y_p = pl.pallas_call(
        _combine_kernel,
        grid=(n_tok_tiles,),
        in_specs=[
            pl.BlockSpec((TOK_TILE, D_MODEL), lambda i: (i, 0)),
            pl.BlockSpec((SORT_ROWS, D_MODEL), lambda i: (i, 0)),
            pl.BlockSpec((None, 8, TOK_TILE), lambda i: (i, 0, 0)),
            pl.BlockSpec((1, D_MODEL), lambda i: (0, 0)),
        ],
        out_specs=pl.BlockSpec((TOK_TILE, D_MODEL), lambda i: (i, 0)),
        out_shape=jax.ShapeDtypeStruct((batch * seq, D_MODEL), F32),
        compiler_params=pltpu.CompilerParams(
            dimension_semantics=("arbitrary",),
            vmem_limit_bytes=_vmem_limit(40 * 1024 * 1024)),
        name="moe_combine",
    )(xmid_p, yl_p, meta_p, gf)

    comb_s = _combine_matrix(meta_s[None])
    y_s = _dense_moe(h2_s, xmid_s, comb_s, wg, wu, wd, gf, dec_batch)

    return (y_p.reshape(batch, seq, D_MODEL),
            y_s.reshape(dec_batch, 1, D_MODEL),
            k_p.reshape(1, batch, WINDOW, N_KV_HEADS, HEAD_DIM),
            v_p.reshape(1, batch, WINDOW, N_KV_HEADS, HEAD_DIM),
            k_s.reshape(1, dec_batch, 1, N_KV_HEADS, HEAD_DIM),
            v_s.reshape(1, dec_batch, 1, N_KV_HEADS, HEAD_DIM),
            vn_s.reshape(1, dec_batch, 1, GMLP_WIDTH))
```

```python
import functools

import jax
import jax.numpy as jnp
import numpy as np
from jax import lax
from jax.experimental import pallas as pl
from jax.experimental.pallas import tpu as pltpu

F32 = jnp.float32
BF16 = jnp.bfloat16

D_MODEL = 1024
HEAD_DIM = 64
HALF = HEAD_DIM // 2
N_Q_HEADS = 8
N_KV_HEADS = 2
ATTN_WIDTH = N_Q_HEADS * HEAD_DIM
KV_WIDTH = N_KV_HEADS * HEAD_DIM
WINDOW = 128
ROPE_THETA = 10000.0
GMLP_WIDTH = D_MODEL - ATTN_WIDTH
GMLP_GROUPS = 8
CHUNK = 128
N_GROUPS = 4
EXPERTS_PER_GROUP = 8
N_EXPERTS = N_GROUPS * EXPERTS_PER_GROUP
D_EXPERT = 256
EPS = 1e-6
NEG_INF = -1e30
PAST_LEN = 16384

LANES = 128
V7X_VMEM_BYTES = 64 * 1024 * 1024

C_Q = 0
C_K = C_Q + ATTN_WIDTH
C_V = C_K + KV_WIDTH
C_U = C_V + KV_WIDTH
C_VG = C_U + GMLP_WIDTH
IN_WIDTH = C_VG + GMLP_WIDTH

ROUTER_ROWS = 48
TOK_TILE = 512
BLOCKS_PER_TILE = TOK_TILE // WINDOW
ROW_GRANULE = 16
SORT_CHUNK = 512
SORT_ROWS = -(-(2 * TOK_TILE + N_EXPERTS * (ROW_GRANULE - 1)) // SORT_CHUNK) * SORT_CHUNK
GRANULES_PER_TILE = SORT_ROWS // ROW_GRANULE
FFN_ROWS = 512
FFN_SLOTS = FFN_ROWS // ROW_GRANULE
GATHER_BUFS = 3
SCATTER_BUFS = 3


def _dot(a, b):
    return jnp.dot(a, b, preferred_element_type=F32)


def _dot_nt(a, b):
    return lax.dot_general(a, b, (((1,), (1,)), ((), ())), preferred_element_type=F32)


def _gelu(x):
    return 0.5 * x * (1.0 + lax.erf(x * np.float32(np.sqrt(0.5))))


def _rmsnorm(x, g):
    return x * lax.rsqrt(jnp.mean(x * x, axis=-1, keepdims=True) + EPS) * g


def _layernorm(x, g, b):
    mu = jnp.mean(x, axis=-1, keepdims=True)
    xc = x - mu
    return xc * lax.rsqrt(jnp.mean(xc * xc, axis=-1, keepdims=True) + EPS) * g + b


def _first_argmax_rows(x, row_iota, n_rows):
    m = jnp.max(x, axis=0, keepdims=True)
    idx = jnp.min(jnp.where(x == m, row_iota, n_rows), axis=0, keepdims=True)
    return m, idx


def _route(h2b, wrt_ref, brt_ref):
    n_tok = h2b.shape[0]
    lt = _dot_nt(wrt_ref[...], h2b) + brt_ref[...]
    row8 = lax.broadcasted_iota(jnp.int32, (8, n_tok), 0).astype(F32)
    glog = lt[0:8]
    gmax, gidx = _first_argmax_rows(glog, row8, 8)
    g_w = 1.0 / jnp.sum(jnp.exp(glog - gmax), axis=0, keepdims=True)
    esel = lt[8:16]
    for g in range(1, N_GROUPS):
        esel = jnp.where(gidx == g, lt[8 + 8 * g:16 + 8 * g], esel)
    m1, i1 = _first_argmax_rows(esel, row8, 8)
    esel2 = jnp.where(row8 == i1, -jnp.inf, esel)
    m2, i2 = _first_argmax_rows(esel2, row8, 8)
    r = jnp.exp(m2 - m1)
    w1 = 1.0 / (1.0 + r)
    w2 = r / (1.0 + r)
    e1 = gidx * EXPERTS_PER_GROUP + i1
    e2 = gidx * EXPERTS_PER_GROUP + i2
    slab = jnp.where(row8 == 0, e1, 0.0)
    for r_idx, val in ((1, e2), (2, w1 * g_w), (3, w2 * g_w)):
        slab = jnp.where(row8 == r_idx, val, slab)
    return slab


def _softmax_with_sink(s, sink):
    m = jnp.maximum(jnp.max(s, axis=-1, keepdims=True), sink)
    e = jnp.exp(s - m)
    den = jnp.sum(e, axis=-1, keepdims=True) + jnp.exp(sink - m)
    return e * (1.0 / den)


def _rope(x, cos, sin_signed):
    first_half = (lax.broadcasted_iota(jnp.int32, x.shape, 1) & HALF) == 0
    partner = jnp.where(first_half, pltpu.roll(x, LANES - HALF, 1), pltpu.roll(x, HALF, 1))
    return x * cos + partner * sin_signed


_NAT_HEADS = (0, 2, 5, 7)
_SWP_HEADS = (1, 3, 4, 6)


def _prompt_mixer_kernel(sinks_ref, x_next_ref, x_ref, g1_ref, win_ref, cos_ref, sin_ref, lng_ref, lnb_ref, ws_ref,
                         bsf_ref, wout_ref, g2_ref, wrt_ref, brt_ref, upper_ref, lpad_ref,
                         xmid_ref, xl_ref, meta_ref, tab_ref, kout_ref, vout_ref,
                         z_a, z_b, mix_ref, k_n, k_s, v_n, v_s, q_nat, q_swp, s_ref, p_ref, vn_ref, h_ref, h2_ref,
                         *, tiles_per_seq):
    step = pl.program_id(0)
    t = jnp.maximum(step - 1, 0) % tiles_per_seq
    kv_bufs = (k_n, k_s, v_n, v_s)

    @pl.when(step == 0)
    def _():
        z_b[...] = jnp.zeros_like(z_b)

    @pl.when(t == 0)
    def _():
        for ref in kv_bufs:
            ref[0:WINDOW, :] = jnp.zeros((WINDOW, KV_WIDTH), BF16)

    args = (t, sinks_ref, x_next_ref, x_ref, g1_ref, win_ref, cos_ref, sin_ref, lng_ref, lnb_ref, ws_ref, bsf_ref,
            wout_ref, g2_ref, wrt_ref, brt_ref, upper_ref, lpad_ref, xmid_ref, xl_ref, meta_ref, tab_ref, kout_ref,
            vout_ref, kv_bufs, q_nat, q_swp, s_ref, p_ref, vn_ref, h_ref, h2_ref)

    @pl.when(step % 2 == 0)
    def _():
        _prompt_mixer_body(z_a, z_b, mix_ref, *args)

    @pl.when(step % 2 == 1)
    def _():
        _prompt_mixer_body(z_b, z_a, mix_ref, *args)


def _prompt_mixer_body(z_next, z_ref, mix_ref, t, sinks_ref, x_next_ref, x_ref, g1_ref, win_ref, cos_ref,
                       sin_ref, lng_ref, lnb_ref, ws_ref, bsf_ref, wout_ref, g2_ref, wrt_ref, brt_ref, upper_ref,
                       lpad_ref, xmid_ref, xl_ref, meta_ref, tab_ref, kout_ref, vout_ref, kv_bufs,
                       q_nat, q_swp, s_ref, p_ref, vn_ref, h_ref, h2_ref):
    k_n, k_s, v_n, v_s = kv_bufs
    cos = cos_ref[...]
    sin = sin_ref[...]
    lane = lax.broadcasted_iota(jnp.int32, (WINDOW, LANES), 1)
    lo = lane < HEAD_DIM
    row = lax.broadcasted_iota(jnp.int32, (WINDOW, WINDOW), 0)
    col = lax.broadcasted_iota(jnp.int32, (WINDOW, WINDOW), 1)
    mask_cur = col <= row
    mask_prev_band = col >= row
    mask_prev_first = jnp.logical_and(mask_prev_band, (jnp.zeros_like(row) + t) > 0)
    mask_band = jnp.concatenate([mask_prev_band, mask_cur], axis=1)
    mask_first = jnp.concatenate([mask_prev_first, mask_cur], axis=1)

    cq = cos * np.float32(HEAD_DIM ** -0.5)
    sq = sin * np.float32(HEAD_DIM ** -0.5)
    kf = _rope(z_ref[:, C_K:C_K + KV_WIDTH], cos, sin)
    vf = z_ref[:, C_V:C_V + KV_WIDTH]
    k_n[WINDOW:, :] = kf.astype(BF16)
    k_s[WINDOW:, :] = pltpu.roll(kf, HEAD_DIM, 1).astype(BF16)
    v_n[WINDOW:, :] = vf.astype(BF16)
    v_s[WINDOW:, :] = pltpu.roll(vf, HEAD_DIM, 1).astype(BF16)

    kout_ref[...] = kf[TOK_TILE - WINDOW:]
    vout_ref[...] = vf[TOK_TILE - WINDOW:]

    lo_t = lax.broadcasted_iota(jnp.int32, (TOK_TILE, LANES), 1) < HEAD_DIM
    for m in range(N_Q_HEADS // 2):
        qc = _rope(z_ref[:, C_Q + m * LANES:C_Q + (m + 1) * LANES], cq, sq)
        for hd, qh in ((2 * m, jnp.where(lo_t, qc, 0.0)), (2 * m + 1, jnp.where(lo_t, 0.0, qc))):
            if hd in _NAT_HEADS:
                q_nat[_NAT_HEADS.index(hd)] = qh.astype(BF16)
            else:
                q_swp[_SWP_HEADS.index(hd)] = qh.astype(BF16)

    stacks = ((q_nat, _NAT_HEADS, k_n, v_n), (q_swp, _SWP_HEADS, k_s, v_s))
    proj_cols = 2 * LANES

    def project_next(c):
        cols = slice(c * proj_cols, (c + 1) * proj_cols)
        z_next[:, cols] = _dot(h_ref[...], win_ref[:, cols])

    def scores(j):
        rows = slice(j * WINDOW, (j + 1) * WINDOW)
        keys = slice(j * WINDOW, (j + 2) * WINDOW)
        for si, (q_ref, _, k_buf, _) in enumerate(stacks):
            q_stack = jnp.concatenate([q_ref[i, rows, :] for i in range(len(_NAT_HEADS))], axis=0)
            s_ref[2 * j + si] = _dot_nt(q_stack, k_buf[keys, :])

    def softmax(j):
        mask = mask_first if j == 0 else mask_band
        for si, (_, heads, _, _) in enumerate(stacks):
            for i, hd in enumerate(heads):
                pr = slice(i * WINDOW, (i + 1) * WINDOW)
                p_ref[2 * j + si, pr, :] = _softmax_with_sink(
                    jnp.where(mask, s_ref[2 * j + si, pr, :], NEG_INF), sinks_ref[hd]).astype(BF16)

    def attend(j):
        rows = slice(j * WINDOW, (j + 1) * WINDOW)
        keys = slice(j * WINDOW, (j + 2) * WINDOW)
        o_nat = _dot(p_ref[2 * j], v_n[keys, :])
        o_swp = _dot(p_ref[2 * j + 1], v_s[keys, :])
        for m in range(N_Q_HEADS // 2):
            pr = slice(m * WINDOW, (m + 1) * WINDOW)
            even_nat = (2 * m) in _NAT_HEADS
            att = jnp.where(lo, o_nat[pr], o_swp[pr]) if even_nat else jnp.where(lo, o_swp[pr], o_nat[pr])
            mix_ref[rows, m * LANES:(m + 1) * LANES] = att.astype(BF16)

    def gmlp(m):
        cs = slice(m * LANES, (m + 1) * LANES)
        w0 = jnp.where(mask_cur, ws_ref[2 * m], 0.0).astype(BF16)
        w1 = jnp.where(mask_cur, ws_ref[2 * m + 1], 0.0).astype(BF16)
        wcat = jnp.concatenate([w0, w1], axis=1)
        for j in range(BLOCKS_PER_TILE):
            rows = slice(j * WINDOW, (j + 1) * WINDOW)
            vcol = vn_ref[rows, cs]
            rhs = jnp.concatenate([jnp.where(lo, vcol, jnp.zeros_like(vcol)),
                                   jnp.where(lo, jnp.zeros_like(vcol), vcol)], axis=0)
            sp = _dot(wcat, rhs) + bsf_ref[:, cs]
            u = _gelu(z_ref[rows, C_U + m * LANES:C_U + (m + 1) * LANES])
            mix_ref[rows, ATTN_WIDTH + m * LANES:ATTN_WIDTH + (m + 1) * LANES] = (u * sp).astype(BF16)

    def project_out(c):
        cols = slice(c * proj_cols, (c + 1) * proj_cols)
        xmid_ref[:, cols] = x_ref[:, cols] + _dot(mix_ref[...], wout_ref[:, cols])

    h_ref[...] = _rmsnorm(x_next_ref[...], g1_ref[...]).astype(BF16)
    project_next(0)
    project_next(1)
    for j in range(BLOCKS_PER_TILE):
        scores(j)
    vn_ref[...] = _layernorm(_gelu(z_ref[:, C_VG:C_VG + GMLP_WIDTH]), lng_ref[...], lnb_ref[...]).astype(BF16)
    project_next(2)
    softmax(0)
    project_next(3)
    softmax(1)
    project_next(4)
    softmax(2)
    project_next(5)
    softmax(3)
    project_next(6)
    for j in range(BLOCKS_PER_TILE):
        attend(j)
    for m in range(GMLP_GROUPS // 2):
        gmlp(m)
    for c in range(D_MODEL // proj_cols):
        project_out(c)
    h2_ref[...] = _rmsnorm(xmid_ref[...], g2_ref[...]).astype(BF16)
    d1, d2 = _sort_plan(_route(h2_ref[...], wrt_ref, brt_ref), upper_ref, lpad_ref, meta_ref, tab_ref)
    for c in range(SORT_ROWS // SORT_CHUNK):
        _sort_chunk(c, d1, d2, h2_ref[...], xl_ref)

    for ref in kv_bufs:
        ref[0:WINDOW, :] = ref[TOK_TILE:TOK_TILE + WINDOW, :]


def _sort_plan(slab, upper_ref, lpad_ref, meta_ref, tab_ref):
    n_tok = slab.shape[1]
    e1, e2 = slab[0:1], slab[1:2]
    row32 = lax.broadcasted_iota(jnp.int32, (N_EXPERTS, n_tok), 0).astype(F32)
    sel1 = row32 == e1
    sel2 = row32 == e2
    onehot = jnp.where(sel1, 1.0, jnp.where(sel2, 1.0, 0.0))
    earlier = _dot(onehot.astype(BF16), upper_ref[...])
    cnt = jnp.sum(onehot, axis=1, keepdims=True)
    pc = jnp.floor((cnt + (ROW_GRANULE - 1)) * (1.0 / ROW_GRANULE)) * ROW_GRANULE
    pc_b = jnp.broadcast_to(pc, (N_EXPERTS, LANES))
    pc_pad = jnp.concatenate([pc_b, jnp.zeros((LANES - N_EXPERTS, LANES), F32)], axis=0).astype(BF16)
    start = _dot(lpad_ref[...], pc_pad)
    base = start[:, 0:1] + earlier
    d1 = jnp.sum(jnp.where(sel1, base, 0.0), axis=0, keepdims=True)
    d2 = jnp.sum(jnp.where(sel2, base, 0.0), axis=0, keepdims=True)
    row8 = lax.broadcasted_iota(jnp.int32, (8, n_tok), 0)
    meta_ref[...] = jnp.where(row8 == 0, d1, jnp.where(row8 == 1, d2, jnp.where(row8 >= 4, 0.0, slab)))
    lane = lax.broadcasted_iota(jnp.int32, (N_EXPERTS, LANES), 1)
    tab_ref[...] = jnp.where(lane == 0, pc_b, jnp.where(lane == 1, start, 0.0))
    return d1, d2


def _sort_chunk(c, d1, d2, h2b, xl_ref):
    n_tok = h2b.shape[0]
    r_iota = (lax.broadcasted_iota(jnp.int32, (SORT_CHUNK, n_tok), 0) + c * SORT_CHUNK).astype(F32)
    perm = jnp.where(r_iota == d1, 1.0, jnp.where(r_iota == d2, 1.0, 0.0)).astype(BF16)
    xl_ref[c * SORT_CHUNK:(c + 1) * SORT_CHUNK, :] = _dot(perm, h2b).astype(BF16)


def _sample_mixer_kernel(sinks_ref, x_ref, ck_ref, cv_ref, g1_ref, win_ref, cos_ref, sin_ref, lng_ref, lnb_ref,
                         ws0_ref, bs0_ref, wout_ref, g2_ref, wrt_ref, brt_ref,
                         xmid_ref, h2_ref, meta_ref, kout_ref, vout_ref, vnout_ref,
                         mix_ref):
    n_seq = x_ref.shape[0]
    seq_chunk = 16
    x = x_ref[...]
    h = _rmsnorm(x, g1_ref[...]).astype(BF16)
    z = _dot(h, win_ref[...])
    cos = cos_ref[...]
    sin = sin_ref[...]
    scale = np.float32(HEAD_DIM ** -0.5)
    lane = lax.broadcasted_iota(jnp.int32, (n_seq, LANES), 1)
    lo = lane < HEAD_DIM
    kf = _rope(z[:, C_K:C_K + KV_WIDTH], cos, sin)
    vf = z[:, C_V:C_V + KV_WIDTH]
    kout_ref[...] = kf
    vout_ref[...] = vf
    kb = kf.astype(BF16).astype(F32)
    vb = vf.astype(BF16).astype(F32)

    q_heads = []
    for hd in range(N_Q_HEADS):
        m = hd // 2
        qc = _rope(z[:, C_Q + m * LANES:C_Q + (m + 1) * LANES], cos, sin) * scale
        keep = lo if hd % 2 == 0 else ~lo
        qm = jnp.where(keep, qc, 0.0)
        if (hd % 2) != (hd // (N_Q_HEADS // N_KV_HEADS)):
            qm = pltpu.roll(qm, HEAD_DIM, 1)
        q_heads.append(qm.astype(BF16))

    s_new = [jnp.sum(q_heads[hd].astype(F32) * kb, axis=-1, keepdims=True) for hd in range(N_Q_HEADS)]

    rr = lax.broadcasted_iota(jnp.int32, (N_Q_HEADS * seq_chunk, seq_chunk * WINDOW), 0)
    cc = lax.broadcasted_iota(jnp.int32, (N_Q_HEADS * seq_chunk, seq_chunk * WINDOW), 1)
    same_seq = (rr % seq_chunk) == (cc // WINDOW)
    kv_lo = lax.broadcasted_iota(jnp.int32, (seq_chunk, LANES), 1) < HEAD_DIM

    for c in range(n_seq // seq_chunk):
        sr = slice(c * seq_chunk, (c + 1) * seq_chunk)
        kc = ck_ref[sr].reshape(seq_chunk * WINDOW, KV_WIDTH).astype(BF16)
        vc = cv_ref[sr].reshape(seq_chunk * WINDOW, KV_WIDTH).astype(BF16)
        qs = jnp.concatenate([q_heads[hd][sr] for hd in range(N_Q_HEADS)], axis=0)
        s = jnp.where(same_seq, _dot_nt(qs, kc), NEG_INF)
        sn = jnp.concatenate([s_new[hd][sr] for hd in range(N_Q_HEADS)], axis=0)
        sink = jnp.concatenate([jnp.full((seq_chunk, 1), sinks_ref[hd], F32) for hd in range(N_Q_HEADS)], axis=0)
        m = jnp.maximum(jnp.maximum(jnp.max(s, axis=-1, keepdims=True), sn), sink)
        e = jnp.exp(s - m)
        en = jnp.exp(sn - m)
        inv = 1.0 / (jnp.sum(e, axis=-1, keepdims=True) + en + jnp.exp(sink - m))
        o = _dot((e * inv).astype(BF16), vc)
        pn = (en * inv).astype(BF16).astype(F32)
        for mcol in range(N_Q_HEADS // 2):
            halves = []
            for hd in (2 * mcol, 2 * mcol + 1):
                oh = o[hd * seq_chunk:(hd + 1) * seq_chunk] + pn[hd * seq_chunk:(hd + 1) * seq_chunk] * vb[sr]
                if (hd % 2) != (hd // (N_Q_HEADS // N_KV_HEADS)):
                    oh = pltpu.roll(oh, HEAD_DIM, 1)
                halves.append(oh)
            att = jnp.where(kv_lo, halves[0], halves[1])
            mix_ref[sr, mcol * LANES:(mcol + 1) * LANES] = att.astype(BF16)

    u = _gelu(z[:, C_U:C_U + GMLP_WIDTH])
    vn = _layernorm(_gelu(z[:, C_VG:C_VG + GMLP_WIDTH]), lng_ref[...], lnb_ref[...])
    vnout_ref[...] = vn
    sp = ws0_ref[...].astype(BF16).astype(F32) * vn.astype(BF16).astype(F32) + bs0_ref[...]
    mix_ref[:, ATTN_WIDTH:] = (u * sp).astype(BF16)

    xmid = x + _dot(mix_ref[...], wout_ref[...])
    xmid_ref[...] = xmid
    h2b = _rmsnorm(xmid, g2_ref[...]).astype(BF16)
    h2_ref[...] = h2b
    meta_ref[...] = _route(h2b, wrt_ref, brt_ref)


def _dense_moe_kernel(t_ref, xmid_ref, comb_ref, wg_ref, wu_ref, wd_ref, gf_ref, y_ref, acc_ref):
    e = pl.program_id(1)

    @pl.when(e == 0)
    def _():
        acc_ref[...] = xmid_ref[...]

    tb = t_ref[...]
    gate = _dot(tb, wg_ref[...].astype(BF16))
    up = _dot(tb, wu_ref[...].astype(BF16))
    hid = (gate * (1.0 / (1.0 + jnp.exp(-gate))) * up).astype(BF16)
    out = _dot(hid, wd_ref[...].astype(BF16))
    lane = lax.broadcasted_iota(jnp.int32, comb_ref.shape, 1)
    c_e = jnp.sum(jnp.where(lane == e, comb_ref[...], 0.0), axis=-1, keepdims=True)
    acc_ref[...] += c_e * out

    @pl.when(e == pl.num_programs(1) - 1)
    def _():
        y_ref[...] = _rmsnorm(acc_ref[...], gf_ref[...])


def _grouped_ffn_kernel(texp_ref, gsrc_ref, sdst_ref, ntiles_ref, xl_hbm, wg_ref, wu_ref, wd_ref, yl_hbm,
                        xbuf, ybuf, wg_b, wu_b, wd_b, gsem, ssem):
    j = pl.program_id(0)
    n_tiles = ntiles_ref[0]

    def gather_copy(tile, s, b):
        return pltpu.make_async_copy(xl_hbm.at[gsrc_ref[tile * FFN_SLOTS + s]], xbuf.at[b, s], gsem.at[b])

    def scatter_copy(tile, s):
        b = (tile + SCATTER_BUFS) % SCATTER_BUFS
        return pltpu.make_async_copy(ybuf.at[b, s], yl_hbm.at[sdst_ref[(tile + 1) * FFN_SLOTS + s]], ssem.at[b])

    @pl.when(j < n_tiles)
    def _():
        gb = j % GATHER_BUFS
        ahead = GATHER_BUFS - 1
        nxt = j + ahead
        nxt_b = nxt % GATHER_BUFS

        @pl.when(j == 0)
        def _():
            ybuf[SCATTER_BUFS - 1] = jnp.zeros(ybuf.shape[1:], BF16)
            for k in range(ahead):
                for s in range(FFN_SLOTS):
                    gather_copy(k, s, k).start()

        for s in range(FFN_SLOTS):
            gather_copy(j, s, gb).wait()

        @pl.when(j >= SCATTER_BUFS - 1)
        def _():
            for s in range(FFN_SLOTS):
                scatter_copy(j - SCATTER_BUFS, s).wait()

        @pl.when(jnp.logical_or(j == 0, texp_ref[j] != texp_ref[jnp.maximum(j - 1, 0)]))
        def _():
            wg_b[...] = wg_ref[...].astype(BF16)
            wu_b[...] = wu_ref[...].astype(BF16)
            wd_b[...] = wd_ref[...].astype(BF16)

        x = xbuf[gb].reshape(FFN_ROWS, D_MODEL)
        gate = _dot(x, wg_b[...])
        for s in range(FFN_SLOTS):
            scatter_copy(j - 1, s).start()
        up = _dot(x, wu_b[...])
        for s in range(FFN_SLOTS):
            gather_copy(nxt, s, nxt_b).start()
        hid = (gate * (1.0 / (1.0 + jnp.exp(-gate))) * up).astype(BF16)
        ybuf[j % SCATTER_BUFS] = _dot(hid, wd_b[...]).astype(BF16).reshape(FFN_SLOTS, ROW_GRANULE, D_MODEL)

        @pl.when(j == n_tiles - 1)
        def _():
            for k in range(1, GATHER_BUFS):
                for s in range(FFN_SLOTS):
                    gather_copy(nxt, s, (j + k) % GATHER_BUFS).wait()
            for s in range(FFN_SLOTS):
                scatter_copy(j, s).start()
            for back in range(SCATTER_BUFS):
                @pl.when(j - back >= -1)
                def _():
                    for s in range(FFN_SLOTS):
                        scatter_copy(j - back, s).wait()


def _combine_kernel(xmid_ref, yl_ref, meta_ref, gf_ref, y_ref):
    n_tok = xmid_ref.shape[0]
    meta = meta_ref[...]
    meta_t = jnp.concatenate([meta, jnp.zeros((LANES - 8, n_tok), F32)], axis=0).T
    d1, d2, w1, w2 = meta_t[:, 0:1], meta_t[:, 1:2], meta_t[:, 2:3], meta_t[:, 3:4]
    acc = xmid_ref[...]
    for c in range(SORT_ROWS // SORT_CHUNK):
        r_iota = (lax.broadcasted_iota(jnp.int32, (n_tok, SORT_CHUNK), 1) + c * SORT_CHUNK).astype(F32)
        unsort = jnp.where(r_iota == d1, w1, jnp.where(r_iota == d2, w2, 0.0)).astype(BF16)
        acc = acc + _dot(unsort, yl_ref[c * SORT_CHUNK:(c + 1) * SORT_CHUNK, :])
    y_ref[...] = _rmsnorm(acc, gf_ref[...])


def _ffn_schedule(tab):
    n_tok_tiles = tab.shape[0]
    strips = (tab[:, :, 0] * (1.0 / ROW_GRANULE)).astype(jnp.int32)
    starts = (tab[:, :, 1] * (1.0 / ROW_GRANULE)).astype(jnp.int32)
    cnt = strips.T
    row0 = starts.T
    cs = jnp.cumsum(cnt, axis=1) - cnt
    n_str = jnp.sum(cnt, axis=1)
    np_str = (n_str + FFN_SLOTS - 1) // FFN_SLOTS * FFN_SLOTS
    ends = jnp.cumsum(np_str)
    base = ends - np_str
    n_steps_max = _ffn_steps_max(n_tok_tiles) + GATHER_BUFS - 1
    step0 = jnp.arange(n_steps_max, dtype=jnp.int32) * FFN_SLOTS
    stream = jnp.minimum(jnp.sum(ends[None, :] <= step0[:, None], axis=1), N_EXPERTS - 1)
    pick = stream[:, None] == jnp.arange(N_EXPERTS, dtype=jnp.int32)[None, :]
    sel = lambda x: jnp.sum(jnp.where(pick[:, :, None], x[None], 0), axis=1)
    cs_j, cnt_j, row0_j = sel(cs), sel(cnt), sel(row0)
    base_j = jnp.sum(jnp.where(pick, base[None, :], 0), axis=1)
    q = step0[:, None] + jnp.arange(FFN_SLOTS, dtype=jnp.int32)[None, :] - base_j[:, None]
    reached = cs_j[:, None, :] <= q[:, :, None]
    last = lambda x: jnp.sum(jnp.where(reached, jnp.diff(x, axis=1, prepend=0)[:, None, :], 0), axis=2)
    tile_idx = jnp.sum(reached, axis=2).astype(jnp.int32) - 1
    g = q - last(cs_j)
    valid = g < last(cnt_j)
    granule = tile_idx * GRANULES_PER_TILE + last(row0_j) + g
    assert n_tok_tiles >= 2 * FFN_SLOTS
    slot = jnp.arange(FFN_SLOTS, dtype=jnp.int32)[None, :]
    parity = jnp.arange(n_steps_max, dtype=jnp.int32)[:, None] % 2
    pad_dst = lambda par: (slot + par * FFN_SLOTS) * GRANULES_PER_TILE + GRANULES_PER_TILE - 1
    gsrc = jnp.where(valid, granule, slot * GRANULES_PER_TILE + GRANULES_PER_TILE - 2).astype(jnp.int32)
    sdst = jnp.where(valid, granule, pad_dst(parity))
    sdst = jnp.concatenate([pad_dst(1), sdst], axis=0)
    texp = stream.astype(jnp.int32)
    n_steps = (ends[-1] // FFN_SLOTS).astype(jnp.int32).reshape(1)
    return texp, gsrc.reshape(-1), sdst.astype(jnp.int32).reshape(-1), n_steps


assert GRANULES_PER_TILE - (2 * TOK_TILE + N_EXPERTS * (ROW_GRANULE - 1)) // ROW_GRANULE >= 2


def _ffn_steps_max(n_tok_tiles):
    return -(-(n_tok_tiles * GRANULES_PER_TILE + N_EXPERTS * (FFN_SLOTS - 1)) // FFN_SLOTS)


def _vmem_limit(n_bytes):
    return int(min(n_bytes, V7X_VMEM_BYTES - 4 * 1024 * 1024))


def _rope_tables(pos):
    inv_freq = ROPE_THETA ** (-jnp.arange(HALF, dtype=F32) * 2.0 / HEAD_DIM)
    ang = pos.astype(F32)[:, None] * inv_freq[None, :]
    cos, sin = jnp.cos(ang), jnp.sin(ang)
    reps = LANES // HEAD_DIM
    return jnp.tile(jnp.concatenate([cos, cos], axis=1), (1, reps)), jnp.tile(jnp.concatenate([-sin, sin], axis=1), (1, reps))


def _combine_matrix(meta):
    m = jnp.swapaxes(meta, 1, 2).reshape(-1, 8)
    ids = m[:, 0:2].astype(jnp.int32)
    return jnp.sum(jax.nn.one_hot(ids, N_EXPERTS, dtype=F32) * m[:, 2:4, None], axis=1)


def _dense_moe(t, xmid, comb, wg, wu, wd, gf, tile):
    n = t.shape[0]
    comb = jnp.pad(comb, ((0, 0), (0, LANES - N_EXPERTS)))
    return pl.pallas_call(
        _dense_moe_kernel,
        grid=(n // tile, N_EXPERTS),
        in_specs=[
            pl.BlockSpec((tile, D_MODEL), lambda i, e: (i, 0)),
            pl.BlockSpec((tile, D_MODEL), lambda i, e: (i, 0)),
            pl.BlockSpec((tile, LANES), lambda i, e: (i, 0)),
            pl.BlockSpec((None, D_MODEL, D_EXPERT), lambda i, e: (e, 0, 0)),
            pl.BlockSpec((None, D_MODEL, D_EXPERT), lambda i, e: (e, 0, 0)),
            pl.BlockSpec((None, D_EXPERT, D_MODEL), lambda i, e: (e, 0, 0)),
            pl.BlockSpec((1, D_MODEL), lambda i, e: (0, 0)),
        ],
        out_specs=pl.BlockSpec((tile, D_MODEL), lambda i, e: (i, 0)),
        out_shape=jax.ShapeDtypeStruct((n, D_MODEL), F32),
        scratch_shapes=[pltpu.VMEM((tile, D_MODEL), F32)],
        compiler_params=pltpu.CompilerParams(
            dimension_semantics=("arbitrary", "arbitrary"),
            vmem_limit_bytes=_vmem_limit(48 * 1024 * 1024)),
        name="dense_moe",
    )(t, xmid, comb, wg, wu, wd, gf)


def kernel(x_prompt, x_sample, cache_swa_k, cache_swa_v, norm_mix_g, w_in, attn_sinks, gmlp_ln_g, gmlp_ln_b,
           gmlp_w_s, gmlp_b_s, w_out, norm_ffn_g, router_group_w, router_group_b, router_expert_w,
           router_expert_b, expert_w_gate, expert_w_up, expert_w_down, final_norm_g):
    assert norm_mix_g.shape[0] == 1, "single-layer trunk"
    batch, seq, _ = x_prompt.shape
    dec_batch = x_sample.shape[0]
    assert x_sample.shape[1] == 1 and seq % TOK_TILE == 0

    win_ext = w_in[0].astype(BF16)
    wout = w_out[0].astype(BF16)
    g1 = norm_mix_g[0][None, :]
    g2 = norm_ffn_g[0][None, :]
    gf = final_norm_g[None, :]
    lng = gmlp_ln_g[0][None, :]
    lnb = gmlp_ln_b[0][None, :]
    sinks = attn_sinks[0]
    ws = gmlp_w_s[0]
    group_dim = GMLP_WIDTH // GMLP_GROUPS
    bsf = jnp.repeat(gmlp_b_s[0].T, group_dim, axis=1)
    ws0 = jnp.repeat(ws[:, 0, 0], group_dim)[None, :]
    bs0 = jnp.repeat(gmlp_b_s[0][:, 0], group_dim)[None, :]
    wrt = jnp.zeros((ROUTER_ROWS, D_MODEL), F32)
    wrt = wrt.at[0:N_GROUPS].set(router_group_w[0].T)
    wrt = wrt.at[8:8 + N_EXPERTS].set(router_expert_w[0].reshape(D_MODEL, N_EXPERTS).T).astype(BF16)
    brt = jnp.full((ROUTER_ROWS, 1), NEG_INF, F32)
    brt = brt.at[0:N_GROUPS, 0].set(router_group_b[0])
    brt = brt.at[8:8 + N_EXPERTS, 0].set(router_expert_b[0].reshape(N_EXPERTS))
    wg, wu, wd = expert_w_gate[0], expert_w_up[0], expert_w_down[0]
    cos_p, sin_p = _rope_tables(jnp.arange(seq, dtype=jnp.int32))
    cos_s, sin_s = _rope_tables(PAST_LEN + jnp.arange(1, dtype=jnp.int32))

    full = lambda shape: pl.BlockSpec(shape, lambda *_: (0,) * len(shape))
    smem = pl.BlockSpec(memory_space=pltpu.SMEM)
    n_tiles = seq // TOK_TILE

    upper = jnp.triu(jnp.ones((TOK_TILE, TOK_TILE), BF16), k=1)
    lpad = (jnp.arange(LANES)[None, :] < jnp.arange(N_EXPERTS)[:, None]).astype(BF16)
    n_tok_tiles = batch * n_tiles

    x2d = x_prompt.reshape(batch * seq, D_MODEL)
    tile_a = lambda s: jnp.minimum(s, n_tok_tiles - 1)
    cur = lambda s: jnp.maximum(s - 1, 0)
    tile_b = cur
    xmid_p, xl_p, meta_p, tab_p, k_p, v_p = pl.pallas_call(
        functools.partial(_prompt_mixer_kernel, tiles_per_seq=n_tiles),
        grid=(n_tok_tiles + 1,),
        in_specs=[
            smem,
            pl.BlockSpec((TOK_TILE, D_MODEL), lambda s: (tile_a(s), 0)),
            pl.BlockSpec((TOK_TILE, D_MODEL), lambda s: (cur(s), 0)),
            full((1, D_MODEL)),
            full((D_MODEL, IN_WIDTH)),
            pl.BlockSpec((TOK_TILE, LANES), lambda s: (tile_b(s) % n_tiles, 0)),
            pl.BlockSpec((TOK_TILE, LANES), lambda s: (tile_b(s) % n_tiles, 0)),
            full((1, GMLP_WIDTH)),
            full((1, GMLP_WIDTH)),
            full((GMLP_GROUPS, CHUNK, CHUNK)),
            full((CHUNK, GMLP_WIDTH)),
            full((D_MODEL, D_MODEL)),
            full((1, D_MODEL)),
            full((ROUTER_ROWS, D_MODEL)),
            full((ROUTER_ROWS, 1)),
            full((TOK_TILE, TOK_TILE)),
            full((N_EXPERTS, LANES)),
        ],
        out_specs=[
            pl.BlockSpec((TOK_TILE, D_MODEL), lambda s: (cur(s), 0)),
            pl.BlockSpec((None, SORT_ROWS, D_MODEL), lambda s: (cur(s), 0, 0)),
            pl.BlockSpec((None, 8, TOK_TILE), lambda s: (cur(s), 0, 0)),
            pl.BlockSpec((None, N_EXPERTS, LANES), lambda s: (cur(s), 0, 0)),
            pl.BlockSpec((None, WINDOW, KV_WIDTH), lambda s: (tile_b(s) // n_tiles, 0, 0)),
            pl.BlockSpec((None, WINDOW, KV_WIDTH), lambda s: (tile_b(s) // n_tiles, 0, 0)),
        ],
        out_shape=[
            jax.ShapeDtypeStruct((batch * seq, D_MODEL), F32),
            jax.ShapeDtypeStruct((n_tok_tiles, SORT_ROWS, D_MODEL), BF16),
            jax.ShapeDtypeStruct((n_tok_tiles, 8, TOK_TILE), F32),
            jax.ShapeDtypeStruct((n_tok_tiles, N_EXPERTS, LANES), F32),
            jax.ShapeDtypeStruct((batch, WINDOW, KV_WIDTH), F32),
            jax.ShapeDtypeStruct((batch, WINDOW, KV_WIDTH), F32),
        ],
        scratch_shapes=[
            pltpu.VMEM((TOK_TILE, IN_WIDTH), F32),
            pltpu.VMEM((TOK_TILE, IN_WIDTH), F32),
            pltpu.VMEM((TOK_TILE, D_MODEL), BF16),
        ] + [pltpu.VMEM((WINDOW + TOK_TILE, KV_WIDTH), BF16)] * 4 + [
            pltpu.VMEM((len(_NAT_HEADS), TOK_TILE, LANES), BF16),
            pltpu.VMEM((len(_SWP_HEADS), TOK_TILE, LANES), BF16),
            pltpu.VMEM((2 * BLOCKS_PER_TILE, len(_NAT_HEADS) * WINDOW, 2 * WINDOW), F32),
            pltpu.VMEM((2 * BLOCKS_PER_TILE, len(_NAT_HEADS) * WINDOW, 2 * WINDOW), BF16),
            pltpu.VMEM((TOK_TILE, GMLP_WIDTH), BF16),
            pltpu.VMEM((TOK_TILE, D_MODEL), BF16),
            pltpu.VMEM((TOK_TILE, D_MODEL), BF16),
        ],
        compiler_params=pltpu.CompilerParams(
            dimension_semantics=("arbitrary",),
            vmem_limit_bytes=_vmem_limit(56 * 1024 * 1024)),
        name="prompt_mixer",
    )(sinks, x2d, x2d, g1, win_ext, cos_p, sin_p, lng, lnb, ws, bsf, wout, g2, wrt, brt, upper, lpad)

    xs = x_sample.reshape(dec_batch, D_MODEL)
    ck = cache_swa_k[0].reshape(dec_batch, WINDOW, KV_WIDTH)
    cv = cache_swa_v[0].reshape(dec_batch, WINDOW, KV_WIDTH)
    xmid_s, h2_s, meta_s, k_s, v_s, vn_s = pl.pallas_call(
        _sample_mixer_kernel,
        in_specs=[smem] + [pl.BlockSpec(memory_space=pltpu.VMEM)] * 15,
        out_shape=[
            jax.ShapeDtypeStruct((dec_batch, D_MODEL), F32),
            jax.ShapeDtypeStruct((dec_batch, D_MODEL), BF16),
            jax.ShapeDtypeStruct((8, dec_batch), F32),
            jax.ShapeDtypeStruct((dec_batch, KV_WIDTH), F32),
            jax.ShapeDtypeStruct((dec_batch, KV_WIDTH), F32),
            jax.ShapeDtypeStruct((dec_batch, GMLP_WIDTH), F32),
        ],
        scratch_shapes=[pltpu.VMEM((dec_batch, D_MODEL), BF16)],
        compiler_params=pltpu.CompilerParams(vmem_limit_bytes=_vmem_limit(56 * 1024 * 1024)),
        name="sample_mixer",
    )(sinks, xs, ck, cv, g1, win_ext, cos_s, sin_s, lng, lnb, ws0, bs0, wout, g2, wrt, brt)

    texp, gsrc, sdst, n_steps = _ffn_schedule(tab_p)
    yl_p = pl.pallas_call(
        _grouped_ffn_kernel,
        grid_spec=pltpu.PrefetchScalarGridSpec(
            num_scalar_prefetch=4,
            grid=(_ffn_steps_max(n_tok_tiles),),
            in_specs=[
                pl.BlockSpec(memory_space=pl.ANY),
                pl.BlockSpec((None, D_MODEL, D_EXPERT), lambda j, te, gs, sd, ns: (te[j], 0, 0)),
                pl.BlockSpec((None, D_MODEL, D_EXPERT), lambda j, te, gs, sd, ns: (te[j], 0, 0)),
                pl.BlockSpec((None, D_EXPERT, D_MODEL), lambda j, te, gs, sd, ns: (te[j], 0, 0)),
            ],
            out_specs=pl.BlockSpec(memory_space=pl.ANY),
            scratch_shapes=[
                pltpu.VMEM((GATHER_BUFS, FFN_SLOTS, ROW_GRANULE, D_MODEL), BF16),
                pltpu.VMEM((SCATTER_BUFS, FFN_SLOTS, ROW_GRANULE, D_MODEL), BF16),
                pltpu.VMEM((D_MODEL, D_EXPERT), BF16),
                pltpu.VMEM((D_MODEL, D_EXPERT), BF16),
                pltpu.VMEM((D_EXPERT, D_MODEL), BF16),
                pltpu.SemaphoreType.DMA((GATHER_BUFS,)),
                pltpu.SemaphoreType.DMA((SCATTER_BUFS,)),
            ],
        ),
        out_shape=jax.ShapeDtypeStruct((n_tok_tiles * GRANULES_PER_TILE, ROW_GRANULE, D_MODEL), BF16),
        compiler_params=pltpu.CompilerParams(
            dimension_semantics=("arbitrary",),
            vmem_limit_bytes=_vmem_limit(32 * 1024 * 1024)),
        input_output_aliases={4: 0},
        name="grouped_ffn",
    )(texp, gsrc, sdst, n_steps, xl_p.reshape(n_tok_tiles * GRANULES_PER_TILE, ROW_GRANULE, D_MODEL), wg, wu, wd)
    yl_p = yl_p.reshape(n_tok_tiles * SORT_ROWS, D_MODEL)

    y_p = pl.pallas_call(
        _combine_kernel,
        grid=(n_tok_tiles,),
        in_specs=[
            pl.BlockSpec((TOK_TILE, D_MODEL), lambda i: (i, 0)),
            pl.BlockSpec((SORT_ROWS, D_MODEL), lambda i: (i, 0)),
            pl.BlockSpec((None, 8, TOK_TILE), lambda i: (i, 0, 0)),
            pl.BlockSpec((1, D_MODEL), lambda i: (0, 0)),
        ],
        out_specs=pl.BlockSpec((TOK_TILE, D_MODEL), lambda i: (i, 0)),
        out_shape=jax.ShapeDtypeStruct((batch * seq, D_MODEL), F32),
        compiler_params=pltpu.CompilerParams(
            dimension_semantics=("arbitrary",),
            vmem_limit_bytes=_vmem_limit(40 * 1024 * 1024)),
        name="moe_combine",
    )(xmid_p, yl_p, meta_p, gf)

    comb_s = _combine_matrix(meta_s[None])
    y_s = _dense_moe(h2_s, xmid_s, comb_s, wg, wu, wd, gf, dec_batch)

    return (y_p.reshape(batch, seq, D_MODEL),
            y_s.reshape(dec_batch, 1, D_MODEL),
            k_p.reshape(1, batch, WINDOW, N_KV_HEADS, HEAD_DIM),
            v_p.reshape(1, batch, WINDOW, N_KV_HEADS, HEAD_DIM),
            k_s.reshape(1, dec_batch, 1, N_KV_HEADS, HEAD_DIM),
            v_s.reshape(1, dec_batch, 1, N_KV_HEADS, HEAD_DIM),
            vn_s.reshape(1, dec_batch, 1, GMLP_WIDTH))
```

```python
import functools

import jax
import jax.numpy as jnp
import numpy as np
from jax import lax
from jax.experimental import pallas as pl
from jax.experimental.pallas import tpu as pltpu

F32 = jnp.float32
BF16 = jnp.bfloat16

D_MODEL = 1024
HEAD_DIM = 64
HALF = HEAD_DIM // 2
N_Q_HEADS = 8
N_KV_HEADS = 2
ATTN_WIDTH = N_Q_HEADS * HEAD_DIM
KV_WIDTH = N_KV_HEADS * HEAD_DIM
WINDOW = 128
ROPE_THETA = 10000.0
GMLP_WIDTH = D_MODEL - ATTN_WIDTH
GMLP_GROUPS = 8
CHUNK = 128
N_GROUPS = 4
EXPERTS_PER_GROUP = 8
N_EXPERTS = N_GROUPS * EXPERTS_PER_GROUP
D_EXPERT = 256
EPS = 1e-6
NEG_INF = -1e30
PAST_LEN = 16384

LANES = 128
V7X_VMEM_BYTES = 64 * 1024 * 1024

C_Q = 0
C_K = C_Q + ATTN_WIDTH
C_V = C_K + KV_WIDTH
C_U = C_V + KV_WIDTH
C_VG = C_U + GMLP_WIDTH
IN_WIDTH = C_VG + GMLP_WIDTH

ROUTER_ROWS = 48
TOK_TILE = 512
BLOCKS_PER_TILE = TOK_TILE // WINDOW
ROW_GRANULE = 16
SORT_CHUNK = 512
SORT_ROWS = -(-(2 * TOK_TILE + N_EXPERTS * (ROW_GRANULE - 1)) // SORT_CHUNK) * SORT_CHUNK
GRANULES_PER_TILE = SORT_ROWS // ROW_GRANULE
FFN_ROWS = 512
FFN_SLOTS = FFN_ROWS // ROW_GRANULE
GATHER_BUFS = 3
SCATTER_BUFS = 3


def _dot(a, b):
    return jnp.dot(a, b, preferred_element_type=F32)


def _dot_nt(a, b):
    return lax.dot_general(a, b, (((1,), (1,)), ((), ())), preferred_element_type=F32)


def _gelu(x):
    return 0.5 * x * (1.0 + lax.erf(x * np.float32(np.sqrt(0.5))))


def _rmsnorm(x, g):
    return x * lax.rsqrt(jnp.mean(x * x, axis=-1, keepdims=True) + EPS) * g


def _layernorm(x, g, b):
    mu = jnp.mean(x, axis=-1, keepdims=True)
    xc = x - mu
    return xc * lax.rsqrt(jnp.mean(xc * xc, axis=-1, keepdims=True) + EPS) * g + b


def _first_argmax_rows(x, row_iota, n_rows):
    m = jnp.max(x, axis=0, keepdims=True)
    idx = jnp.min(jnp.where(x == m, row_iota, n_rows), axis=0, keepdims=True)
    return m, idx


def _route(h2b, wrt_ref, brt_ref):
    n_tok = h2b.shape[0]
    lt = _dot_nt(wrt_ref[...], h2b) + brt_ref[...]
    row8 = lax.broadcasted_iota(jnp.int32, (8, n_tok), 0).astype(F32)
    glog = lt[0:8]
    gmax, gidx = _first_argmax_rows(glog, row8, 8)
    g_w = 1.0 / jnp.sum(jnp.exp(glog - gmax), axis=0, keepdims=True)
    esel = lt[8:16]
    for g in range(1, N_GROUPS):
        esel = jnp.where(gidx == g, lt[8 + 8 * g:16 + 8 * g], esel)
    m1, i1 = _first_argmax_rows(esel, row8, 8)
    esel2 = jnp.where(row8 == i1, -jnp.inf, esel)
    m2, i2 = _first_argmax_rows(esel2, row8, 8)
    r = jnp.exp(m2 - m1)
    w1 = 1.0 / (1.0 + r)
    w2 = r / (1.0 + r)
    e1 = gidx * EXPERTS_PER_GROUP + i1
    e2 = gidx * EXPERTS_PER_GROUP + i2
    slab = jnp.where(row8 == 0, e1, 0.0)
    for r_idx, val in ((1, e2), (2, w1 * g_w), (3, w2 * g_w)):
        slab = jnp.where(row8 == r_idx, val, slab)
    return slab


def _softmax_with_sink(s, sink):
    m = jnp.maximum(jnp.max(s, axis=-1, keepdims=True), sink)
    e = jnp.exp(s - m)
    den = jnp.sum(e, axis=-1, keepdims=True) + jnp.exp(sink - m)
    return e * (1.0 / den)


def _rope(x, cos, sin_signed):
    first_half = (lax.broadcasted_iota(jnp.int32, x.shape, 1) & HALF) == 0
    partner = jnp.where(first_half, pltpu.roll(x, LANES - HALF, 1), pltpu.roll(x, HALF, 1))
    return x * cos + partner * sin_signed


_NAT_HEADS = (0, 2, 5, 7)
_SWP_HEADS = (1, 3, 4, 6)


def _prompt_mixer_kernel(sinks_ref, x_next_ref, x_ref, g1_ref, win_ref, cos_ref, sin_ref, lng_ref, lnb_ref, ws_ref,
                         bsf_ref, wout_ref, g2_ref, wrt_ref, brt_ref, upper_ref, lpad_ref,
                         xmid_ref, xl_ref, meta_ref, tab_ref, kout_ref, vout_ref,
                         z_a, z_b, mix_ref, k_n, k_s, v_n, v_s, q_nat, q_swp, s_ref, p_ref, vn_ref, h_ref, h2_ref,
                         *, tiles_per_seq):
    step = pl.program_id(0)
    t = jnp.maximum(step - 1, 0) % tiles_per_seq
    kv_bufs = (k_n, k_s, v_n, v_s)

    @pl.when(step == 0)
    def _():
        z_b[...] = jnp.zeros_like(z_b)

    @pl.when(t == 0)
    def _():
        for ref in kv_bufs:
            ref[0:WINDOW, :] = jnp.zeros((WINDOW, KV_WIDTH), BF16)

    args = (t, sinks_ref, x_next_ref, x_ref, g1_ref, win_ref, cos_ref, sin_ref, lng_ref, lnb_ref, ws_ref, bsf_ref,
            wout_ref, g2_ref, wrt_ref, brt_ref, upper_ref, lpad_ref, xmid_ref, xl_ref, meta_ref, tab_ref, kout_ref,
            vout_ref, kv_bufs, q_nat, q_swp, s_ref, p_ref, vn_ref, h_ref, h2_ref)

    @pl.when(step % 2 == 0)
    def _():
        _prompt_mixer_body(z_a, z_b, mix_ref, *args)

    @pl.when(step % 2 == 1)
    def _():
        _prompt_mixer_body(z_b, z_a, mix_ref, *args)


def _prompt_mixer_body(z_next, z_ref, mix_ref, t, sinks_ref, x_next_ref, x_ref, g1_ref, win_ref, cos_ref,
                       sin_ref, lng_ref, lnb_ref, ws_ref, bsf_ref, wout_ref, g2_ref, wrt_ref, brt_ref, upper_ref,
                       lpad_ref, xmid_ref, xl_ref, meta_ref, tab_ref, kout_ref, vout_ref, kv_bufs,
                       q_nat, q_swp, s_ref, p_ref, vn_ref, h_ref, h2_ref):
    k_n, k_s, v_n, v_s = kv_bufs
    cos = cos_ref[...]
    sin = sin_ref[...]
    lane = lax.broadcasted_iota(jnp.int32, (WINDOW, LANES), 1)
    lo = lane < HEAD_DIM
    row = lax.broadcasted_iota(jnp.int32, (WINDOW, WINDOW), 0)
    col = lax.broadcasted_iota(jnp.int32, (WINDOW, WINDOW), 1)
    mask_cur = col <= row
    mask_prev_band = col >= row
    mask_prev_first = jnp.logical_and(mask_prev_band, (jnp.zeros_like(row) + t) > 0)
    mask_band = jnp.concatenate([mask_prev_band, mask_cur], axis=1)
    mask_first = jnp.concatenate([mask_prev_first, mask_cur], axis=1)

    cq = cos * np.float32(HEAD_DIM ** -0.5)
    sq = sin * np.float32(HEAD_DIM ** -0.5)
    kf = _rope(z_ref[:, C_K:C_K + KV_WIDTH], cos, sin)
    vf = z_ref[:, C_V:C_V + KV_WIDTH]
    k_n[WINDOW:, :] = kf.astype(BF16)
    k_s[WINDOW:, :] = pltpu.roll(kf, HEAD_DIM, 1).astype(BF16)
    v_n[WINDOW:, :] = vf.astype(BF16)
    v_s[WINDOW:, :] = pltpu.roll(vf, HEAD_DIM, 1).astype(BF16)

    kout_ref[...] = kf[TOK_TILE - WINDOW:]
    vout_ref[...] = vf[TOK_TILE - WINDOW:]

    lo_t = lax.broadcasted_iota(jnp.int32, (TOK_TILE, LANES), 1) < HEAD_DIM
    for m in range(N_Q_HEADS // 2):
        qc = _rope(z_ref[:, C_Q + m * LANES:C_Q + (m + 1) * LANES], cq, sq)
        for hd, qh in ((2 * m, jnp.where(lo_t, qc, 0.0)), (2 * m + 1, jnp.where(lo_t, 0.0, qc))):
            if hd in _NAT_HEADS:
                q_nat[_NAT_HEADS.index(hd)] = qh.astype(BF16)
            else:
                q_swp[_SWP_HEADS.index(hd)] = qh.astype(BF16)

    stacks = ((q_nat, _NAT_HEADS, k_n, v_n), (q_swp, _SWP_HEADS, k_s, v_s))
    proj_cols = 2 * LANES

    def project_next(c):
        cols = slice(c * proj_cols, (c + 1) * proj_cols)
        z_next[:, cols] = _dot(h_ref[...], win_ref[:, cols])

    def scores(j):
        rows = slice(j * WINDOW, (j + 1) * WINDOW)
        keys = slice(j * WINDOW, (j + 2) * WINDOW)
        for si, (q_ref, _, k_buf, _) in enumerate(stacks):
            q_stack = jnp.concatenate([q_ref[i, rows, :] for i in range(len(_NAT_HEADS))], axis=0)
            s_ref[2 * j + si] = _dot_nt(q_stack, k_buf[keys, :])

    def softmax(j):
        mask = mask_first if j == 0 else mask_band
        for si, (_, heads, _, _) in enumerate(stacks):
            for i, hd in enumerate(heads):
                pr = slice(i * WINDOW, (i + 1) * WINDOW)
                p_ref[2 * j + si, pr, :] = _softmax_with_sink(
                    jnp.where(mask, s_ref[2 * j + si, pr, :], NEG_INF), sinks_ref[hd]).astype(BF16)

    def attend(j):
        rows = slice(j * WINDOW, (j + 1) * WINDOW)
        keys = slice(j * WINDOW, (j + 2) * WINDOW)
        o_nat = _dot(p_ref[2 * j], v_n[keys, :])
        o_swp = _dot(p_ref[2 * j + 1], v_s[keys, :])
        for m in range(N_Q_HEADS // 2):
            pr = slice(m * WINDOW, (m + 1) * WINDOW)
            even_nat = (2 * m) in _NAT_HEADS
            att = jnp.where(lo, o_nat[pr], o_swp[pr]) if even_nat else jnp.where(lo, o_swp[pr], o_nat[pr])
            mix_ref[rows, m * LANES:(m + 1) * LANES] = att.astype(BF16)

    def gmlp(m):
        cs = slice(m * LANES, (m + 1) * LANES)
        w0 = jnp.where(mask_cur, ws_ref[2 * m], 0.0).astype(BF16)
        w1 = jnp.where(mask_cur, ws_ref[2 * m + 1], 0.0).astype(BF16)
        wcat = jnp.concatenate([w0, w1], axis=1)
        for j in range(BLOCKS_PER_TILE):
            rows = slice(j * WINDOW, (j + 1) * WINDOW)
            vcol = vn_ref[rows, cs]
            rhs = jnp.concatenate([jnp.where(lo, vcol, jnp.zeros_like(vcol)),
                                   jnp.where(lo, jnp.zeros_like(vcol), vcol)], axis=0)
            sp = _dot(wcat, rhs) + bsf_ref[:, cs]
            u = _gelu(z_ref[rows, C_U + m * LANES:C_U + (m + 1) * LANES])
            mix_ref[rows, ATTN_WIDTH + m * LANES:ATTN_WIDTH + (m + 1) * LANES] = (u * sp).astype(BF16)

    def project_out(c):
        cols = slice(c * proj_cols, (c + 1) * proj_cols)
        xmid_ref[:, cols] = x_ref[:, cols] + _dot(mix_ref[...], wout_ref[:, cols])

    h_ref[...] = _rmsnorm(x_next_ref[...], g1_ref[...]).astype(BF16)
    project_next(0)
    project_next(1)
    for j in range(BLOCKS_PER_TILE):
        scores(j)
    vn_ref[...] = _layernorm(_gelu(z_ref[:, C_VG:C_VG + GMLP_WIDTH]), lng_ref[...], lnb_ref[...]).astype(BF16)
    project_next(2)
    softmax(0)
    project_next(3)
    softmax(1)
    project_next(4)
    softmax(2)
    project_next(5)
    softmax(3)
    project_next(6)
    for j in range(BLOCKS_PER_TILE):
        attend(j)
    for m in range(GMLP_GROUPS // 2):
        gmlp(m)
    for c in range(D_MODEL // proj_cols):
        project_out(c)
    h2_ref[...] = _rmsnorm(xmid_ref[...], g2_ref[...]).astype(BF16)
    d1, d2 = _sort_plan(_route(h2_ref[...], wrt_ref, brt_ref), upper_ref, lpad_ref, meta_ref, tab_ref)
    for c in range(SORT_ROWS // SORT_CHUNK):
        _sort_chunk(c, d1, d2, h2_ref[...], xl_ref)

    for ref in kv_bufs:
        ref[0:WINDOW, :] = ref[TOK_TILE:TOK_TILE + WINDOW, :]


def _sort_plan(slab, upper_ref, lpad_ref, meta_ref, tab_ref):
    n_tok = slab.shape[1]
    e1, e2 = slab[0:1], slab[1:2]
    row32 = lax.broadcasted_iota(jnp.int32, (N_EXPERTS, n_tok), 0).astype(F32)
    sel1 = row32 == e1
    sel2 = row32 == e2
    onehot = jnp.where(sel1, 1.0, jnp.where(sel2, 1.0, 0.0))
    earlier = _dot(onehot.astype(BF16), upper_ref[...])
    cnt = jnp.sum(onehot, axis=1, keepdims=True)
    pc = jnp.floor((cnt + (ROW_GRANULE - 1)) * (1.0 / ROW_GRANULE)) * ROW_GRANULE
    pc_b = jnp.broadcast_to(pc, (N_EXPERTS, LANES))
    pc_pad = jnp.concatenate([pc_b, jnp.zeros((LANES - N_EXPERTS, LANES), F32)], axis=0).astype(BF16)
    start = _dot(lpad_ref[...], pc_pad)
    base = start[:, 0:1] + earlier
    d1 = jnp.sum(jnp.where(sel1, base, 0.0), axis=0, keepdims=True)
    d2 = jnp.sum(jnp.where(sel2, base, 0.0), axis=0, keepdims=True)
    row8 = lax.broadcasted_iota(jnp.int32, (8, n_tok), 0)
    meta_ref[...] = jnp.where(row8 == 0, d1, jnp.where(row8 == 1, d2, jnp.where(row8 >= 4, 0.0, slab)))
    lane = lax.broadcasted_iota(jnp.int32, (N_EXPERTS, LANES), 1)
    tab_ref[...] = jnp.where(lane == 0, pc_b, jnp.where(lane == 1, start, 0.0))
    return d1, d2


def _sort_chunk(c, d1, d2, h2b, xl_ref):
    n_tok = h2b.shape[0]
    r_iota = (lax.broadcasted_iota(jnp.int32, (SORT_CHUNK, n_tok), 0) + c * SORT_CHUNK).astype(F32)
    perm = jnp.where(r_iota == d1, 1.0, jnp.where(r_iota == d2, 1.0, 0.0)).astype(BF16)
    xl_ref[c * SORT_CHUNK:(c + 1) * SORT_CHUNK, :] = _dot(perm, h2b).astype(BF16)


def _sample_mixer_kernel(sinks_ref, x_ref, ck_ref, cv_ref, g1_ref, win_ref, cos_ref, sin_ref, lng_ref, lnb_ref,
                         ws0_ref, bs0_ref, wout_ref, g2_ref, wrt_ref, brt_ref, upper_ref, lpad_ref, xl_in_hbm,
                         xmid_ref, meta_ref, tab_ref, kout_ref, vout_ref, vnout_ref, xl_hbm,
                         mix_ref, xl_tile, xl_sem):
    del xl_in_hbm
    n_seq = x_ref.shape[0]
    seq_chunk = 16
    x = x_ref[...]
    h = _rmsnorm(x, g1_ref[...]).astype(BF16)
    z = _dot(h, win_ref[...])
    cos = cos_ref[...]
    sin = sin_ref[...]
    scale = np.float32(HEAD_DIM ** -0.5)
    lane = lax.broadcasted_iota(jnp.int32, (n_seq, LANES), 1)
    lo = lane < HEAD_DIM
    kf = _rope(z[:, C_K:C_K + KV_WIDTH], cos, sin)
    vf = z[:, C_V:C_V + KV_WIDTH]
    kout_ref[...] = kf
    vout_ref[...] = vf
    kb = kf.astype(BF16).astype(F32)
    vb = vf.astype(BF16).astype(F32)

    q_heads = []
    for hd in range(N_Q_HEADS):
        m = hd // 2
        qc = _rope(z[:, C_Q + m * LANES:C_Q + (m + 1) * LANES], cos, sin) * scale
        keep = lo if hd % 2 == 0 else ~lo
        qm = jnp.where(keep, qc, 0.0)
        if (hd % 2) != (hd // (N_Q_HEADS // N_KV_HEADS)):
            qm = pltpu.roll(qm, HEAD_DIM, 1)
        q_heads.append(qm.astype(BF16))

    s_new = [jnp.sum(q_heads[hd].astype(F32) * kb, axis=-1, keepdims=True) for hd in range(N_Q_HEADS)]

    rr = lax.broadcasted_iota(jnp.int32, (N_Q_HEADS * seq_chunk, seq_chunk * WINDOW), 0)
    cc = lax.broadcasted_iota(jnp.int32, (N_Q_HEADS * seq_chunk, seq_chunk * WINDOW), 1)
    same_seq = (rr % seq_chunk) == (cc // WINDOW)
    kv_lo = lax.broadcasted_iota(jnp.int32, (seq_chunk, LANES), 1) < HEAD_DIM

    for c in range(n_seq // seq_chunk):
        sr = slice(c * seq_chunk, (c + 1) * seq_chunk)
        kc = ck_ref[sr].reshape(seq_chunk * WINDOW, KV_WIDTH).astype(BF16)
        vc = cv_ref[sr].reshape(seq_chunk * WINDOW, KV_WIDTH).astype(BF16)
        qs = jnp.concatenate([q_heads[hd][sr] for hd in range(N_Q_HEADS)], axis=0)
        s = jnp.where(same_seq, _dot_nt(qs, kc), NEG_INF)
        sn = jnp.concatenate([s_new[hd][sr] for hd in range(N_Q_HEADS)], axis=0)
        sink = jnp.concatenate([jnp.full((seq_chunk, 1), sinks_ref[hd], F32) for hd in range(N_Q_HEADS)], axis=0)
        m = jnp.maximum(jnp.maximum(jnp.max(s, axis=-1, keepdims=True), sn), sink)
        e = jnp.exp(s - m)
        en = jnp.exp(sn - m)
        inv = 1.0 / (jnp.sum(e, axis=-1, keepdims=True) + en + jnp.exp(sink - m))
        o = _dot((e * inv).astype(BF16), vc)
        pn = (en * inv).astype(BF16).astype(F32)
        for mcol in range(N_Q_HEADS // 2):
            halves = []
            for hd in (2 * mcol, 2 * mcol + 1):
                oh = o[hd * seq_chunk:(hd + 1) * seq_chunk] + pn[hd * seq_chunk:(hd + 1) * seq_chunk] * vb[sr]
                if (hd % 2) != (hd // (N_Q_HEADS // N_KV_HEADS)):
                    oh = pltpu.roll(oh, HEAD_DIM, 1)
                halves.append(oh)
            att = jnp.where(kv_lo, halves[0], halves[1])
            mix_ref[sr, mcol * LANES:(mcol + 1) * LANES] = att.astype(BF16)

    u = _gelu(z[:, C_U:C_U + GMLP_WIDTH])
    vn = _layernorm(_gelu(z[:, C_VG:C_VG + GMLP_WIDTH]), lng_ref[...], lnb_ref[...])
    vnout_ref[...] = vn
    sp = ws0_ref[...].astype(BF16).astype(F32) * vn.astype(BF16).astype(F32) + bs0_ref[...]
    mix_ref[:, ATTN_WIDTH:] = (u * sp).astype(BF16)

    xmid = x + _dot(mix_ref[...], wout_ref[...])
    xmid_ref[...] = xmid
    h2b = _rmsnorm(xmid, g2_ref[...]).astype(BF16)
    d1, d2 = _sort_plan(_route(h2b, wrt_ref, brt_ref), upper_ref, lpad_ref, meta_ref, tab_ref)
    for c in range(SORT_ROWS // SORT_CHUNK):
        _sort_chunk(c, d1, d2, h2b, xl_tile)
    copy = pltpu.make_async_copy(xl_tile, xl_hbm.at[xl_hbm.shape[0] - 1], xl_sem.at[0])
    copy.start()
    copy.wait()


def _grouped_ffn_kernel(texp_ref, gsrc_ref, sdst_ref, ntiles_ref, xl_hbm, wg_ref, wu_ref, wd_ref, yl_hbm,
                        xbuf, ybuf, wg_b, wu_b, wd_b, gsem, ssem):
    j = pl.program_id(0)
    n_tiles = ntiles_ref[0]

    def gather_copy(tile, s, b):
        return pltpu.make_async_copy(xl_hbm.at[gsrc_ref[tile * FFN_SLOTS + s]], xbuf.at[b, s], gsem.at[b])

    def scatter_copy(tile, s):
        b = (tile + SCATTER_BUFS) % SCATTER_BUFS
        return pltpu.make_async_copy(ybuf.at[b, s], yl_hbm.at[sdst_ref[(tile + 1) * FFN_SLOTS + s]], ssem.at[b])

    @pl.when(j < n_tiles)
    def _():
        gb = j % GATHER_BUFS
        ahead = GATHER_BUFS - 1
        nxt = j + ahead
        nxt_b = nxt % GATHER_BUFS

        @pl.when(j == 0)
        def _():
            ybuf[SCATTER_BUFS - 1] = jnp.zeros(ybuf.shape[1:], BF16)
            for k in range(ahead):
                for s in range(FFN_SLOTS):
                    gather_copy(k, s, k).start()

        for s in range(FFN_SLOTS):
            gather_copy(j, s, gb).wait()

        @pl.when(j >= SCATTER_BUFS - 1)
        def _():
            for s in range(FFN_SLOTS):
                scatter_copy(j - SCATTER_BUFS, s).wait()

        @pl.when(jnp.logical_or(j == 0, texp_ref[j] != texp_ref[jnp.maximum(j - 1, 0)]))
        def _():
            wg_b[...] = wg_ref[...].astype(BF16)
            wu_b[...] = wu_ref[...].astype(BF16)
            wd_b[...] = wd_ref[...].astype(BF16)

        x = xbuf[gb].reshape(FFN_ROWS, D_MODEL)
        gate = _dot(x, wg_b[...])
        for s in range(FFN_SLOTS):
            scatter_copy(j - 1, s).start()
        up = _dot(x, wu_b[...])
        for s in range(FFN_SLOTS):
            gather_copy(nxt, s, nxt_b).start()
        hid = (gate * (1.0 / (1.0 + jnp.exp(-gate))) * up).astype(BF16)
        ybuf[j % SCATTER_BUFS] = _dot(hid, wd_b[...]).astype(BF16).reshape(FFN_SLOTS, ROW_GRANULE, D_MODEL)

        @pl.when(j == n_tiles - 1)
        def _():
            for k in range(1, GATHER_BUFS):
                for s in range(FFN_SLOTS):
                    gather_copy(nxt, s, (j + k) % GATHER_BUFS).wait()
            for s in range(FFN_SLOTS):
                scatter_copy(j, s).start()
            for back in range(SCATTER_BUFS):
                @pl.when(j - back >= -1)
                def _():
                    for s in range(FFN_SLOTS):
                        scatter_copy(j - back, s).wait()


def _combine_kernel(xmid_ref, yl_ref, meta_ref, gf_ref, y_ref):
    n_tok = xmid_ref.shape[0]
    meta = meta_ref[...]
    meta_t = jnp.concatenate([meta, jnp.zeros((LANES - 8, n_tok), F32)], axis=0).T
    d1, d2, w1, w2 = meta_t[:, 0:1], meta_t[:, 1:2], meta_t[:, 2:3], meta_t[:, 3:4]
    acc = xmid_ref[...]
    for c in range(SORT_ROWS // SORT_CHUNK):
        r_iota = (lax.broadcasted_iota(jnp.int32, (n_tok, SORT_CHUNK), 1) + c * SORT_CHUNK).astype(F32)
        unsort = jnp.where(r_iota == d1, w1, jnp.where(r_iota == d2, w2, 0.0)).astype(BF16)
        acc = acc + _dot(unsort, yl_ref[c * SORT_CHUNK:(c + 1) * SORT_CHUNK, :])
    y_ref[...] = _rmsnorm(acc, gf_ref[...])


def _ffn_schedule(tab):
    n_tok_tiles = tab.shape[0]
    strips = (tab[:, :, 0] * (1.0 / ROW_GRANULE)).astype(jnp.int32)
    starts = (tab[:, :, 1] * (1.0 / ROW_GRANULE)).astype(jnp.int32)
    cnt = strips.T
    row0 = starts.T
    cs = jnp.cumsum(cnt, axis=1) - cnt
    n_str = jnp.sum(cnt, axis=1)
    np_str = (n_str + FFN_SLOTS - 1) // FFN_SLOTS * FFN_SLOTS
    ends = jnp.cumsum(np_str)
    base = ends - np_str
    n_steps_max = _ffn_steps_max(n_tok_tiles) + GATHER_BUFS - 1
    step0 = jnp.arange(n_steps_max, dtype=jnp.int32) * FFN_SLOTS
    stream = jnp.minimum(jnp.sum(ends[None, :] <= step0[:, None], axis=1), N_EXPERTS - 1)
    pick = stream[:, None] == jnp.arange(N_EXPERTS, dtype=jnp.int32)[None, :]
    sel = lambda x: jnp.sum(jnp.where(pick[:, :, None], x[None], 0), axis=1)
    cs_j, cnt_j, row0_j = sel(cs), sel(cnt), sel(row0)
    base_j = jnp.sum(jnp.where(pick, base[None, :], 0), axis=1)
    q = step0[:, None] + jnp.arange(FFN_SLOTS, dtype=jnp.int32)[None, :] - base_j[:, None]
    reached = cs_j[:, None, :] <= q[:, :, None]
    last = lambda x: jnp.sum(jnp.where(reached, jnp.diff(x, axis=1, prepend=0)[:, None, :], 0), axis=2)
    tile_idx = jnp.sum(reached, axis=2).astype(jnp.int32) - 1
    g = q - last(cs_j)
    valid = g < last(cnt_j)
    granule = tile_idx * GRANULES_PER_TILE + last(row0_j) + g
    assert n_tok_tiles >= 2 * FFN_SLOTS
    slot = jnp.arange(FFN_SLOTS, dtype=jnp.int32)[None, :]
    parity = jnp.arange(n_steps_max, dtype=jnp.int32)[:, None] % 2
    pad_dst = lambda par: (slot + par * FFN_SLOTS) * GRANULES_PER_TILE + GRANULES_PER_TILE - 1
    gsrc = jnp.where(valid, granule, slot * GRANULES_PER_TILE + GRANULES_PER_TILE - 2).astype(jnp.int32)
    sdst = jnp.where(valid, granule, pad_dst(parity))
    sdst = jnp.concatenate([pad_dst(1), sdst], axis=0)
    texp = stream.astype(jnp.int32)
    n_steps = (ends[-1] // FFN_SLOTS).astype(jnp.int32).reshape(1)
    return texp, gsrc.reshape(-1), sdst.astype(jnp.int32).reshape(-1), n_steps


assert GRANULES_PER_TILE - (2 * TOK_TILE + N_EXPERTS * (ROW_GRANULE - 1)) // ROW_GRANULE >= 2


def _ffn_steps_max(n_tok_tiles):
    return -(-(n_tok_tiles * GRANULES_PER_TILE + N_EXPERTS * (FFN_SLOTS - 1)) // FFN_SLOTS)


def _vmem_limit(n_bytes):
    return int(min(n_bytes, V7X_VMEM_BYTES - 4 * 1024 * 1024))


def _rope_tables(pos):
    inv_freq = ROPE_THETA ** (-jnp.arange(HALF, dtype=F32) * 2.0 / HEAD_DIM)
    ang = pos.astype(F32)[:, None] * inv_freq[None, :]
    cos, sin = jnp.cos(ang), jnp.sin(ang)
    reps = LANES // HEAD_DIM
    return jnp.tile(jnp.concatenate([cos, cos], axis=1), (1, reps)), jnp.tile(jnp.concatenate([-sin, sin], axis=1), (1, reps))


def _moe_combine(xmid, yl, meta, gf, n_tok, first_tile, n_tiles):
    return pl.pallas_call(
        _combine_kernel,
        grid=(n_tiles,),
        in_specs=[
            pl.BlockSpec((n_tok, D_MODEL), lambda i: (i, 0)),
            pl.BlockSpec((SORT_ROWS, D_MODEL), lambda i: (first_tile + i, 0)),
            pl.BlockSpec((None, 8, n_tok), lambda i: (i, 0, 0)),
            pl.BlockSpec((1, D_MODEL), lambda i: (0, 0)),
        ],
        out_specs=pl.BlockSpec((n_tok, D_MODEL), lambda i: (i, 0)),
        out_shape=jax.ShapeDtypeStruct((n_tiles * n_tok, D_MODEL), F32),
        compiler_params=pltpu.CompilerParams(
            dimension_semantics=("arbitrary",),
            vmem_limit_bytes=_vmem_limit(40 * 1024 * 1024)),
        name="moe_combine",
    )(xmid, yl, meta, gf)


def kernel(x_prompt, x_sample, cache_swa_k, cache_swa_v, norm_mix_g, w_in, attn_sinks, gmlp_ln_g, gmlp_ln_b,
           gmlp_w_s, gmlp_b_s, w_out, norm_ffn_g, router_group_w, router_group_b, router_expert_w,
           router_expert_b, expert_w_gate, expert_w_up, expert_w_down, final_norm_g):
    assert norm_mix_g.shape[0] == 1, "single-layer trunk"
    batch, seq, _ = x_prompt.shape
    dec_batch = x_sample.shape[0]
    assert x_sample.shape[1] == 1 and seq % TOK_TILE == 0

    win_ext = w_in[0].astype(BF16)
    wout = w_out[0].astype(BF16)
    g1 = norm_mix_g[0][None, :]
    g2 = norm_ffn_g[0][None, :]
    gf = final_norm_g[None, :]
    lng = gmlp_ln_g[0][None, :]
    lnb = gmlp_ln_b[0][None, :]
    sinks = attn_sinks[0]
    ws = gmlp_w_s[0]
    group_dim = GMLP_WIDTH // GMLP_GROUPS
    bsf = jnp.repeat(gmlp_b_s[0].T, group_dim, axis=1)
    ws0 = jnp.repeat(ws[:, 0, 0], group_dim)[None, :]
    bs0 = jnp.repeat(gmlp_b_s[0][:, 0], group_dim)[None, :]
    wrt = jnp.zeros((ROUTER_ROWS, D_MODEL), F32)
    wrt = wrt.at[0:N_GROUPS].set(router_group_w[0].T)
    wrt = wrt.at[8:8 + N_EXPERTS].set(router_expert_w[0].reshape(D_MODEL, N_EXPERTS).T).astype(BF16)
    brt = jnp.full((ROUTER_ROWS, 1), NEG_INF, F32)
    brt = brt.at[0:N_GROUPS, 0].set(router_group_b[0])
    brt = brt.at[8:8 + N_EXPERTS, 0].set(router_expert_b[0].reshape(N_EXPERTS))
    wg, wu, wd = expert_w_gate[0], expert_w_up[0], expert_w_down[0]
    cos_p, sin_p = _rope_tables(jnp.arange(seq, dtype=jnp.int32))
    cos_s, sin_s = _rope_tables(PAST_LEN + jnp.arange(1, dtype=jnp.int32))

    full = lambda shape: pl.BlockSpec(shape, lambda *_: (0,) * len(shape))
    smem = pl.BlockSpec(memory_space=pltpu.SMEM)
    n_tiles = seq // TOK_TILE

    upper = jnp.triu(jnp.ones((TOK_TILE, TOK_TILE), BF16), k=1)
    lpad = (jnp.arange(LANES)[None, :] < jnp.arange(N_EXPERTS)[:, None]).astype(BF16)
    n_tok_tiles = batch * n_tiles

    x2d = x_prompt.reshape(batch * seq, D_MODEL)
    tile_a = lambda s: jnp.minimum(s, n_tok_tiles - 1)
    cur = lambda s: jnp.maximum(s - 1, 0)
    tile_b = cur
    xmid_p, xl_p, meta_p, tab_p, k_p, v_p = pl.pallas_call(
        functools.partial(_prompt_mixer_kernel, tiles_per_seq=n_tiles),
        grid=(n_tok_tiles + 1,),
        in_specs=[
            smem,
            pl.BlockSpec((TOK_TILE, D_MODEL), lambda s: (tile_a(s), 0)),
            pl.BlockSpec((TOK_TILE, D_MODEL), lambda s: (cur(s), 0)),
            full((1, D_MODEL)),
            full((D_MODEL, IN_WIDTH)),
            pl.BlockSpec((TOK_TILE, LANES), lambda s: (tile_b(s) % n_tiles, 0)),
            pl.BlockSpec((TOK_TILE, LANES), lambda s: (tile_b(s) % n_tiles, 0)),
            full((1, GMLP_WIDTH)),
            full((1, GMLP_WIDTH)),
            full((GMLP_GROUPS, CHUNK, CHUNK)),
            full((CHUNK, GMLP_WIDTH)),
            full((D_MODEL, D_MODEL)),
            full((1, D_MODEL)),
            full((ROUTER_ROWS, D_MODEL)),
            full((ROUTER_ROWS, 1)),
            full((TOK_TILE, TOK_TILE)),
            full((N_EXPERTS, LANES)),
        ],
        out_specs=[
            pl.BlockSpec((TOK_TILE, D_MODEL), lambda s: (cur(s), 0)),
            pl.BlockSpec((None, SORT_ROWS, D_MODEL), lambda s: (jnp.where(s == 0, n_tok_tiles, s - 1), 0, 0)),
            pl.BlockSpec((None, 8, TOK_TILE), lambda s: (cur(s), 0, 0)),
            pl.BlockSpec((None, N_EXPERTS, LANES), lambda s: (cur(s), 0, 0)),
            pl.BlockSpec((None, WINDOW, KV_WIDTH), lambda s: (tile_b(s) // n_tiles, 0, 0)),
            pl.BlockSpec((None, WINDOW, KV_WIDTH), lambda s: (tile_b(s) // n_tiles, 0, 0)),
        ],
        out_shape=[
            jax.ShapeDtypeStruct((batch * seq, D_MODEL), F32),
            jax.ShapeDtypeStruct((n_tok_tiles + 1, SORT_ROWS, D_MODEL), BF16),
            jax.ShapeDtypeStruct((n_tok_tiles, 8, TOK_TILE), F32),
            jax.ShapeDtypeStruct((n_tok_tiles, N_EXPERTS, LANES), F32),
            jax.ShapeDtypeStruct((batch, WINDOW, KV_WIDTH), F32),
            jax.ShapeDtypeStruct((batch, WINDOW, KV_WIDTH), F32),
        ],
        scratch_shapes=[
            pltpu.VMEM((TOK_TILE, IN_WIDTH), F32),
            pltpu.VMEM((TOK_TILE, IN_WIDTH), F32),
            pltpu.VMEM((TOK_TILE, D_MODEL), BF16),
        ] + [pltpu.VMEM((WINDOW + TOK_TILE, KV_WIDTH), BF16)] * 4 + [
            pltpu.VMEM((len(_NAT_HEADS), TOK_TILE, LANES), BF16),
            pltpu.VMEM((len(_SWP_HEADS), TOK_TILE, LANES), BF16),
            pltpu.VMEM((2 * BLOCKS_PER_TILE, len(_NAT_HEADS) * WINDOW, 2 * WINDOW), F32),
            pltpu.VMEM((2 * BLOCKS_PER_TILE, len(_NAT_HEADS) * WINDOW, 2 * WINDOW), BF16),
            pltpu.VMEM((TOK_TILE, GMLP_WIDTH), BF16),
            pltpu.VMEM((TOK_TILE, D_MODEL), BF16),
            pltpu.VMEM((TOK_TILE, D_MODEL), BF16),
        ],
        compiler_params=pltpu.CompilerParams(
            dimension_semantics=("arbitrary",),
            vmem_limit_bytes=_vmem_limit(56 * 1024 * 1024)),
        name="prompt_mixer",
    )(sinks, x2d, x2d, g1, win_ext, cos_p, sin_p, lng, lnb, ws, bsf, wout, g2, wrt, brt, upper, lpad)

    xs = x_sample.reshape(dec_batch, D_MODEL)
    ck = cache_swa_k[0].reshape(dec_batch, WINDOW, KV_WIDTH)
    cv = cache_swa_v[0].reshape(dec_batch, WINDOW, KV_WIDTH)
    vmem = pl.BlockSpec(memory_space=pltpu.VMEM)
    hbm = pl.BlockSpec(memory_space=pl.ANY)
    xmid_s, meta_s, tab_s, k_s, v_s, vn_s, xl_all = pl.pallas_call(
        _sample_mixer_kernel,
        in_specs=[smem] + [vmem] * 17 + [hbm],
        out_specs=[vmem] * 6 + [hbm],
        out_shape=[
            jax.ShapeDtypeStruct((dec_batch, D_MODEL), F32),
            jax.ShapeDtypeStruct((8, dec_batch), F32),
            jax.ShapeDtypeStruct((N_EXPERTS, LANES), F32),
            jax.ShapeDtypeStruct((dec_batch, KV_WIDTH), F32),
            jax.ShapeDtypeStruct((dec_batch, KV_WIDTH), F32),
            jax.ShapeDtypeStruct((dec_batch, GMLP_WIDTH), F32),
            jax.ShapeDtypeStruct(xl_p.shape, BF16),
        ],
        scratch_shapes=[pltpu.VMEM((dec_batch, D_MODEL), BF16), pltpu.VMEM((SORT_ROWS, D_MODEL), BF16),
                        pltpu.SemaphoreType.DMA((1,))],
        compiler_params=pltpu.CompilerParams(vmem_limit_bytes=_vmem_limit(56 * 1024 * 1024)),
        input_output_aliases={18: 6},
        name="sample_mixer",
    )(sinks, xs, ck, cv, g1, win_ext, cos_s, sin_s, lng, lnb, ws0, bs0, wout, g2, wrt, brt,
      upper[:dec_batch, :dec_batch], lpad, xl_p)

    n_all_tiles = n_tok_tiles + 1
    texp, gsrc, sdst, n_steps = _ffn_schedule(jnp.concatenate([tab_p, tab_s[None]], axis=0))
    yl_all = pl.pallas_call(
        _grouped_ffn_kernel,
        grid_spec=pltpu.PrefetchScalarGridSpec(
            num_scalar_prefetch=4,
            grid=(_ffn_steps_max(n_all_tiles),),
            in_specs=[
                pl.BlockSpec(memory_space=pl.ANY),
                pl.BlockSpec((None, D_MODEL, D_EXPERT), lambda j, te, gs, sd, ns: (te[j], 0, 0)),
                pl.BlockSpec((None, D_MODEL, D_EXPERT), lambda j, te, gs, sd, ns: (te[j], 0, 0)),
                pl.BlockSpec((None, D_EXPERT, D_MODEL), lambda j, te, gs, sd, ns: (te[j], 0, 0)),
            ],
            out_specs=pl.BlockSpec(memory_space=pl.ANY),
            scratch_shapes=[
                pltpu.VMEM((GATHER_BUFS, FFN_SLOTS, ROW_GRANULE, D_MODEL), BF16),
                pltpu.VMEM((SCATTER_BUFS, FFN_SLOTS, ROW_GRANULE, D_MODEL), BF16),
                pltpu.VMEM((D_MODEL, D_EXPERT), BF16),
                pltpu.VMEM((D_MODEL, D_EXPERT), BF16),
                pltpu.VMEM((D_EXPERT, D_MODEL), BF16),
                pltpu.SemaphoreType.DMA((GATHER_BUFS,)),
                pltpu.SemaphoreType.DMA((SCATTER_BUFS,)),
            ],
        ),
        out_shape=jax.ShapeDtypeStruct((n_all_tiles * GRANULES_PER_TILE, ROW_GRANULE, D_MODEL), BF16),
        compiler_params=pltpu.CompilerParams(
            dimension_semantics=("arbitrary",),
            vmem_limit_bytes=_vmem_limit(32 * 1024 * 1024)),
        input_output_aliases={4: 0},
        name="grouped_ffn",
    )(texp, gsrc, sdst, n_steps, xl_all.reshape(n_all_tiles * GRANULES_PER_TILE, ROW_GRANULE, D_MODEL), wg, wu, wd)
    yl_all = yl_all.reshape(n_all_tiles * SORT_ROWS, D_MODEL)

    y_p = _moe_combine(xmid_p, yl_all, meta_p, gf, TOK_TILE, 0, n_tok_tiles)
    y_s = _moe_combine(xmid_s, yl_all, meta_s[None], gf, dec_batch, n_tok_tiles, 1)

    return (y_p.reshape(batch, seq, D_MODEL),
            y_s.reshape(dec_batch, 1, D_MODEL),
            k_p.reshape(1, batch, WINDOW, N_KV_HEADS, HEAD_DIM),
            v_p.reshape(1, batch, WINDOW, N_KV_HEADS, HEAD_DIM),
            k_s.reshape(1, dec_batch, 1, N_KV_HEADS, HEAD_DIM),
            v_s.reshape(1, dec_batch, 1, N_KV_HEADS, HEAD_DIM),
            vn_s.reshape(1, dec_batch, 1, GMLP_WIDTH))
```

```python
import functools

import jax
import jax.numpy as jnp
import numpy as np
from jax import lax
from jax.experimental import pallas as pl
from jax.experimental.pallas import tpu as pltpu

F32 = jnp.float32
BF16 = jnp.bfloat16

D_MODEL = 1024
HEAD_DIM = 64
HALF = HEAD_DIM // 2
N_Q_HEADS = 8
N_KV_HEADS = 2
ATTN_WIDTH = N_Q_HEADS * HEAD_DIM
KV_WIDTH = N_KV_HEADS * HEAD_DIM
WINDOW = 128
ROPE_THETA = 10000.0
GMLP_WIDTH = D_MODEL - ATTN_WIDTH
GMLP_GROUPS = 8
CHUNK = 128
N_GROUPS = 4
EXPERTS_PER_GROUP = 8
N_EXPERTS = N_GROUPS * EXPERTS_PER_GROUP
D_EXPERT = 256
EPS = 1e-6
NEG_INF = -1e30
PAST_LEN = 16384

LANES = 128
V7X_VMEM_BYTES = 64 * 1024 * 1024

C_Q = 0
C_K = C_Q + ATTN_WIDTH
C_V = C_K + KV_WIDTH
C_U = C_V + KV_WIDTH
C_VG = C_U + GMLP_WIDTH
IN_WIDTH = C_VG + GMLP_WIDTH

ROUTER_ROWS = 48
TOK_TILE = 512
BLOCKS_PER_TILE = TOK_TILE // WINDOW
ROW_GRANULE = 16
SORT_CHUNK = 512
SORT_ROWS = -(-(2 * TOK_TILE + N_EXPERTS * (ROW_GRANULE - 1)) // SORT_CHUNK) * SORT_CHUNK
GRANULES_PER_TILE = SORT_ROWS // ROW_GRANULE
FFN_ROWS = 512
FFN_SLOTS = FFN_ROWS // ROW_GRANULE
GATHER_BUFS = 3
SCATTER_BUFS = 3


def _dot(a, b):
    return jnp.dot(a, b, preferred_element_type=F32)


def _dot_nt(a, b):
    return lax.dot_general(a, b, (((1,), (1,)), ((), ())), preferred_element_type=F32)


def _gelu(x):
    return 0.5 * x * (1.0 + lax.erf(x * np.float32(np.sqrt(0.5))))


def _rmsnorm(x, g):
    return x * lax.rsqrt(jnp.mean(x * x, axis=-1, keepdims=True) + EPS) * g


def _layernorm(x, g, b):
    mu = jnp.mean(x, axis=-1, keepdims=True)
    xc = x - mu
    return xc * lax.rsqrt(jnp.mean(xc * xc, axis=-1, keepdims=True) + EPS) * g + b


def _first_argmax_rows(x, row_iota, n_rows):
    m = jnp.max(x, axis=0, keepdims=True)
    idx = jnp.min(jnp.where(x == m, row_iota, n_rows), axis=0, keepdims=True)
    return m, idx


def _route(h2b, wrt_ref, brt_ref):
    n_tok = h2b.shape[0]
    lt = _dot_nt(wrt_ref[...], h2b) + brt_ref[...]
    row8 = lax.broadcasted_iota(jnp.int32, (8, n_tok), 0).astype(F32)
    glog = lt[0:8]
    gmax, gidx = _first_argmax_rows(glog, row8, 8)
    g_w = 1.0 / jnp.sum(jnp.exp(glog - gmax), axis=0, keepdims=True)
    esel = lt[8:16]
    for g in range(1, N_GROUPS):
        esel = jnp.where(gidx == g, lt[8 + 8 * g:16 + 8 * g], esel)
    m1, i1 = _first_argmax_rows(esel, row8, 8)
    esel2 = jnp.where(row8 == i1, -jnp.inf, esel)
    m2, i2 = _first_argmax_rows(esel2, row8, 8)
    r = jnp.exp(m2 - m1)
    w1 = 1.0 / (1.0 + r)
    w2 = r / (1.0 + r)
    e1 = gidx * EXPERTS_PER_GROUP + i1
    e2 = gidx * EXPERTS_PER_GROUP + i2
    slab = jnp.where(row8 == 0, e1, 0.0)
    for r_idx, val in ((1, e2), (2, w1 * g_w), (3, w2 * g_w)):
        slab = jnp.where(row8 == r_idx, val, slab)
    return slab


def _softmax_with_sink(s, sink):
    m = jnp.maximum(jnp.max(s, axis=-1, keepdims=True), sink)
    e = jnp.exp(s - m)
    den = jnp.sum(e, axis=-1, keepdims=True) + jnp.exp(sink - m)
    return e * (1.0 / den)


def _rope(x, cos, sin_signed):
    first_half = (lax.broadcasted_iota(jnp.int32, x.shape, 1) & HALF) == 0
    partner = jnp.where(first_half, pltpu.roll(x, LANES - HALF, 1), pltpu.roll(x, HALF, 1))
    return x * cos + partner * sin_signed


_NAT_HEADS = (0, 2, 5, 7)
_SWP_HEADS = (1, 3, 4, 6)


def _prompt_mixer_kernel(sinks_ref, x_next_ref, x_ref, g1_ref, win_ref, cos_ref, sin_ref, lng_ref, lnb_ref, ws_ref,
                         bsf_ref, wout_ref, g2_ref, wrt_ref, brt_ref, upper_ref, lpad_ref,
                         xmid_ref, xl_ref, meta_ref, tab_ref, kout_ref, vout_ref,
                         z_a, z_b, mix_ref, k_n, k_s, v_n, v_s, q_nat, q_swp, s_ref, p_ref, vn_ref, h_ref, h2_ref,
                         *, tiles_per_seq):
    step = pl.program_id(0)
    t = jnp.maximum(step - 1, 0) % tiles_per_seq
    kv_bufs = (k_n, k_s, v_n, v_s)

    @pl.when(step == 0)
    def _():
        z_b[...] = jnp.zeros_like(z_b)

    @pl.when(t == 0)
    def _():
        for ref in kv_bufs:
            ref[0:WINDOW, :] = jnp.zeros((WINDOW, KV_WIDTH), BF16)

    args = (t, sinks_ref, x_next_ref, x_ref, g1_ref, win_ref, cos_ref, sin_ref, lng_ref, lnb_ref, ws_ref, bsf_ref,
            wout_ref, g2_ref, wrt_ref, brt_ref, upper_ref, lpad_ref, xmid_ref, xl_ref, meta_ref, tab_ref, kout_ref,
            vout_ref, kv_bufs, q_nat, q_swp, s_ref, p_ref, vn_ref, h_ref, h2_ref)

    @pl.when(step % 2 == 0)
    def _():
        _prompt_mixer_body(z_a, z_b, mix_ref, *args)

    @pl.when(step % 2 == 1)
    def _():
        _prompt_mixer_body(z_b, z_a, mix_ref, *args)


def _prompt_mixer_body(z_next, z_ref, mix_ref, t, sinks_ref, x_next_ref, x_ref, g1_ref, win_ref, cos_ref,
                       sin_ref, lng_ref, lnb_ref, ws_ref, bsf_ref, wout_ref, g2_ref, wrt_ref, brt_ref, upper_ref,
                       lpad_ref, xmid_ref, xl_ref, meta_ref, tab_ref, kout_ref, vout_ref, kv_bufs,
                       q_nat, q_swp, s_ref, p_ref, vn_ref, h_ref, h2_ref):
    k_n, k_s, v_n, v_s = kv_bufs
    cos = cos_ref[...]
    sin = sin_ref[...]
    lane = lax.broadcasted_iota(jnp.int32, (WINDOW, LANES), 1)
    lo = lane < HEAD_DIM
    row = lax.broadcasted_iota(jnp.int32, (WINDOW, WINDOW), 0)
    col = lax.broadcasted_iota(jnp.int32, (WINDOW, WINDOW), 1)
    mask_cur = col <= row
    mask_prev_band = col >= row
    mask_prev_first = jnp.logical_and(mask_prev_band, (jnp.zeros_like(row) + t) > 0)
    mask_band = jnp.concatenate([mask_prev_band, mask_cur], axis=1)
    mask_first = jnp.concatenate([mask_prev_first, mask_cur], axis=1)

    cq = cos * np.float32(HEAD_DIM ** -0.5)
    sq = sin * np.float32(HEAD_DIM ** -0.5)
    kf = _rope(z_ref[:, C_K:C_K + KV_WIDTH], cos, sin)
    vf = z_ref[:, C_V:C_V + KV_WIDTH]
    k_n[WINDOW:, :] = kf.astype(BF16)
    k_s[WINDOW:, :] = pltpu.roll(kf, HEAD_DIM, 1).astype(BF16)
    v_n[WINDOW:, :] = vf.astype(BF16)
    v_s[WINDOW:, :] = pltpu.roll(vf, HEAD_DIM, 1).astype(BF16)

    kout_ref[...] = kf[TOK_TILE - WINDOW:]
    vout_ref[...] = vf[TOK_TILE - WINDOW:]

    lo_t = lax.broadcasted_iota(jnp.int32, (TOK_TILE, LANES), 1) < HEAD_DIM
    for m in range(N_Q_HEADS // 2):
        qc = _rope(z_ref[:, C_Q + m * LANES:C_Q + (m + 1) * LANES], cq, sq)
        for hd, qh in ((2 * m, jnp.where(lo_t, qc, 0.0)), (2 * m + 1, jnp.where(lo_t, 0.0, qc))):
            if hd in _NAT_HEADS:
                q_nat[_NAT_HEADS.index(hd)] = qh.astype(BF16)
            else:
                q_swp[_SWP_HEADS.index(hd)] = qh.astype(BF16)

    stacks = ((q_nat, _NAT_HEADS, k_n, v_n), (q_swp, _SWP_HEADS, k_s, v_s))
    proj_cols = 2 * LANES

    def project_next(c):
        cols = slice(c * proj_cols, (c + 1) * proj_cols)
        z_next[:, cols] = _dot(h_ref[...], win_ref[:, cols])

    def scores(j):
        rows = slice(j * WINDOW, (j + 1) * WINDOW)
        keys = slice(j * WINDOW, (j + 2) * WINDOW)
        for si, (q_ref, _, k_buf, _) in enumerate(stacks):
            q_stack = jnp.concatenate([q_ref[i, rows, :] for i in range(len(_NAT_HEADS))], axis=0)
            s_ref[2 * j + si] = _dot_nt(q_stack, k_buf[keys, :])

    def softmax(j):
        mask = mask_first if j == 0 else mask_band
        for si, (_, heads, _, _) in enumerate(stacks):
            for i, hd in enumerate(heads):
                pr = slice(i * WINDOW, (i + 1) * WINDOW)
                p_ref[2 * j + si, pr, :] = _softmax_with_sink(
                    jnp.where(mask, s_ref[2 * j + si, pr, :], NEG_INF), sinks_ref[hd]).astype(BF16)

    def attend(j):
        rows = slice(j * WINDOW, (j + 1) * WINDOW)
        keys = slice(j * WINDOW, (j + 2) * WINDOW)
        o_nat = _dot(p_ref[2 * j], v_n[keys, :])
        o_swp = _dot(p_ref[2 * j + 1], v_s[keys, :])
        for m in range(N_Q_HEADS // 2):
            pr = slice(m * WINDOW, (m + 1) * WINDOW)
            even_nat = (2 * m) in _NAT_HEADS
            att = jnp.where(lo, o_nat[pr], o_swp[pr]) if even_nat else jnp.where(lo, o_swp[pr], o_nat[pr])
            mix_ref[rows, m * LANES:(m + 1) * LANES] = att.astype(BF16)

    def gmlp(m):
        cs = slice(m * LANES, (m + 1) * LANES)
        w0 = jnp.where(mask_cur, ws_ref[2 * m], 0.0).astype(BF16)
        w1 = jnp.where(mask_cur, ws_ref[2 * m + 1], 0.0).astype(BF16)
        wcat = jnp.concatenate([w0, w1], axis=1)
        for j in range(BLOCKS_PER_TILE):
            rows = slice(j * WINDOW, (j + 1) * WINDOW)
            vcol = vn_ref[rows, cs]
            rhs = jnp.concatenate([jnp.where(lo, vcol, jnp.zeros_like(vcol)),
                                   jnp.where(lo, jnp.zeros_like(vcol), vcol)], axis=0)
            sp = _dot(wcat, rhs) + bsf_ref[:, cs]
            u = _gelu(z_ref[rows, C_U + m * LANES:C_U + (m + 1) * LANES])
            mix_ref[rows, ATTN_WIDTH + m * LANES:ATTN_WIDTH + (m + 1) * LANES] = (u * sp).astype(BF16)

    def project_out(c):
        cols = slice(c * proj_cols, (c + 1) * proj_cols)
        xmid_ref[:, cols] = x_ref[:, cols] + _dot(mix_ref[...], wout_ref[:, cols])

    h_ref[...] = _rmsnorm(x_next_ref[...], g1_ref[...]).astype(BF16)
    project_next(0)
    project_next(1)
    for j in range(BLOCKS_PER_TILE):
        scores(j)
    vn_ref[...] = _layernorm(_gelu(z_ref[:, C_VG:C_VG + GMLP_WIDTH]), lng_ref[...], lnb_ref[...]).astype(BF16)
    project_next(2)
    softmax(0)
    project_next(3)
    softmax(1)
    project_next(4)
    softmax(2)
    project_next(5)
    softmax(3)
    project_next(6)
    for j in range(BLOCKS_PER_TILE):
        attend(j)
    for m in range(GMLP_GROUPS // 2):
        gmlp(m)
    for c in range(D_MODEL // proj_cols):
        project_out(c)
    h2_ref[...] = _rmsnorm(xmid_ref[...], g2_ref[...]).astype(BF16)
    d1, d2 = _sort_plan(_route(h2_ref[...], wrt_ref, brt_ref), upper_ref, lpad_ref, meta_ref, tab_ref)
    _sort_rows(d1, d2, h2_ref[...], xl_ref)

    for ref in kv_bufs:
        ref[0:WINDOW, :] = ref[TOK_TILE:TOK_TILE + WINDOW, :]


def _sort_plan(slab, upper_ref, lpad_ref, meta_ref, tab_ref):
    n_tok = slab.shape[1]
    e1, e2 = slab[0:1], slab[1:2]
    row32 = lax.broadcasted_iota(jnp.int32, (N_EXPERTS, n_tok), 0).astype(F32)
    sel1 = row32 == e1
    sel2 = row32 == e2
    onehot = jnp.where(sel1, 1.0, jnp.where(sel2, 1.0, 0.0))
    earlier = _dot(onehot.astype(BF16), upper_ref[...])
    cnt = jnp.sum(onehot, axis=1, keepdims=True)
    pc = jnp.floor((cnt + (ROW_GRANULE - 1)) * (1.0 / ROW_GRANULE)) * ROW_GRANULE
    pc_b = jnp.broadcast_to(pc, (N_EXPERTS, LANES))
    pc_pad = jnp.concatenate([pc_b, jnp.zeros((LANES - N_EXPERTS, LANES), F32)], axis=0).astype(BF16)
    start = _dot(lpad_ref[...], pc_pad)
    base = start[:, 0:1] + earlier
    d1 = jnp.sum(jnp.where(sel1, base, 0.0), axis=0, keepdims=True)
    d2 = jnp.sum(jnp.where(sel2, base, 0.0), axis=0, keepdims=True)
    row8 = lax.broadcasted_iota(jnp.int32, (8, n_tok), 0)
    meta_ref[...] = jnp.where(row8 == 0, d1, jnp.where(row8 == 1, d2, jnp.where(row8 >= 4, 0.0, slab)))
    lane = lax.broadcasted_iota(jnp.int32, (N_EXPERTS, LANES), 1)
    tab_ref[...] = jnp.where(lane == 0, pc_b, jnp.where(lane == 1, start, 0.0))
    return d1, d2


def _sort_rows(d1, d2, h2b, xl_ref):
    n_tok = h2b.shape[0]
    for c in range(SORT_ROWS // SORT_CHUNK):
        r_iota = (lax.broadcasted_iota(jnp.int32, (SORT_CHUNK, n_tok), 0) + c * SORT_CHUNK).astype(F32)
        perm = jnp.where(r_iota == d1, 1.0, jnp.where(r_iota == d2, 1.0, 0.0)).astype(BF16)
        xl_ref[c * SORT_CHUNK:(c + 1) * SORT_CHUNK, :] = _dot(perm, h2b).astype(BF16)


def _sample_mixer_kernel(sinks_ref, x_ref, ck_ref, cv_ref, g1_ref, win_ref, cos_ref, sin_ref, lng_ref, lnb_ref,
                         ws0_ref, bs0_ref, wout_ref, g2_ref, wrt_ref, brt_ref, upper_ref, lpad_ref, xl_in_hbm,
                         xmid_ref, meta_ref, tab_ref, kout_ref, vout_ref, vnout_ref, xl_hbm,
                         mix_ref, xl_tile, xl_sem):
    del xl_in_hbm
    n_seq = x_ref.shape[0]
    seq_chunk = 16
    x = x_ref[...]
    h = _rmsnorm(x, g1_ref[...]).astype(BF16)
    z = _dot(h, win_ref[...])
    cos = cos_ref[...]
    sin = sin_ref[...]
    scale = np.float32(HEAD_DIM ** -0.5)
    lane = lax.broadcasted_iota(jnp.int32, (n_seq, LANES), 1)
    lo = lane < HEAD_DIM
    kf = _rope(z[:, C_K:C_K + KV_WIDTH], cos, sin)
    vf = z[:, C_V:C_V + KV_WIDTH]
    kout_ref[...] = kf
    vout_ref[...] = vf
    kb = kf.astype(BF16).astype(F32)
    vb = vf.astype(BF16).astype(F32)

    q_heads = []
    for hd in range(N_Q_HEADS):
        m = hd // 2
        qc = _rope(z[:, C_Q + m * LANES:C_Q + (m + 1) * LANES], cos, sin) * scale
        keep = lo if hd % 2 == 0 else ~lo
        qm = jnp.where(keep, qc, 0.0)
        if (hd % 2) != (hd // (N_Q_HEADS // N_KV_HEADS)):
            qm = pltpu.roll(qm, HEAD_DIM, 1)
        q_heads.append(qm.astype(BF16))

    s_new = [jnp.sum(q_heads[hd].astype(F32) * kb, axis=-1, keepdims=True) for hd in range(N_Q_HEADS)]

    rr = lax.broadcasted_iota(jnp.int32, (N_Q_HEADS * seq_chunk, seq_chunk * WINDOW), 0)
    cc = lax.broadcasted_iota(jnp.int32, (N_Q_HEADS * seq_chunk, seq_chunk * WINDOW), 1)
    same_seq = (rr % seq_chunk) == (cc // WINDOW)
    kv_lo = lax.broadcasted_iota(jnp.int32, (seq_chunk, LANES), 1) < HEAD_DIM

    for c in range(n_seq // seq_chunk):
        sr = slice(c * seq_chunk, (c + 1) * seq_chunk)
        kc = ck_ref[sr].reshape(seq_chunk * WINDOW, KV_WIDTH).astype(BF16)
        vc = cv_ref[sr].reshape(seq_chunk * WINDOW, KV_WIDTH).astype(BF16)
        qs = jnp.concatenate([q_heads[hd][sr] for hd in range(N_Q_HEADS)], axis=0)
        s = jnp.where(same_seq, _dot_nt(qs, kc), NEG_INF)
        sn = jnp.concatenate([s_new[hd][sr] for hd in range(N_Q_HEADS)], axis=0)
        sink = jnp.concatenate([jnp.full((seq_chunk, 1), sinks_ref[hd], F32) for hd in range(N_Q_HEADS)], axis=0)
        m = jnp.maximum(jnp.maximum(jnp.max(s, axis=-1, keepdims=True), sn), sink)
        e = jnp.exp(s - m)
        en = jnp.exp(sn - m)
        inv = 1.0 / (jnp.sum(e, axis=-1, keepdims=True) + en + jnp.exp(sink - m))
        o = _dot((e * inv).astype(BF16), vc)
        pn = (en * inv).astype(BF16).astype(F32)
        for mcol in range(N_Q_HEADS // 2):
            halves = []
            for hd in (2 * mcol, 2 * mcol + 1):
                oh = o[hd * seq_chunk:(hd + 1) * seq_chunk] + pn[hd * seq_chunk:(hd + 1) * seq_chunk] * vb[sr]
                if (hd % 2) != (hd // (N_Q_HEADS // N_KV_HEADS)):
                    oh = pltpu.roll(oh, HEAD_DIM, 1)
                halves.append(oh)
            att = jnp.where(kv_lo, halves[0], halves[1])
            mix_ref[sr, mcol * LANES:(mcol + 1) * LANES] = att.astype(BF16)

    u = _gelu(z[:, C_U:C_U + GMLP_WIDTH])
    vn = _layernorm(_gelu(z[:, C_VG:C_VG + GMLP_WIDTH]), lng_ref[...], lnb_ref[...])
    vnout_ref[...] = vn
    sp = ws0_ref[...].astype(BF16).astype(F32) * vn.astype(BF16).astype(F32) + bs0_ref[...]
    mix_ref[:, ATTN_WIDTH:] = (u * sp).astype(BF16)

    xmid = x + _dot(mix_ref[...], wout_ref[...])
    xmid_ref[...] = xmid
    h2b = _rmsnorm(xmid, g2_ref[...]).astype(BF16)
    d1, d2 = _sort_plan(_route(h2b, wrt_ref, brt_ref), upper_ref, lpad_ref, meta_ref, tab_ref)
    _sort_rows(d1, d2, h2b, xl_tile)
    copy = pltpu.make_async_copy(xl_tile, xl_hbm.at[xl_hbm.shape[0] - 1], xl_sem.at[0])
    copy.start()
    copy.wait()


def _grouped_ffn_kernel(texp_ref, gsrc_ref, sdst_ref, ntiles_ref, xl_hbm, wg_ref, wu_ref, wd_ref, yl_hbm,
                        xbuf, ybuf, wg_b, wu_b, wd_b, gsem, ssem):
    j = pl.program_id(0)
    n_tiles = ntiles_ref[0]

    def gather_copy(tile, s, b):
        return pltpu.make_async_copy(xl_hbm.at[gsrc_ref[tile * FFN_SLOTS + s]], xbuf.at[b, s], gsem.at[b])

    def scatter_copy(tile, s):
        b = (tile + SCATTER_BUFS) % SCATTER_BUFS
        return pltpu.make_async_copy(ybuf.at[b, s], yl_hbm.at[sdst_ref[(tile + 1) * FFN_SLOTS + s]], ssem.at[b])

    @pl.when(j < n_tiles)
    def _():
        gb = j % GATHER_BUFS
        ahead = GATHER_BUFS - 1
        nxt = j + ahead
        nxt_b = nxt % GATHER_BUFS

        @pl.when(j == 0)
        def _():
            ybuf[SCATTER_BUFS - 1] = jnp.zeros(ybuf.shape[1:], BF16)
            for k in range(ahead):
                for s in range(FFN_SLOTS):
                    gather_copy(k, s, k).start()

        for s in range(FFN_SLOTS):
            gather_copy(j, s, gb).wait()

        @pl.when(j >= SCATTER_BUFS - 1)
        def _():
            for s in range(FFN_SLOTS):
                scatter_copy(j - SCATTER_BUFS, s).wait()

        @pl.when(jnp.logical_or(j == 0, texp_ref[j] != texp_ref[jnp.maximum(j - 1, 0)]))
        def _():
            wg_b[...] = wg_ref[...].astype(BF16)
            wu_b[...] = wu_ref[...].astype(BF16)
            wd_b[...] = wd_ref[...].astype(BF16)

        x = xbuf[gb].reshape(FFN_ROWS, D_MODEL)
        gate = _dot(x, wg_b[...])
        for s in range(FFN_SLOTS):
            scatter_copy(j - 1, s).start()
        up = _dot(x, wu_b[...])
        for s in range(FFN_SLOTS):
            gather_copy(nxt, s, nxt_b).start()
        hid = (gate * (1.0 / (1.0 + jnp.exp(-gate))) * up).astype(BF16)
        ybuf[j % SCATTER_BUFS] = _dot(hid, wd_b[...]).astype(BF16).reshape(FFN_SLOTS, ROW_GRANULE, D_MODEL)

        @pl.when(j == n_tiles - 1)
        def _():
            for k in range(1, GATHER_BUFS):
                for s in range(FFN_SLOTS):
                    gather_copy(nxt, s, (j + k) % GATHER_BUFS).wait()
            for s in range(FFN_SLOTS):
                scatter_copy(j, s).start()
            for back in range(SCATTER_BUFS):
                @pl.when(j - back >= -1)
                def _():
                    for s in range(FFN_SLOTS):
                        scatter_copy(j - back, s).wait()


def _combine_kernel(xmid_ref, yl_ref, meta_ref, gf_ref, y_ref):
    n_tok = xmid_ref.shape[0]
    meta = meta_ref[...]
    meta_t = jnp.concatenate([meta, jnp.zeros((LANES - 8, n_tok), F32)], axis=0).T
    half = max(n_tok // 2, LANES)
    for r0 in range(0, n_tok, half):
        rows = slice(r0, r0 + half)
        d1, d2, w1, w2 = (meta_t[rows, i:i + 1] for i in range(4))
        acc = xmid_ref[rows, :]
        for c in range(SORT_ROWS // SORT_CHUNK):
            r_iota = (lax.broadcasted_iota(jnp.int32, (half, SORT_CHUNK), 1) + c * SORT_CHUNK).astype(F32)
            unsort = jnp.where(r_iota == d1, w1, jnp.where(r_iota == d2, w2, 0.0)).astype(BF16)
            acc = acc + _dot(unsort, yl_ref[c * SORT_CHUNK:(c + 1) * SORT_CHUNK, :])
        y_ref[rows, :] = _rmsnorm(acc, gf_ref[...])


def _ffn_schedule(tab):
    n_tok_tiles = tab.shape[0]
    strips = (tab[:, :, 0] * (1.0 / ROW_GRANULE)).astype(jnp.int32)
    starts = (tab[:, :, 1] * (1.0 / ROW_GRANULE)).astype(jnp.int32)
    cnt = strips.T
    row0 = starts.T
    cs = jnp.cumsum(cnt, axis=1) - cnt
    n_str = jnp.sum(cnt, axis=1)
    np_str = (n_str + FFN_SLOTS - 1) // FFN_SLOTS * FFN_SLOTS
    ends = jnp.cumsum(np_str)
    base = ends - np_str
    n_steps_max = _ffn_steps_max(n_tok_tiles) + GATHER_BUFS - 1
    step0 = jnp.arange(n_steps_max, dtype=jnp.int32) * FFN_SLOTS
    stream = jnp.minimum(jnp.sum(ends[None, :] <= step0[:, None], axis=1), N_EXPERTS - 1)
    pick = stream[:, None] == jnp.arange(N_EXPERTS, dtype=jnp.int32)[None, :]
    sel = lambda x: jnp.sum(jnp.where(pick[:, :, None], x[None], 0), axis=1)
    cs_j, cnt_j, row0_j = sel(cs), sel(cnt), sel(row0)
    base_j = jnp.sum(jnp.where(pick, base[None, :], 0), axis=1)
    q = step0[:, None] + jnp.arange(FFN_SLOTS, dtype=jnp.int32)[None, :] - base_j[:, None]
    reached = cs_j[:, None, :] <= q[:, :, None]
    last = lambda x: jnp.sum(jnp.where(reached, jnp.diff(x, axis=1, prepend=0)[:, None, :], 0), axis=2)
    tile_idx = jnp.sum(reached, axis=2).astype(jnp.int32) - 1
    g = q - last(cs_j)
    valid = g < last(cnt_j)
    granule = tile_idx * GRANULES_PER_TILE + last(row0_j) + g
    assert n_tok_tiles >= 2 * FFN_SLOTS
    slot = jnp.arange(FFN_SLOTS, dtype=jnp.int32)[None, :]
    parity = jnp.arange(n_steps_max, dtype=jnp.int32)[:, None] % 2
    pad_dst = lambda par: (slot + par * FFN_SLOTS) * GRANULES_PER_TILE + GRANULES_PER_TILE - 1
    gsrc = jnp.where(valid, granule, slot * GRANULES_PER_TILE + GRANULES_PER_TILE - 2).astype(jnp.int32)
    sdst = jnp.where(valid, granule, pad_dst(parity))
    sdst = jnp.concatenate([pad_dst(1), sdst], axis=0)
    texp = stream.astype(jnp.int32)
    n_steps = (ends[-1] // FFN_SLOTS).astype(jnp.int32).reshape(1)
    return texp, gsrc.reshape(-1), sdst.astype(jnp.int32).reshape(-1), n_steps


assert GRANULES_PER_TILE - (2 * TOK_TILE + N_EXPERTS * (ROW_GRANULE - 1)) // ROW_GRANULE >= 2


def _ffn_steps_max(n_tok_tiles):
    return -(-(n_tok_tiles * GRANULES_PER_TILE + N_EXPERTS * (FFN_SLOTS - 1)) // FFN_SLOTS)


def _vmem_limit(mib):
    n_bytes = mib * 1024 * 1024
    assert n_bytes < V7X_VMEM_BYTES
    return n_bytes


def _rope_tables(pos):
    inv_freq = ROPE_THETA ** (-jnp.arange(HALF, dtype=F32) * 2.0 / HEAD_DIM)
    ang = pos.astype(F32)[:, None] * inv_freq[None, :]
    cos, sin = jnp.cos(ang), jnp.sin(ang)
    reps = LANES // HEAD_DIM
    return jnp.tile(jnp.concatenate([cos, cos], axis=1), (1, reps)), jnp.tile(jnp.concatenate([-sin, sin], axis=1), (1, reps))


def _moe_combine(xmid, yl, meta, gf, n_tok, first_tile, n_tiles):
    return pl.pallas_call(
        _combine_kernel,
        grid=(n_tiles,),
        in_specs=[
            pl.BlockSpec((n_tok, D_MODEL), lambda i: (i, 0)),
            pl.BlockSpec((SORT_ROWS, D_MODEL), lambda i: (first_tile + i, 0)),
            pl.BlockSpec((None, 8, n_tok), lambda i: (i, 0, 0)),
            pl.BlockSpec((1, D_MODEL), lambda i: (0, 0)),
        ],
        out_specs=pl.BlockSpec((n_tok, D_MODEL), lambda i: (i, 0)),
        out_shape=jax.ShapeDtypeStruct((n_tiles * n_tok, D_MODEL), F32),
        compiler_params=pltpu.CompilerParams(
            dimension_semantics=("arbitrary",),
            vmem_limit_bytes=_vmem_limit(40)),
        name="moe_combine",
    )(xmid, yl, meta, gf)


def kernel(x_prompt, x_sample, cache_swa_k, cache_swa_v, norm_mix_g, w_in, attn_sinks, gmlp_ln_g, gmlp_ln_b,
           gmlp_w_s, gmlp_b_s, w_out, norm_ffn_g, router_group_w, router_group_b, router_expert_w,
           router_expert_b, expert_w_gate, expert_w_up, expert_w_down, final_norm_g):
    assert norm_mix_g.shape[0] == 1, "single-layer trunk"
    batch, seq, _ = x_prompt.shape
    dec_batch = x_sample.shape[0]
    assert x_sample.shape[1] == 1 and seq % TOK_TILE == 0

    win_ext = w_in[0].astype(BF16)
    wout = w_out[0].astype(BF16)
    g1 = norm_mix_g[0][None, :]
    g2 = norm_ffn_g[0][None, :]
    gf = final_norm_g[None, :]
    lng = gmlp_ln_g[0][None, :]
    lnb = gmlp_ln_b[0][None, :]
    sinks = attn_sinks[0]
    ws = gmlp_w_s[0]
    group_dim = GMLP_WIDTH // GMLP_GROUPS
    bsf = jnp.repeat(gmlp_b_s[0].T, group_dim, axis=1)
    ws0 = jnp.repeat(ws[:, 0, 0], group_dim)[None, :]
    bs0 = jnp.repeat(gmlp_b_s[0][:, 0], group_dim)[None, :]
    wrt = jnp.zeros((ROUTER_ROWS, D_MODEL), F32)
    wrt = wrt.at[0:N_GROUPS].set(router_group_w[0].T)
    wrt = wrt.at[8:8 + N_EXPERTS].set(router_expert_w[0].reshape(D_MODEL, N_EXPERTS).T).astype(BF16)
    brt = jnp.full((ROUTER_ROWS, 1), NEG_INF, F32)
    brt = brt.at[0:N_GROUPS, 0].set(router_group_b[0])
    brt = brt.at[8:8 + N_EXPERTS, 0].set(router_expert_b[0].reshape(N_EXPERTS))
    wg, wu, wd = expert_w_gate[0], expert_w_up[0], expert_w_down[0]
    cos_p, sin_p = _rope_tables(jnp.arange(seq, dtype=jnp.int32))
    cos_s, sin_s = _rope_tables(PAST_LEN + jnp.arange(1, dtype=jnp.int32))

    full = lambda shape: pl.BlockSpec(shape, lambda *_: (0,) * len(shape))
    smem = pl.BlockSpec(memory_space=pltpu.SMEM)
    n_tiles = seq // TOK_TILE

    upper = jnp.triu(jnp.ones((TOK_TILE, TOK_TILE), BF16), k=1)
    lpad = (jnp.arange(LANES)[None, :] < jnp.arange(N_EXPERTS)[:, None]).astype(BF16)
    n_tok_tiles = batch * n_tiles

    x2d = x_prompt.reshape(batch * seq, D_MODEL)
    tile_a = lambda s: jnp.minimum(s, n_tok_tiles - 1)
    cur = lambda s: jnp.maximum(s - 1, 0)
    tile_b = cur
    xmid_p, xl_p, meta_p, tab_p, k_p, v_p = pl.pallas_call(
        functools.partial(_prompt_mixer_kernel, tiles_per_seq=n_tiles),
        grid=(n_tok_tiles + 1,),
        in_specs=[
            smem,
            pl.BlockSpec((TOK_TILE, D_MODEL), lambda s: (tile_a(s), 0)),
            pl.BlockSpec((TOK_TILE, D_MODEL), lambda s: (cur(s), 0)),
            full((1, D_MODEL)),
            full((D_MODEL, IN_WIDTH)),
            pl.BlockSpec((TOK_TILE, LANES), lambda s: (tile_b(s) % n_tiles, 0)),
            pl.BlockSpec((TOK_TILE, LANES), lambda s: (tile_b(s) % n_tiles, 0)),
            full((1, GMLP_WIDTH)),
            full((1, GMLP_WIDTH)),
            full((GMLP_GROUPS, CHUNK, CHUNK)),
            full((CHUNK, GMLP_WIDTH)),
            full((D_MODEL, D_MODEL)),
            full((1, D_MODEL)),
            full((ROUTER_ROWS, D_MODEL)),
            full((ROUTER_ROWS, 1)),
            full((TOK_TILE, TOK_TILE)),
            full((N_EXPERTS, LANES)),
        ],
        out_specs=[
            pl.BlockSpec((TOK_TILE, D_MODEL), lambda s: (cur(s), 0)),
            pl.BlockSpec((None, SORT_ROWS, D_MODEL), lambda s: (jnp.where(s == 0, n_tok_tiles, s - 1), 0, 0)),
            pl.BlockSpec((None, 8, TOK_TILE), lambda s: (cur(s), 0, 0)),
            pl.BlockSpec((None, N_EXPERTS, LANES), lambda s: (cur(s), 0, 0)),
            pl.BlockSpec((None, WINDOW, KV_WIDTH), lambda s: (tile_b(s) // n_tiles, 0, 0)),
            pl.BlockSpec((None, WINDOW, KV_WIDTH), lambda s: (tile_b(s) // n_tiles, 0, 0)),
        ],
        out_shape=[
            jax.ShapeDtypeStruct((batch * seq, D_MODEL), F32),
            jax.ShapeDtypeStruct((n_tok_tiles + 1, SORT_ROWS, D_MODEL), BF16),
            jax.ShapeDtypeStruct((n_tok_tiles, 8, TOK_TILE), F32),
            jax.ShapeDtypeStruct((n_tok_tiles, N_EXPERTS, LANES), F32),
            jax.ShapeDtypeStruct((batch, WINDOW, KV_WIDTH), F32),
            jax.ShapeDtypeStruct((batch, WINDOW, KV_WIDTH), F32),
        ],
        scratch_shapes=[
            pltpu.VMEM((TOK_TILE, IN_WIDTH), F32),
            pltpu.VMEM((TOK_TILE, IN_WIDTH), F32),
            pltpu.VMEM((TOK_TILE, D_MODEL), BF16),
        ] + [pltpu.VMEM((WINDOW + TOK_TILE, KV_WIDTH), BF16)] * 4 + [
            pltpu.VMEM((len(_NAT_HEADS), TOK_TILE, LANES), BF16),
            pltpu.VMEM((len(_SWP_HEADS), TOK_TILE, LANES), BF16),
            pltpu.VMEM((2 * BLOCKS_PER_TILE, len(_NAT_HEADS) * WINDOW, 2 * WINDOW), F32),
            pltpu.VMEM((2 * BLOCKS_PER_TILE, len(_NAT_HEADS) * WINDOW, 2 * WINDOW), BF16),
            pltpu.VMEM((TOK_TILE, GMLP_WIDTH), BF16),
            pltpu.VMEM((TOK_TILE, D_MODEL), BF16),
            pltpu.VMEM((TOK_TILE, D_MODEL), BF16),
        ],
        compiler_params=pltpu.CompilerParams(
            dimension_semantics=("arbitrary",),
            vmem_limit_bytes=_vmem_limit(56)),
        name="prompt_mixer",
    )(sinks, x2d, x2d, g1, win_ext, cos_p, sin_p, lng, lnb, ws, bsf, wout, g2, wrt, brt, upper, lpad)

    xs = x_sample.reshape(dec_batch, D_MODEL)
    ck = cache_swa_k[0].reshape(dec_batch, WINDOW, KV_WIDTH)
    cv = cache_swa_v[0].reshape(dec_batch, WINDOW, KV_WIDTH)
    vmem = pl.BlockSpec(memory_space=pltpu.VMEM)
    hbm = pl.BlockSpec(memory_space=pl.ANY)
    xmid_s, meta_s, tab_s, k_s, v_s, vn_s, xl_all = pl.pallas_call(
        _sample_mixer_kernel,
        in_specs=[smem] + [vmem] * 17 + [hbm],
        out_specs=[vmem] * 6 + [hbm],
        out_shape=[
            jax.ShapeDtypeStruct((dec_batch, D_MODEL), F32),
            jax.ShapeDtypeStruct((8, dec_batch), F32),
            jax.ShapeDtypeStruct((N_EXPERTS, LANES), F32),
            jax.ShapeDtypeStruct((dec_batch, KV_WIDTH), F32),
            jax.ShapeDtypeStruct((dec_batch, KV_WIDTH), F32),
            jax.ShapeDtypeStruct((dec_batch, GMLP_WIDTH), F32),
            jax.ShapeDtypeStruct(xl_p.shape, BF16),
        ],
        scratch_shapes=[pltpu.VMEM((dec_batch, D_MODEL), BF16), pltpu.VMEM((SORT_ROWS, D_MODEL), BF16),
                        pltpu.SemaphoreType.DMA((1,))],
        compiler_params=pltpu.CompilerParams(vmem_limit_bytes=_vmem_limit(56)),
        input_output_aliases={18: 6},
        name="sample_mixer",
    )(sinks, xs, ck, cv, g1, win_ext, cos_s, sin_s, lng, lnb, ws0, bs0, wout, g2, wrt, brt,
      upper[:dec_batch, :dec_batch], lpad, xl_p)

    n_all_tiles = n_tok_tiles + 1
    tab_all = jnp.concatenate([tab_p, tab_s[None]], axis=0)
    texp, gsrc, sdst, n_steps = _ffn_schedule(tab_all)
    yl_all = pl.pallas_call(
        _grouped_ffn_kernel,
        grid_spec=pltpu.PrefetchScalarGridSpec(
            num_scalar_prefetch=4,
            grid=(_ffn_steps_max(n_all_tiles),),
            in_specs=[
                pl.BlockSpec(memory_space=pl.ANY),
                pl.BlockSpec((None, D_MODEL, D_EXPERT), lambda j, te, gs, sd, ns: (te[j], 0, 0)),
                pl.BlockSpec((None, D_MODEL, D_EXPERT), lambda j, te, gs, sd, ns: (te[j], 0, 0)),
                pl.BlockSpec((None, D_EXPERT, D_MODEL), lambda j, te, gs, sd, ns: (te[j], 0, 0)),
            ],
            out_specs=pl.BlockSpec(memory_space=pl.ANY),
            scratch_shapes=[
                pltpu.VMEM((GATHER_BUFS, FFN_SLOTS, ROW_GRANULE, D_MODEL), BF16),
                pltpu.VMEM((SCATTER_BUFS, FFN_SLOTS, ROW_GRANULE, D_MODEL), BF16),
                pltpu.VMEM((D_MODEL, D_EXPERT), BF16),
                pltpu.VMEM((D_MODEL, D_EXPERT), BF16),
                pltpu.VMEM((D_EXPERT, D_MODEL), BF16),
                pltpu.SemaphoreType.DMA((GATHER_BUFS,)),
                pltpu.SemaphoreType.DMA((SCATTER_BUFS,)),
            ],
        ),
        out_shape=jax.ShapeDtypeStruct((n_all_tiles * GRANULES_PER_TILE, ROW_GRANULE, D_MODEL), BF16),
        compiler_params=pltpu.CompilerParams(
            dimension_semantics=("arbitrary",),
            vmem_limit_bytes=_vmem_limit(32)),
        input_output_aliases={4: 0},
        name="grouped_ffn",
    )(texp, gsrc, sdst, n_steps, xl_all.reshape(n_all_tiles * GRANULES_PER_TILE, ROW_GRANULE, D_MODEL), wg, wu, wd)
    yl_all = yl_all.reshape(n_all_tiles * SORT_ROWS, D_MODEL)

    y_p = _moe_combine(xmid_p, yl_all, meta_p, gf, TOK_TILE, 0, n_tok_tiles)
    y_s = _moe_combine(xmid_s, yl_all, meta_s[None], gf, dec_batch, n_tok_tiles, 1)

    return (y_p.reshape(batch, seq, D_MODEL),
            y_s.reshape(dec_batch, 1, D_MODEL),
            k_p.reshape(1, batch, WINDOW, N_KV_HEADS, HEAD_DIM),
            v_p.reshape(1, batch, WINDOW, N_KV_HEADS, HEAD_DIM),
            k_s.reshape(1, dec_batch, 1, N_KV_HEADS, HEAD_DIM),
            v_s.reshape(1, dec_batch, 1, N_KV_HEADS, HEAD_DIM),
            vn_s.reshape(1, dec_batch, 1, GMLP_WIDTH))
```

```python
import functools

import jax
import jax.numpy as jnp
import numpy as np
from jax import lax
from jax.experimental import pallas as pl
from jax.experimental.pallas import tpu as pltpu

F32 = jnp.float32
BF16 = jnp.bfloat16

D_MODEL = 1024
HEAD_DIM = 64
HALF = HEAD_DIM // 2
N_Q_HEADS = 8
N_KV_HEADS = 2
ATTN_WIDTH = N_Q_HEADS * HEAD_DIM
KV_WIDTH = N_KV_HEADS * HEAD_DIM
WINDOW = 128
ROPE_THETA = 10000.0
GMLP_WIDTH = D_MODEL - ATTN_WIDTH
GMLP_GROUPS = 8
CHUNK = 128
N_GROUPS = 4
EXPERTS_PER_GROUP = 8
N_EXPERTS = N_GROUPS * EXPERTS_PER_GROUP
D_EXPERT = 256
EPS = 1e-6
NEG_INF = -1e30
PAST_LEN = 16384

LANES = 128
V7X_VMEM_BYTES = 64 * 1024 * 1024

C_Q = 0
C_K = C_Q + ATTN_WIDTH
C_V = C_K + KV_WIDTH
C_U = C_V + KV_WIDTH
C_VG = C_U + GMLP_WIDTH
IN_WIDTH = C_VG + GMLP_WIDTH

ROUTER_ROWS = 48
TOK_TILE = 512
BLOCKS_PER_TILE = TOK_TILE // WINDOW
ROW_GRANULE = 16
SORT_CHUNK = 512
SORT_ROWS = -(-(2 * TOK_TILE + N_EXPERTS * (ROW_GRANULE - 1)) // SORT_CHUNK) * SORT_CHUNK
GRANULES_PER_TILE = SORT_ROWS // ROW_GRANULE
FFN_ROWS = 512
FFN_SLOTS = FFN_ROWS // ROW_GRANULE
GATHER_BUFS = 3
SCATTER_BUFS = 3
COMBINE_BUFS = 3


def _dot(a, b):
    return jnp.dot(a, b, preferred_element_type=F32)


def _dot_nt(a, b):
    return lax.dot_general(a, b, (((1,), (1,)), ((), ())), preferred_element_type=F32)


def _gelu(x):
    return 0.5 * x * (1.0 + lax.erf(x * np.float32(np.sqrt(0.5))))


def _rmsnorm(x, g):
    return x * lax.rsqrt(jnp.mean(x * x, axis=-1, keepdims=True) + EPS) * g


def _layernorm(x, g, b):
    mu = jnp.mean(x, axis=-1, keepdims=True)
    xc = x - mu
    return xc * lax.rsqrt(jnp.mean(xc * xc, axis=-1, keepdims=True) + EPS) * g + b


def _first_argmax_rows(x, row_iota, n_rows):
    m = jnp.max(x, axis=0, keepdims=True)
    idx = jnp.min(jnp.where(x == m, row_iota, n_rows), axis=0, keepdims=True)
    return m, idx


def _route(h2b, wrt_ref, brt_ref):
    n_tok = h2b.shape[0]
    lt = _dot_nt(wrt_ref[...], h2b) + brt_ref[...]
    row8 = lax.broadcasted_iota(jnp.int32, (8, n_tok), 0).astype(F32)
    glog = lt[0:8]
    gmax, gidx = _first_argmax_rows(glog, row8, 8)
    g_w = 1.0 / jnp.sum(jnp.exp(glog - gmax), axis=0, keepdims=True)
    esel = lt[8:16]
    for g in range(1, N_GROUPS):
        esel = jnp.where(gidx == g, lt[8 + 8 * g:16 + 8 * g], esel)
    m1, i1 = _first_argmax_rows(esel, row8, 8)
    esel2 = jnp.where(row8 == i1, -jnp.inf, esel)
    m2, i2 = _first_argmax_rows(esel2, row8, 8)
    r = jnp.exp(m2 - m1)
    w1 = 1.0 / (1.0 + r)
    w2 = r / (1.0 + r)
    e1 = gidx * EXPERTS_PER_GROUP + i1
    e2 = gidx * EXPERTS_PER_GROUP + i2
    slab = jnp.where(row8 == 0, e1, 0.0)
    for r_idx, val in ((1, e2), (2, w1 * g_w), (3, w2 * g_w)):
        slab = jnp.where(row8 == r_idx, val, slab)
    return slab


def _softmax_with_sink(s, sink):
    m = jnp.maximum(jnp.max(s, axis=-1, keepdims=True), sink)
    e = jnp.exp(s - m)
    den = jnp.sum(e, axis=-1, keepdims=True) + jnp.exp(sink - m)
    return e * (1.0 / den)


def _rope(x, cos, sin_signed):
    first_half = (lax.broadcasted_iota(jnp.int32, x.shape, 1) & HALF) == 0
    partner = jnp.where(first_half, pltpu.roll(x, LANES - HALF, 1), pltpu.roll(x, HALF, 1))
    return x * cos + partner * sin_signed


_NAT_HEADS = (0, 2, 5, 7)
_SWP_HEADS = (1, 3, 4, 6)


def _prompt_mixer_kernel(sinks_ref, x_next_ref, x_ref, g1_ref, win_ref, cos_ref, sin_ref, lng_ref, lnb_ref, ws_ref,
                         bsf_ref, wout_ref, g2_ref, wrt_ref, brt_ref, upper_ref, lpad_ref,
                         xmid_ref, xl_ref, meta_ref, tab_ref, kout_ref, vout_ref,
                         z_a, z_b, mix_ref, k_n, k_s, v_n, v_s, q_nat, q_swp, s_ref, p_ref, vn_ref, h_ref, h2_ref,
                         *, tiles_per_seq):
    step = pl.program_id(0)
    t = jnp.maximum(step - 1, 0) % tiles_per_seq
    kv_bufs = (k_n, k_s, v_n, v_s)

    @pl.when(step == 0)
    def _():
        z_b[...] = jnp.zeros_like(z_b)

    @pl.when(t == 0)
    def _():
        for ref in kv_bufs:
            ref[0:WINDOW, :] = jnp.zeros((WINDOW, KV_WIDTH), BF16)

    args = (t, sinks_ref, x_next_ref, x_ref, g1_ref, win_ref, cos_ref, sin_ref, lng_ref, lnb_ref, ws_ref, bsf_ref,
            wout_ref, g2_ref, wrt_ref, brt_ref, upper_ref, lpad_ref, xmid_ref, xl_ref, meta_ref, tab_ref, kout_ref,
            vout_ref, kv_bufs, q_nat, q_swp, s_ref, p_ref, vn_ref, h_ref, h2_ref)

    @pl.when(step % 2 == 0)
    def _():
        _prompt_mixer_body(z_a, z_b, mix_ref, *args)

    @pl.when(step % 2 == 1)
    def _():
        _prompt_mixer_body(z_b, z_a, mix_ref, *args)


def _prompt_mixer_body(z_next, z_ref, mix_ref, t, sinks_ref, x_next_ref, x_ref, g1_ref, win_ref, cos_ref,
                       sin_ref, lng_ref, lnb_ref, ws_ref, bsf_ref, wout_ref, g2_ref, wrt_ref, brt_ref, upper_ref,
                       lpad_ref, xmid_ref, xl_ref, meta_ref, tab_ref, kout_ref, vout_ref, kv_bufs,
                       q_nat, q_swp, s_ref, p_ref, vn_ref, h_ref, h2_ref):
    k_n, k_s, v_n, v_s = kv_bufs
    cos = cos_ref[...]
    sin = sin_ref[...]
    lane = lax.broadcasted_iota(jnp.int32, (WINDOW, LANES), 1)
    lo = lane < HEAD_DIM
    row = lax.broadcasted_iota(jnp.int32, (WINDOW, WINDOW), 0)
    col = lax.broadcasted_iota(jnp.int32, (WINDOW, WINDOW), 1)
    mask_cur = col <= row
    mask_prev_band = col >= row
    mask_prev_first = jnp.logical_and(mask_prev_band, (jnp.zeros_like(row) + t) > 0)
    mask_band = jnp.concatenate([mask_prev_band, mask_cur], axis=1)
    mask_first = jnp.concatenate([mask_prev_first, mask_cur], axis=1)

    cq = cos * np.float32(HEAD_DIM ** -0.5)
    sq = sin * np.float32(HEAD_DIM ** -0.5)
    kf = _rope(z_ref[:, C_K:C_K + KV_WIDTH], cos, sin)
    vf = z_ref[:, C_V:C_V + KV_WIDTH]
    k_n[WINDOW:, :] = kf.astype(BF16)
    k_s[WINDOW:, :] = pltpu.roll(kf, HEAD_DIM, 1).astype(BF16)
    v_n[WINDOW:, :] = vf.astype(BF16)
    v_s[WINDOW:, :] = pltpu.roll(vf, HEAD_DIM, 1).astype(BF16)

    kout_ref[...] = kf[TOK_TILE - WINDOW:]
    vout_ref[...] = vf[TOK_TILE - WINDOW:]

    lo_t = lax.broadcasted_iota(jnp.int32, (TOK_TILE, LANES), 1) < HEAD_DIM
    for m in range(N_Q_HEADS // 2):
        qc = _rope(z_ref[:, C_Q + m * LANES:C_Q + (m + 1) * LANES], cq, sq)
        for hd, qh in ((2 * m, jnp.where(lo_t, qc, 0.0)), (2 * m + 1, jnp.where(lo_t, 0.0, qc))):
            if hd in _NAT_HEADS:
                q_nat[_NAT_HEADS.index(hd)] = qh.astype(BF16)
            else:
                q_swp[_SWP_HEADS.index(hd)] = qh.astype(BF16)

    stacks = ((q_nat, _NAT_HEADS, k_n, v_n), (q_swp, _SWP_HEADS, k_s, v_s))
    proj_cols = 2 * LANES

    def project_next(c):
        cols = slice(c * proj_cols, (c + 1) * proj_cols)
        z_next[:, cols] = _dot(h_ref[...], win_ref[:, cols])

    def scores(j):
        rows = slice(j * WINDOW, (j + 1) * WINDOW)
        keys = slice(j * WINDOW, (j + 2) * WINDOW)
        for si, (q_ref, _, k_buf, _) in enumerate(stacks):
            q_stack = jnp.concatenate([q_ref[i, rows, :] for i in range(len(_NAT_HEADS))], axis=0)
            s_ref[2 * j + si] = _dot_nt(q_stack, k_buf[keys, :])

    def softmax(j):
        mask = mask_first if j == 0 else mask_band
        for si, (_, heads, _, _) in enumerate(stacks):
            for i, hd in enumerate(heads):
                pr = slice(i * WINDOW, (i + 1) * WINDOW)
                p_ref[2 * j + si, pr, :] = _softmax_with_sink(
                    jnp.where(mask, s_ref[2 * j + si, pr, :], NEG_INF), sinks_ref[hd]).astype(BF16)

    def attend(j):
        rows = slice(j * WINDOW, (j + 1) * WINDOW)
        keys = slice(j * WINDOW, (j + 2) * WINDOW)
        o_nat = _dot(p_ref[2 * j], v_n[keys, :])
        o_swp = _dot(p_ref[2 * j + 1], v_s[keys, :])
        for m in range(N_Q_HEADS // 2):
            pr = slice(m * WINDOW, (m + 1) * WINDOW)
            even_nat = (2 * m) in _NAT_HEADS
            att = jnp.where(lo, o_nat[pr], o_swp[pr]) if even_nat else jnp.where(lo, o_swp[pr], o_nat[pr])
            mix_ref[rows, m * LANES:(m + 1) * LANES] = att.astype(BF16)

    def gmlp(m):
        cs = slice(m * LANES, (m + 1) * LANES)
        w0 = jnp.where(mask_cur, ws_ref[2 * m], 0.0).astype(BF16)
        w1 = jnp.where(mask_cur, ws_ref[2 * m + 1], 0.0).astype(BF16)
        wcat = jnp.concatenate([w0, w1], axis=1)
        for j in range(BLOCKS_PER_TILE):
            rows = slice(j * WINDOW, (j + 1) * WINDOW)
            vcol = vn_ref[rows, cs]
            rhs = jnp.concatenate([jnp.where(lo, vcol, jnp.zeros_like(vcol)),
                                   jnp.where(lo, jnp.zeros_like(vcol), vcol)], axis=0)
            sp = _dot(wcat, rhs) + bsf_ref[:, cs]
            u = _gelu(z_ref[rows, C_U + m * LANES:C_U + (m + 1) * LANES])
            mix_ref[rows, ATTN_WIDTH + m * LANES:ATTN_WIDTH + (m + 1) * LANES] = (u * sp).astype(BF16)

    def project_out(c):
        cols = slice(c * proj_cols, (c + 1) * proj_cols)
        xmid_ref[:, cols] = x_ref[:, cols] + _dot(mix_ref[...], wout_ref[:, cols])

    h_ref[...] = _rmsnorm(x_next_ref[...], g1_ref[...]).astype(BF16)
    project_next(0)
    project_next(1)
    for j in range(BLOCKS_PER_TILE):
        scores(j)
    vn_ref[...] = _layernorm(_gelu(z_ref[:, C_VG:C_VG + GMLP_WIDTH]), lng_ref[...], lnb_ref[...]).astype(BF16)
    project_next(2)
    softmax(0)
    project_next(3)
    softmax(1)
    project_next(4)
    softmax(2)
    project_next(5)
    softmax(3)
    project_next(6)
    for j in range(BLOCKS_PER_TILE):
        attend(j)
    for m in range(GMLP_GROUPS // 2):
        gmlp(m)
    for c in range(D_MODEL // proj_cols):
        project_out(c)
    h2_ref[...] = _rmsnorm(xmid_ref[...], g2_ref[...]).astype(BF16)
    d1, d2 = _sort_plan(_route(h2_ref[...], wrt_ref, brt_ref), upper_ref, lpad_ref, meta_ref, tab_ref)
    _sort_rows(d1, d2, h2_ref[...], xl_ref)

    for ref in kv_bufs:
        ref[0:WINDOW, :] = ref[TOK_TILE:TOK_TILE + WINDOW, :]


def _sort_plan(slab, upper_ref, lpad_ref, meta_ref, tab_ref):
    n_tok = slab.shape[1]
    e1, e2 = slab[0:1], slab[1:2]
    row32 = lax.broadcasted_iota(jnp.int32, (N_EXPERTS, n_tok), 0).astype(F32)
    sel1 = row32 == e1
    sel2 = row32 == e2
    onehot = jnp.where(sel1, 1.0, jnp.where(sel2, 1.0, 0.0))
    earlier = _dot(onehot.astype(BF16), upper_ref[...])
    cnt = jnp.sum(onehot, axis=1, keepdims=True)
    pc = jnp.floor((cnt + (ROW_GRANULE - 1)) * (1.0 / ROW_GRANULE)) * ROW_GRANULE
    pc_b = jnp.broadcast_to(pc, (N_EXPERTS, LANES))
    pc_pad = jnp.concatenate([pc_b, jnp.zeros((LANES - N_EXPERTS, LANES), F32)], axis=0).astype(BF16)
    start = _dot(lpad_ref[...], pc_pad)
    base = start[:, 0:1] + earlier
    d1 = jnp.sum(jnp.where(sel1, base, 0.0), axis=0, keepdims=True)
    d2 = jnp.sum(jnp.where(sel2, base, 0.0), axis=0, keepdims=True)
    row8 = lax.broadcasted_iota(jnp.int32, (8, n_tok), 0)
    meta_ref[...] = jnp.where(row8 == 0, d1, jnp.where(row8 == 1, d2, jnp.where(row8 >= 4, 0.0, slab)))
    lane = lax.broadcasted_iota(jnp.int32, (N_EXPERTS, LANES), 1)
    tab_ref[...] = jnp.where(lane == 0, pc_b, jnp.where(lane == 1, start, 0.0))
    return d1, d2


def _sort_rows(d1, d2, h2b, xl_ref):
    n_tok = h2b.shape[0]
    for c in range(SORT_ROWS // SORT_CHUNK):
        r_iota = (lax.broadcasted_iota(jnp.int32, (SORT_CHUNK, n_tok), 0) + c * SORT_CHUNK).astype(F32)
        perm = jnp.where(r_iota == d1, 1.0, jnp.where(r_iota == d2, 1.0, 0.0)).astype(BF16)
        xl_ref[c * SORT_CHUNK:(c + 1) * SORT_CHUNK, :] = _dot(perm, h2b).astype(BF16)


def _sample_mixer_kernel(sinks_ref, x_ref, ck_ref, cv_ref, g1_ref, win_ref, cos_ref, sin_ref, lng_ref, lnb_ref,
                         ws0_ref, bs0_ref, wout_ref, g2_ref, wrt_ref, brt_ref, upper_ref, lpad_ref, xl_in_hbm,
                         xmid_ref, meta_ref, tab_ref, kout_ref, vout_ref, vnout_ref, xl_hbm,
                         mix_ref, xl_tile, xl_sem):
    del xl_in_hbm
    n_seq = x_ref.shape[0]
    seq_chunk = 16
    x = x_ref[...]
    h = _rmsnorm(x, g1_ref[...]).astype(BF16)
    z = _dot(h, win_ref[...])
    cos = cos_ref[...]
    sin = sin_ref[...]
    scale = np.float32(HEAD_DIM ** -0.5)
    lane = lax.broadcasted_iota(jnp.int32, (n_seq, LANES), 1)
    lo = lane < HEAD_DIM
    kf = _rope(z[:, C_K:C_K + KV_WIDTH], cos, sin)
    vf = z[:, C_V:C_V + KV_WIDTH]
    kout_ref[...] = kf
    vout_ref[...] = vf
    kb = kf.astype(BF16).astype(F32)
    vb = vf.astype(BF16).astype(F32)

    q_heads = []
    for hd in range(N_Q_HEADS):
        m = hd // 2
        qc = _rope(z[:, C_Q + m * LANES:C_Q + (m + 1) * LANES], cos, sin) * scale
        keep = lo if hd % 2 == 0 else ~lo
        qm = jnp.where(keep, qc, 0.0)
        if (hd % 2) != (hd // (N_Q_HEADS // N_KV_HEADS)):
            qm = pltpu.roll(qm, HEAD_DIM, 1)
        q_heads.append(qm.astype(BF16))

    s_new = [jnp.sum(q_heads[hd].astype(F32) * kb, axis=-1, keepdims=True) for hd in range(N_Q_HEADS)]

    rr = lax.broadcasted_iota(jnp.int32, (N_Q_HEADS * seq_chunk, seq_chunk * WINDOW), 0)
    cc = lax.broadcasted_iota(jnp.int32, (N_Q_HEADS * seq_chunk, seq_chunk * WINDOW), 1)
    same_seq = (rr % seq_chunk) == (cc // WINDOW)
    kv_lo = lax.broadcasted_iota(jnp.int32, (seq_chunk, LANES), 1) < HEAD_DIM

    for c in range(n_seq // seq_chunk):
        sr = slice(c * seq_chunk, (c + 1) * seq_chunk)
        kc = ck_ref[sr].reshape(seq_chunk * WINDOW, KV_WIDTH).astype(BF16)
        vc = cv_ref[sr].reshape(seq_chunk * WINDOW, KV_WIDTH).astype(BF16)
        qs = jnp.concatenate([q_heads[hd][sr] for hd in range(N_Q_HEADS)], axis=0)
        s = jnp.where(same_seq, _dot_nt(qs, kc), NEG_INF)
        sn = jnp.concatenate([s_new[hd][sr] for hd in range(N_Q_HEADS)], axis=0)
        sink = jnp.concatenate([jnp.full((seq_chunk, 1), sinks_ref[hd], F32) for hd in range(N_Q_HEADS)], axis=0)
        m = jnp.maximum(jnp.maximum(jnp.max(s, axis=-1, keepdims=True), sn), sink)
        e = jnp.exp(s - m)
        en = jnp.exp(sn - m)
        inv = 1.0 / (jnp.sum(e, axis=-1, keepdims=True) + en + jnp.exp(sink - m))
        o = _dot((e * inv).astype(BF16), vc)
        pn = (en * inv).astype(BF16).astype(F32)
        for mcol in range(N_Q_HEADS // 2):
            halves = []
            for hd in (2 * mcol, 2 * mcol + 1):
                oh = o[hd * seq_chunk:(hd + 1) * seq_chunk] + pn[hd * seq_chunk:(hd + 1) * seq_chunk] * vb[sr]
                if (hd % 2) != (hd // (N_Q_HEADS // N_KV_HEADS)):
                    oh = pltpu.roll(oh, HEAD_DIM, 1)
                halves.append(oh)
            att = jnp.where(kv_lo, halves[0], halves[1])
            mix_ref[sr, mcol * LANES:(mcol + 1) * LANES] = att.astype(BF16)

    u = _gelu(z[:, C_U:C_U + GMLP_WIDTH])
    vn = _layernorm(_gelu(z[:, C_VG:C_VG + GMLP_WIDTH]), lng_ref[...], lnb_ref[...])
    vnout_ref[...] = vn
    sp = ws0_ref[...].astype(BF16).astype(F32) * vn.astype(BF16).astype(F32) + bs0_ref[...]
    mix_ref[:, ATTN_WIDTH:] = (u * sp).astype(BF16)

    xmid = x + _dot(mix_ref[...], wout_ref[...])
    xmid_ref[...] = xmid
    h2b = _rmsnorm(xmid, g2_ref[...]).astype(BF16)
    d1, d2 = _sort_plan(_route(h2b, wrt_ref, brt_ref), upper_ref, lpad_ref, meta_ref, tab_ref)
    _sort_rows(d1, d2, h2b, xl_tile)
    copy = pltpu.make_async_copy(xl_tile, xl_hbm.at[xl_hbm.shape[0] - 1], xl_sem.at[0])
    copy.start()
    copy.wait()


def _grouped_ffn_kernel(texp_ref, gsrc_ref, sdst_ref, ntiles_ref, xl_hbm, wg_ref, wu_ref, wd_ref, yl_hbm,
                        xbuf, ybuf, wg_b, wu_b, wd_b, gsem, ssem):
    j = pl.program_id(0)
    n_tiles = ntiles_ref[0]

    def gather_copy(tile, s, b):
        return pltpu.make_async_copy(xl_hbm.at[gsrc_ref[tile * FFN_SLOTS + s]], xbuf.at[b, s], gsem.at[b])

    def scatter_copy(tile, s):
        b = (tile + SCATTER_BUFS) % SCATTER_BUFS
        return pltpu.make_async_copy(ybuf.at[b, s], yl_hbm.at[sdst_ref[(tile + 1) * FFN_SLOTS + s]], ssem.at[b])

    @pl.when(j < n_tiles)
    def _():
        gb = j % GATHER_BUFS
        ahead = GATHER_BUFS - 1
        nxt = j + ahead
        nxt_b = nxt % GATHER_BUFS

        @pl.when(j == 0)
        def _():
            ybuf[SCATTER_BUFS - 1] = jnp.zeros(ybuf.shape[1:], BF16)
            for k in range(ahead):
                for s in range(FFN_SLOTS):
                    gather_copy(k, s, k).start()

        for s in range(FFN_SLOTS):
            gather_copy(j, s, gb).wait()

        @pl.when(j >= SCATTER_BUFS - 1)
        def _():
            for s in range(FFN_SLOTS):
                scatter_copy(j - SCATTER_BUFS, s).wait()

        @pl.when(jnp.logical_or(j == 0, texp_ref[j] != texp_ref[jnp.maximum(j - 1, 0)]))
        def _():
            wg_b[...] = wg_ref[...].astype(BF16)
            wu_b[...] = wu_ref[...].astype(BF16)
            wd_b[...] = wd_ref[...].astype(BF16)

        x = xbuf[gb].reshape(FFN_ROWS, D_MODEL)
        gate = _dot(x, wg_b[...])
        for s in range(FFN_SLOTS):
            scatter_copy(j - 1, s).start()
        up = _dot(x, wu_b[...])
        for s in range(FFN_SLOTS):
            gather_copy(nxt, s, nxt_b).start()
        hid = (gate * (1.0 / (1.0 + jnp.exp(-gate))) * up).astype(BF16)
        ybuf[j % SCATTER_BUFS] = _dot(hid, wd_b[...]).astype(BF16).reshape(FFN_SLOTS, ROW_GRANULE, D_MODEL)

        @pl.when(j == n_tiles - 1)
        def _():
            for k in range(1, GATHER_BUFS):
                for s in range(FFN_SLOTS):
                    gather_copy(nxt, s, (j + k) % GATHER_BUFS).wait()
            for s in range(FFN_SLOTS):
                scatter_copy(j, s).start()
            for back in range(SCATTER_BUFS):
                @pl.when(j - back >= -1)
                def _():
                    for s in range(FFN_SLOTS):
                        scatter_copy(j - back, s).wait()


def _combine_kernel(xmid_ref, yl_hbm, meta_ref, gf_ref, y_ref, ybuf, ysem, *, first_tile, n_tiles):
    n_tok = xmid_ref.shape[0]
    i = pl.program_id(0)
    ahead = COMBINE_BUFS - 1

    def fetch(tile):
        row0 = pl.multiple_of((first_tile + tile) * SORT_ROWS, SORT_ROWS)
        slot = tile % COMBINE_BUFS
        return pltpu.make_async_copy(yl_hbm.at[pl.ds(row0, SORT_ROWS), :], ybuf.at[slot], ysem.at[slot])

    @pl.when(i == 0)
    def _():
        for k in range(min(ahead, n_tiles)):
            fetch(k).start()

    @pl.when(i + ahead < n_tiles)
    def _():
        fetch(i + ahead).start()

    fetch(i).wait()
    yl_ref = ybuf.at[i % COMBINE_BUFS]
    meta = meta_ref[...]
    meta_t = jnp.concatenate([meta, jnp.zeros((LANES - 8, n_tok), F32)], axis=0).T
    half = max(n_tok // 2, LANES)
    for r0 in range(0, n_tok, half):
        rows = slice(r0, r0 + half)
        d1, d2, w1, w2 = (meta_t[rows, i:i + 1] for i in range(4))
        acc = xmid_ref[rows, :]
        for c in range(SORT_ROWS // SORT_CHUNK):
            r_iota = (lax.broadcasted_iota(jnp.int32, (half, SORT_CHUNK), 1) + c * SORT_CHUNK).astype(F32)
            unsort = jnp.where(r_iota == d1, w1, jnp.where(r_iota == d2, w2, 0.0)).astype(BF16)
            acc = acc + _dot(unsort, yl_ref[c * SORT_CHUNK:(c + 1) * SORT_CHUNK, :])
        y_ref[rows, :] = _rmsnorm(acc, gf_ref[...])


def _ffn_schedule(tab):
    n_tok_tiles = tab.shape[0]
    strips = (tab[:, :, 0] * (1.0 / ROW_GRANULE)).astype(jnp.int32)
    starts = (tab[:, :, 1] * (1.0 / ROW_GRANULE)).astype(jnp.int32)
    cnt = strips.T
    row0 = starts.T
    cs = jnp.cumsum(cnt, axis=1) - cnt
    n_str = jnp.sum(cnt, axis=1)
    np_str = (n_str + FFN_SLOTS - 1) // FFN_SLOTS * FFN_SLOTS
    ends = jnp.cumsum(np_str)
    base = ends - np_str
    n_steps_max = _ffn_steps_max(n_tok_tiles) + GATHER_BUFS - 1
    step0 = jnp.arange(n_steps_max, dtype=jnp.int32) * FFN_SLOTS
    stream = jnp.minimum(jnp.sum(ends[None, :] <= step0[:, None], axis=1), N_EXPERTS - 1)
    pick = stream[:, None] == jnp.arange(N_EXPERTS, dtype=jnp.int32)[None, :]
    sel = lambda x: jnp.sum(jnp.where(pick[:, :, None], x[None], 0), axis=1)
    cs_j, cnt_j, row0_j = sel(cs), sel(cnt), sel(row0)
    base_j = jnp.sum(jnp.where(pick, base[None, :], 0), axis=1)
    q = step0[:, None] + jnp.arange(FFN_SLOTS, dtype=jnp.int32)[None, :] - base_j[:, None]
    reached = cs_j[:, None, :] <= q[:, :, None]
    last = lambda x: jnp.sum(jnp.where(reached, jnp.diff(x, axis=1, prepend=0)[:, None, :], 0), axis=2)
    tile_idx = jnp.sum(reached, axis=2).astype(jnp.int32) - 1
    g = q - last(cs_j)
    valid = g < last(cnt_j)
    granule = tile_idx * GRANULES_PER_TILE + last(row0_j) + g
    assert n_tok_tiles >= 2 * FFN_SLOTS
    slot = jnp.arange(FFN_SLOTS, dtype=jnp.int32)[None, :]
    parity = jnp.arange(n_steps_max, dtype=jnp.int32)[:, None] % 2
    pad_dst = lambda par: (slot + par * FFN_SLOTS) * GRANULES_PER_TILE + GRANULES_PER_TILE - 1
    gsrc = jnp.where(valid, granule, slot * GRANULES_PER_TILE + GRANULES_PER_TILE - 2).astype(jnp.int32)
    sdst = jnp.where(valid, granule, pad_dst(parity))
    sdst = jnp.concatenate([pad_dst(1), sdst], axis=0)
    texp = stream.astype(jnp.int32)
    n_steps = (ends[-1] // FFN_SLOTS).astype(jnp.int32).reshape(1)
    return texp, gsrc.reshape(-1), sdst.astype(jnp.int32).reshape(-1), n_steps


assert GRANULES_PER_TILE - (2 * TOK_TILE + N_EXPERTS * (ROW_GRANULE - 1)) // ROW_GRANULE >= 2


def _ffn_steps_max(n_tok_tiles):
    return -(-(n_tok_tiles * GRANULES_PER_TILE + N_EXPERTS * (FFN_SLOTS - 1)) // FFN_SLOTS)


def _vmem_limit(mib):
    n_bytes = mib * 1024 * 1024
    assert n_bytes < V7X_VMEM_BYTES
    return n_bytes


def _rope_tables(pos):
    inv_freq = ROPE_THETA ** (-jnp.arange(HALF, dtype=F32) * 2.0 / HEAD_DIM)
    ang = pos.astype(F32)[:, None] * inv_freq[None, :]
    cos, sin = jnp.cos(ang), jnp.sin(ang)
    reps = LANES // HEAD_DIM
    return jnp.tile(jnp.concatenate([cos, cos], axis=1), (1, reps)), jnp.tile(jnp.concatenate([-sin, sin], axis=1), (1, reps))


def _moe_combine(xmid, yl, meta, gf, n_tok, first_tile, n_tiles):
    return pl.pallas_call(
        functools.partial(_combine_kernel, first_tile=first_tile, n_tiles=n_tiles),
        grid=(n_tiles,),
        in_specs=[
            pl.BlockSpec((n_tok, D_MODEL), lambda i: (i, 0)),
            pl.BlockSpec(memory_space=pl.ANY),
            pl.BlockSpec((None, 8, n_tok), lambda i: (i, 0, 0)),
            pl.BlockSpec((1, D_MODEL), lambda i: (0, 0)),
        ],
        out_specs=pl.BlockSpec((n_tok, D_MODEL), lambda i: (i, 0)),
        out_shape=jax.ShapeDtypeStruct((n_tiles * n_tok, D_MODEL), F32),
        scratch_shapes=[pltpu.VMEM((COMBINE_BUFS, SORT_ROWS, D_MODEL), BF16), pltpu.SemaphoreType.DMA((COMBINE_BUFS,))],
        compiler_params=pltpu.CompilerParams(
            dimension_semantics=("arbitrary",),
            vmem_limit_bytes=_vmem_limit(40)),
        name="moe_combine",
    )(xmid, yl, meta, gf)


def kernel(x_prompt, x_sample, cache_swa_k, cache_swa_v, norm_mix_g, w_in, attn_sinks, gmlp_ln_g, gmlp_ln_b,
           gmlp_w_s, gmlp_b_s, w_out, norm_ffn_g, router_group_w, router_group_b, router_expert_w,
           router_expert_b, expert_w_gate, expert_w_up, expert_w_down, final_norm_g):
    assert norm_mix_g.shape[0] == 1, "single-layer trunk"
    batch, seq, _ = x_prompt.shape
    dec_batch = x_sample.shape[0]
    assert x_sample.shape[1] == 1 and seq % TOK_TILE == 0

    win_ext = w_in[0].astype(BF16)
    wout = w_out[0].astype(BF16)
    g1 = norm_mix_g[0][None, :]
    g2 = norm_ffn_g[0][None, :]
    gf = final_norm_g[None, :]
    lng = gmlp_ln_g[0][None, :]
    lnb = gmlp_ln_b[0][None, :]
    sinks = attn_sinks[0]
    ws = gmlp_w_s[0]
    group_dim = GMLP_WIDTH // GMLP_GROUPS
    bsf = jnp.repeat(gmlp_b_s[0].T, group_dim, axis=1)
    ws0 = jnp.repeat(ws[:, 0, 0], group_dim)[None, :]
    bs0 = jnp.repeat(gmlp_b_s[0][:, 0], group_dim)[None, :]
    wrt = jnp.zeros((ROUTER_ROWS, D_MODEL), F32)
    wrt = wrt.at[0:N_GROUPS].set(router_group_w[0].T)
    wrt = wrt.at[8:8 + N_EXPERTS].set(router_expert_w[0].reshape(D_MODEL, N_EXPERTS).T).astype(BF16)
    brt = jnp.full((ROUTER_ROWS, 1), NEG_INF, F32)
    brt = brt.at[0:N_GROUPS, 0].set(router_group_b[0])
    brt = brt.at[8:8 + N_EXPERTS, 0].set(router_expert_b[0].reshape(N_EXPERTS))
    wg, wu, wd = expert_w_gate[0], expert_w_up[0], expert_w_down[0]
    cos_p, sin_p = _rope_tables(jnp.arange(seq, dtype=jnp.int32))
    cos_s, sin_s = _rope_tables(PAST_LEN + jnp.arange(1, dtype=jnp.int32))

    full = lambda shape: pl.BlockSpec(shape, lambda *_: (0,) * len(shape))
    smem = pl.BlockSpec(memory_space=pltpu.SMEM)
    n_tiles = seq // TOK_TILE

    upper = jnp.triu(jnp.ones((TOK_TILE, TOK_TILE), BF16), k=1)
    lpad = (jnp.arange(LANES)[None, :] < jnp.arange(N_EXPERTS)[:, None]).astype(BF16)
    n_tok_tiles = batch * n_tiles

    x2d = x_prompt.reshape(batch * seq, D_MODEL)
    tile_a = lambda s: jnp.minimum(s, n_tok_tiles - 1)
    cur = lambda s: jnp.maximum(s - 1, 0)
    tile_b = cur
    xmid_p, xl_p, meta_p, tab_p, k_p, v_p = pl.pallas_call(
        functools.partial(_prompt_mixer_kernel, tiles_per_seq=n_tiles),
        grid=(n_tok_tiles + 1,),
        in_specs=[
            smem,
            pl.BlockSpec((TOK_TILE, D_MODEL), lambda s: (tile_a(s), 0)),
            pl.BlockSpec((TOK_TILE, D_MODEL), lambda s: (cur(s), 0)),
            full((1, D_MODEL)),
            full((D_MODEL, IN_WIDTH)),
            pl.BlockSpec((TOK_TILE, LANES), lambda s: (tile_b(s) % n_tiles, 0)),
            pl.BlockSpec((TOK_TILE, LANES), lambda s: (tile_b(s) % n_tiles, 0)),
            full((1, GMLP_WIDTH)),
            full((1, GMLP_WIDTH)),
            full((GMLP_GROUPS, CHUNK, CHUNK)),
            full((CHUNK, GMLP_WIDTH)),
            full((D_MODEL, D_MODEL)),
            full((1, D_MODEL)),
            full((ROUTER_ROWS, D_MODEL)),
            full((ROUTER_ROWS, 1)),
            full((TOK_TILE, TOK_TILE)),
            full((N_EXPERTS, LANES)),
        ],
        out_specs=[
            pl.BlockSpec((TOK_TILE, D_MODEL), lambda s: (cur(s), 0)),
            pl.BlockSpec((None, SORT_ROWS, D_MODEL), lambda s: (jnp.where(s == 0, n_tok_tiles, s - 1), 0, 0)),
            pl.BlockSpec((None, 8, TOK_TILE), lambda s: (cur(s), 0, 0)),
            pl.BlockSpec((None, N_EXPERTS, LANES), lambda s: (cur(s), 0, 0)),
            pl.BlockSpec((None, WINDOW, KV_WIDTH), lambda s: (tile_b(s) // n_tiles, 0, 0)),
            pl.BlockSpec((None, WINDOW, KV_WIDTH), lambda s: (tile_b(s) // n_tiles, 0, 0)),
        ],
        out_shape=[
            jax.ShapeDtypeStruct((batch * seq, D_MODEL), F32),
            jax.ShapeDtypeStruct((n_tok_tiles + 1, SORT_ROWS, D_MODEL), BF16),
            jax.ShapeDtypeStruct((n_tok_tiles, 8, TOK_TILE), F32),
            jax.ShapeDtypeStruct((n_tok_tiles, N_EXPERTS, LANES), F32),
            jax.ShapeDtypeStruct((batch, WINDOW, KV_WIDTH), F32),
            jax.ShapeDtypeStruct((batch, WINDOW, KV_WIDTH), F32),
        ],
        scratch_shapes=[
            pltpu.VMEM((TOK_TILE, IN_WIDTH), F32),
            pltpu.VMEM((TOK_TILE, IN_WIDTH), F32),
            pltpu.VMEM((TOK_TILE, D_MODEL), BF16),
        ] + [pltpu.VMEM((WINDOW + TOK_TILE, KV_WIDTH), BF16)] * 4 + [
            pltpu.VMEM((len(_NAT_HEADS), TOK_TILE, LANES), BF16),
            pltpu.VMEM((len(_SWP_HEADS), TOK_TILE, LANES), BF16),
            pltpu.VMEM((2 * BLOCKS_PER_TILE, len(_NAT_HEADS) * WINDOW, 2 * WINDOW), F32),
            pltpu.VMEM((2 * BLOCKS_PER_TILE, len(_NAT_HEADS) * WINDOW, 2 * WINDOW), BF16),
            pltpu.VMEM((TOK_TILE, GMLP_WIDTH), BF16),
            pltpu.VMEM((TOK_TILE, D_MODEL), BF16),
            pltpu.VMEM((TOK_TILE, D_MODEL), BF16),
        ],
        compiler_params=pltpu.CompilerParams(
            dimension_semantics=("arbitrary",),
            vmem_limit_bytes=_vmem_limit(56)),
        name="prompt_mixer",
    )(sinks, x2d, x2d, g1, win_ext, cos_p, sin_p, lng, lnb, ws, bsf, wout, g2, wrt, brt, upper, lpad)

    xs = x_sample.reshape(dec_batch, D_MODEL)
    ck = cache_swa_k[0].reshape(dec_batch, WINDOW, KV_WIDTH)
    cv = cache_swa_v[0].reshape(dec_batch, WINDOW, KV_WIDTH)
    vmem = pl.BlockSpec(memory_space=pltpu.VMEM)
    hbm = pl.BlockSpec(memory_space=pl.ANY)
    xmid_s, meta_s, tab_s, k_s, v_s, vn_s, xl_all = pl.pallas_call(
        _sample_mixer_kernel,
        in_specs=[smem] + [vmem] * 17 + [hbm],
        out_specs=[vmem] * 6 + [hbm],
        out_shape=[
            jax.ShapeDtypeStruct((dec_batch, D_MODEL), F32),
            jax.ShapeDtypeStruct((8, dec_batch), F32),
            jax.ShapeDtypeStruct((N_EXPERTS, LANES), F32),
            jax.ShapeDtypeStruct((dec_batch, KV_WIDTH), F32),
            jax.ShapeDtypeStruct((dec_batch, KV_WIDTH), F32),
            jax.ShapeDtypeStruct((dec_batch, GMLP_WIDTH), F32),
            jax.ShapeDtypeStruct(xl_p.shape, BF16),
        ],
        scratch_shapes=[pltpu.VMEM((dec_batch, D_MODEL), BF16), pltpu.VMEM((SORT_ROWS, D_MODEL), BF16),
                        pltpu.SemaphoreType.DMA((1,))],
        compiler_params=pltpu.CompilerParams(vmem_limit_bytes=_vmem_limit(56)),
        input_output_aliases={18: 6},
        name="sample_mixer",
    )(sinks, xs, ck, cv, g1, win_ext, cos_s, sin_s, lng, lnb, ws0, bs0, wout, g2, wrt, brt,
      upper[:dec_batch, :dec_batch], lpad, xl_p)

    n_all_tiles = n_tok_tiles + 1
    tab_all = jnp.concatenate([tab_p, tab_s[None]], axis=0)
    texp, gsrc, sdst, n_steps = _ffn_schedule(tab_all)
    yl_all = pl.pallas_call(
        _grouped_ffn_kernel,
        grid_spec=pltpu.PrefetchScalarGridSpec(
            num_scalar_prefetch=4,
            grid=(_ffn_steps_max(n_all_tiles),),
            in_specs=[
                pl.BlockSpec(memory_space=pl.ANY),
                pl.BlockSpec((None, D_MODEL, D_EXPERT), lambda j, te, gs, sd, ns: (te[j], 0, 0)),
                pl.BlockSpec((None, D_MODEL, D_EXPERT), lambda j, te, gs, sd, ns: (te[j], 0, 0)),
                pl.BlockSpec((None, D_EXPERT, D_MODEL), lambda j, te, gs, sd, ns: (te[j], 0, 0)),
            ],
            out_specs=pl.BlockSpec(memory_space=pl.ANY),
            scratch_shapes=[
                pltpu.VMEM((GATHER_BUFS, FFN_SLOTS, ROW_GRANULE, D_MODEL), BF16),
                pltpu.VMEM((SCATTER_BUFS, FFN_SLOTS, ROW_GRANULE, D_MODEL), BF16),
                pltpu.VMEM((D_MODEL, D_EXPERT), BF16),
                pltpu.VMEM((D_MODEL, D_EXPERT), BF16),
                pltpu.VMEM((D_EXPERT, D_MODEL), BF16),
                pltpu.SemaphoreType.DMA((GATHER_BUFS,)),
                pltpu.SemaphoreType.DMA((SCATTER_BUFS,)),
            ],
        ),
        out_shape=jax.ShapeDtypeStruct((n_all_tiles * GRANULES_PER_TILE, ROW_GRANULE, D_MODEL), BF16),
        compiler_params=pltpu.CompilerParams(
            dimension_semantics=("arbitrary",),
            vmem_limit_bytes=_vmem_limit(32)),
        input_output_aliases={4: 0},
        name="grouped_ffn",
    )(texp, gsrc, sdst, n_steps, xl_all.reshape(n_all_tiles * GRANULES_PER_TILE, ROW_GRANULE, D_MODEL), wg, wu, wd)
    yl_all = yl_all.reshape(n_all_tiles * SORT_ROWS, D_MODEL)

    y_p = _moe_combine(xmid_p, yl_all, meta_p, gf, TOK_TILE, 0, n_tok_tiles)
    y_s = _moe_combine(xmid_s, yl_all, meta_s[None], gf, dec_batch, n_tok_tiles, 1)

    return (y_p.reshape(batch, seq, D_MODEL),
            y_s.reshape(dec_batch, 1, D_MODEL),
            k_p.reshape(1, batch, WINDOW, N_KV_HEADS, HEAD_DIM),
            v_p.reshape(1, batch, WINDOW, N_KV_HEADS, HEAD_DIM),
            k_s.reshape(1, dec_batch, 1, N_KV_HEADS, HEAD_DIM),
            v_s.reshape(1, dec_batch, 1, N_KV_HEADS, HEAD_DIM),
            vn_s.reshape(1, dec_batch, 1, GMLP_WIDTH))
```

```python
import functools

import jax
import jax.numpy as jnp
import numpy as np
from jax import lax
from jax.experimental import pallas as pl
from jax.experimental.pallas import tpu as pltpu

F32 = jnp.float32
BF16 = jnp.bfloat16

D_MODEL = 1024
HEAD_DIM = 64
HALF = HEAD_DIM // 2
N_Q_HEADS = 8
N_KV_HEADS = 2
ATTN_WIDTH = N_Q_HEADS * HEAD_DIM
KV_WIDTH = N_KV_HEADS * HEAD_DIM
WINDOW = 128
ROPE_THETA = 10000.0
GMLP_WIDTH = D_MODEL - ATTN_WIDTH
GMLP_GROUPS = 8
CHUNK = 128
N_GROUPS = 4
EXPERTS_PER_GROUP = 8
N_EXPERTS = N_GROUPS * EXPERTS_PER_GROUP
D_EXPERT = 256
EPS = 1e-6
NEG_INF = -1e30
PAST_LEN = 16384

LANES = 128
V7X_VMEM_BYTES = 64 * 1024 * 1024

C_Q = 0
C_K = C_Q + ATTN_WIDTH
C_V = C_K + KV_WIDTH
C_U = C_V + KV_WIDTH
C_VG = C_U + GMLP_WIDTH
IN_WIDTH = C_VG + GMLP_WIDTH

ROUTER_ROWS = 48
TOK_TILE = 512
BLOCKS_PER_TILE = TOK_TILE // WINDOW
ROW_GRANULE = 16
SORT_CHUNK = 512
SORT_ROWS = -(-(2 * TOK_TILE + N_EXPERTS * (ROW_GRANULE - 1)) // SORT_CHUNK) * SORT_CHUNK
GRANULES_PER_TILE = SORT_ROWS // ROW_GRANULE
FFN_ROWS = 512
FFN_SLOTS = FFN_ROWS // ROW_GRANULE
GATHER_BUFS = 3
SCATTER_BUFS = 3
COMBINE_BUFS = 3


def _dot(a, b):
    return jnp.dot(a, b, preferred_element_type=F32)


def _dot_nt(a, b):
    return lax.dot_general(a, b, (((1,), (1,)), ((), ())), preferred_element_type=F32)


def _gelu(x):
    return 0.5 * x * (1.0 + lax.erf(x * np.float32(np.sqrt(0.5))))


def _rmsnorm(x, g):
    return x * lax.rsqrt(jnp.mean(x * x, axis=-1, keepdims=True) + EPS) * g


def _layernorm(x, g, b):
    mu = jnp.mean(x, axis=-1, keepdims=True)
    xc = x - mu
    return xc * lax.rsqrt(jnp.mean(xc * xc, axis=-1, keepdims=True) + EPS) * g + b


def _first_argmax_rows(x, row_iota, n_rows):
    m = jnp.max(x, axis=0, keepdims=True)
    idx = jnp.min(jnp.where(x == m, row_iota, n_rows), axis=0, keepdims=True)
    return m, idx


def _route(h2b, wrt_ref, brt_ref):
    n_tok = h2b.shape[0]
    lt = _dot_nt(wrt_ref[...], h2b) + brt_ref[...]
    row8 = lax.broadcasted_iota(jnp.int32, (8, n_tok), 0).astype(F32)
    glog = lt[0:8]
    gmax, gidx = _first_argmax_rows(glog, row8, 8)
    g_w = 1.0 / jnp.sum(jnp.exp(glog - gmax), axis=0, keepdims=True)
    esel = lt[8:16]
    for g in range(1, N_GROUPS):
        esel = jnp.where(gidx == g, lt[8 + 8 * g:16 + 8 * g], esel)
    m1, i1 = _first_argmax_rows(esel, row8, 8)
    esel2 = jnp.where(row8 == i1, -jnp.inf, esel)
    m2, i2 = _first_argmax_rows(esel2, row8, 8)
    r = jnp.exp(m2 - m1)
    w1 = 1.0 / (1.0 + r)
    w2 = r / (1.0 + r)
    e1 = gidx * EXPERTS_PER_GROUP + i1
    e2 = gidx * EXPERTS_PER_GROUP + i2
    slab = jnp.where(row8 == 0, e1, 0.0)
    for r_idx, val in ((1, e2), (2, w1 * g_w), (3, w2 * g_w)):
        slab = jnp.where(row8 == r_idx, val, slab)
    return slab


def _softmax_with_sink(s, sink):
    m = jnp.maximum(jnp.max(s, axis=-1, keepdims=True), sink)
    e = jnp.exp(s - m)
    den = jnp.sum(e, axis=-1, keepdims=True) + jnp.exp(sink - m)
    return e * (1.0 / den)


def _rope(x, cos, sin_signed):
    first_half = (lax.broadcasted_iota(jnp.int32, x.shape, 1) & HALF) == 0
    partner = jnp.where(first_half, pltpu.roll(x, LANES - HALF, 1), pltpu.roll(x, HALF, 1))
    return x * cos + partner * sin_signed


_NAT_HEADS = (0, 2, 5, 7)
_SWP_HEADS = (1, 3, 4, 6)


def _prompt_mixer_kernel(sinks_ref, x_next_ref, x_ref, g1_ref, win_ref, cos_ref, sin_ref, lng_ref, lnb_ref, ws_ref,
                         bsf_ref, wout_ref, g2_ref, wrt_ref, brt_ref, upper_ref, lpad_ref,
                         xmid_ref, xl_ref, meta_ref, tab_ref, kout_ref, vout_ref,
                         z_a, z_b, mix_ref, k_n, k_s, v_n, v_s, q_nat, q_swp, s_ref, p_ref, vn_ref, h_ref, h2_ref,
                         *, tiles_per_seq):
    step = pl.program_id(0)
    t = jnp.maximum(step - 1, 0) % tiles_per_seq
    kv_bufs = (k_n, k_s, v_n, v_s)

    @pl.when(step == 0)
    def _():
        z_b[...] = jnp.zeros_like(z_b)

    @pl.when(t == 0)
    def _():
        for ref in kv_bufs:
            ref[0:WINDOW, :] = jnp.zeros((WINDOW, KV_WIDTH), BF16)

    args = (t, sinks_ref, x_next_ref, x_ref, g1_ref, win_ref, cos_ref, sin_ref, lng_ref, lnb_ref, ws_ref, bsf_ref,
            wout_ref, g2_ref, wrt_ref, brt_ref, upper_ref, lpad_ref, xmid_ref, xl_ref, meta_ref, tab_ref, kout_ref,
            vout_ref, kv_bufs, q_nat, q_swp, s_ref, p_ref, vn_ref, h_ref, h2_ref)

    @pl.when(step % 2 == 0)
    def _():
        _prompt_mixer_body(z_a, z_b, mix_ref, *args)

    @pl.when(step % 2 == 1)
    def _():
        _prompt_mixer_body(z_b, z_a, mix_ref, *args)


def _prompt_mixer_body(z_next, z_ref, mix_ref, t, sinks_ref, x_next_ref, x_ref, g1_ref, win_ref, cos_ref,
                       sin_ref, lng_ref, lnb_ref, ws_ref, bsf_ref, wout_ref, g2_ref, wrt_ref, brt_ref, upper_ref,
                       lpad_ref, xmid_ref, xl_ref, meta_ref, tab_ref, kout_ref, vout_ref, kv_bufs,
                       q_nat, q_swp, s_ref, p_ref, vn_ref, h_ref, h2_ref):
    k_n, k_s, v_n, v_s = kv_bufs
    cos = cos_ref[...]
    sin = sin_ref[...]
    lane = lax.broadcasted_iota(jnp.int32, (WINDOW, LANES), 1)
    lo = lane < HEAD_DIM
    row = lax.broadcasted_iota(jnp.int32, (WINDOW, WINDOW), 0)
    col = lax.broadcasted_iota(jnp.int32, (WINDOW, WINDOW), 1)
    mask_cur = col <= row
    mask_prev_band = col >= row
    mask_prev_first = jnp.logical_and(mask_prev_band, (jnp.zeros_like(row) + t) > 0)
    mask_band = jnp.concatenate([mask_prev_band, mask_cur], axis=1)
    mask_first = jnp.concatenate([mask_prev_first, mask_cur], axis=1)

    cq = cos * np.float32(HEAD_DIM ** -0.5)
    sq = sin * np.float32(HEAD_DIM ** -0.5)
    kf = _rope(z_ref[:, C_K:C_K + KV_WIDTH], cos, sin)
    vf = z_ref[:, C_V:C_V + KV_WIDTH]
    k_n[WINDOW:, :] = kf.astype(BF16)
    k_s[WINDOW:, :] = pltpu.roll(kf, HEAD_DIM, 1).astype(BF16)
    v_n[WINDOW:, :] = vf.astype(BF16)
    v_s[WINDOW:, :] = pltpu.roll(vf, HEAD_DIM, 1).astype(BF16)

    kout_ref[...] = kf[TOK_TILE - WINDOW:]
    vout_ref[...] = vf[TOK_TILE - WINDOW:]

    lo_t = lax.broadcasted_iota(jnp.int32, (TOK_TILE, LANES), 1) < HEAD_DIM
    for m in range(N_Q_HEADS // 2):
        qc = _rope(z_ref[:, C_Q + m * LANES:C_Q + (m + 1) * LANES], cq, sq)
        for hd, qh in ((2 * m, jnp.where(lo_t, qc, 0.0)), (2 * m + 1, jnp.where(lo_t, 0.0, qc))):
            if hd in _NAT_HEADS:
                q_nat[_NAT_HEADS.index(hd)] = qh.astype(BF16)
            else:
                q_swp[_SWP_HEADS.index(hd)] = qh.astype(BF16)

    stacks = ((q_nat, _NAT_HEADS, k_n, v_n), (q_swp, _SWP_HEADS, k_s, v_s))
    proj_cols = 2 * LANES

    def project_next(c):
        cols = slice(c * proj_cols, (c + 1) * proj_cols)
        z_next[:, cols] = _dot(h_ref[...], win_ref[:, cols])

    def scores(j):
        rows = slice(j * WINDOW, (j + 1) * WINDOW)
        keys = slice(j * WINDOW, (j + 2) * WINDOW)
        for si, (q_ref, _, k_buf, _) in enumerate(stacks):
            q_stack = jnp.concatenate([q_ref[i, rows, :] for i in range(len(_NAT_HEADS))], axis=0)
            s_ref[2 * j + si] = _dot_nt(q_stack, k_buf[keys, :])

    def softmax(j):
        mask = mask_first if j == 0 else mask_band
        for si, (_, heads, _, _) in enumerate(stacks):
            for i, hd in enumerate(heads):
                pr = slice(i * WINDOW, (i + 1) * WINDOW)
                p_ref[2 * j + si, pr, :] = _softmax_with_sink(
                    jnp.where(mask, s_ref[2 * j + si, pr, :], NEG_INF), sinks_ref[hd]).astype(BF16)

    def attend(j):
        rows = slice(j * WINDOW, (j + 1) * WINDOW)
        keys = slice(j * WINDOW, (j + 2) * WINDOW)
        o_nat = _dot(p_ref[2 * j], v_n[keys, :])
        o_swp = _dot(p_ref[2 * j + 1], v_s[keys, :])
        for m in range(N_Q_HEADS // 2):
            pr = slice(m * WINDOW, (m + 1) * WINDOW)
            even_nat = (2 * m) in _NAT_HEADS
            att = jnp.where(lo, o_nat[pr], o_swp[pr]) if even_nat else jnp.where(lo, o_swp[pr], o_nat[pr])
            mix_ref[rows, m * LANES:(m + 1) * LANES] = att.astype(BF16)

    def gmlp(m):
        cs = slice(m * LANES, (m + 1) * LANES)
        w0 = jnp.where(mask_cur, ws_ref[2 * m], 0.0).astype(BF16)
        w1 = jnp.where(mask_cur, ws_ref[2 * m + 1], 0.0).astype(BF16)
        wcat = jnp.concatenate([w0, w1], axis=1)
        for j in range(BLOCKS_PER_TILE):
            rows = slice(j * WINDOW, (j + 1) * WINDOW)
            vcol = vn_ref[rows, cs]
            rhs = jnp.concatenate([jnp.where(lo, vcol, jnp.zeros_like(vcol)),
                                   jnp.where(lo, jnp.zeros_like(vcol), vcol)], axis=0)
            sp = _dot(wcat, rhs) + bsf_ref[:, cs]
            u = _gelu(z_ref[rows, C_U + m * LANES:C_U + (m + 1) * LANES])
            mix_ref[rows, ATTN_WIDTH + m * LANES:ATTN_WIDTH + (m + 1) * LANES] = (u * sp).astype(BF16)

    def project_out(c):
        cols = slice(c * proj_cols, (c + 1) * proj_cols)
        xmid_ref[:, cols] = x_ref[:, cols] + _dot(mix_ref[...], wout_ref[:, cols])

    h_ref[...] = _rmsnorm(x_next_ref[...], g1_ref[...]).astype(BF16)
    project_next(0)
    project_next(1)
    for j in range(BLOCKS_PER_TILE):
        scores(j)
    vn_ref[...] = _layernorm(_gelu(z_ref[:, C_VG:C_VG + GMLP_WIDTH]), lng_ref[...], lnb_ref[...]).astype(BF16)
    project_next(2)
    softmax(0)
    project_next(3)
    softmax(1)
    project_next(4)
    softmax(2)
    project_next(5)
    softmax(3)
    project_next(6)
    for j in range(BLOCKS_PER_TILE):
        attend(j)
    for m in range(GMLP_GROUPS // 2):
        gmlp(m)
    for c in range(D_MODEL // proj_cols):
        project_out(c)
    h2_ref[...] = _rmsnorm(xmid_ref[...], g2_ref[...]).astype(BF16)
    d1, d2 = _sort_plan(_route(h2_ref[...], wrt_ref, brt_ref), upper_ref, lpad_ref, meta_ref, tab_ref)
    _sort_rows(d1, d2, h2_ref[...], xl_ref)

    for ref in kv_bufs:
        ref[0:WINDOW, :] = ref[TOK_TILE:TOK_TILE + WINDOW, :]


def _sort_plan(slab, upper_ref, lpad_ref, meta_ref, tab_ref):
    n_tok = slab.shape[1]
    e1, e2 = slab[0:1], slab[1:2]
    row32 = lax.broadcasted_iota(jnp.int32, (N_EXPERTS, n_tok), 0).astype(F32)
    sel1 = row32 == e1
    sel2 = row32 == e2
    onehot = jnp.where(sel1, 1.0, jnp.where(sel2, 1.0, 0.0))
    earlier = _dot(onehot.astype(BF16), upper_ref[...])
    cnt = jnp.sum(onehot, axis=1, keepdims=True)
    pc = jnp.floor((cnt + (ROW_GRANULE - 1)) * (1.0 / ROW_GRANULE)) * ROW_GRANULE
    pc_b = jnp.broadcast_to(pc, (N_EXPERTS, LANES))
    pc_pad = jnp.concatenate([pc_b, jnp.zeros((LANES - N_EXPERTS, LANES), F32)], axis=0).astype(BF16)
    start = _dot(lpad_ref[...], pc_pad)
    base = start[:, 0:1] + earlier
    d1 = jnp.sum(jnp.where(sel1, base, 0.0), axis=0, keepdims=True)
    d2 = jnp.sum(jnp.where(sel2, base, 0.0), axis=0, keepdims=True)
    row8 = lax.broadcasted_iota(jnp.int32, (8, n_tok), 0)
    meta_ref[...] = jnp.where(row8 == 0, d1, jnp.where(row8 == 1, d2, jnp.where(row8 >= 4, 0.0, slab)))
    lane = lax.broadcasted_iota(jnp.int32, (N_EXPERTS, LANES), 1)
    tab_ref[...] = jnp.where(lane == 0, pc_b, jnp.where(lane == 1, start, 0.0))
    return d1, d2


def _sort_rows(d1, d2, h2b, xl_ref):
    n_tok = h2b.shape[0]
    for c in range(SORT_ROWS // SORT_CHUNK):
        r_iota = (lax.broadcasted_iota(jnp.int32, (SORT_CHUNK, n_tok), 0) + c * SORT_CHUNK).astype(F32)
        perm = jnp.where(r_iota == d1, 1.0, jnp.where(r_iota == d2, 1.0, 0.0)).astype(BF16)
        xl_ref[c * SORT_CHUNK:(c + 1) * SORT_CHUNK, :] = _dot(perm, h2b).astype(BF16)


def _sample_mixer_kernel(sinks_ref, x_ref, ck_ref, cv_ref, g1_ref, win_ref, cos_ref, sin_ref, lng_ref, lnb_ref,
                         ws0_ref, bs0_ref, wout_ref, g2_ref, wrt_ref, brt_ref, upper_ref, lpad_ref, xl_in_hbm,
                         xmid_ref, meta_ref, tab_ref, kout_ref, vout_ref, vnout_ref, xl_hbm,
                         mix_ref, xl_tile, xl_sem):
    del xl_in_hbm
    n_seq = x_ref.shape[0]
    seq_chunk = 16
    x = x_ref[...]
    h = _rmsnorm(x, g1_ref[...]).astype(BF16)
    z = _dot(h, win_ref[...])
    cos = cos_ref[...]
    sin = sin_ref[...]
    scale = np.float32(HEAD_DIM ** -0.5)
    lane = lax.broadcasted_iota(jnp.int32, (n_seq, LANES), 1)
    lo = lane < HEAD_DIM
    kf = _rope(z[:, C_K:C_K + KV_WIDTH], cos, sin)
    vf = z[:, C_V:C_V + KV_WIDTH]
    kout_ref[...] = kf
    vout_ref[...] = vf
    kb = kf.astype(BF16).astype(F32)
    vb = vf.astype(BF16).astype(F32)

    q_heads = []
    for hd in range(N_Q_HEADS):
        m = hd // 2
        qc = _rope(z[:, C_Q + m * LANES:C_Q + (m + 1) * LANES], cos, sin) * scale
        keep = lo if hd % 2 == 0 else ~lo
        qm = jnp.where(keep, qc, 0.0)
        if (hd % 2) != (hd // (N_Q_HEADS // N_KV_HEADS)):
            qm = pltpu.roll(qm, HEAD_DIM, 1)
        q_heads.append(qm.astype(BF16))

    s_new = [jnp.sum(q_heads[hd].astype(F32) * kb, axis=-1, keepdims=True) for hd in range(N_Q_HEADS)]

    rr = lax.broadcasted_iota(jnp.int32, (N_Q_HEADS * seq_chunk, seq_chunk * WINDOW), 0)
    cc = lax.broadcasted_iota(jnp.int32, (N_Q_HEADS * seq_chunk, seq_chunk * WINDOW), 1)
    same_seq = (rr % seq_chunk) == (cc // WINDOW)
    kv_lo = lax.broadcasted_iota(jnp.int32, (seq_chunk, LANES), 1) < HEAD_DIM

    for c in range(n_seq // seq_chunk):
        sr = slice(c * seq_chunk, (c + 1) * seq_chunk)
        kc = ck_ref[sr].reshape(seq_chunk * WINDOW, KV_WIDTH).astype(BF16)
        vc = cv_ref[sr].reshape(seq_chunk * WINDOW, KV_WIDTH).astype(BF16)
        qs = jnp.concatenate([q_heads[hd][sr] for hd in range(N_Q_HEADS)], axis=0)
        s = jnp.where(same_seq, _dot_nt(qs, kc), NEG_INF)
        sn = jnp.concatenate([s_new[hd][sr] for hd in range(N_Q_HEADS)], axis=0)
        sink = jnp.concatenate([jnp.full((seq_chunk, 1), sinks_ref[hd], F32) for hd in range(N_Q_HEADS)], axis=0)
        m = jnp.maximum(jnp.maximum(jnp.max(s, axis=-1, keepdims=True), sn), sink)
        e = jnp.exp(s - m)
        en = jnp.exp(sn - m)
        inv = 1.0 / (jnp.sum(e, axis=-1, keepdims=True) + en + jnp.exp(sink - m))
        o = _dot((e * inv).astype(BF16), vc)
        pn = (en * inv).astype(BF16).astype(F32)
        for mcol in range(N_Q_HEADS // 2):
            halves = []
            for hd in (2 * mcol, 2 * mcol + 1):
                oh = o[hd * seq_chunk:(hd + 1) * seq_chunk] + pn[hd * seq_chunk:(hd + 1) * seq_chunk] * vb[sr]
                if (hd % 2) != (hd // (N_Q_HEADS // N_KV_HEADS)):
                    oh = pltpu.roll(oh, HEAD_DIM, 1)
                halves.append(oh)
            att = jnp.where(kv_lo, halves[0], halves[1])
            mix_ref[sr, mcol * LANES:(mcol + 1) * LANES] = att.astype(BF16)

    u = _gelu(z[:, C_U:C_U + GMLP_WIDTH])
    vn = _layernorm(_gelu(z[:, C_VG:C_VG + GMLP_WIDTH]), lng_ref[...], lnb_ref[...])
    vnout_ref[...] = vn
    sp = ws0_ref[...].astype(BF16).astype(F32) * vn.astype(BF16).astype(F32) + bs0_ref[...]
    mix_ref[:, ATTN_WIDTH:] = (u * sp).astype(BF16)

    xmid = x + _dot(mix_ref[...], wout_ref[...])
    xmid_ref[...] = xmid
    h2b = _rmsnorm(xmid, g2_ref[...]).astype(BF16)
    d1, d2 = _sort_plan(_route(h2b, wrt_ref, brt_ref), upper_ref, lpad_ref, meta_ref, tab_ref)
    _sort_rows(d1, d2, h2b, xl_tile)
    copy = pltpu.make_async_copy(xl_tile, xl_hbm.at[xl_hbm.shape[0] - 1], xl_sem.at[0])
    copy.start()
    copy.wait()


def _grouped_ffn_kernel(texp_ref, gsrc_ref, sdst_ref, ntiles_ref, xl_hbm, wg_ref, wu_ref, wd_ref, yl_hbm,
                        xbuf, ybuf, wg_b, wu_b, wd_b, gsem, ssem):
    j = pl.program_id(0)
    n_tiles = ntiles_ref[0]

    def gather_copy(tile, s, b):
        return pltpu.make_async_copy(xl_hbm.at[gsrc_ref[tile * FFN_SLOTS + s]], xbuf.at[b, s], gsem.at[b])

    def scatter_copy(tile, s):
        b = (tile + SCATTER_BUFS) % SCATTER_BUFS
        return pltpu.make_async_copy(ybuf.at[b, s], yl_hbm.at[sdst_ref[(tile + 1) * FFN_SLOTS + s]], ssem.at[b])

    @pl.when(j < n_tiles)
    def _():
        gb = j % GATHER_BUFS
        ahead = GATHER_BUFS - 1
        nxt = j + ahead
        nxt_b = nxt % GATHER_BUFS

        @pl.when(j == 0)
        def _():
            ybuf[SCATTER_BUFS - 1] = jnp.zeros(ybuf.shape[1:], BF16)
            for k in range(ahead):
                for s in range(FFN_SLOTS):
                    gather_copy(k, s, k).start()

        for s in range(FFN_SLOTS):
            gather_copy(j, s, gb).wait()

        @pl.when(j >= SCATTER_BUFS - 1)
        def _():
            for s in range(FFN_SLOTS):
                scatter_copy(j - SCATTER_BUFS, s).wait()

        @pl.when(jnp.logical_or(j == 0, texp_ref[j] != texp_ref[jnp.maximum(j - 1, 0)]))
        def _():
            wg_b[...] = wg_ref[...].astype(BF16)
            wu_b[...] = wu_ref[...].astype(BF16)
            wd_b[...] = wd_ref[...].astype(BF16)

        x = xbuf[gb].reshape(FFN_ROWS, D_MODEL)
        gate = _dot(x, wg_b[...])
        up = _dot(x, wu_b[...])
        hid = (gate * (1.0 / (1.0 + jnp.exp(-gate))) * up).astype(BF16)
        wd = wd_b[...]
        for s in range(FFN_SLOTS):
            scatter_copy(j - 1, s).start()
        for s in range(FFN_SLOTS):
            gather_copy(nxt, s, nxt_b).start()
        ybuf[j % SCATTER_BUFS] = _dot(hid, wd).astype(BF16).reshape(FFN_SLOTS, ROW_GRANULE, D_MODEL)

        @pl.when(j == n_tiles - 1)
        def _():
            for k in range(1, GATHER_BUFS):
                for s in range(FFN_SLOTS):
                    gather_copy(nxt, s, (j + k) % GATHER_BUFS).wait()
            for s in range(FFN_SLOTS):
                scatter_copy(j, s).start()
            for back in range(SCATTER_BUFS):
                @pl.when(j - back >= -1)
                def _():
                    for s in range(FFN_SLOTS):
                        scatter_copy(j - back, s).wait()


def _combine_kernel(xmid_ref, yl_hbm, meta_ref, gf_ref, y_ref, ybuf, ysem, *, first_tile, n_tiles):
    n_tok = xmid_ref.shape[0]
    i = pl.program_id(0)
    ahead = COMBINE_BUFS - 1

    def fetch(tile):
        row0 = pl.multiple_of((first_tile + tile) * SORT_ROWS, SORT_ROWS)
        slot = tile % COMBINE_BUFS
        return pltpu.make_async_copy(yl_hbm.at[pl.ds(row0, SORT_ROWS), :], ybuf.at[slot], ysem.at[slot])

    @pl.when(i == 0)
    def _():
        for k in range(min(ahead, n_tiles)):
            fetch(k).start()

    @pl.when(i + ahead < n_tiles)
    def _():
        fetch(i + ahead).start()

    fetch(i).wait()
    yl_ref = ybuf.at[i % COMBINE_BUFS]
    meta = meta_ref[...]
    meta_t = jnp.concatenate([meta, jnp.zeros((LANES - 8, n_tok), F32)], axis=0).T
    half = max(n_tok // 2, LANES)
    for r0 in range(0, n_tok, half):
        rows = slice(r0, r0 + half)
        d1, d2, w1, w2 = (meta_t[rows, i:i + 1] for i in range(4))
        acc = xmid_ref[rows, :]
        for c in range(SORT_ROWS // SORT_CHUNK):
            r_iota = (lax.broadcasted_iota(jnp.int32, (half, SORT_CHUNK), 1) + c * SORT_CHUNK).astype(F32)
            unsort = jnp.where(r_iota == d1, w1, jnp.where(r_iota == d2, w2, 0.0)).astype(BF16)
            acc = acc + _dot(unsort, yl_ref[c * SORT_CHUNK:(c + 1) * SORT_CHUNK, :])
        y_ref[rows, :] = _rmsnorm(acc, gf_ref[...])


def _ffn_schedule(tab):
    n_tok_tiles = tab.shape[0]
    strips = (tab[:, :, 0] * (1.0 / ROW_GRANULE)).astype(jnp.int32)
    starts = (tab[:, :, 1] * (1.0 / ROW_GRANULE)).astype(jnp.int32)
    cnt = strips.T
    row0 = starts.T
    cs = jnp.cumsum(cnt, axis=1) - cnt
    n_str = jnp.sum(cnt, axis=1)
    np_str = (n_str + FFN_SLOTS - 1) // FFN_SLOTS * FFN_SLOTS
    ends = jnp.cumsum(np_str)
    base = ends - np_str
    n_steps_max = _ffn_steps_max(n_tok_tiles) + GATHER_BUFS - 1
    step0 = jnp.arange(n_steps_max, dtype=jnp.int32) * FFN_SLOTS
    stream = jnp.minimum(jnp.sum(ends[None, :] <= step0[:, None], axis=1), N_EXPERTS - 1)
    pick = stream[:, None] == jnp.arange(N_EXPERTS, dtype=jnp.int32)[None, :]
    sel = lambda x: jnp.sum(jnp.where(pick[:, :, None], x[None], 0), axis=1)
    cs_j, cnt_j, row0_j = sel(cs), sel(cnt), sel(row0)
    base_j = jnp.sum(jnp.where(pick, base[None, :], 0), axis=1)
    q = step0[:, None] + jnp.arange(FFN_SLOTS, dtype=jnp.int32)[None, :] - base_j[:, None]
    reached = cs_j[:, None, :] <= q[:, :, None]
    last = lambda x: jnp.sum(jnp.where(reached, jnp.diff(x, axis=1, prepend=0)[:, None, :], 0), axis=2)
    tile_idx = jnp.sum(reached, axis=2).astype(jnp.int32) - 1
    g = q - last(cs_j)
    valid = g < last(cnt_j)
    granule = tile_idx * GRANULES_PER_TILE + last(row0_j) + g
    assert n_tok_tiles >= 2 * FFN_SLOTS
    slot = jnp.arange(FFN_SLOTS, dtype=jnp.int32)[None, :]
    parity = jnp.arange(n_steps_max, dtype=jnp.int32)[:, None] % 2
    pad_dst = lambda par: (slot + par * FFN_SLOTS) * GRANULES_PER_TILE + GRANULES_PER_TILE - 1
    gsrc = jnp.where(valid, granule, slot * GRANULES_PER_TILE + GRANULES_PER_TILE - 2).astype(jnp.int32)
    sdst = jnp.where(valid, granule, pad_dst(parity))
    sdst = jnp.concatenate([pad_dst(1), sdst], axis=0)
    texp = stream.astype(jnp.int32)
    n_steps = (ends[-1] // FFN_SLOTS).astype(jnp.int32).reshape(1)
    return texp, gsrc.reshape(-1), sdst.astype(jnp.int32).reshape(-1), n_steps


assert GRANULES_PER_TILE - (2 * TOK_TILE + N_EXPERTS * (ROW_GRANULE - 1)) // ROW_GRANULE >= 2


def _ffn_steps_max(n_tok_tiles):
    return -(-(n_tok_tiles * GRANULES_PER_TILE + N_EXPERTS * (FFN_SLOTS - 1)) // FFN_SLOTS)


def _vmem_limit(mib):
    n_bytes = mib * 1024 * 1024
    assert n_bytes < V7X_VMEM_BYTES
    return n_bytes


def _rope_tables(pos):
    inv_freq = ROPE_THETA ** (-jnp.arange(HALF, dtype=F32) * 2.0 / HEAD_DIM)
    ang = pos.astype(F32)[:, None] * inv_freq[None, :]
    cos, sin = jnp.cos(ang), jnp.sin(ang)
    reps = LANES // HEAD_DIM
    return jnp.tile(jnp.concatenate([cos, cos], axis=1), (1, reps)), jnp.tile(jnp.concatenate([-sin, sin], axis=1), (1, reps))


def _moe_combine(xmid, yl, meta, gf, n_tok, first_tile, n_tiles):
    return pl.pallas_call(
        functools.partial(_combine_kernel, first_tile=first_tile, n_tiles=n_tiles),
        grid=(n_tiles,),
        in_specs=[
            pl.BlockSpec((n_tok, D_MODEL), lambda i: (i, 0)),
            pl.BlockSpec(memory_space=pl.ANY),
            pl.BlockSpec((None, 8, n_tok), lambda i: (i, 0, 0)),
            pl.BlockSpec((1, D_MODEL), lambda i: (0, 0)),
        ],
        out_specs=pl.BlockSpec((n_tok, D_MODEL), lambda i: (i, 0)),
        out_shape=jax.ShapeDtypeStruct((n_tiles * n_tok, D_MODEL), F32),
        scratch_shapes=[pltpu.VMEM((COMBINE_BUFS, SORT_ROWS, D_MODEL), BF16), pltpu.SemaphoreType.DMA((COMBINE_BUFS,))],
        compiler_params=pltpu.CompilerParams(
            dimension_semantics=("arbitrary",),
            vmem_limit_bytes=_vmem_limit(40)),
        name="moe_combine",
    )(xmid, yl, meta, gf)


def kernel(x_prompt, x_sample, cache_swa_k, cache_swa_v, norm_mix_g, w_in, attn_sinks, gmlp_ln_g, gmlp_ln_b,
           gmlp_w_s, gmlp_b_s, w_out, norm_ffn_g, router_group_w, router_group_b, router_expert_w,
           router_expert_b, expert_w_gate, expert_w_up, expert_w_down, final_norm_g):
    assert norm_mix_g.shape[0] == 1, "single-layer trunk"
    batch, seq, _ = x_prompt.shape
    dec_batch = x_sample.shape[0]
    assert x_sample.shape[1] == 1 and seq % TOK_TILE == 0

    win_ext = w_in[0].astype(BF16)
    wout = w_out[0].astype(BF16)
    g1 = norm_mix_g[0][None, :]
    g2 = norm_ffn_g[0][None, :]
    gf = final_norm_g[None, :]
    lng = gmlp_ln_g[0][None, :]
    lnb = gmlp_ln_b[0][None, :]
    sinks = attn_sinks[0]
    ws = gmlp_w_s[0]
    group_dim = GMLP_WIDTH // GMLP_GROUPS
    bsf = jnp.repeat(gmlp_b_s[0].T, group_dim, axis=1)
    ws0 = jnp.repeat(ws[:, 0, 0], group_dim)[None, :]
    bs0 = jnp.repeat(gmlp_b_s[0][:, 0], group_dim)[None, :]
    wrt = jnp.zeros((ROUTER_ROWS, D_MODEL), F32)
    wrt = wrt.at[0:N_GROUPS].set(router_group_w[0].T)
    wrt = wrt.at[8:8 + N_EXPERTS].set(router_expert_w[0].reshape(D_MODEL, N_EXPERTS).T).astype(BF16)
    brt = jnp.full((ROUTER_ROWS, 1), NEG_INF, F32)
    brt = brt.at[0:N_GROUPS, 0].set(router_group_b[0])
    brt = brt.at[8:8 + N_EXPERTS, 0].set(router_expert_b[0].reshape(N_EXPERTS))
    wg, wu, wd = expert_w_gate[0], expert_w_up[0], expert_w_down[0]
    cos_p, sin_p = _rope_tables(jnp.arange(seq, dtype=jnp.int32))
    cos_s, sin_s = _rope_tables(PAST_LEN + jnp.arange(1, dtype=jnp.int32))

    full = lambda shape: pl.BlockSpec(shape, lambda *_: (0,) * len(shape))
    smem = pl.BlockSpec(memory_space=pltpu.SMEM)
    n_tiles = seq // TOK_TILE

    upper = jnp.triu(jnp.ones((TOK_TILE, TOK_TILE), BF16), k=1)
    lpad = (jnp.arange(LANES)[None, :] < jnp.arange(N_EXPERTS)[:, None]).astype(BF16)
    n_tok_tiles = batch * n_tiles

    x2d = x_prompt.reshape(batch * seq, D_MODEL)
    tile_a = lambda s: jnp.minimum(s, n_tok_tiles - 1)
    cur = lambda s: jnp.maximum(s - 1, 0)
    tile_b = cur
    xmid_p, xl_p, meta_p, tab_p, k_p, v_p = pl.pallas_call(
        functools.partial(_prompt_mixer_kernel, tiles_per_seq=n_tiles),
        grid=(n_tok_tiles + 1,),
        in_specs=[
            smem,
            pl.BlockSpec((TOK_TILE, D_MODEL), lambda s: (tile_a(s), 0)),
            pl.BlockSpec((TOK_TILE, D_MODEL), lambda s: (cur(s), 0)),
            full((1, D_MODEL)),
            full((D_MODEL, IN_WIDTH)),
            pl.BlockSpec((TOK_TILE, LANES), lambda s: (tile_b(s) % n_tiles, 0)),
            pl.BlockSpec((TOK_TILE, LANES), lambda s: (tile_b(s) % n_tiles, 0)),
            full((1, GMLP_WIDTH)),
            full((1, GMLP_WIDTH)),
            full((GMLP_GROUPS, CHUNK, CHUNK)),
            full((CHUNK, GMLP_WIDTH)),
            full((D_MODEL, D_MODEL)),
            full((1, D_MODEL)),
            full((ROUTER_ROWS, D_MODEL)),
            full((ROUTER_ROWS, 1)),
            full((TOK_TILE, TOK_TILE)),
            full((N_EXPERTS, LANES)),
        ],
        out_specs=[
            pl.BlockSpec((TOK_TILE, D_MODEL), lambda s: (cur(s), 0)),
            pl.BlockSpec((None, SORT_ROWS, D_MODEL), lambda s: (jnp.where(s == 0, n_tok_tiles, s - 1), 0, 0)),
            pl.BlockSpec((None, 8, TOK_TILE), lambda s: (cur(s), 0, 0)),
            pl.BlockSpec((None, N_EXPERTS, LANES), lambda s: (cur(s), 0, 0)),
            pl.BlockSpec((None, WINDOW, KV_WIDTH), lambda s: (tile_b(s) // n_tiles, 0, 0)),
            pl.BlockSpec((None, WINDOW, KV_WIDTH), lambda s: (tile_b(s) // n_tiles, 0, 0)),
        ],
        out_shape=[
            jax.ShapeDtypeStruct((batch * seq, D_MODEL), F32),
            jax.ShapeDtypeStruct((n_tok_tiles + 1, SORT_ROWS, D_MODEL), BF16),
            jax.ShapeDtypeStruct((n_tok_tiles, 8, TOK_TILE), F32),
            jax.ShapeDtypeStruct((n_tok_tiles, N_EXPERTS, LANES), F32),
            jax.ShapeDtypeStruct((batch, WINDOW, KV_WIDTH), F32),
            jax.ShapeDtypeStruct((batch, WINDOW, KV_WIDTH), F32),
        ],
        scratch_shapes=[
            pltpu.VMEM((TOK_TILE, IN_WIDTH), F32),
            pltpu.VMEM((TOK_TILE, IN_WIDTH), F32),
            pltpu.VMEM((TOK_TILE, D_MODEL), BF16),
        ] + [pltpu.VMEM((WINDOW + TOK_TILE, KV_WIDTH), BF16)] * 4 + [
            pltpu.VMEM((len(_NAT_HEADS), TOK_TILE, LANES), BF16),
            pltpu.VMEM((len(_SWP_HEADS), TOK_TILE, LANES), BF16),
            pltpu.VMEM((2 * BLOCKS_PER_TILE, len(_NAT_HEADS) * WINDOW, 2 * WINDOW), F32),
            pltpu.VMEM((2 * BLOCKS_PER_TILE, len(_NAT_HEADS) * WINDOW, 2 * WINDOW), BF16),
            pltpu.VMEM((TOK_TILE, GMLP_WIDTH), BF16),
            pltpu.VMEM((TOK_TILE, D_MODEL), BF16),
            pltpu.VMEM((TOK_TILE, D_MODEL), BF16),
        ],
        compiler_params=pltpu.CompilerParams(
            dimension_semantics=("arbitrary",),
            vmem_limit_bytes=_vmem_limit(56)),
        name="prompt_mixer",
    )(sinks, x2d, x2d, g1, win_ext, cos_p, sin_p, lng, lnb, ws, bsf, wout, g2, wrt, brt, upper, lpad)

    xs = x_sample.reshape(dec_batch, D_MODEL)
    ck = cache_swa_k[0].reshape(dec_batch, WINDOW, KV_WIDTH)
    cv = cache_swa_v[0].reshape(dec_batch, WINDOW, KV_WIDTH)
    vmem = pl.BlockSpec(memory_space=pltpu.VMEM)
    hbm = pl.BlockSpec(memory_space=pl.ANY)
    xmid_s, meta_s, tab_s, k_s, v_s, vn_s, xl_all = pl.pallas_call(
        _sample_mixer_kernel,
        in_specs=[smem] + [vmem] * 17 + [hbm],
        out_specs=[vmem] * 6 + [hbm],
        out_shape=[
            jax.ShapeDtypeStruct((dec_batch, D_MODEL), F32),
            jax.ShapeDtypeStruct((8, dec_batch), F32),
            jax.ShapeDtypeStruct((N_EXPERTS, LANES), F32),
            jax.ShapeDtypeStruct((dec_batch, KV_WIDTH), F32),
            jax.ShapeDtypeStruct((dec_batch, KV_WIDTH), F32),
            jax.ShapeDtypeStruct((dec_batch, GMLP_WIDTH), F32),
            jax.ShapeDtypeStruct(xl_p.shape, BF16),
        ],
        scratch_shapes=[pltpu.VMEM((dec_batch, D_MODEL), BF16), pltpu.VMEM((SORT_ROWS, D_MODEL), BF16),
                        pltpu.SemaphoreType.DMA((1,))],
        compiler_params=pltpu.CompilerParams(vmem_limit_bytes=_vmem_limit(56)),
        input_output_aliases={18: 6},
        name="sample_mixer",
    )(sinks, xs, ck, cv, g1, win_ext, cos_s, sin_s, lng, lnb, ws0, bs0, wout, g2, wrt, brt,
      upper[:dec_batch, :dec_batch], lpad, xl_p)

    n_all_tiles = n_tok_tiles + 1
    tab_all = jnp.concatenate([tab_p, tab_s[None]], axis=0)
    texp, gsrc, sdst, n_steps = _ffn_schedule(tab_all)
    yl_all = pl.pallas_call(
        _grouped_ffn_kernel,
        grid_spec=pltpu.PrefetchScalarGridSpec(
            num_scalar_prefetch=4,
            grid=(_ffn_steps_max(n_all_tiles),),
            in_specs=[
                pl.BlockSpec(memory_space=pl.ANY),
                pl.BlockSpec((None, D_MODEL, D_EXPERT), lambda j, te, gs, sd, ns: (te[j], 0, 0)),
                pl.BlockSpec((None, D_MODEL, D_EXPERT), lambda j, te, gs, sd, ns: (te[j], 0, 0)),
                pl.BlockSpec((None, D_EXPERT, D_MODEL), lambda j, te, gs, sd, ns: (te[j], 0, 0)),
            ],
            out_specs=pl.BlockSpec(memory_space=pl.ANY),
            scratch_shapes=[
                pltpu.VMEM((GATHER_BUFS, FFN_SLOTS, ROW_GRANULE, D_MODEL), BF16),
                pltpu.VMEM((SCATTER_BUFS, FFN_SLOTS, ROW_GRANULE, D_MODEL), BF16),
                pltpu.VMEM((D_MODEL, D_EXPERT), BF16),
                pltpu.VMEM((D_MODEL, D_EXPERT), BF16),
                pltpu.VMEM((D_EXPERT, D_MODEL), BF16),
                pltpu.SemaphoreType.DMA((GATHER_BUFS,)),
                pltpu.SemaphoreType.DMA((SCATTER_BUFS,)),
            ],
        ),
        out_shape=jax.ShapeDtypeStruct((n_all_tiles * GRANULES_PER_TILE, ROW_GRANULE, D_MODEL), BF16),
        compiler_params=pltpu.CompilerParams(
            dimension_semantics=("arbitrary",),
            vmem_limit_bytes=_vmem_limit(32)),
        input_output_aliases={4: 0},
        name="grouped_ffn",
    )(texp, gsrc, sdst, n_steps, xl_all.reshape(n_all_tiles * GRANULES_PER_TILE, ROW_GRANULE, D_MODEL), wg, wu, wd)
    yl_all = yl_all.reshape(n_all_tiles * SORT_ROWS, D_MODEL)

    y_p = _moe_combine(xmid_p, yl_all, meta_p, gf, TOK_TILE, 0, n_tok_tiles)
    y_s = _moe_combine(xmid_s, yl_all, meta_s[None], gf, dec_batch, n_tok_tiles, 1)

    return (y_p.reshape(batch, seq, D_MODEL),
            y_s.reshape(dec_batch, 1, D_MODEL),
            k_p.reshape(1, batch, WINDOW, N_KV_HEADS, HEAD_DIM),
            v_p.reshape(1, batch, WINDOW, N_KV_HEADS, HEAD_DIM),
            k_s.reshape(1, dec_batch, 1, N_KV_HEADS, HEAD_DIM),
            v_s.reshape(1, dec_batch, 1, N_KV_HEADS, HEAD_DIM),
            vn_s.reshape(1, dec_batch, 1, GMLP_WIDTH))
```

```python
import functools

import jax
import jax.numpy as jnp
import numpy as np
from jax import lax
from jax.experimental import pallas as pl
from jax.experimental.pallas import tpu as pltpu

F32 = jnp.float32
BF16 = jnp.bfloat16

D_MODEL = 1024
HEAD_DIM = 64
HALF = HEAD_DIM // 2
N_Q_HEADS = 8
N_KV_HEADS = 2
ATTN_WIDTH = N_Q_HEADS * HEAD_DIM
KV_WIDTH = N_KV_HEADS * HEAD_DIM
WINDOW = 128
ROPE_THETA = 10000.0
GMLP_WIDTH = D_MODEL - ATTN_WIDTH
GMLP_GROUPS = 8
CHUNK = 128
N_GROUPS = 4
EXPERTS_PER_GROUP = 8
N_EXPERTS = N_GROUPS * EXPERTS_PER_GROUP
D_EXPERT = 256
EPS = 1e-6
NEG_INF = -1e30
PAST_LEN = 16384

LANES = 128
V7X_VMEM_BYTES = 64 * 1024 * 1024

C_Q = 0
C_K = C_Q + ATTN_WIDTH
C_V = C_K + KV_WIDTH
C_U = C_V + KV_WIDTH
C_VG = C_U + GMLP_WIDTH
IN_WIDTH = C_VG + GMLP_WIDTH

ROUTER_ROWS = 48
TOK_TILE = 512
BLOCKS_PER_TILE = TOK_TILE // WINDOW
ROW_GRANULE = 16
SORT_CHUNK = 512
SORT_ROWS = -(-(2 * TOK_TILE + N_EXPERTS * (ROW_GRANULE - 1)) // SORT_CHUNK) * SORT_CHUNK
GRANULES_PER_TILE = SORT_ROWS // ROW_GRANULE
FFN_ROWS = 512
FFN_SLOTS = FFN_ROWS // ROW_GRANULE
GATHER_BUFS = 3
SCATTER_BUFS = 3
COMBINE_BUFS = 3


def _dot(a, b):
    return jnp.dot(a, b, preferred_element_type=F32)


def _dot_nt(a, b):
    return lax.dot_general(a, b, (((1,), (1,)), ((), ())), preferred_element_type=F32)


def _gelu(x):
    return 0.5 * x * (1.0 + lax.erf(x * np.float32(np.sqrt(0.5))))


def _rmsnorm(x, g):
    return x * lax.rsqrt(jnp.mean(x * x, axis=-1, keepdims=True) + EPS) * g


def _layernorm(x, g, b):
    mu = jnp.mean(x, axis=-1, keepdims=True)
    xc = x - mu
    return xc * lax.rsqrt(jnp.mean(xc * xc, axis=-1, keepdims=True) + EPS) * g + b


def _first_argmax_rows(x, row_iota, n_rows):
    m = jnp.max(x, axis=0, keepdims=True)
    idx = jnp.min(jnp.where(x == m, row_iota, n_rows), axis=0, keepdims=True)
    return m, idx


def _route(h2b, wrt_ref, brt_ref):
    n_tok = h2b.shape[0]
    lt = _dot_nt(wrt_ref[...], h2b) + brt_ref[...]
    row8 = lax.broadcasted_iota(jnp.int32, (8, n_tok), 0).astype(F32)
    glog = lt[0:8]
    gmax, gidx = _first_argmax_rows(glog, row8, 8)
    g_w = 1.0 / jnp.sum(jnp.exp(glog - gmax), axis=0, keepdims=True)
    esel = lt[8:16]
    for g in range(1, N_GROUPS):
        esel = jnp.where(gidx == g, lt[8 + 8 * g:16 + 8 * g], esel)
    m1, i1 = _first_argmax_rows(esel, row8, 8)
    esel2 = jnp.where(row8 == i1, -jnp.inf, esel)
    m2, i2 = _first_argmax_rows(esel2, row8, 8)
    r = jnp.exp(m2 - m1)
    w1 = 1.0 / (1.0 + r)
    w2 = r / (1.0 + r)
    e1 = gidx * EXPERTS_PER_GROUP + i1
    e2 = gidx * EXPERTS_PER_GROUP + i2
    slab = jnp.where(row8 == 0, e1, 0.0)
    for r_idx, val in ((1, e2), (2, w1 * g_w), (3, w2 * g_w)):
        slab = jnp.where(row8 == r_idx, val, slab)
    return slab


def _softmax_with_sink(s, sink):
    m = jnp.maximum(jnp.max(s, axis=-1, keepdims=True), sink)
    e = jnp.exp(s - m)
    den = jnp.sum(e, axis=-1, keepdims=True) + jnp.exp(sink - m)
    return e * (1.0 / den)


def _rope(x, cos, sin_signed):
    first_half = (lax.broadcasted_iota(jnp.int32, x.shape, 1) & HALF) == 0
    partner = jnp.where(first_half, pltpu.roll(x, LANES - HALF, 1), pltpu.roll(x, HALF, 1))
    return x * cos + partner * sin_signed


_NAT_HEADS = (0, 2, 5, 7)
_SWP_HEADS = (1, 3, 4, 6)


def _prompt_mixer_kernel(sinks_ref, x_next_ref, x_ref, g1_ref, win_ref, cos_ref, sin_ref, lng_ref, lnb_ref, ws_ref,
                         bsf_ref, wout_ref, g2_ref, wrt_ref, brt_ref, upper_ref, lpad_ref,
                         xmid_ref, xl_ref, meta_ref, tab_ref, kout_ref, vout_ref,
                         z_a, z_b, mix_ref, k_n, k_s, v_n, v_s, q_nat, q_swp, s_ref, p_ref, vn_ref, h_ref, h2_ref,
                         *, tiles_per_seq):
    step = pl.program_id(0)
    t = jnp.maximum(step - 1, 0) % tiles_per_seq
    kv_bufs = (k_n, k_s, v_n, v_s)

    @pl.when(step == 0)
    def _():
        z_b[...] = jnp.zeros_like(z_b)

    @pl.when(t == 0)
    def _():
        for ref in kv_bufs:
            ref[0:WINDOW, :] = jnp.zeros((WINDOW, KV_WIDTH), BF16)

    args = (t, sinks_ref, x_next_ref, x_ref, g1_ref, win_ref, cos_ref, sin_ref, lng_ref, lnb_ref, ws_ref, bsf_ref,
            wout_ref, g2_ref, wrt_ref, brt_ref, upper_ref, lpad_ref, xmid_ref, xl_ref, meta_ref, tab_ref, kout_ref,
            vout_ref, kv_bufs, q_nat, q_swp, s_ref, p_ref, vn_ref, h_ref, h2_ref)

    @pl.when(step % 2 == 0)
    def _():
        _prompt_mixer_body(z_a, z_b, mix_ref, *args)

    @pl.when(step % 2 == 1)
    def _():
        _prompt_mixer_body(z_b, z_a, mix_ref, *args)


def _prompt_mixer_body(z_next, z_ref, mix_ref, t, sinks_ref, x_next_ref, x_ref, g1_ref, win_ref, cos_ref,
                       sin_ref, lng_ref, lnb_ref, ws_ref, bsf_ref, wout_ref, g2_ref, wrt_ref, brt_ref, upper_ref,
                       lpad_ref, xmid_ref, xl_ref, meta_ref, tab_ref, kout_ref, vout_ref, kv_bufs,
                       q_nat, q_swp, s_ref, p_ref, vn_ref, h_ref, h2_ref):
    k_n, k_s, v_n, v_s = kv_bufs
    cos = cos_ref[...]
    sin = sin_ref[...]
    lane = lax.broadcasted_iota(jnp.int32, (WINDOW, LANES), 1)
    lo = lane < HEAD_DIM
    row = lax.broadcasted_iota(jnp.int32, (WINDOW, WINDOW), 0)
    col = lax.broadcasted_iota(jnp.int32, (WINDOW, WINDOW), 1)
    mask_cur = col <= row
    mask_prev_band = col >= row
    mask_prev_first = jnp.logical_and(mask_prev_band, (jnp.zeros_like(row) + t) > 0)
    mask_band = jnp.concatenate([mask_prev_band, mask_cur], axis=1)
    mask_first = jnp.concatenate([mask_prev_first, mask_cur], axis=1)

    cq = cos * np.float32(HEAD_DIM ** -0.5)
    sq = sin * np.float32(HEAD_DIM ** -0.5)
    kf = _rope(z_ref[:, C_K:C_K + KV_WIDTH], cos, sin)
    vf = z_ref[:, C_V:C_V + KV_WIDTH]
    k_n[WINDOW:, :] = kf.astype(BF16)
    k_s[WINDOW:, :] = pltpu.roll(kf, HEAD_DIM, 1).astype(BF16)
    v_n[WINDOW:, :] = vf.astype(BF16)
    v_s[WINDOW:, :] = pltpu.roll(vf, HEAD_DIM, 1).astype(BF16)

    kout_ref[...] = kf[TOK_TILE - WINDOW:]
    vout_ref[...] = vf[TOK_TILE - WINDOW:]

    lo_t = lax.broadcasted_iota(jnp.int32, (TOK_TILE, LANES), 1) < HEAD_DIM
    for m in range(N_Q_HEADS // 2):
        qc = _rope(z_ref[:, C_Q + m * LANES:C_Q + (m + 1) * LANES], cq, sq)
        for hd, qh in ((2 * m, jnp.where(lo_t, qc, 0.0)), (2 * m + 1, jnp.where(lo_t, 0.0, qc))):
            if hd in _NAT_HEADS:
                q_nat[_NAT_HEADS.index(hd)] = qh.astype(BF16)
            else:
                q_swp[_SWP_HEADS.index(hd)] = qh.astype(BF16)

    stacks = ((q_nat, _NAT_HEADS, k_n, v_n), (q_swp, _SWP_HEADS, k_s, v_s))
    proj_cols = 2 * LANES

    def project_next(c):
        cols = slice(c * proj_cols, (c + 1) * proj_cols)
        z_next[:, cols] = _dot(h_ref[...], win_ref[:, cols])

    def scores(j):
        rows = slice(j * WINDOW, (j + 1) * WINDOW)
        keys = slice(j * WINDOW, (j + 2) * WINDOW)
        for si, (q_ref, _, k_buf, _) in enumerate(stacks):
            q_stack = jnp.concatenate([q_ref[i, rows, :] for i in range(len(_NAT_HEADS))], axis=0)
            s_ref[2 * j + si] = _dot_nt(q_stack, k_buf[keys, :])

    def softmax(j):
        mask = mask_first if j == 0 else mask_band
        for si, (_, heads, _, _) in enumerate(stacks):
            for i, hd in enumerate(heads):
                pr = slice(i * WINDOW, (i + 1) * WINDOW)
                p_ref[2 * j + si, pr, :] = _softmax_with_sink(
                    jnp.where(mask, s_ref[2 * j + si, pr, :], NEG_INF), sinks_ref[hd]).astype(BF16)

    def attend(j):
        rows = slice(j * WINDOW, (j + 1) * WINDOW)
        keys = slice(j * WINDOW, (j + 2) * WINDOW)
        o_nat = _dot(p_ref[2 * j], v_n[keys, :])
        o_swp = _dot(p_ref[2 * j + 1], v_s[keys, :])
        for m in range(N_Q_HEADS // 2):
            pr = slice(m * WINDOW, (m + 1) * WINDOW)
            even_nat = (2 * m) in _NAT_HEADS
            att = jnp.where(lo, o_nat[pr], o_swp[pr]) if even_nat else jnp.where(lo, o_swp[pr], o_nat[pr])
            mix_ref[rows, m * LANES:(m + 1) * LANES] = att.astype(BF16)

    def gmlp(m):
        cs = slice(m * LANES, (m + 1) * LANES)
        w0 = jnp.where(mask_cur, ws_ref[2 * m], 0.0).astype(BF16)
        w1 = jnp.where(mask_cur, ws_ref[2 * m + 1], 0.0).astype(BF16)
        wcat = jnp.concatenate([w0, w1], axis=1)
        for j in range(BLOCKS_PER_TILE):
            rows = slice(j * WINDOW, (j + 1) * WINDOW)
            vcol = vn_ref[rows, cs]
            rhs = jnp.concatenate([jnp.where(lo, vcol, jnp.zeros_like(vcol)),
                                   jnp.where(lo, jnp.zeros_like(vcol), vcol)], axis=0)
            sp = _dot(wcat, rhs) + bsf_ref[:, cs]
            u = _gelu(z_ref[rows, C_U + m * LANES:C_U + (m + 1) * LANES])
            mix_ref[rows, ATTN_WIDTH + m * LANES:ATTN_WIDTH + (m + 1) * LANES] = (u * sp).astype(BF16)

    def project_out(c):
        cols = slice(c * proj_cols, (c + 1) * proj_cols)
        xmid_ref[:, cols] = x_ref[:, cols] + _dot(mix_ref[...], wout_ref[:, cols])

    h_ref[...] = _rmsnorm(x_next_ref[...], g1_ref[...]).astype(BF16)
    project_next(0)
    project_next(1)
    for j in range(BLOCKS_PER_TILE):
        scores(j)
    vn_ref[...] = _layernorm(_gelu(z_ref[:, C_VG:C_VG + GMLP_WIDTH]), lng_ref[...], lnb_ref[...]).astype(BF16)
    project_next(2)
    softmax(0)
    project_next(3)
    softmax(1)
    project_next(4)
    softmax(2)
    project_next(5)
    softmax(3)
    project_next(6)
    for j in range(BLOCKS_PER_TILE):
        attend(j)
    for m in range(GMLP_GROUPS // 2):
        gmlp(m)
    for c in range(D_MODEL // proj_cols):
        project_out(c)
    h2_ref[...] = _rmsnorm(xmid_ref[...], g2_ref[...]).astype(BF16)
    d1, d2 = _sort_plan(_route(h2_ref[...], wrt_ref, brt_ref), upper_ref, lpad_ref, meta_ref, tab_ref)
    _sort_rows(d1, d2, h2_ref[...], xl_ref)

    for ref in kv_bufs:
        ref[0:WINDOW, :] = ref[TOK_TILE:TOK_TILE + WINDOW, :]


def _sort_plan(slab, upper_ref, lpad_ref, meta_ref, tab_ref):
    n_tok = slab.shape[1]
    e1, e2 = slab[0:1], slab[1:2]
    row32 = lax.broadcasted_iota(jnp.int32, (N_EXPERTS, n_tok), 0).astype(F32)
    sel1 = row32 == e1
    sel2 = row32 == e2
    onehot = jnp.where(sel1, 1.0, jnp.where(sel2, 1.0, 0.0))
    earlier = _dot(onehot.astype(BF16), upper_ref[...])
    cnt = jnp.sum(onehot, axis=1, keepdims=True)
    pc = jnp.floor((cnt + (ROW_GRANULE - 1)) * (1.0 / ROW_GRANULE)) * ROW_GRANULE
    pc_b = jnp.broadcast_to(pc, (N_EXPERTS, LANES))
    pc_pad = jnp.concatenate([pc_b, jnp.zeros((LANES - N_EXPERTS, LANES), F32)], axis=0).astype(BF16)
    start = _dot(lpad_ref[...], pc_pad)
    base = start[:, 0:1] + earlier
    d1 = jnp.sum(jnp.where(sel1, base, 0.0), axis=0, keepdims=True)
    d2 = jnp.sum(jnp.where(sel2, base, 0.0), axis=0, keepdims=True)
    row8 = lax.broadcasted_iota(jnp.int32, (8, n_tok), 0)
    meta_ref[...] = jnp.where(row8 == 0, d1, jnp.where(row8 == 1, d2, jnp.where(row8 >= 4, 0.0, slab)))
    lane = lax.broadcasted_iota(jnp.int32, (N_EXPERTS, LANES), 1)
    tab_ref[...] = jnp.where(lane == 0, pc_b, jnp.where(lane == 1, start, 0.0))
    return d1, d2


def _sort_rows(d1, d2, h2b, xl_ref):
    n_tok = h2b.shape[0]
    for c in range(SORT_ROWS // SORT_CHUNK):
        r_iota = (lax.broadcasted_iota(jnp.int32, (SORT_CHUNK, n_tok), 0) + c * SORT_CHUNK).astype(F32)
        perm = jnp.where(r_iota == d1, 1.0, jnp.where(r_iota == d2, 1.0, 0.0)).astype(BF16)
        xl_ref[c * SORT_CHUNK:(c + 1) * SORT_CHUNK, :] = _dot(perm, h2b).astype(BF16)


def _sample_mixer_kernel(sinks_ref, x_ref, ck_ref, cv_ref, g1_ref, win_ref, cos_ref, sin_ref, lng_ref, lnb_ref,
                         ws0_ref, bs0_ref, wout_ref, g2_ref, wrt_ref, brt_ref, upper_ref, lpad_ref, xl_in_hbm,
                         xmid_ref, meta_ref, tab_ref, kout_ref, vout_ref, vnout_ref, xl_hbm,
                         mix_ref, xl_tile, xl_sem):
    del xl_in_hbm
    n_seq = x_ref.shape[0]
    seq_chunk = 16
    x = x_ref[...]
    h = _rmsnorm(x, g1_ref[...]).astype(BF16)
    z = _dot(h, win_ref[...])
    cos = cos_ref[...]
    sin = sin_ref[...]
    scale = np.float32(HEAD_DIM ** -0.5)
    lane = lax.broadcasted_iota(jnp.int32, (n_seq, LANES), 1)
    lo = lane < HEAD_DIM
    kf = _rope(z[:, C_K:C_K + KV_WIDTH], cos, sin)
    vf = z[:, C_V:C_V + KV_WIDTH]
    kout_ref[...] = kf
    vout_ref[...] = vf
    kb = kf.astype(BF16).astype(F32)
    vb = vf.astype(BF16).astype(F32)

    q_heads = []
    for hd in range(N_Q_HEADS):
        m = hd // 2
        qc = _rope(z[:, C_Q + m * LANES:C_Q + (m + 1) * LANES], cos, sin) * scale
        keep = lo if hd % 2 == 0 else ~lo
        qm = jnp.where(keep, qc, 0.0)
        if (hd % 2) != (hd // (N_Q_HEADS // N_KV_HEADS)):
            qm = pltpu.roll(qm, HEAD_DIM, 1)
        q_heads.append(qm.astype(BF16))

    s_new = [jnp.sum(q_heads[hd].astype(F32) * kb, axis=-1, keepdims=True) for hd in range(N_Q_HEADS)]

    rr = lax.broadcasted_iota(jnp.int32, (N_Q_HEADS * seq_chunk, seq_chunk * WINDOW), 0)
    cc = lax.broadcasted_iota(jnp.int32, (N_Q_HEADS * seq_chunk, seq_chunk * WINDOW), 1)
    same_seq = (rr % seq_chunk) == (cc // WINDOW)
    kv_lo = lax.broadcasted_iota(jnp.int32, (seq_chunk, LANES), 1) < HEAD_DIM

    for c in range(n_seq // seq_chunk):
        sr = slice(c * seq_chunk, (c + 1) * seq_chunk)
        kc = ck_ref[sr].reshape(seq_chunk * WINDOW, KV_WIDTH).astype(BF16)
        vc = cv_ref[sr].reshape(seq_chunk * WINDOW, KV_WIDTH).astype(BF16)
        qs = jnp.concatenate([q_heads[hd][sr] for hd in range(N_Q_HEADS)], axis=0)
        s = jnp.where(same_seq, _dot_nt(qs, kc), NEG_INF)
        sn = jnp.concatenate([s_new[hd][sr] for hd in range(N_Q_HEADS)], axis=0)
        sink = jnp.concatenate([jnp.full((seq_chunk, 1), sinks_ref[hd], F32) for hd in range(N_Q_HEADS)], axis=0)
        m = jnp.maximum(jnp.maximum(jnp.max(s, axis=-1, keepdims=True), sn), sink)
        e = jnp.exp(s - m)
        en = jnp.exp(sn - m)
        inv = 1.0 / (jnp.sum(e, axis=-1, keepdims=True) + en + jnp.exp(sink - m))
        o = _dot((e * inv).astype(BF16), vc)
        pn = (en * inv).astype(BF16).astype(F32)
        for mcol in range(N_Q_HEADS // 2):
            halves = []
            for hd in (2 * mcol, 2 * mcol + 1):
                oh = o[hd * seq_chunk:(hd + 1) * seq_chunk] + pn[hd * seq_chunk:(hd + 1) * seq_chunk] * vb[sr]
                if (hd % 2) != (hd // (N_Q_HEADS // N_KV_HEADS)):
                    oh = pltpu.roll(oh, HEAD_DIM, 1)
                halves.append(oh)
            att = jnp.where(kv_lo, halves[0], halves[1])
            mix_ref[sr, mcol * LANES:(mcol + 1) * LANES] = att.astype(BF16)

    u = _gelu(z[:, C_U:C_U + GMLP_WIDTH])
    vn = _layernorm(_gelu(z[:, C_VG:C_VG + GMLP_WIDTH]), lng_ref[...], lnb_ref[...])
    vnout_ref[...] = vn
    sp = ws0_ref[...].astype(BF16).astype(F32) * vn.astype(BF16).astype(F32) + bs0_ref[...]
    mix_ref[:, ATTN_WIDTH:] = (u * sp).astype(BF16)

    xmid = x + _dot(mix_ref[...], wout_ref[...])
    xmid_ref[...] = xmid
    h2b = _rmsnorm(xmid, g2_ref[...]).astype(BF16)
    d1, d2 = _sort_plan(_route(h2b, wrt_ref, brt_ref), upper_ref, lpad_ref, meta_ref, tab_ref)
    _sort_rows(d1, d2, h2b, xl_tile)
    copy = pltpu.make_async_copy(xl_tile, xl_hbm.at[xl_hbm.shape[0] - 1], xl_sem.at[0])
    copy.start()
    copy.wait()


def _grouped_ffn_kernel(texp_ref, gsrc_ref, sdst_ref, ntiles_ref, xl_hbm, wg_ref, wu_ref, wd_ref, yl_hbm,
                        xbuf, ybuf, wg_b, wu_b, wd_b, gsem, ssem):
    j = pl.program_id(0)
    n_tiles = ntiles_ref[0]

    def gather_copy(tile, s, b):
        return pltpu.make_async_copy(xl_hbm.at[gsrc_ref[tile * FFN_SLOTS + s]], xbuf.at[b, s], gsem.at[b])

    def scatter_copy(tile, s):
        b = (tile + SCATTER_BUFS) % SCATTER_BUFS
        return pltpu.make_async_copy(ybuf.at[b, s], yl_hbm.at[sdst_ref[(tile + 1) * FFN_SLOTS + s]], ssem.at[b])

    @pl.when(j < n_tiles)
    def _():
        gb = j % GATHER_BUFS
        ahead = GATHER_BUFS - 1
        nxt = j + ahead
        nxt_b = nxt % GATHER_BUFS

        @pl.when(j == 0)
        def _():
            ybuf[SCATTER_BUFS - 1] = jnp.zeros(ybuf.shape[1:], BF16)
            for k in range(ahead):
                for s in range(FFN_SLOTS):
                    gather_copy(k, s, k).start()

        for s in range(FFN_SLOTS):
            gather_copy(j, s, gb).wait()

        @pl.when(j >= SCATTER_BUFS - 1)
        def _():
            for s in range(FFN_SLOTS):
                scatter_copy(j - SCATTER_BUFS, s).wait()

        @pl.when(jnp.logical_or(j == 0, texp_ref[j] != texp_ref[jnp.maximum(j - 1, 0)]))
        def _():
            wg_b[...] = wg_ref[...].astype(BF16)
            wu_b[...] = wu_ref[...].astype(BF16)
            wd_b[...] = wd_ref[...].astype(BF16)

        x = xbuf[gb].reshape(FFN_ROWS, D_MODEL)
        gate = _dot(x, wg_b[...])
        up = _dot(x, wu_b[...])
        hid = (gate * (1.0 / (1.0 + jnp.exp(-gate))) * up).astype(BF16)
        wd = wd_b[...]
        for s in range(FFN_SLOTS):
            scatter_copy(j - 1, s).start(priority=s % 2)
        for s in range(FFN_SLOTS):
            gather_copy(nxt, s, nxt_b).start(priority=s % 2)
        ybuf[j % SCATTER_BUFS] = _dot(hid, wd).astype(BF16).reshape(FFN_SLOTS, ROW_GRANULE, D_MODEL)

        @pl.when(j == n_tiles - 1)
        def _():
            for k in range(1, GATHER_BUFS):
                for s in range(FFN_SLOTS):
                    gather_copy(nxt, s, (j + k) % GATHER_BUFS).wait()
            for s in range(FFN_SLOTS):
                scatter_copy(j, s).start()
            for back in range(SCATTER_BUFS):
                @pl.when(j - back >= -1)
                def _():
                    for s in range(FFN_SLOTS):
                        scatter_copy(j - back, s).wait()


def _combine_kernel(xmid_ref, yl_hbm, meta_ref, gf_ref, y_ref, ybuf, ysem, *, first_tile, n_tiles):
    n_tok = xmid_ref.shape[0]
    i = pl.program_id(0)
    ahead = COMBINE_BUFS - 1

    def fetch(tile):
        row0 = pl.multiple_of((first_tile + tile) * SORT_ROWS, SORT_ROWS)
        slot = tile % COMBINE_BUFS
        return pltpu.make_async_copy(yl_hbm.at[pl.ds(row0, SORT_ROWS), :], ybuf.at[slot], ysem.at[slot])

    @pl.when(i == 0)
    def _():
        for k in range(min(ahead, n_tiles)):
            fetch(k).start()

    @pl.when(i + ahead < n_tiles)
    def _():
        fetch(i + ahead).start()

    fetch(i).wait()
    yl_ref = ybuf.at[i % COMBINE_BUFS]
    meta = meta_ref[...]
    meta_t = jnp.concatenate([meta, jnp.zeros((LANES - 8, n_tok), F32)], axis=0).T
    half = max(n_tok // 2, LANES)
    for r0 in range(0, n_tok, half):
        rows = slice(r0, r0 + half)
        d1, d2, w1, w2 = (meta_t[rows, i:i + 1] for i in range(4))
        acc = xmid_ref[rows, :]
        for c in range(SORT_ROWS // SORT_CHUNK):
            r_iota = (lax.broadcasted_iota(jnp.int32, (half, SORT_CHUNK), 1) + c * SORT_CHUNK).astype(F32)
            unsort = jnp.where(r_iota == d1, w1, jnp.where(r_iota == d2, w2, 0.0)).astype(BF16)
            acc = acc + _dot(unsort, yl_ref[c * SORT_CHUNK:(c + 1) * SORT_CHUNK, :])
        y_ref[rows, :] = _rmsnorm(acc, gf_ref[...])


def _ffn_schedule(tab):
    n_tok_tiles = tab.shape[0]
    strips = (tab[:, :, 0] * (1.0 / ROW_GRANULE)).astype(jnp.int32)
    starts = (tab[:, :, 1] * (1.0 / ROW_GRANULE)).astype(jnp.int32)
    cnt = strips.T
    row0 = starts.T
    cs = jnp.cumsum(cnt, axis=1) - cnt
    n_str = jnp.sum(cnt, axis=1)
    np_str = (n_str + FFN_SLOTS - 1) // FFN_SLOTS * FFN_SLOTS
    ends = jnp.cumsum(np_str)
    base = ends - np_str
    n_steps_max = _ffn_steps_max(n_tok_tiles) + GATHER_BUFS - 1
    step0 = jnp.arange(n_steps_max, dtype=jnp.int32) * FFN_SLOTS
    stream = jnp.minimum(jnp.sum(ends[None, :] <= step0[:, None], axis=1), N_EXPERTS - 1)
    pick = stream[:, None] == jnp.arange(N_EXPERTS, dtype=jnp.int32)[None, :]
    sel = lambda x: jnp.sum(jnp.where(pick[:, :, None], x[None], 0), axis=1)
    cs_j, cnt_j, row0_j = sel(cs), sel(cnt), sel(row0)
    base_j = jnp.sum(jnp.where(pick, base[None, :], 0), axis=1)
    q = step0[:, None] + jnp.arange(FFN_SLOTS, dtype=jnp.int32)[None, :] - base_j[:, None]
    reached = cs_j[:, None, :] <= q[:, :, None]
    last = lambda x: jnp.sum(jnp.where(reached, jnp.diff(x, axis=1, prepend=0)[:, None, :], 0), axis=2)
    tile_idx = jnp.sum(reached, axis=2).astype(jnp.int32) - 1
    g = q - last(cs_j)
    valid = g < last(cnt_j)
    granule = tile_idx * GRANULES_PER_TILE + last(row0_j) + g
    assert n_tok_tiles >= 2 * FFN_SLOTS
    slot = jnp.arange(FFN_SLOTS, dtype=jnp.int32)[None, :]
    parity = jnp.arange(n_steps_max, dtype=jnp.int32)[:, None] % 2
    pad_dst = lambda par: (slot + par * FFN_SLOTS) * GRANULES_PER_TILE + GRANULES_PER_TILE - 1
    gsrc = jnp.where(valid, granule, slot * GRANULES_PER_TILE + GRANULES_PER_TILE - 2).astype(jnp.int32)
    sdst = jnp.where(valid, granule, pad_dst(parity))
    sdst = jnp.concatenate([pad_dst(1), sdst], axis=0)
    texp = stream.astype(jnp.int32)
    n_steps = (ends[-1] // FFN_SLOTS).astype(jnp.int32).reshape(1)
    return texp, gsrc.reshape(-1), sdst.astype(jnp.int32).reshape(-1), n_steps


assert GRANULES_PER_TILE - (2 * TOK_TILE + N_EXPERTS * (ROW_GRANULE - 1)) // ROW_GRANULE >= 2


def _ffn_steps_max(n_tok_tiles):
    return -(-(n_tok_tiles * GRANULES_PER_TILE + N_EXPERTS * (FFN_SLOTS - 1)) // FFN_SLOTS)


def _vmem_limit(mib):
    n_bytes = mib * 1024 * 1024
    assert n_bytes < V7X_VMEM_BYTES
    return n_bytes


def _rope_tables(pos):
    inv_freq = ROPE_THETA ** (-jnp.arange(HALF, dtype=F32) * 2.0 / HEAD_DIM)
    ang = pos.astype(F32)[:, None] * inv_freq[None, :]
    cos, sin = jnp.cos(ang), jnp.sin(ang)
    reps = LANES // HEAD_DIM
    return jnp.tile(jnp.concatenate([cos, cos], axis=1), (1, reps)), jnp.tile(jnp.concatenate([-sin, sin], axis=1), (1, reps))


def _moe_combine(xmid, yl, meta, gf, n_tok, first_tile, n_tiles):
    return pl.pallas_call(
        functools.partial(_combine_kernel, first_tile=first_tile, n_tiles=n_tiles),
        grid=(n_tiles,),
        in_specs=[
            pl.BlockSpec((n_tok, D_MODEL), lambda i: (i, 0)),
            pl.BlockSpec(memory_space=pl.ANY),
            pl.BlockSpec((None, 8, n_tok), lambda i: (i, 0, 0)),
            pl.BlockSpec((1, D_MODEL), lambda i: (0, 0)),
        ],
        out_specs=pl.BlockSpec((n_tok, D_MODEL), lambda i: (i, 0)),
        out_shape=jax.ShapeDtypeStruct((n_tiles * n_tok, D_MODEL), F32),
        scratch_shapes=[pltpu.VMEM((COMBINE_BUFS, SORT_ROWS, D_MODEL), BF16), pltpu.SemaphoreType.DMA((COMBINE_BUFS,))],
        compiler_params=pltpu.CompilerParams(
            dimension_semantics=("arbitrary",),
            vmem_limit_bytes=_vmem_limit(40)),
        name="moe_combine",
    )(xmid, yl, meta, gf)


def kernel(x_prompt, x_sample, cache_swa_k, cache_swa_v, norm_mix_g, w_in, attn_sinks, gmlp_ln_g, gmlp_ln_b,
           gmlp_w_s, gmlp_b_s, w_out, norm_ffn_g, router_group_w, router_group_b, router_expert_w,
           router_expert_b, expert_w_gate, expert_w_up, expert_w_down, final_norm_g):
    assert norm_mix_g.shape[0] == 1, "single-layer trunk"
    batch, seq, _ = x_prompt.shape
    dec_batch = x_sample.shape[0]
    assert x_sample.shape[1] == 1 and seq % TOK_TILE == 0

    win_ext = w_in[0].astype(BF16)
    wout = w_out[0].astype(BF16)
    g1 = norm_mix_g[0][None, :]
    g2 = norm_ffn_g[0][None, :]
    gf = final_norm_g[None, :]
    lng = gmlp_ln_g[0][None, :]
    lnb = gmlp_ln_b[0][None, :]
    sinks = attn_sinks[0]
    ws = gmlp_w_s[0]
    group_dim = GMLP_WIDTH // GMLP_GROUPS
    bsf = jnp.repeat(gmlp_b_s[0].T, group_dim, axis=1)
    ws0 = jnp.repeat(ws[:, 0, 0], group_dim)[None, :]
    bs0 = jnp.repeat(gmlp_b_s[0][:, 0], group_dim)[None, :]
    wrt = jnp.zeros((ROUTER_ROWS, D_MODEL), F32)
    wrt = wrt.at[0:N_GROUPS].set(router_group_w[0].T)
    wrt = wrt.at[8:8 + N_EXPERTS].set(router_expert_w[0].reshape(D_MODEL, N_EXPERTS).T).astype(BF16)
    brt = jnp.full((ROUTER_ROWS, 1), NEG_INF, F32)
    brt = brt.at[0:N_GROUPS, 0].set(router_group_b[0])
    brt = brt.at[8:8 + N_EXPERTS, 0].set(router_expert_b[0].reshape(N_EXPERTS))
    wg, wu, wd = expert_w_gate[0], expert_w_up[0], expert_w_down[0]
    cos_p, sin_p = _rope_tables(jnp.arange(seq, dtype=jnp.int32))
    cos_s, sin_s = _rope_tables(PAST_LEN + jnp.arange(1, dtype=jnp.int32))

    full = lambda shape: pl.BlockSpec(shape, lambda *_: (0,) * len(shape))
    smem = pl.BlockSpec(memory_space=pltpu.SMEM)
    n_tiles = seq // TOK_TILE

    upper = jnp.triu(jnp.ones((TOK_TILE, TOK_TILE), BF16), k=1)
    lpad = (jnp.arange(LANES)[None, :] < jnp.arange(N_EXPERTS)[:, None]).astype(BF16)
    n_tok_tiles = batch * n_tiles

    x2d = x_prompt.reshape(batch * seq, D_MODEL)
    tile_a = lambda s: jnp.minimum(s, n_tok_tiles - 1)
    cur = lambda s: jnp.maximum(s - 1, 0)
    tile_b = cur
    xmid_p, xl_p, meta_p, tab_p, k_p, v_p = pl.pallas_call(
        functools.partial(_prompt_mixer_kernel, tiles_per_seq=n_tiles),
        grid=(n_tok_tiles + 1,),
        in_specs=[
            smem,
            pl.BlockSpec((TOK_TILE, D_MODEL), lambda s: (tile_a(s), 0)),
            pl.BlockSpec((TOK_TILE, D_MODEL), lambda s: (cur(s), 0)),
            full((1, D_MODEL)),
            full((D_MODEL, IN_WIDTH)),
            pl.BlockSpec((TOK_TILE, LANES), lambda s: (tile_b(s) % n_tiles, 0)),
            pl.BlockSpec((TOK_TILE, LANES), lambda s: (tile_b(s) % n_tiles, 0)),
            full((1, GMLP_WIDTH)),
            full((1, GMLP_WIDTH)),
            full((GMLP_GROUPS, CHUNK, CHUNK)),
            full((CHUNK, GMLP_WIDTH)),
            full((D_MODEL, D_MODEL)),
            full((1, D_MODEL)),
            full((ROUTER_ROWS, D_MODEL)),
            full((ROUTER_ROWS, 1)),
            full((TOK_TILE, TOK_TILE)),
            full((N_EXPERTS, LANES)),
        ],
        out_specs=[
            pl.BlockSpec((TOK_TILE, D_MODEL), lambda s: (cur(s), 0)),
            pl.BlockSpec((None, SORT_ROWS, D_MODEL), lambda s: (jnp.where(s == 0, n_tok_tiles, s - 1), 0, 0)),
            pl.BlockSpec((None, 8, TOK_TILE), lambda s: (cur(s), 0, 0)),
            pl.BlockSpec((None, N_EXPERTS, LANES), lambda s: (cur(s), 0, 0)),
            pl.BlockSpec((None, WINDOW, KV_WIDTH), lambda s: (tile_b(s) // n_tiles, 0, 0)),
            pl.BlockSpec((None, WINDOW, KV_WIDTH), lambda s: (tile_b(s) // n_tiles, 0, 0)),
        ],
        out_shape=[
            jax.ShapeDtypeStruct((batch * seq, D_MODEL), F32),
            jax.ShapeDtypeStruct((n_tok_tiles + 1, SORT_ROWS, D_MODEL), BF16),
            jax.ShapeDtypeStruct((n_tok_tiles, 8, TOK_TILE), F32),
            jax.ShapeDtypeStruct((n_tok_tiles, N_EXPERTS, LANES), F32),
            jax.ShapeDtypeStruct((batch, WINDOW, KV_WIDTH), F32),
            jax.ShapeDtypeStruct((batch, WINDOW, KV_WIDTH), F32),
        ],
        scratch_shapes=[
            pltpu.VMEM((TOK_TILE, IN_WIDTH), F32),
            pltpu.VMEM((TOK_TILE, IN_WIDTH), F32),
            pltpu.VMEM((TOK_TILE, D_MODEL), BF16),
        ] + [pltpu.VMEM((WINDOW + TOK_TILE, KV_WIDTH), BF16)] * 4 + [
            pltpu.VMEM((len(_NAT_HEADS), TOK_TILE, LANES), BF16),
            pltpu.VMEM((len(_SWP_HEADS), TOK_TILE, LANES), BF16),
            pltpu.VMEM((2 * BLOCKS_PER_TILE, len(_NAT_HEADS) * WINDOW, 2 * WINDOW), F32),
            pltpu.VMEM((2 * BLOCKS_PER_TILE, len(_NAT_HEADS) * WINDOW, 2 * WINDOW), BF16),
            pltpu.VMEM((TOK_TILE, GMLP_WIDTH), BF16),
            pltpu.VMEM((TOK_TILE, D_MODEL), BF16),
            pltpu.VMEM((TOK_TILE, D_MODEL), BF16),
        ],
        compiler_params=pltpu.CompilerParams(
            dimension_semantics=("arbitrary",),
            vmem_limit_bytes=_vmem_limit(56)),
        name="prompt_mixer",
    )(sinks, x2d, x2d, g1, win_ext, cos_p, sin_p, lng, lnb, ws, bsf, wout, g2, wrt, brt, upper, lpad)

    xs = x_sample.reshape(dec_batch, D_MODEL)
    ck = cache_swa_k[0].reshape(dec_batch, WINDOW, KV_WIDTH)
    cv = cache_swa_v[0].reshape(dec_batch, WINDOW, KV_WIDTH)
    vmem = pl.BlockSpec(memory_space=pltpu.VMEM)
    hbm = pl.BlockSpec(memory_space=pl.ANY)
    xmid_s, meta_s, tab_s, k_s, v_s, vn_s, xl_all = pl.pallas_call(
        _sample_mixer_kernel,
        in_specs=[smem] + [vmem] * 17 + [hbm],
        out_specs=[vmem] * 6 + [hbm],
        out_shape=[
            jax.ShapeDtypeStruct((dec_batch, D_MODEL), F32),
            jax.ShapeDtypeStruct((8, dec_batch), F32),
            jax.ShapeDtypeStruct((N_EXPERTS, LANES), F32),
            jax.ShapeDtypeStruct((dec_batch, KV_WIDTH), F32),
            jax.ShapeDtypeStruct((dec_batch, KV_WIDTH), F32),
            jax.ShapeDtypeStruct((dec_batch, GMLP_WIDTH), F32),
            jax.ShapeDtypeStruct(xl_p.shape, BF16),
        ],
        scratch_shapes=[pltpu.VMEM((dec_batch, D_MODEL), BF16), pltpu.VMEM((SORT_ROWS, D_MODEL), BF16),
                        pltpu.SemaphoreType.DMA((1,))],
        compiler_params=pltpu.CompilerParams(vmem_limit_bytes=_vmem_limit(56)),
        input_output_aliases={18: 6},
        name="sample_mixer",
    )(sinks, xs, ck, cv, g1, win_ext, cos_s, sin_s, lng, lnb, ws0, bs0, wout, g2, wrt, brt,
      upper[:dec_batch, :dec_batch], lpad, xl_p)

    n_all_tiles = n_tok_tiles + 1
    tab_all = jnp.concatenate([tab_p, tab_s[None]], axis=0)
    texp, gsrc, sdst, n_steps = _ffn_schedule(tab_all)
    yl_all = pl.pallas_call(
        _grouped_ffn_kernel,
        grid_spec=pltpu.PrefetchScalarGridSpec(
            num_scalar_prefetch=4,
            grid=(_ffn_steps_max(n_all_tiles),),
            in_specs=[
                pl.BlockSpec(memory_space=pl.ANY),
                pl.BlockSpec((None, D_MODEL, D_EXPERT), lambda j, te, gs, sd, ns: (te[j], 0, 0)),
                pl.BlockSpec((None, D_MODEL, D_EXPERT), lambda j, te, gs, sd, ns: (te[j], 0, 0)),
                pl.BlockSpec((None, D_EXPERT, D_MODEL), lambda j, te, gs, sd, ns: (te[j], 0, 0)),
            ],
            out_specs=pl.BlockSpec(memory_space=pl.ANY),
            scratch_shapes=[
                pltpu.VMEM((GATHER_BUFS, FFN_SLOTS, ROW_GRANULE, D_MODEL), BF16),
                pltpu.VMEM((SCATTER_BUFS, FFN_SLOTS, ROW_GRANULE, D_MODEL), BF16),
                pltpu.VMEM((D_MODEL, D_EXPERT), BF16),
                pltpu.VMEM((D_MODEL, D_EXPERT), BF16),
                pltpu.VMEM((D_EXPERT, D_MODEL), BF16),
                pltpu.SemaphoreType.DMA((GATHER_BUFS,)),
                pltpu.SemaphoreType.DMA((SCATTER_BUFS,)),
            ],
        ),
        out_shape=jax.ShapeDtypeStruct((n_all_tiles * GRANULES_PER_TILE, ROW_GRANULE, D_MODEL), BF16),
        compiler_params=pltpu.CompilerParams(
            dimension_semantics=("arbitrary",),
            vmem_limit_bytes=_vmem_limit(32)),
        input_output_aliases={4: 0},
        name="grouped_ffn",
    )(texp, gsrc, sdst, n_steps, xl_all.reshape(n_all_tiles * GRANULES_PER_TILE, ROW_GRANULE, D_MODEL), wg, wu, wd)
    yl_all = yl_all.reshape(n_all_tiles * SORT_ROWS, D_MODEL)

    y_p = _moe_combine(xmid_p, yl_all, meta_p, gf, TOK_TILE, 0, n_tok_tiles)
    y_s = _moe_combine(xmid_s, yl_all, meta_s[None], gf, dec_batch, n_tok_tiles, 1)

    return (y_p.reshape(batch, seq, D_MODEL),
            y_s.reshape(dec_batch, 1, D_MODEL),
            k_p.reshape(1, batch, WINDOW, N_KV_HEADS, HEAD_DIM),
            v_p.reshape(1, batch, WINDOW, N_KV_HEADS, HEAD_DIM),
            k_s.reshape(1, dec_batch, 1, N_KV_HEADS, HEAD_DIM),
            v_s.reshape(1, dec_batch, 1, N_KV_HEADS, HEAD_DIM),
            vn_s.reshape(1, dec_batch, 1, GMLP_WIDTH))
```

```python
import functools

import jax
import jax.numpy as jnp
import numpy as np
from jax import lax
from jax.experimental import pallas as pl
from jax.experimental.pallas import tpu as pltpu

F32 = jnp.float32
BF16 = jnp.bfloat16

D_MODEL = 1024
HEAD_DIM = 64
HALF = HEAD_DIM // 2
N_Q_HEADS = 8
N_KV_HEADS = 2
ATTN_WIDTH = N_Q_HEADS * HEAD_DIM
KV_WIDTH = N_KV_HEADS * HEAD_DIM
WINDOW = 128
ROPE_THETA = 10000.0
GMLP_WIDTH = D_MODEL - ATTN_WIDTH
GMLP_GROUPS = 8
CHUNK = 128
N_GROUPS = 4
EXPERTS_PER_GROUP = 8
N_EXPERTS = N_GROUPS * EXPERTS_PER_GROUP
D_EXPERT = 256
EPS = 1e-6
NEG_INF = -1e30
PAST_LEN = 16384

LANES = 128
V7X_VMEM_BYTES = 64 * 1024 * 1024

C_Q = 0
C_K = C_Q + ATTN_WIDTH
C_V = C_K + KV_WIDTH
C_U = C_V + KV_WIDTH
C_VG = C_U + GMLP_WIDTH
IN_WIDTH = C_VG + GMLP_WIDTH

ROUTER_ROWS = 48
TOK_TILE = 512
BLOCKS_PER_TILE = TOK_TILE // WINDOW
ROW_GRANULE = 16
SORT_CHUNK = 512
SORT_ROWS = -(-(2 * TOK_TILE + N_EXPERTS * (ROW_GRANULE - 1)) // SORT_CHUNK) * SORT_CHUNK
GRANULES_PER_TILE = SORT_ROWS // ROW_GRANULE
FFN_ROWS = 512
FFN_SLOTS = FFN_ROWS // ROW_GRANULE
GATHER_BUFS = 3
SCATTER_BUFS = 3
COMBINE_BUFS = 3


def _dot(a, b):
    return jnp.dot(a, b, preferred_element_type=F32)


def _dot_nt(a, b):
    return lax.dot_general(a, b, (((1,), (1,)), ((), ())), preferred_element_type=F32)


def _gelu(x):
    return 0.5 * x * (1.0 + lax.erf(x * np.float32(np.sqrt(0.5))))


def _rmsnorm(x, g):
    return x * lax.rsqrt(jnp.mean(x * x, axis=-1, keepdims=True) + EPS) * g


def _layernorm(x, g, b):
    mu = jnp.mean(x, axis=-1, keepdims=True)
    xc = x - mu
    return xc * lax.rsqrt(jnp.mean(xc * xc, axis=-1, keepdims=True) + EPS) * g + b


def _first_argmax_rows(x, row_iota, n_rows):
    m = jnp.max(x, axis=0, keepdims=True)
    idx = jnp.min(jnp.where(x == m, row_iota, n_rows), axis=0, keepdims=True)
    return m, idx


def _route(h2b, wrt_ref, brt_ref):
    n_tok = h2b.shape[0]
    lt = _dot_nt(wrt_ref[...], h2b) + brt_ref[...]
    row8 = lax.broadcasted_iota(jnp.int32, (8, n_tok), 0).astype(F32)
    glog = lt[0:8]
    gmax, gidx = _first_argmax_rows(glog, row8, 8)
    g_w = 1.0 / jnp.sum(jnp.exp(glog - gmax), axis=0, keepdims=True)
    esel = lt[8:16]
    for g in range(1, N_GROUPS):
        esel = jnp.where(gidx == g, lt[8 + 8 * g:16 + 8 * g], esel)
    m1, i1 = _first_argmax_rows(esel, row8, 8)
    esel2 = jnp.where(row8 == i1, -jnp.inf, esel)
    m2, i2 = _first_argmax_rows(esel2, row8, 8)
    r = jnp.exp(m2 - m1)
    w1 = 1.0 / (1.0 + r)
    w2 = r / (1.0 + r)
    e1 = gidx * EXPERTS_PER_GROUP + i1
    e2 = gidx * EXPERTS_PER_GROUP + i2
    slab = jnp.where(row8 == 0, e1, 0.0)
    for r_idx, val in ((1, e2), (2, w1 * g_w), (3, w2 * g_w)):
        slab = jnp.where(row8 == r_idx, val, slab)
    return slab


def _softmax_with_sink(s, sink):
    m = jnp.maximum(jnp.max(s, axis=-1, keepdims=True), sink)
    e = jnp.exp(s - m)
    den = jnp.sum(e, axis=-1, keepdims=True) + jnp.exp(sink - m)
    return e * (1.0 / den)


def _rope(x, cos, sin_signed):
    first_half = (lax.broadcasted_iota(jnp.int32, x.shape, 1) & HALF) == 0
    partner = jnp.where(first_half, pltpu.roll(x, LANES - HALF, 1), pltpu.roll(x, HALF, 1))
    return x * cos + partner * sin_signed


_NAT_HEADS = (0, 2, 5, 7)
_SWP_HEADS = (1, 3, 4, 6)


def _prompt_mixer_kernel(sinks_ref, x_next_ref, x_ref, g1_ref, win_ref, cos_ref, sin_ref, lng_ref, lnb_ref, ws_ref,
                         bsf_ref, wout_ref, g2_ref, wrt_ref, brt_ref, upper_ref, lpad_ref, wgf_ref, wuf_ref, wdf_ref,
                         xmid_ref, xl_ref, meta_ref, tab_ref, kout_ref, vout_ref, wgo_ref, wuo_ref, wdo_ref,
                         z_a, z_b, mix_ref, k_n, k_s, v_n, v_s, q_nat, q_swp, s_ref, p_ref, vn_ref, h_ref, h2_ref,
                         *, tiles_per_seq):
    step = pl.program_id(0)
    t = jnp.maximum(step - 1, 0) % tiles_per_seq
    kv_bufs = (k_n, k_s, v_n, v_s)

    @pl.when(step == 0)
    def _():
        z_b[...] = jnp.zeros_like(z_b)

    for src, dst in ((wgf_ref, wgo_ref), (wuf_ref, wuo_ref), (wdf_ref, wdo_ref)):
        dst[...] = src[...].astype(BF16)

    @pl.when(t == 0)
    def _():
        for ref in kv_bufs:
            ref[0:WINDOW, :] = jnp.zeros((WINDOW, KV_WIDTH), BF16)

    args = (t, sinks_ref, x_next_ref, x_ref, g1_ref, win_ref, cos_ref, sin_ref, lng_ref, lnb_ref, ws_ref, bsf_ref,
            wout_ref, g2_ref, wrt_ref, brt_ref, upper_ref, lpad_ref, xmid_ref, xl_ref, meta_ref, tab_ref, kout_ref,
            vout_ref, kv_bufs, q_nat, q_swp, s_ref, p_ref, vn_ref, h_ref, h2_ref)

    @pl.when(step % 2 == 0)
    def _():
        _prompt_mixer_body(z_a, z_b, mix_ref, *args)

    @pl.when(step % 2 == 1)
    def _():
        _prompt_mixer_body(z_b, z_a, mix_ref, *args)


def _prompt_mixer_body(z_next, z_ref, mix_ref, t, sinks_ref, x_next_ref, x_ref, g1_ref, win_ref, cos_ref,
                       sin_ref, lng_ref, lnb_ref, ws_ref, bsf_ref, wout_ref, g2_ref, wrt_ref, brt_ref, upper_ref,
                       lpad_ref, xmid_ref, xl_ref, meta_ref, tab_ref, kout_ref, vout_ref, kv_bufs,
                       q_nat, q_swp, s_ref, p_ref, vn_ref, h_ref, h2_ref):
    k_n, k_s, v_n, v_s = kv_bufs
    cos = cos_ref[...]
    sin = sin_ref[...]
    lane = lax.broadcasted_iota(jnp.int32, (WINDOW, LANES), 1)
    lo = lane < HEAD_DIM
    row = lax.broadcasted_iota(jnp.int32, (WINDOW, WINDOW), 0)
    col = lax.broadcasted_iota(jnp.int32, (WINDOW, WINDOW), 1)
    mask_cur = col <= row
    mask_prev_band = col >= row
    mask_prev_first = jnp.logical_and(mask_prev_band, (jnp.zeros_like(row) + t) > 0)
    mask_band = jnp.concatenate([mask_prev_band, mask_cur], axis=1)
    mask_first = jnp.concatenate([mask_prev_first, mask_cur], axis=1)

    cq = cos * np.float32(HEAD_DIM ** -0.5)
    sq = sin * np.float32(HEAD_DIM ** -0.5)
    kf = _rope(z_ref[:, C_K:C_K + KV_WIDTH], cos, sin)
    vf = z_ref[:, C_V:C_V + KV_WIDTH]
    k_n[WINDOW:, :] = kf.astype(BF16)
    k_s[WINDOW:, :] = pltpu.roll(kf, HEAD_DIM, 1).astype(BF16)
    v_n[WINDOW:, :] = vf.astype(BF16)
    v_s[WINDOW:, :] = pltpu.roll(vf, HEAD_DIM, 1).astype(BF16)

    kout_ref[...] = kf[TOK_TILE - WINDOW:]
    vout_ref[...] = vf[TOK_TILE - WINDOW:]

    lo_t = lax.broadcasted_iota(jnp.int32, (TOK_TILE, LANES), 1) < HEAD_DIM
    for m in range(N_Q_HEADS // 2):
        qc = _rope(z_ref[:, C_Q + m * LANES:C_Q + (m + 1) * LANES], cq, sq)
        for hd, qh in ((2 * m, jnp.where(lo_t, qc, 0.0)), (2 * m + 1, jnp.where(lo_t, 0.0, qc))):
            if hd in _NAT_HEADS:
                q_nat[_NAT_HEADS.index(hd)] = qh.astype(BF16)
            else:
                q_swp[_SWP_HEADS.index(hd)] = qh.astype(BF16)

    stacks = ((q_nat, _NAT_HEADS, k_n, v_n), (q_swp, _SWP_HEADS, k_s, v_s))
    proj_cols = 2 * LANES

    def project_next(c):
        cols = slice(c * proj_cols, (c + 1) * proj_cols)
        z_next[:, cols] = _dot(h_ref[...], win_ref[:, cols])

    def scores(j):
        rows = slice(j * WINDOW, (j + 1) * WINDOW)
        keys = slice(j * WINDOW, (j + 2) * WINDOW)
        for si, (q_ref, _, k_buf, _) in enumerate(stacks):
            q_stack = jnp.concatenate([q_ref[i, rows, :] for i in range(len(_NAT_HEADS))], axis=0)
            s_ref[2 * j + si] = _dot_nt(q_stack, k_buf[keys, :])

    def softmax(j):
        mask = mask_first if j == 0 else mask_band
        for si, (_, heads, _, _) in enumerate(stacks):
            for i, hd in enumerate(heads):
                pr = slice(i * WINDOW, (i + 1) * WINDOW)
                p_ref[2 * j + si, pr, :] = _softmax_with_sink(
                    jnp.where(mask, s_ref[2 * j + si, pr, :], NEG_INF), sinks_ref[hd]).astype(BF16)

    def attend(j):
        rows = slice(j * WINDOW, (j + 1) * WINDOW)
        keys = slice(j * WINDOW, (j + 2) * WINDOW)
        o_nat = _dot(p_ref[2 * j], v_n[keys, :])
        o_swp = _dot(p_ref[2 * j + 1], v_s[keys, :])
        for m in range(N_Q_HEADS // 2):
            pr = slice(m * WINDOW, (m + 1) * WINDOW)
            even_nat = (2 * m) in _NAT_HEADS
            att = jnp.where(lo, o_nat[pr], o_swp[pr]) if even_nat else jnp.where(lo, o_swp[pr], o_nat[pr])
            mix_ref[rows, m * LANES:(m + 1) * LANES] = att.astype(BF16)

    def gmlp(m):
        cs = slice(m * LANES, (m + 1) * LANES)
        w0 = jnp.where(mask_cur, ws_ref[2 * m], 0.0).astype(BF16)
        w1 = jnp.where(mask_cur, ws_ref[2 * m + 1], 0.0).astype(BF16)
        wcat = jnp.concatenate([w0, w1], axis=1)
        for j in range(BLOCKS_PER_TILE):
            rows = slice(j * WINDOW, (j + 1) * WINDOW)
            vcol = vn_ref[rows, cs]
            rhs = jnp.concatenate([jnp.where(lo, vcol, jnp.zeros_like(vcol)),
                                   jnp.where(lo, jnp.zeros_like(vcol), vcol)], axis=0)
            sp = _dot(wcat, rhs) + bsf_ref[:, cs]
            u = _gelu(z_ref[rows, C_U + m * LANES:C_U + (m + 1) * LANES])
            mix_ref[rows, ATTN_WIDTH + m * LANES:ATTN_WIDTH + (m + 1) * LANES] = (u * sp).astype(BF16)

    def project_out(c):
        cols = slice(c * proj_cols, (c + 1) * proj_cols)
        xmid_ref[:, cols] = x_ref[:, cols] + _dot(mix_ref[...], wout_ref[:, cols])

    h_ref[...] = _rmsnorm(x_next_ref[...], g1_ref[...]).astype(BF16)
    project_next(0)
    project_next(1)
    for j in range(BLOCKS_PER_TILE):
        scores(j)
    vn_ref[...] = _layernorm(_gelu(z_ref[:, C_VG:C_VG + GMLP_WIDTH]), lng_ref[...], lnb_ref[...]).astype(BF16)
    project_next(2)
    softmax(0)
    project_next(3)
    softmax(1)
    project_next(4)
    softmax(2)
    project_next(5)
    softmax(3)
    project_next(6)
    for j in range(BLOCKS_PER_TILE):
        attend(j)
    for m in range(GMLP_GROUPS // 2):
        gmlp(m)
    for c in range(D_MODEL // proj_cols):
        project_out(c)
    h2_ref[...] = _rmsnorm(xmid_ref[...], g2_ref[...]).astype(BF16)
    d1, d2 = _sort_plan(_route(h2_ref[...], wrt_ref, brt_ref), upper_ref, lpad_ref, meta_ref, tab_ref)
    _sort_rows(d1, d2, h2_ref[...], xl_ref)

    for ref in kv_bufs:
        ref[0:WINDOW, :] = ref[TOK_TILE:TOK_TILE + WINDOW, :]


def _sort_plan(slab, upper_ref, lpad_ref, meta_ref, tab_ref):
    n_tok = slab.shape[1]
    e1, e2 = slab[0:1], slab[1:2]
    row32 = lax.broadcasted_iota(jnp.int32, (N_EXPERTS, n_tok), 0).astype(F32)
    sel1 = row32 == e1
    sel2 = row32 == e2
    onehot = jnp.where(sel1, 1.0, jnp.where(sel2, 1.0, 0.0))
    earlier = _dot(onehot.astype(BF16), upper_ref[...])
    cnt = jnp.sum(onehot, axis=1, keepdims=True)
    pc = jnp.floor((cnt + (ROW_GRANULE - 1)) * (1.0 / ROW_GRANULE)) * ROW_GRANULE
    pc_b = jnp.broadcast_to(pc, (N_EXPERTS, LANES))
    pc_pad = jnp.concatenate([pc_b, jnp.zeros((LANES - N_EXPERTS, LANES), F32)], axis=0).astype(BF16)
    start = _dot(lpad_ref[...], pc_pad)
    base = start[:, 0:1] + earlier
    d1 = jnp.sum(jnp.where(sel1, base, 0.0), axis=0, keepdims=True)
    d2 = jnp.sum(jnp.where(sel2, base, 0.0), axis=0, keepdims=True)
    row8 = lax.broadcasted_iota(jnp.int32, (8, n_tok), 0)
    meta_ref[...] = jnp.where(row8 == 0, d1, jnp.where(row8 == 1, d2, jnp.where(row8 >= 4, 0.0, slab)))
    lane = lax.broadcasted_iota(jnp.int32, (N_EXPERTS, LANES), 1)
    tab_ref[...] = jnp.where(lane == 0, pc_b, jnp.where(lane == 1, start, 0.0))
    return d1, d2


def _sort_rows(d1, d2, h2b, xl_ref):
    n_tok = h2b.shape[0]
    for c in range(SORT_ROWS // SORT_CHUNK):
        r_iota = (lax.broadcasted_iota(jnp.int32, (SORT_CHUNK, n_tok), 0) + c * SORT_CHUNK).astype(F32)
        perm = jnp.where(r_iota == d1, 1.0, jnp.where(r_iota == d2, 1.0, 0.0)).astype(BF16)
        xl_ref[c * SORT_CHUNK:(c + 1) * SORT_CHUNK, :] = _dot(perm, h2b).astype(BF16)


def _sample_mixer_kernel(sinks_ref, x_ref, ck_ref, cv_ref, g1_ref, win_ref, cos_ref, sin_ref, lng_ref, lnb_ref,
                         ws0_ref, bs0_ref, wout_ref, g2_ref, wrt_ref, brt_ref, upper_ref, lpad_ref, xl_in_hbm,
                         xmid_ref, meta_ref, tab_ref, kout_ref, vout_ref, vnout_ref, xl_hbm,
                         mix_ref, xl_tile, xl_sem):
    del xl_in_hbm
    n_seq = x_ref.shape[0]
    seq_chunk = 16
    x = x_ref[...]
    h = _rmsnorm(x, g1_ref[...]).astype(BF16)
    z = _dot(h, win_ref[...])
    cos = cos_ref[...]
    sin = sin_ref[...]
    scale = np.float32(HEAD_DIM ** -0.5)
    lane = lax.broadcasted_iota(jnp.int32, (n_seq, LANES), 1)
    lo = lane < HEAD_DIM
    kf = _rope(z[:, C_K:C_K + KV_WIDTH], cos, sin)
    vf = z[:, C_V:C_V + KV_WIDTH]
    kout_ref[...] = kf
    vout_ref[...] = vf
    kb = kf.astype(BF16).astype(F32)
    vb = vf.astype(BF16).astype(F32)

    q_heads = []
    for hd in range(N_Q_HEADS):
        m = hd // 2
        qc = _rope(z[:, C_Q + m * LANES:C_Q + (m + 1) * LANES], cos, sin) * scale
        keep = lo if hd % 2 == 0 else ~lo
        qm = jnp.where(keep, qc, 0.0)
        if (hd % 2) != (hd // (N_Q_HEADS // N_KV_HEADS)):
            qm = pltpu.roll(qm, HEAD_DIM, 1)
        q_heads.append(qm.astype(BF16))

    s_new = [jnp.sum(q_heads[hd].astype(F32) * kb, axis=-1, keepdims=True) for hd in range(N_Q_HEADS)]

    rr = lax.broadcasted_iota(jnp.int32, (N_Q_HEADS * seq_chunk, seq_chunk * WINDOW), 0)
    cc = lax.broadcasted_iota(jnp.int32, (N_Q_HEADS * seq_chunk, seq_chunk * WINDOW), 1)
    same_seq = (rr % seq_chunk) == (cc // WINDOW)
    kv_lo = lax.broadcasted_iota(jnp.int32, (seq_chunk, LANES), 1) < HEAD_DIM

    for c in range(n_seq // seq_chunk):
        sr = slice(c * seq_chunk, (c + 1) * seq_chunk)
        kc = ck_ref[sr].reshape(seq_chunk * WINDOW, KV_WIDTH).astype(BF16)
        vc = cv_ref[sr].reshape(seq_chunk * WINDOW, KV_WIDTH).astype(BF16)
        qs = jnp.concatenate([q_heads[hd][sr] for hd in range(N_Q_HEADS)], axis=0)
        s = jnp.where(same_seq, _dot_nt(qs, kc), NEG_INF)
        sn = jnp.concatenate([s_new[hd][sr] for hd in range(N_Q_HEADS)], axis=0)
        sink = jnp.concatenate([jnp.full((seq_chunk, 1), sinks_ref[hd], F32) for hd in range(N_Q_HEADS)], axis=0)
        m = jnp.maximum(jnp.maximum(jnp.max(s, axis=-1, keepdims=True), sn), sink)
        e = jnp.exp(s - m)
        en = jnp.exp(sn - m)
        inv = 1.0 / (jnp.sum(e, axis=-1, keepdims=True) + en + jnp.exp(sink - m))
        o = _dot((e * inv).astype(BF16), vc)
        pn = (en * inv).astype(BF16).astype(F32)
        for mcol in range(N_Q_HEADS // 2):
            halves = []
            for hd in (2 * mcol, 2 * mcol + 1):
                oh = o[hd * seq_chunk:(hd + 1) * seq_chunk] + pn[hd * seq_chunk:(hd + 1) * seq_chunk] * vb[sr]
                if (hd % 2) != (hd // (N_Q_HEADS // N_KV_HEADS)):
                    oh = pltpu.roll(oh, HEAD_DIM, 1)
                halves.append(oh)
            att = jnp.where(kv_lo, halves[0], halves[1])
            mix_ref[sr, mcol * LANES:(mcol + 1) * LANES] = att.astype(BF16)

    u = _gelu(z[:, C_U:C_U + GMLP_WIDTH])
    vn = _layernorm(_gelu(z[:, C_VG:C_VG + GMLP_WIDTH]), lng_ref[...], lnb_ref[...])
    vnout_ref[...] = vn
    sp = ws0_ref[...].astype(BF16).astype(F32) * vn.astype(BF16).astype(F32) + bs0_ref[...]
    mix_ref[:, ATTN_WIDTH:] = (u * sp).astype(BF16)

    xmid = x + _dot(mix_ref[...], wout_ref[...])
    xmid_ref[...] = xmid
    h2b = _rmsnorm(xmid, g2_ref[...]).astype(BF16)
    d1, d2 = _sort_plan(_route(h2b, wrt_ref, brt_ref), upper_ref, lpad_ref, meta_ref, tab_ref)
    _sort_rows(d1, d2, h2b, xl_tile)
    copy = pltpu.make_async_copy(xl_tile, xl_hbm.at[xl_hbm.shape[0] - 1], xl_sem.at[0])
    copy.start()
    copy.wait()


def _grouped_ffn_kernel(texp_ref, gsrc_ref, sdst_ref, ntiles_ref, xl_hbm, wg_ref, wu_ref, wd_ref, yl_hbm,
                        xbuf, ybuf, gsem, ssem):
    j = pl.program_id(0)
    n_tiles = ntiles_ref[0]

    def gather_copy(tile, s, b):
        return pltpu.make_async_copy(xl_hbm.at[gsrc_ref[tile * FFN_SLOTS + s]], xbuf.at[b, s], gsem.at[b])

    def scatter_copy(tile, s):
        b = (tile + SCATTER_BUFS) % SCATTER_BUFS
        return pltpu.make_async_copy(ybuf.at[b, s], yl_hbm.at[sdst_ref[(tile + 1) * FFN_SLOTS + s]], ssem.at[b])

    @pl.when(j < n_tiles)
    def _():
        gb = j % GATHER_BUFS
        ahead = GATHER_BUFS - 1
        nxt = j + ahead
        nxt_b = nxt % GATHER_BUFS

        @pl.when(j == 0)
        def _():
            ybuf[SCATTER_BUFS - 1] = jnp.zeros(ybuf.shape[1:], BF16)
            for k in range(ahead):
                for s in range(FFN_SLOTS):
                    gather_copy(k, s, k).start()

        for s in range(FFN_SLOTS):
            gather_copy(j, s, gb).wait()

        @pl.when(j >= SCATTER_BUFS - 1)
        def _():
            for s in range(FFN_SLOTS):
                scatter_copy(j - SCATTER_BUFS, s).wait()

        x = xbuf[gb].reshape(FFN_ROWS, D_MODEL)
        gate = _dot(x, wg_ref[...])
        up = _dot(x, wu_ref[...])
        hid = (gate * (1.0 / (1.0 + jnp.exp(-gate))) * up).astype(BF16)
        wd = wd_ref[...]
        for s in range(FFN_SLOTS):
            scatter_copy(j - 1, s).start()
        for s in range(FFN_SLOTS):
            gather_copy(nxt, s, nxt_b).start()
        ybuf[j % SCATTER_BUFS] = _dot(hid, wd).astype(BF16).reshape(FFN_SLOTS, ROW_GRANULE, D_MODEL)

        @pl.when(j == n_tiles - 1)
        def _():
            for k in range(1, GATHER_BUFS):
                for s in range(FFN_SLOTS):
                    gather_copy(nxt, s, (j + k) % GATHER_BUFS).wait()
            for s in range(FFN_SLOTS):
                scatter_copy(j, s).start()
            for back in range(SCATTER_BUFS):
                @pl.when(j - back >= -1)
                def _():
                    for s in range(FFN_SLOTS):
                        scatter_copy(j - back, s).wait()


def _combine_kernel(xmid_ref, yl_hbm, meta_ref, gf_ref, y_ref, ybuf, ysem, *, first_tile, n_tiles):
    n_tok = xmid_ref.shape[0]
    i = pl.program_id(0)
    ahead = COMBINE_BUFS - 1

    def fetch(tile):
        row0 = pl.multiple_of((first_tile + tile) * SORT_ROWS, SORT_ROWS)
        slot = tile % COMBINE_BUFS
        return pltpu.make_async_copy(yl_hbm.at[pl.ds(row0, SORT_ROWS), :], ybuf.at[slot], ysem.at[slot])

    @pl.when(i == 0)
    def _():
        for k in range(min(ahead, n_tiles)):
            fetch(k).start()

    @pl.when(i + ahead < n_tiles)
    def _():
        fetch(i + ahead).start()

    fetch(i).wait()
    yl_ref = ybuf.at[i % COMBINE_BUFS]
    meta = meta_ref[...]
    meta_t = jnp.concatenate([meta, jnp.zeros((LANES - 8, n_tok), F32)], axis=0).T
    half = max(n_tok // 2, LANES)
    for r0 in range(0, n_tok, half):
        rows = slice(r0, r0 + half)
        d1, d2, w1, w2 = (meta_t[rows, i:i + 1] for i in range(4))
        acc = xmid_ref[rows, :]
        for c in range(SORT_ROWS // SORT_CHUNK):
            r_iota = (lax.broadcasted_iota(jnp.int32, (half, SORT_CHUNK), 1) + c * SORT_CHUNK).astype(F32)
            unsort = jnp.where(r_iota == d1, w1, jnp.where(r_iota == d2, w2, 0.0)).astype(BF16)
            acc = acc + _dot(unsort, yl_ref[c * SORT_CHUNK:(c + 1) * SORT_CHUNK, :])
        y_ref[rows, :] = _rmsnorm(acc, gf_ref[...])


def _ffn_schedule(tab):
    n_tok_tiles = tab.shape[0]
    strips = (tab[:, :, 0] * (1.0 / ROW_GRANULE)).astype(jnp.int32)
    starts = (tab[:, :, 1] * (1.0 / ROW_GRANULE)).astype(jnp.int32)
    cnt = strips.T
    row0 = starts.T
    cs = jnp.cumsum(cnt, axis=1) - cnt
    n_str = jnp.sum(cnt, axis=1)
    np_str = (n_str + FFN_SLOTS - 1) // FFN_SLOTS * FFN_SLOTS
    ends = jnp.cumsum(np_str)
    base = ends - np_str
    n_steps_max = _ffn_steps_max(n_tok_tiles) + GATHER_BUFS - 1
    step0 = jnp.arange(n_steps_max, dtype=jnp.int32) * FFN_SLOTS
    stream = jnp.minimum(jnp.sum(ends[None, :] <= step0[:, None], axis=1), N_EXPERTS - 1)
    pick = stream[:, None] == jnp.arange(N_EXPERTS, dtype=jnp.int32)[None, :]
    sel = lambda x: jnp.sum(jnp.where(pick[:, :, None], x[None], 0), axis=1)
    cs_j, cnt_j, row0_j = sel(cs), sel(cnt), sel(row0)
    base_j = jnp.sum(jnp.where(pick, base[None, :], 0), axis=1)
    q = step0[:, None] + jnp.arange(FFN_SLOTS, dtype=jnp.int32)[None, :] - base_j[:, None]
    reached = cs_j[:, None, :] <= q[:, :, None]
    last = lambda x: jnp.sum(jnp.where(reached, jnp.diff(x, axis=1, prepend=0)[:, None, :], 0), axis=2)
    tile_idx = jnp.sum(reached, axis=2).astype(jnp.int32) - 1
    g = q - last(cs_j)
    valid = g < last(cnt_j)
    granule = tile_idx * GRANULES_PER_TILE + last(row0_j) + g
    assert n_tok_tiles >= 2 * FFN_SLOTS
    slot = jnp.arange(FFN_SLOTS, dtype=jnp.int32)[None, :]
    parity = jnp.arange(n_steps_max, dtype=jnp.int32)[:, None] % 2
    pad_dst = lambda par: (slot + par * FFN_SLOTS) * GRANULES_PER_TILE + GRANULES_PER_TILE - 1
    gsrc = jnp.where(valid, granule, slot * GRANULES_PER_TILE + GRANULES_PER_TILE - 2).astype(jnp.int32)
    sdst = jnp.where(valid, granule, pad_dst(parity))
    sdst = jnp.concatenate([pad_dst(1), sdst], axis=0)
    texp = stream.astype(jnp.int32)
    n_steps = (ends[-1] // FFN_SLOTS).astype(jnp.int32).reshape(1)
    return texp, gsrc.reshape(-1), sdst.astype(jnp.int32).reshape(-1), n_steps


assert GRANULES_PER_TILE - (2 * TOK_TILE + N_EXPERTS * (ROW_GRANULE - 1)) // ROW_GRANULE >= 2


def _ffn_steps_max(n_tok_tiles):
    return -(-(n_tok_tiles * GRANULES_PER_TILE + N_EXPERTS * (FFN_SLOTS - 1)) // FFN_SLOTS)


def _vmem_limit(mib):
    n_bytes = mib * 1024 * 1024
    assert n_bytes < V7X_VMEM_BYTES
    return n_bytes


def _rope_tables(pos):
    inv_freq = ROPE_THETA ** (-jnp.arange(HALF, dtype=F32) * 2.0 / HEAD_DIM)
    ang = pos.astype(F32)[:, None] * inv_freq[None, :]
    cos, sin = jnp.cos(ang), jnp.sin(ang)
    reps = LANES // HEAD_DIM
    return jnp.tile(jnp.concatenate([cos, cos], axis=1), (1, reps)), jnp.tile(jnp.concatenate([-sin, sin], axis=1), (1, reps))


def _moe_combine(xmid, yl, meta, gf, n_tok, first_tile, n_tiles):
    return pl.pallas_call(
        functools.partial(_combine_kernel, first_tile=first_tile, n_tiles=n_tiles),
        grid=(n_tiles,),
        in_specs=[
            pl.BlockSpec((n_tok, D_MODEL), lambda i: (i, 0)),
            pl.BlockSpec(memory_space=pl.ANY),
            pl.BlockSpec((None, 8, n_tok), lambda i: (i, 0, 0)),
            pl.BlockSpec((1, D_MODEL), lambda i: (0, 0)),
        ],
        out_specs=pl.BlockSpec((n_tok, D_MODEL), lambda i: (i, 0)),
        out_shape=jax.ShapeDtypeStruct((n_tiles * n_tok, D_MODEL), F32),
        scratch_shapes=[pltpu.VMEM((COMBINE_BUFS, SORT_ROWS, D_MODEL), BF16), pltpu.SemaphoreType.DMA((COMBINE_BUFS,))],
        compiler_params=pltpu.CompilerParams(
            dimension_semantics=("arbitrary",),
            vmem_limit_bytes=_vmem_limit(40)),
        name="moe_combine",
    )(xmid, yl, meta, gf)


def kernel(x_prompt, x_sample, cache_swa_k, cache_swa_v, norm_mix_g, w_in, attn_sinks, gmlp_ln_g, gmlp_ln_b,
           gmlp_w_s, gmlp_b_s, w_out, norm_ffn_g, router_group_w, router_group_b, router_expert_w,
           router_expert_b, expert_w_gate, expert_w_up, expert_w_down, final_norm_g):
    assert norm_mix_g.shape[0] == 1, "single-layer trunk"
    batch, seq, _ = x_prompt.shape
    dec_batch = x_sample.shape[0]
    assert x_sample.shape[1] == 1 and seq % TOK_TILE == 0

    win_ext = w_in[0].astype(BF16)
    wout = w_out[0].astype(BF16)
    g1 = norm_mix_g[0][None, :]
    g2 = norm_ffn_g[0][None, :]
    gf = final_norm_g[None, :]
    lng = gmlp_ln_g[0][None, :]
    lnb = gmlp_ln_b[0][None, :]
    sinks = attn_sinks[0]
    ws = gmlp_w_s[0]
    group_dim = GMLP_WIDTH // GMLP_GROUPS
    bsf = jnp.repeat(gmlp_b_s[0].T, group_dim, axis=1)
    ws0 = jnp.repeat(ws[:, 0, 0], group_dim)[None, :]
    bs0 = jnp.repeat(gmlp_b_s[0][:, 0], group_dim)[None, :]
    wrt = jnp.zeros((ROUTER_ROWS, D_MODEL), F32)
    wrt = wrt.at[0:N_GROUPS].set(router_group_w[0].T)
    wrt = wrt.at[8:8 + N_EXPERTS].set(router_expert_w[0].reshape(D_MODEL, N_EXPERTS).T).astype(BF16)
    brt = jnp.full((ROUTER_ROWS, 1), NEG_INF, F32)
    brt = brt.at[0:N_GROUPS, 0].set(router_group_b[0])
    brt = brt.at[8:8 + N_EXPERTS, 0].set(router_expert_b[0].reshape(N_EXPERTS))
    wg_f = expert_w_gate[0].reshape(N_EXPERTS * D_MODEL, D_EXPERT)
    wu_f = expert_w_up[0].reshape(N_EXPERTS * D_MODEL, D_EXPERT)
    wd_f = expert_w_down[0].reshape(N_EXPERTS * D_EXPERT, D_MODEL)
    cos_p, sin_p = _rope_tables(jnp.arange(seq, dtype=jnp.int32))
    cos_s, sin_s = _rope_tables(PAST_LEN + jnp.arange(1, dtype=jnp.int32))

    full = lambda shape: pl.BlockSpec(shape, lambda *_: (0,) * len(shape))
    smem = pl.BlockSpec(memory_space=pltpu.SMEM)
    n_tiles = seq // TOK_TILE

    upper = jnp.triu(jnp.ones((TOK_TILE, TOK_TILE), BF16), k=1)
    lpad = (jnp.arange(LANES)[None, :] < jnp.arange(N_EXPERTS)[:, None]).astype(BF16)
    n_tok_tiles = batch * n_tiles

    x2d = x_prompt.reshape(batch * seq, D_MODEL)
    tile_a = lambda s: jnp.minimum(s, n_tok_tiles - 1)
    cur = lambda s: jnp.maximum(s - 1, 0)
    tile_b = cur
    gu_rows, dn_rows = wg_f.shape[0] // n_tok_tiles, wd_f.shape[0] // n_tok_tiles
    assert gu_rows * n_tok_tiles == wg_f.shape[0] and dn_rows * n_tok_tiles == wd_f.shape[0] and dn_rows % 16 == 0
    xmid_p, xl_p, meta_p, tab_p, k_p, v_p, wg, wu, wd = pl.pallas_call(
        functools.partial(_prompt_mixer_kernel, tiles_per_seq=n_tiles),
        grid=(n_tok_tiles + 1,),
        in_specs=[
            smem,
            pl.BlockSpec((TOK_TILE, D_MODEL), lambda s: (tile_a(s), 0)),
            pl.BlockSpec((TOK_TILE, D_MODEL), lambda s: (cur(s), 0)),
            full((1, D_MODEL)),
            full((D_MODEL, IN_WIDTH)),
            pl.BlockSpec((TOK_TILE, LANES), lambda s: (tile_b(s) % n_tiles, 0)),
            pl.BlockSpec((TOK_TILE, LANES), lambda s: (tile_b(s) % n_tiles, 0)),
            full((1, GMLP_WIDTH)),
            full((1, GMLP_WIDTH)),
            full((GMLP_GROUPS, CHUNK, CHUNK)),
            full((CHUNK, GMLP_WIDTH)),
            full((D_MODEL, D_MODEL)),
            full((1, D_MODEL)),
            full((ROUTER_ROWS, D_MODEL)),
            full((ROUTER_ROWS, 1)),
            full((TOK_TILE, TOK_TILE)),
            full((N_EXPERTS, LANES)),
            pl.BlockSpec((gu_rows, D_EXPERT), lambda s: (tile_a(s), 0)),
            pl.BlockSpec((gu_rows, D_EXPERT), lambda s: (tile_a(s), 0)),
            pl.BlockSpec((dn_rows, D_MODEL), lambda s: (tile_a(s), 0)),
        ],
        out_specs=[
            pl.BlockSpec((TOK_TILE, D_MODEL), lambda s: (cur(s), 0)),
            pl.BlockSpec((None, SORT_ROWS, D_MODEL), lambda s: (jnp.where(s == 0, n_tok_tiles, s - 1), 0, 0)),
            pl.BlockSpec((None, 8, TOK_TILE), lambda s: (cur(s), 0, 0)),
            pl.BlockSpec((None, N_EXPERTS, LANES), lambda s: (cur(s), 0, 0)),
            pl.BlockSpec((None, WINDOW, KV_WIDTH), lambda s: (tile_b(s) // n_tiles, 0, 0)),
            pl.BlockSpec((None, WINDOW, KV_WIDTH), lambda s: (tile_b(s) // n_tiles, 0, 0)),
            pl.BlockSpec((gu_rows, D_EXPERT), lambda s: (tile_a(s), 0)),
            pl.BlockSpec((gu_rows, D_EXPERT), lambda s: (tile_a(s), 0)),
            pl.BlockSpec((dn_rows, D_MODEL), lambda s: (tile_a(s), 0)),
        ],
        out_shape=[
            jax.ShapeDtypeStruct((batch * seq, D_MODEL), F32),
            jax.ShapeDtypeStruct((n_tok_tiles + 1, SORT_ROWS, D_MODEL), BF16),
            jax.ShapeDtypeStruct((n_tok_tiles, 8, TOK_TILE), F32),
            jax.ShapeDtypeStruct((n_tok_tiles, N_EXPERTS, LANES), F32),
            jax.ShapeDtypeStruct((batch, WINDOW, KV_WIDTH), F32),
            jax.ShapeDtypeStruct((batch, WINDOW, KV_WIDTH), F32),
            jax.ShapeDtypeStruct(wg_f.shape, BF16),
            jax.ShapeDtypeStruct(wu_f.shape, BF16),
            jax.ShapeDtypeStruct(wd_f.shape, BF16),
        ],
        scratch_shapes=[
            pltpu.VMEM((TOK_TILE, IN_WIDTH), F32),
            pltpu.VMEM((TOK_TILE, IN_WIDTH), F32),
            pltpu.VMEM((TOK_TILE, D_MODEL), BF16),
        ] + [pltpu.VMEM((WINDOW + TOK_TILE, KV_WIDTH), BF16)] * 4 + [
            pltpu.VMEM((len(_NAT_HEADS), TOK_TILE, LANES), BF16),
            pltpu.VMEM((len(_SWP_HEADS), TOK_TILE, LANES), BF16),
            pltpu.VMEM((2 * BLOCKS_PER_TILE, len(_NAT_HEADS) * WINDOW, 2 * WINDOW), F32),
            pltpu.VMEM((2 * BLOCKS_PER_TILE, len(_NAT_HEADS) * WINDOW, 2 * WINDOW), BF16),
            pltpu.VMEM((TOK_TILE, GMLP_WIDTH), BF16),
            pltpu.VMEM((TOK_TILE, D_MODEL), BF16),
            pltpu.VMEM((TOK_TILE, D_MODEL), BF16),
        ],
        compiler_params=pltpu.CompilerParams(
            dimension_semantics=("arbitrary",),
            vmem_limit_bytes=_vmem_limit(56)),
        name="prompt_mixer",
    )(sinks, x2d, x2d, g1, win_ext, cos_p, sin_p, lng, lnb, ws, bsf, wout, g2, wrt, brt, upper, lpad,
      wg_f, wu_f, wd_f)
    wg = wg.reshape(N_EXPERTS, D_MODEL, D_EXPERT)
    wu = wu.reshape(N_EXPERTS, D_MODEL, D_EXPERT)
    wd = wd.reshape(N_EXPERTS, D_EXPERT, D_MODEL)

    xs = x_sample.reshape(dec_batch, D_MODEL)
    ck = cache_swa_k[0].reshape(dec_batch, WINDOW, KV_WIDTH)
    cv = cache_swa_v[0].reshape(dec_batch, WINDOW, KV_WIDTH)
    vmem = pl.BlockSpec(memory_space=pltpu.VMEM)
    hbm = pl.BlockSpec(memory_space=pl.ANY)
    xmid_s, meta_s, tab_s, k_s, v_s, vn_s, xl_all = pl.pallas_call(
        _sample_mixer_kernel,
        in_specs=[smem] + [vmem] * 17 + [hbm],
        out_specs=[vmem] * 6 + [hbm],
        out_shape=[
            jax.ShapeDtypeStruct((dec_batch, D_MODEL), F32),
            jax.ShapeDtypeStruct((8, dec_batch), F32),
            jax.ShapeDtypeStruct((N_EXPERTS, LANES), F32),
            jax.ShapeDtypeStruct((dec_batch, KV_WIDTH), F32),
            jax.ShapeDtypeStruct((dec_batch, KV_WIDTH), F32),
            jax.ShapeDtypeStruct((dec_batch, GMLP_WIDTH), F32),
            jax.ShapeDtypeStruct(xl_p.shape, BF16),
        ],
        scratch_shapes=[pltpu.VMEM((dec_batch, D_MODEL), BF16), pltpu.VMEM((SORT_ROWS, D_MODEL), BF16),
                        pltpu.SemaphoreType.DMA((1,))],
        compiler_params=pltpu.CompilerParams(vmem_limit_bytes=_vmem_limit(56)),
        input_output_aliases={18: 6},
        name="sample_mixer",
    )(sinks, xs, ck, cv, g1, win_ext, cos_s, sin_s, lng, lnb, ws0, bs0, wout, g2, wrt, brt,
      upper[:dec_batch, :dec_batch], lpad, xl_p)

    n_all_tiles = n_tok_tiles + 1
    tab_all = jnp.concatenate([tab_p, tab_s[None]], axis=0)
    texp, gsrc, sdst, n_steps = _ffn_schedule(tab_all)
    yl_all = pl.pallas_call(
        _grouped_ffn_kernel,
        grid_spec=pltpu.PrefetchScalarGridSpec(
            num_scalar_prefetch=4,
            grid=(_ffn_steps_max(n_all_tiles),),
            in_specs=[
                pl.BlockSpec(memory_space=pl.ANY),
                pl.BlockSpec((None, D_MODEL, D_EXPERT), lambda j, te, gs, sd, ns: (te[j], 0, 0)),
                pl.BlockSpec((None, D_MODEL, D_EXPERT), lambda j, te, gs, sd, ns: (te[j], 0, 0)),
                pl.BlockSpec((None, D_EXPERT, D_MODEL), lambda j, te, gs, sd, ns: (te[j], 0, 0)),
            ],
            out_specs=pl.BlockSpec(memory_space=pl.ANY),
            scratch_shapes=[
                pltpu.VMEM((GATHER_BUFS, FFN_SLOTS, ROW_GRANULE, D_MODEL), BF16),
                pltpu.VMEM((SCATTER_BUFS, FFN_SLOTS, ROW_GRANULE, D_MODEL), BF16),
                pltpu.SemaphoreType.DMA((GATHER_BUFS,)),
                pltpu.SemaphoreType.DMA((SCATTER_BUFS,)),
            ],
        ),
        out_shape=jax.ShapeDtypeStruct((n_all_tiles * GRANULES_PER_TILE, ROW_GRANULE, D_MODEL), BF16),
        compiler_params=pltpu.CompilerParams(
            dimension_semantics=("arbitrary",),
            vmem_limit_bytes=_vmem_limit(32)),
        input_output_aliases={4: 0},
        name="grouped_ffn",
    )(texp, gsrc, sdst, n_steps, xl_all.reshape(n_all_tiles * GRANULES_PER_TILE, ROW_GRANULE, D_MODEL), wg, wu, wd)
    yl_all = yl_all.reshape(n_all_tiles * SORT_ROWS, D_MODEL)

    y_p = _moe_combine(xmid_p, yl_all, meta_p, gf, TOK_TILE, 0, n_tok_tiles)
    y_s = _moe_combine(xmid_s, yl_all, meta_s[None], gf, dec_batch, n_tok_tiles, 1)

    return (y_p.reshape(batch, seq, D_MODEL),
            y_s.reshape(dec_batch, 1, D_MODEL),
            k_p.reshape(1, batch, WINDOW, N_KV_HEADS, HEAD_DIM),
            v_p.reshape(1, batch, WINDOW, N_KV_HEADS, HEAD_DIM),
            k_s.reshape(1, dec_batch, 1, N_KV_HEADS, HEAD_DIM),
            v_s.reshape(1, dec_batch, 1, N_KV_HEADS, HEAD_DIM),
            vn_s.reshape(1, dec_batch, 1, GMLP_WIDTH))
```

```python
import functools

import jax
import jax.numpy as jnp
import numpy as np
from jax import lax
from jax.experimental import pallas as pl
from jax.experimental.pallas import tpu as pltpu

F32 = jnp.float32
BF16 = jnp.bfloat16

D_MODEL = 1024
HEAD_DIM = 64
HALF = HEAD_DIM // 2
N_Q_HEADS = 8
N_KV_HEADS = 2
ATTN_WIDTH = N_Q_HEADS * HEAD_DIM
KV_WIDTH = N_KV_HEADS * HEAD_DIM
WINDOW = 128
ROPE_THETA = 10000.0
GMLP_WIDTH = D_MODEL - ATTN_WIDTH
GMLP_GROUPS = 8
CHUNK = 128
N_GROUPS = 4
EXPERTS_PER_GROUP = 8
N_EXPERTS = N_GROUPS * EXPERTS_PER_GROUP
D_EXPERT = 256
EPS = 1e-6
NEG_INF = -1e30
PAST_LEN = 16384

LANES = 128
V7X_VMEM_BYTES = 64 * 1024 * 1024

C_Q = 0
C_K = C_Q + ATTN_WIDTH
C_V = C_K + KV_WIDTH
C_U = C_V + KV_WIDTH
C_VG = C_U + GMLP_WIDTH
IN_WIDTH = C_VG + GMLP_WIDTH

ROUTER_ROWS = 48
TOK_TILE = 512
BLOCKS_PER_TILE = TOK_TILE // WINDOW
ROW_GRANULE = 16
SORT_CHUNK = 512
SORT_ROWS = -(-(2 * TOK_TILE + N_EXPERTS * (ROW_GRANULE - 1)) // SORT_CHUNK) * SORT_CHUNK
GRANULES_PER_TILE = SORT_ROWS // ROW_GRANULE
FFN_ROWS = 512
FFN_SLOTS = FFN_ROWS // ROW_GRANULE
GATHER_BUFS = 3
SCATTER_BUFS = 3
COMBINE_BUFS = 3


def _dot(a, b):
    return jnp.dot(a, b, preferred_element_type=F32)


def _dot_nt(a, b):
    return lax.dot_general(a, b, (((1,), (1,)), ((), ())), preferred_element_type=F32)


def _gelu(x):
    return 0.5 * x * (1.0 + lax.erf(x * np.float32(np.sqrt(0.5))))


def _rmsnorm(x, g):
    return x * lax.rsqrt(jnp.mean(x * x, axis=-1, keepdims=True) + EPS) * g


def _layernorm(x, g, b):
    mu = jnp.mean(x, axis=-1, keepdims=True)
    xc = x - mu
    return xc * lax.rsqrt(jnp.mean(xc * xc, axis=-1, keepdims=True) + EPS) * g + b


def _first_argmax_rows(x, row_iota, n_rows):
    m = jnp.max(x, axis=0, keepdims=True)
    idx = jnp.min(jnp.where(x == m, row_iota, n_rows), axis=0, keepdims=True)
    return m, idx


def _route(h2b, wrt_ref, brt_ref):
    n_tok = h2b.shape[0]
    lt = _dot_nt(wrt_ref[...], h2b) + brt_ref[...]
    row8 = lax.broadcasted_iota(jnp.int32, (8, n_tok), 0).astype(F32)
    glog = lt[0:8]
    gmax, gidx = _first_argmax_rows(glog, row8, 8)
    g_w = 1.0 / jnp.sum(jnp.exp(glog - gmax), axis=0, keepdims=True)
    esel = lt[8:16]
    for g in range(1, N_GROUPS):
        esel = jnp.where(gidx == g, lt[8 + 8 * g:16 + 8 * g], esel)
    m1, i1 = _first_argmax_rows(esel, row8, 8)
    esel2 = jnp.where(row8 == i1, -jnp.inf, esel)
    m2, i2 = _first_argmax_rows(esel2, row8, 8)
    r = jnp.exp(m2 - m1)
    w1 = 1.0 / (1.0 + r)
    w2 = r / (1.0 + r)
    e1 = gidx * EXPERTS_PER_GROUP + i1
    e2 = gidx * EXPERTS_PER_GROUP + i2
    slab = jnp.where(row8 == 0, e1, 0.0)
    for r_idx, val in ((1, e2), (2, w1 * g_w), (3, w2 * g_w)):
        slab = jnp.where(row8 == r_idx, val, slab)
    return slab


def _softmax_with_sink(s, sink):
    m = jnp.maximum(jnp.max(s, axis=-1, keepdims=True), sink)
    e = jnp.exp(s - m)
    den = jnp.sum(e, axis=-1, keepdims=True) + jnp.exp(sink - m)
    return e * (1.0 / den)


def _rope(x, cos, sin_signed):
    first_half = (lax.broadcasted_iota(jnp.int32, x.shape, 1) & HALF) == 0
    partner = jnp.where(first_half, pltpu.roll(x, LANES - HALF, 1), pltpu.roll(x, HALF, 1))
    return x * cos + partner * sin_signed


_NAT_HEADS = (0, 2, 5, 7)
_SWP_HEADS = (1, 3, 4, 6)


def _prompt_mixer_kernel(sinks_ref, x_next_ref, x_ref, g1_ref, win_ref, cos_ref, sin_ref, lng_ref, lnb_ref, ws_ref,
                         bsf_ref, wout_ref, g2_ref, wrt_ref, brt_ref, upper_ref, lpad_ref, wgf_ref, wuf_ref, wdf_ref,
                         xmid_ref, xl_ref, meta_ref, tab_ref, kout_ref, vout_ref, wgo_ref, wuo_ref, wdo_ref,
                         z_a, z_b, mix_ref, k_n, k_s, v_n, v_s, q_nat, q_swp, s_ref, p_ref, vn_ref, h_ref, h2_ref,
                         *, tiles_per_seq, n_tok_tiles):
    step = pl.program_id(0)
    t = jnp.maximum(step - 1, 0) % tiles_per_seq
    kv_bufs = (k_n, k_s, v_n, v_s)

    for src, dst in ((wgf_ref, wgo_ref), (wuf_ref, wuo_ref), (wdf_ref, wdo_ref)):
        dst[...] = src[...].astype(BF16)

    @pl.when(t == 0)
    def _():
        for ref in kv_bufs:
            ref[0:WINDOW, :] = jnp.zeros((WINDOW, KV_WIDTH), BF16)

    args = (t, sinks_ref, x_next_ref, x_ref, g1_ref, win_ref, cos_ref, sin_ref, lng_ref, lnb_ref, ws_ref, bsf_ref,
            wout_ref, g2_ref, wrt_ref, brt_ref, upper_ref, lpad_ref, xmid_ref, xl_ref, meta_ref, tab_ref, kout_ref,
            vout_ref, kv_bufs, q_nat, q_swp, s_ref, p_ref, vn_ref, h_ref, h2_ref)

    z_bufs = (z_a, z_b)
    middle = jnp.logical_and(step > 0, step < n_tok_tiles)

    @pl.when(step == 0)
    def _():
        z_a[...] = _dot(_rmsnorm(x_next_ref[...], g1_ref[...]).astype(BF16), win_ref[...])
        xl_ref[...] = jnp.zeros_like(xl_ref)

    for parity in range(2):
        @pl.when(jnp.logical_and(middle, step % 2 == parity))
        def _(parity=parity):
            _prompt_mixer_body(z_bufs[parity], z_bufs[1 - parity], mix_ref, *args)

    @pl.when(step == n_tok_tiles)
    def _():
        last = n_tok_tiles % 2
        _prompt_mixer_body(z_bufs[last], z_bufs[1 - last], mix_ref, *args, do_project=False)


def _prompt_mixer_body(z_next, z_ref, mix_ref, t, sinks_ref, x_next_ref, x_ref, g1_ref, win_ref, cos_ref,
                       sin_ref, lng_ref, lnb_ref, ws_ref, bsf_ref, wout_ref, g2_ref, wrt_ref, brt_ref, upper_ref,
                       lpad_ref, xmid_ref, xl_ref, meta_ref, tab_ref, kout_ref, vout_ref, kv_bufs,
                       q_nat, q_swp, s_ref, p_ref, vn_ref, h_ref, h2_ref, do_project=True):
    k_n, k_s, v_n, v_s = kv_bufs
    cos = cos_ref[...]
    sin = sin_ref[...]
    lane = lax.broadcasted_iota(jnp.int32, (WINDOW, LANES), 1)
    lo = lane < HEAD_DIM
    row = lax.broadcasted_iota(jnp.int32, (WINDOW, WINDOW), 0)
    col = lax.broadcasted_iota(jnp.int32, (WINDOW, WINDOW), 1)
    mask_cur = col <= row
    mask_prev_band = col >= row
    mask_prev_first = jnp.logical_and(mask_prev_band, (jnp.zeros_like(row) + t) > 0)
    mask_band = jnp.concatenate([mask_prev_band, mask_cur], axis=1)
    mask_first = jnp.concatenate([mask_prev_first, mask_cur], axis=1)

    cq = cos * np.float32(HEAD_DIM ** -0.5)
    sq = sin * np.float32(HEAD_DIM ** -0.5)
    kf = _rope(z_ref[:, C_K:C_K + KV_WIDTH], cos, sin)
    vf = z_ref[:, C_V:C_V + KV_WIDTH]
    k_n[WINDOW:, :] = kf.astype(BF16)
    k_s[WINDOW:, :] = pltpu.roll(kf, HEAD_DIM, 1).astype(BF16)
    v_n[WINDOW:, :] = vf.astype(BF16)
    v_s[WINDOW:, :] = pltpu.roll(vf, HEAD_DIM, 1).astype(BF16)

    kout_ref[...] = kf[TOK_TILE - WINDOW:]
    vout_ref[...] = vf[TOK_TILE - WINDOW:]

    lo_t = lax.broadcasted_iota(jnp.int32, (TOK_TILE, LANES), 1) < HEAD_DIM
    for m in range(N_Q_HEADS // 2):
        qc = _rope(z_ref[:, C_Q + m * LANES:C_Q + (m + 1) * LANES], cq, sq)
        for hd, qh in ((2 * m, jnp.where(lo_t, qc, 0.0)), (2 * m + 1, jnp.where(lo_t, 0.0, qc))):
            if hd in _NAT_HEADS:
                q_nat[_NAT_HEADS.index(hd)] = qh.astype(BF16)
            else:
                q_swp[_SWP_HEADS.index(hd)] = qh.astype(BF16)

    stacks = ((q_nat, _NAT_HEADS, k_n, v_n), (q_swp, _SWP_HEADS, k_s, v_s))
    proj_cols = 2 * LANES

    def project_next(c):
        if do_project:
            cols = slice(c * proj_cols, (c + 1) * proj_cols)
            z_next[:, cols] = _dot(h_ref[...], win_ref[:, cols])

    def scores(j):
        rows = slice(j * WINDOW, (j + 1) * WINDOW)
        keys = slice(j * WINDOW, (j + 2) * WINDOW)
        for si, (q_ref, _, k_buf, _) in enumerate(stacks):
            q_stack = jnp.concatenate([q_ref[i, rows, :] for i in range(len(_NAT_HEADS))], axis=0)
            s_ref[2 * j + si] = _dot_nt(q_stack, k_buf[keys, :])

    def softmax(j):
        mask = mask_first if j == 0 else mask_band
        for si, (_, heads, _, _) in enumerate(stacks):
            for i, hd in enumerate(heads):
                pr = slice(i * WINDOW, (i + 1) * WINDOW)
                p_ref[2 * j + si, pr, :] = _softmax_with_sink(
                    jnp.where(mask, s_ref[2 * j + si, pr, :], NEG_INF), sinks_ref[hd]).astype(BF16)

    def attend(j):
        rows = slice(j * WINDOW, (j + 1) * WINDOW)
        keys = slice(j * WINDOW, (j + 2) * WINDOW)
        o_nat = _dot(p_ref[2 * j], v_n[keys, :])
        o_swp = _dot(p_ref[2 * j + 1], v_s[keys, :])
        for m in range(N_Q_HEADS // 2):
            pr = slice(m * WINDOW, (m + 1) * WINDOW)
            even_nat = (2 * m) in _NAT_HEADS
            att = jnp.where(lo, o_nat[pr], o_swp[pr]) if even_nat else jnp.where(lo, o_swp[pr], o_nat[pr])
            mix_ref[rows, m * LANES:(m + 1) * LANES] = att.astype(BF16)

    def gmlp(m):
        cs = slice(m * LANES, (m + 1) * LANES)
        w0 = jnp.where(mask_cur, ws_ref[2 * m], 0.0).astype(BF16)
        w1 = jnp.where(mask_cur, ws_ref[2 * m + 1], 0.0).astype(BF16)
        wcat = jnp.concatenate([w0, w1], axis=1)
        for j in range(BLOCKS_PER_TILE):
            rows = slice(j * WINDOW, (j + 1) * WINDOW)
            vcol = vn_ref[rows, cs]
            rhs = jnp.concatenate([jnp.where(lo, vcol, jnp.zeros_like(vcol)),
                                   jnp.where(lo, jnp.zeros_like(vcol), vcol)], axis=0)
            sp = _dot(wcat, rhs) + bsf_ref[:, cs]
            u = _gelu(z_ref[rows, C_U + m * LANES:C_U + (m + 1) * LANES])
            mix_ref[rows, ATTN_WIDTH + m * LANES:ATTN_WIDTH + (m + 1) * LANES] = (u * sp).astype(BF16)

    def project_out(c):
        cols = slice(c * proj_cols, (c + 1) * proj_cols)
        xmid_ref[:, cols] = x_ref[:, cols] + _dot(mix_ref[...], wout_ref[:, cols])

    if do_project:
        h_ref[...] = _rmsnorm(x_next_ref[...], g1_ref[...]).astype(BF16)
    project_next(0)
    project_next(1)
    for j in range(BLOCKS_PER_TILE):
        scores(j)
    vn_ref[...] = _layernorm(_gelu(z_ref[:, C_VG:C_VG + GMLP_WIDTH]), lng_ref[...], lnb_ref[...]).astype(BF16)
    project_next(2)
    softmax(0)
    project_next(3)
    softmax(1)
    project_next(4)
    softmax(2)
    project_next(5)
    softmax(3)
    project_next(6)
    for j in range(BLOCKS_PER_TILE):
        attend(j)
    for m in range(GMLP_GROUPS // 2):
        gmlp(m)
    for c in range(D_MODEL // proj_cols):
        project_out(c)
    h2_ref[...] = _rmsnorm(xmid_ref[...], g2_ref[...]).astype(BF16)
    d1, d2 = _sort_plan(_route(h2_ref[...], wrt_ref, brt_ref), upper_ref, lpad_ref, meta_ref, tab_ref)
    _sort_rows(d1, d2, h2_ref[...], xl_ref)

    for ref in kv_bufs:
        ref[0:WINDOW, :] = ref[TOK_TILE:TOK_TILE + WINDOW, :]


def _sort_plan(slab, upper_ref, lpad_ref, meta_ref, tab_ref):
    n_tok = slab.shape[1]
    e1, e2 = slab[0:1], slab[1:2]
    row32 = lax.broadcasted_iota(jnp.int32, (N_EXPERTS, n_tok), 0).astype(F32)
    sel1 = row32 == e1
    sel2 = row32 == e2
    onehot = jnp.where(sel1, 1.0, jnp.where(sel2, 1.0, 0.0))
    earlier = _dot(onehot.astype(BF16), upper_ref[...])
    cnt = jnp.sum(onehot, axis=1, keepdims=True)
    pc = jnp.floor((cnt + (ROW_GRANULE - 1)) * (1.0 / ROW_GRANULE)) * ROW_GRANULE
    pc_b = jnp.broadcast_to(pc, (N_EXPERTS, LANES))
    pc_pad = jnp.concatenate([pc_b, jnp.zeros((LANES - N_EXPERTS, LANES), F32)], axis=0).astype(BF16)
    start = _dot(lpad_ref[...], pc_pad)
    base = start[:, 0:1] + earlier
    d1 = jnp.sum(jnp.where(sel1, base, 0.0), axis=0, keepdims=True)
    d2 = jnp.sum(jnp.where(sel2, base, 0.0), axis=0, keepdims=True)
    row8 = lax.broadcasted_iota(jnp.int32, (8, n_tok), 0)
    meta_ref[...] = jnp.where(row8 == 0, d1, jnp.where(row8 == 1, d2, jnp.where(row8 >= 4, 0.0, slab)))
    lane = lax.broadcasted_iota(jnp.int32, (N_EXPERTS, LANES), 1)
    tab_ref[...] = jnp.where(lane == 0, pc_b, jnp.where(lane == 1, start, 0.0))
    return d1, d2


def _sort_rows(d1, d2, h2b, xl_ref):
    n_tok = h2b.shape[0]
    for c in range(SORT_ROWS // SORT_CHUNK):
        r_iota = (lax.broadcasted_iota(jnp.int32, (SORT_CHUNK, n_tok), 0) + c * SORT_CHUNK).astype(F32)
        perm = jnp.where(r_iota == d1, 1.0, jnp.where(r_iota == d2, 1.0, 0.0)).astype(BF16)
        xl_ref[c * SORT_CHUNK:(c + 1) * SORT_CHUNK, :] = _dot(perm, h2b).astype(BF16)


def _sample_mixer_kernel(sinks_ref, x_ref, ck_ref, cv_ref, g1_ref, win_ref, cos_ref, sin_ref, lng_ref, lnb_ref,
                         ws0_ref, bs0_ref, wout_ref, g2_ref, wrt_ref, brt_ref, upper_ref, lpad_ref, xl_in_hbm,
                         xmid_ref, meta_ref, tab_ref, kout_ref, vout_ref, vnout_ref, xl_hbm,
                         mix_ref, xl_tile, xl_sem):
    del xl_in_hbm
    n_seq = x_ref.shape[0]
    seq_chunk = 16
    x = x_ref[...]
    h = _rmsnorm(x, g1_ref[...]).astype(BF16)
    z = _dot(h, win_ref[...])
    cos = cos_ref[...]
    sin = sin_ref[...]
    scale = np.float32(HEAD_DIM ** -0.5)
    lane = lax.broadcasted_iota(jnp.int32, (n_seq, LANES), 1)
    lo = lane < HEAD_DIM
    kf = _rope(z[:, C_K:C_K + KV_WIDTH], cos, sin)
    vf = z[:, C_V:C_V + KV_WIDTH]
    kout_ref[...] = kf
    vout_ref[...] = vf
    kb = kf.astype(BF16).astype(F32)
    vb = vf.astype(BF16).astype(F32)

    q_heads = []
    for hd in range(N_Q_HEADS):
        m = hd // 2
        qc = _rope(z[:, C_Q + m * LANES:C_Q + (m + 1) * LANES], cos, sin) * scale
        keep = lo if hd % 2 == 0 else ~lo
        qm = jnp.where(keep, qc, 0.0)
        if (hd % 2) != (hd // (N_Q_HEADS // N_KV_HEADS)):
            qm = pltpu.roll(qm, HEAD_DIM, 1)
        q_heads.append(qm.astype(BF16))

    s_new = [jnp.sum(q_heads[hd].astype(F32) * kb, axis=-1, keepdims=True) for hd in range(N_Q_HEADS)]

    rr = lax.broadcasted_iota(jnp.int32, (N_Q_HEADS * seq_chunk, seq_chunk * WINDOW), 0)
    cc = lax.broadcasted_iota(jnp.int32, (N_Q_HEADS * seq_chunk, seq_chunk * WINDOW), 1)
    same_seq = (rr % seq_chunk) == (cc // WINDOW)
    kv_lo = lax.broadcasted_iota(jnp.int32, (seq_chunk, LANES), 1) < HEAD_DIM

    for c in range(n_seq // seq_chunk):
        sr = slice(c * seq_chunk, (c + 1) * seq_chunk)
        kc = ck_ref[sr].reshape(seq_chunk * WINDOW, KV_WIDTH).astype(BF16)
        vc = cv_ref[sr].reshape(seq_chunk * WINDOW, KV_WIDTH).astype(BF16)
        qs = jnp.concatenate([q_heads[hd][sr] for hd in range(N_Q_HEADS)], axis=0)
        s = jnp.where(same_seq, _dot_nt(qs, kc), NEG_INF)
        sn = jnp.concatenate([s_new[hd][sr] for hd in range(N_Q_HEADS)], axis=0)
        sink = jnp.concatenate([jnp.full((seq_chunk, 1), sinks_ref[hd], F32) for hd in range(N_Q_HEADS)], axis=0)
        m = jnp.maximum(jnp.maximum(jnp.max(s, axis=-1, keepdims=True), sn), sink)
        e = jnp.exp(s - m)
        en = jnp.exp(sn - m)
        inv = 1.0 / (jnp.sum(e, axis=-1, keepdims=True) + en + jnp.exp(sink - m))
        o = _dot((e * inv).astype(BF16), vc)
        pn = (en * inv).astype(BF16).astype(F32)
        for mcol in range(N_Q_HEADS // 2):
            halves = []
            for hd in (2 * mcol, 2 * mcol + 1):
                oh = o[hd * seq_chunk:(hd + 1) * seq_chunk] + pn[hd * seq_chunk:(hd + 1) * seq_chunk] * vb[sr]
                if (hd % 2) != (hd // (N_Q_HEADS // N_KV_HEADS)):
                    oh = pltpu.roll(oh, HEAD_DIM, 1)
                halves.append(oh)
            att = jnp.where(kv_lo, halves[0], halves[1])
            mix_ref[sr, mcol * LANES:(mcol + 1) * LANES] = att.astype(BF16)

    u = _gelu(z[:, C_U:C_U + GMLP_WIDTH])
    vn = _layernorm(_gelu(z[:, C_VG:C_VG + GMLP_WIDTH]), lng_ref[...], lnb_ref[...])
    vnout_ref[...] = vn
    sp = ws0_ref[...].astype(BF16).astype(F32) * vn.astype(BF16).astype(F32) + bs0_ref[...]
    mix_ref[:, ATTN_WIDTH:] = (u * sp).astype(BF16)

    xmid = x + _dot(mix_ref[...], wout_ref[...])
    xmid_ref[...] = xmid
    h2b = _rmsnorm(xmid, g2_ref[...]).astype(BF16)
    d1, d2 = _sort_plan(_route(h2b, wrt_ref, brt_ref), upper_ref, lpad_ref, meta_ref, tab_ref)
    _sort_rows(d1, d2, h2b, xl_tile)
    copy = pltpu.make_async_copy(xl_tile, xl_hbm.at[xl_hbm.shape[0] - 1], xl_sem.at[0])
    copy.start()
    copy.wait()


def _grouped_ffn_kernel(texp_ref, gsrc_ref, sdst_ref, ntiles_ref, xl_hbm, wg_ref, wu_ref, wd_ref, yl_hbm,
                        xbuf, ybuf, gsem, ssem):
    j = pl.program_id(0)
    n_tiles = ntiles_ref[0]

    def gather_copy(tile, s, b):
        return pltpu.make_async_copy(xl_hbm.at[gsrc_ref[tile * FFN_SLOTS + s]], xbuf.at[b, s], gsem.at[b])

    def scatter_copy(tile, s):
        b = (tile + SCATTER_BUFS) % SCATTER_BUFS
        return pltpu.make_async_copy(ybuf.at[b, s], yl_hbm.at[sdst_ref[(tile + 1) * FFN_SLOTS + s]], ssem.at[b])

    @pl.when(j < n_tiles)
    def _():
        gb = j % GATHER_BUFS
        ahead = GATHER_BUFS - 1
        nxt = j + ahead
        nxt_b = nxt % GATHER_BUFS

        @pl.when(j == 0)
        def _():
            ybuf[SCATTER_BUFS - 1] = jnp.zeros(ybuf.shape[1:], BF16)
            for k in range(ahead):
                for s in range(FFN_SLOTS):
                    gather_copy(k, s, k).start()

        for s in range(FFN_SLOTS):
            gather_copy(j, s, gb).wait()

        @pl.when(j >= SCATTER_BUFS - 1)
        def _():
            for s in range(FFN_SLOTS):
                scatter_copy(j - SCATTER_BUFS, s).wait()

        x = xbuf[gb].reshape(FFN_ROWS, D_MODEL)
        gate = _dot(x, wg_ref[...])
        up = _dot(x, wu_ref[...])
        hid = (gate * (1.0 / (1.0 + jnp.exp(-gate))) * up).astype(BF16)
        wd = wd_ref[...]
        for s in range(FFN_SLOTS):
            scatter_copy(j - 1, s).start()
        for s in range(FFN_SLOTS):
            gather_copy(nxt, s, nxt_b).start()
        ybuf[j % SCATTER_BUFS] = _dot(hid, wd).astype(BF16).reshape(FFN_SLOTS, ROW_GRANULE, D_MODEL)

        @pl.when(j == n_tiles - 1)
        def _():
            for k in range(1, GATHER_BUFS):
                for s in range(FFN_SLOTS):
                    gather_copy(nxt, s, (j + k) % GATHER_BUFS).wait()
            for s in range(FFN_SLOTS):
                scatter_copy(j, s).start()
            for back in range(SCATTER_BUFS):
                @pl.when(j - back >= -1)
                def _():
                    for s in range(FFN_SLOTS):
                        scatter_copy(j - back, s).wait()


def _combine_kernel(xmid_ref, yl_hbm, meta_ref, gf_ref, y_ref, ybuf, ysem, *, first_tile, n_tiles):
    n_tok = xmid_ref.shape[0]
    i = pl.program_id(0)
    ahead = COMBINE_BUFS - 1

    def fetch(tile):
        row0 = pl.multiple_of((first_tile + tile) * SORT_ROWS, SORT_ROWS)
        slot = tile % COMBINE_BUFS
        return pltpu.make_async_copy(yl_hbm.at[pl.ds(row0, SORT_ROWS), :], ybuf.at[slot], ysem.at[slot])

    @pl.when(i == 0)
    def _():
        for k in range(min(ahead, n_tiles)):
            fetch(k).start()

    @pl.when(i + ahead < n_tiles)
    def _():
        fetch(i + ahead).start()

    fetch(i).wait()
    yl_ref = ybuf.at[i % COMBINE_BUFS]
    meta = meta_ref[...]
    meta_t = jnp.concatenate([meta, jnp.zeros((LANES - 8, n_tok), F32)], axis=0).T
    half = max(n_tok // 2, LANES)
    for r0 in range(0, n_tok, half):
        rows = slice(r0, r0 + half)
        d1, d2, w1, w2 = (meta_t[rows, i:i + 1] for i in range(4))
        acc = xmid_ref[rows, :]
        for c in range(SORT_ROWS // SORT_CHUNK):
            r_iota = (lax.broadcasted_iota(jnp.int32, (half, SORT_CHUNK), 1) + c * SORT_CHUNK).astype(F32)
            unsort = jnp.where(r_iota == d1, w1, jnp.where(r_iota == d2, w2, 0.0)).astype(BF16)
            acc = acc + _dot(unsort, yl_ref[c * SORT_CHUNK:(c + 1) * SORT_CHUNK, :])
        y_ref[rows, :] = _rmsnorm(acc, gf_ref[...])


def _ffn_schedule(tab):
    n_tok_tiles = tab.shape[0]
    strips = (tab[:, :, 0] * (1.0 / ROW_GRANULE)).astype(jnp.int32)
    starts = (tab[:, :, 1] * (1.0 / ROW_GRANULE)).astype(jnp.int32)
    cnt = strips.T
    row0 = starts.T
    cs = jnp.cumsum(cnt, axis=1) - cnt
    n_str = jnp.sum(cnt, axis=1)
    np_str = (n_str + FFN_SLOTS - 1) // FFN_SLOTS * FFN_SLOTS
    ends = jnp.cumsum(np_str)
    base = ends - np_str
    n_steps_max = _ffn_steps_max(n_tok_tiles) + GATHER_BUFS - 1
    step0 = jnp.arange(n_steps_max, dtype=jnp.int32) * FFN_SLOTS
    stream = jnp.minimum(jnp.sum(ends[None, :] <= step0[:, None], axis=1), N_EXPERTS - 1)
    pick = stream[:, None] == jnp.arange(N_EXPERTS, dtype=jnp.int32)[None, :]
    sel = lambda x: jnp.sum(jnp.where(pick[:, :, None], x[None], 0), axis=1)
    cs_j, cnt_j, row0_j = sel(cs), sel(cnt), sel(row0)
    base_j = jnp.sum(jnp.where(pick, base[None, :], 0), axis=1)
    q = step0[:, None] + jnp.arange(FFN_SLOTS, dtype=jnp.int32)[None, :] - base_j[:, None]
    reached = cs_j[:, None, :] <= q[:, :, None]
    last = lambda x: jnp.sum(jnp.where(reached, jnp.diff(x, axis=1, prepend=0)[:, None, :], 0), axis=2)
    tile_idx = jnp.sum(reached, axis=2).astype(jnp.int32) - 1
    g = q - last(cs_j)
    valid = g < last(cnt_j)
    granule = tile_idx * GRANULES_PER_TILE + last(row0_j) + g
    assert n_tok_tiles >= 2 * FFN_SLOTS
    slot = jnp.arange(FFN_SLOTS, dtype=jnp.int32)[None, :]
    parity = jnp.arange(n_steps_max, dtype=jnp.int32)[:, None] % 2
    pad_dst = lambda par: (slot + par * FFN_SLOTS) * GRANULES_PER_TILE + GRANULES_PER_TILE - 1
    gsrc = jnp.where(valid, granule, slot * GRANULES_PER_TILE + GRANULES_PER_TILE - 2).astype(jnp.int32)
    sdst = jnp.where(valid, granule, pad_dst(parity))
    sdst = jnp.concatenate([pad_dst(1), sdst], axis=0)
    texp = stream.astype(jnp.int32)
    n_steps = (ends[-1] // FFN_SLOTS).astype(jnp.int32).reshape(1)
    return texp, gsrc.reshape(-1), sdst.astype(jnp.int32).reshape(-1), n_steps


assert GRANULES_PER_TILE - (2 * TOK_TILE + N_EXPERTS * (ROW_GRANULE - 1)) // ROW_GRANULE >= 2


def _ffn_steps_max(n_tok_tiles):
    return -(-(n_tok_tiles * GRANULES_PER_TILE + N_EXPERTS * (FFN_SLOTS - 1)) // FFN_SLOTS)


def _vmem_limit(mib):
    n_bytes = mib * 1024 * 1024
    assert n_bytes < V7X_VMEM_BYTES
    return n_bytes


def _rope_tables(pos):
    inv_freq = ROPE_THETA ** (-jnp.arange(HALF, dtype=F32) * 2.0 / HEAD_DIM)
    ang = pos.astype(F32)[:, None] * inv_freq[None, :]
    cos, sin = jnp.cos(ang), jnp.sin(ang)
    reps = LANES // HEAD_DIM
    return jnp.tile(jnp.concatenate([cos, cos], axis=1), (1, reps)), jnp.tile(jnp.concatenate([-sin, sin], axis=1), (1, reps))


def _moe_combine(xmid, yl, meta, gf, n_tok, first_tile, n_tiles):
    return pl.pallas_call(
        functools.partial(_combine_kernel, first_tile=first_tile, n_tiles=n_tiles),
        grid=(n_tiles,),
        in_specs=[
            pl.BlockSpec((n_tok, D_MODEL), lambda i: (i, 0)),
            pl.BlockSpec(memory_space=pl.ANY),
            pl.BlockSpec((None, 8, n_tok), lambda i: (i, 0, 0)),
            pl.BlockSpec((1, D_MODEL), lambda i: (0, 0)),
        ],
        out_specs=pl.BlockSpec((n_tok, D_MODEL), lambda i: (i, 0)),
        out_shape=jax.ShapeDtypeStruct((n_tiles * n_tok, D_MODEL), F32),
        scratch_shapes=[pltpu.VMEM((COMBINE_BUFS, SORT_ROWS, D_MODEL), BF16), pltpu.SemaphoreType.DMA((COMBINE_BUFS,))],
        compiler_params=pltpu.CompilerParams(
            dimension_semantics=("arbitrary",),
            vmem_limit_bytes=_vmem_limit(40)),
        name="moe_combine",
    )(xmid, yl, meta, gf)


def kernel(x_prompt, x_sample, cache_swa_k, cache_swa_v, norm_mix_g, w_in, attn_sinks, gmlp_ln_g, gmlp_ln_b,
           gmlp_w_s, gmlp_b_s, w_out, norm_ffn_g, router_group_w, router_group_b, router_expert_w,
           router_expert_b, expert_w_gate, expert_w_up, expert_w_down, final_norm_g):
    assert norm_mix_g.shape[0] == 1, "single-layer trunk"
    batch, seq, _ = x_prompt.shape
    dec_batch = x_sample.shape[0]
    assert x_sample.shape[1] == 1 and seq % TOK_TILE == 0

    win_ext = w_in[0].astype(BF16)
    wout = w_out[0].astype(BF16)
    g1 = norm_mix_g[0][None, :]
    g2 = norm_ffn_g[0][None, :]
    gf = final_norm_g[None, :]
    lng = gmlp_ln_g[0][None, :]
    lnb = gmlp_ln_b[0][None, :]
    sinks = attn_sinks[0]
    ws = gmlp_w_s[0]
    group_dim = GMLP_WIDTH // GMLP_GROUPS
    bsf = jnp.repeat(gmlp_b_s[0].T, group_dim, axis=1)
    ws0 = jnp.repeat(ws[:, 0, 0], group_dim)[None, :]
    bs0 = jnp.repeat(gmlp_b_s[0][:, 0], group_dim)[None, :]
    wrt = jnp.zeros((ROUTER_ROWS, D_MODEL), F32)
    wrt = wrt.at[0:N_GROUPS].set(router_group_w[0].T)
    wrt = wrt.at[8:8 + N_EXPERTS].set(router_expert_w[0].reshape(D_MODEL, N_EXPERTS).T).astype(BF16)
    brt = jnp.full((ROUTER_ROWS, 1), NEG_INF, F32)
    brt = brt.at[0:N_GROUPS, 0].set(router_group_b[0])
    brt = brt.at[8:8 + N_EXPERTS, 0].set(router_expert_b[0].reshape(N_EXPERTS))
    wg_f = expert_w_gate[0].reshape(N_EXPERTS * D_MODEL, D_EXPERT)
    wu_f = expert_w_up[0].reshape(N_EXPERTS * D_MODEL, D_EXPERT)
    wd_f = expert_w_down[0].reshape(N_EXPERTS * D_EXPERT, D_MODEL)
    cos_p, sin_p = _rope_tables(jnp.arange(seq, dtype=jnp.int32))
    cos_s, sin_s = _rope_tables(PAST_LEN + jnp.arange(1, dtype=jnp.int32))

    full = lambda shape: pl.BlockSpec(shape, lambda *_: (0,) * len(shape))
    smem = pl.BlockSpec(memory_space=pltpu.SMEM)
    n_tiles = seq // TOK_TILE

    upper = jnp.triu(jnp.ones((TOK_TILE, TOK_TILE), BF16), k=1)
    lpad = (jnp.arange(LANES)[None, :] < jnp.arange(N_EXPERTS)[:, None]).astype(BF16)
    n_tok_tiles = batch * n_tiles

    x2d = x_prompt.reshape(batch * seq, D_MODEL)
    tile_a = lambda s: jnp.minimum(s, n_tok_tiles - 1)
    cur = lambda s: jnp.maximum(s - 1, 0)
    tile_b = cur
    gu_rows, dn_rows = wg_f.shape[0] // n_tok_tiles, wd_f.shape[0] // n_tok_tiles
    assert gu_rows * n_tok_tiles == wg_f.shape[0] and dn_rows * n_tok_tiles == wd_f.shape[0] and dn_rows % 16 == 0
    xmid_p, xl_p, meta_p, tab_p, k_p, v_p, wg, wu, wd = pl.pallas_call(
        functools.partial(_prompt_mixer_kernel, tiles_per_seq=n_tiles, n_tok_tiles=n_tok_tiles),
        grid=(n_tok_tiles + 1,),
        in_specs=[
            smem,
            pl.BlockSpec((TOK_TILE, D_MODEL), lambda s: (tile_a(s), 0)),
            pl.BlockSpec((TOK_TILE, D_MODEL), lambda s: (cur(s), 0)),
            full((1, D_MODEL)),
            full((D_MODEL, IN_WIDTH)),
            pl.BlockSpec((TOK_TILE, LANES), lambda s: (tile_b(s) % n_tiles, 0)),
            pl.BlockSpec((TOK_TILE, LANES), lambda s: (tile_b(s) % n_tiles, 0)),
            full((1, GMLP_WIDTH)),
            full((1, GMLP_WIDTH)),
            full((GMLP_GROUPS, CHUNK, CHUNK)),
            full((CHUNK, GMLP_WIDTH)),
            full((D_MODEL, D_MODEL)),
            full((1, D_MODEL)),
            full((ROUTER_ROWS, D_MODEL)),
            full((ROUTER_ROWS, 1)),
            full((TOK_TILE, TOK_TILE)),
            full((N_EXPERTS, LANES)),
            pl.BlockSpec((gu_rows, D_EXPERT), lambda s: (tile_a(s), 0)),
            pl.BlockSpec((gu_rows, D_EXPERT), lambda s: (tile_a(s), 0)),
            pl.BlockSpec((dn_rows, D_MODEL), lambda s: (tile_a(s), 0)),
        ],
        out_specs=[
            pl.BlockSpec((TOK_TILE, D_MODEL), lambda s: (cur(s), 0)),
            pl.BlockSpec((None, SORT_ROWS, D_MODEL), lambda s: (jnp.where(s == 0, n_tok_tiles, s - 1), 0, 0)),
            pl.BlockSpec((None, 8, TOK_TILE), lambda s: (cur(s), 0, 0)),
            pl.BlockSpec((None, N_EXPERTS, LANES), lambda s: (cur(s), 0, 0)),
            pl.BlockSpec((None, WINDOW, KV_WIDTH), lambda s: (tile_b(s) // n_tiles, 0, 0)),
            pl.BlockSpec((None, WINDOW, KV_WIDTH), lambda s: (tile_b(s) // n_tiles, 0, 0)),
            pl.BlockSpec((gu_rows, D_EXPERT), lambda s: (tile_a(s), 0)),
            pl.BlockSpec((gu_rows, D_EXPERT), lambda s: (tile_a(s), 0)),
            pl.BlockSpec((dn_rows, D_MODEL), lambda s: (tile_a(s), 0)),
        ],
        out_shape=[
            jax.ShapeDtypeStruct((batch * seq, D_MODEL), F32),
            jax.ShapeDtypeStruct((n_tok_tiles + 1, SORT_ROWS, D_MODEL), BF16),
            jax.ShapeDtypeStruct((n_tok_tiles, 8, TOK_TILE), F32),
            jax.ShapeDtypeStruct((n_tok_tiles, N_EXPERTS, LANES), F32),
            jax.ShapeDtypeStruct((batch, WINDOW, KV_WIDTH), F32),
            jax.ShapeDtypeStruct((batch, WINDOW, KV_WIDTH), F32),
            jax.ShapeDtypeStruct(wg_f.shape, BF16),
            jax.ShapeDtypeStruct(wu_f.shape, BF16),
            jax.ShapeDtypeStruct(wd_f.shape, BF16),
        ],
        scratch_shapes=[
            pltpu.VMEM((TOK_TILE, IN_WIDTH), F32),
            pltpu.VMEM((TOK_TILE, IN_WIDTH), F32),
            pltpu.VMEM((TOK_TILE, D_MODEL), BF16),
        ] + [pltpu.VMEM((WINDOW + TOK_TILE, KV_WIDTH), BF16)] * 4 + [
            pltpu.VMEM((len(_NAT_HEADS), TOK_TILE, LANES), BF16),
            pltpu.VMEM((len(_SWP_HEADS), TOK_TILE, LANES), BF16),
            pltpu.VMEM((2 * BLOCKS_PER_TILE, len(_NAT_HEADS) * WINDOW, 2 * WINDOW), F32),
            pltpu.VMEM((2 * BLOCKS_PER_TILE, len(_NAT_HEADS) * WINDOW, 2 * WINDOW), BF16),
            pltpu.VMEM((TOK_TILE, GMLP_WIDTH), BF16),
            pltpu.VMEM((TOK_TILE, D_MODEL), BF16),
            pltpu.VMEM((TOK_TILE, D_MODEL), BF16),
        ],
        compiler_params=pltpu.CompilerParams(
            dimension_semantics=("arbitrary",),
            vmem_limit_bytes=_vmem_limit(56)),
        name="prompt_mixer",
    )(sinks, x2d, x2d, g1, win_ext, cos_p, sin_p, lng, lnb, ws, bsf, wout, g2, wrt, brt, upper, lpad,
      wg_f, wu_f, wd_f)
    wg = wg.reshape(N_EXPERTS, D_MODEL, D_EXPERT)
    wu = wu.reshape(N_EXPERTS, D_MODEL, D_EXPERT)
    wd = wd.reshape(N_EXPERTS, D_EXPERT, D_MODEL)

    xs = x_sample.reshape(dec_batch, D_MODEL)
    ck = cache_swa_k[0].reshape(dec_batch, WINDOW, KV_WIDTH)
    cv = cache_swa_v[0].reshape(dec_batch, WINDOW, KV_WIDTH)
    vmem = pl.BlockSpec(memory_space=pltpu.VMEM)
    hbm = pl.BlockSpec(memory_space=pl.ANY)
    xmid_s, meta_s, tab_s, k_s, v_s, vn_s, xl_all = pl.pallas_call(
        _sample_mixer_kernel,
        in_specs=[smem] + [vmem] * 17 + [hbm],
        out_specs=[vmem] * 6 + [hbm],
        out_shape=[
            jax.ShapeDtypeStruct((dec_batch, D_MODEL), F32),
            jax.ShapeDtypeStruct((8, dec_batch), F32),
            jax.ShapeDtypeStruct((N_EXPERTS, LANES), F32),
            jax.ShapeDtypeStruct((dec_batch, KV_WIDTH), F32),
            jax.ShapeDtypeStruct((dec_batch, KV_WIDTH), F32),
            jax.ShapeDtypeStruct((dec_batch, GMLP_WIDTH), F32),
            jax.ShapeDtypeStruct(xl_p.shape, BF16),
        ],
        scratch_shapes=[pltpu.VMEM((dec_batch, D_MODEL), BF16), pltpu.VMEM((SORT_ROWS, D_MODEL), BF16),
                        pltpu.SemaphoreType.DMA((1,))],
        compiler_params=pltpu.CompilerParams(vmem_limit_bytes=_vmem_limit(56)),
        input_output_aliases={18: 6},
        name="sample_mixer",
    )(sinks, xs, ck, cv, g1, win_ext, cos_s, sin_s, lng, lnb, ws0, bs0, wout, g2, wrt, brt,
      upper[:dec_batch, :dec_batch], lpad, xl_p)

    n_all_tiles = n_tok_tiles + 1
    tab_all = jnp.concatenate([tab_p, tab_s[None]], axis=0)
    texp, gsrc, sdst, n_steps = _ffn_schedule(tab_all)
    yl_all = pl.pallas_call(
        _grouped_ffn_kernel,
        grid_spec=pltpu.PrefetchScalarGridSpec(
            num_scalar_prefetch=4,
            grid=(_ffn_steps_max(n_all_tiles),),
            in_specs=[
                pl.BlockSpec(memory_space=pl.ANY),
                pl.BlockSpec((None, D_MODEL, D_EXPERT), lambda j, te, gs, sd, ns: (te[j], 0, 0)),
                pl.BlockSpec((None, D_MODEL, D_EXPERT), lambda j, te, gs, sd, ns: (te[j], 0, 0)),
                pl.BlockSpec((None, D_EXPERT, D_MODEL), lambda j, te, gs, sd, ns: (te[j], 0, 0)),
            ],
            out_specs=pl.BlockSpec(memory_space=pl.ANY),
            scratch_shapes=[
                pltpu.VMEM((GATHER_BUFS, FFN_SLOTS, ROW_GRANULE, D_MODEL), BF16),
                pltpu.VMEM((SCATTER_BUFS, FFN_SLOTS, ROW_GRANULE, D_MODEL), BF16),
                pltpu.SemaphoreType.DMA((GATHER_BUFS,)),
                pltpu.SemaphoreType.DMA((SCATTER_BUFS,)),
            ],
        ),
        out_shape=jax.ShapeDtypeStruct((n_all_tiles * GRANULES_PER_TILE, ROW_GRANULE, D_MODEL), BF16),
        compiler_params=pltpu.CompilerParams(
            dimension_semantics=("arbitrary",),
            vmem_limit_bytes=_vmem_limit(32)),
        input_output_aliases={4: 0},
        name="grouped_ffn",
    )(texp, gsrc, sdst, n_steps, xl_all.reshape(n_all_tiles * GRANULES_PER_TILE, ROW_GRANULE, D_MODEL), wg, wu, wd)
    yl_all = yl_all.reshape(n_all_tiles * SORT_ROWS, D_MODEL)

    y_p = _moe_combine(xmid_p, yl_all, meta_p, gf, TOK_TILE, 0, n_tok_tiles)
    y_s = _moe_combine(xmid_s, yl_all, meta_s[None], gf, dec_batch, n_tok_tiles, 1)

    return (y_p.reshape(batch, seq, D_MODEL),
            y_s.reshape(dec_batch, 1, D_MODEL),
            k_p.reshape(1, batch, WINDOW, N_KV_HEADS, HEAD_DIM),
            v_p.reshape(1, batch, WINDOW, N_KV_HEADS, HEAD_DIM),
            k_s.reshape(1, dec_batch, 1, N_KV_HEADS, HEAD_DIM),
            v_s.reshape(1, dec_batch, 1, N_KV_HEADS, HEAD_DIM),
            vn_s.reshape(1, dec_batch, 1, GMLP_WIDTH))
```

```python
import functools

import jax
import jax.numpy as jnp
import numpy as np
from jax import lax
from jax.experimental import pallas as pl
from jax.experimental.pallas import tpu as pltpu

F32 = jnp.float32
BF16 = jnp.bfloat16

D_MODEL = 1024
HEAD_DIM = 64
HALF = HEAD_DIM // 2
N_Q_HEADS = 8
N_KV_HEADS = 2
ATTN_WIDTH = N_Q_HEADS * HEAD_DIM
KV_WIDTH = N_KV_HEADS * HEAD_DIM
WINDOW = 128
ROPE_THETA = 10000.0
GMLP_WIDTH = D_MODEL - ATTN_WIDTH
GMLP_GROUPS = 8
CHUNK = 128
N_GROUPS = 4
EXPERTS_PER_GROUP = 8
N_EXPERTS = N_GROUPS * EXPERTS_PER_GROUP
D_EXPERT = 256
EPS = 1e-6
NEG_INF = -1e30
PAST_LEN = 16384

LANES = 128
V7X_VMEM_BYTES = 64 * 1024 * 1024

C_Q = 0
C_K = C_Q + ATTN_WIDTH
C_V = C_K + KV_WIDTH
C_U = C_V + KV_WIDTH
C_VG = C_U + GMLP_WIDTH
IN_WIDTH = C_VG + GMLP_WIDTH

ROUTER_ROWS = 48
TOK_TILE = 512
BLOCKS_PER_TILE = TOK_TILE // WINDOW
ROW_GRANULE = 16
SORT_CHUNK = 512
SORT_ROWS = -(-(2 * TOK_TILE + N_EXPERTS * (ROW_GRANULE - 1)) // SORT_CHUNK) * SORT_CHUNK
GRANULES_PER_TILE = SORT_ROWS // ROW_GRANULE
FFN_ROWS = 512
FFN_SLOTS = FFN_ROWS // ROW_GRANULE
GATHER_BUFS = 3
SCATTER_BUFS = 3
COMBINE_BUFS = 3


def _dot(a, b):
    return jnp.dot(a, b, preferred_element_type=F32)


def _dot_nt(a, b):
    return lax.dot_general(a, b, (((1,), (1,)), ((), ())), preferred_element_type=F32)


def _gelu(x):
    return 0.5 * x * (1.0 + lax.erf(x * np.float32(np.sqrt(0.5))))


def _rmsnorm(x, g):
    return x * lax.rsqrt(jnp.mean(x * x, axis=-1, keepdims=True) + EPS) * g


def _layernorm(x, g, b):
    mu = jnp.mean(x, axis=-1, keepdims=True)
    xc = x - mu
    return xc * lax.rsqrt(jnp.mean(xc * xc, axis=-1, keepdims=True) + EPS) * g + b


def _first_argmax_rows(x, row_iota, n_rows):
    m = jnp.max(x, axis=0, keepdims=True)
    idx = jnp.min(jnp.where(x == m, row_iota, n_rows), axis=0, keepdims=True)
    return m, idx


def _route(h2b, wrt_ref, brt_ref):
    n_tok = h2b.shape[0]
    lt = _dot_nt(wrt_ref[...], h2b) + brt_ref[...]
    row8 = lax.broadcasted_iota(jnp.int32, (8, n_tok), 0).astype(F32)
    glog = lt[0:8]
    gmax, gidx = _first_argmax_rows(glog, row8, 8)
    g_w = 1.0 / jnp.sum(jnp.exp(glog - gmax), axis=0, keepdims=True)
    esel = lt[8:16]
    for g in range(1, N_GROUPS):
        esel = jnp.where(gidx == g, lt[8 + 8 * g:16 + 8 * g], esel)
    m1, i1 = _first_argmax_rows(esel, row8, 8)
    esel2 = jnp.where(row8 == i1, -jnp.inf, esel)
    m2, i2 = _first_argmax_rows(esel2, row8, 8)
    r = jnp.exp(m2 - m1)
    w1 = 1.0 / (1.0 + r)
    w2 = r / (1.0 + r)
    e1 = gidx * EXPERTS_PER_GROUP + i1
    e2 = gidx * EXPERTS_PER_GROUP + i2
    slab = jnp.where(row8 == 0, e1, 0.0)
    for r_idx, val in ((1, e2), (2, w1 * g_w), (3, w2 * g_w)):
        slab = jnp.where(row8 == r_idx, val, slab)
    return slab


def _softmax_with_sink(s, sink):
    m = jnp.maximum(jnp.max(s, axis=-1, keepdims=True), sink)
    e = jnp.exp(s - m)
    den = jnp.sum(e, axis=-1, keepdims=True) + jnp.exp(sink - m)
    return e * (1.0 / den)


def _rope(x, cos, sin_signed):
    first_half = (lax.broadcasted_iota(jnp.int32, x.shape, 1) & HALF) == 0
    partner = jnp.where(first_half, pltpu.roll(x, LANES - HALF, 1), pltpu.roll(x, HALF, 1))
    return x * cos + partner * sin_signed


_NAT_HEADS = (0, 2, 5, 7)
_SWP_HEADS = (1, 3, 4, 6)


def _prompt_mixer_kernel(sinks_ref, x_next_ref, x_ref, g1_ref, win_ref, cos_ref, sin_ref, lng_ref, lnb_ref, ws_ref,
                         bsf_ref, wout_ref, g2_ref, wrt_ref, brt_ref, upper_ref, lpad_ref, wgf_ref, wuf_ref, wdf_ref,
                         xmid_ref, xl_ref, meta_ref, tab_ref, kout_ref, vout_ref, wgo_ref, wuo_ref, wdo_ref,
                         z_a, z_b, mix_ref, k_n, k_s, v_n, v_s, q_nat, q_swp, s_ref, p_ref, vn_ref, h_ref, h2_ref,
                         *, tiles_per_seq):
    step = pl.program_id(0)
    t = jnp.maximum(step - 1, 0) % tiles_per_seq
    kv_bufs = (k_n, k_s, v_n, v_s)

    @pl.when(step == 0)
    def _():
        z_b[...] = jnp.zeros_like(z_b)

    for src, dst in ((wgf_ref, wgo_ref), (wuf_ref, wuo_ref), (wdf_ref, wdo_ref)):
        dst[...] = src[...].astype(BF16)

    @pl.when(t == 0)
    def _():
        for ref in kv_bufs:
            ref[0:WINDOW, :] = jnp.zeros((WINDOW, KV_WIDTH), BF16)

    args = (t, sinks_ref, x_next_ref, x_ref, g1_ref, win_ref, cos_ref, sin_ref, lng_ref, lnb_ref, ws_ref, bsf_ref,
            wout_ref, g2_ref, wrt_ref, brt_ref, upper_ref, lpad_ref, xmid_ref, xl_ref, meta_ref, tab_ref, kout_ref,
            vout_ref, kv_bufs, q_nat, q_swp, s_ref, p_ref, vn_ref, h_ref, h2_ref)

    @pl.when(step % 2 == 0)
    def _():
        _prompt_mixer_body(z_a, z_b, mix_ref, *args)

    @pl.when(step % 2 == 1)
    def _():
        _prompt_mixer_body(z_b, z_a, mix_ref, *args)


def _prompt_mixer_body(z_next, z_ref, mix_ref, t, sinks_ref, x_next_ref, x_ref, g1_ref, win_ref, cos_ref,
                       sin_ref, lng_ref, lnb_ref, ws_ref, bsf_ref, wout_ref, g2_ref, wrt_ref, brt_ref, upper_ref,
                       lpad_ref, xmid_ref, xl_ref, meta_ref, tab_ref, kout_ref, vout_ref, kv_bufs,
                       q_nat, q_swp, s_ref, p_ref, vn_ref, h_ref, h2_ref):
    k_n, k_s, v_n, v_s = kv_bufs
    cos = cos_ref[...]
    sin = sin_ref[...]
    lane = lax.broadcasted_iota(jnp.int32, (WINDOW, LANES), 1)
    lo = lane < HEAD_DIM
    row = lax.broadcasted_iota(jnp.int32, (WINDOW, WINDOW), 0)
    col = lax.broadcasted_iota(jnp.int32, (WINDOW, WINDOW), 1)
    mask_cur = col <= row
    mask_prev_band = col >= row
    mask_prev_first = jnp.logical_and(mask_prev_band, (jnp.zeros_like(row) + t) > 0)
    mask_band = jnp.concatenate([mask_prev_band, mask_cur], axis=1)
    mask_first = jnp.concatenate([mask_prev_first, mask_cur], axis=1)

    cq = cos * np.float32(HEAD_DIM ** -0.5)
    sq = sin * np.float32(HEAD_DIM ** -0.5)
    kf = _rope(z_ref[:, C_K:C_K + KV_WIDTH], cos, sin)
    vf = z_ref[:, C_V:C_V + KV_WIDTH]
    k_n[WINDOW:, :] = kf.astype(BF16)
    k_s[WINDOW:, :] = pltpu.roll(kf, HEAD_DIM, 1).astype(BF16)
    v_n[WINDOW:, :] = vf.astype(BF16)
    v_s[WINDOW:, :] = pltpu.roll(vf, HEAD_DIM, 1).astype(BF16)

    kout_ref[...] = kf[TOK_TILE - WINDOW:]
    vout_ref[...] = vf[TOK_TILE - WINDOW:]

    lo_t = lax.broadcasted_iota(jnp.int32, (TOK_TILE, LANES), 1) < HEAD_DIM
    for m in range(N_Q_HEADS // 2):
        qc = _rope(z_ref[:, C_Q + m * LANES:C_Q + (m + 1) * LANES], cq, sq)
        for hd, qh in ((2 * m, jnp.where(lo_t, qc, 0.0)), (2 * m + 1, jnp.where(lo_t, 0.0, qc))):
            if hd in _NAT_HEADS:
                q_nat[_NAT_HEADS.index(hd)] = qh.astype(BF16)
            else:
                q_swp[_SWP_HEADS.index(hd)] = qh.astype(BF16)

    stacks = ((q_nat, _NAT_HEADS, k_n, v_n), (q_swp, _SWP_HEADS, k_s, v_s))
    proj_cols = 2 * LANES

    def project_next(c):
        cols = slice(c * proj_cols, (c + 1) * proj_cols)
        z_next[:, cols] = _dot(h_ref[...], win_ref[:, cols])

    def scores(j):
        rows = slice(j * WINDOW, (j + 1) * WINDOW)
        keys = slice(j * WINDOW, (j + 2) * WINDOW)
        for si, (q_ref, _, k_buf, _) in enumerate(stacks):
            q_stack = jnp.concatenate([q_ref[i, rows, :] for i in range(len(_NAT_HEADS))], axis=0)
            s_ref[2 * j + si] = _dot_nt(q_stack, k_buf[keys, :])

    def softmax(j):
        mask = mask_first if j == 0 else mask_band
        for si, (_, heads, _, _) in enumerate(stacks):
            for i, hd in enumerate(heads):
                pr = slice(i * WINDOW, (i + 1) * WINDOW)
                p_ref[2 * j + si, pr, :] = _softmax_with_sink(
                    jnp.where(mask, s_ref[2 * j + si, pr, :], NEG_INF), sinks_ref[hd]).astype(BF16)

    def attend(j):
        rows = slice(j * WINDOW, (j + 1) * WINDOW)
        keys = slice(j * WINDOW, (j + 2) * WINDOW)
        o_nat = _dot(p_ref[2 * j], v_n[keys, :])
        o_swp = _dot(p_ref[2 * j + 1], v_s[keys, :])
        for m in range(N_Q_HEADS // 2):
            pr = slice(m * WINDOW, (m + 1) * WINDOW)
            even_nat = (2 * m) in _NAT_HEADS
            att = jnp.where(lo, o_nat[pr], o_swp[pr]) if even_nat else jnp.where(lo, o_swp[pr], o_nat[pr])
            mix_ref[rows, m * LANES:(m + 1) * LANES] = att.astype(BF16)

    def gmlp(m):
        cs = slice(m * LANES, (m + 1) * LANES)
        w0 = jnp.where(mask_cur, ws_ref[2 * m], 0.0).astype(BF16)
        w1 = jnp.where(mask_cur, ws_ref[2 * m + 1], 0.0).astype(BF16)
        wcat = jnp.concatenate([w0, w1], axis=1)
        for j in range(BLOCKS_PER_TILE):
            rows = slice(j * WINDOW, (j + 1) * WINDOW)
            vcol = vn_ref[rows, cs]
            rhs = jnp.concatenate([jnp.where(lo, vcol, jnp.zeros_like(vcol)),
                                   jnp.where(lo, jnp.zeros_like(vcol), vcol)], axis=0)
            sp = _dot(wcat, rhs) + bsf_ref[:, cs]
            u = _gelu(z_ref[rows, C_U + m * LANES:C_U + (m + 1) * LANES])
            mix_ref[rows, ATTN_WIDTH + m * LANES:ATTN_WIDTH + (m + 1) * LANES] = (u * sp).astype(BF16)

    def project_out(c):
        cols = slice(c * proj_cols, (c + 1) * proj_cols)
        xmid_ref[:, cols] = x_ref[:, cols] + _dot(mix_ref[...], wout_ref[:, cols])

    h_ref[...] = _rmsnorm(x_next_ref[...], g1_ref[...]).astype(BF16)
    project_next(0)
    project_next(1)
    for j in range(BLOCKS_PER_TILE):
        scores(j)
    vn_ref[...] = _layernorm(_gelu(z_ref[:, C_VG:C_VG + GMLP_WIDTH]), lng_ref[...], lnb_ref[...]).astype(BF16)
    project_next(2)
    softmax(0)
    project_next(3)
    softmax(1)
    project_next(4)
    softmax(2)
    project_next(5)
    softmax(3)
    project_next(6)
    for j in range(BLOCKS_PER_TILE):
        attend(j)
    for m in range(GMLP_GROUPS // 2):
        gmlp(m)
    for c in range(D_MODEL // proj_cols):
        project_out(c)
    h2_ref[...] = _rmsnorm(xmid_ref[...], g2_ref[...]).astype(BF16)
    d1, d2 = _sort_plan(_route(h2_ref[...], wrt_ref, brt_ref), upper_ref, lpad_ref, meta_ref, tab_ref)
    _sort_rows(d1, d2, h2_ref[...], xl_ref)

    for ref in kv_bufs:
        ref[0:WINDOW, :] = ref[TOK_TILE:TOK_TILE + WINDOW, :]


def _sort_plan(slab, upper_ref, lpad_ref, meta_ref, tab_ref):
    n_tok = slab.shape[1]
    e1, e2 = slab[0:1], slab[1:2]
    row32 = lax.broadcasted_iota(jnp.int32, (N_EXPERTS, n_tok), 0).astype(F32)
    sel1 = row32 == e1
    sel2 = row32 == e2
    onehot = jnp.where(sel1, 1.0, jnp.where(sel2, 1.0, 0.0))
    earlier = _dot(onehot.astype(BF16), upper_ref[...])
    cnt = jnp.sum(onehot, axis=1, keepdims=True)
    pc = jnp.floor((cnt + (ROW_GRANULE - 1)) * (1.0 / ROW_GRANULE)) * ROW_GRANULE
    pc_b = jnp.broadcast_to(pc, (N_EXPERTS, LANES))
    pc_pad = jnp.concatenate([pc_b, jnp.zeros((LANES - N_EXPERTS, LANES), F32)], axis=0).astype(BF16)
    start = _dot(lpad_ref[...], pc_pad)
    base = start[:, 0:1] + earlier
    d1 = jnp.sum(jnp.where(sel1, base, 0.0), axis=0, keepdims=True)
    d2 = jnp.sum(jnp.where(sel2, base, 0.0), axis=0, keepdims=True)
    row8 = lax.broadcasted_iota(jnp.int32, (8, n_tok), 0)
    meta_ref[...] = jnp.where(row8 == 0, d1, jnp.where(row8 == 1, d2, jnp.where(row8 >= 4, 0.0, slab)))
    lane = lax.broadcasted_iota(jnp.int32, (N_EXPERTS, LANES), 1)
    tab_ref[...] = jnp.where(lane == 0, pc_b, jnp.where(lane == 1, start, 0.0))
    return d1, d2


def _sort_rows(d1, d2, h2b, xl_ref):
    n_tok = h2b.shape[0]
    for c in range(SORT_ROWS // SORT_CHUNK):
        r_iota = (lax.broadcasted_iota(jnp.int32, (SORT_CHUNK, n_tok), 0) + c * SORT_CHUNK).astype(F32)
        perm = jnp.where(r_iota == d1, 1.0, jnp.where(r_iota == d2, 1.0, 0.0)).astype(BF16)
        xl_ref[c * SORT_CHUNK:(c + 1) * SORT_CHUNK, :] = _dot(perm, h2b).astype(BF16)


def _sample_mixer_kernel(sinks_ref, x_ref, ck_ref, cv_ref, g1_ref, win_ref, cos_ref, sin_ref, lng_ref, lnb_ref,
                         ws0_ref, bs0_ref, wout_ref, g2_ref, wrt_ref, brt_ref, upper_ref, lpad_ref, xl_in_hbm,
                         xmid_ref, meta_ref, tab_ref, kout_ref, vout_ref, vnout_ref, xl_hbm,
                         mix_ref, xl_tile, xl_sem):
    del xl_in_hbm
    n_seq = x_ref.shape[0]
    seq_chunk = 16
    x = x_ref[...]
    h = _rmsnorm(x, g1_ref[...]).astype(BF16)
    z = _dot(h, win_ref[...])
    cos = cos_ref[...]
    sin = sin_ref[...]
    scale = np.float32(HEAD_DIM ** -0.5)
    lane = lax.broadcasted_iota(jnp.int32, (n_seq, LANES), 1)
    lo = lane < HEAD_DIM
    kf = _rope(z[:, C_K:C_K + KV_WIDTH], cos, sin)
    vf = z[:, C_V:C_V + KV_WIDTH]
    kout_ref[...] = kf
    vout_ref[...] = vf
    kb = kf.astype(BF16).astype(F32)
    vb = vf.astype(BF16).astype(F32)

    q_heads = []
    for hd in range(N_Q_HEADS):
        m = hd // 2
        qc = _rope(z[:, C_Q + m * LANES:C_Q + (m + 1) * LANES], cos, sin) * scale
        keep = lo if hd % 2 == 0 else ~lo
        qm = jnp.where(keep, qc, 0.0)
        if (hd % 2) != (hd // (N_Q_HEADS // N_KV_HEADS)):
            qm = pltpu.roll(qm, HEAD_DIM, 1)
        q_heads.append(qm.astype(BF16))

    s_new = [jnp.sum(q_heads[hd].astype(F32) * kb, axis=-1, keepdims=True) for hd in range(N_Q_HEADS)]

    rr = lax.broadcasted_iota(jnp.int32, (N_Q_HEADS * seq_chunk, seq_chunk * WINDOW), 0)
    cc = lax.broadcasted_iota(jnp.int32, (N_Q_HEADS * seq_chunk, seq_chunk * WINDOW), 1)
    same_seq = (rr % seq_chunk) == (cc // WINDOW)
    kv_lo = lax.broadcasted_iota(jnp.int32, (seq_chunk, LANES), 1) < HEAD_DIM

    for c in range(n_seq // seq_chunk):
        sr = slice(c * seq_chunk, (c + 1) * seq_chunk)
        kc = ck_ref[sr].reshape(seq_chunk * WINDOW, KV_WIDTH).astype(BF16)
        vc = cv_ref[sr].reshape(seq_chunk * WINDOW, KV_WIDTH).astype(BF16)
        qs = jnp.concatenate([q_heads[hd][sr] for hd in range(N_Q_HEADS)], axis=0)
        s = jnp.where(same_seq, _dot_nt(qs, kc), NEG_INF)
        sn = jnp.concatenate([s_new[hd][sr] for hd in range(N_Q_HEADS)], axis=0)
        sink = jnp.concatenate([jnp.full((seq_chunk, 1), sinks_ref[hd], F32) for hd in range(N_Q_HEADS)], axis=0)
        m = jnp.maximum(jnp.maximum(jnp.max(s, axis=-1, keepdims=True), sn), sink)
        e = jnp.exp(s - m)
        en = jnp.exp(sn - m)
        inv = 1.0 / (jnp.sum(e, axis=-1, keepdims=True) + en + jnp.exp(sink - m))
        o = _dot((e * inv).astype(BF16), vc)
        pn = (en * inv).astype(BF16).astype(F32)
        for mcol in range(N_Q_HEADS // 2):
            halves = []
            for hd in (2 * mcol, 2 * mcol + 1):
                oh = o[hd * seq_chunk:(hd + 1) * seq_chunk] + pn[hd * seq_chunk:(hd + 1) * seq_chunk] * vb[sr]
                if (hd % 2) != (hd // (N_Q_HEADS // N_KV_HEADS)):
                    oh = pltpu.roll(oh, HEAD_DIM, 1)
                halves.append(oh)
            att = jnp.where(kv_lo, halves[0], halves[1])
            mix_ref[sr, mcol * LANES:(mcol + 1) * LANES] = att.astype(BF16)

    u = _gelu(z[:, C_U:C_U + GMLP_WIDTH])
    vn = _layernorm(_gelu(z[:, C_VG:C_VG + GMLP_WIDTH]), lng_ref[...], lnb_ref[...])
    vnout_ref[...] = vn
    sp = ws0_ref[...].astype(BF16).astype(F32) * vn.astype(BF16).astype(F32) + bs0_ref[...]
    mix_ref[:, ATTN_WIDTH:] = (u * sp).astype(BF16)

    xmid = x + _dot(mix_ref[...], wout_ref[...])
    xmid_ref[...] = xmid
    h2b = _rmsnorm(xmid, g2_ref[...]).astype(BF16)
    d1, d2 = _sort_plan(_route(h2b, wrt_ref, brt_ref), upper_ref, lpad_ref, meta_ref, tab_ref)
    _sort_rows(d1, d2, h2b, xl_tile)
    copy = pltpu.make_async_copy(xl_tile, xl_hbm.at[xl_hbm.shape[0] - 1], xl_sem.at[0])
    copy.start()
    copy.wait()


def _grouped_ffn_kernel(texp_ref, gsrc_ref, sdst_ref, ntiles_ref, xl_hbm, wg_ref, wu_ref, wd_ref, yl_hbm,
                        xbuf, ybuf, gsem, ssem):
    j = pl.program_id(0)
    n_tiles = ntiles_ref[0]

    def gather_copy(tile, s, b):
        return pltpu.make_async_copy(xl_hbm.at[gsrc_ref[tile * FFN_SLOTS + s]], xbuf.at[b, s], gsem.at[b])

    def scatter_copy(tile, s):
        b = (tile + SCATTER_BUFS) % SCATTER_BUFS
        return pltpu.make_async_copy(ybuf.at[b, s], yl_hbm.at[sdst_ref[(tile + 1) * FFN_SLOTS + s]], ssem.at[b])

    @pl.when(j < n_tiles)
    def _():
        gb = j % GATHER_BUFS
        ahead = GATHER_BUFS - 1
        nxt = j + ahead
        nxt_b = nxt % GATHER_BUFS

        @pl.when(j == 0)
        def _():
            ybuf[SCATTER_BUFS - 1] = jnp.zeros(ybuf.shape[1:], BF16)
            for k in range(ahead):
                for s in range(FFN_SLOTS):
                    gather_copy(k, s, k).start()

        for s in range(FFN_SLOTS):
            gather_copy(j, s, gb).wait()

        @pl.when(j >= SCATTER_BUFS - 1)
        def _():
            for s in range(FFN_SLOTS):
                scatter_copy(j - SCATTER_BUFS, s).wait()

        x = xbuf[gb].reshape(FFN_ROWS, D_MODEL)
        gate = _dot(x, wg_ref[...])
        up = _dot(x, wu_ref[...])
        hid = (gate * (1.0 / (1.0 + jnp.exp(-gate))) * up).astype(BF16)
        wd = wd_ref[...]
        for s in range(FFN_SLOTS):
            scatter_copy(j - 1, s).start()
        for s in range(FFN_SLOTS):
            gather_copy(nxt, s, nxt_b).start()
        ybuf[j % SCATTER_BUFS] = _dot(hid, wd).astype(BF16).reshape(FFN_SLOTS, ROW_GRANULE, D_MODEL)

        @pl.when(j == n_tiles - 1)
        def _():
            for k in range(1, GATHER_BUFS):
                for s in range(FFN_SLOTS):
                    gather_copy(nxt, s, (j + k) % GATHER_BUFS).wait()
            for s in range(FFN_SLOTS):
                scatter_copy(j, s).start()
            for back in range(SCATTER_BUFS):
                @pl.when(j - back >= -1)
                def _():
                    for s in range(FFN_SLOTS):
                        scatter_copy(j - back, s).wait()


def _combine_kernel(xmid_ref, yl_hbm, meta_ref, gf_ref, y_ref, ybuf, ysem, *, first_tile, n_tiles):
    n_tok = xmid_ref.shape[0]
    i = pl.program_id(0)
    ahead = COMBINE_BUFS - 1

    def fetch(tile):
        row0 = pl.multiple_of((first_tile + tile) * SORT_ROWS, SORT_ROWS)
        slot = tile % COMBINE_BUFS
        return pltpu.make_async_copy(yl_hbm.at[pl.ds(row0, SORT_ROWS), :], ybuf.at[slot], ysem.at[slot])

    @pl.when(i == 0)
    def _():
        for k in range(min(ahead, n_tiles)):
            fetch(k).start()

    @pl.when(i + ahead < n_tiles)
    def _():
        fetch(i + ahead).start()

    fetch(i).wait()
    yl_ref = ybuf.at[i % COMBINE_BUFS]
    meta = meta_ref[...]
    meta_t = jnp.concatenate([meta, jnp.zeros((LANES - 8, n_tok), F32)], axis=0).T
    half = max(n_tok // 2, LANES)
    for r0 in range(0, n_tok, half):
        rows = slice(r0, r0 + half)
        d1, d2, w1, w2 = (meta_t[rows, i:i + 1] for i in range(4))
        acc = xmid_ref[rows, :]
        for c in range(SORT_ROWS // SORT_CHUNK):
            r_iota = (lax.broadcasted_iota(jnp.int32, (half, SORT_CHUNK), 1) + c * SORT_CHUNK).astype(F32)
            unsort = jnp.where(r_iota == d1, w1, jnp.where(r_iota == d2, w2, 0.0)).astype(BF16)
            acc = acc + _dot(unsort, yl_ref[c * SORT_CHUNK:(c + 1) * SORT_CHUNK, :])
        y_ref[rows, :] = _rmsnorm(acc, gf_ref[...])


def _ffn_schedule(tab):
    n_tok_tiles = tab.shape[0]
    strips = (tab[:, :, 0] * (1.0 / ROW_GRANULE)).astype(jnp.int32)
    starts = (tab[:, :, 1] * (1.0 / ROW_GRANULE)).astype(jnp.int32)
    cnt = strips.T
    row0 = starts.T
    cs = jnp.cumsum(cnt, axis=1) - cnt
    n_str = jnp.sum(cnt, axis=1)
    np_str = (n_str + FFN_SLOTS - 1) // FFN_SLOTS * FFN_SLOTS
    ends = jnp.cumsum(np_str)
    base = ends - np_str
    n_steps_max = _ffn_steps_max(n_tok_tiles) + GATHER_BUFS - 1
    step0 = jnp.arange(n_steps_max, dtype=jnp.int32) * FFN_SLOTS
    stream = jnp.minimum(jnp.sum(ends[None, :] <= step0[:, None], axis=1), N_EXPERTS - 1)
    pick = stream[:, None] == jnp.arange(N_EXPERTS, dtype=jnp.int32)[None, :]
    sel = lambda x: jnp.sum(jnp.where(pick[:, :, None], x[None], 0), axis=1)
    cs_j, cnt_j, row0_j = sel(cs), sel(cnt), sel(row0)
    base_j = jnp.sum(jnp.where(pick, base[None, :], 0), axis=1)
    q = step0[:, None] + jnp.arange(FFN_SLOTS, dtype=jnp.int32)[None, :] - base_j[:, None]
    reached = cs_j[:, None, :] <= q[:, :, None]
    last = lambda x: jnp.sum(jnp.where(reached, jnp.diff(x, axis=1, prepend=0)[:, None, :], 0), axis=2)
    tile_idx = jnp.sum(reached, axis=2).astype(jnp.int32) - 1
    g = q - last(cs_j)
    valid = g < last(cnt_j)
    granule = tile_idx * GRANULES_PER_TILE + last(row0_j) + g
    assert n_tok_tiles >= 2 * FFN_SLOTS
    slot = jnp.arange(FFN_SLOTS, dtype=jnp.int32)[None, :]
    parity = jnp.arange(n_steps_max, dtype=jnp.int32)[:, None] % 2
    pad_dst = lambda par: (slot + par * FFN_SLOTS) * GRANULES_PER_TILE + GRANULES_PER_TILE - 1
    gsrc = jnp.where(valid, granule, slot * GRANULES_PER_TILE + GRANULES_PER_TILE - 2).astype(jnp.int32)
    sdst = jnp.where(valid, granule, pad_dst(parity))
    sdst = jnp.concatenate([pad_dst(1), sdst], axis=0)
    texp = stream.astype(jnp.int32)
    n_steps = (ends[-1] // FFN_SLOTS).astype(jnp.int32).reshape(1)
    return texp, gsrc.reshape(-1), sdst.astype(jnp.int32).reshape(-1), n_steps


assert GRANULES_PER_TILE - (2 * TOK_TILE + N_EXPERTS * (ROW_GRANULE - 1)) // ROW_GRANULE >= 2


def _ffn_steps_max(n_tok_tiles):
    return -(-(n_tok_tiles * GRANULES_PER_TILE + N_EXPERTS * (FFN_SLOTS - 1)) // FFN_SLOTS)


def _vmem_limit(mib):
    n_bytes = mib * 1024 * 1024
    assert n_bytes < V7X_VMEM_BYTES
    return n_bytes


def _rope_tables(pos):
    inv_freq = ROPE_THETA ** (-np.arange(HALF, dtype=np.float64) * 2.0 / HEAD_DIM)
    ang = np.asarray(pos, np.float64)[:, None] * inv_freq[None, :]
    cos, sin = np.cos(ang).astype(np.float32), np.sin(ang).astype(np.float32)
    reps = LANES // HEAD_DIM
    return (jnp.asarray(np.tile(np.concatenate([cos, cos], axis=1), (1, reps))),
            jnp.asarray(np.tile(np.concatenate([-sin, sin], axis=1), (1, reps))))


def _moe_combine(xmid, yl, meta, gf, n_tok, first_tile, n_tiles):
    return pl.pallas_call(
        functools.partial(_combine_kernel, first_tile=first_tile, n_tiles=n_tiles),
        grid=(n_tiles,),
        in_specs=[
            pl.BlockSpec((n_tok, D_MODEL), lambda i: (i, 0)),
            pl.BlockSpec(memory_space=pl.ANY),
            pl.BlockSpec((None, 8, n_tok), lambda i: (i, 0, 0)),
            pl.BlockSpec((1, D_MODEL), lambda i: (0, 0)),
        ],
        out_specs=pl.BlockSpec((n_tok, D_MODEL), lambda i: (i, 0)),
        out_shape=jax.ShapeDtypeStruct((n_tiles * n_tok, D_MODEL), F32),
        scratch_shapes=[pltpu.VMEM((COMBINE_BUFS, SORT_ROWS, D_MODEL), BF16), pltpu.SemaphoreType.DMA((COMBINE_BUFS,))],
        compiler_params=pltpu.CompilerParams(
            dimension_semantics=("arbitrary",),
            vmem_limit_bytes=_vmem_limit(40)),
        name="moe_combine",
    )(xmid, yl, meta, gf)


def kernel(x_prompt, x_sample, cache_swa_k, cache_swa_v, norm_mix_g, w_in, attn_sinks, gmlp_ln_g, gmlp_ln_b,
           gmlp_w_s, gmlp_b_s, w_out, norm_ffn_g, router_group_w, router_group_b, router_expert_w,
           router_expert_b, expert_w_gate, expert_w_up, expert_w_down, final_norm_g):
    assert norm_mix_g.shape[0] == 1, "single-layer trunk"
    batch, seq, _ = x_prompt.shape
    dec_batch = x_sample.shape[0]
    assert x_sample.shape[1] == 1 and seq % TOK_TILE == 0

    win_ext = w_in[0].astype(BF16)
    wout = w_out[0].astype(BF16)
    g1 = norm_mix_g[0][None, :]
    g2 = norm_ffn_g[0][None, :]
    gf = final_norm_g[None, :]
    lng = gmlp_ln_g[0][None, :]
    lnb = gmlp_ln_b[0][None, :]
    sinks = attn_sinks[0]
    ws = gmlp_w_s[0]
    group_dim = GMLP_WIDTH // GMLP_GROUPS
    bsf = jnp.repeat(gmlp_b_s[0].T, group_dim, axis=1)
    ws0 = jnp.repeat(ws[:, 0, 0], group_dim)[None, :]
    bs0 = jnp.repeat(gmlp_b_s[0][:, 0], group_dim)[None, :]
    pad_g, pad_e = 8 - N_GROUPS, ROUTER_ROWS - 8 - N_EXPERTS
    wrt = jnp.concatenate([router_group_w[0], jnp.zeros((D_MODEL, pad_g), F32),
                           router_expert_w[0].reshape(D_MODEL, N_EXPERTS), jnp.zeros((D_MODEL, pad_e), F32)],
                          axis=1).T.astype(BF16)
    brt = jnp.concatenate([router_group_b[0], jnp.full((pad_g,), NEG_INF, F32),
                           router_expert_b[0].reshape(N_EXPERTS), jnp.full((pad_e,), NEG_INF, F32)])[:, None]
    wg_f = expert_w_gate[0].reshape(N_EXPERTS * D_MODEL, D_EXPERT)
    wu_f = expert_w_up[0].reshape(N_EXPERTS * D_MODEL, D_EXPERT)
    wd_f = expert_w_down[0].reshape(N_EXPERTS * D_EXPERT, D_MODEL)
    cos_p, sin_p = _rope_tables(np.arange(seq))
    cos_s, sin_s = _rope_tables(PAST_LEN + np.arange(1))

    full = lambda shape: pl.BlockSpec(shape, lambda *_: (0,) * len(shape))
    smem = pl.BlockSpec(memory_space=pltpu.SMEM)
    n_tiles = seq // TOK_TILE

    upper = jnp.triu(jnp.ones((TOK_TILE, TOK_TILE), BF16), k=1)
    lpad = (jnp.arange(LANES)[None, :] < jnp.arange(N_EXPERTS)[:, None]).astype(BF16)
    n_tok_tiles = batch * n_tiles

    x2d = x_prompt.reshape(batch * seq, D_MODEL)
    tile_a = lambda s: jnp.minimum(s, n_tok_tiles - 1)
    cur = lambda s: jnp.maximum(s - 1, 0)
    tile_b = cur
    gu_rows, dn_rows = wg_f.shape[0] // n_tok_tiles, wd_f.shape[0] // n_tok_tiles
    assert gu_rows * n_tok_tiles == wg_f.shape[0] and dn_rows * n_tok_tiles == wd_f.shape[0] and dn_rows % 16 == 0
    xmid_p, xl_p, meta_p, tab_p, k_p, v_p, wg, wu, wd = pl.pallas_call(
        functools.partial(_prompt_mixer_kernel, tiles_per_seq=n_tiles),
        grid=(n_tok_tiles + 1,),
        in_specs=[
            smem,
            pl.BlockSpec((TOK_TILE, D_MODEL), lambda s: (tile_a(s), 0)),
            pl.BlockSpec((TOK_TILE, D_MODEL), lambda s: (cur(s), 0)),
            full((1, D_MODEL)),
            full((D_MODEL, IN_WIDTH)),
            pl.BlockSpec((TOK_TILE, LANES), lambda s: (tile_b(s) % n_tiles, 0)),
            pl.BlockSpec((TOK_TILE, LANES), lambda s: (tile_b(s) % n_tiles, 0)),
            full((1, GMLP_WIDTH)),
            full((1, GMLP_WIDTH)),
            full((GMLP_GROUPS, CHUNK, CHUNK)),
            full((CHUNK, GMLP_WIDTH)),
            full((D_MODEL, D_MODEL)),
            full((1, D_MODEL)),
            full((ROUTER_ROWS, D_MODEL)),
            full((ROUTER_ROWS, 1)),
            full((TOK_TILE, TOK_TILE)),
            full((N_EXPERTS, LANES)),
            pl.BlockSpec((gu_rows, D_EXPERT), lambda s: (tile_a(s), 0)),
            pl.BlockSpec((gu_rows, D_EXPERT), lambda s: (tile_a(s), 0)),
            pl.BlockSpec((dn_rows, D_MODEL), lambda s: (tile_a(s), 0)),
        ],
        out_specs=[
            pl.BlockSpec((TOK_TILE, D_MODEL), lambda s: (cur(s), 0)),
            pl.BlockSpec((None, SORT_ROWS, D_MODEL), lambda s: (jnp.where(s == 0, n_tok_tiles, s - 1), 0, 0)),
            pl.BlockSpec((None, 8, TOK_TILE), lambda s: (cur(s), 0, 0)),
            pl.BlockSpec((None, N_EXPERTS, LANES), lambda s: (cur(s), 0, 0)),
            pl.BlockSpec((None, WINDOW, KV_WIDTH), lambda s: (tile_b(s) // n_tiles, 0, 0)),
            pl.BlockSpec((None, WINDOW, KV_WIDTH), lambda s: (tile_b(s) // n_tiles, 0, 0)),
            pl.BlockSpec((gu_rows, D_EXPERT), lambda s: (tile_a(s), 0)),
            pl.BlockSpec((gu_rows, D_EXPERT), lambda s: (tile_a(s), 0)),
            pl.BlockSpec((dn_rows, D_MODEL), lambda s: (tile_a(s), 0)),
        ],
        out_shape=[
            jax.ShapeDtypeStruct((batch * seq, D_MODEL), F32),
            jax.ShapeDtypeStruct((n_tok_tiles + 1, SORT_ROWS, D_MODEL), BF16),
            jax.ShapeDtypeStruct((n_tok_tiles, 8, TOK_TILE), F32),
            jax.ShapeDtypeStruct((n_tok_tiles, N_EXPERTS, LANES), F32),
            jax.ShapeDtypeStruct((batch, WINDOW, KV_WIDTH), F32),
            jax.ShapeDtypeStruct((batch, WINDOW, KV_WIDTH), F32),
            jax.ShapeDtypeStruct(wg_f.shape, BF16),
            jax.ShapeDtypeStruct(wu_f.shape, BF16),
            jax.ShapeDtypeStruct(wd_f.shape, BF16),
        ],
        scratch_shapes=[
            pltpu.VMEM((TOK_TILE, IN_WIDTH), F32),
            pltpu.VMEM((TOK_TILE, IN_WIDTH), F32),
            pltpu.VMEM((TOK_TILE, D_MODEL), BF16),
        ] + [pltpu.VMEM((WINDOW + TOK_TILE, KV_WIDTH), BF16)] * 4 + [
            pltpu.VMEM((len(_NAT_HEADS), TOK_TILE, LANES), BF16),
            pltpu.VMEM((len(_SWP_HEADS), TOK_TILE, LANES), BF16),
            pltpu.VMEM((2 * BLOCKS_PER_TILE, len(_NAT_HEADS) * WINDOW, 2 * WINDOW), F32),
            pltpu.VMEM((2 * BLOCKS_PER_TILE, len(_NAT_HEADS) * WINDOW, 2 * WINDOW), BF16),
            pltpu.VMEM((TOK_TILE, GMLP_WIDTH), BF16),
            pltpu.VMEM((TOK_TILE, D_MODEL), BF16),
            pltpu.VMEM((TOK_TILE, D_MODEL), BF16),
        ],
        compiler_params=pltpu.CompilerParams(
            dimension_semantics=("arbitrary",),
            vmem_limit_bytes=_vmem_limit(56)),
        name="prompt_mixer",
    )(sinks, x2d, x2d, g1, win_ext, cos_p, sin_p, lng, lnb, ws, bsf, wout, g2, wrt, brt, upper, lpad,
      wg_f, wu_f, wd_f)
    wg = wg.reshape(N_EXPERTS, D_MODEL, D_EXPERT)
    wu = wu.reshape(N_EXPERTS, D_MODEL, D_EXPERT)
    wd = wd.reshape(N_EXPERTS, D_EXPERT, D_MODEL)

    xs = x_sample.reshape(dec_batch, D_MODEL)
    ck = cache_swa_k[0].reshape(dec_batch, WINDOW, KV_WIDTH)
    cv = cache_swa_v[0].reshape(dec_batch, WINDOW, KV_WIDTH)
    vmem = pl.BlockSpec(memory_space=pltpu.VMEM)
    hbm = pl.BlockSpec(memory_space=pl.ANY)
    xmid_s, meta_s, tab_s, k_s, v_s, vn_s, xl_all = pl.pallas_call(
        _sample_mixer_kernel,
        in_specs=[smem] + [vmem] * 17 + [hbm],
        out_specs=[vmem] * 6 + [hbm],
        out_shape=[
            jax.ShapeDtypeStruct((dec_batch, D_MODEL), F32),
            jax.ShapeDtypeStruct((8, dec_batch), F32),
            jax.ShapeDtypeStruct((N_EXPERTS, LANES), F32),
            jax.ShapeDtypeStruct((dec_batch, KV_WIDTH), F32),
            jax.ShapeDtypeStruct((dec_batch, KV_WIDTH), F32),
            jax.ShapeDtypeStruct((dec_batch, GMLP_WIDTH), F32),
            jax.ShapeDtypeStruct(xl_p.shape, BF16),
        ],
        scratch_shapes=[pltpu.VMEM((dec_batch, D_MODEL), BF16), pltpu.VMEM((SORT_ROWS, D_MODEL), BF16),
                        pltpu.SemaphoreType.DMA((1,))],
        compiler_params=pltpu.CompilerParams(vmem_limit_bytes=_vmem_limit(56)),
        input_output_aliases={18: 6},
        name="sample_mixer",
    )(sinks, xs, ck, cv, g1, win_ext, cos_s, sin_s, lng, lnb, ws0, bs0, wout, g2, wrt, brt,
      upper[:dec_batch, :dec_batch], lpad, xl_p)

    n_all_tiles = n_tok_tiles + 1
    tab_all = jnp.concatenate([tab_p, tab_s[None]], axis=0)
    texp, gsrc, sdst, n_steps = _ffn_schedule(tab_all)
    yl_all = pl.pallas_call(
        _grouped_ffn_kernel,
        grid_spec=pltpu.PrefetchScalarGridSpec(
            num_scalar_prefetch=4,
            grid=(_ffn_steps_max(n_all_tiles),),
            in_specs=[
                pl.BlockSpec(memory_space=pl.ANY),
                pl.BlockSpec((None, D_MODEL, D_EXPERT), lambda j, te, gs, sd, ns: (te[j], 0, 0)),
                pl.BlockSpec((None, D_MODEL, D_EXPERT), lambda j, te, gs, sd, ns: (te[j], 0, 0)),
                pl.BlockSpec((None, D_EXPERT, D_MODEL), lambda j, te, gs, sd, ns: (te[j], 0, 0)),
            ],
            out_specs=pl.BlockSpec(memory_space=pl.ANY),
            scratch_shapes=[
                pltpu.VMEM((GATHER_BUFS, FFN_SLOTS, ROW_GRANULE, D_MODEL), BF16),
                pltpu.VMEM((SCATTER_BUFS, FFN_SLOTS, ROW_GRANULE, D_MODEL), BF16),
                pltpu.SemaphoreType.DMA((GATHER_BUFS,)),
                pltpu.SemaphoreType.DMA((SCATTER_BUFS,)),
            ],
        ),
        out_shape=jax.ShapeDtypeStruct((n_all_tiles * GRANULES_PER_TILE, ROW_GRANULE, D_MODEL), BF16),
        compiler_params=pltpu.CompilerParams(
            dimension_semantics=("arbitrary",),
            vmem_limit_bytes=_vmem_limit(32)),
        input_output_aliases={4: 0},
        name="grouped_ffn",
    )(texp, gsrc, sdst, n_steps, xl_all.reshape(n_all_tiles * GRANULES_PER_TILE, ROW_GRANULE, D_MODEL), wg, wu, wd)
    yl_all = yl_all.reshape(n_all_tiles * SORT_ROWS, D_MODEL)

    y_p = _moe_combine(xmid_p, yl_all, meta_p, gf, TOK_TILE, 0, n_tok_tiles)
    y_s = _moe_combine(xmid_s, yl_all, meta_s[None], gf, dec_batch, n_tok_tiles, 1)

    return (y_p.reshape(batch, seq, D_MODEL),
            y_s.reshape(dec_batch, 1, D_MODEL),
            k_p.reshape(1, batch, WINDOW, N_KV_HEADS, HEAD_DIM),
            v_p.reshape(1, batch, WINDOW, N_KV_HEADS, HEAD_DIM),
            k_s.reshape(1, dec_batch, 1, N_KV_HEADS, HEAD_DIM),
            v_s.reshape(1, dec_batch, 1, N_KV_HEADS, HEAD_DIM),
            vn_s.reshape(1, dec_batch, 1, GMLP_WIDTH))
```

```python
import functools

import jax
import jax.numpy as jnp
import numpy as np
from jax import lax
from jax.experimental import pallas as pl
from jax.experimental.pallas import tpu as pltpu

F32 = jnp.float32
BF16 = jnp.bfloat16

D_MODEL = 1024
HEAD_DIM = 64
HALF = HEAD_DIM // 2
N_Q_HEADS = 8
N_KV_HEADS = 2
ATTN_WIDTH = N_Q_HEADS * HEAD_DIM
KV_WIDTH = N_KV_HEADS * HEAD_DIM
WINDOW = 128
ROPE_THETA = 10000.0
GMLP_WIDTH = D_MODEL - ATTN_WIDTH
GMLP_GROUPS = 8
CHUNK = 128
N_GROUPS = 4
EXPERTS_PER_GROUP = 8
N_EXPERTS = N_GROUPS * EXPERTS_PER_GROUP
D_EXPERT = 256
EPS = 1e-6
NEG_INF = -1e30
PAST_LEN = 16384

LANES = 128
V7X_VMEM_BYTES = 64 * 1024 * 1024

C_Q = 0
C_K = C_Q + ATTN_WIDTH
C_V = C_K + KV_WIDTH
C_U = C_V + KV_WIDTH
C_VG = C_U + GMLP_WIDTH
IN_WIDTH = C_VG + GMLP_WIDTH

ROUTER_ROWS = 48
TOK_TILE = 512
BLOCKS_PER_TILE = TOK_TILE // WINDOW
ROW_GRANULE = 16
SORT_CHUNK = 512
SORT_ROWS = -(-(2 * TOK_TILE + N_EXPERTS * (ROW_GRANULE - 1)) // SORT_CHUNK) * SORT_CHUNK
GRANULES_PER_TILE = SORT_ROWS // ROW_GRANULE
FFN_ROWS = 512
FFN_SLOTS = FFN_ROWS // ROW_GRANULE
GATHER_BUFS = 4
SCATTER_BUFS = 3
COMBINE_BUFS = 3


def _dot(a, b):
    return jnp.dot(a, b, preferred_element_type=F32)


def _dot_nt(a, b):
    return lax.dot_general(a, b, (((1,), (1,)), ((), ())), preferred_element_type=F32)


def _gelu(x):
    return 0.5 * x * (1.0 + lax.erf(x * np.float32(np.sqrt(0.5))))


def _rmsnorm(x, g):
    return x * lax.rsqrt(jnp.mean(x * x, axis=-1, keepdims=True) + EPS) * g


def _layernorm(x, g, b):
    mu = jnp.mean(x, axis=-1, keepdims=True)
    xc = x - mu
    return xc * lax.rsqrt(jnp.mean(xc * xc, axis=-1, keepdims=True) + EPS) * g + b


def _first_argmax_rows(x, row_iota, n_rows):
    m = jnp.max(x, axis=0, keepdims=True)
    idx = jnp.min(jnp.where(x == m, row_iota, n_rows), axis=0, keepdims=True)
    return m, idx


def _route(h2b, wrt_ref, brt_ref):
    n_tok = h2b.shape[0]
    lt = _dot_nt(wrt_ref[...], h2b) + brt_ref[...]
    row8 = lax.broadcasted_iota(jnp.int32, (8, n_tok), 0).astype(F32)
    glog = lt[0:8]
    gmax, gidx = _first_argmax_rows(glog, row8, 8)
    g_w = 1.0 / jnp.sum(jnp.exp(glog - gmax), axis=0, keepdims=True)
    esel = lt[8:16]
    for g in range(1, N_GROUPS):
        esel = jnp.where(gidx == g, lt[8 + 8 * g:16 + 8 * g], esel)
    m1, i1 = _first_argmax_rows(esel, row8, 8)
    esel2 = jnp.where(row8 == i1, -jnp.inf, esel)
    m2, i2 = _first_argmax_rows(esel2, row8, 8)
    r = jnp.exp(m2 - m1)
    w1 = 1.0 / (1.0 + r)
    w2 = r / (1.0 + r)
    e1 = gidx * EXPERTS_PER_GROUP + i1
    e2 = gidx * EXPERTS_PER_GROUP + i2
    slab = jnp.where(row8 == 0, e1, 0.0)
    for r_idx, val in ((1, e2), (2, w1 * g_w), (3, w2 * g_w)):
        slab = jnp.where(row8 == r_idx, val, slab)
    return slab


def _softmax_with_sink(s, sink):
    m = jnp.maximum(jnp.max(s, axis=-1, keepdims=True), sink)
    e = jnp.exp(s - m)
    den = jnp.sum(e, axis=-1, keepdims=True) + jnp.exp(sink - m)
    return e * (1.0 / den)


def _rope(x, cos, sin_signed):
    first_half = (lax.broadcasted_iota(jnp.int32, x.shape, 1) & HALF) == 0
    partner = jnp.where(first_half, pltpu.roll(x, LANES - HALF, 1), pltpu.roll(x, HALF, 1))
    return x * cos + partner * sin_signed


_NAT_HEADS = (0, 2, 5, 7)
_SWP_HEADS = (1, 3, 4, 6)


def _prompt_mixer_kernel(sinks_ref, x_next_ref, x_ref, g1_ref, win_ref, cos_ref, sin_ref, lng_ref, lnb_ref, ws_ref,
                         bsf_ref, wout_ref, g2_ref, wrt_ref, brt_ref, upper_ref, lpad_ref, wgf_ref, wuf_ref, wdf_ref,
                         xmid_ref, xl_ref, meta_ref, tab_ref, kout_ref, vout_ref, wgo_ref, wuo_ref, wdo_ref,
                         z_a, z_b, mix_ref, k_n, k_s, v_n, v_s, q_nat, q_swp, s_ref, p_ref, vn_ref, h_ref, h2_ref,
                         *, tiles_per_seq):
    step = pl.program_id(0)
    t = jnp.maximum(step - 1, 0) % tiles_per_seq
    kv_bufs = (k_n, k_s, v_n, v_s)

    @pl.when(step == 0)
    def _():
        z_b[...] = jnp.zeros_like(z_b)

    for src, dst in ((wgf_ref, wgo_ref), (wuf_ref, wuo_ref), (wdf_ref, wdo_ref)):
        dst[...] = src[...].astype(BF16)

    @pl.when(t == 0)
    def _():
        for ref in kv_bufs:
            ref[0:WINDOW, :] = jnp.zeros((WINDOW, KV_WIDTH), BF16)

    args = (t, sinks_ref, x_next_ref, x_ref, g1_ref, win_ref, cos_ref, sin_ref, lng_ref, lnb_ref, ws_ref, bsf_ref,
            wout_ref, g2_ref, wrt_ref, brt_ref, upper_ref, lpad_ref, xmid_ref, xl_ref, meta_ref, tab_ref, kout_ref,
            vout_ref, kv_bufs, q_nat, q_swp, s_ref, p_ref, vn_ref, h_ref, h2_ref)

    @pl.when(step % 2 == 0)
    def _():
        _prompt_mixer_body(z_a, z_b, mix_ref, *args)

    @pl.when(step % 2 == 1)
    def _():
        _prompt_mixer_body(z_b, z_a, mix_ref, *args)


def _prompt_mixer_body(z_next, z_ref, mix_ref, t, sinks_ref, x_next_ref, x_ref, g1_ref, win_ref, cos_ref,
                       sin_ref, lng_ref, lnb_ref, ws_ref, bsf_ref, wout_ref, g2_ref, wrt_ref, brt_ref, upper_ref,
                       lpad_ref, xmid_ref, xl_ref, meta_ref, tab_ref, kout_ref, vout_ref, kv_bufs,
                       q_nat, q_swp, s_ref, p_ref, vn_ref, h_ref, h2_ref):
    k_n, k_s, v_n, v_s = kv_bufs
    cos = cos_ref[...]
    sin = sin_ref[...]
    lane = lax.broadcasted_iota(jnp.int32, (WINDOW, LANES), 1)
    lo = lane < HEAD_DIM
    row = lax.broadcasted_iota(jnp.int32, (WINDOW, WINDOW), 0)
    col = lax.broadcasted_iota(jnp.int32, (WINDOW, WINDOW), 1)
    mask_cur = col <= row
    mask_prev_band = col >= row
    mask_prev_first = jnp.logical_and(mask_prev_band, (jnp.zeros_like(row) + t) > 0)
    mask_band = jnp.concatenate([mask_prev_band, mask_cur], axis=1)
    mask_first = jnp.concatenate([mask_prev_first, mask_cur], axis=1)

    cq = cos * np.float32(HEAD_DIM ** -0.5)
    sq = sin * np.float32(HEAD_DIM ** -0.5)
    kf = _rope(z_ref[:, C_K:C_K + KV_WIDTH], cos, sin)
    vf = z_ref[:, C_V:C_V + KV_WIDTH]
    k_n[WINDOW:, :] = kf.astype(BF16)
    k_s[WINDOW:, :] = pltpu.roll(kf, HEAD_DIM, 1).astype(BF16)
    v_n[WINDOW:, :] = vf.astype(BF16)
    v_s[WINDOW:, :] = pltpu.roll(vf, HEAD_DIM, 1).astype(BF16)

    kout_ref[...] = kf[TOK_TILE - WINDOW:]
    vout_ref[...] = vf[TOK_TILE - WINDOW:]

    lo_t = lax.broadcasted_iota(jnp.int32, (TOK_TILE, LANES), 1) < HEAD_DIM
    for m in range(N_Q_HEADS // 2):
        qc = _rope(z_ref[:, C_Q + m * LANES:C_Q + (m + 1) * LANES], cq, sq)
        for hd, qh in ((2 * m, jnp.where(lo_t, qc, 0.0)), (2 * m + 1, jnp.where(lo_t, 0.0, qc))):
            if hd in _NAT_HEADS:
                q_nat[_NAT_HEADS.index(hd)] = qh.astype(BF16)
            else:
                q_swp[_SWP_HEADS.index(hd)] = qh.astype(BF16)

    stacks = ((q_nat, _NAT_HEADS, k_n, v_n), (q_swp, _SWP_HEADS, k_s, v_s))
    proj_cols = 2 * LANES

    def project_next(c):
        cols = slice(c * proj_cols, (c + 1) * proj_cols)
        z_next[:, cols] = _dot(h_ref[...], win_ref[:, cols])

    def scores(j):
        rows = slice(j * WINDOW, (j + 1) * WINDOW)
        keys = slice(j * WINDOW, (j + 2) * WINDOW)
        for si, (q_ref, _, k_buf, _) in enumerate(stacks):
            q_stack = jnp.concatenate([q_ref[i, rows, :] for i in range(len(_NAT_HEADS))], axis=0)
            s_ref[2 * j + si] = _dot_nt(q_stack, k_buf[keys, :])

    def softmax(j):
        mask = mask_first if j == 0 else mask_band
        for si, (_, heads, _, _) in enumerate(stacks):
            for i, hd in enumerate(heads):
                pr = slice(i * WINDOW, (i + 1) * WINDOW)
                p_ref[2 * j + si, pr, :] = _softmax_with_sink(
                    jnp.where(mask, s_ref[2 * j + si, pr, :], NEG_INF), sinks_ref[hd]).astype(BF16)

    def attend(j):
        rows = slice(j * WINDOW, (j + 1) * WINDOW)
        keys = slice(j * WINDOW, (j + 2) * WINDOW)
        o_nat = _dot(p_ref[2 * j], v_n[keys, :])
        o_swp = _dot(p_ref[2 * j + 1], v_s[keys, :])
        for m in range(N_Q_HEADS // 2):
            pr = slice(m * WINDOW, (m + 1) * WINDOW)
            even_nat = (2 * m) in _NAT_HEADS
            att = jnp.where(lo, o_nat[pr], o_swp[pr]) if even_nat else jnp.where(lo, o_swp[pr], o_nat[pr])
            mix_ref[rows, m * LANES:(m + 1) * LANES] = att.astype(BF16)

    def gmlp(m):
        cs = slice(m * LANES, (m + 1) * LANES)
        w0 = jnp.where(mask_cur, ws_ref[2 * m], 0.0).astype(BF16)
        w1 = jnp.where(mask_cur, ws_ref[2 * m + 1], 0.0).astype(BF16)
        wcat = jnp.concatenate([w0, w1], axis=1)
        for j in range(BLOCKS_PER_TILE):
            rows = slice(j * WINDOW, (j + 1) * WINDOW)
            vcol = vn_ref[rows, cs]
            rhs = jnp.concatenate([jnp.where(lo, vcol, jnp.zeros_like(vcol)),
                                   jnp.where(lo, jnp.zeros_like(vcol), vcol)], axis=0)
            sp = _dot(wcat, rhs) + bsf_ref[:, cs]
            u = _gelu(z_ref[rows, C_U + m * LANES:C_U + (m + 1) * LANES])
            mix_ref[rows, ATTN_WIDTH + m * LANES:ATTN_WIDTH + (m + 1) * LANES] = (u * sp).astype(BF16)

    def project_out(c):
        cols = slice(c * proj_cols, (c + 1) * proj_cols)
        xmid_ref[:, cols] = x_ref[:, cols] + _dot(mix_ref[...], wout_ref[:, cols])

    h_ref[...] = _rmsnorm(x_next_ref[...], g1_ref[...]).astype(BF16)
    project_next(0)
    project_next(1)
    for j in range(BLOCKS_PER_TILE):
        scores(j)
    vn_ref[...] = _layernorm(_gelu(z_ref[:, C_VG:C_VG + GMLP_WIDTH]), lng_ref[...], lnb_ref[...]).astype(BF16)
    project_next(2)
    softmax(0)
    project_next(3)
    softmax(1)
    project_next(4)
    softmax(2)
    project_next(5)
    softmax(3)
    project_next(6)
    for j in range(BLOCKS_PER_TILE):
        attend(j)
    for m in range(GMLP_GROUPS // 2):
        gmlp(m)
    for c in range(D_MODEL // proj_cols):
        project_out(c)
    h2_ref[...] = _rmsnorm(xmid_ref[...], g2_ref[...]).astype(BF16)
    d1, d2 = _sort_plan(_route(h2_ref[...], wrt_ref, brt_ref), upper_ref, lpad_ref, meta_ref, tab_ref)
    _sort_rows(d1, d2, h2_ref[...], xl_ref)

    for ref in kv_bufs:
        ref[0:WINDOW, :] = ref[TOK_TILE:TOK_TILE + WINDOW, :]


def _sort_plan(slab, upper_ref, lpad_ref, meta_ref, tab_ref):
    n_tok = slab.shape[1]
    e1, e2 = slab[0:1], slab[1:2]
    row32 = lax.broadcasted_iota(jnp.int32, (N_EXPERTS, n_tok), 0).astype(F32)
    sel1 = row32 == e1
    sel2 = row32 == e2
    onehot = jnp.where(sel1, 1.0, jnp.where(sel2, 1.0, 0.0))
    earlier = _dot(onehot.astype(BF16), upper_ref[...])
    cnt = jnp.sum(onehot, axis=1, keepdims=True)
    pc = jnp.floor((cnt + (ROW_GRANULE - 1)) * (1.0 / ROW_GRANULE)) * ROW_GRANULE
    pc_b = jnp.broadcast_to(pc, (N_EXPERTS, LANES))
    pc_pad = jnp.concatenate([pc_b, jnp.zeros((LANES - N_EXPERTS, LANES), F32)], axis=0).astype(BF16)
    start = _dot(lpad_ref[...], pc_pad)
    base = start[:, 0:1] + earlier
    d1 = jnp.sum(jnp.where(sel1, base, 0.0), axis=0, keepdims=True)
    d2 = jnp.sum(jnp.where(sel2, base, 0.0), axis=0, keepdims=True)
    row8 = lax.broadcasted_iota(jnp.int32, (8, n_tok), 0)
    meta_ref[...] = jnp.where(row8 == 0, d1, jnp.where(row8 == 1, d2, jnp.where(row8 >= 4, 0.0, slab)))
    lane = lax.broadcasted_iota(jnp.int32, (N_EXPERTS, LANES), 1)
    tab_ref[...] = jnp.where(lane == 0, pc_b, jnp.where(lane == 1, start, 0.0))
    return d1, d2


def _sort_rows(d1, d2, h2b, xl_ref):
    n_tok = h2b.shape[0]
    for c in range(SORT_ROWS // SORT_CHUNK):
        r_iota = (lax.broadcasted_iota(jnp.int32, (SORT_CHUNK, n_tok), 0) + c * SORT_CHUNK).astype(F32)
        perm = jnp.where(r_iota == d1, 1.0, jnp.where(r_iota == d2, 1.0, 0.0)).astype(BF16)
        xl_ref[c * SORT_CHUNK:(c + 1) * SORT_CHUNK, :] = _dot(perm, h2b).astype(BF16)


def _sample_mixer_kernel(sinks_ref, x_ref, ck_ref, cv_ref, g1_ref, win_ref, cos_ref, sin_ref, lng_ref, lnb_ref,
                         ws0_ref, bs0_ref, wout_ref, g2_ref, wrt_ref, brt_ref, upper_ref, lpad_ref, xl_in_hbm,
                         xmid_ref, meta_ref, tab_ref, kout_ref, vout_ref, vnout_ref, xl_hbm,
                         mix_ref, xl_tile, xl_sem):
    del xl_in_hbm
    n_seq = x_ref.shape[0]
    seq_chunk = 16
    x = x_ref[...]
    h = _rmsnorm(x, g1_ref[...]).astype(BF16)
    z = _dot(h, win_ref[...])
    cos = cos_ref[...]
    sin = sin_ref[...]
    scale = np.float32(HEAD_DIM ** -0.5)
    lane = lax.broadcasted_iota(jnp.int32, (n_seq, LANES), 1)
    lo = lane < HEAD_DIM
    kf = _rope(z[:, C_K:C_K + KV_WIDTH], cos, sin)
    vf = z[:, C_V:C_V + KV_WIDTH]
    kout_ref[...] = kf
    vout_ref[...] = vf
    kb = kf.astype(BF16).astype(F32)
    vb = vf.astype(BF16).astype(F32)

    q_heads = []
    for hd in range(N_Q_HEADS):
        m = hd // 2
        qc = _rope(z[:, C_Q + m * LANES:C_Q + (m + 1) * LANES], cos, sin) * scale
        keep = lo if hd % 2 == 0 else ~lo
        qm = jnp.where(keep, qc, 0.0)
        if (hd % 2) != (hd // (N_Q_HEADS // N_KV_HEADS)):
            qm = pltpu.roll(qm, HEAD_DIM, 1)
        q_heads.append(qm.astype(BF16))

    s_new = [jnp.sum(q_heads[hd].astype(F32) * kb, axis=-1, keepdims=True) for hd in range(N_Q_HEADS)]

    rr = lax.broadcasted_iota(jnp.int32, (N_Q_HEADS * seq_chunk, seq_chunk * WINDOW), 0)
    cc = lax.broadcasted_iota(jnp.int32, (N_Q_HEADS * seq_chunk, seq_chunk * WINDOW), 1)
    same_seq = (rr % seq_chunk) == (cc // WINDOW)
    kv_lo = lax.broadcasted_iota(jnp.int32, (seq_chunk, LANES), 1) < HEAD_DIM

    for c in range(n_seq // seq_chunk):
        sr = slice(c * seq_chunk, (c + 1) * seq_chunk)
        kc = ck_ref[sr].reshape(seq_chunk * WINDOW, KV_WIDTH).astype(BF16)
        vc = cv_ref[sr].reshape(seq_chunk * WINDOW, KV_WIDTH).astype(BF16)
        qs = jnp.concatenate([q_heads[hd][sr] for hd in range(N_Q_HEADS)], axis=0)
        s = jnp.where(same_seq, _dot_nt(qs, kc), NEG_INF)
        sn = jnp.concatenate([s_new[hd][sr] for hd in range(N_Q_HEADS)], axis=0)
        sink = jnp.concatenate([jnp.full((seq_chunk, 1), sinks_ref[hd], F32) for hd in range(N_Q_HEADS)], axis=0)
        m = jnp.maximum(jnp.maximum(jnp.max(s, axis=-1, keepdims=True), sn), sink)
        e = jnp.exp(s - m)
        en = jnp.exp(sn - m)
        inv = 1.0 / (jnp.sum(e, axis=-1, keepdims=True) + en + jnp.exp(sink - m))
        o = _dot((e * inv).astype(BF16), vc)
        pn = (en * inv).astype(BF16).astype(F32)
        for mcol in range(N_Q_HEADS // 2):
            halves = []
            for hd in (2 * mcol, 2 * mcol + 1):
                oh = o[hd * seq_chunk:(hd + 1) * seq_chunk] + pn[hd * seq_chunk:(hd + 1) * seq_chunk] * vb[sr]
                if (hd % 2) != (hd // (N_Q_HEADS // N_KV_HEADS)):
                    oh = pltpu.roll(oh, HEAD_DIM, 1)
                halves.append(oh)
            att = jnp.where(kv_lo, halves[0], halves[1])
            mix_ref[sr, mcol * LANES:(mcol + 1) * LANES] = att.astype(BF16)

    u = _gelu(z[:, C_U:C_U + GMLP_WIDTH])
    vn = _layernorm(_gelu(z[:, C_VG:C_VG + GMLP_WIDTH]), lng_ref[...], lnb_ref[...])
    vnout_ref[...] = vn
    sp = ws0_ref[...].astype(BF16).astype(F32) * vn.astype(BF16).astype(F32) + bs0_ref[...]
    mix_ref[:, ATTN_WIDTH:] = (u * sp).astype(BF16)

    xmid = x + _dot(mix_ref[...], wout_ref[...])
    xmid_ref[...] = xmid
    h2b = _rmsnorm(xmid, g2_ref[...]).astype(BF16)
    d1, d2 = _sort_plan(_route(h2b, wrt_ref, brt_ref), upper_ref, lpad_ref, meta_ref, tab_ref)
    _sort_rows(d1, d2, h2b, xl_tile)
    copy = pltpu.make_async_copy(xl_tile, xl_hbm.at[xl_hbm.shape[0] - 1], xl_sem.at[0])
    copy.start()
    copy.wait()


def _grouped_ffn_kernel(texp_ref, gsrc_ref, sdst_ref, ntiles_ref, xl_hbm, wg_ref, wu_ref, wd_ref, yl_hbm,
                        xbuf, ybuf, gsem, ssem):
    j = pl.program_id(0)
    n_tiles = ntiles_ref[0]

    def gather_copy(tile, s, b):
        return pltpu.make_async_copy(xl_hbm.at[gsrc_ref[tile * FFN_SLOTS + s]], xbuf.at[b, s], gsem.at[b])

    def scatter_copy(tile, s):
        b = (tile + SCATTER_BUFS) % SCATTER_BUFS
        return pltpu.make_async_copy(ybuf.at[b, s], yl_hbm.at[sdst_ref[(tile + 1) * FFN_SLOTS + s]], ssem.at[b])

    @pl.when(j < n_tiles)
    def _():
        gb = j % GATHER_BUFS
        ahead = GATHER_BUFS - 1
        nxt = j + ahead
        nxt_b = nxt % GATHER_BUFS

        @pl.when(j == 0)
        def _():
            ybuf[SCATTER_BUFS - 1] = jnp.zeros(ybuf.shape[1:], BF16)
            for k in range(ahead):
                for s in range(FFN_SLOTS):
                    gather_copy(k, s, k).start()

        for s in range(FFN_SLOTS):
            gather_copy(j, s, gb).wait()

        @pl.when(j >= SCATTER_BUFS - 1)
        def _():
            for s in range(FFN_SLOTS):
                scatter_copy(j - SCATTER_BUFS, s).wait()

        x = xbuf[gb].reshape(FFN_ROWS, D_MODEL)
        gate = _dot(x, wg_ref[...])
        up = _dot(x, wu_ref[...])
        hid = (gate * (1.0 / (1.0 + jnp.exp(-gate))) * up).astype(BF16)
        wd = wd_ref[...]
        for s in range(FFN_SLOTS):
            scatter_copy(j - 1, s).start()
        for s in range(FFN_SLOTS):
            gather_copy(nxt, s, nxt_b).start()
        ybuf[j % SCATTER_BUFS] = _dot(hid, wd).astype(BF16).reshape(FFN_SLOTS, ROW_GRANULE, D_MODEL)

        @pl.when(j == n_tiles - 1)
        def _():
            for k in range(1, GATHER_BUFS):
                for s in range(FFN_SLOTS):
                    gather_copy(nxt, s, (j + k) % GATHER_BUFS).wait()
            for s in range(FFN_SLOTS):
                scatter_copy(j, s).start()
            for back in range(SCATTER_BUFS):
                @pl.when(j - back >= -1)
                def _():
                    for s in range(FFN_SLOTS):
                        scatter_copy(j - back, s).wait()


def _combine_kernel(xmid_ref, yl_hbm, meta_ref, gf_ref, y_ref, ybuf, ysem, *, first_tile, n_tiles):
    n_tok = xmid_ref.shape[0]
    i = pl.program_id(0)
    ahead = COMBINE_BUFS - 1

    def fetch(tile):
        row0 = pl.multiple_of((first_tile + tile) * SORT_ROWS, SORT_ROWS)
        slot = tile % COMBINE_BUFS
        return pltpu.make_async_copy(yl_hbm.at[pl.ds(row0, SORT_ROWS), :], ybuf.at[slot], ysem.at[slot])

    @pl.when(i == 0)
    def _():
        for k in range(min(ahead, n_tiles)):
            fetch(k).start()

    @pl.when(i + ahead < n_tiles)
    def _():
        fetch(i + ahead).start()

    fetch(i).wait()
    yl_ref = ybuf.at[i % COMBINE_BUFS]
    meta = meta_ref[...]
    meta_t = jnp.concatenate([meta, jnp.zeros((LANES - 8, n_tok), F32)], axis=0).T
    half = max(n_tok // 2, LANES)
    for r0 in range(0, n_tok, half):
        rows = slice(r0, r0 + half)
        d1, d2, w1, w2 = (meta_t[rows, i:i + 1] for i in range(4))
        acc = xmid_ref[rows, :]
        for c in range(SORT_ROWS // SORT_CHUNK):
            r_iota = (lax.broadcasted_iota(jnp.int32, (half, SORT_CHUNK), 1) + c * SORT_CHUNK).astype(F32)
            unsort = jnp.where(r_iota == d1, w1, jnp.where(r_iota == d2, w2, 0.0)).astype(BF16)
            acc = acc + _dot(unsort, yl_ref[c * SORT_CHUNK:(c + 1) * SORT_CHUNK, :])
        y_ref[rows, :] = _rmsnorm(acc, gf_ref[...])


def _ffn_schedule(tab):
    n_tok_tiles = tab.shape[0]
    strips = (tab[:, :, 0] * (1.0 / ROW_GRANULE)).astype(jnp.int32)
    starts = (tab[:, :, 1] * (1.0 / ROW_GRANULE)).astype(jnp.int32)
    cnt = strips.T
    row0 = starts.T
    cs = jnp.cumsum(cnt, axis=1) - cnt
    n_str = jnp.sum(cnt, axis=1)
    np_str = (n_str + FFN_SLOTS - 1) // FFN_SLOTS * FFN_SLOTS
    ends = jnp.cumsum(np_str)
    base = ends - np_str
    n_steps_max = _ffn_steps_max(n_tok_tiles) + GATHER_BUFS - 1
    step0 = jnp.arange(n_steps_max, dtype=jnp.int32) * FFN_SLOTS
    stream = jnp.minimum(jnp.sum(ends[None, :] <= step0[:, None], axis=1), N_EXPERTS - 1)
    pick = stream[:, None] == jnp.arange(N_EXPERTS, dtype=jnp.int32)[None, :]
    sel = lambda x: jnp.sum(jnp.where(pick[:, :, None], x[None], 0), axis=1)
    cs_j, cnt_j, row0_j = sel(cs), sel(cnt), sel(row0)
    base_j = jnp.sum(jnp.where(pick, base[None, :], 0), axis=1)
    q = step0[:, None] + jnp.arange(FFN_SLOTS, dtype=jnp.int32)[None, :] - base_j[:, None]
    reached = cs_j[:, None, :] <= q[:, :, None]
    last = lambda x: jnp.sum(jnp.where(reached, jnp.diff(x, axis=1, prepend=0)[:, None, :], 0), axis=2)
    tile_idx = jnp.sum(reached, axis=2).astype(jnp.int32) - 1
    g = q - last(cs_j)
    valid = g < last(cnt_j)
    granule = tile_idx * GRANULES_PER_TILE + last(row0_j) + g
    assert n_tok_tiles >= 2 * FFN_SLOTS
    slot = jnp.arange(FFN_SLOTS, dtype=jnp.int32)[None, :]
    parity = jnp.arange(n_steps_max, dtype=jnp.int32)[:, None] % 2
    pad_dst = lambda par: (slot + par * FFN_SLOTS) * GRANULES_PER_TILE + GRANULES_PER_TILE - 1
    gsrc = jnp.where(valid, granule, slot * GRANULES_PER_TILE + GRANULES_PER_TILE - 2).astype(jnp.int32)
    sdst = jnp.where(valid, granule, pad_dst(parity))
    sdst = jnp.concatenate([pad_dst(1), sdst], axis=0)
    texp = stream.astype(jnp.int32)
    n_steps = (ends[-1] // FFN_SLOTS).astype(jnp.int32).reshape(1)
    return texp, gsrc.reshape(-1), sdst.astype(jnp.int32).reshape(-1), n_steps


assert GRANULES_PER_TILE - (2 * TOK_TILE + N_EXPERTS * (ROW_GRANULE - 1)) // ROW_GRANULE >= 2


def _ffn_steps_max(n_tok_tiles):
    return -(-(n_tok_tiles * GRANULES_PER_TILE + N_EXPERTS * (FFN_SLOTS - 1)) // FFN_SLOTS)


def _vmem_limit(mib):
    n_bytes = mib * 1024 * 1024
    assert n_bytes < V7X_VMEM_BYTES
    return n_bytes


def _rope_tables(pos):
    inv_freq = ROPE_THETA ** (-np.arange(HALF, dtype=np.float64) * 2.0 / HEAD_DIM)
    ang = np.asarray(pos, np.float64)[:, None] * inv_freq[None, :]
    cos, sin = np.cos(ang).astype(np.float32), np.sin(ang).astype(np.float32)
    reps = LANES // HEAD_DIM
    return (jnp.asarray(np.tile(np.concatenate([cos, cos], axis=1), (1, reps))),
            jnp.asarray(np.tile(np.concatenate([-sin, sin], axis=1), (1, reps))))


def _moe_combine(xmid, yl, meta, gf, n_tok, first_tile, n_tiles):
    return pl.pallas_call(
        functools.partial(_combine_kernel, first_tile=first_tile, n_tiles=n_tiles),
        grid=(n_tiles,),
        in_specs=[
            pl.BlockSpec((n_tok, D_MODEL), lambda i: (i, 0)),
            pl.BlockSpec(memory_space=pl.ANY),
            pl.BlockSpec((None, 8, n_tok), lambda i: (i, 0, 0)),
            pl.BlockSpec((1, D_MODEL), lambda i: (0, 0)),
        ],
        out_specs=pl.BlockSpec((n_tok, D_MODEL), lambda i: (i, 0)),
        out_shape=jax.ShapeDtypeStruct((n_tiles * n_tok, D_MODEL), F32),
        scratch_shapes=[pltpu.VMEM((COMBINE_BUFS, SORT_ROWS, D_MODEL), BF16), pltpu.SemaphoreType.DMA((COMBINE_BUFS,))],
        compiler_params=pltpu.CompilerParams(
            dimension_semantics=("arbitrary",),
            vmem_limit_bytes=_vmem_limit(40)),
        name="moe_combine",
    )(xmid, yl, meta, gf)


def kernel(x_prompt, x_sample, cache_swa_k, cache_swa_v, norm_mix_g, w_in, attn_sinks, gmlp_ln_g, gmlp_ln_b,
           gmlp_w_s, gmlp_b_s, w_out, norm_ffn_g, router_group_w, router_group_b, router_expert_w,
           router_expert_b, expert_w_gate, expert_w_up, expert_w_down, final_norm_g):
    assert norm_mix_g.shape[0] == 1, "single-layer trunk"
    batch, seq, _ = x_prompt.shape
    dec_batch = x_sample.shape[0]
    assert x_sample.shape[1] == 1 and seq % TOK_TILE == 0

    win_ext = w_in[0].astype(BF16)
    wout = w_out[0].astype(BF16)
    g1 = norm_mix_g[0][None, :]
    g2 = norm_ffn_g[0][None, :]
    gf = final_norm_g[None, :]
    lng = gmlp_ln_g[0][None, :]
    lnb = gmlp_ln_b[0][None, :]
    sinks = attn_sinks[0]
    ws = gmlp_w_s[0]
    group_dim = GMLP_WIDTH // GMLP_GROUPS
    bsf = jnp.repeat(gmlp_b_s[0].T, group_dim, axis=1)
    ws0 = jnp.repeat(ws[:, 0, 0], group_dim)[None, :]
    bs0 = jnp.repeat(gmlp_b_s[0][:, 0], group_dim)[None, :]
    pad_g, pad_e = 8 - N_GROUPS, ROUTER_ROWS - 8 - N_EXPERTS
    wrt = jnp.concatenate([router_group_w[0], jnp.zeros((D_MODEL, pad_g), F32),
                           router_expert_w[0].reshape(D_MODEL, N_EXPERTS), jnp.zeros((D_MODEL, pad_e), F32)],
                          axis=1).T.astype(BF16)
    brt = jnp.concatenate([router_group_b[0], jnp.full((pad_g,), NEG_INF, F32),
                           router_expert_b[0].reshape(N_EXPERTS), jnp.full((pad_e,), NEG_INF, F32)])[:, None]
    wg_f = expert_w_gate[0].reshape(N_EXPERTS * D_MODEL, D_EXPERT)
    wu_f = expert_w_up[0].reshape(N_EXPERTS * D_MODEL, D_EXPERT)
    wd_f = expert_w_down[0].reshape(N_EXPERTS * D_EXPERT, D_MODEL)
    cos_p, sin_p = _rope_tables(np.arange(seq))
    cos_s, sin_s = _rope_tables(PAST_LEN + np.arange(1))

    full = lambda shape: pl.BlockSpec(shape, lambda *_: (0,) * len(shape))
    smem = pl.BlockSpec(memory_space=pltpu.SMEM)
    n_tiles = seq // TOK_TILE

    upper = jnp.triu(jnp.ones((TOK_TILE, TOK_TILE), BF16), k=1)
    lpad = (jnp.arange(LANES)[None, :] < jnp.arange(N_EXPERTS)[:, None]).astype(BF16)
    n_tok_tiles = batch * n_tiles

    x2d = x_prompt.reshape(batch * seq, D_MODEL)
    tile_a = lambda s: jnp.minimum(s, n_tok_tiles - 1)
    cur = lambda s: jnp.maximum(s - 1, 0)
    tile_b = cur
    gu_rows, dn_rows = wg_f.shape[0] // n_tok_tiles, wd_f.shape[0] // n_tok_tiles
    assert gu_rows * n_tok_tiles == wg_f.shape[0] and dn_rows * n_tok_tiles == wd_f.shape[0] and dn_rows % 16 == 0
    xmid_p, xl_p, meta_p, tab_p, k_p, v_p, wg, wu, wd = pl.pallas_call(
        functools.partial(_prompt_mixer_kernel, tiles_per_seq=n_tiles),
        grid=(n_tok_tiles + 1,),
        in_specs=[
            smem,
            pl.BlockSpec((TOK_TILE, D_MODEL), lambda s: (tile_a(s), 0)),
            pl.BlockSpec((TOK_TILE, D_MODEL), lambda s: (cur(s), 0)),
            full((1, D_MODEL)),
            full((D_MODEL, IN_WIDTH)),
            pl.BlockSpec((TOK_TILE, LANES), lambda s: (tile_b(s) % n_tiles, 0)),
            pl.BlockSpec((TOK_TILE, LANES), lambda s: (tile_b(s) % n_tiles, 0)),
            full((1, GMLP_WIDTH)),
            full((1, GMLP_WIDTH)),
            full((GMLP_GROUPS, CHUNK, CHUNK)),
            full((CHUNK, GMLP_WIDTH)),
            full((D_MODEL, D_MODEL)),
            full((1, D_MODEL)),
            full((ROUTER_ROWS, D_MODEL)),
            full((ROUTER_ROWS, 1)),
            full((TOK_TILE, TOK_TILE)),
            full((N_EXPERTS, LANES)),
            pl.BlockSpec((gu_rows, D_EXPERT), lambda s: (tile_a(s), 0)),
            pl.BlockSpec((gu_rows, D_EXPERT), lambda s: (tile_a(s), 0)),
            pl.BlockSpec((dn_rows, D_MODEL), lambda s: (tile_a(s), 0)),
        ],
        out_specs=[
            pl.BlockSpec((TOK_TILE, D_MODEL), lambda s: (cur(s), 0)),
            pl.BlockSpec((None, SORT_ROWS, D_MODEL), lambda s: (jnp.where(s == 0, n_tok_tiles, s - 1), 0, 0)),
            pl.BlockSpec((None, 8, TOK_TILE), lambda s: (cur(s), 0, 0)),
            pl.BlockSpec((None, N_EXPERTS, LANES), lambda s: (cur(s), 0, 0)),
            pl.BlockSpec((None, WINDOW, KV_WIDTH), lambda s: (tile_b(s) // n_tiles, 0, 0)),
            pl.BlockSpec((None, WINDOW, KV_WIDTH), lambda s: (tile_b(s) // n_tiles, 0, 0)),
            pl.BlockSpec((gu_rows, D_EXPERT), lambda s: (tile_a(s), 0)),
            pl.BlockSpec((gu_rows, D_EXPERT), lambda s: (tile_a(s), 0)),
            pl.BlockSpec((dn_rows, D_MODEL), lambda s: (tile_a(s), 0)),
        ],
        out_shape=[
            jax.ShapeDtypeStruct((batch * seq, D_MODEL), F32),
            jax.ShapeDtypeStruct((n_tok_tiles + 1, SORT_ROWS, D_MODEL), BF16),
            jax.ShapeDtypeStruct((n_tok_tiles, 8, TOK_TILE), F32),
            jax.ShapeDtypeStruct((n_tok_tiles, N_EXPERTS, LANES), F32),
            jax.ShapeDtypeStruct((batch, WINDOW, KV_WIDTH), F32),
            jax.ShapeDtypeStruct((batch, WINDOW, KV_WIDTH), F32),
            jax.ShapeDtypeStruct(wg_f.shape, BF16),
            jax.ShapeDtypeStruct(wu_f.shape, BF16),
            jax.ShapeDtypeStruct(wd_f.shape, BF16),
        ],
        scratch_shapes=[
            pltpu.VMEM((TOK_TILE, IN_WIDTH), F32),
            pltpu.VMEM((TOK_TILE, IN_WIDTH), F32),
            pltpu.VMEM((TOK_TILE, D_MODEL), BF16),
        ] + [pltpu.VMEM((WINDOW + TOK_TILE, KV_WIDTH), BF16)] * 4 + [
            pltpu.VMEM((len(_NAT_HEADS), TOK_TILE, LANES), BF16),
            pltpu.VMEM((len(_SWP_HEADS), TOK_TILE, LANES), BF16),
            pltpu.VMEM((2 * BLOCKS_PER_TILE, len(_NAT_HEADS) * WINDOW, 2 * WINDOW), F32),
            pltpu.VMEM((2 * BLOCKS_PER_TILE, len(_NAT_HEADS) * WINDOW, 2 * WINDOW), BF16),
            pltpu.VMEM((TOK_TILE, GMLP_WIDTH), BF16),
            pltpu.VMEM((TOK_TILE, D_MODEL), BF16),
            pltpu.VMEM((TOK_TILE, D_MODEL), BF16),
        ],
        compiler_params=pltpu.CompilerParams(
            dimension_semantics=("arbitrary",),
            vmem_limit_bytes=_vmem_limit(56)),
        name="prompt_mixer",
    )(sinks, x2d, x2d, g1, win_ext, cos_p, sin_p, lng, lnb, ws, bsf, wout, g2, wrt, brt, upper, lpad,
      wg_f, wu_f, wd_f)
    wg = wg.reshape(N_EXPERTS, D_MODEL, D_EXPERT)
    wu = wu.reshape(N_EXPERTS, D_MODEL, D_EXPERT)
    wd = wd.reshape(N_EXPERTS, D_EXPERT, D_MODEL)

    xs = x_sample.reshape(dec_batch, D_MODEL)
    ck = cache_swa_k[0].reshape(dec_batch, WINDOW, KV_WIDTH)
    cv = cache_swa_v[0].reshape(dec_batch, WINDOW, KV_WIDTH)
    vmem = pl.BlockSpec(memory_space=pltpu.VMEM)
    hbm = pl.BlockSpec(memory_space=pl.ANY)
    xmid_s, meta_s, tab_s, k_s, v_s, vn_s, xl_all = pl.pallas_call(
        _sample_mixer_kernel,
        in_specs=[smem] + [vmem] * 17 + [hbm],
        out_specs=[vmem] * 6 + [hbm],
        out_shape=[
            jax.ShapeDtypeStruct((dec_batch, D_MODEL), F32),
            jax.ShapeDtypeStruct((8, dec_batch), F32),
            jax.ShapeDtypeStruct((N_EXPERTS, LANES), F32),
            jax.ShapeDtypeStruct((dec_batch, KV_WIDTH), F32),
            jax.ShapeDtypeStruct((dec_batch, KV_WIDTH), F32),
            jax.ShapeDtypeStruct((dec_batch, GMLP_WIDTH), F32),
            jax.ShapeDtypeStruct(xl_p.shape, BF16),
        ],
        scratch_shapes=[pltpu.VMEM((dec_batch, D_MODEL), BF16), pltpu.VMEM((SORT_ROWS, D_MODEL), BF16),
                        pltpu.SemaphoreType.DMA((1,))],
        compiler_params=pltpu.CompilerParams(vmem_limit_bytes=_vmem_limit(56)),
        input_output_aliases={18: 6},
        name="sample_mixer",
    )(sinks, xs, ck, cv, g1, win_ext, cos_s, sin_s, lng, lnb, ws0, bs0, wout, g2, wrt, brt,
      upper[:dec_batch, :dec_batch], lpad, xl_p)

    n_all_tiles = n_tok_tiles + 1
    tab_all = jnp.concatenate([tab_p, tab_s[None]], axis=0)
    texp, gsrc, sdst, n_steps = _ffn_schedule(tab_all)
    yl_all = pl.pallas_call(
        _grouped_ffn_kernel,
        grid_spec=pltpu.PrefetchScalarGridSpec(
            num_scalar_prefetch=4,
            grid=(_ffn_steps_max(n_all_tiles),),
            in_specs=[
                pl.BlockSpec(memory_space=pl.ANY),
                pl.BlockSpec((None, D_MODEL, D_EXPERT), lambda j, te, gs, sd, ns: (te[j], 0, 0)),
                pl.BlockSpec((None, D_MODEL, D_EXPERT), lambda j, te, gs, sd, ns: (te[j], 0, 0)),
                pl.BlockSpec((None, D_EXPERT, D_MODEL), lambda j, te, gs, sd, ns: (te[j], 0, 0)),
            ],
            out_specs=pl.BlockSpec(memory_space=pl.ANY),
            scratch_shapes=[
                pltpu.VMEM((GATHER_BUFS, FFN_SLOTS, ROW_GRANULE, D_MODEL), BF16),
                pltpu.VMEM((SCATTER_BUFS, FFN_SLOTS, ROW_GRANULE, D_MODEL), BF16),
                pltpu.SemaphoreType.DMA((GATHER_BUFS,)),
                pltpu.SemaphoreType.DMA((SCATTER_BUFS,)),
            ],
        ),
        out_shape=jax.ShapeDtypeStruct((n_all_tiles * GRANULES_PER_TILE, ROW_GRANULE, D_MODEL), BF16),
        compiler_params=pltpu.CompilerParams(
            dimension_semantics=("arbitrary",),
            vmem_limit_bytes=_vmem_limit(32)),
        input_output_aliases={4: 0},
        name="grouped_ffn",
    )(texp, gsrc, sdst, n_steps, xl_all.reshape(n_all_tiles * GRANULES_PER_TILE, ROW_GRANULE, D_MODEL), wg, wu, wd)
    yl_all = yl_all.reshape(n_all_tiles * SORT_ROWS, D_MODEL)

    y_p = _moe_combine(xmid_p, yl_all, meta_p, gf, TOK_TILE, 0, n_tok_tiles)
    y_s = _moe_combine(xmid_s, yl_all, meta_s[None], gf, dec_batch, n_tok_tiles, 1)

    return (y_p.reshape(batch, seq, D_MODEL),
            y_s.reshape(dec_batch, 1, D_MODEL),
            k_p.reshape(1, batch, WINDOW, N_KV_HEADS, HEAD_DIM),
            v_p.reshape(1, batch, WINDOW, N_KV_HEADS, HEAD_DIM),
            k_s.reshape(1, dec_batch, 1, N_KV_HEADS, HEAD_DIM),
            v_s.reshape(1, dec_batch, 1, N_KV_HEADS, HEAD_DIM),
            vn_s.reshape(1, dec_batch, 1, GMLP_WIDTH))
```

```python
import functools

import jax
import jax.numpy as jnp
import numpy as np
from jax import lax
from jax.experimental import pallas as pl
from jax.experimental.pallas import tpu as pltpu

F32 = jnp.float32
BF16 = jnp.bfloat16

D_MODEL = 1024
HEAD_DIM = 64
HALF = HEAD_DIM // 2
N_Q_HEADS = 8
N_KV_HEADS = 2
ATTN_WIDTH = N_Q_HEADS * HEAD_DIM
KV_WIDTH = N_KV_HEADS * HEAD_DIM
WINDOW = 128
ROPE_THETA = 10000.0
GMLP_WIDTH = D_MODEL - ATTN_WIDTH
GMLP_GROUPS = 8
CHUNK = 128
N_GROUPS = 4
EXPERTS_PER_GROUP = 8
N_EXPERTS = N_GROUPS * EXPERTS_PER_GROUP
D_EXPERT = 256
EPS = 1e-6
NEG_INF = -1e30
PAST_LEN = 16384

LANES = 128
V7X_VMEM_BYTES = 64 * 1024 * 1024

C_Q = 0
C_K = C_Q + ATTN_WIDTH
C_V = C_K + KV_WIDTH
C_U = C_V + KV_WIDTH
C_VG = C_U + GMLP_WIDTH
IN_WIDTH = C_VG + GMLP_WIDTH

ROUTER_ROWS = 48
TOK_TILE = 512
BLOCKS_PER_TILE = TOK_TILE // WINDOW
ROW_GRANULE = 16
SORT_CHUNK = 512
SORT_ROWS = -(-(2 * TOK_TILE + N_EXPERTS * (ROW_GRANULE - 1)) // SORT_CHUNK) * SORT_CHUNK
GRANULES_PER_TILE = SORT_ROWS // ROW_GRANULE
FFN_ROWS = 512
FFN_SLOTS = FFN_ROWS // ROW_GRANULE
GATHER_BUFS = 6
SCATTER_BUFS = 3
COMBINE_BUFS = 3


def _dot(a, b):
    return jnp.dot(a, b, preferred_element_type=F32)


def _dot_nt(a, b):
    return lax.dot_general(a, b, (((1,), (1,)), ((), ())), preferred_element_type=F32)


def _gelu(x):
    return 0.5 * x * (1.0 + lax.erf(x * np.float32(np.sqrt(0.5))))


def _rmsnorm(x, g):
    return x * lax.rsqrt(jnp.mean(x * x, axis=-1, keepdims=True) + EPS) * g


def _layernorm(x, g, b):
    mu = jnp.mean(x, axis=-1, keepdims=True)
    xc = x - mu
    return xc * lax.rsqrt(jnp.mean(xc * xc, axis=-1, keepdims=True) + EPS) * g + b


def _first_argmax_rows(x, row_iota, n_rows):
    m = jnp.max(x, axis=0, keepdims=True)
    idx = jnp.min(jnp.where(x == m, row_iota, n_rows), axis=0, keepdims=True)
    return m, idx


def _route(h2b, wrt_ref, brt_ref):
    n_tok = h2b.shape[0]
    lt = _dot_nt(wrt_ref[...], h2b) + brt_ref[...]
    row8 = lax.broadcasted_iota(jnp.int32, (8, n_tok), 0).astype(F32)
    glog = lt[0:8]
    gmax, gidx = _first_argmax_rows(glog, row8, 8)
    g_w = 1.0 / jnp.sum(jnp.exp(glog - gmax), axis=0, keepdims=True)
    esel = lt[8:16]
    for g in range(1, N_GROUPS):
        esel = jnp.where(gidx == g, lt[8 + 8 * g:16 + 8 * g], esel)
    m1, i1 = _first_argmax_rows(esel, row8, 8)
    esel2 = jnp.where(row8 == i1, -jnp.inf, esel)
    m2, i2 = _first_argmax_rows(esel2, row8, 8)
    r = jnp.exp(m2 - m1)
    w1 = 1.0 / (1.0 + r)
    w2 = r / (1.0 + r)
    e1 = gidx * EXPERTS_PER_GROUP + i1
    e2 = gidx * EXPERTS_PER_GROUP + i2
    slab = jnp.where(row8 == 0, e1, 0.0)
    for r_idx, val in ((1, e2), (2, w1 * g_w), (3, w2 * g_w)):
        slab = jnp.where(row8 == r_idx, val, slab)
    return slab


def _softmax_with_sink(s, sink):
    m = jnp.maximum(jnp.max(s, axis=-1, keepdims=True), sink)
    e = jnp.exp(s - m)
    den = jnp.sum(e, axis=-1, keepdims=True) + jnp.exp(sink - m)
    return e * (1.0 / den)


def _rope(x, cos, sin_signed):
    first_half = (lax.broadcasted_iota(jnp.int32, x.shape, 1) & HALF) == 0
    partner = jnp.where(first_half, pltpu.roll(x, LANES - HALF, 1), pltpu.roll(x, HALF, 1))
    return x * cos + partner * sin_signed


_NAT_HEADS = (0, 2, 5, 7)
_SWP_HEADS = (1, 3, 4, 6)


def _prompt_mixer_kernel(sinks_ref, x_next_ref, x_ref, g1_ref, win_ref, cos_ref, sin_ref, lng_ref, lnb_ref, ws_ref,
                         bsf_ref, wout_ref, g2_ref, wrt_ref, brt_ref, upper_ref, lpad_ref, wgf_ref, wuf_ref, wdf_ref,
                         xmid_ref, xl_ref, meta_ref, tab_ref, kout_ref, vout_ref, wgo_ref, wuo_ref, wdo_ref,
                         z_a, z_b, mix_ref, k_n, k_s, v_n, v_s, q_nat, q_swp, s_ref, p_ref, vn_ref, h_ref, h2_ref,
                         *, tiles_per_seq):
    step = pl.program_id(0)
    t = jnp.maximum(step - 1, 0) % tiles_per_seq
    kv_bufs = (k_n, k_s, v_n, v_s)

    @pl.when(step == 0)
    def _():
        z_b[...] = jnp.zeros_like(z_b)

    for src, dst in ((wgf_ref, wgo_ref), (wuf_ref, wuo_ref), (wdf_ref, wdo_ref)):
        dst[...] = src[...].astype(BF16)

    @pl.when(t == 0)
    def _():
        for ref in kv_bufs:
            ref[0:WINDOW, :] = jnp.zeros((WINDOW, KV_WIDTH), BF16)

    args = (t, sinks_ref, x_next_ref, x_ref, g1_ref, win_ref, cos_ref, sin_ref, lng_ref, lnb_ref, ws_ref, bsf_ref,
            wout_ref, g2_ref, wrt_ref, brt_ref, upper_ref, lpad_ref, xmid_ref, xl_ref, meta_ref, tab_ref, kout_ref,
            vout_ref, kv_bufs, q_nat, q_swp, s_ref, p_ref, vn_ref, h_ref, h2_ref)

    @pl.when(step % 2 == 0)
    def _():
        _prompt_mixer_body(z_a, z_b, mix_ref, *args)

    @pl.when(step % 2 == 1)
    def _():
        _prompt_mixer_body(z_b, z_a, mix_ref, *args)


def _prompt_mixer_body(z_next, z_ref, mix_ref, t, sinks_ref, x_next_ref, x_ref, g1_ref, win_ref, cos_ref,
                       sin_ref, lng_ref, lnb_ref, ws_ref, bsf_ref, wout_ref, g2_ref, wrt_ref, brt_ref, upper_ref,
                       lpad_ref, xmid_ref, xl_ref, meta_ref, tab_ref, kout_ref, vout_ref, kv_bufs,
                       q_nat, q_swp, s_ref, p_ref, vn_ref, h_ref, h2_ref):
    k_n, k_s, v_n, v_s = kv_bufs
    cos = cos_ref[...]
    sin = sin_ref[...]
    lane = lax.broadcasted_iota(jnp.int32, (WINDOW, LANES), 1)
    lo = lane < HEAD_DIM
    row = lax.broadcasted_iota(jnp.int32, (WINDOW, WINDOW), 0)
    col = lax.broadcasted_iota(jnp.int32, (WINDOW, WINDOW), 1)
    mask_cur = col <= row
    mask_prev_band = col >= row
    mask_prev_first = jnp.logical_and(mask_prev_band, (jnp.zeros_like(row) + t) > 0)
    mask_band = jnp.concatenate([mask_prev_band, mask_cur], axis=1)
    mask_first = jnp.concatenate([mask_prev_first, mask_cur], axis=1)

    cq = cos * np.float32(HEAD_DIM ** -0.5)
    sq = sin * np.float32(HEAD_DIM ** -0.5)
    kf = _rope(z_ref[:, C_K:C_K + KV_WIDTH], cos, sin)
    vf = z_ref[:, C_V:C_V + KV_WIDTH]
    k_n[WINDOW:, :] = kf.astype(BF16)
    k_s[WINDOW:, :] = pltpu.roll(kf, HEAD_DIM, 1).astype(BF16)
    v_n[WINDOW:, :] = vf.astype(BF16)
    v_s[WINDOW:, :] = pltpu.roll(vf, HEAD_DIM, 1).astype(BF16)

    kout_ref[...] = kf[TOK_TILE - WINDOW:]
    vout_ref[...] = vf[TOK_TILE - WINDOW:]

    lo_t = lax.broadcasted_iota(jnp.int32, (TOK_TILE, LANES), 1) < HEAD_DIM
    for m in range(N_Q_HEADS // 2):
        qc = _rope(z_ref[:, C_Q + m * LANES:C_Q + (m + 1) * LANES], cq, sq)
        for hd, qh in ((2 * m, jnp.where(lo_t, qc, 0.0)), (2 * m + 1, jnp.where(lo_t, 0.0, qc))):
            if hd in _NAT_HEADS:
                q_nat[_NAT_HEADS.index(hd)] = qh.astype(BF16)
            else:
                q_swp[_SWP_HEADS.index(hd)] = qh.astype(BF16)

    stacks = ((q_nat, _NAT_HEADS, k_n, v_n), (q_swp, _SWP_HEADS, k_s, v_s))
    proj_cols = 2 * LANES

    def project_next(c):
        cols = slice(c * proj_cols, (c + 1) * proj_cols)
        z_next[:, cols] = _dot(h_ref[...], win_ref[:, cols])

    def scores(j):
        rows = slice(j * WINDOW, (j + 1) * WINDOW)
        keys = slice(j * WINDOW, (j + 2) * WINDOW)
        for si, (q_ref, _, k_buf, _) in enumerate(stacks):
            q_stack = jnp.concatenate([q_ref[i, rows, :] for i in range(len(_NAT_HEADS))], axis=0)
            s_ref[2 * j + si] = _dot_nt(q_stack, k_buf[keys, :])

    def softmax(j):
        mask = mask_first if j == 0 else mask_band
        for si, (_, heads, _, _) in enumerate(stacks):
            for i, hd in enumerate(heads):
                pr = slice(i * WINDOW, (i + 1) * WINDOW)
                p_ref[2 * j + si, pr, :] = _softmax_with_sink(
                    jnp.where(mask, s_ref[2 * j + si, pr, :], NEG_INF), sinks_ref[hd]).astype(BF16)

    def attend(j):
        rows = slice(j * WINDOW, (j + 1) * WINDOW)
        keys = slice(j * WINDOW, (j + 2) * WINDOW)
        o_nat = _dot(p_ref[2 * j], v_n[keys, :])
        o_swp = _dot(p_ref[2 * j + 1], v_s[keys, :])
        for m in range(N_Q_HEADS // 2):
            pr = slice(m * WINDOW, (m + 1) * WINDOW)
            even_nat = (2 * m) in _NAT_HEADS
            att = jnp.where(lo, o_nat[pr], o_swp[pr]) if even_nat else jnp.where(lo, o_swp[pr], o_nat[pr])
            mix_ref[rows, m * LANES:(m + 1) * LANES] = att.astype(BF16)

    def gmlp(m):
        cs = slice(m * LANES, (m + 1) * LANES)
        w0 = jnp.where(mask_cur, ws_ref[2 * m], 0.0).astype(BF16)
        w1 = jnp.where(mask_cur, ws_ref[2 * m + 1], 0.0).astype(BF16)
        wcat = jnp.concatenate([w0, w1], axis=1)
        for j in range(BLOCKS_PER_TILE):
            rows = slice(j * WINDOW, (j + 1) * WINDOW)
            vcol = vn_ref[rows, cs]
            rhs = jnp.concatenate([jnp.where(lo, vcol, jnp.zeros_like(vcol)),
                                   jnp.where(lo, jnp.zeros_like(vcol), vcol)], axis=0)
            sp = _dot(wcat, rhs) + bsf_ref[:, cs]
            u = _gelu(z_ref[rows, C_U + m * LANES:C_U + (m + 1) * LANES])
            mix_ref[rows, ATTN_WIDTH + m * LANES:ATTN_WIDTH + (m + 1) * LANES] = (u * sp).astype(BF16)

    def project_out(c):
        cols = slice(c * proj_cols, (c + 1) * proj_cols)
        xmid_ref[:, cols] = x_ref[:, cols] + _dot(mix_ref[...], wout_ref[:, cols])

    h_ref[...] = _rmsnorm(x_next_ref[...], g1_ref[...]).astype(BF16)
    project_next(0)
    project_next(1)
    for j in range(BLOCKS_PER_TILE):
        scores(j)
    vn_ref[...] = _layernorm(_gelu(z_ref[:, C_VG:C_VG + GMLP_WIDTH]), lng_ref[...], lnb_ref[...]).astype(BF16)
    project_next(2)
    softmax(0)
    project_next(3)
    softmax(1)
    project_next(4)
    softmax(2)
    project_next(5)
    softmax(3)
    project_next(6)
    for j in range(BLOCKS_PER_TILE):
        attend(j)
    for m in range(GMLP_GROUPS // 2):
        gmlp(m)
    for c in range(D_MODEL // proj_cols):
        project_out(c)
    h2_ref[...] = _rmsnorm(xmid_ref[...], g2_ref[...]).astype(BF16)
    d1, d2 = _sort_plan(_route(h2_ref[...], wrt_ref, brt_ref), upper_ref, lpad_ref, meta_ref, tab_ref)
    _sort_rows(d1, d2, h2_ref[...], xl_ref)

    for ref in kv_bufs:
        ref[0:WINDOW, :] = ref[TOK_TILE:TOK_TILE + WINDOW, :]


def _sort_plan(slab, upper_ref, lpad_ref, meta_ref, tab_ref):
    n_tok = slab.shape[1]
    e1, e2 = slab[0:1], slab[1:2]
    row32 = lax.broadcasted_iota(jnp.int32, (N_EXPERTS, n_tok), 0).astype(F32)
    sel1 = row32 == e1
    sel2 = row32 == e2
    onehot = jnp.where(sel1, 1.0, jnp.where(sel2, 1.0, 0.0))
    earlier = _dot(onehot.astype(BF16), upper_ref[...])
    cnt = jnp.sum(onehot, axis=1, keepdims=True)
    pc = jnp.floor((cnt + (ROW_GRANULE - 1)) * (1.0 / ROW_GRANULE)) * ROW_GRANULE
    pc_b = jnp.broadcast_to(pc, (N_EXPERTS, LANES))
    pc_pad = jnp.concatenate([pc_b, jnp.zeros((LANES - N_EXPERTS, LANES), F32)], axis=0).astype(BF16)
    start = _dot(lpad_ref[...], pc_pad)
    base = start[:, 0:1] + earlier
    d1 = jnp.sum(jnp.where(sel1, base, 0.0), axis=0, keepdims=True)
    d2 = jnp.sum(jnp.where(sel2, base, 0.0), axis=0, keepdims=True)
    row8 = lax.broadcasted_iota(jnp.int32, (8, n_tok), 0)
    meta_ref[...] = jnp.where(row8 == 0, d1, jnp.where(row8 == 1, d2, jnp.where(row8 >= 4, 0.0, slab)))
    lane = lax.broadcasted_iota(jnp.int32, (N_EXPERTS, LANES), 1)
    tab_ref[...] = jnp.where(lane == 0, pc_b, jnp.where(lane == 1, start, 0.0))
    return d1, d2


def _sort_rows(d1, d2, h2b, xl_ref):
    n_tok = h2b.shape[0]
    for c in range(SORT_ROWS // SORT_CHUNK):
        r_iota = (lax.broadcasted_iota(jnp.int32, (SORT_CHUNK, n_tok), 0) + c * SORT_CHUNK).astype(F32)
        perm = jnp.where(r_iota == d1, 1.0, jnp.where(r_iota == d2, 1.0, 0.0)).astype(BF16)
        xl_ref[c * SORT_CHUNK:(c + 1) * SORT_CHUNK, :] = _dot(perm, h2b).astype(BF16)


def _sample_mixer_kernel(sinks_ref, x_ref, ck_ref, cv_ref, g1_ref, win_ref, cos_ref, sin_ref, lng_ref, lnb_ref,
                         ws0_ref, bs0_ref, wout_ref, g2_ref, wrt_ref, brt_ref, upper_ref, lpad_ref, xl_in_hbm,
                         xmid_ref, meta_ref, tab_ref, kout_ref, vout_ref, vnout_ref, xl_hbm,
                         mix_ref, xl_tile, xl_sem):
    del xl_in_hbm
    n_seq = x_ref.shape[0]
    seq_chunk = 16
    x = x_ref[...]
    h = _rmsnorm(x, g1_ref[...]).astype(BF16)
    z = _dot(h, win_ref[...])
    cos = cos_ref[...]
    sin = sin_ref[...]
    scale = np.float32(HEAD_DIM ** -0.5)
    lane = lax.broadcasted_iota(jnp.int32, (n_seq, LANES), 1)
    lo = lane < HEAD_DIM
    kf = _rope(z[:, C_K:C_K + KV_WIDTH], cos, sin)
    vf = z[:, C_V:C_V + KV_WIDTH]
    kout_ref[...] = kf
    vout_ref[...] = vf
    kb = kf.astype(BF16).astype(F32)
    vb = vf.astype(BF16).astype(F32)

    q_heads = []
    for hd in range(N_Q_HEADS):
        m = hd // 2
        qc = _rope(z[:, C_Q + m * LANES:C_Q + (m + 1) * LANES], cos, sin) * scale
        keep = lo if hd % 2 == 0 else ~lo
        qm = jnp.where(keep, qc, 0.0)
        if (hd % 2) != (hd // (N_Q_HEADS // N_KV_HEADS)):
            qm = pltpu.roll(qm, HEAD_DIM, 1)
        q_heads.append(qm.astype(BF16))

    s_new = [jnp.sum(q_heads[hd].astype(F32) * kb, axis=-1, keepdims=True) for hd in range(N_Q_HEADS)]

    rr = lax.broadcasted_iota(jnp.int32, (N_Q_HEADS * seq_chunk, seq_chunk * WINDOW), 0)
    cc = lax.broadcasted_iota(jnp.int32, (N_Q_HEADS * seq_chunk, seq_chunk * WINDOW), 1)
    same_seq = (rr % seq_chunk) == (cc // WINDOW)
    kv_lo = lax.broadcasted_iota(jnp.int32, (seq_chunk, LANES), 1) < HEAD_DIM

    for c in range(n_seq // seq_chunk):
        sr = slice(c * seq_chunk, (c + 1) * seq_chunk)
        kc = ck_ref[sr].reshape(seq_chunk * WINDOW, KV_WIDTH).astype(BF16)
        vc = cv_ref[sr].reshape(seq_chunk * WINDOW, KV_WIDTH).astype(BF16)
        qs = jnp.concatenate([q_heads[hd][sr] for hd in range(N_Q_HEADS)], axis=0)
        s = jnp.where(same_seq, _dot_nt(qs, kc), NEG_INF)
        sn = jnp.concatenate([s_new[hd][sr] for hd in range(N_Q_HEADS)], axis=0)
        sink = jnp.concatenate([jnp.full((seq_chunk, 1), sinks_ref[hd], F32) for hd in range(N_Q_HEADS)], axis=0)
        m = jnp.maximum(jnp.maximum(jnp.max(s, axis=-1, keepdims=True), sn), sink)
        e = jnp.exp(s - m)
        en = jnp.exp(sn - m)
        inv = 1.0 / (jnp.sum(e, axis=-1, keepdims=True) + en + jnp.exp(sink - m))
        o = _dot((e * inv).astype(BF16), vc)
        pn = (en * inv).astype(BF16).astype(F32)
        for mcol in range(N_Q_HEADS // 2):
            halves = []
            for hd in (2 * mcol, 2 * mcol + 1):
                oh = o[hd * seq_chunk:(hd + 1) * seq_chunk] + pn[hd * seq_chunk:(hd + 1) * seq_chunk] * vb[sr]
                if (hd % 2) != (hd // (N_Q_HEADS // N_KV_HEADS)):
                    oh = pltpu.roll(oh, HEAD_DIM, 1)
                halves.append(oh)
            att = jnp.where(kv_lo, halves[0], halves[1])
            mix_ref[sr, mcol * LANES:(mcol + 1) * LANES] = att.astype(BF16)

    u = _gelu(z[:, C_U:C_U + GMLP_WIDTH])
    vn = _layernorm(_gelu(z[:, C_VG:C_VG + GMLP_WIDTH]), lng_ref[...], lnb_ref[...])
    vnout_ref[...] = vn
    sp = ws0_ref[...].astype(BF16).astype(F32) * vn.astype(BF16).astype(F32) + bs0_ref[...]
    mix_ref[:, ATTN_WIDTH:] = (u * sp).astype(BF16)

    xmid = x + _dot(mix_ref[...], wout_ref[...])
    xmid_ref[...] = xmid
    h2b = _rmsnorm(xmid, g2_ref[...]).astype(BF16)
    d1, d2 = _sort_plan(_route(h2b, wrt_ref, brt_ref), upper_ref, lpad_ref, meta_ref, tab_ref)
    _sort_rows(d1, d2, h2b, xl_tile)
    copy = pltpu.make_async_copy(xl_tile, xl_hbm.at[xl_hbm.shape[0] - 1], xl_sem.at[0])
    copy.start()
    copy.wait()


def _grouped_ffn_kernel(texp_ref, gsrc_ref, sdst_ref, ntiles_ref, xl_hbm, wg_ref, wu_ref, wd_ref, yl_hbm,
                        xbuf, ybuf, gsem, ssem):
    j = pl.program_id(0)
    n_tiles = ntiles_ref[0]

    def gather_copy(tile, s, b):
        return pltpu.make_async_copy(xl_hbm.at[gsrc_ref[tile * FFN_SLOTS + s]], xbuf.at[b, s], gsem.at[b])

    def scatter_copy(tile, s):
        b = (tile + SCATTER_BUFS) % SCATTER_BUFS
        return pltpu.make_async_copy(ybuf.at[b, s], yl_hbm.at[sdst_ref[(tile + 1) * FFN_SLOTS + s]], ssem.at[b])

    @pl.when(j < n_tiles)
    def _():
        gb = j % GATHER_BUFS
        ahead = GATHER_BUFS - 1
        nxt = j + ahead
        nxt_b = nxt % GATHER_BUFS

        @pl.when(j == 0)
        def _():
            ybuf[SCATTER_BUFS - 1] = jnp.zeros(ybuf.shape[1:], BF16)
            for k in range(ahead):
                for s in range(FFN_SLOTS):
                    gather_copy(k, s, k).start()

        for s in range(FFN_SLOTS):
            gather_copy(j, s, gb).wait()

        @pl.when(j >= SCATTER_BUFS - 1)
        def _():
            for s in range(FFN_SLOTS):
                scatter_copy(j - SCATTER_BUFS, s).wait()

        x = xbuf[gb].reshape(FFN_ROWS, D_MODEL)
        gate = _dot(x, wg_ref[...])
        up = _dot(x, wu_ref[...])
        hid = (gate * (1.0 / (1.0 + jnp.exp(-gate))) * up).astype(BF16)
        wd = wd_ref[...]
        for s in range(FFN_SLOTS):
            scatter_copy(j - 1, s).start()
        for s in range(FFN_SLOTS):
            gather_copy(nxt, s, nxt_b).start()
        ybuf[j % SCATTER_BUFS] = _dot(hid, wd).astype(BF16).reshape(FFN_SLOTS, ROW_GRANULE, D_MODEL)

        @pl.when(j == n_tiles - 1)
        def _():
            for k in range(1, GATHER_BUFS):
                for s in range(FFN_SLOTS):
                    gather_copy(nxt, s, (j + k) % GATHER_BUFS).wait()
            for s in range(FFN_SLOTS):
                scatter_copy(j, s).start()
            for back in range(SCATTER_BUFS):
                @pl.when(j - back >= -1)
                def _():
                    for s in range(FFN_SLOTS):
                        scatter_copy(j - back, s).wait()


def _combine_kernel(xmid_ref, yl_hbm, meta_ref, gf_ref, y_ref, ybuf, ysem, *, first_tile, n_tiles):
    n_tok = xmid_ref.shape[0]
    i = pl.program_id(0)
    ahead = COMBINE_BUFS - 1

    def fetch(tile):
        row0 = pl.multiple_of((first_tile + tile) * SORT_ROWS, SORT_ROWS)
        slot = tile % COMBINE_BUFS
        return pltpu.make_async_copy(yl_hbm.at[pl.ds(row0, SORT_ROWS), :], ybuf.at[slot], ysem.at[slot])

    @pl.when(i == 0)
    def _():
        for k in range(min(ahead, n_tiles)):
            fetch(k).start()

    @pl.when(i + ahead < n_tiles)
    def _():
        fetch(i + ahead).start()

    fetch(i).wait()
    yl_ref = ybuf.at[i % COMBINE_BUFS]
    meta = meta_ref[...]
    meta_t = jnp.concatenate([meta, jnp.zeros((LANES - 8, n_tok), F32)], axis=0).T
    half = max(n_tok // 2, LANES)
    for r0 in range(0, n_tok, half):
        rows = slice(r0, r0 + half)
        d1, d2, w1, w2 = (meta_t[rows, i:i + 1] for i in range(4))
        acc = xmid_ref[rows, :]
        for c in range(SORT_ROWS // SORT_CHUNK):
            r_iota = (lax.broadcasted_iota(jnp.int32, (half, SORT_CHUNK), 1) + c * SORT_CHUNK).astype(F32)
            unsort = jnp.where(r_iota == d1, w1, jnp.where(r_iota == d2, w2, 0.0)).astype(BF16)
            acc = acc + _dot(unsort, yl_ref[c * SORT_CHUNK:(c + 1) * SORT_CHUNK, :])
        y_ref[rows, :] = _rmsnorm(acc, gf_ref[...])


def _ffn_schedule(tab):
    n_tok_tiles = tab.shape[0]
    strips = (tab[:, :, 0] * (1.0 / ROW_GRANULE)).astype(jnp.int32)
    starts = (tab[:, :, 1] * (1.0 / ROW_GRANULE)).astype(jnp.int32)
    cnt = strips.T
    row0 = starts.T
    cs = jnp.cumsum(cnt, axis=1) - cnt
    n_str = jnp.sum(cnt, axis=1)
    np_str = (n_str + FFN_SLOTS - 1) // FFN_SLOTS * FFN_SLOTS
    ends = jnp.cumsum(np_str)
    base = ends - np_str
    n_steps_max = _ffn_steps_max(n_tok_tiles) + GATHER_BUFS - 1
    step0 = jnp.arange(n_steps_max, dtype=jnp.int32) * FFN_SLOTS
    stream = jnp.minimum(jnp.sum(ends[None, :] <= step0[:, None], axis=1), N_EXPERTS - 1)
    pick = stream[:, None] == jnp.arange(N_EXPERTS, dtype=jnp.int32)[None, :]
    sel = lambda x: jnp.sum(jnp.where(pick[:, :, None], x[None], 0), axis=1)
    cs_j, cnt_j, row0_j = sel(cs), sel(cnt), sel(row0)
    base_j = jnp.sum(jnp.where(pick, base[None, :], 0), axis=1)
    q = step0[:, None] + jnp.arange(FFN_SLOTS, dtype=jnp.int32)[None, :] - base_j[:, None]
    reached = cs_j[:, None, :] <= q[:, :, None]
    last = lambda x: jnp.sum(jnp.where(reached, jnp.diff(x, axis=1, prepend=0)[:, None, :], 0), axis=2)
    tile_idx = jnp.sum(reached, axis=2).astype(jnp.int32) - 1
    g = q - last(cs_j)
    valid = g < last(cnt_j)
    granule = tile_idx * GRANULES_PER_TILE + last(row0_j) + g
    assert n_tok_tiles >= 2 * FFN_SLOTS
    slot = jnp.arange(FFN_SLOTS, dtype=jnp.int32)[None, :]
    parity = jnp.arange(n_steps_max, dtype=jnp.int32)[:, None] % 2
    pad_dst = lambda par: (slot + par * FFN_SLOTS) * GRANULES_PER_TILE + GRANULES_PER_TILE - 1
    gsrc = jnp.where(valid, granule, slot * GRANULES_PER_TILE + GRANULES_PER_TILE - 2).astype(jnp.int32)
    sdst = jnp.where(valid, granule, pad_dst(parity))
    sdst = jnp.concatenate([pad_dst(1), sdst], axis=0)
    texp = stream.astype(jnp.int32)
    n_steps = (ends[-1] // FFN_SLOTS).astype(jnp.int32).reshape(1)
    return texp, gsrc.reshape(-1), sdst.astype(jnp.int32).reshape(-1), n_steps


assert GRANULES_PER_TILE - (2 * TOK_TILE + N_EXPERTS * (ROW_GRANULE - 1)) // ROW_GRANULE >= 2


def _ffn_steps_max(n_tok_tiles):
    return -(-(n_tok_tiles * GRANULES_PER_TILE + N_EXPERTS * (FFN_SLOTS - 1)) // FFN_SLOTS)


def _vmem_limit(mib):
    n_bytes = mib * 1024 * 1024
    assert n_bytes < V7X_VMEM_BYTES
    return n_bytes


def _rope_tables(pos):
    inv_freq = ROPE_THETA ** (-np.arange(HALF, dtype=np.float64) * 2.0 / HEAD_DIM)
    ang = np.asarray(pos, np.float64)[:, None] * inv_freq[None, :]
    cos, sin = np.cos(ang).astype(np.float32), np.sin(ang).astype(np.float32)
    reps = LANES // HEAD_DIM
    return (jnp.asarray(np.tile(np.concatenate([cos, cos], axis=1), (1, reps))),
            jnp.asarray(np.tile(np.concatenate([-sin, sin], axis=1), (1, reps))))


def _moe_combine(xmid, yl, meta, gf, n_tok, first_tile, n_tiles):
    return pl.pallas_call(
        functools.partial(_combine_kernel, first_tile=first_tile, n_tiles=n_tiles),
        grid=(n_tiles,),
        in_specs=[
            pl.BlockSpec((n_tok, D_MODEL), lambda i: (i, 0)),
            pl.BlockSpec(memory_space=pl.ANY),
            pl.BlockSpec((None, 8, n_tok), lambda i: (i, 0, 0)),
            pl.BlockSpec((1, D_MODEL), lambda i: (0, 0)),
        ],
        out_specs=pl.BlockSpec((n_tok, D_MODEL), lambda i: (i, 0)),
        out_shape=jax.ShapeDtypeStruct((n_tiles * n_tok, D_MODEL), F32),
        scratch_shapes=[pltpu.VMEM((COMBINE_BUFS, SORT_ROWS, D_MODEL), BF16), pltpu.SemaphoreType.DMA((COMBINE_BUFS,))],
        compiler_params=pltpu.CompilerParams(
            dimension_semantics=("arbitrary",),
            vmem_limit_bytes=_vmem_limit(40)),
        name="moe_combine",
    )(xmid, yl, meta, gf)


def kernel(x_prompt, x_sample, cache_swa_k, cache_swa_v, norm_mix_g, w_in, attn_sinks, gmlp_ln_g, gmlp_ln_b,
           gmlp_w_s, gmlp_b_s, w_out, norm_ffn_g, router_group_w, router_group_b, router_expert_w,
           router_expert_b, expert_w_gate, expert_w_up, expert_w_down, final_norm_g):
    assert norm_mix_g.shape[0] == 1, "single-layer trunk"
    batch, seq, _ = x_prompt.shape
    dec_batch = x_sample.shape[0]
    assert x_sample.shape[1] == 1 and seq % TOK_TILE == 0

    win_ext = w_in[0].astype(BF16)
    wout = w_out[0].astype(BF16)
    g1 = norm_mix_g[0][None, :]
    g2 = norm_ffn_g[0][None, :]
    gf = final_norm_g[None, :]
    lng = gmlp_ln_g[0][None, :]
    lnb = gmlp_ln_b[0][None, :]
    sinks = attn_sinks[0]
    ws = gmlp_w_s[0]
    group_dim = GMLP_WIDTH // GMLP_GROUPS
    bsf = jnp.repeat(gmlp_b_s[0].T, group_dim, axis=1)
    ws0 = jnp.repeat(ws[:, 0, 0], group_dim)[None, :]
    bs0 = jnp.repeat(gmlp_b_s[0][:, 0], group_dim)[None, :]
    pad_g, pad_e = 8 - N_GROUPS, ROUTER_ROWS - 8 - N_EXPERTS
    wrt = jnp.concatenate([router_group_w[0], jnp.zeros((D_MODEL, pad_g), F32),
                           router_expert_w[0].reshape(D_MODEL, N_EXPERTS), jnp.zeros((D_MODEL, pad_e), F32)],
                          axis=1).T.astype(BF16)
    brt = jnp.concatenate([router_group_b[0], jnp.full((pad_g,), NEG_INF, F32),
                           router_expert_b[0].reshape(N_EXPERTS), jnp.full((pad_e,), NEG_INF, F32)])[:, None]
    wg_f = expert_w_gate[0].reshape(N_EXPERTS * D_MODEL, D_EXPERT)
    wu_f = expert_w_up[0].reshape(N_EXPERTS * D_MODEL, D_EXPERT)
    wd_f = expert_w_down[0].reshape(N_EXPERTS * D_EXPERT, D_MODEL)
    cos_p, sin_p = _rope_tables(np.arange(seq))
    cos_s, sin_s = _rope_tables(PAST_LEN + np.arange(1))

    full = lambda shape: pl.BlockSpec(shape, lambda *_: (0,) * len(shape))
    smem = pl.BlockSpec(memory_space=pltpu.SMEM)
    n_tiles = seq // TOK_TILE

    upper = jnp.triu(jnp.ones((TOK_TILE, TOK_TILE), BF16), k=1)
    lpad = (jnp.arange(LANES)[None, :] < jnp.arange(N_EXPERTS)[:, None]).astype(BF16)
    n_tok_tiles = batch * n_tiles

    x2d = x_prompt.reshape(batch * seq, D_MODEL)
    tile_a = lambda s: jnp.minimum(s, n_tok_tiles - 1)
    cur = lambda s: jnp.maximum(s - 1, 0)
    tile_b = cur
    gu_rows, dn_rows = wg_f.shape[0] // n_tok_tiles, wd_f.shape[0] // n_tok_tiles
    assert gu_rows * n_tok_tiles == wg_f.shape[0] and dn_rows * n_tok_tiles == wd_f.shape[0] and dn_rows % 16 == 0
    xmid_p, xl_p, meta_p, tab_p, k_p, v_p, wg, wu, wd = pl.pallas_call(
        functools.partial(_prompt_mixer_kernel, tiles_per_seq=n_tiles),
        grid=(n_tok_tiles + 1,),
        in_specs=[
            smem,
            pl.BlockSpec((TOK_TILE, D_MODEL), lambda s: (tile_a(s), 0)),
            pl.BlockSpec((TOK_TILE, D_MODEL), lambda s: (cur(s), 0)),
            full((1, D_MODEL)),
            full((D_MODEL, IN_WIDTH)),
            pl.BlockSpec((TOK_TILE, LANES), lambda s: (tile_b(s) % n_tiles, 0)),
            pl.BlockSpec((TOK_TILE, LANES), lambda s: (tile_b(s) % n_tiles, 0)),
            full((1, GMLP_WIDTH)),
            full((1, GMLP_WIDTH)),
            full((GMLP_GROUPS, CHUNK, CHUNK)),
            full((CHUNK, GMLP_WIDTH)),
            full((D_MODEL, D_MODEL)),
            full((1, D_MODEL)),
            full((ROUTER_ROWS, D_MODEL)),
            full((ROUTER_ROWS, 1)),
            full((TOK_TILE, TOK_TILE)),
            full((N_EXPERTS, LANES)),
            pl.BlockSpec((gu_rows, D_EXPERT), lambda s: (tile_a(s), 0)),
            pl.BlockSpec((gu_rows, D_EXPERT), lambda s: (tile_a(s), 0)),
            pl.BlockSpec((dn_rows, D_MODEL), lambda s: (tile_a(s), 0)),
        ],
        out_specs=[
            pl.BlockSpec((TOK_TILE, D_MODEL), lambda s: (cur(s), 0)),
            pl.BlockSpec((None, SORT_ROWS, D_MODEL), lambda s: (jnp.where(s == 0, n_tok_tiles, s - 1), 0, 0)),
            pl.BlockSpec((None, 8, TOK_TILE), lambda s: (cur(s), 0, 0)),
            pl.BlockSpec((None, N_EXPERTS, LANES), lambda s: (cur(s), 0, 0)),
            pl.BlockSpec((None, WINDOW, KV_WIDTH), lambda s: (tile_b(s) // n_tiles, 0, 0)),
            pl.BlockSpec((None, WINDOW, KV_WIDTH), lambda s: (tile_b(s) // n_tiles, 0, 0)),
            pl.BlockSpec((gu_rows, D_EXPERT), lambda s: (tile_a(s), 0)),
            pl.BlockSpec((gu_rows, D_EXPERT), lambda s: (tile_a(s), 0)),
            pl.BlockSpec((dn_rows, D_MODEL), lambda s: (tile_a(s), 0)),
        ],
        out_shape=[
            jax.ShapeDtypeStruct((batch * seq, D_MODEL), F32),
            jax.ShapeDtypeStruct((n_tok_tiles + 1, SORT_ROWS, D_MODEL), BF16),
            jax.ShapeDtypeStruct((n_tok_tiles, 8, TOK_TILE), F32),
            jax.ShapeDtypeStruct((n_tok_tiles, N_EXPERTS, LANES), F32),
            jax.ShapeDtypeStruct((batch, WINDOW, KV_WIDTH), F32),
            jax.ShapeDtypeStruct((batch, WINDOW, KV_WIDTH), F32),
            jax.ShapeDtypeStruct(wg_f.shape, BF16),
            jax.ShapeDtypeStruct(wu_f.shape, BF16),
            jax.ShapeDtypeStruct(wd_f.shape, BF16),
        ],
        scratch_shapes=[
            pltpu.VMEM((TOK_TILE, IN_WIDTH), F32),
            pltpu.VMEM((TOK_TILE, IN_WIDTH), F32),
            pltpu.VMEM((TOK_TILE, D_MODEL), BF16),
        ] + [pltpu.VMEM((WINDOW + TOK_TILE, KV_WIDTH), BF16)] * 4 + [
            pltpu.VMEM((len(_NAT_HEADS), TOK_TILE, LANES), BF16),
            pltpu.VMEM((len(_SWP_HEADS), TOK_TILE, LANES), BF16),
            pltpu.VMEM((2 * BLOCKS_PER_TILE, len(_NAT_HEADS) * WINDOW, 2 * WINDOW), F32),
            pltpu.VMEM((2 * BLOCKS_PER_TILE, len(_NAT_HEADS) * WINDOW, 2 * WINDOW), BF16),
            pltpu.VMEM((TOK_TILE, GMLP_WIDTH), BF16),
            pltpu.VMEM((TOK_TILE, D_MODEL), BF16),
            pltpu.VMEM((TOK_TILE, D_MODEL), BF16),
        ],
        compiler_params=pltpu.CompilerParams(
            dimension_semantics=("arbitrary",),
            vmem_limit_bytes=_vmem_limit(56)),
        name="prompt_mixer",
    )(sinks, x2d, x2d, g1, win_ext, cos_p, sin_p, lng, lnb, ws, bsf, wout, g2, wrt, brt, upper, lpad,
      wg_f, wu_f, wd_f)
    wg = wg.reshape(N_EXPERTS, D_MODEL, D_EXPERT)
    wu = wu.reshape(N_EXPERTS, D_MODEL, D_EXPERT)
    wd = wd.reshape(N_EXPERTS, D_EXPERT, D_MODEL)

    xs = x_sample.reshape(dec_batch, D_MODEL)
    ck = cache_swa_k[0].reshape(dec_batch, WINDOW, KV_WIDTH)
    cv = cache_swa_v[0].reshape(dec_batch, WINDOW, KV_WIDTH)
    vmem = pl.BlockSpec(memory_space=pltpu.VMEM)
    hbm = pl.BlockSpec(memory_space=pl.ANY)
    xmid_s, meta_s, tab_s, k_s, v_s, vn_s, xl_all = pl.pallas_call(
        _sample_mixer_kernel,
        in_specs=[smem] + [vmem] * 17 + [hbm],
        out_specs=[vmem] * 6 + [hbm],
        out_shape=[
            jax.ShapeDtypeStruct((dec_batch, D_MODEL), F32),
            jax.ShapeDtypeStruct((8, dec_batch), F32),
            jax.ShapeDtypeStruct((N_EXPERTS, LANES), F32),
            jax.ShapeDtypeStruct((dec_batch, KV_WIDTH), F32),
            jax.ShapeDtypeStruct((dec_batch, KV_WIDTH), F32),
            jax.ShapeDtypeStruct((dec_batch, GMLP_WIDTH), F32),
            jax.ShapeDtypeStruct(xl_p.shape, BF16),
        ],
        scratch_shapes=[pltpu.VMEM((dec_batch, D_MODEL), BF16), pltpu.VMEM((SORT_ROWS, D_MODEL), BF16),
                        pltpu.SemaphoreType.DMA((1,))],
        compiler_params=pltpu.CompilerParams(vmem_limit_bytes=_vmem_limit(56)),
        input_output_aliases={18: 6},
        name="sample_mixer",
    )(sinks, xs, ck, cv, g1, win_ext, cos_s, sin_s, lng, lnb, ws0, bs0, wout, g2, wrt, brt,
      upper[:dec_batch, :dec_batch], lpad, xl_p)

    n_all_tiles = n_tok_tiles + 1
    tab_all = jnp.concatenate([tab_p, tab_s[None]], axis=0)
    texp, gsrc, sdst, n_steps = _ffn_schedule(tab_all)
    yl_all = pl.pallas_call(
        _grouped_ffn_kernel,
        grid_spec=pltpu.PrefetchScalarGridSpec(
            num_scalar_prefetch=4,
            grid=(_ffn_steps_max(n_all_tiles),),
            in_specs=[
                pl.BlockSpec(memory_space=pl.ANY),
                pl.BlockSpec((None, D_MODEL, D_EXPERT), lambda j, te, gs, sd, ns: (te[j], 0, 0)),
                pl.BlockSpec((None, D_MODEL, D_EXPERT), lambda j, te, gs, sd, ns: (te[j], 0, 0)),
                pl.BlockSpec((None, D_EXPERT, D_MODEL), lambda j, te, gs, sd, ns: (te[j], 0, 0)),
            ],
            out_specs=pl.BlockSpec(memory_space=pl.ANY),
            scratch_shapes=[
                pltpu.VMEM((GATHER_BUFS, FFN_SLOTS, ROW_GRANULE, D_MODEL), BF16),
                pltpu.VMEM((SCATTER_BUFS, FFN_SLOTS, ROW_GRANULE, D_MODEL), BF16),
                pltpu.SemaphoreType.DMA((GATHER_BUFS,)),
                pltpu.SemaphoreType.DMA((SCATTER_BUFS,)),
            ],
        ),
        out_shape=jax.ShapeDtypeStruct((n_all_tiles * GRANULES_PER_TILE, ROW_GRANULE, D_MODEL), BF16),
        compiler_params=pltpu.CompilerParams(
            dimension_semantics=("arbitrary",),
            vmem_limit_bytes=_vmem_limit(32)),
        input_output_aliases={4: 0},
        name="grouped_ffn",
    )(texp, gsrc, sdst, n_steps, xl_all.reshape(n_all_tiles * GRANULES_PER_TILE, ROW_GRANULE, D_MODEL), wg, wu, wd)
    yl_all = yl_all.reshape(n_all_tiles * SORT_ROWS, D_MODEL)

    y_p = _moe_combine(xmid_p, yl_all, meta_p, gf, TOK_TILE, 0, n_tok_tiles)
    y_s = _moe_combine(xmid_s, yl_all, meta_s[None], gf, dec_batch, n_tok_tiles, 1)

    return (y_p.reshape(batch, seq, D_MODEL),
            y_s.reshape(dec_batch, 1, D_MODEL),
            k_p.reshape(1, batch, WINDOW, N_KV_HEADS, HEAD_DIM),
            v_p.reshape(1, batch, WINDOW, N_KV_HEADS, HEAD_DIM),
            k_s.reshape(1, dec_batch, 1, N_KV_HEADS, HEAD_DIM),
            v_s.reshape(1, dec_batch, 1, N_KV_HEADS, HEAD_DIM),
            vn_s.reshape(1, dec_batch, 1, GMLP_WIDTH))
```

```python
import functools

import jax
import jax.numpy as jnp
import numpy as np
from jax import lax
from jax.experimental import pallas as pl
from jax.experimental.pallas import tpu as pltpu

F32 = jnp.float32
BF16 = jnp.bfloat16

D_MODEL = 1024
HEAD_DIM = 64
HALF = HEAD_DIM // 2
N_Q_HEADS = 8
N_KV_HEADS = 2
ATTN_WIDTH = N_Q_HEADS * HEAD_DIM
KV_WIDTH = N_KV_HEADS * HEAD_DIM
WINDOW = 128
ROPE_THETA = 10000.0
GMLP_WIDTH = D_MODEL - ATTN_WIDTH
GMLP_GROUPS = 8
CHUNK = 128
N_GROUPS = 4
EXPERTS_PER_GROUP = 8
N_EXPERTS = N_GROUPS * EXPERTS_PER_GROUP
D_EXPERT = 256
EPS = 1e-6
NEG_INF = -1e30
PAST_LEN = 16384

LANES = 128
V7X_VMEM_BYTES = 64 * 1024 * 1024

C_Q = 0
C_K = C_Q + ATTN_WIDTH
C_V = C_K + KV_WIDTH
C_U = C_V + KV_WIDTH
C_VG = C_U + GMLP_WIDTH
IN_WIDTH = C_VG + GMLP_WIDTH

ROUTER_ROWS = 48
TOK_TILE = 512
BLOCKS_PER_TILE = TOK_TILE // WINDOW
ROW_GRANULE = 16
SORT_CHUNK = 512
SORT_ROWS = -(-(2 * TOK_TILE + N_EXPERTS * (ROW_GRANULE - 1)) // SORT_CHUNK) * SORT_CHUNK
GRANULES_PER_TILE = SORT_ROWS // ROW_GRANULE
FFN_ROWS = 512
FFN_SLOTS = FFN_ROWS // ROW_GRANULE
GATHER_BUFS = 4
SCATTER_BUFS = 3
COMBINE_BUFS = 3


def _dot(a, b):
    return jnp.dot(a, b, preferred_element_type=F32)


def _dot_nt(a, b):
    return lax.dot_general(a, b, (((1,), (1,)), ((), ())), preferred_element_type=F32)


def _gelu(x):
    return 0.5 * x * (1.0 + lax.erf(x * np.float32(np.sqrt(0.5))))


def _rmsnorm(x, g):
    return x * lax.rsqrt(jnp.mean(x * x, axis=-1, keepdims=True) + EPS) * g


def _layernorm(x, g, b):
    mu = jnp.mean(x, axis=-1, keepdims=True)
    xc = x - mu
    return xc * lax.rsqrt(jnp.mean(xc * xc, axis=-1, keepdims=True) + EPS) * g + b


def _first_argmax_rows(x, row_iota, n_rows):
    m = jnp.max(x, axis=0, keepdims=True)
    idx = jnp.min(jnp.where(x == m, row_iota, n_rows), axis=0, keepdims=True)
    return m, idx


def _route(h2b, wrt_ref, brt_ref):
    n_tok = h2b.shape[0]
    lt = _dot_nt(wrt_ref[...], h2b) + brt_ref[...]
    row8 = lax.broadcasted_iota(jnp.int32, (8, n_tok), 0).astype(F32)
    glog = lt[0:8]
    gmax, gidx = _first_argmax_rows(glog, row8, 8)
    g_w = 1.0 / jnp.sum(jnp.exp(glog - gmax), axis=0, keepdims=True)
    esel = lt[8:16]
    for g in range(1, N_GROUPS):
        esel = jnp.where(gidx == g, lt[8 + 8 * g:16 + 8 * g], esel)
    m1, i1 = _first_argmax_rows(esel, row8, 8)
    esel2 = jnp.where(row8 == i1, -jnp.inf, esel)
    m2, i2 = _first_argmax_rows(esel2, row8, 8)
    r = jnp.exp(m2 - m1)
    w1 = 1.0 / (1.0 + r)
    w2 = r / (1.0 + r)
    e1 = gidx * EXPERTS_PER_GROUP + i1
    e2 = gidx * EXPERTS_PER_GROUP + i2
    slab = jnp.where(row8 == 0, e1, 0.0)
    for r_idx, val in ((1, e2), (2, w1 * g_w), (3, w2 * g_w)):
        slab = jnp.where(row8 == r_idx, val, slab)
    return slab


def _softmax_with_sink(s, sink):
    m = jnp.maximum(jnp.max(s, axis=-1, keepdims=True), sink)
    e = jnp.exp(s - m)
    den = jnp.sum(e, axis=-1, keepdims=True) + jnp.exp(sink - m)
    return e * (1.0 / den)


def _rope(x, cos, sin_signed):
    first_half = (lax.broadcasted_iota(jnp.int32, x.shape, 1) & HALF) == 0
    partner = jnp.where(first_half, pltpu.roll(x, LANES - HALF, 1), pltpu.roll(x, HALF, 1))
    return x * cos + partner * sin_signed


_NAT_HEADS = (0, 2, 5, 7)
_SWP_HEADS = (1, 3, 4, 6)


def _prompt_mixer_kernel(sinks_ref, x_next_ref, x_ref, g1_ref, win_ref, cos_ref, sin_ref, lng_ref, lnb_ref, ws_ref,
                         bsf_ref, wout_ref, g2_ref, wrt_ref, brt_ref, upper_ref, lpad_ref, wgf_ref, wuf_ref, wdf_ref,
                         xmid_ref, xl_ref, meta_ref, tab_ref, kout_ref, vout_ref, wgo_ref, wuo_ref, wdo_ref,
                         z_a, z_b, mix_ref, k_n, k_s, v_n, v_s, q_nat, q_swp, s_ref, p_ref, vn_ref, h_ref, h2_ref,
                         *, tiles_per_seq):
    step = pl.program_id(0)
    t = jnp.maximum(step - 1, 0) % tiles_per_seq
    kv_bufs = (k_n, k_s, v_n, v_s)

    @pl.when(step == 0)
    def _():
        z_b[...] = jnp.zeros_like(z_b)

    for src, dst in ((wgf_ref, wgo_ref), (wuf_ref, wuo_ref), (wdf_ref, wdo_ref)):
        dst[...] = src[...].astype(BF16)

    @pl.when(t == 0)
    def _():
        for ref in kv_bufs:
            ref[0:WINDOW, :] = jnp.zeros((WINDOW, KV_WIDTH), BF16)

    args = (t, sinks_ref, x_next_ref, x_ref, g1_ref, win_ref, cos_ref, sin_ref, lng_ref, lnb_ref, ws_ref, bsf_ref,
            wout_ref, g2_ref, wrt_ref, brt_ref, upper_ref, lpad_ref, xmid_ref, xl_ref, meta_ref, tab_ref, kout_ref,
            vout_ref, kv_bufs, q_nat, q_swp, s_ref, p_ref, vn_ref, h_ref, h2_ref)

    @pl.when(step % 2 == 0)
    def _():
        _prompt_mixer_body(z_a, z_b, mix_ref, *args)

    @pl.when(step % 2 == 1)
    def _():
        _prompt_mixer_body(z_b, z_a, mix_ref, *args)


def _prompt_mixer_body(z_next, z_ref, mix_ref, t, sinks_ref, x_next_ref, x_ref, g1_ref, win_ref, cos_ref,
                       sin_ref, lng_ref, lnb_ref, ws_ref, bsf_ref, wout_ref, g2_ref, wrt_ref, brt_ref, upper_ref,
                       lpad_ref, xmid_ref, xl_ref, meta_ref, tab_ref, kout_ref, vout_ref, kv_bufs,
                       q_nat, q_swp, s_ref, p_ref, vn_ref, h_ref, h2_ref):
    k_n, k_s, v_n, v_s = kv_bufs
    cos = cos_ref[...]
    sin = sin_ref[...]
    lane = lax.broadcasted_iota(jnp.int32, (WINDOW, LANES), 1)
    lo = lane < HEAD_DIM
    row = lax.broadcasted_iota(jnp.int32, (WINDOW, WINDOW), 0)
    col = lax.broadcasted_iota(jnp.int32, (WINDOW, WINDOW), 1)
    mask_cur = col <= row
    mask_prev_band = col >= row
    mask_prev_first = jnp.logical_and(mask_prev_band, (jnp.zeros_like(row) + t) > 0)
    mask_band = jnp.concatenate([mask_prev_band, mask_cur], axis=1)
    mask_first = jnp.concatenate([mask_prev_first, mask_cur], axis=1)

    cq = cos * np.float32(HEAD_DIM ** -0.5)
    sq = sin * np.float32(HEAD_DIM ** -0.5)
    kf = _rope(z_ref[:, C_K:C_K + KV_WIDTH], cos, sin)
    vf = z_ref[:, C_V:C_V + KV_WIDTH]
    k_n[WINDOW:, :] = kf.astype(BF16)
    k_s[WINDOW:, :] = pltpu.roll(kf, HEAD_DIM, 1).astype(BF16)
    v_n[WINDOW:, :] = vf.astype(BF16)
    v_s[WINDOW:, :] = pltpu.roll(vf, HEAD_DIM, 1).astype(BF16)

    kout_ref[...] = kf[TOK_TILE - WINDOW:]
    vout_ref[...] = vf[TOK_TILE - WINDOW:]

    lo_t = lax.broadcasted_iota(jnp.int32, (TOK_TILE, LANES), 1) < HEAD_DIM
    for m in range(N_Q_HEADS // 2):
        qc = _rope(z_ref[:, C_Q + m * LANES:C_Q + (m + 1) * LANES], cq, sq)
        for hd, qh in ((2 * m, jnp.where(lo_t, qc, 0.0)), (2 * m + 1, jnp.where(lo_t, 0.0, qc))):
            if hd in _NAT_HEADS:
                q_nat[_NAT_HEADS.index(hd)] = qh.astype(BF16)
            else:
                q_swp[_SWP_HEADS.index(hd)] = qh.astype(BF16)

    stacks = ((q_nat, _NAT_HEADS, k_n, v_n), (q_swp, _SWP_HEADS, k_s, v_s))
    proj_cols = 2 * LANES

    def project_next(c):
        cols = slice(c * proj_cols, (c + 1) * proj_cols)
        z_next[:, cols] = _dot(h_ref[...], win_ref[:, cols])

    def scores(j):
        rows = slice(j * WINDOW, (j + 1) * WINDOW)
        keys = slice(j * WINDOW, (j + 2) * WINDOW)
        for si, (q_ref, _, k_buf, _) in enumerate(stacks):
            q_stack = jnp.concatenate([q_ref[i, rows, :] for i in range(len(_NAT_HEADS))], axis=0)
            s_ref[2 * j + si] = _dot_nt(q_stack, k_buf[keys, :])

    def softmax(j):
        mask = mask_first if j == 0 else mask_band
        for si, (_, heads, _, _) in enumerate(stacks):
            for i, hd in enumerate(heads):
                pr = slice(i * WINDOW, (i + 1) * WINDOW)
                p_ref[2 * j + si, pr, :] = _softmax_with_sink(
                    jnp.where(mask, s_ref[2 * j + si, pr, :], NEG_INF), sinks_ref[hd]).astype(BF16)

    def attend(j):
        rows = slice(j * WINDOW, (j + 1) * WINDOW)
        keys = slice(j * WINDOW, (j + 2) * WINDOW)
        o_nat = _dot(p_ref[2 * j], v_n[keys, :])
        o_swp = _dot(p_ref[2 * j + 1], v_s[keys, :])
        for m in range(N_Q_HEADS // 2):
            pr = slice(m * WINDOW, (m + 1) * WINDOW)
            even_nat = (2 * m) in _NAT_HEADS
            att = jnp.where(lo, o_nat[pr], o_swp[pr]) if even_nat else jnp.where(lo, o_swp[pr], o_nat[pr])
            mix_ref[rows, m * LANES:(m + 1) * LANES] = att.astype(BF16)

    def gmlp(m):
        cs = slice(m * LANES, (m + 1) * LANES)
        w0 = jnp.where(mask_cur, ws_ref[2 * m], 0.0).astype(BF16)
        w1 = jnp.where(mask_cur, ws_ref[2 * m + 1], 0.0).astype(BF16)
        wcat = jnp.concatenate([w0, w1], axis=1)
        for j in range(BLOCKS_PER_TILE):
            rows = slice(j * WINDOW, (j + 1) * WINDOW)
            vcol = vn_ref[rows, cs]
            rhs = jnp.concatenate([jnp.where(lo, vcol, jnp.zeros_like(vcol)),
                                   jnp.where(lo, jnp.zeros_like(vcol), vcol)], axis=0)
            sp = _dot(wcat, rhs) + bsf_ref[:, cs]
            u = _gelu(z_ref[rows, C_U + m * LANES:C_U + (m + 1) * LANES])
            mix_ref[rows, ATTN_WIDTH + m * LANES:ATTN_WIDTH + (m + 1) * LANES] = (u * sp).astype(BF16)

    def project_out(c):
        cols = slice(c * proj_cols, (c + 1) * proj_cols)
        xmid_ref[:, cols] = x_ref[:, cols] + _dot(mix_ref[...], wout_ref[:, cols])

    h_ref[...] = _rmsnorm(x_next_ref[...], g1_ref[...]).astype(BF16)
    project_next(0)
    project_next(1)
    for j in range(BLOCKS_PER_TILE):
        scores(j)
    vn_ref[...] = _layernorm(_gelu(z_ref[:, C_VG:C_VG + GMLP_WIDTH]), lng_ref[...], lnb_ref[...]).astype(BF16)
    project_next(2)
    softmax(0)
    project_next(3)
    softmax(1)
    project_next(4)
    softmax(2)
    project_next(5)
    softmax(3)
    project_next(6)
    for j in range(BLOCKS_PER_TILE):
        attend(j)
    for m in range(GMLP_GROUPS // 2):
        gmlp(m)
    for c in range(D_MODEL // proj_cols):
        project_out(c)
    h2_ref[...] = _rmsnorm(xmid_ref[...], g2_ref[...]).astype(BF16)
    d1, d2 = _sort_plan(_route(h2_ref[...], wrt_ref, brt_ref), upper_ref, lpad_ref, meta_ref, tab_ref)
    _sort_rows(d1, d2, h2_ref[...], xl_ref)

    for ref in kv_bufs:
        ref[0:WINDOW, :] = ref[TOK_TILE:TOK_TILE + WINDOW, :]


def _sort_plan(slab, upper_ref, lpad_ref, meta_ref, tab_ref):
    n_tok = slab.shape[1]
    e1, e2 = slab[0:1], slab[1:2]
    row32 = lax.broadcasted_iota(jnp.int32, (N_EXPERTS, n_tok), 0).astype(F32)
    sel1 = row32 == e1
    sel2 = row32 == e2
    onehot = jnp.where(sel1, 1.0, jnp.where(sel2, 1.0, 0.0))
    earlier = _dot(onehot.astype(BF16), upper_ref[...])
    cnt = jnp.sum(onehot, axis=1, keepdims=True)
    pc = jnp.floor((cnt + (ROW_GRANULE - 1)) * (1.0 / ROW_GRANULE)) * ROW_GRANULE
    pc_b = jnp.broadcast_to(pc, (N_EXPERTS, LANES))
    pc_pad = jnp.concatenate([pc_b, jnp.zeros((LANES - N_EXPERTS, LANES), F32)], axis=0).astype(BF16)
    start = _dot(lpad_ref[...], pc_pad)
    base = start[:, 0:1] + earlier
    d1 = jnp.sum(jnp.where(sel1, base, 0.0), axis=0, keepdims=True)
    d2 = jnp.sum(jnp.where(sel2, base, 0.0), axis=0, keepdims=True)
    row8 = lax.broadcasted_iota(jnp.int32, (8, n_tok), 0)
    meta_ref[...] = jnp.where(row8 == 0, d1, jnp.where(row8 == 1, d2, jnp.where(row8 >= 4, 0.0, slab)))
    lane = lax.broadcasted_iota(jnp.int32, (N_EXPERTS, LANES), 1)
    tab_ref[...] = jnp.where(lane == 0, pc_b, jnp.where(lane == 1, start, 0.0))
    return d1, d2


def _sort_rows(d1, d2, h2b, xl_ref):
    n_tok = h2b.shape[0]
    for c in range(SORT_ROWS // SORT_CHUNK):
        r_iota = (lax.broadcasted_iota(jnp.int32, (SORT_CHUNK, n_tok), 0) + c * SORT_CHUNK).astype(F32)
        perm = jnp.where(r_iota == d1, 1.0, jnp.where(r_iota == d2, 1.0, 0.0)).astype(BF16)
        xl_ref[c * SORT_CHUNK:(c + 1) * SORT_CHUNK, :] = _dot(perm, h2b).astype(BF16)


def _sample_mixer_kernel(sinks_ref, x_ref, ck_ref, cv_ref, g1_ref, win_ref, cos_ref, sin_ref, lng_ref, lnb_ref,
                         ws0_ref, bs0_ref, wout_ref, g2_ref, wrt_ref, brt_ref, upper_ref, lpad_ref, xl_in_hbm,
                         xmid_ref, meta_ref, tab_ref, kout_ref, vout_ref, vnout_ref, xl_hbm,
                         mix_ref, xl_tile, xl_sem):
    del xl_in_hbm
    n_seq = x_ref.shape[0]
    seq_chunk = 16
    x = x_ref[...]
    h = _rmsnorm(x, g1_ref[...]).astype(BF16)
    z = _dot(h, win_ref[...])
    cos = cos_ref[...]
    sin = sin_ref[...]
    scale = np.float32(HEAD_DIM ** -0.5)
    lane = lax.broadcasted_iota(jnp.int32, (n_seq, LANES), 1)
    lo = lane < HEAD_DIM
    kf = _rope(z[:, C_K:C_K + KV_WIDTH], cos, sin)
    vf = z[:, C_V:C_V + KV_WIDTH]
    kout_ref[...] = kf
    vout_ref[...] = vf
    kb = kf.astype(BF16).astype(F32)
    vb = vf.astype(BF16).astype(F32)

    q_heads = []
    for hd in range(N_Q_HEADS):
        m = hd // 2
        qc = _rope(z[:, C_Q + m * LANES:C_Q + (m + 1) * LANES], cos, sin) * scale
        keep = lo if hd % 2 == 0 else ~lo
        qm = jnp.where(keep, qc, 0.0)
        if (hd % 2) != (hd // (N_Q_HEADS // N_KV_HEADS)):
            qm = pltpu.roll(qm, HEAD_DIM, 1)
        q_heads.append(qm.astype(BF16))

    s_new = [jnp.sum(q_heads[hd].astype(F32) * kb, axis=-1, keepdims=True) for hd in range(N_Q_HEADS)]

    rr = lax.broadcasted_iota(jnp.int32, (N_Q_HEADS * seq_chunk, seq_chunk * WINDOW), 0)
    cc = lax.broadcasted_iota(jnp.int32, (N_Q_HEADS * seq_chunk, seq_chunk * WINDOW), 1)
    same_seq = (rr % seq_chunk) == (cc // WINDOW)
    kv_lo = lax.broadcasted_iota(jnp.int32, (seq_chunk, LANES), 1) < HEAD_DIM

    for c in range(n_seq // seq_chunk):
        sr = slice(c * seq_chunk, (c + 1) * seq_chunk)
        kc = ck_ref[sr].reshape(seq_chunk * WINDOW, KV_WIDTH).astype(BF16)
        vc = cv_ref[sr].reshape(seq_chunk * WINDOW, KV_WIDTH).astype(BF16)
        qs = jnp.concatenate([q_heads[hd][sr] for hd in range(N_Q_HEADS)], axis=0)
        s = jnp.where(same_seq, _dot_nt(qs, kc), NEG_INF)
        sn = jnp.concatenate([s_new[hd][sr] for hd in range(N_Q_HEADS)], axis=0)
        sink = jnp.concatenate([jnp.full((seq_chunk, 1), sinks_ref[hd], F32) for hd in range(N_Q_HEADS)], axis=0)
        m = jnp.maximum(jnp.maximum(jnp.max(s, axis=-1, keepdims=True), sn), sink)
        e = jnp.exp(s - m)
        en = jnp.exp(sn - m)
        inv = 1.0 / (jnp.sum(e, axis=-1, keepdims=True) + en + jnp.exp(sink - m))
        o = _dot((e * inv).astype(BF16), vc)
        pn = (en * inv).astype(BF16).astype(F32)
        for mcol in range(N_Q_HEADS // 2):
            halves = []
            for hd in (2 * mcol, 2 * mcol + 1):
                oh = o[hd * seq_chunk:(hd + 1) * seq_chunk] + pn[hd * seq_chunk:(hd + 1) * seq_chunk] * vb[sr]
                if (hd % 2) != (hd // (N_Q_HEADS // N_KV_HEADS)):
                    oh = pltpu.roll(oh, HEAD_DIM, 1)
                halves.append(oh)
            att = jnp.where(kv_lo, halves[0], halves[1])
            mix_ref[sr, mcol * LANES:(mcol + 1) * LANES] = att.astype(BF16)

    u = _gelu(z[:, C_U:C_U + GMLP_WIDTH])
    vn = _layernorm(_gelu(z[:, C_VG:C_VG + GMLP_WIDTH]), lng_ref[...], lnb_ref[...])
    vnout_ref[...] = vn
    sp = ws0_ref[...].astype(BF16).astype(F32) * vn.astype(BF16).astype(F32) + bs0_ref[...]
    mix_ref[:, ATTN_WIDTH:] = (u * sp).astype(BF16)

    xmid = x + _dot(mix_ref[...], wout_ref[...])
    xmid_ref[...] = xmid
    h2b = _rmsnorm(xmid, g2_ref[...]).astype(BF16)
    d1, d2 = _sort_plan(_route(h2b, wrt_ref, brt_ref), upper_ref, lpad_ref, meta_ref, tab_ref)
    _sort_rows(d1, d2, h2b, xl_tile)
    copy = pltpu.make_async_copy(xl_tile, xl_hbm.at[xl_hbm.shape[0] - 1], xl_sem.at[0])
    copy.start()
    copy.wait()


def _grouped_ffn_kernel(texp_ref, gsrc_ref, sdst_ref, ntiles_ref, xl_hbm, wg_ref, wu_ref, wd_ref, yl_hbm,
                        xbuf, ybuf, gsem, ssem):
    j = pl.program_id(0)
    n_tiles = ntiles_ref[0]

    def gather_copy(tile, s, b):
        return pltpu.make_async_copy(xl_hbm.at[gsrc_ref[tile * FFN_SLOTS + s]], xbuf.at[b, s], gsem.at[b])

    def scatter_copy(tile, s):
        b = (tile + SCATTER_BUFS) % SCATTER_BUFS
        return pltpu.make_async_copy(ybuf.at[b, s], yl_hbm.at[sdst_ref[(tile + 1) * FFN_SLOTS + s]], ssem.at[b])

    @pl.when(j < n_tiles)
    def _():
        gb = j % GATHER_BUFS
        ahead = GATHER_BUFS - 1
        nxt = j + ahead
        nxt_b = nxt % GATHER_BUFS

        @pl.when(j == 0)
        def _():
            ybuf[SCATTER_BUFS - 1] = jnp.zeros(ybuf.shape[1:], BF16)
            for k in range(ahead):
                for s in range(FFN_SLOTS):
                    gather_copy(k, s, k).start()

        for s in range(FFN_SLOTS):
            gather_copy(j, s, gb).wait()

        @pl.when(j >= SCATTER_BUFS - 1)
        def _():
            for s in range(FFN_SLOTS):
                scatter_copy(j - SCATTER_BUFS, s).wait()

        x = xbuf[gb].reshape(FFN_ROWS, D_MODEL)
        gate = _dot(x, wg_ref[...])
        up = _dot(x, wu_ref[...])
        hid = (gate * (1.0 / (1.0 + jnp.exp(-gate))) * up).astype(BF16)
        wd = wd_ref[...]
        for s in range(FFN_SLOTS):
            scatter_copy(j - 1, s).start()
        for s in range(FFN_SLOTS):
            gather_copy(nxt, s, nxt_b).start()
        ybuf[j % SCATTER_BUFS] = _dot(hid, wd).astype(BF16).reshape(FFN_SLOTS, ROW_GRANULE, D_MODEL)

        @pl.when(j == n_tiles - 1)
        def _():
            for k in range(1, GATHER_BUFS):
                for s in range(FFN_SLOTS):
                    gather_copy(nxt, s, (j + k) % GATHER_BUFS).wait()
            for s in range(FFN_SLOTS):
                scatter_copy(j, s).start()
            for back in range(SCATTER_BUFS):
                @pl.when(j - back >= -1)
                def _():
                    for s in range(FFN_SLOTS):
                        scatter_copy(j - back, s).wait()


def _combine_kernel(xmid_ref, yl_hbm, meta_ref, gf_ref, y_ref, ybuf, ysem, *, first_tile, n_tiles):
    n_tok = xmid_ref.shape[0]
    i = pl.program_id(0)
    ahead = COMBINE_BUFS - 1

    def fetch(tile):
        row0 = pl.multiple_of((first_tile + tile) * SORT_ROWS, SORT_ROWS)
        slot = tile % COMBINE_BUFS
        return pltpu.make_async_copy(yl_hbm.at[pl.ds(row0, SORT_ROWS), :], ybuf.at[slot], ysem.at[slot])

    @pl.when(i == 0)
    def _():
        for k in range(min(ahead, n_tiles)):
            fetch(k).start()

    @pl.when(i + ahead < n_tiles)
    def _():
        fetch(i + ahead).start()

    fetch(i).wait()
    yl_ref = ybuf.at[i % COMBINE_BUFS]
    meta = meta_ref[...]
    meta_t = jnp.concatenate([meta, jnp.zeros((LANES - 8, n_tok), F32)], axis=0).T
    half = max(n_tok // 2, LANES)
    tail0 = SORT_ROWS - SORT_CHUNK // 2
    tail_used = jnp.max(meta[0:2, :]) >= float(tail0)
    for r0 in range(0, n_tok, half):
        rows = slice(r0, r0 + half)
        d1, d2, w1, w2 = (meta_t[rows, i:i + 1] for i in range(4))

        def unsort_add(acc, c0, width):
            r_iota = (lax.broadcasted_iota(jnp.int32, (half, width), 1) + c0).astype(F32)
            unsort = jnp.where(r_iota == d1, w1, jnp.where(r_iota == d2, w2, 0.0)).astype(BF16)
            return acc + _dot(unsort, yl_ref[c0:c0 + width, :])

        acc = xmid_ref[rows, :]
        for c0 in range(0, SORT_ROWS - SORT_CHUNK, SORT_CHUNK):
            acc = unsort_add(acc, c0, SORT_CHUNK)
        acc = unsort_add(acc, SORT_ROWS - SORT_CHUNK, SORT_CHUNK // 2)
        acc = lax.cond(tail_used, lambda a: unsort_add(a, tail0, SORT_CHUNK // 2), lambda a: a, acc)
        y_ref[rows, :] = _rmsnorm(acc, gf_ref[...])


def _ffn_schedule(tab):
    n_tok_tiles = tab.shape[0]
    strips = (tab[:, :, 0] * (1.0 / ROW_GRANULE)).astype(jnp.int32)
    starts = (tab[:, :, 1] * (1.0 / ROW_GRANULE)).astype(jnp.int32)
    cnt = strips.T
    row0 = starts.T
    cs = jnp.cumsum(cnt, axis=1) - cnt
    n_str = jnp.sum(cnt, axis=1)
    np_str = (n_str + FFN_SLOTS - 1) // FFN_SLOTS * FFN_SLOTS
    ends = jnp.cumsum(np_str)
    base = ends - np_str
    n_steps_max = _ffn_steps_max(n_tok_tiles) + GATHER_BUFS - 1
    step0 = jnp.arange(n_steps_max, dtype=jnp.int32) * FFN_SLOTS
    stream = jnp.minimum(jnp.sum(ends[None, :] <= step0[:, None], axis=1), N_EXPERTS - 1)
    pick = stream[:, None] == jnp.arange(N_EXPERTS, dtype=jnp.int32)[None, :]
    sel = lambda x: jnp.sum(jnp.where(pick[:, :, None], x[None], 0), axis=1)
    cs_j, cnt_j, row0_j = sel(cs), sel(cnt), sel(row0)
    base_j = jnp.sum(jnp.where(pick, base[None, :], 0), axis=1)
    q = step0[:, None] + jnp.arange(FFN_SLOTS, dtype=jnp.int32)[None, :] - base_j[:, None]
    reached = cs_j[:, None, :] <= q[:, :, None]
    last = lambda x: jnp.sum(jnp.where(reached, jnp.diff(x, axis=1, prepend=0)[:, None, :], 0), axis=2)
    tile_idx = jnp.sum(reached, axis=2).astype(jnp.int32) - 1
    g = q - last(cs_j)
    valid = g < last(cnt_j)
    granule = tile_idx * GRANULES_PER_TILE + last(row0_j) + g
    assert n_tok_tiles >= 2 * FFN_SLOTS
    slot = jnp.arange(FFN_SLOTS, dtype=jnp.int32)[None, :]
    parity = jnp.arange(n_steps_max, dtype=jnp.int32)[:, None] % 2
    pad_dst = lambda par: (slot + par * FFN_SLOTS) * GRANULES_PER_TILE + GRANULES_PER_TILE - 1
    gsrc = jnp.where(valid, granule, slot * GRANULES_PER_TILE + GRANULES_PER_TILE - 2).astype(jnp.int32)
    sdst = jnp.where(valid, granule, pad_dst(parity))
    sdst = jnp.concatenate([pad_dst(1), sdst], axis=0)
    texp = stream.astype(jnp.int32)
    n_steps = (ends[-1] // FFN_SLOTS).astype(jnp.int32).reshape(1)
    return texp, gsrc.reshape(-1), sdst.astype(jnp.int32).reshape(-1), n_steps


assert GRANULES_PER_TILE - (2 * TOK_TILE + N_EXPERTS * (ROW_GRANULE - 1)) // ROW_GRANULE >= 2


def _ffn_steps_max(n_tok_tiles):
    return -(-(n_tok_tiles * GRANULES_PER_TILE + N_EXPERTS * (FFN_SLOTS - 1)) // FFN_SLOTS)


def _vmem_limit(mib):
    n_bytes = mib * 1024 * 1024
    assert n_bytes < V7X_VMEM_BYTES
    return n_bytes


def _rope_tables(pos):
    inv_freq = ROPE_THETA ** (-np.arange(HALF, dtype=np.float64) * 2.0 / HEAD_DIM)
    ang = np.asarray(pos, np.float64)[:, None] * inv_freq[None, :]
    cos, sin = np.cos(ang).astype(np.float32), np.sin(ang).astype(np.float32)
    reps = LANES // HEAD_DIM
    return (jnp.asarray(np.tile(np.concatenate([cos, cos], axis=1), (1, reps))),
            jnp.asarray(np.tile(np.concatenate([-sin, sin], axis=1), (1, reps))))


def _moe_combine(xmid, yl, meta, gf, n_tok, first_tile, n_tiles):
    return pl.pallas_call(
        functools.partial(_combine_kernel, first_tile=first_tile, n_tiles=n_tiles),
        grid=(n_tiles,),
        in_specs=[
            pl.BlockSpec((n_tok, D_MODEL), lambda i: (i, 0)),
            pl.BlockSpec(memory_space=pl.ANY),
            pl.BlockSpec((None, 8, n_tok), lambda i: (i, 0, 0)),
            pl.BlockSpec((1, D_MODEL), lambda i: (0, 0)),
        ],
        out_specs=pl.BlockSpec((n_tok, D_MODEL), lambda i: (i, 0)),
        out_shape=jax.ShapeDtypeStruct((n_tiles * n_tok, D_MODEL), F32),
        scratch_shapes=[pltpu.VMEM((COMBINE_BUFS, SORT_ROWS, D_MODEL), BF16), pltpu.SemaphoreType.DMA((COMBINE_BUFS,))],
        compiler_params=pltpu.CompilerParams(
            dimension_semantics=("arbitrary",),
            vmem_limit_bytes=_vmem_limit(40)),
        name="moe_combine",
    )(xmid, yl, meta, gf)


def kernel(x_prompt, x_sample, cache_swa_k, cache_swa_v, norm_mix_g, w_in, attn_sinks, gmlp_ln_g, gmlp_ln_b,
           gmlp_w_s, gmlp_b_s, w_out, norm_ffn_g, router_group_w, router_group_b, router_expert_w,
           router_expert_b, expert_w_gate, expert_w_up, expert_w_down, final_norm_g):
    assert norm_mix_g.shape[0] == 1, "single-layer trunk"
    batch, seq, _ = x_prompt.shape
    dec_batch = x_sample.shape[0]
    assert x_sample.shape[1] == 1 and seq % TOK_TILE == 0

    win_ext = w_in[0].astype(BF16)
    wout = w_out[0].astype(BF16)
    g1 = norm_mix_g[0][None, :]
    g2 = norm_ffn_g[0][None, :]
    gf = final_norm_g[None, :]
    lng = gmlp_ln_g[0][None, :]
    lnb = gmlp_ln_b[0][None, :]
    sinks = attn_sinks[0]
    ws = gmlp_w_s[0]
    group_dim = GMLP_WIDTH // GMLP_GROUPS
    bsf = jnp.repeat(gmlp_b_s[0].T, group_dim, axis=1)
    ws0 = jnp.repeat(ws[:, 0, 0], group_dim)[None, :]
    bs0 = jnp.repeat(gmlp_b_s[0][:, 0], group_dim)[None, :]
    pad_g, pad_e = 8 - N_GROUPS, ROUTER_ROWS - 8 - N_EXPERTS
    wrt = jnp.concatenate([router_group_w[0], jnp.zeros((D_MODEL, pad_g), F32),
                           router_expert_w[0].reshape(D_MODEL, N_EXPERTS), jnp.zeros((D_MODEL, pad_e), F32)],
                          axis=1).T.astype(BF16)
    brt = jnp.concatenate([router_group_b[0], jnp.full((pad_g,), NEG_INF, F32),
                           router_expert_b[0].reshape(N_EXPERTS), jnp.full((pad_e,), NEG_INF, F32)])[:, None]
    wg_f = expert_w_gate[0].reshape(N_EXPERTS * D_MODEL, D_EXPERT)
    wu_f = expert_w_up[0].reshape(N_EXPERTS * D_MODEL, D_EXPERT)
    wd_f = expert_w_down[0].reshape(N_EXPERTS * D_EXPERT, D_MODEL)
    cos_p, sin_p = _rope_tables(np.arange(seq))
    cos_s, sin_s = _rope_tables(PAST_LEN + np.arange(1))

    full = lambda shape: pl.BlockSpec(shape, lambda *_: (0,) * len(shape))
    smem = pl.BlockSpec(memory_space=pltpu.SMEM)
    n_tiles = seq // TOK_TILE

    upper = jnp.triu(jnp.ones((TOK_TILE, TOK_TILE), BF16), k=1)
    lpad = (jnp.arange(LANES)[None, :] < jnp.arange(N_EXPERTS)[:, None]).astype(BF16)
    n_tok_tiles = batch * n_tiles

    x2d = x_prompt.reshape(batch * seq, D_MODEL)
    tile_a = lambda s: jnp.minimum(s, n_tok_tiles - 1)
    cur = lambda s: jnp.maximum(s - 1, 0)
    tile_b = cur
    gu_rows, dn_rows = wg_f.shape[0] // n_tok_tiles, wd_f.shape[0] // n_tok_tiles
    assert gu_rows * n_tok_tiles == wg_f.shape[0] and dn_rows * n_tok_tiles == wd_f.shape[0] and dn_rows % 16 == 0
    xmid_p, xl_p, meta_p, tab_p, k_p, v_p, wg, wu, wd = pl.pallas_call(
        functools.partial(_prompt_mixer_kernel, tiles_per_seq=n_tiles),
        grid=(n_tok_tiles + 1,),
        in_specs=[
            smem,
            pl.BlockSpec((TOK_TILE, D_MODEL), lambda s: (tile_a(s), 0)),
            pl.BlockSpec((TOK_TILE, D_MODEL), lambda s: (cur(s), 0)),
            full((1, D_MODEL)),
            full((D_MODEL, IN_WIDTH)),
            pl.BlockSpec((TOK_TILE, LANES), lambda s: (tile_b(s) % n_tiles, 0)),
            pl.BlockSpec((TOK_TILE, LANES), lambda s: (tile_b(s) % n_tiles, 0)),
            full((1, GMLP_WIDTH)),
            full((1, GMLP_WIDTH)),
            full((GMLP_GROUPS, CHUNK, CHUNK)),
            full((CHUNK, GMLP_WIDTH)),
            full((D_MODEL, D_MODEL)),
            full((1, D_MODEL)),
            full((ROUTER_ROWS, D_MODEL)),
            full((ROUTER_ROWS, 1)),
            full((TOK_TILE, TOK_TILE)),
            full((N_EXPERTS, LANES)),
            pl.BlockSpec((gu_rows, D_EXPERT), lambda s: (tile_a(s), 0)),
            pl.BlockSpec((gu_rows, D_EXPERT), lambda s: (tile_a(s), 0)),
            pl.BlockSpec((dn_rows, D_MODEL), lambda s: (tile_a(s), 0)),
        ],
        out_specs=[
            pl.BlockSpec((TOK_TILE, D_MODEL), lambda s: (cur(s), 0)),
            pl.BlockSpec((None, SORT_ROWS, D_MODEL), lambda s: (jnp.where(s == 0, n_tok_tiles, s - 1), 0, 0)),
            pl.BlockSpec((None, 8, TOK_TILE), lambda s: (cur(s), 0, 0)),
            pl.BlockSpec((None, N_EXPERTS, LANES), lambda s: (cur(s), 0, 0)),
            pl.BlockSpec((None, WINDOW, KV_WIDTH), lambda s: (tile_b(s) // n_tiles, 0, 0)),
            pl.BlockSpec((None, WINDOW, KV_WIDTH), lambda s: (tile_b(s) // n_tiles, 0, 0)),
            pl.BlockSpec((gu_rows, D_EXPERT), lambda s: (tile_a(s), 0)),
            pl.BlockSpec((gu_rows, D_EXPERT), lambda s: (tile_a(s), 0)),
            pl.BlockSpec((dn_rows, D_MODEL), lambda s: (tile_a(s), 0)),
        ],
        out_shape=[
            jax.ShapeDtypeStruct((batch * seq, D_MODEL), F32),
            jax.ShapeDtypeStruct((n_tok_tiles + 1, SORT_ROWS, D_MODEL), BF16),
            jax.ShapeDtypeStruct((n_tok_tiles, 8, TOK_TILE), F32),
            jax.ShapeDtypeStruct((n_tok_tiles, N_EXPERTS, LANES), F32),
            jax.ShapeDtypeStruct((batch, WINDOW, KV_WIDTH), F32),
            jax.ShapeDtypeStruct((batch, WINDOW, KV_WIDTH), F32),
            jax.ShapeDtypeStruct(wg_f.shape, BF16),
            jax.ShapeDtypeStruct(wu_f.shape, BF16),
            jax.ShapeDtypeStruct(wd_f.shape, BF16),
        ],
        scratch_shapes=[
            pltpu.VMEM((TOK_TILE, IN_WIDTH), F32),
            pltpu.VMEM((TOK_TILE, IN_WIDTH), F32),
            pltpu.VMEM((TOK_TILE, D_MODEL), BF16),
        ] + [pltpu.VMEM((WINDOW + TOK_TILE, KV_WIDTH), BF16)] * 4 + [
            pltpu.VMEM((len(_NAT_HEADS), TOK_TILE, LANES), BF16),
            pltpu.VMEM((len(_SWP_HEADS), TOK_TILE, LANES), BF16),
            pltpu.VMEM((2 * BLOCKS_PER_TILE, len(_NAT_HEADS) * WINDOW, 2 * WINDOW), F32),
            pltpu.VMEM((2 * BLOCKS_PER_TILE, len(_NAT_HEADS) * WINDOW, 2 * WINDOW), BF16),
            pltpu.VMEM((TOK_TILE, GMLP_WIDTH), BF16),
            pltpu.VMEM((TOK_TILE, D_MODEL), BF16),
            pltpu.VMEM((TOK_TILE, D_MODEL), BF16),
        ],
        compiler_params=pltpu.CompilerParams(
            dimension_semantics=("arbitrary",),
            vmem_limit_bytes=_vmem_limit(56)),
        name="prompt_mixer",
    )(sinks, x2d, x2d, g1, win_ext, cos_p, sin_p, lng, lnb, ws, bsf, wout, g2, wrt, brt, upper, lpad,
      wg_f, wu_f, wd_f)
    wg = wg.reshape(N_EXPERTS, D_MODEL, D_EXPERT)
    wu = wu.reshape(N_EXPERTS, D_MODEL, D_EXPERT)
    wd = wd.reshape(N_EXPERTS, D_EXPERT, D_MODEL)

    xs = x_sample.reshape(dec_batch, D_MODEL)
    ck = cache_swa_k[0].reshape(dec_batch, WINDOW, KV_WIDTH)
    cv = cache_swa_v[0].reshape(dec_batch, WINDOW, KV_WIDTH)
    vmem = pl.BlockSpec(memory_space=pltpu.VMEM)
    hbm = pl.BlockSpec(memory_space=pl.ANY)
    xmid_s, meta_s, tab_s, k_s, v_s, vn_s, xl_all = pl.pallas_call(
        _sample_mixer_kernel,
        in_specs=[smem] + [vmem] * 17 + [hbm],
        out_specs=[vmem] * 6 + [hbm],
        out_shape=[
            jax.ShapeDtypeStruct((dec_batch, D_MODEL), F32),
            jax.ShapeDtypeStruct((8, dec_batch), F32),
            jax.ShapeDtypeStruct((N_EXPERTS, LANES), F32),
            jax.ShapeDtypeStruct((dec_batch, KV_WIDTH), F32),
            jax.ShapeDtypeStruct((dec_batch, KV_WIDTH), F32),
            jax.ShapeDtypeStruct((dec_batch, GMLP_WIDTH), F32),
            jax.ShapeDtypeStruct(xl_p.shape, BF16),
        ],
        scratch_shapes=[pltpu.VMEM((dec_batch, D_MODEL), BF16), pltpu.VMEM((SORT_ROWS, D_MODEL), BF16),
                        pltpu.SemaphoreType.DMA((1,))],
        compiler_params=pltpu.CompilerParams(vmem_limit_bytes=_vmem_limit(56)),
        input_output_aliases={18: 6},
        name="sample_mixer",
    )(sinks, xs, ck, cv, g1, win_ext, cos_s, sin_s, lng, lnb, ws0, bs0, wout, g2, wrt, brt,
      upper[:dec_batch, :dec_batch], lpad, xl_p)

    n_all_tiles = n_tok_tiles + 1
    tab_all = jnp.concatenate([tab_p, tab_s[None]], axis=0)
    texp, gsrc, sdst, n_steps = _ffn_schedule(tab_all)
    yl_all = pl.pallas_call(
        _grouped_ffn_kernel,
        grid_spec=pltpu.PrefetchScalarGridSpec(
            num_scalar_prefetch=4,
            grid=(_ffn_steps_max(n_all_tiles),),
            in_specs=[
                pl.BlockSpec(memory_space=pl.ANY),
                pl.BlockSpec((None, D_MODEL, D_EXPERT), lambda j, te, gs, sd, ns: (te[j], 0, 0)),
                pl.BlockSpec((None, D_MODEL, D_EXPERT), lambda j, te, gs, sd, ns: (te[j], 0, 0)),
                pl.BlockSpec((None, D_EXPERT, D_MODEL), lambda j, te, gs, sd, ns: (te[j], 0, 0)),
            ],
            out_specs=pl.BlockSpec(memory_space=pl.ANY),
            scratch_shapes=[
                pltpu.VMEM((GATHER_BUFS, FFN_SLOTS, ROW_GRANULE, D_MODEL), BF16),
                pltpu.VMEM((SCATTER_BUFS, FFN_SLOTS, ROW_GRANULE, D_MODEL), BF16),
                pltpu.SemaphoreType.DMA((GATHER_BUFS,)),
                pltpu.SemaphoreType.DMA((SCATTER_BUFS,)),
            ],
        ),
        out_shape=jax.ShapeDtypeStruct((n_all_tiles * GRANULES_PER_TILE, ROW_GRANULE, D_MODEL), BF16),
        compiler_params=pltpu.CompilerParams(
            dimension_semantics=("arbitrary",),
            vmem_limit_bytes=_vmem_limit(32)),
        input_output_aliases={4: 0},
        name="grouped_ffn",
    )(texp, gsrc, sdst, n_steps, xl_all.reshape(n_all_tiles * GRANULES_PER_TILE, ROW_GRANULE, D_MODEL), wg, wu, wd)
    yl_all = yl_all.reshape(n_all_tiles * SORT_ROWS, D_MODEL)

    y_p = _moe_combine(xmid_p, yl_all, meta_p, gf, TOK_TILE, 0, n_tok_tiles)
    y_s = _moe_combine(xmid_s, yl_all, meta_s[None], gf, dec_batch, n_tok_tiles, 1)

    return (y_p.reshape(batch, seq, D_MODEL),
            y_s.reshape(dec_batch, 1, D_MODEL),
            k_p.reshape(1, batch, WINDOW, N_KV_HEADS, HEAD_DIM),
            v_p.reshape(1, batch, WINDOW, N_KV_HEADS, HEAD_DIM),
            k_s.reshape(1, dec_batch, 1, N_KV_HEADS, HEAD_DIM),
            v_s.reshape(1, dec_batch, 1, N_KV_HEADS, HEAD_DIM),
            vn_s.reshape(1, dec_batch, 1, GMLP_WIDTH))
```

```python
import functools

import jax
import jax.numpy as jnp
import numpy as np
from jax import lax
from jax.experimental import pallas as pl
from jax.experimental.pallas import tpu as pltpu

F32 = jnp.float32
BF16 = jnp.bfloat16

D_MODEL = 1024
HEAD_DIM = 64
HALF = HEAD_DIM // 2
N_Q_HEADS = 8
N_KV_HEADS = 2
ATTN_WIDTH = N_Q_HEADS * HEAD_DIM
KV_WIDTH = N_KV_HEADS * HEAD_DIM
WINDOW = 128
ROPE_THETA = 10000.0
GMLP_WIDTH = D_MODEL - ATTN_WIDTH
GMLP_GROUPS = 8
CHUNK = 128
N_GROUPS = 4
EXPERTS_PER_GROUP = 8
N_EXPERTS = N_GROUPS * EXPERTS_PER_GROUP
D_EXPERT = 256
EPS = 1e-6
NEG_INF = -1e30
PAST_LEN = 16384

LANES = 128
V7X_VMEM_BYTES = 64 * 1024 * 1024

C_Q = 0
C_K = C_Q + ATTN_WIDTH
C_V = C_K + KV_WIDTH
C_U = C_V + KV_WIDTH
C_VG = C_U + GMLP_WIDTH
IN_WIDTH = C_VG + GMLP_WIDTH

ROUTER_ROWS = 48
TOK_TILE = 512
BLOCKS_PER_TILE = TOK_TILE // WINDOW
ROW_GRANULE = 16
SORT_CHUNK = 512
SORT_ROWS = -(-(2 * TOK_TILE + N_EXPERTS * (ROW_GRANULE - 1)) // SORT_CHUNK) * SORT_CHUNK
GRANULES_PER_TILE = SORT_ROWS // ROW_GRANULE
FFN_ROWS = 256
FFN_SLOTS = FFN_ROWS // ROW_GRANULE
GATHER_BUFS = 7
SCATTER_BUFS = 3
COMBINE_BUFS = 3


def _dot(a, b):
    return jnp.dot(a, b, preferred_element_type=F32)


def _dot_nt(a, b):
    return lax.dot_general(a, b, (((1,), (1,)), ((), ())), preferred_element_type=F32)


def _gelu(x):
    return 0.5 * x * (1.0 + lax.erf(x * np.float32(np.sqrt(0.5))))


def _rmsnorm(x, g):
    return x * lax.rsqrt(jnp.mean(x * x, axis=-1, keepdims=True) + EPS) * g


def _layernorm(x, g, b):
    mu = jnp.mean(x, axis=-1, keepdims=True)
    xc = x - mu
    return xc * lax.rsqrt(jnp.mean(xc * xc, axis=-1, keepdims=True) + EPS) * g + b


def _first_argmax_rows(x, row_iota, n_rows):
    m = jnp.max(x, axis=0, keepdims=True)
    idx = jnp.min(jnp.where(x == m, row_iota, n_rows), axis=0, keepdims=True)
    return m, idx


def _route(h2b, wrt_ref, brt_ref):
    n_tok = h2b.shape[0]
    lt = _dot_nt(wrt_ref[...], h2b) + brt_ref[...]
    row8 = lax.broadcasted_iota(jnp.int32, (8, n_tok), 0).astype(F32)
    glog = lt[0:8]
    gmax, gidx = _first_argmax_rows(glog, row8, 8)
    g_w = 1.0 / jnp.sum(jnp.exp(glog - gmax), axis=0, keepdims=True)
    esel = lt[8:16]
    for g in range(1, N_GROUPS):
        esel = jnp.where(gidx == g, lt[8 + 8 * g:16 + 8 * g], esel)
    m1, i1 = _first_argmax_rows(esel, row8, 8)
    esel2 = jnp.where(row8 == i1, -jnp.inf, esel)
    m2, i2 = _first_argmax_rows(esel2, row8, 8)
    r = jnp.exp(m2 - m1)
    w1 = 1.0 / (1.0 + r)
    w2 = r / (1.0 + r)
    e1 = gidx * EXPERTS_PER_GROUP + i1
    e2 = gidx * EXPERTS_PER_GROUP + i2
    slab = jnp.where(row8 == 0, e1, 0.0)
    for r_idx, val in ((1, e2), (2, w1 * g_w), (3, w2 * g_w)):
        slab = jnp.where(row8 == r_idx, val, slab)
    return slab


def _softmax_with_sink(s, sink):
    m = jnp.maximum(jnp.max(s, axis=-1, keepdims=True), sink)
    e = jnp.exp(s - m)
    den = jnp.sum(e, axis=-1, keepdims=True) + jnp.exp(sink - m)
    return e * (1.0 / den)


def _rope(x, cos, sin_signed):
    first_half = (lax.broadcasted_iota(jnp.int32, x.shape, 1) & HALF) == 0
    partner = jnp.where(first_half, pltpu.roll(x, LANES - HALF, 1), pltpu.roll(x, HALF, 1))
    return x * cos + partner * sin_signed


_NAT_HEADS = (0, 2, 5, 7)
_SWP_HEADS = (1, 3, 4, 6)


def _prompt_mixer_kernel(sinks_ref, x_next_ref, x_ref, g1_ref, win_ref, cos_ref, sin_ref, lng_ref, lnb_ref, ws_ref,
                         bsf_ref, wout_ref, g2_ref, wrt_ref, brt_ref, upper_ref, lpad_ref, wgf_ref, wuf_ref, wdf_ref,
                         xmid_ref, xl_ref, meta_ref, tab_ref, kout_ref, vout_ref, wgo_ref, wuo_ref, wdo_ref,
                         z_a, z_b, mix_ref, k_n, k_s, v_n, v_s, q_nat, q_swp, s_ref, p_ref, vn_ref, h_ref, h2_ref,
                         *, tiles_per_seq):
    step = pl.program_id(0)
    t = jnp.maximum(step - 1, 0) % tiles_per_seq
    kv_bufs = (k_n, k_s, v_n, v_s)

    @pl.when(step == 0)
    def _():
        z_b[...] = jnp.zeros_like(z_b)

    for src, dst in ((wgf_ref, wgo_ref), (wuf_ref, wuo_ref), (wdf_ref, wdo_ref)):
        dst[...] = src[...].astype(BF16)

    @pl.when(t == 0)
    def _():
        for ref in kv_bufs:
            ref[0:WINDOW, :] = jnp.zeros((WINDOW, KV_WIDTH), BF16)

    args = (t, sinks_ref, x_next_ref, x_ref, g1_ref, win_ref, cos_ref, sin_ref, lng_ref, lnb_ref, ws_ref, bsf_ref,
            wout_ref, g2_ref, wrt_ref, brt_ref, upper_ref, lpad_ref, xmid_ref, xl_ref, meta_ref, tab_ref, kout_ref,
            vout_ref, kv_bufs, q_nat, q_swp, s_ref, p_ref, vn_ref, h_ref, h2_ref)

    @pl.when(step % 2 == 0)
    def _():
        _prompt_mixer_body(z_a, z_b, mix_ref, *args)

    @pl.when(step % 2 == 1)
    def _():
        _prompt_mixer_body(z_b, z_a, mix_ref, *args)


def _prompt_mixer_body(z_next, z_ref, mix_ref, t, sinks_ref, x_next_ref, x_ref, g1_ref, win_ref, cos_ref,
                       sin_ref, lng_ref, lnb_ref, ws_ref, bsf_ref, wout_ref, g2_ref, wrt_ref, brt_ref, upper_ref,
                       lpad_ref, xmid_ref, xl_ref, meta_ref, tab_ref, kout_ref, vout_ref, kv_bufs,
                       q_nat, q_swp, s_ref, p_ref, vn_ref, h_ref, h2_ref):
    k_n, k_s, v_n, v_s = kv_bufs
    cos = cos_ref[...]
    sin = sin_ref[...]
    lane = lax.broadcasted_iota(jnp.int32, (WINDOW, LANES), 1)
    lo = lane < HEAD_DIM
    row = lax.broadcasted_iota(jnp.int32, (WINDOW, WINDOW), 0)
    col = lax.broadcasted_iota(jnp.int32, (WINDOW, WINDOW), 1)
    mask_cur = col <= row
    mask_prev_band = col >= row
    mask_prev_first = jnp.logical_and(mask_prev_band, (jnp.zeros_like(row) + t) > 0)
    mask_band = jnp.concatenate([mask_prev_band, mask_cur], axis=1)
    mask_first = jnp.concatenate([mask_prev_first, mask_cur], axis=1)

    cq = cos * np.float32(HEAD_DIM ** -0.5)
    sq = sin * np.float32(HEAD_DIM ** -0.5)
    kf = _rope(z_ref[:, C_K:C_K + KV_WIDTH], cos, sin)
    vf = z_ref[:, C_V:C_V + KV_WIDTH]
    k_n[WINDOW:, :] = kf.astype(BF16)
    k_s[WINDOW:, :] = pltpu.roll(kf, HEAD_DIM, 1).astype(BF16)
    v_n[WINDOW:, :] = vf.astype(BF16)
    v_s[WINDOW:, :] = pltpu.roll(vf, HEAD_DIM, 1).astype(BF16)

    kout_ref[...] = kf[TOK_TILE - WINDOW:]
    vout_ref[...] = vf[TOK_TILE - WINDOW:]

    lo_t = lax.broadcasted_iota(jnp.int32, (TOK_TILE, LANES), 1) < HEAD_DIM
    for m in range(N_Q_HEADS // 2):
        qc = _rope(z_ref[:, C_Q + m * LANES:C_Q + (m + 1) * LANES], cq, sq)
        for hd, qh in ((2 * m, jnp.where(lo_t, qc, 0.0)), (2 * m + 1, jnp.where(lo_t, 0.0, qc))):
            if hd in _NAT_HEADS:
                q_nat[_NAT_HEADS.index(hd)] = qh.astype(BF16)
            else:
                q_swp[_SWP_HEADS.index(hd)] = qh.astype(BF16)

    stacks = ((q_nat, _NAT_HEADS, k_n, v_n), (q_swp, _SWP_HEADS, k_s, v_s))
    proj_cols = 2 * LANES

    def project_next(c):
        cols = slice(c * proj_cols, (c + 1) * proj_cols)
        z_next[:, cols] = _dot(h_ref[...], win_ref[:, cols])

    def scores(j):
        rows = slice(j * WINDOW, (j + 1) * WINDOW)
        keys = slice(j * WINDOW, (j + 2) * WINDOW)
        for si, (q_ref, _, k_buf, _) in enumerate(stacks):
            q_stack = jnp.concatenate([q_ref[i, rows, :] for i in range(len(_NAT_HEADS))], axis=0)
            s_ref[2 * j + si] = _dot_nt(q_stack, k_buf[keys, :])

    def softmax(j):
        mask = mask_first if j == 0 else mask_band
        for si, (_, heads, _, _) in enumerate(stacks):
            for i, hd in enumerate(heads):
                pr = slice(i * WINDOW, (i + 1) * WINDOW)
                p_ref[2 * j + si, pr, :] = _softmax_with_sink(
                    jnp.where(mask, s_ref[2 * j + si, pr, :], NEG_INF), sinks_ref[hd]).astype(BF16)

    def attend(j):
        rows = slice(j * WINDOW, (j + 1) * WINDOW)
        keys = slice(j * WINDOW, (j + 2) * WINDOW)
        o_nat = _dot(p_ref[2 * j], v_n[keys, :])
        o_swp = _dot(p_ref[2 * j + 1], v_s[keys, :])
        for m in range(N_Q_HEADS // 2):
            pr = slice(m * WINDOW, (m + 1) * WINDOW)
            even_nat = (2 * m) in _NAT_HEADS
            att = jnp.where(lo, o_nat[pr], o_swp[pr]) if even_nat else jnp.where(lo, o_swp[pr], o_nat[pr])
            mix_ref[rows, m * LANES:(m + 1) * LANES] = att.astype(BF16)

    def gmlp(m):
        cs = slice(m * LANES, (m + 1) * LANES)
        w0 = jnp.where(mask_cur, ws_ref[2 * m], 0.0).astype(BF16)
        w1 = jnp.where(mask_cur, ws_ref[2 * m + 1], 0.0).astype(BF16)
        wcat = jnp.concatenate([w0, w1], axis=1)
        for j in range(BLOCKS_PER_TILE):
            rows = slice(j * WINDOW, (j + 1) * WINDOW)
            vcol = vn_ref[rows, cs]
            rhs = jnp.concatenate([jnp.where(lo, vcol, jnp.zeros_like(vcol)),
                                   jnp.where(lo, jnp.zeros_like(vcol), vcol)], axis=0)
            sp = _dot(wcat, rhs) + bsf_ref[:, cs]
            u = _gelu(z_ref[rows, C_U + m * LANES:C_U + (m + 1) * LANES])
            mix_ref[rows, ATTN_WIDTH + m * LANES:ATTN_WIDTH + (m + 1) * LANES] = (u * sp).astype(BF16)

    def project_out(c):
        cols = slice(c * proj_cols, (c + 1) * proj_cols)
        xmid_ref[:, cols] = x_ref[:, cols] + _dot(mix_ref[...], wout_ref[:, cols])

    h_ref[...] = _rmsnorm(x_next_ref[...], g1_ref[...]).astype(BF16)
    project_next(0)
    project_next(1)
    for j in range(BLOCKS_PER_TILE):
        scores(j)
    vn_ref[...] = _layernorm(_gelu(z_ref[:, C_VG:C_VG + GMLP_WIDTH]), lng_ref[...], lnb_ref[...]).astype(BF16)
    project_next(2)
    softmax(0)
    project_next(3)
    softmax(1)
    project_next(4)
    softmax(2)
    project_next(5)
    softmax(3)
    project_next(6)
    for j in range(BLOCKS_PER_TILE):
        attend(j)
    for m in range(GMLP_GROUPS // 2):
        gmlp(m)
    for c in range(D_MODEL // proj_cols):
        project_out(c)
    h2_ref[...] = _rmsnorm(xmid_ref[...], g2_ref[...]).astype(BF16)
    d1, d2 = _sort_plan(_route(h2_ref[...], wrt_ref, brt_ref), upper_ref, lpad_ref, meta_ref, tab_ref)
    _sort_rows(d1, d2, h2_ref[...], xl_ref)

    for ref in kv_bufs:
        ref[0:WINDOW, :] = ref[TOK_TILE:TOK_TILE + WINDOW, :]


def _sort_plan(slab, upper_ref, lpad_ref, meta_ref, tab_ref):
    n_tok = slab.shape[1]
    e1, e2 = slab[0:1], slab[1:2]
    row32 = lax.broadcasted_iota(jnp.int32, (N_EXPERTS, n_tok), 0).astype(F32)
    sel1 = row32 == e1
    sel2 = row32 == e2
    onehot = jnp.where(sel1, 1.0, jnp.where(sel2, 1.0, 0.0))
    earlier = _dot(onehot.astype(BF16), upper_ref[...])
    cnt = jnp.sum(onehot, axis=1, keepdims=True)
    pc = jnp.floor((cnt + (ROW_GRANULE - 1)) * (1.0 / ROW_GRANULE)) * ROW_GRANULE
    pc_b = jnp.broadcast_to(pc, (N_EXPERTS, LANES))
    pc_pad = jnp.concatenate([pc_b, jnp.zeros((LANES - N_EXPERTS, LANES), F32)], axis=0).astype(BF16)
    start = _dot(lpad_ref[...], pc_pad)
    base = start[:, 0:1] + earlier
    d1 = jnp.sum(jnp.where(sel1, base, 0.0), axis=0, keepdims=True)
    d2 = jnp.sum(jnp.where(sel2, base, 0.0), axis=0, keepdims=True)
    row8 = lax.broadcasted_iota(jnp.int32, (8, n_tok), 0)
    meta_ref[...] = jnp.where(row8 == 0, d1, jnp.where(row8 == 1, d2, jnp.where(row8 >= 4, 0.0, slab)))
    lane = lax.broadcasted_iota(jnp.int32, (N_EXPERTS, LANES), 1)
    tab_ref[...] = jnp.where(lane == 0, pc_b, jnp.where(lane == 1, start, 0.0))
    return d1, d2


def _sort_rows(d1, d2, h2b, xl_ref):
    n_tok = h2b.shape[0]
    for c in range(SORT_ROWS // SORT_CHUNK):
        r_iota = (lax.broadcasted_iota(jnp.int32, (SORT_CHUNK, n_tok), 0) + c * SORT_CHUNK).astype(F32)
        perm = jnp.where(r_iota == d1, 1.0, jnp.where(r_iota == d2, 1.0, 0.0)).astype(BF16)
        xl_ref[c * SORT_CHUNK:(c + 1) * SORT_CHUNK, :] = _dot(perm, h2b).astype(BF16)


def _sample_mixer_kernel(sinks_ref, x_ref, ck_ref, cv_ref, g1_ref, win_ref, cos_ref, sin_ref, lng_ref, lnb_ref,
                         ws0_ref, bs0_ref, wout_ref, g2_ref, wrt_ref, brt_ref, upper_ref, lpad_ref, xl_in_hbm,
                         xmid_ref, meta_ref, tab_ref, kout_ref, vout_ref, vnout_ref, xl_hbm,
                         mix_ref, xl_tile, xl_sem):
    del xl_in_hbm
    n_seq = x_ref.shape[0]
    seq_chunk = 16
    x = x_ref[...]
    h = _rmsnorm(x, g1_ref[...]).astype(BF16)
    z = _dot(h, win_ref[...])
    cos = cos_ref[...]
    sin = sin_ref[...]
    scale = np.float32(HEAD_DIM ** -0.5)
    lane = lax.broadcasted_iota(jnp.int32, (n_seq, LANES), 1)
    lo = lane < HEAD_DIM
    kf = _rope(z[:, C_K:C_K + KV_WIDTH], cos, sin)
    vf = z[:, C_V:C_V + KV_WIDTH]
    kout_ref[...] = kf
    vout_ref[...] = vf
    kb = kf.astype(BF16).astype(F32)
    vb = vf.astype(BF16).astype(F32)

    q_heads = []
    for hd in range(N_Q_HEADS):
        m = hd // 2
        qc = _rope(z[:, C_Q + m * LANES:C_Q + (m + 1) * LANES], cos, sin) * scale
        keep = lo if hd % 2 == 0 else ~lo
        qm = jnp.where(keep, qc, 0.0)
        if (hd % 2) != (hd // (N_Q_HEADS // N_KV_HEADS)):
            qm = pltpu.roll(qm, HEAD_DIM, 1)
        q_heads.append(qm.astype(BF16))

    s_new = [jnp.sum(q_heads[hd].astype(F32) * kb, axis=-1, keepdims=True) for hd in range(N_Q_HEADS)]

    rr = lax.broadcasted_iota(jnp.int32, (N_Q_HEADS * seq_chunk, seq_chunk * WINDOW), 0)
    cc = lax.broadcasted_iota(jnp.int32, (N_Q_HEADS * seq_chunk, seq_chunk * WINDOW), 1)
    same_seq = (rr % seq_chunk) == (cc // WINDOW)
    kv_lo = lax.broadcasted_iota(jnp.int32, (seq_chunk, LANES), 1) < HEAD_DIM

    for c in range(n_seq // seq_chunk):
        sr = slice(c * seq_chunk, (c + 1) * seq_chunk)
        kc = ck_ref[sr].reshape(seq_chunk * WINDOW, KV_WIDTH).astype(BF16)
        vc = cv_ref[sr].reshape(seq_chunk * WINDOW, KV_WIDTH).astype(BF16)
        qs = jnp.concatenate([q_heads[hd][sr] for hd in range(N_Q_HEADS)], axis=0)
        s = jnp.where(same_seq, _dot_nt(qs, kc), NEG_INF)
        sn = jnp.concatenate([s_new[hd][sr] for hd in range(N_Q_HEADS)], axis=0)
        sink = jnp.concatenate([jnp.full((seq_chunk, 1), sinks_ref[hd], F32) for hd in range(N_Q_HEADS)], axis=0)
        m = jnp.maximum(jnp.maximum(jnp.max(s, axis=-1, keepdims=True), sn), sink)
        e = jnp.exp(s - m)
        en = jnp.exp(sn - m)
        inv = 1.0 / (jnp.sum(e, axis=-1, keepdims=True) + en + jnp.exp(sink - m))
        o = _dot((e * inv).astype(BF16), vc)
        pn = (en * inv).astype(BF16).astype(F32)
        for mcol in range(N_Q_HEADS // 2):
            halves = []
            for hd in (2 * mcol, 2 * mcol + 1):
                oh = o[hd * seq_chunk:(hd + 1) * seq_chunk] + pn[hd * seq_chunk:(hd + 1) * seq_chunk] * vb[sr]
                if (hd % 2) != (hd // (N_Q_HEADS // N_KV_HEADS)):
                    oh = pltpu.roll(oh, HEAD_DIM, 1)
                halves.append(oh)
            att = jnp.where(kv_lo, halves[0], halves[1])
            mix_ref[sr, mcol * LANES:(mcol + 1) * LANES] = att.astype(BF16)

    u = _gelu(z[:, C_U:C_U + GMLP_WIDTH])
    vn = _layernorm(_gelu(z[:, C_VG:C_VG + GMLP_WIDTH]), lng_ref[...], lnb_ref[...])
    vnout_ref[...] = vn
    sp = ws0_ref[...].astype(BF16).astype(F32) * vn.astype(BF16).astype(F32) + bs0_ref[...]
    mix_ref[:, ATTN_WIDTH:] = (u * sp).astype(BF16)

    xmid = x + _dot(mix_ref[...], wout_ref[...])
    xmid_ref[...] = xmid
    h2b = _rmsnorm(xmid, g2_ref[...]).astype(BF16)
    d1, d2 = _sort_plan(_route(h2b, wrt_ref, brt_ref), upper_ref, lpad_ref, meta_ref, tab_ref)
    _sort_rows(d1, d2, h2b, xl_tile)
    copy = pltpu.make_async_copy(xl_tile, xl_hbm.at[xl_hbm.shape[0] - 1], xl_sem.at[0])
    copy.start()
    copy.wait()


def _grouped_ffn_kernel(texp_ref, gsrc_ref, sdst_ref, ntiles_ref, xl_hbm, wg_ref, wu_ref, wd_ref, yl_hbm,
                        xbuf, ybuf, gsem, ssem):
    j = pl.program_id(0)
    n_tiles = ntiles_ref[0]

    def gather_copy(tile, s, b):
        return pltpu.make_async_copy(xl_hbm.at[gsrc_ref[tile * FFN_SLOTS + s]], xbuf.at[b, s], gsem.at[b])

    def scatter_copy(tile, s):
        b = (tile + SCATTER_BUFS) % SCATTER_BUFS
        return pltpu.make_async_copy(ybuf.at[b, s], yl_hbm.at[sdst_ref[(tile + 1) * FFN_SLOTS + s]], ssem.at[b])

    @pl.when(j < n_tiles)
    def _():
        gb = j % GATHER_BUFS
        ahead = GATHER_BUFS - 1
        nxt = j + ahead
        nxt_b = nxt % GATHER_BUFS

        @pl.when(j == 0)
        def _():
            ybuf[SCATTER_BUFS - 1] = jnp.zeros(ybuf.shape[1:], BF16)
            for k in range(ahead):
                for s in range(FFN_SLOTS):
                    gather_copy(k, s, k).start()

        for s in range(FFN_SLOTS):
            gather_copy(j, s, gb).wait()

        @pl.when(j >= SCATTER_BUFS - 1)
        def _():
            for s in range(FFN_SLOTS):
                scatter_copy(j - SCATTER_BUFS, s).wait()

        x = xbuf[gb].reshape(FFN_ROWS, D_MODEL)
        gate = _dot(x, wg_ref[...])
        up = _dot(x, wu_ref[...])
        hid = (gate * (1.0 / (1.0 + jnp.exp(-gate))) * up).astype(BF16)
        wd = wd_ref[...]
        for s in range(FFN_SLOTS):
            scatter_copy(j - 1, s).start()
        for s in range(FFN_SLOTS):
            gather_copy(nxt, s, nxt_b).start()
        ybuf[j % SCATTER_BUFS] = _dot(hid, wd).astype(BF16).reshape(FFN_SLOTS, ROW_GRANULE, D_MODEL)

        @pl.when(j == n_tiles - 1)
        def _():
            for k in range(1, GATHER_BUFS):
                for s in range(FFN_SLOTS):
                    gather_copy(nxt, s, (j + k) % GATHER_BUFS).wait()
            for s in range(FFN_SLOTS):
                scatter_copy(j, s).start()
            for back in range(SCATTER_BUFS):
                @pl.when(j - back >= -1)
                def _():
                    for s in range(FFN_SLOTS):
                        scatter_copy(j - back, s).wait()


def _combine_kernel(xmid_ref, yl_hbm, meta_ref, gf_ref, y_ref, ybuf, ysem, *, first_tile, n_tiles):
    n_tok = xmid_ref.shape[0]
    i = pl.program_id(0)
    ahead = COMBINE_BUFS - 1

    def fetch(tile):
        row0 = pl.multiple_of((first_tile + tile) * SORT_ROWS, SORT_ROWS)
        slot = tile % COMBINE_BUFS
        return pltpu.make_async_copy(yl_hbm.at[pl.ds(row0, SORT_ROWS), :], ybuf.at[slot], ysem.at[slot])

    @pl.when(i == 0)
    def _():
        for k in range(min(ahead, n_tiles)):
            fetch(k).start()

    @pl.when(i + ahead < n_tiles)
    def _():
        fetch(i + ahead).start()

    fetch(i).wait()
    yl_ref = ybuf.at[i % COMBINE_BUFS]
    meta = meta_ref[...]
    meta_t = jnp.concatenate([meta, jnp.zeros((LANES - 8, n_tok), F32)], axis=0).T
    half = max(n_tok // 2, LANES)
    for r0 in range(0, n_tok, half):
        rows = slice(r0, r0 + half)
        d1, d2, w1, w2 = (meta_t[rows, i:i + 1] for i in range(4))
        acc = xmid_ref[rows, :]
        for c in range(SORT_ROWS // SORT_CHUNK):
            r_iota = (lax.broadcasted_iota(jnp.int32, (half, SORT_CHUNK), 1) + c * SORT_CHUNK).astype(F32)
            unsort = jnp.where(r_iota == d1, w1, jnp.where(r_iota == d2, w2, 0.0)).astype(BF16)
            acc = acc + _dot(unsort, yl_ref[c * SORT_CHUNK:(c + 1) * SORT_CHUNK, :])
        y_ref[rows, :] = _rmsnorm(acc, gf_ref[...])


def _ffn_schedule(tab):
    n_tok_tiles = tab.shape[0]
    strips = (tab[:, :, 0] * (1.0 / ROW_GRANULE)).astype(jnp.int32)
    starts = (tab[:, :, 1] * (1.0 / ROW_GRANULE)).astype(jnp.int32)
    cnt = strips.T
    row0 = starts.T
    cs = jnp.cumsum(cnt, axis=1) - cnt
    n_str = jnp.sum(cnt, axis=1)
    np_str = (n_str + FFN_SLOTS - 1) // FFN_SLOTS * FFN_SLOTS
    ends = jnp.cumsum(np_str)
    base = ends - np_str
    n_steps_max = _ffn_steps_max(n_tok_tiles) + GATHER_BUFS - 1
    step0 = jnp.arange(n_steps_max, dtype=jnp.int32) * FFN_SLOTS
    stream = jnp.minimum(jnp.sum(ends[None, :] <= step0[:, None], axis=1), N_EXPERTS - 1)
    pick = stream[:, None] == jnp.arange(N_EXPERTS, dtype=jnp.int32)[None, :]
    sel = lambda x: jnp.sum(jnp.where(pick[:, :, None], x[None], 0), axis=1)
    cs_j, cnt_j, row0_j = sel(cs), sel(cnt), sel(row0)
    base_j = jnp.sum(jnp.where(pick, base[None, :], 0), axis=1)
    q = step0[:, None] + jnp.arange(FFN_SLOTS, dtype=jnp.int32)[None, :] - base_j[:, None]
    reached = cs_j[:, None, :] <= q[:, :, None]
    last = lambda x: jnp.sum(jnp.where(reached, jnp.diff(x, axis=1, prepend=0)[:, None, :], 0), axis=2)
    tile_idx = jnp.sum(reached, axis=2).astype(jnp.int32) - 1
    g = q - last(cs_j)
    valid = g < last(cnt_j)
    granule = tile_idx * GRANULES_PER_TILE + last(row0_j) + g
    assert n_tok_tiles >= 2 * FFN_SLOTS
    slot = jnp.arange(FFN_SLOTS, dtype=jnp.int32)[None, :]
    parity = jnp.arange(n_steps_max, dtype=jnp.int32)[:, None] % 2
    pad_dst = lambda par: (slot + par * FFN_SLOTS) * GRANULES_PER_TILE + GRANULES_PER_TILE - 1
    gsrc = jnp.where(valid, granule, slot * GRANULES_PER_TILE + GRANULES_PER_TILE - 2).astype(jnp.int32)
    sdst = jnp.where(valid, granule, pad_dst(parity))
    sdst = jnp.concatenate([pad_dst(1), sdst], axis=0)
    texp = stream.astype(jnp.int32)
    n_steps = (ends[-1] // FFN_SLOTS).astype(jnp.int32).reshape(1)
    return texp, gsrc.reshape(-1), sdst.astype(jnp.int32).reshape(-1), n_steps


assert GRANULES_PER_TILE - (2 * TOK_TILE + N_EXPERTS * (ROW_GRANULE - 1)) // ROW_GRANULE >= 2


def _ffn_steps_max(n_tok_tiles):
    return -(-(n_tok_tiles * GRANULES_PER_TILE + N_EXPERTS * (FFN_SLOTS - 1)) // FFN_SLOTS)


def _vmem_limit(mib):
    n_bytes = mib * 1024 * 1024
    assert n_bytes < V7X_VMEM_BYTES
    return n_bytes


def _rope_tables(pos):
    inv_freq = ROPE_THETA ** (-np.arange(HALF, dtype=np.float64) * 2.0 / HEAD_DIM)
    ang = np.asarray(pos, np.float64)[:, None] * inv_freq[None, :]
    cos, sin = np.cos(ang).astype(np.float32), np.sin(ang).astype(np.float32)
    reps = LANES // HEAD_DIM
    return (jnp.asarray(np.tile(np.concatenate([cos, cos], axis=1), (1, reps))),
            jnp.asarray(np.tile(np.concatenate([-sin, sin], axis=1), (1, reps))))


def _moe_combine(xmid, yl, meta, gf, n_tok, first_tile, n_tiles):
    return pl.pallas_call(
        functools.partial(_combine_kernel, first_tile=first_tile, n_tiles=n_tiles),
        grid=(n_tiles,),
        in_specs=[
            pl.BlockSpec((n_tok, D_MODEL), lambda i: (i, 0)),
            pl.BlockSpec(memory_space=pl.ANY),
            pl.BlockSpec((None, 8, n_tok), lambda i: (i, 0, 0)),
            pl.BlockSpec((1, D_MODEL), lambda i: (0, 0)),
        ],
        out_specs=pl.BlockSpec((n_tok, D_MODEL), lambda i: (i, 0)),
        out_shape=jax.ShapeDtypeStruct((n_tiles * n_tok, D_MODEL), F32),
        scratch_shapes=[pltpu.VMEM((COMBINE_BUFS, SORT_ROWS, D_MODEL), BF16), pltpu.SemaphoreType.DMA((COMBINE_BUFS,))],
        compiler_params=pltpu.CompilerParams(
            dimension_semantics=("arbitrary",),
            vmem_limit_bytes=_vmem_limit(40)),
        name="moe_combine",
    )(xmid, yl, meta, gf)


def kernel(x_prompt, x_sample, cache_swa_k, cache_swa_v, norm_mix_g, w_in, attn_sinks, gmlp_ln_g, gmlp_ln_b,
           gmlp_w_s, gmlp_b_s, w_out, norm_ffn_g, router_group_w, router_group_b, router_expert_w,
           router_expert_b, expert_w_gate, expert_w_up, expert_w_down, final_norm_g):
    assert norm_mix_g.shape[0] == 1, "single-layer trunk"
    batch, seq, _ = x_prompt.shape
    dec_batch = x_sample.shape[0]
    assert x_sample.shape[1] == 1 and seq % TOK_TILE == 0

    win_ext = w_in[0].astype(BF16)
    wout = w_out[0].astype(BF16)
    g1 = norm_mix_g[0][None, :]
    g2 = norm_ffn_g[0][None, :]
    gf = final_norm_g[None, :]
    lng = gmlp_ln_g[0][None, :]
    lnb = gmlp_ln_b[0][None, :]
    sinks = attn_sinks[0]
    ws = gmlp_w_s[0]
    group_dim = GMLP_WIDTH // GMLP_GROUPS
    bsf = jnp.repeat(gmlp_b_s[0].T, group_dim, axis=1)
    ws0 = jnp.repeat(ws[:, 0, 0], group_dim)[None, :]
    bs0 = jnp.repeat(gmlp_b_s[0][:, 0], group_dim)[None, :]
    pad_g, pad_e = 8 - N_GROUPS, ROUTER_ROWS - 8 - N_EXPERTS
    wrt = jnp.concatenate([router_group_w[0], jnp.zeros((D_MODEL, pad_g), F32),
                           router_expert_w[0].reshape(D_MODEL, N_EXPERTS), jnp.zeros((D_MODEL, pad_e), F32)],
                          axis=1).T.astype(BF16)
    brt = jnp.concatenate([router_group_b[0], jnp.full((pad_g,), NEG_INF, F32),
                           router_expert_b[0].reshape(N_EXPERTS), jnp.full((pad_e,), NEG_INF, F32)])[:, None]
    wg_f = expert_w_gate[0].reshape(N_EXPERTS * D_MODEL, D_EXPERT)
    wu_f = expert_w_up[0].reshape(N_EXPERTS * D_MODEL, D_EXPERT)
    wd_f = expert_w_down[0].reshape(N_EXPERTS * D_EXPERT, D_MODEL)
    cos_p, sin_p = _rope_tables(np.arange(seq))
    cos_s, sin_s = _rope_tables(PAST_LEN + np.arange(1))

    full = lambda shape: pl.BlockSpec(shape, lambda *_: (0,) * len(shape))
    smem = pl.BlockSpec(memory_space=pltpu.SMEM)
    n_tiles = seq // TOK_TILE

    upper = jnp.triu(jnp.ones((TOK_TILE, TOK_TILE), BF16), k=1)
    lpad = (jnp.arange(LANES)[None, :] < jnp.arange(N_EXPERTS)[:, None]).astype(BF16)
    n_tok_tiles = batch * n_tiles

    x2d = x_prompt.reshape(batch * seq, D_MODEL)
    tile_a = lambda s: jnp.minimum(s, n_tok_tiles - 1)
    cur = lambda s: jnp.maximum(s - 1, 0)
    tile_b = cur
    gu_rows, dn_rows = wg_f.shape[0] // n_tok_tiles, wd_f.shape[0] // n_tok_tiles
    assert gu_rows * n_tok_tiles == wg_f.shape[0] and dn_rows * n_tok_tiles == wd_f.shape[0] and dn_rows % 16 == 0
    xmid_p, xl_p, meta_p, tab_p, k_p, v_p, wg, wu, wd = pl.pallas_call(
        functools.partial(_prompt_mixer_kernel, tiles_per_seq=n_tiles),
        grid=(n_tok_tiles + 1,),
        in_specs=[
            smem,
            pl.BlockSpec((TOK_TILE, D_MODEL), lambda s: (tile_a(s), 0)),
            pl.BlockSpec((TOK_TILE, D_MODEL), lambda s: (cur(s), 0)),
            full((1, D_MODEL)),
            full((D_MODEL, IN_WIDTH)),
            pl.BlockSpec((TOK_TILE, LANES), lambda s: (tile_b(s) % n_tiles, 0)),
            pl.BlockSpec((TOK_TILE, LANES), lambda s: (tile_b(s) % n_tiles, 0)),
            full((1, GMLP_WIDTH)),
            full((1, GMLP_WIDTH)),
            full((GMLP_GROUPS, CHUNK, CHUNK)),
            full((CHUNK, GMLP_WIDTH)),
            full((D_MODEL, D_MODEL)),
            full((1, D_MODEL)),
            full((ROUTER_ROWS, D_MODEL)),
            full((ROUTER_ROWS, 1)),
            full((TOK_TILE, TOK_TILE)),
            full((N_EXPERTS, LANES)),
            pl.BlockSpec((gu_rows, D_EXPERT), lambda s: (tile_a(s), 0)),
            pl.BlockSpec((gu_rows, D_EXPERT), lambda s: (tile_a(s), 0)),
            pl.BlockSpec((dn_rows, D_MODEL), lambda s: (tile_a(s), 0)),
        ],
        out_specs=[
            pl.BlockSpec((TOK_TILE, D_MODEL), lambda s: (cur(s), 0)),
            pl.BlockSpec((None, SORT_ROWS, D_MODEL), lambda s: (jnp.where(s == 0, n_tok_tiles, s - 1), 0, 0)),
            pl.BlockSpec((None, 8, TOK_TILE), lambda s: (cur(s), 0, 0)),
            pl.BlockSpec((None, N_EXPERTS, LANES), lambda s: (cur(s), 0, 0)),
            pl.BlockSpec((None, WINDOW, KV_WIDTH), lambda s: (tile_b(s) // n_tiles, 0, 0)),
            pl.BlockSpec((None, WINDOW, KV_WIDTH), lambda s: (tile_b(s) // n_tiles, 0, 0)),
            pl.BlockSpec((gu_rows, D_EXPERT), lambda s: (tile_a(s), 0)),
            pl.BlockSpec((gu_rows, D_EXPERT), lambda s: (tile_a(s), 0)),
            pl.BlockSpec((dn_rows, D_MODEL), lambda s: (tile_a(s), 0)),
        ],
        out_shape=[
            jax.ShapeDtypeStruct((batch * seq, D_MODEL), F32),
            jax.ShapeDtypeStruct((n_tok_tiles + 1, SORT_ROWS, D_MODEL), BF16),
            jax.ShapeDtypeStruct((n_tok_tiles, 8, TOK_TILE), F32),
            jax.ShapeDtypeStruct((n_tok_tiles, N_EXPERTS, LANES), F32),
            jax.ShapeDtypeStruct((batch, WINDOW, KV_WIDTH), F32),
            jax.ShapeDtypeStruct((batch, WINDOW, KV_WIDTH), F32),
            jax.ShapeDtypeStruct(wg_f.shape, BF16),
            jax.ShapeDtypeStruct(wu_f.shape, BF16),
            jax.ShapeDtypeStruct(wd_f.shape, BF16),
        ],
        scratch_shapes=[
            pltpu.VMEM((TOK_TILE, IN_WIDTH), F32),
            pltpu.VMEM((TOK_TILE, IN_WIDTH), F32),
            pltpu.VMEM((TOK_TILE, D_MODEL), BF16),
        ] + [pltpu.VMEM((WINDOW + TOK_TILE, KV_WIDTH), BF16)] * 4 + [
            pltpu.VMEM((len(_NAT_HEADS), TOK_TILE, LANES), BF16),
            pltpu.VMEM((len(_SWP_HEADS), TOK_TILE, LANES), BF16),
            pltpu.VMEM((2 * BLOCKS_PER_TILE, len(_NAT_HEADS) * WINDOW, 2 * WINDOW), F32),
            pltpu.VMEM((2 * BLOCKS_PER_TILE, len(_NAT_HEADS) * WINDOW, 2 * WINDOW), BF16),
            pltpu.VMEM((TOK_TILE, GMLP_WIDTH), BF16),
            pltpu.VMEM((TOK_TILE, D_MODEL), BF16),
            pltpu.VMEM((TOK_TILE, D_MODEL), BF16),
        ],
        compiler_params=pltpu.CompilerParams(
            dimension_semantics=("arbitrary",),
            vmem_limit_bytes=_vmem_limit(56)),
        name="prompt_mixer",
    )(sinks, x2d, x2d, g1, win_ext, cos_p, sin_p, lng, lnb, ws, bsf, wout, g2, wrt, brt, upper, lpad,
      wg_f, wu_f, wd_f)
    wg = wg.reshape(N_EXPERTS, D_MODEL, D_EXPERT)
    wu = wu.reshape(N_EXPERTS, D_MODEL, D_EXPERT)
    wd = wd.reshape(N_EXPERTS, D_EXPERT, D_MODEL)

    xs = x_sample.reshape(dec_batch, D_MODEL)
    ck = cache_swa_k[0].reshape(dec_batch, WINDOW, KV_WIDTH)
    cv = cache_swa_v[0].reshape(dec_batch, WINDOW, KV_WIDTH)
    vmem = pl.BlockSpec(memory_space=pltpu.VMEM)
    hbm = pl.BlockSpec(memory_space=pl.ANY)
    xmid_s, meta_s, tab_s, k_s, v_s, vn_s, xl_all = pl.pallas_call(
        _sample_mixer_kernel,
        in_specs=[smem] + [vmem] * 17 + [hbm],
        out_specs=[vmem] * 6 + [hbm],
        out_shape=[
            jax.ShapeDtypeStruct((dec_batch, D_MODEL), F32),
            jax.ShapeDtypeStruct((8, dec_batch), F32),
            jax.ShapeDtypeStruct((N_EXPERTS, LANES), F32),
            jax.ShapeDtypeStruct((dec_batch, KV_WIDTH), F32),
            jax.ShapeDtypeStruct((dec_batch, KV_WIDTH), F32),
            jax.ShapeDtypeStruct((dec_batch, GMLP_WIDTH), F32),
            jax.ShapeDtypeStruct(xl_p.shape, BF16),
        ],
        scratch_shapes=[pltpu.VMEM((dec_batch, D_MODEL), BF16), pltpu.VMEM((SORT_ROWS, D_MODEL), BF16),
                        pltpu.SemaphoreType.DMA((1,))],
        compiler_params=pltpu.CompilerParams(vmem_limit_bytes=_vmem_limit(56)),
        input_output_aliases={18: 6},
        name="sample_mixer",
    )(sinks, xs, ck, cv, g1, win_ext, cos_s, sin_s, lng, lnb, ws0, bs0, wout, g2, wrt, brt,
      upper[:dec_batch, :dec_batch], lpad, xl_p)

    n_all_tiles = n_tok_tiles + 1
    tab_all = jnp.concatenate([tab_p, tab_s[None]], axis=0)
    texp, gsrc, sdst, n_steps = _ffn_schedule(tab_all)
    yl_all = pl.pallas_call(
        _grouped_ffn_kernel,
        grid_spec=pltpu.PrefetchScalarGridSpec(
            num_scalar_prefetch=4,
            grid=(_ffn_steps_max(n_all_tiles),),
            in_specs=[
                pl.BlockSpec(memory_space=pl.ANY),
                pl.BlockSpec((None, D_MODEL, D_EXPERT), lambda j, te, gs, sd, ns: (te[j], 0, 0)),
                pl.BlockSpec((None, D_MODEL, D_EXPERT), lambda j, te, gs, sd, ns: (te[j], 0, 0)),
                pl.BlockSpec((None, D_EXPERT, D_MODEL), lambda j, te, gs, sd, ns: (te[j], 0, 0)),
            ],
            out_specs=pl.BlockSpec(memory_space=pl.ANY),
            scratch_shapes=[
                pltpu.VMEM((GATHER_BUFS, FFN_SLOTS, ROW_GRANULE, D_MODEL), BF16),
                pltpu.VMEM((SCATTER_BUFS, FFN_SLOTS, ROW_GRANULE, D_MODEL), BF16),
                pltpu.SemaphoreType.DMA((GATHER_BUFS,)),
                pltpu.SemaphoreType.DMA((SCATTER_BUFS,)),
            ],
        ),
        out_shape=jax.ShapeDtypeStruct((n_all_tiles * GRANULES_PER_TILE, ROW_GRANULE, D_MODEL), BF16),
        compiler_params=pltpu.CompilerParams(
            dimension_semantics=("arbitrary",),
            vmem_limit_bytes=_vmem_limit(32)),
        input_output_aliases={4: 0},
        name="grouped_ffn",
    )(texp, gsrc, sdst, n_steps, xl_all.reshape(n_all_tiles * GRANULES_PER_TILE, ROW_GRANULE, D_MODEL), wg, wu, wd)
    yl_all = yl_all.reshape(n_all_tiles * SORT_ROWS, D_MODEL)

    y_p = _moe_combine(xmid_p, yl_all, meta_p, gf, TOK_TILE, 0, n_tok_tiles)
    y_s = _moe_combine(xmid_s, yl_all, meta_s[None], gf, dec_batch, n_tok_tiles, 1)

    return (y_p.reshape(batch, seq, D_MODEL),
            y_s.reshape(dec_batch, 1, D_MODEL),
            k_p.reshape(1, batch, WINDOW, N_KV_HEADS, HEAD_DIM),
            v_p.reshape(1, batch, WINDOW, N_KV_HEADS, HEAD_DIM),
            k_s.reshape(1, dec_batch, 1, N_KV_HEADS, HEAD_DIM),
            v_s.reshape(1, dec_batch, 1, N_KV_HEADS, HEAD_DIM),
            vn_s.reshape(1, dec_batch, 1, GMLP_WIDTH))
```

```python
import functools

import jax
import jax.numpy as jnp
import numpy as np
from jax import lax
from jax.experimental import pallas as pl
from jax.experimental.pallas import tpu as pltpu

F32 = jnp.float32
BF16 = jnp.bfloat16

D_MODEL = 1024
HEAD_DIM = 64
HALF = HEAD_DIM // 2
N_Q_HEADS = 8
N_KV_HEADS = 2
ATTN_WIDTH = N_Q_HEADS * HEAD_DIM
KV_WIDTH = N_KV_HEADS * HEAD_DIM
WINDOW = 128
ROPE_THETA = 10000.0
GMLP_WIDTH = D_MODEL - ATTN_WIDTH
GMLP_GROUPS = 8
CHUNK = 128
N_GROUPS = 4
EXPERTS_PER_GROUP = 8
N_EXPERTS = N_GROUPS * EXPERTS_PER_GROUP
D_EXPERT = 256
EPS = 1e-6
NEG_INF = -1e30
PAST_LEN = 16384

LANES = 128
V7X_VMEM_BYTES = 64 * 1024 * 1024

C_Q = 0
C_K = C_Q + ATTN_WIDTH
C_V = C_K + KV_WIDTH
C_U = C_V + KV_WIDTH
C_VG = C_U + GMLP_WIDTH
IN_WIDTH = C_VG + GMLP_WIDTH

ROUTER_ROWS = 48
TOK_TILE = 512
BLOCKS_PER_TILE = TOK_TILE // WINDOW
ROW_GRANULE = 16
SORT_CHUNK = 512
SORT_ROWS = -(-(2 * TOK_TILE + N_EXPERTS * (ROW_GRANULE - 1)) // SORT_CHUNK) * SORT_CHUNK
GRANULES_PER_TILE = SORT_ROWS // ROW_GRANULE
FFN_ROWS = 512
FFN_SLOTS = FFN_ROWS // ROW_GRANULE
GATHER_BUFS = 4
SCATTER_BUFS = 3
COMBINE_BUFS = 3


def _dot(a, b):
    return jnp.dot(a, b, preferred_element_type=F32)


def _dot_nt(a, b):
    return lax.dot_general(a, b, (((1,), (1,)), ((), ())), preferred_element_type=F32)


def _gelu(x):
    return 0.5 * x * (1.0 + lax.erf(x * np.float32(np.sqrt(0.5))))


def _rmsnorm(x, g):
    return x * lax.rsqrt(jnp.mean(x * x, axis=-1, keepdims=True) + EPS) * g


def _layernorm(x, g, b):
    mu = jnp.mean(x, axis=-1, keepdims=True)
    xc = x - mu
    return xc * lax.rsqrt(jnp.mean(xc * xc, axis=-1, keepdims=True) + EPS) * g + b


def _first_argmax_rows(x, row_iota, n_rows):
    m = jnp.max(x, axis=0, keepdims=True)
    idx = jnp.min(jnp.where(x == m, row_iota, n_rows), axis=0, keepdims=True)
    return m, idx


def _route(h2b, wrt_ref, brt_ref):
    n_tok = h2b.shape[0]
    lt = _dot_nt(wrt_ref[...], h2b) + brt_ref[...]
    row8 = lax.broadcasted_iota(jnp.int32, (8, n_tok), 0).astype(F32)
    glog = lt[0:8]
    gmax, gidx = _first_argmax_rows(glog, row8, 8)
    g_w = 1.0 / jnp.sum(jnp.exp(glog - gmax), axis=0, keepdims=True)
    esel = lt[8:16]
    for g in range(1, N_GROUPS):
        esel = jnp.where(gidx == g, lt[8 + 8 * g:16 + 8 * g], esel)
    m1, i1 = _first_argmax_rows(esel, row8, 8)
    esel2 = jnp.where(row8 == i1, -jnp.inf, esel)
    m2, i2 = _first_argmax_rows(esel2, row8, 8)
    r = jnp.exp(m2 - m1)
    w1 = 1.0 / (1.0 + r)
    w2 = r / (1.0 + r)
    e1 = gidx * EXPERTS_PER_GROUP + i1
    e2 = gidx * EXPERTS_PER_GROUP + i2
    slab = jnp.where(row8 == 0, e1, 0.0)
    for r_idx, val in ((1, e2), (2, w1 * g_w), (3, w2 * g_w)):
        slab = jnp.where(row8 == r_idx, val, slab)
    return slab


def _softmax_with_sink(s, sink):
    m = jnp.maximum(jnp.max(s, axis=-1, keepdims=True), sink)
    e = jnp.exp(s - m)
    den = jnp.sum(e, axis=-1, keepdims=True) + jnp.exp(sink - m)
    return e * (1.0 / den)


def _rope(x, cos, sin_signed):
    first_half = (lax.broadcasted_iota(jnp.int32, x.shape, 1) & HALF) == 0
    partner = jnp.where(first_half, pltpu.roll(x, LANES - HALF, 1), pltpu.roll(x, HALF, 1))
    return x * cos + partner * sin_signed


_NAT_HEADS = (0, 2, 5, 7)
_SWP_HEADS = (1, 3, 4, 6)


def _prompt_mixer_kernel(sinks_ref, x_next_ref, x_ref, g1_ref, win_ref, cos_ref, sin_ref, lng_ref, lnb_ref, ws_ref,
                         bsf_ref, wout_ref, g2_ref, wrt_ref, brt_ref, upper_ref, lpad_ref, wgf_ref, wuf_ref, wdf_ref,
                         xmid_ref, xl_ref, meta_ref, tab_ref, kout_ref, vout_ref, wgo_ref, wuo_ref, wdo_ref,
                         z_a, z_b, mix_ref, k_n, k_s, v_n, v_s, q_nat, q_swp, s_ref, p_ref, vn_ref, h_ref, h2_ref,
                         *, tiles_per_seq):
    step = pl.program_id(0)
    t = jnp.maximum(step - 1, 0) % tiles_per_seq
    kv_bufs = (k_n, k_s, v_n, v_s)

    @pl.when(step == 0)
    def _():
        z_b[...] = jnp.zeros_like(z_b)

    for src, dst in ((wgf_ref, wgo_ref), (wuf_ref, wuo_ref), (wdf_ref, wdo_ref)):
        dst[...] = src[...].astype(BF16)

    @pl.when(t == 0)
    def _():
        for ref in kv_bufs:
            ref[0:WINDOW, :] = jnp.zeros((WINDOW, KV_WIDTH), BF16)

    args = (t, sinks_ref, x_next_ref, x_ref, g1_ref, win_ref, cos_ref, sin_ref, lng_ref, lnb_ref, ws_ref, bsf_ref,
            wout_ref, g2_ref, wrt_ref, brt_ref, upper_ref, lpad_ref, xmid_ref, xl_ref, meta_ref, tab_ref, kout_ref,
            vout_ref, kv_bufs, q_nat, q_swp, s_ref, p_ref, vn_ref, h_ref, h2_ref)

    @pl.when(step % 2 == 0)
    def _():
        _prompt_mixer_body(z_a, z_b, mix_ref, *args)

    @pl.when(step % 2 == 1)
    def _():
        _prompt_mixer_body(z_b, z_a, mix_ref, *args)


def _prompt_mixer_body(z_next, z_ref, mix_ref, t, sinks_ref, x_next_ref, x_ref, g1_ref, win_ref, cos_ref,
                       sin_ref, lng_ref, lnb_ref, ws_ref, bsf_ref, wout_ref, g2_ref, wrt_ref, brt_ref, upper_ref,
                       lpad_ref, xmid_ref, xl_ref, meta_ref, tab_ref, kout_ref, vout_ref, kv_bufs,
                       q_nat, q_swp, s_ref, p_ref, vn_ref, h_ref, h2_ref):
    k_n, k_s, v_n, v_s = kv_bufs
    cos = cos_ref[...]
    sin = sin_ref[...]
    lane = lax.broadcasted_iota(jnp.int32, (WINDOW, LANES), 1)
    lo = lane < HEAD_DIM
    row = lax.broadcasted_iota(jnp.int32, (WINDOW, WINDOW), 0)
    col = lax.broadcasted_iota(jnp.int32, (WINDOW, WINDOW), 1)
    mask_cur = col <= row
    mask_prev_band = col >= row
    mask_prev_first = jnp.logical_and(mask_prev_band, (jnp.zeros_like(row) + t) > 0)
    mask_band = jnp.concatenate([mask_prev_band, mask_cur], axis=1)
    mask_first = jnp.concatenate([mask_prev_first, mask_cur], axis=1)

    cq = cos * np.float32(HEAD_DIM ** -0.5)
    sq = sin * np.float32(HEAD_DIM ** -0.5)
    kf = _rope(z_ref[:, C_K:C_K + KV_WIDTH], cos, sin)
    vf = z_ref[:, C_V:C_V + KV_WIDTH]
    k_n[WINDOW:, :] = kf.astype(BF16)
    k_s[WINDOW:, :] = pltpu.roll(kf, HEAD_DIM, 1).astype(BF16)
    v_n[WINDOW:, :] = vf.astype(BF16)
    v_s[WINDOW:, :] = pltpu.roll(vf, HEAD_DIM, 1).astype(BF16)

    kout_ref[...] = kf[TOK_TILE - WINDOW:]
    vout_ref[...] = vf[TOK_TILE - WINDOW:]

    lo_t = lax.broadcasted_iota(jnp.int32, (TOK_TILE, LANES), 1) < HEAD_DIM
    for m in range(N_Q_HEADS // 2):
        qc = _rope(z_ref[:, C_Q + m * LANES:C_Q + (m + 1) * LANES], cq, sq)
        for hd, qh in ((2 * m, jnp.where(lo_t, qc, 0.0)), (2 * m + 1, jnp.where(lo_t, 0.0, qc))):
            if hd in _NAT_HEADS:
                q_nat[_NAT_HEADS.index(hd)] = qh.astype(BF16)
            else:
                q_swp[_SWP_HEADS.index(hd)] = qh.astype(BF16)

    stacks = ((q_nat, _NAT_HEADS, k_n, v_n), (q_swp, _SWP_HEADS, k_s, v_s))
    proj_cols = 2 * LANES

    def project_next(c):
        cols = slice(c * proj_cols, (c + 1) * proj_cols)
        z_next[:, cols] = _dot(h_ref[...], win_ref[:, cols])

    def scores(j):
        rows = slice(j * WINDOW, (j + 1) * WINDOW)
        keys = slice(j * WINDOW, (j + 2) * WINDOW)
        for si, (q_ref, _, k_buf, _) in enumerate(stacks):
            q_stack = jnp.concatenate([q_ref[i, rows, :] for i in range(len(_NAT_HEADS))], axis=0)
            s_ref[2 * j + si] = _dot_nt(q_stack, k_buf[keys, :])

    def softmax(j):
        mask = mask_first if j == 0 else mask_band
        for si, (_, heads, _, _) in enumerate(stacks):
            for i, hd in enumerate(heads):
                pr = slice(i * WINDOW, (i + 1) * WINDOW)
                p_ref[2 * j + si, pr, :] = _softmax_with_sink(
                    jnp.where(mask, s_ref[2 * j + si, pr, :], NEG_INF), sinks_ref[hd]).astype(BF16)

    def attend(j):
        rows = slice(j * WINDOW, (j + 1) * WINDOW)
        keys = slice(j * WINDOW, (j + 2) * WINDOW)
        o_nat = _dot(p_ref[2 * j], v_n[keys, :])
        o_swp = _dot(p_ref[2 * j + 1], v_s[keys, :])
        for m in range(N_Q_HEADS // 2):
            pr = slice(m * WINDOW, (m + 1) * WINDOW)
            even_nat = (2 * m) in _NAT_HEADS
            att = jnp.where(lo, o_nat[pr], o_swp[pr]) if even_nat else jnp.where(lo, o_swp[pr], o_nat[pr])
            mix_ref[rows, m * LANES:(m + 1) * LANES] = att.astype(BF16)

    def gmlp(m):
        cs = slice(m * LANES, (m + 1) * LANES)
        w0 = jnp.where(mask_cur, ws_ref[2 * m], 0.0).astype(BF16)
        w1 = jnp.where(mask_cur, ws_ref[2 * m + 1], 0.0).astype(BF16)
        wcat = jnp.concatenate([w0, w1], axis=1)
        for j in range(BLOCKS_PER_TILE):
            rows = slice(j * WINDOW, (j + 1) * WINDOW)
            vcol = vn_ref[rows, cs]
            rhs = jnp.concatenate([jnp.where(lo, vcol, jnp.zeros_like(vcol)),
                                   jnp.where(lo, jnp.zeros_like(vcol), vcol)], axis=0)
            sp = _dot(wcat, rhs) + bsf_ref[:, cs]
            u = _gelu(z_ref[rows, C_U + m * LANES:C_U + (m + 1) * LANES])
            mix_ref[rows, ATTN_WIDTH + m * LANES:ATTN_WIDTH + (m + 1) * LANES] = (u * sp).astype(BF16)

    def project_out(c):
        cols = slice(c * proj_cols, (c + 1) * proj_cols)
        xmid_ref[:, cols] = x_ref[:, cols] + _dot(mix_ref[...], wout_ref[:, cols])

    h_ref[...] = _rmsnorm(x_next_ref[...], g1_ref[...]).astype(BF16)
    project_next(0)
    project_next(1)
    for j in range(BLOCKS_PER_TILE):
        scores(j)
    vn_ref[...] = _layernorm(_gelu(z_ref[:, C_VG:C_VG + GMLP_WIDTH]), lng_ref[...], lnb_ref[...]).astype(BF16)
    project_next(2)
    softmax(0)
    project_next(3)
    softmax(1)
    project_next(4)
    softmax(2)
    project_next(5)
    softmax(3)
    project_next(6)
    for j in range(BLOCKS_PER_TILE):
        attend(j)
    for m in range(GMLP_GROUPS // 2):
        gmlp(m)
    for c in range(D_MODEL // proj_cols):
        project_out(c)
    h2_ref[...] = _rmsnorm(xmid_ref[...], g2_ref[...]).astype(BF16)
    d1, d2 = _sort_plan(_route(h2_ref[...], wrt_ref, brt_ref), upper_ref, lpad_ref, meta_ref, tab_ref)
    _sort_rows(d1, d2, h2_ref[...], xl_ref)

    for ref in kv_bufs:
        ref[0:WINDOW, :] = ref[TOK_TILE:TOK_TILE + WINDOW, :]


def _sort_plan(slab, upper_ref, lpad_ref, meta_ref, tab_ref):
    n_tok = slab.shape[1]
    e1, e2 = slab[0:1], slab[1:2]
    row32 = lax.broadcasted_iota(jnp.int32, (N_EXPERTS, n_tok), 0).astype(F32)
    sel1 = row32 == e1
    sel2 = row32 == e2
    onehot = jnp.where(sel1, 1.0, jnp.where(sel2, 1.0, 0.0))
    earlier = _dot(onehot.astype(BF16), upper_ref[...])
    cnt = jnp.sum(onehot, axis=1, keepdims=True)
    pc = jnp.floor((cnt + (ROW_GRANULE - 1)) * (1.0 / ROW_GRANULE)) * ROW_GRANULE
    pc_b = jnp.broadcast_to(pc, (N_EXPERTS, LANES))
    pc_pad = jnp.concatenate([pc_b, jnp.zeros((LANES - N_EXPERTS, LANES), F32)], axis=0).astype(BF16)
    start = _dot(lpad_ref[...], pc_pad)
    base = start[:, 0:1] + earlier
    d1 = jnp.sum(jnp.where(sel1, base, 0.0), axis=0, keepdims=True)
    d2 = jnp.sum(jnp.where(sel2, base, 0.0), axis=0, keepdims=True)
    row8 = lax.broadcasted_iota(jnp.int32, (8, n_tok), 0)
    meta_ref[...] = jnp.where(row8 == 0, d1, jnp.where(row8 == 1, d2, jnp.where(row8 >= 4, 0.0, slab)))
    lane = lax.broadcasted_iota(jnp.int32, (N_EXPERTS, LANES), 1)
    tab_ref[...] = jnp.where(lane == 0, pc_b, jnp.where(lane == 1, start, 0.0))
    return d1, d2


def _sort_rows(d1, d2, h2b, xl_ref):
    n_tok = h2b.shape[0]
    for c in range(SORT_ROWS // SORT_CHUNK):
        r_iota = (lax.broadcasted_iota(jnp.int32, (SORT_CHUNK, n_tok), 0) + c * SORT_CHUNK).astype(F32)
        perm = jnp.where(r_iota == d1, 1.0, jnp.where(r_iota == d2, 1.0, 0.0)).astype(BF16)
        xl_ref[c * SORT_CHUNK:(c + 1) * SORT_CHUNK, :] = _dot(perm, h2b).astype(BF16)


def _sample_mixer_kernel(sinks_ref, x_ref, ck_ref, cv_ref, g1_ref, win_ref, cos_ref, sin_ref, lng_ref, lnb_ref,
                         ws0_ref, bs0_ref, wout_ref, g2_ref, wrt_ref, brt_ref, upper_ref, lpad_ref, xl_in_hbm,
                         xmid_ref, meta_ref, tab_ref, kout_ref, vout_ref, vnout_ref, xl_hbm,
                         mix_ref, xl_tile, xl_sem):
    del xl_in_hbm
    n_seq = x_ref.shape[0]
    seq_chunk = 16
    x = x_ref[...]
    h = _rmsnorm(x, g1_ref[...]).astype(BF16)
    z = _dot(h, win_ref[...])
    cos = cos_ref[...]
    sin = sin_ref[...]
    scale = np.float32(HEAD_DIM ** -0.5)
    lane = lax.broadcasted_iota(jnp.int32, (n_seq, LANES), 1)
    lo = lane < HEAD_DIM
    kf = _rope(z[:, C_K:C_K + KV_WIDTH], cos, sin)
    vf = z[:, C_V:C_V + KV_WIDTH]
    kout_ref[...] = kf
    vout_ref[...] = vf
    kb = kf.astype(BF16).astype(F32)
    vb = vf.astype(BF16).astype(F32)

    q_heads = []
    for hd in range(N_Q_HEADS):
        m = hd // 2
        qc = _rope(z[:, C_Q + m * LANES:C_Q + (m + 1) * LANES], cos, sin) * scale
        keep = lo if hd % 2 == 0 else ~lo
        qm = jnp.where(keep, qc, 0.0)
        if (hd % 2) != (hd // (N_Q_HEADS // N_KV_HEADS)):
            qm = pltpu.roll(qm, HEAD_DIM, 1)
        q_heads.append(qm.astype(BF16))

    s_new = [jnp.sum(q_heads[hd].astype(F32) * kb, axis=-1, keepdims=True) for hd in range(N_Q_HEADS)]

    rr = lax.broadcasted_iota(jnp.int32, (N_Q_HEADS * seq_chunk, seq_chunk * WINDOW), 0)
    cc = lax.broadcasted_iota(jnp.int32, (N_Q_HEADS * seq_chunk, seq_chunk * WINDOW), 1)
    same_seq = (rr % seq_chunk) == (cc // WINDOW)
    kv_lo = lax.broadcasted_iota(jnp.int32, (seq_chunk, LANES), 1) < HEAD_DIM

    for c in range(n_seq // seq_chunk):
        sr = slice(c * seq_chunk, (c + 1) * seq_chunk)
        kc = ck_ref[sr].reshape(seq_chunk * WINDOW, KV_WIDTH).astype(BF16)
        vc = cv_ref[sr].reshape(seq_chunk * WINDOW, KV_WIDTH).astype(BF16)
        qs = jnp.concatenate([q_heads[hd][sr] for hd in range(N_Q_HEADS)], axis=0)
        s = jnp.where(same_seq, _dot_nt(qs, kc), NEG_INF)
        sn = jnp.concatenate([s_new[hd][sr] for hd in range(N_Q_HEADS)], axis=0)
        sink = jnp.concatenate([jnp.full((seq_chunk, 1), sinks_ref[hd], F32) for hd in range(N_Q_HEADS)], axis=0)
        m = jnp.maximum(jnp.maximum(jnp.max(s, axis=-1, keepdims=True), sn), sink)
        e = jnp.exp(s - m)
        en = jnp.exp(sn - m)
        inv = 1.0 / (jnp.sum(e, axis=-1, keepdims=True) + en + jnp.exp(sink - m))
        o = _dot((e * inv).astype(BF16), vc)
        pn = (en * inv).astype(BF16).astype(F32)
        for mcol in range(N_Q_HEADS // 2):
            halves = []
            for hd in (2 * mcol, 2 * mcol + 1):
                oh = o[hd * seq_chunk:(hd + 1) * seq_chunk] + pn[hd * seq_chunk:(hd + 1) * seq_chunk] * vb[sr]
                if (hd % 2) != (hd // (N_Q_HEADS // N_KV_HEADS)):
                    oh = pltpu.roll(oh, HEAD_DIM, 1)
                halves.append(oh)
            att = jnp.where(kv_lo, halves[0], halves[1])
            mix_ref[sr, mcol * LANES:(mcol + 1) * LANES] = att.astype(BF16)

    u = _gelu(z[:, C_U:C_U + GMLP_WIDTH])
    vn = _layernorm(_gelu(z[:, C_VG:C_VG + GMLP_WIDTH]), lng_ref[...], lnb_ref[...])
    vnout_ref[...] = vn
    sp = ws0_ref[...].astype(BF16).astype(F32) * vn.astype(BF16).astype(F32) + bs0_ref[...]
    mix_ref[:, ATTN_WIDTH:] = (u * sp).astype(BF16)

    xmid = x + _dot(mix_ref[...], wout_ref[...])
    xmid_ref[...] = xmid
    h2b = _rmsnorm(xmid, g2_ref[...]).astype(BF16)
    d1, d2 = _sort_plan(_route(h2b, wrt_ref, brt_ref), upper_ref, lpad_ref, meta_ref, tab_ref)
    _sort_rows(d1, d2, h2b, xl_tile)
    copy = pltpu.make_async_copy(xl_tile, xl_hbm.at[xl_hbm.shape[0] - 1], xl_sem.at[0])
    copy.start()
    copy.wait()


def _grouped_ffn_kernel(texp_ref, gsrc_ref, sdst_ref, ntiles_ref, xl_hbm, wg_ref, wu_ref, wd_ref, yl_hbm,
                        xbuf, ybuf, gsem, ssem):
    j = pl.program_id(0)
    n_tiles = ntiles_ref[0]

    def gather_copy(tile, s, b):
        return pltpu.make_async_copy(xl_hbm.at[gsrc_ref[tile * FFN_SLOTS + s]], xbuf.at[b, s], gsem.at[b])

    def scatter_copy(tile, s):
        b = (tile + SCATTER_BUFS) % SCATTER_BUFS
        return pltpu.make_async_copy(ybuf.at[b, s], yl_hbm.at[sdst_ref[(tile + 1) * FFN_SLOTS + s]], ssem.at[b])

    @pl.when(j < n_tiles)
    def _():
        gb = j % GATHER_BUFS
        ahead = GATHER_BUFS - 1
        nxt = j + ahead
        nxt_b = nxt % GATHER_BUFS

        @pl.when(j == 0)
        def _():
            ybuf[SCATTER_BUFS - 1] = jnp.zeros(ybuf.shape[1:], BF16)
            for k in range(ahead):
                for s in range(FFN_SLOTS):
                    gather_copy(k, s, k).start()

        for s in range(FFN_SLOTS):
            gather_copy(j, s, gb).wait()

        @pl.when(j >= SCATTER_BUFS - 1)
        def _():
            for s in range(FFN_SLOTS):
                scatter_copy(j - SCATTER_BUFS, s).wait()

        x = xbuf[gb].reshape(FFN_ROWS, D_MODEL)
        gate = _dot(x, wg_ref[...])
        up = _dot(x, wu_ref[...])
        hid = (gate * (1.0 / (1.0 + jnp.exp(-gate))) * up).astype(BF16)
        wd = wd_ref[...]
        for s in range(FFN_SLOTS):
            scatter_copy(j - 1, s).start()
        for s in range(FFN_SLOTS):
            gather_copy(nxt, s, nxt_b).start()
        ybuf[j % SCATTER_BUFS] = _dot(hid, wd).astype(BF16).reshape(FFN_SLOTS, ROW_GRANULE, D_MODEL)

        @pl.when(j == n_tiles - 1)
        def _():
            for k in range(1, GATHER_BUFS):
                for s in range(FFN_SLOTS):
                    gather_copy(nxt, s, (j + k) % GATHER_BUFS).wait()
            for s in range(FFN_SLOTS):
                scatter_copy(j, s).start()
            for back in range(SCATTER_BUFS):
                @pl.when(j - back >= -1)
                def _():
                    for s in range(FFN_SLOTS):
                        scatter_copy(j - back, s).wait()


def _combine_kernel(xmid_ref, yl_hbm, meta_ref, gf_ref, y_ref, ybuf, ysem, *, first_tile, n_tiles):
    n_tok = xmid_ref.shape[0]
    i = pl.program_id(0)
    ahead = COMBINE_BUFS - 1

    def fetch(tile):
        row0 = pl.multiple_of((first_tile + tile) * SORT_ROWS, SORT_ROWS)
        slot = tile % COMBINE_BUFS
        return pltpu.make_async_copy(yl_hbm.at[pl.ds(row0, SORT_ROWS), :], ybuf.at[slot], ysem.at[slot])

    @pl.when(i == 0)
    def _():
        for k in range(min(ahead, n_tiles)):
            fetch(k).start()

    @pl.when(i + ahead < n_tiles)
    def _():
        fetch(i + ahead).start()

    fetch(i).wait()
    yl_ref = ybuf.at[i % COMBINE_BUFS]
    meta = meta_ref[...]
    meta_t = jnp.concatenate([meta, jnp.zeros((LANES - 8, n_tok), F32)], axis=0).T
    half = max(n_tok // 2, LANES)
    for r0 in range(0, n_tok, half):
        rows = slice(r0, r0 + half)
        d1, d2, w1, w2 = (meta_t[rows, i:i + 1] for i in range(4))
        acc = xmid_ref[rows, :]
        for c in range(SORT_ROWS // SORT_CHUNK):
            r_iota = (lax.broadcasted_iota(jnp.int32, (half, SORT_CHUNK), 1) + c * SORT_CHUNK).astype(F32)
            unsort = jnp.where(r_iota == d1, w1, jnp.where(r_iota == d2, w2, 0.0)).astype(BF16)
            acc = acc + _dot(unsort, yl_ref[c * SORT_CHUNK:(c + 1) * SORT_CHUNK, :])
        y_ref[rows, :] = _rmsnorm(acc, gf_ref[...])


def _ffn_schedule(tab):
    n_tok_tiles = tab.shape[0]
    strips = (tab[:, :, 0] * (1.0 / ROW_GRANULE)).astype(jnp.int32)
    starts = (tab[:, :, 1] * (1.0 / ROW_GRANULE)).astype(jnp.int32)
    cnt = strips.T
    row0 = starts.T
    cs = jnp.cumsum(cnt, axis=1) - cnt
    n_str = jnp.sum(cnt, axis=1)
    np_str = (n_str + FFN_SLOTS - 1) // FFN_SLOTS * FFN_SLOTS
    ends = jnp.cumsum(np_str)
    base = ends - np_str
    n_steps_max = _ffn_steps_max(n_tok_tiles) + GATHER_BUFS - 1
    step0 = jnp.arange(n_steps_max, dtype=jnp.int32) * FFN_SLOTS
    stream = jnp.minimum(jnp.sum(ends[None, :] <= step0[:, None], axis=1), N_EXPERTS - 1)
    pick = stream[:, None] == jnp.arange(N_EXPERTS, dtype=jnp.int32)[None, :]
    sel = lambda x: jnp.sum(jnp.where(pick[:, :, None], x[None], 0), axis=1)
    cs_j, cnt_j, row0_j = sel(cs), sel(cnt), sel(row0)
    base_j = jnp.sum(jnp.where(pick, base[None, :], 0), axis=1)
    q = step0[:, None] + jnp.arange(FFN_SLOTS, dtype=jnp.int32)[None, :] - base_j[:, None]
    reached = cs_j[:, None, :] <= q[:, :, None]
    last = lambda x: jnp.sum(jnp.where(reached, jnp.diff(x, axis=1, prepend=0)[:, None, :], 0), axis=2)
    tile_idx = jnp.sum(reached, axis=2).astype(jnp.int32) - 1
    g = q - last(cs_j)
    valid = g < last(cnt_j)
    granule = tile_idx * GRANULES_PER_TILE + last(row0_j) + g
    assert n_tok_tiles >= 2 * FFN_SLOTS
    slot = jnp.arange(FFN_SLOTS, dtype=jnp.int32)[None, :]
    parity = jnp.arange(n_steps_max, dtype=jnp.int32)[:, None] % 2
    pad_dst = lambda par: (slot + par * FFN_SLOTS) * GRANULES_PER_TILE + GRANULES_PER_TILE - 1
    gsrc = jnp.where(valid, granule, slot * GRANULES_PER_TILE + GRANULES_PER_TILE - 2).astype(jnp.int32)
    sdst = jnp.where(valid, granule, pad_dst(parity))
    sdst = jnp.concatenate([pad_dst(1), sdst], axis=0)
    texp = stream.astype(jnp.int32)
    n_steps = (ends[-1] // FFN_SLOTS).astype(jnp.int32).reshape(1)
    return texp, gsrc.reshape(-1), sdst.astype(jnp.int32).reshape(-1), n_steps


assert GRANULES_PER_TILE - (2 * TOK_TILE + N_EXPERTS * (ROW_GRANULE - 1)) // ROW_GRANULE >= 2


def _ffn_steps_max(n_tok_tiles):
    return -(-(n_tok_tiles * GRANULES_PER_TILE + N_EXPERTS * (FFN_SLOTS - 1)) // FFN_SLOTS)


def _vmem_limit(mib):
    n_bytes = mib * 1024 * 1024
    assert n_bytes < V7X_VMEM_BYTES
    return n_bytes


def _rope_tables(pos):
    inv_freq = ROPE_THETA ** (-np.arange(HALF, dtype=np.float64) * 2.0 / HEAD_DIM)
    ang = np.asarray(pos, np.float64)[:, None] * inv_freq[None, :]
    cos, sin = np.cos(ang).astype(np.float32), np.sin(ang).astype(np.float32)
    reps = LANES // HEAD_DIM
    return (jnp.asarray(np.tile(np.concatenate([cos, cos], axis=1), (1, reps))),
            jnp.asarray(np.tile(np.concatenate([-sin, sin], axis=1), (1, reps))))


def _moe_combine(xmid, yl, meta, gf, n_tok, first_tile, n_tiles):
    return pl.pallas_call(
        functools.partial(_combine_kernel, first_tile=first_tile, n_tiles=n_tiles),
        grid=(n_tiles,),
        in_specs=[
            pl.BlockSpec((n_tok, D_MODEL), lambda i: (i, 0)),
            pl.BlockSpec(memory_space=pl.ANY),
            pl.BlockSpec((None, 8, n_tok), lambda i: (i, 0, 0)),
            pl.BlockSpec((1, D_MODEL), lambda i: (0, 0)),
        ],
        out_specs=pl.BlockSpec((n_tok, D_MODEL), lambda i: (i, 0)),
        out_shape=jax.ShapeDtypeStruct((n_tiles * n_tok, D_MODEL), F32),
        scratch_shapes=[pltpu.VMEM((COMBINE_BUFS, SORT_ROWS, D_MODEL), BF16), pltpu.SemaphoreType.DMA((COMBINE_BUFS,))],
        compiler_params=pltpu.CompilerParams(
            dimension_semantics=("arbitrary",),
            vmem_limit_bytes=_vmem_limit(40)),
        name="moe_combine",
    )(xmid, yl, meta, gf)


def kernel(x_prompt, x_sample, cache_swa_k, cache_swa_v, norm_mix_g, w_in, attn_sinks, gmlp_ln_g, gmlp_ln_b,
           gmlp_w_s, gmlp_b_s, w_out, norm_ffn_g, router_group_w, router_group_b, router_expert_w,
           router_expert_b, expert_w_gate, expert_w_up, expert_w_down, final_norm_g):
    assert norm_mix_g.shape[0] == 1, "single-layer trunk"
    batch, seq, _ = x_prompt.shape
    dec_batch = x_sample.shape[0]
    assert x_sample.shape[1] == 1 and seq % TOK_TILE == 0

    win_ext = w_in[0].astype(BF16)
    wout = w_out[0].astype(BF16)
    g1 = norm_mix_g[0][None, :]
    g2 = norm_ffn_g[0][None, :]
    gf = final_norm_g[None, :]
    lng = gmlp_ln_g[0][None, :]
    lnb = gmlp_ln_b[0][None, :]
    sinks = attn_sinks[0]
    ws = gmlp_w_s[0]
    group_dim = GMLP_WIDTH // GMLP_GROUPS
    bsf = jnp.repeat(gmlp_b_s[0].T, group_dim, axis=1)
    ws0 = jnp.repeat(ws[:, 0, 0], group_dim)[None, :]
    bs0 = jnp.repeat(gmlp_b_s[0][:, 0], group_dim)[None, :]
    pad_g, pad_e = 8 - N_GROUPS, ROUTER_ROWS - 8 - N_EXPERTS
    wrt = jnp.concatenate([router_group_w[0], jnp.zeros((D_MODEL, pad_g), F32),
                           router_expert_w[0].reshape(D_MODEL, N_EXPERTS), jnp.zeros((D_MODEL, pad_e), F32)],
                          axis=1).T.astype(BF16)
    brt = jnp.concatenate([router_group_b[0], jnp.full((pad_g,), NEG_INF, F32),
                           router_expert_b[0].reshape(N_EXPERTS), jnp.full((pad_e,), NEG_INF, F32)])[:, None]
    wg_f = expert_w_gate[0].reshape(N_EXPERTS * D_MODEL, D_EXPERT)
    wu_f = expert_w_up[0].reshape(N_EXPERTS * D_MODEL, D_EXPERT)
    wd_f = expert_w_down[0].reshape(N_EXPERTS * D_EXPERT, D_MODEL)
    cos_p, sin_p = _rope_tables(np.arange(seq))
    cos_s, sin_s = _rope_tables(PAST_LEN + np.arange(1))

    full = lambda shape: pl.BlockSpec(shape, lambda *_: (0,) * len(shape))
    smem = pl.BlockSpec(memory_space=pltpu.SMEM)
    n_tiles = seq // TOK_TILE

    upper_np = np.triu(np.ones((TOK_TILE, TOK_TILE), np.float32), k=1)
    upper = jnp.asarray(upper_np, BF16)
    lpad = jnp.asarray((np.arange(LANES)[None, :] < np.arange(N_EXPERTS)[:, None]).astype(np.float32), BF16)
    n_tok_tiles = batch * n_tiles

    x2d = x_prompt.reshape(batch * seq, D_MODEL)
    tile_a = lambda s: jnp.minimum(s, n_tok_tiles - 1)
    cur = lambda s: jnp.maximum(s - 1, 0)
    tile_b = cur
    gu_rows, dn_rows = wg_f.shape[0] // n_tok_tiles, wd_f.shape[0] // n_tok_tiles
    assert gu_rows * n_tok_tiles == wg_f.shape[0] and dn_rows * n_tok_tiles == wd_f.shape[0] and dn_rows % 16 == 0
    xmid_p, xl_p, meta_p, tab_p, k_p, v_p, wg, wu, wd = pl.pallas_call(
        functools.partial(_prompt_mixer_kernel, tiles_per_seq=n_tiles),
        grid=(n_tok_tiles + 1,),
        in_specs=[
            smem,
            pl.BlockSpec((TOK_TILE, D_MODEL), lambda s: (tile_a(s), 0)),
            pl.BlockSpec((TOK_TILE, D_MODEL), lambda s: (cur(s), 0)),
            full((1, D_MODEL)),
            full((D_MODEL, IN_WIDTH)),
            pl.BlockSpec((TOK_TILE, LANES), lambda s: (tile_b(s) % n_tiles, 0)),
            pl.BlockSpec((TOK_TILE, LANES), lambda s: (tile_b(s) % n_tiles, 0)),
            full((1, GMLP_WIDTH)),
            full((1, GMLP_WIDTH)),
            full((GMLP_GROUPS, CHUNK, CHUNK)),
            full((CHUNK, GMLP_WIDTH)),
            full((D_MODEL, D_MODEL)),
            full((1, D_MODEL)),
            full((ROUTER_ROWS, D_MODEL)),
            full((ROUTER_ROWS, 1)),
            full((TOK_TILE, TOK_TILE)),
            full((N_EXPERTS, LANES)),
            pl.BlockSpec((gu_rows, D_EXPERT), lambda s: (tile_a(s), 0)),
            pl.BlockSpec((gu_rows, D_EXPERT), lambda s: (tile_a(s), 0)),
            pl.BlockSpec((dn_rows, D_MODEL), lambda s: (tile_a(s), 0)),
        ],
        out_specs=[
            pl.BlockSpec((TOK_TILE, D_MODEL), lambda s: (cur(s), 0)),
            pl.BlockSpec((None, SORT_ROWS, D_MODEL), lambda s: (jnp.where(s == 0, n_tok_tiles, s - 1), 0, 0)),
            pl.BlockSpec((None, 8, TOK_TILE), lambda s: (cur(s), 0, 0)),
            pl.BlockSpec((None, N_EXPERTS, LANES), lambda s: (cur(s), 0, 0)),
            pl.BlockSpec((None, WINDOW, KV_WIDTH), lambda s: (tile_b(s) // n_tiles, 0, 0)),
            pl.BlockSpec((None, WINDOW, KV_WIDTH), lambda s: (tile_b(s) // n_tiles, 0, 0)),
            pl.BlockSpec((gu_rows, D_EXPERT), lambda s: (tile_a(s), 0)),
            pl.BlockSpec((gu_rows, D_EXPERT), lambda s: (tile_a(s), 0)),
            pl.BlockSpec((dn_rows, D_MODEL), lambda s: (tile_a(s), 0)),
        ],
        out_shape=[
            jax.ShapeDtypeStruct((batch * seq, D_MODEL), F32),
            jax.ShapeDtypeStruct((n_tok_tiles + 1, SORT_ROWS, D_MODEL), BF16),
            jax.ShapeDtypeStruct((n_tok_tiles, 8, TOK_TILE), F32),
            jax.ShapeDtypeStruct((n_tok_tiles, N_EXPERTS, LANES), F32),
            jax.ShapeDtypeStruct((batch, WINDOW, KV_WIDTH), F32),
            jax.ShapeDtypeStruct((batch, WINDOW, KV_WIDTH), F32),
            jax.ShapeDtypeStruct(wg_f.shape, BF16),
            jax.ShapeDtypeStruct(wu_f.shape, BF16),
            jax.ShapeDtypeStruct(wd_f.shape, BF16),
        ],
        scratch_shapes=[
            pltpu.VMEM((TOK_TILE, IN_WIDTH), F32),
            pltpu.VMEM((TOK_TILE, IN_WIDTH), F32),
            pltpu.VMEM((TOK_TILE, D_MODEL), BF16),
        ] + [pltpu.VMEM((WINDOW + TOK_TILE, KV_WIDTH), BF16)] * 4 + [
            pltpu.VMEM((len(_NAT_HEADS), TOK_TILE, LANES), BF16),
            pltpu.VMEM((len(_SWP_HEADS), TOK_TILE, LANES), BF16),
            pltpu.VMEM((2 * BLOCKS_PER_TILE, len(_NAT_HEADS) * WINDOW, 2 * WINDOW), F32),
            pltpu.VMEM((2 * BLOCKS_PER_TILE, len(_NAT_HEADS) * WINDOW, 2 * WINDOW), BF16),
            pltpu.VMEM((TOK_TILE, GMLP_WIDTH), BF16),
            pltpu.VMEM((TOK_TILE, D_MODEL), BF16),
            pltpu.VMEM((TOK_TILE, D_MODEL), BF16),
        ],
        compiler_params=pltpu.CompilerParams(
            dimension_semantics=("arbitrary",),
            vmem_limit_bytes=_vmem_limit(56)),
        name="prompt_mixer",
    )(sinks, x2d, x2d, g1, win_ext, cos_p, sin_p, lng, lnb, ws, bsf, wout, g2, wrt, brt, upper, lpad,
      wg_f, wu_f, wd_f)
    wg = wg.reshape(N_EXPERTS, D_MODEL, D_EXPERT)
    wu = wu.reshape(N_EXPERTS, D_MODEL, D_EXPERT)
    wd = wd.reshape(N_EXPERTS, D_EXPERT, D_MODEL)

    xs = x_sample.reshape(dec_batch, D_MODEL)
    ck = cache_swa_k[0].reshape(dec_batch, WINDOW, KV_WIDTH)
    cv = cache_swa_v[0].reshape(dec_batch, WINDOW, KV_WIDTH)
    vmem = pl.BlockSpec(memory_space=pltpu.VMEM)
    hbm = pl.BlockSpec(memory_space=pl.ANY)
    xmid_s, meta_s, tab_s, k_s, v_s, vn_s, xl_all = pl.pallas_call(
        _sample_mixer_kernel,
        in_specs=[smem] + [vmem] * 17 + [hbm],
        out_specs=[vmem] * 6 + [hbm],
        out_shape=[
            jax.ShapeDtypeStruct((dec_batch, D_MODEL), F32),
            jax.ShapeDtypeStruct((8, dec_batch), F32),
            jax.ShapeDtypeStruct((N_EXPERTS, LANES), F32),
            jax.ShapeDtypeStruct((dec_batch, KV_WIDTH), F32),
            jax.ShapeDtypeStruct((dec_batch, KV_WIDTH), F32),
            jax.ShapeDtypeStruct((dec_batch, GMLP_WIDTH), F32),
            jax.ShapeDtypeStruct(xl_p.shape, BF16),
        ],
        scratch_shapes=[pltpu.VMEM((dec_batch, D_MODEL), BF16), pltpu.VMEM((SORT_ROWS, D_MODEL), BF16),
                        pltpu.SemaphoreType.DMA((1,))],
        compiler_params=pltpu.CompilerParams(vmem_limit_bytes=_vmem_limit(56)),
        input_output_aliases={18: 6},
        name="sample_mixer",
    )(sinks, xs, ck, cv, g1, win_ext, cos_s, sin_s, lng, lnb, ws0, bs0, wout, g2, wrt, brt,
      jnp.asarray(upper_np[:dec_batch, :dec_batch], BF16), lpad, xl_p)

    n_all_tiles = n_tok_tiles + 1
    tab_all = jnp.concatenate([tab_p, tab_s[None]], axis=0)
    texp, gsrc, sdst, n_steps = _ffn_schedule(tab_all)
    yl_all = pl.pallas_call(
        _grouped_ffn_kernel,
        grid_spec=pltpu.PrefetchScalarGridSpec(
            num_scalar_prefetch=4,
            grid=(_ffn_steps_max(n_all_tiles),),
            in_specs=[
                pl.BlockSpec(memory_space=pl.ANY),
                pl.BlockSpec((None, D_MODEL, D_EXPERT), lambda j, te, gs, sd, ns: (te[j], 0, 0)),
                pl.BlockSpec((None, D_MODEL, D_EXPERT), lambda j, te, gs, sd, ns: (te[j], 0, 0)),
                pl.BlockSpec((None, D_EXPERT, D_MODEL), lambda j, te, gs, sd, ns: (te[j], 0, 0)),
            ],
            out_specs=pl.BlockSpec(memory_space=pl.ANY),
            scratch_shapes=[
                pltpu.VMEM((GATHER_BUFS, FFN_SLOTS, ROW_GRANULE, D_MODEL), BF16),
                pltpu.VMEM((SCATTER_BUFS, FFN_SLOTS, ROW_GRANULE, D_MODEL), BF16),
                pltpu.SemaphoreType.DMA((GATHER_BUFS,)),
                pltpu.SemaphoreType.DMA((SCATTER_BUFS,)),
            ],
        ),
        out_shape=jax.ShapeDtypeStruct((n_all_tiles * GRANULES_PER_TILE, ROW_GRANULE, D_MODEL), BF16),
        compiler_params=pltpu.CompilerParams(
            dimension_semantics=("arbitrary",),
            vmem_limit_bytes=_vmem_limit(32)),
        input_output_aliases={4: 0},
        name="grouped_ffn",
    )(texp, gsrc, sdst, n_steps, xl_all.reshape(n_all_tiles * GRANULES_PER_TILE, ROW_GRANULE, D_MODEL), wg, wu, wd)
    yl_all = yl_all.reshape(n_all_tiles * SORT_ROWS, D_MODEL)

    y_p = _moe_combine(xmid_p, yl_all, meta_p, gf, TOK_TILE, 0, n_tok_tiles)
    y_s = _moe_combine(xmid_s, yl_all, meta_s[None], gf, dec_batch, n_tok_tiles, 1)

    return (y_p.reshape(batch, seq, D_MODEL),
            y_s.reshape(dec_batch, 1, D_MODEL),
            k_p.reshape(1, batch, WINDOW, N_KV_HEADS, HEAD_DIM),
            v_p.reshape(1, batch, WINDOW, N_KV_HEADS, HEAD_DIM),
            k_s.reshape(1, dec_batch, 1, N_KV_HEADS, HEAD_DIM),
            v_s.reshape(1, dec_batch, 1, N_KV_HEADS, HEAD_DIM),
            vn_s.reshape(1, dec_batch, 1, GMLP_WIDTH))
```
